```python
import jax, jax.numpy as jnp
from jax import lax
import numpy as np

D_MODEL = 2048
BATCH = 8
SEQ = 8192
DEPTH = 1

D_MIX = D_MODEL
D_LRU = D_MIX // 2
D_RET = D_MIX - D_LRU
LRU_BLOCKS = 8
LRU_BLOCK_W = D_LRU // LRU_BLOCKS
CONV_W = 4
LRU_C = 8.0
RET_HEADS = 8
RET_HD = D_RET // RET_HEADS
CHUNK = 128
ROPE_BASE = 10000.0
D_FF = ((8 * D_MODEL + 3 * 256 - 1) // (3 * 256)) * 256
EPS = 1e-6
SPLITS = (D_LRU, 2 * D_LRU, 2 * D_LRU + D_RET, 2 * D_LRU + 2 * D_RET, 2 * D_LRU + 3 * D_RET)
D_IN = 2 * D_LRU + 4 * D_RET

kernel_name = "hymba_rglru_retention_hybrid"


def rmsnorm(x, w):
    xf = x.astype(jnp.float32)
    y = xf * lax.rsqrt(jnp.mean(xf * xf, axis=-1, keepdims=True) + EPS)
    return (y * w.astype(jnp.float32)).astype(x.dtype)


def causal_depthwise_conv(x, w, b):
    S = x.shape[1]
    xp = jnp.pad(x, ((0, 0), (CONV_W - 1, 0), (0, 0)))
    y = b
    for tap in range(CONV_W):
        y = y + xp[:, tap:tap + S, :] * w[tap]
    return y


def rg_lru(u, wa, ba, wx, bx, lam):
    B, S, _ = u.shape
    uf = u.astype(jnp.float32)
    ub = uf.reshape(B, S, LRU_BLOCKS, LRU_BLOCK_W)
    r = jax.nn.sigmoid(jnp.einsum('bsnc,ncd->bsnd', ub, wa.astype(jnp.float32)).reshape(B, S, D_LRU) + ba)
    i = jax.nn.sigmoid(jnp.einsum('bsnc,ncd->bsnd', ub, wx.astype(jnp.float32)).reshape(B, S, D_LRU) + bx)
    log_a = LRU_C * r * jax.nn.log_sigmoid(lam.astype(jnp.float32))
    a = jnp.exp(log_a)
    b = jnp.sqrt(-jnp.expm1(2.0 * log_a)) * (i * uf)

    def combine(left, right):
        a1, b1 = left
        a2, b2 = right
        return a1 * a2, a2 * b1 + b2

    _, h = lax.associative_scan(combine, (a, b), axis=1)
    return h.astype(u.dtype)


def rope(t, cos, sin):
    half = t.shape[-1] // 2
    t1, t2 = t[..., :half], t[..., half:]
    c = cos[None, :, None, :]
    s = sin[None, :, None, :]
    return jnp.concatenate([t1 * c - t2 * s, t1 * s + t2 * c], axis=-1)


def retention(q, k, v, g, gn_w):
    B, S, _ = q.shape
    H, Dh, C = RET_HEADS, RET_HD, CHUNK
    N = S // C
    f32 = jnp.float32
    pos = jnp.arange(S, dtype=f32)
    inv_freq = ROPE_BASE ** (-jnp.arange(0, Dh, 2, dtype=f32) / Dh)
    ang = pos[:, None] * inv_freq[None, :]
    cos, sin = jnp.cos(ang), jnp.sin(ang)
    qh = rope(q.astype(f32).reshape(B, S, H, Dh), cos, sin)
    kh = rope(k.astype(f32).reshape(B, S, H, Dh), cos, sin) * (Dh ** -0.5)
    vh = v.astype(f32).reshape(B, S, H, Dh)
    qc = qh.reshape(B, N, C, H, Dh)
    kc = kh.reshape(B, N, C, H, Dh)
    vc = vh.reshape(B, N, C, H, Dh)

    log_gamma = jnp.log1p(-jnp.exp2(-5.0 - jnp.arange(H, dtype=f32)))
    idx = jnp.arange(C)
    diff = idx[:, None] - idx[None, :]
    causal = diff >= 0
    decay = jnp.where(causal[None], jnp.exp(log_gamma[:, None, None] * jnp.where(causal, diff, 0)[None].astype(f32)), 0.0)

    scores = jnp.einsum('bnqhd,bnkhd->bnhqk', qc, kc) * decay[None, None]
    inner = jnp.einsum('bnhqk,bnkhe->bnqhe', scores, vc)

    zeta = jnp.exp(log_gamma[None, :] * (C - 1 - idx).astype(f32)[:, None])
    kv = jnp.einsum('bnkhd,bnkhe->bnhde', kc * zeta[:, :, None], vc)
    chunk_decay = jnp.exp(log_gamma * C)[:, None, None]

    def step(R, kv_n):
        return R * chunk_decay + kv_n, R

    _, R_prev = lax.scan(step, jnp.zeros((B, H, Dh, Dh), f32), jnp.moveaxis(kv, 1, 0))
    R_prev = jnp.moveaxis(R_prev, 0, 1)

    xi = jnp.exp(log_gamma[None, :] * (idx + 1).astype(f32)[:, None])
    cross = jnp.einsum('bnqhd,bnhde->bnqhe', qc * xi[:, :, None], R_prev)
    o = (inner + cross).reshape(B, S, H, Dh)

    mu = jnp.mean(o, axis=-1, keepdims=True)
    var = jnp.mean(jnp.square(o - mu), axis=-1, keepdims=True)
    on = ((o - mu) * lax.rsqrt(var + EPS)).reshape(B, S, D_RET) * gn_w.astype(f32)
    return (jax.nn.silu(g.astype(f32)) * on).astype(q.dtype)


def _fwd_setup_inputs(seed: int = 0) -> dict:
    key = jax.random.key(seed)
    ks = jax.random.split(key, 20)
    f32 = jnp.float32

    def nrm(k, shape, scale):
        return jax.random.normal(k, shape, f32) * scale

    u = jax.random.uniform(ks[8], (DEPTH, D_LRU), f32, 0.9, 0.999)
    s = u ** (1.0 / LRU_C)
    lru_lambda = jnp.log(s) - jnp.log1p(-s)
    return {
        "x": nrm(ks[0], (BATCH, SEQ, D_MODEL), 1.0),
        "ln1_w": 1.0 + nrm(ks[1], (DEPTH, D_MODEL), 0.02),
        "w_in": nrm(ks[2], (DEPTH, D_MODEL, D_IN), D_MODEL ** -0.5),
        "conv_w": nrm(ks[3], (DEPTH, CONV_W, D_LRU), CONV_W ** -0.5),
        "conv_b": nrm(ks[4], (DEPTH, D_LRU), 0.01),
        "gate_a_w": nrm(ks[5], (DEPTH, LRU_BLOCKS, LRU_BLOCK_W, LRU_BLOCK_W), LRU_BLOCK_W ** -0.5),
        "gate_a_b": nrm(ks[6], (DEPTH, D_LRU), 0.01),
        "gate_x_w": nrm(ks[7], (DEPTH, LRU_BLOCKS, LRU_BLOCK_W, LRU_BLOCK_W), LRU_BLOCK_W ** -0.5),
        "gate_x_b": nrm(ks[9], (DEPTH, D_LRU), 0.01),
        "lru_lambda": lru_lambda,
        "ret_gn_w": 1.0 + nrm(ks[10], (DEPTH, D_RET), 0.02),
        "w_out": nrm(ks[11], (DEPTH, D_MIX, D_MODEL), D_MIX ** -0.5),
        "ln2_w": 1.0 + nrm(ks[12], (DEPTH, D_MODEL), 0.02),
        "w_ffn_gate": nrm(ks[13], (DEPTH, D_MODEL, D_FF), D_MODEL ** -0.5),
        "w_ffn_up": nrm(ks[14], (DEPTH, D_MODEL, D_FF), D_MODEL ** -0.5),
        "w_ffn_down": nrm(ks[15], (DEPTH, D_FF, D_MODEL), D_FF ** -0.5),
        "final_norm_w": 1.0 + nrm(ks[16], (D_MODEL,), 0.02),
    }


def _fwd_reference(x, ln1_w, w_in, conv_w, conv_b, gate_a_w, gate_a_b, gate_x_w, gate_x_b,
              lru_lambda, ret_gn_w, w_out, ln2_w, w_ffn_gate, w_ffn_up, w_ffn_down, final_norm_w):
    h = x
    for l in range(DEPTH):
        u = rmsnorm(h, ln1_w[l])
        proj = jnp.einsum('bsd,de->bse', u, w_in[l])
        lru_x, lru_g, q, k, v, ret_g = jnp.split(proj, SPLITS, axis=-1)
        lru_x = causal_depthwise_conv(lru_x, conv_w[l], conv_b[l])
        y_lru = rg_lru(lru_x, gate_a_w[l], gate_a_b[l], gate_x_w[l], gate_x_b[l], lru_lambda[l])
        y_lru = y_lru * jax.nn.gelu(lru_g)
        y_ret = retention(q, k, v, ret_g, ret_gn_w[l])
        y = jnp.concatenate([y_lru, y_ret], axis=-1)
        h = h + jnp.einsum('bse,ed->bsd', y, w_out[l])
        u = rmsnorm(h, ln2_w[l])
        ff = jax.nn.silu(jnp.einsum('bsd,df->bsf', u, w_ffn_gate[l])) * jnp.einsum('bsd,df->bsf', u, w_ffn_up[l])
        h = h + jnp.einsum('bsf,fd->bsd', ff, w_ffn_down[l])
    return rmsnorm(h, final_norm_w)


import jax as _jax
import jax.numpy as _jnp

TWIN_FORMAT = 'train_step'
FWD_PARAMS = ['x', 'ln1_w', 'w_in', 'conv_w', 'conv_b', 'gate_a_w', 'gate_a_b', 'gate_x_w', 'gate_x_b', 'lru_lambda', 'ret_gn_w', 'w_out', 'ln2_w', 'w_ffn_gate', 'w_ffn_up', 'w_ffn_down', 'final_norm_w']
TWIN_WEIGHTS = ['ln1_w', 'w_in', 'conv_w', 'conv_b', 'gate_a_w', 'gate_a_b', 'gate_x_w', 'gate_x_b', 'lru_lambda', 'ret_gn_w', 'w_out', 'ln2_w', 'w_ffn_gate', 'w_ffn_up', 'w_ffn_down', 'final_norm_w']
TWIN_DIFF_INPUT = 'x'
TWIN_INPUTS = ['x', 'ln1_w', 'w_in', 'conv_w', 'conv_b', 'gate_a_w', 'gate_a_b', 'gate_x_w', 'gate_x_b', 'lru_lambda', 'ret_gn_w', 'w_out', 'ln2_w', 'w_ffn_gate', 'w_ffn_up', 'w_ffn_down', 'final_norm_w', 'loss_target', 'm_ln1_w', 'm_w_in', 'm_conv_w', 'm_conv_b', 'm_gate_a_w', 'm_gate_a_b', 'm_gate_x_w', 'm_gate_x_b', 'm_lru_lambda', 'm_ret_gn_w', 'm_w_out', 'm_ln2_w', 'm_w_ffn_gate', 'm_w_ffn_up', 'm_w_ffn_down', 'm_final_norm_w', 'v_ln1_w', 'v_w_in', 'v_conv_w', 'v_conv_b', 'v_gate_a_w', 'v_gate_a_b', 'v_gate_x_w', 'v_gate_x_b', 'v_lru_lambda', 'v_ret_gn_w', 'v_w_out', 'v_ln2_w', 'v_w_ffn_gate', 'v_w_ffn_up', 'v_w_ffn_down', 'v_final_norm_w']
TWIN_OUTPUTS = ['loss', 'grad_x', 'grad_ln1_w', 'grad_w_in', 'grad_conv_w', 'grad_conv_b', 'grad_gate_a_w', 'grad_gate_a_b', 'grad_gate_x_w', 'grad_gate_x_b', 'grad_lru_lambda', 'grad_ret_gn_w', 'grad_w_out', 'grad_ln2_w', 'grad_w_ffn_gate', 'grad_w_ffn_up', 'grad_w_ffn_down', 'grad_final_norm_w', 'delta_ln1_w', 'delta_w_in', 'delta_conv_w', 'delta_conv_b', 'delta_gate_a_w', 'delta_gate_a_b', 'delta_gate_x_w', 'delta_gate_x_b', 'delta_lru_lambda', 'delta_ret_gn_w', 'delta_w_out', 'delta_ln2_w', 'delta_w_ffn_gate', 'delta_w_ffn_up', 'delta_w_ffn_down', 'delta_final_norm_w', 'new_m_ln1_w', 'new_m_w_in', 'new_m_conv_w', 'new_m_conv_b', 'new_m_gate_a_w', 'new_m_gate_a_b', 'new_m_gate_x_w', 'new_m_gate_x_b', 'new_m_lru_lambda', 'new_m_ret_gn_w', 'new_m_w_out', 'new_m_ln2_w', 'new_m_w_ffn_gate', 'new_m_w_ffn_up', 'new_m_w_ffn_down', 'new_m_final_norm_w', 'new_v_ln1_w', 'new_v_w_in', 'new_v_conv_w', 'new_v_conv_b', 'new_v_gate_a_w', 'new_v_gate_a_b', 'new_v_gate_x_w', 'new_v_gate_x_b', 'new_v_lru_lambda', 'new_v_ret_gn_w', 'new_v_w_out', 'new_v_ln2_w', 'new_v_w_ffn_gate', 'new_v_w_ffn_up', 'new_v_w_ffn_down', 'new_v_final_norm_w']
TWIN_LEAF_KINDS = {'loss': 'loss', 'grad_x': 'grad_x', 'grad_ln1_w': 'grad_w', 'grad_w_in': 'grad_w', 'grad_conv_w': 'grad_w', 'grad_conv_b': 'grad_w', 'grad_gate_a_w': 'grad_w', 'grad_gate_a_b': 'grad_w', 'grad_gate_x_w': 'grad_w', 'grad_gate_x_b': 'grad_w', 'grad_lru_lambda': 'grad_w', 'grad_ret_gn_w': 'grad_w', 'grad_w_out': 'grad_w', 'grad_ln2_w': 'grad_w', 'grad_w_ffn_gate': 'grad_w', 'grad_w_ffn_up': 'grad_w', 'grad_w_ffn_down': 'grad_w', 'grad_final_norm_w': 'grad_w', 'delta_ln1_w': 'delta_w', 'delta_w_in': 'delta_w', 'delta_conv_w': 'delta_w', 'delta_conv_b': 'delta_w', 'delta_gate_a_w': 'delta_w', 'delta_gate_a_b': 'delta_w', 'delta_gate_x_w': 'delta_w', 'delta_gate_x_b': 'delta_w', 'delta_lru_lambda': 'delta_w', 'delta_ret_gn_w': 'delta_w', 'delta_w_out': 'delta_w', 'delta_ln2_w': 'delta_w', 'delta_w_ffn_gate': 'delta_w', 'delta_w_ffn_up': 'delta_w', 'delta_w_ffn_down': 'delta_w', 'delta_final_norm_w': 'delta_w', 'new_m_ln1_w': 'new_m', 'new_m_w_in': 'new_m', 'new_m_conv_w': 'new_m', 'new_m_conv_b': 'new_m', 'new_m_gate_a_w': 'new_m', 'new_m_gate_a_b': 'new_m', 'new_m_gate_x_w': 'new_m', 'new_m_gate_x_b': 'new_m', 'new_m_lru_lambda': 'new_m', 'new_m_ret_gn_w': 'new_m', 'new_m_w_out': 'new_m', 'new_m_ln2_w': 'new_m', 'new_m_w_ffn_gate': 'new_m', 'new_m_w_ffn_up': 'new_m', 'new_m_w_ffn_down': 'new_m', 'new_m_final_norm_w': 'new_m', 'new_v_ln1_w': 'new_v', 'new_v_w_in': 'new_v', 'new_v_conv_w': 'new_v', 'new_v_conv_b': 'new_v', 'new_v_gate_a_w': 'new_v', 'new_v_gate_a_b': 'new_v', 'new_v_gate_x_w': 'new_v', 'new_v_gate_x_b': 'new_v', 'new_v_lru_lambda': 'new_v', 'new_v_ret_gn_w': 'new_v', 'new_v_w_out': 'new_v', 'new_v_ln2_w': 'new_v', 'new_v_w_ffn_gate': 'new_v', 'new_v_w_ffn_up': 'new_v', 'new_v_w_ffn_down': 'new_v', 'new_v_final_norm_w': 'new_v'}


def _forward(args):
    return _fwd_reference(*[args[k] for k in FWD_PARAMS])


def _output_shape():
    def fwd():
        inp = _fwd_setup_inputs(0)
        return _fwd_reference(*[inp[k] for k in FWD_PARAMS])
    out = _jax.eval_shape(fwd)
    return out.shape, out.dtype

N_MICROBATCH = 1
ADAM_LR = 0.001
ADAM_B1 = 0.9
ADAM_B2 = 0.999
ADAM_EPS = 1e-08
ADAM_WD = 0.01
ADAM_STEP = 10
PER_EXAMPLE_BATCH_AXIS = {'x': 0, 'loss_target': 0}
SHARED_INPUTS = []
_WEIGHT_DTYPES = {'ln1_w': _jnp.float32, 'w_in': _jnp.float32, 'conv_w': _jnp.float32, 'conv_b': _jnp.float32, 'gate_a_w': _jnp.float32, 'gate_a_b': _jnp.float32, 'gate_x_w': _jnp.float32, 'gate_x_b': _jnp.float32, 'lru_lambda': _jnp.float32, 'ret_gn_w': _jnp.float32, 'w_out': _jnp.float32, 'ln2_w': _jnp.float32, 'w_ffn_gate': _jnp.float32, 'w_ffn_up': _jnp.float32, 'w_ffn_down': _jnp.float32, 'final_norm_w': _jnp.float32}
MOMENT_SCALE = {'ln1_w': 1.230863e-01, 'w_in': 6.771251e-02, 'conv_w': 5.589896e-02, 'conv_b': 7.165233e-01, 'gate_a_w': 1.841619e-02, 'gate_a_b': 1.684117e-02, 'gate_x_w': 3.301364e-02, 'gate_x_b': 2.065005e-02, 'lru_lambda': 3.451770e-02, 'ret_gn_w': 7.520430e-02, 'w_out': 6.407726e-02, 'ln2_w': 8.564041e-02, 'w_ffn_gate': 3.685696e-02, 'w_ffn_up': 3.571551e-02, 'w_ffn_down': 5.922155e-02, 'final_norm_w': 3.196088e+01}


def _to_microbatches(a, axis):
    t = _jnp.moveaxis(a, axis, 0)
    t = t.reshape((N_MICROBATCH, t.shape[0] // N_MICROBATCH) + t.shape[1:])
    return _jnp.moveaxis(t, 1, axis + 1)


def setup_inputs(seed: int = 0) -> dict:
    inp = _fwd_setup_inputs(seed)
    key = _jax.random.fold_in(_jax.random.key(seed), 7919)
    shape, _ = _output_shape()
    out = dict(inp)
    out["loss_target"] = _jax.random.normal(_jax.random.fold_in(key, 0), shape, _jnp.float32)
    for i, name in enumerate(TWIN_WEIGHTS):
        w = inp[name].astype(_jnp.float32)
        if MOMENT_SCALE is None:
            s = _jnp.sqrt(_jnp.mean(_jnp.square(w)) + 1e-30)
        else:
            s = MOMENT_SCALE[name]
        km, kv = _jax.random.split(_jax.random.fold_in(key, i + 1))
        out[name] = w
        out["m_" + name] = s * _jax.random.normal(km, w.shape, _jnp.float32)
        out["v_" + name] = (s * s) * _jax.random.uniform(kv, w.shape, _jnp.float32, 0.5, 1.5)
    if N_MICROBATCH > 1:
        for name, axis in PER_EXAMPLE_BATCH_AXIS.items():
            out[name] = _to_microbatches(out[name], axis)
    return {'x': out['x'], 'ln1_w': out['ln1_w'], 'w_in': out['w_in'], 'conv_w': out['conv_w'], 'conv_b': out['conv_b'], 'gate_a_w': out['gate_a_w'], 'gate_a_b': out['gate_a_b'], 'gate_x_w': out['gate_x_w'], 'gate_x_b': out['gate_x_b'], 'lru_lambda': out['lru_lambda'], 'ret_gn_w': out['ret_gn_w'], 'w_out': out['w_out'], 'ln2_w': out['ln2_w'], 'w_ffn_gate': out['w_ffn_gate'], 'w_ffn_up': out['w_ffn_up'], 'w_ffn_down': out['w_ffn_down'], 'final_norm_w': out['final_norm_w'], 'loss_target': out['loss_target'], 'm_ln1_w': out['m_ln1_w'], 'm_w_in': out['m_w_in'], 'm_conv_w': out['m_conv_w'], 'm_conv_b': out['m_conv_b'], 'm_gate_a_w': out['m_gate_a_w'], 'm_gate_a_b': out['m_gate_a_b'], 'm_gate_x_w': out['m_gate_x_w'], 'm_gate_x_b': out['m_gate_x_b'], 'm_lru_lambda': out['m_lru_lambda'], 'm_ret_gn_w': out['m_ret_gn_w'], 'm_w_out': out['m_w_out'], 'm_ln2_w': out['m_ln2_w'], 'm_w_ffn_gate': out['m_w_ffn_gate'], 'm_w_ffn_up': out['m_w_ffn_up'], 'm_w_ffn_down': out['m_w_ffn_down'], 'm_final_norm_w': out['m_final_norm_w'], 'v_ln1_w': out['v_ln1_w'], 'v_w_in': out['v_w_in'], 'v_conv_w': out['v_conv_w'], 'v_conv_b': out['v_conv_b'], 'v_gate_a_w': out['v_gate_a_w'], 'v_gate_a_b': out['v_gate_a_b'], 'v_gate_x_w': out['v_gate_x_w'], 'v_gate_x_b': out['v_gate_x_b'], 'v_lru_lambda': out['v_lru_lambda'], 'v_ret_gn_w': out['v_ret_gn_w'], 'v_w_out': out['v_w_out'], 'v_ln2_w': out['v_ln2_w'], 'v_w_ffn_gate': out['v_w_ffn_gate'], 'v_w_ffn_up': out['v_w_ffn_up'], 'v_w_ffn_down': out['v_w_ffn_down'], 'v_final_norm_w': out['v_final_norm_w']}


def _loss(weights, diff, rest, loss_target):
    with _jax.named_scope("forward"):
        args = {**rest, TWIN_DIFF_INPUT: diff, **{k: w.astype(_WEIGHT_DTYPES[k]) for k, w in weights.items()}}
        y = _forward(args)
    with _jax.named_scope("loss_head"):
        err = _jnp.square(y.astype(_jnp.float32) - loss_target)
        return 0.5 * _jnp.sum(_jnp.mean(err, axis=-1)) if err.ndim else 0.5 * err


def _adamw(w, g, m, v):
    m = ADAM_B1 * m + (1.0 - ADAM_B1) * g
    v = ADAM_B2 * v + (1.0 - ADAM_B2) * _jnp.square(g)
    m_hat = m / (1.0 - ADAM_B1 ** ADAM_STEP)
    v_hat = v / (1.0 - ADAM_B2 ** ADAM_STEP)
    delta = -ADAM_LR * (m_hat / (_jnp.sqrt(v_hat) + ADAM_EPS) + ADAM_WD * w)
    return delta, m, v


def reference(x, ln1_w, w_in, conv_w, conv_b, gate_a_w, gate_a_b, gate_x_w, gate_x_b, lru_lambda, ret_gn_w, w_out, ln2_w, w_ffn_gate, w_ffn_up, w_ffn_down, final_norm_w, loss_target, m_ln1_w, m_w_in, m_conv_w, m_conv_b, m_gate_a_w, m_gate_a_b, m_gate_x_w, m_gate_x_b, m_lru_lambda, m_ret_gn_w, m_w_out, m_ln2_w, m_w_ffn_gate, m_w_ffn_up, m_w_ffn_down, m_final_norm_w, v_ln1_w, v_w_in, v_conv_w, v_conv_b, v_gate_a_w, v_gate_a_b, v_gate_x_w, v_gate_x_b, v_lru_lambda, v_ret_gn_w, v_w_out, v_ln2_w, v_w_ffn_gate, v_w_ffn_up, v_w_ffn_down, v_final_norm_w):
    given = dict(x=x, ln1_w=ln1_w, w_in=w_in, conv_w=conv_w, conv_b=conv_b, gate_a_w=gate_a_w, gate_a_b=gate_a_b, gate_x_w=gate_x_w, gate_x_b=gate_x_b, lru_lambda=lru_lambda, ret_gn_w=ret_gn_w, w_out=w_out, ln2_w=ln2_w, w_ffn_gate=w_ffn_gate, w_ffn_up=w_ffn_up, w_ffn_down=w_ffn_down, final_norm_w=final_norm_w, loss_target=loss_target, m_ln1_w=m_ln1_w, m_w_in=m_w_in, m_conv_w=m_conv_w, m_conv_b=m_conv_b, m_gate_a_w=m_gate_a_w, m_gate_a_b=m_gate_a_b, m_gate_x_w=m_gate_x_w, m_gate_x_b=m_gate_x_b, m_lru_lambda=m_lru_lambda, m_ret_gn_w=m_ret_gn_w, m_w_out=m_w_out, m_ln2_w=m_ln2_w, m_w_ffn_gate=m_w_ffn_gate, m_w_ffn_up=m_w_ffn_up, m_w_ffn_down=m_w_ffn_down, m_final_norm_w=m_final_norm_w, v_ln1_w=v_ln1_w, v_w_in=v_w_in, v_conv_w=v_conv_w, v_conv_b=v_conv_b, v_gate_a_w=v_gate_a_w, v_gate_a_b=v_gate_a_b, v_gate_x_w=v_gate_x_w, v_gate_x_b=v_gate_x_b, v_lru_lambda=v_lru_lambda, v_ret_gn_w=v_ret_gn_w, v_w_out=v_w_out, v_ln2_w=v_ln2_w, v_w_ffn_gate=v_w_ffn_gate, v_w_ffn_up=v_w_ffn_up, v_w_ffn_down=v_w_ffn_down, v_final_norm_w=v_final_norm_w)
    weights = {n: given[n] for n in TWIN_WEIGHTS}
    shared = {n: given[n] for n in SHARED_INPUTS}
    per_example = {n: given[n] for n in ['x']}
    grad_fn = _jax.value_and_grad(_loss, argnums=(0, 1))

    def one_microbatch(ex, loss_target):
        ex = dict(ex)
        diff = ex.pop(TWIN_DIFF_INPUT)
        return grad_fn(weights, diff, {**shared, **ex}, loss_target)

    if N_MICROBATCH == 1:
        loss, (grad_w, grad_x) = one_microbatch(per_example, given["loss_target"])
    else:
        def body(carry, xs):
            loss_sum, grad_sum = carry
            l_k, (gw_k, gx_k) = one_microbatch(xs[0], xs[1])
            with _jax.named_scope("update"):
                return (loss_sum + l_k, _jax.tree.map(_jnp.add, grad_sum, gw_k)), gx_k

        init = (_jnp.zeros((), _jnp.float32), _jax.tree.map(_jnp.zeros_like, weights))
        (loss, grad_w), grad_x = _jax.lax.scan(body, init, (per_example, given["loss_target"]))
    with _jax.named_scope("update"):
        delta_w, new_m, new_v = {}, {}, {}
        for n in TWIN_WEIGHTS:
            delta_w[n], new_m[n], new_v[n] = _adamw(weights[n], grad_w[n], given["m_" + n], given["v_" + n])
    return (loss, grad_x, *[grad_w[n] for n in TWIN_WEIGHTS], *[delta_w[n] for n in TWIN_WEIGHTS],
            *[new_m[n] for n in TWIN_WEIGHTS], *[new_v[n] for n in TWIN_WEIGHTS])
```

```python
import functools

import jax
import jax.numpy as jnp
from jax import lax
from jax.experimental import pallas as pl
from jax.experimental.pallas import tpu as pltpu

F32 = jnp.float32
BF16 = jnp.bfloat16
MESH = pl.DeviceIdType.MESH

EPS = 1e-6
LRU_C = 8.0
ROPE_BASE = 10000.0
CHUNK = 128
HEAD_DIM = 128
CONV_W = 4
ADAM_LR = 0.001
ADAM_B1 = 0.9
ADAM_B2 = 0.999
ADAM_EPS = 1e-08
ADAM_WD = 0.01
ADAM_STEP = 10

V7X_VMEM_BYTES = 64 * 1024 * 1024
VMEM_LIMIT = V7X_VMEM_BYTES - 8 * 1024 * 1024
LANES = 128
SUBLANES_16BIT = 16

NN = (((1,), (0,)), ((), ()))
NT = (((1,), (1,)), ((), ()))
TN = (((0,), (0,)), ((), ()))


def _dot(a, b, dims=NN):
    return lax.dot_general(a, b, dims, preferred_element_type=F32)


def _tile(n, pref, mult=SUBLANES_16BIT):
    best = None
    t = mult
    while t <= min(n, pref):
        if n % t == 0:
            best = t
        t += mult
    return best if best is not None else n


def _row_tile(rows, cols, budget_bytes=2 * 1024 * 1024):
    return _tile(rows, max(SUBLANES_16BIT, budget_bytes // (cols * 4)))


def _params(sem):
    return pltpu.CompilerParams(dimension_semantics=sem, vmem_limit_bytes=VMEM_LIMIT)


def _ew(name, fn, ins, outs, grid, sp=None):
    n_in = len(ins)

    def body(*refs):
        if sp is not None:
            refs = refs[1:]
        vals = [r[...] for r in refs[:n_in]]
        res = fn(*vals)
        for o_ref, v in zip(refs[n_in:], res):
            o_ref[...] = v.astype(o_ref.dtype)

    in_specs = [s for _, s in ins]
    out_specs = [s for _, s in outs]
    out_shape = [s for s, _ in outs]
    sem = ("arbitrary",) * len(grid)
    if sp is None:
        return pl.pallas_call(body, grid=grid, in_specs=in_specs, out_specs=out_specs, out_shape=out_shape,
                              name=name, compiler_params=_params(sem))(*[a for a, _ in ins])
    gs = pltpu.PrefetchScalarGridSpec(num_scalar_prefetch=1, grid=grid, in_specs=in_specs, out_specs=out_specs)
    return pl.pallas_call(body, grid_spec=gs, out_shape=out_shape, name=name,
                          compiler_params=_params(sem))(sp, *[a for a, _ in ins])


def _matmul(name, pairs, dims, grid, out_shape, out_spec, acc_shape, res=None):
    n = len(pairs)
    nk = grid[2]

    def body(*refs):
        ab = refs[:2 * n]
        pos = 2 * n
        res_ref = None
        if res is not None:
            res_ref = refs[pos]
            pos += 1
        o_ref = refs[pos]
        acc_ref = refs[pos + 1] if nk > 1 else None

        def partial():
            t = None
            for p in range(n):
                d = _dot(ab[2 * p][...], ab[2 * p + 1][...], dims)
                t = d if t is None else t + d
            return t

        def finish(t):
            if res_ref is not None:
                t = t + res_ref[...]
            o_ref[...] = t.astype(o_ref.dtype)

        if nk == 1:
            finish(partial())
        else:
            k = pl.program_id(2)

            @pl.when(k == 0)
            def _():
                acc_ref[...] = partial()

            @pl.when(k > 0)
            def _():
                acc_ref[...] += partial()

            @pl.when(k == nk - 1)
            def _():
                finish(acc_ref[...])

    operands, in_specs = [], []
    for a, a_spec, b, b_spec in pairs:
        operands += [a, b]
        in_specs += [a_spec, b_spec]
    if res is not None:
        operands.append(res[0])
        in_specs.append(res[1])
    scratch = [pltpu.VMEM(acc_shape, F32)] if nk > 1 else []
    return pl.pallas_call(body, grid=grid, in_specs=in_specs, out_specs=out_spec, out_shape=out_shape,
                          scratch_shapes=scratch, name=name,
                          compiler_params=_params(("arbitrary", "arbitrary", "arbitrary")))(*operands)


def _mm_nn_stacked(name, a, b_st, out_dtype, tm):
    M, K = a.shape
    J, _, Nj = b_st.shape
    tm = _tile(M, tm)
    return _matmul(
        name, [(a, pl.BlockSpec((tm, K), lambda j, i, k: (i, 0)), b_st, pl.BlockSpec((None, K, Nj), lambda j, i, k: (j, 0, 0)))],
        NN, (J, M // tm, 1), jax.ShapeDtypeStruct((M, J * Nj), out_dtype), pl.BlockSpec((tm, Nj), lambda j, i, k: (i, j)), None)


def _mm_nn(name, a, b, res, out_dtype, tm, tn, tk):
    M, K = a.shape
    N = b.shape[1]
    tm, tn, tk = _tile(M, tm), _tile(N, tn, LANES), _tile(K, tk, LANES)
    return _matmul(
        name, [(a, pl.BlockSpec((tm, tk), lambda j, i, k: (i, k)), b, pl.BlockSpec((tk, tn), lambda j, i, k: (k, j)))],
        NN, (N // tn, M // tm, K // tk), jax.ShapeDtypeStruct((M, N), out_dtype), pl.BlockSpec((tm, tn), lambda j, i, k: (i, j)),
        (tm, tn), res=(res, pl.BlockSpec((tm, tn), lambda j, i, k: (i, j))))


def _mm_nt(name, a, b, out_dtype, tm, tn):
    M, K = a.shape
    N = b.shape[0]
    tm, tn = _tile(M, tm), _tile(N, tn, LANES)
    return _matmul(
        name, [(a, pl.BlockSpec((tm, K), lambda j, i, k: (i, 0)), b, pl.BlockSpec((tn, K), lambda j, i, k: (j, 0)))],
        NT, (N // tn, M // tm, 1), jax.ShapeDtypeStruct((M, N), out_dtype), pl.BlockSpec((tm, tn), lambda j, i, k: (i, j)), None)


def _mm_nt_stacked(name, a_list, b_list, out_dtype, tm, tn):
    M = a_list[0].shape[0]
    J, N, Nj = b_list[0].shape
    tm, tn = _tile(M, tm), _tile(N, tn, LANES)
    pairs = [(a, pl.BlockSpec((tm, Nj), lambda j, i, k: (i, k)), b, pl.BlockSpec((None, tn, Nj), lambda j, i, k: (k, j, 0)))
             for a, b in zip(a_list, b_list)]
    return _matmul(name, pairs, NT, (N // tn, M // tm, J), jax.ShapeDtypeStruct((M, N), out_dtype),
                   pl.BlockSpec((tm, tn), lambda j, i, k: (i, j)), (tm, tn))


def _mm_tn(name, a, b, tmo, tn, tk, stacked_cols=None):
    S, Mo = a.shape
    N = b.shape[1]
    tmo, tk = _tile(Mo, tmo, LANES), _tile(S, tk)
    if stacked_cols is None:
        tn = _tile(N, tn, LANES)
        out_shape = jax.ShapeDtypeStruct((Mo, N), F32)
        out_spec = pl.BlockSpec((tmo, tn), lambda i, j, k: (i, j))
    else:
        tn = stacked_cols
        out_shape = jax.ShapeDtypeStruct((N // tn, Mo, tn), F32)
        out_spec = pl.BlockSpec((None, tmo, tn), lambda i, j, k: (j, i, 0))
    return _matmul(
        name, [(a, pl.BlockSpec((tk, tmo), lambda i, j, k: (k, i)), b, pl.BlockSpec((tk, tn), lambda i, j, k: (k, j)))],
        TN, (Mo // tmo, N // tn, S // tk), out_shape, out_spec, (tmo, tn))


def _rms_fwd(name, x, w, tm):
    S, D = x.shape
    tm = _tile(S, tm)

    def fn(xv, wv):
        r = lax.rsqrt(jnp.mean(xv * xv, axis=-1, keepdims=True) + EPS)
        return ((xv * r) * wv,)

    row = pl.BlockSpec((tm, D), lambda i: (i, 0))
    return _ew(name, fn, [(x, row), (w, pl.BlockSpec((1, D), lambda i: (0, 0)))],
               [(jax.ShapeDtypeStruct((S, D), BF16), row)], (S // tm,))[0]


def _rms_bwd(name, x, w, dy, dres, tm, want_bf16):
    S, D = x.shape
    tm = _tile(S, tm)

    def body(x_ref, w_ref, dy_ref, dres_ref, dx_ref, *rest):
        dw_ref = rest[-1]
        i = pl.program_id(0)

        @pl.when(i == 0)
        def _():
            dw_ref[...] = jnp.zeros_like(dw_ref)

        xv = x_ref[...]
        r = lax.rsqrt(jnp.mean(xv * xv, axis=-1, keepdims=True) + EPS)
        nv = xv * r
        dyv = dy_ref[...]
        dn = dyv * w_ref[...]
        dw_ref[...] += jnp.sum(dyv * nv, axis=0, keepdims=True)
        dx = dres_ref[...] + r * (dn - nv * jnp.mean(dn * nv, axis=-1, keepdims=True))
        dx_ref[...] = dx
        if want_bf16:
            rest[0][...] = dx.astype(BF16)

    row = pl.BlockSpec((tm, D), lambda i: (i, 0))
    vec = pl.BlockSpec((1, D), lambda i: (0, 0))
    out_shape = [jax.ShapeDtypeStruct((S, D), F32)] + ([jax.ShapeDtypeStruct((S, D), BF16)] if want_bf16 else []) + \
                [jax.ShapeDtypeStruct((1, D), F32)]
    out_specs = [row] + ([row] if want_bf16 else []) + [vec]
    return pl.pallas_call(body, grid=(S // tm,), in_specs=[row, vec, row, row], out_specs=out_specs, out_shape=out_shape,
                          name=name, compiler_params=_params(("arbitrary",)))(x, w, dy, dres)


def _loss_head(h2, tgt, fw, tm):
    S, D = h2.shape
    tm = _tile(S, tm)

    def body(h_ref, t_ref, w_ref, dh_ref, dhb_ref, dw_ref, loss_ref):
        i = pl.program_id(0)

        @pl.when(i == 0)
        def _():
            dw_ref[...] = jnp.zeros_like(dw_ref)
            loss_ref[...] = jnp.zeros_like(loss_ref)

        hv = h_ref[...]
        wv = w_ref[...]
        r = lax.rsqrt(jnp.mean(hv * hv, axis=-1, keepdims=True) + EPS)
        nv = hv * r
        err = nv * wv - t_ref[...]
        row_loss = jnp.mean(err * err, axis=-1, keepdims=True)
        loss_ref[...] += 0.5 * jnp.sum(row_loss, axis=0, keepdims=True)
        dyo = err * (1.0 / D)
        dn = dyo * wv
        dw_ref[...] += jnp.sum(dyo * nv, axis=0, keepdims=True)
        dh = r * (dn - nv * jnp.mean(dn * nv, axis=-1, keepdims=True))
        dh_ref[...] = dh
        dhb_ref[...] = dh.astype(BF16)

    row = pl.BlockSpec((tm, D), lambda i: (i, 0))
    vec = pl.BlockSpec((1, D), lambda i: (0, 0))
    return pl.pallas_call(
        body, grid=(S // tm,), in_specs=[row, row, vec],
        out_specs=[row, row, vec, pl.BlockSpec((1, LANES), lambda i: (0, 0))],
        out_shape=[jax.ShapeDtypeStruct((S, D), F32), jax.ShapeDtypeStruct((S, D), BF16),
                   jax.ShapeDtypeStruct((1, D), F32), jax.ShapeDtypeStruct((1, LANES), F32)],
        name="loss_head", compiler_params=_params(("arbitrary",)))(h2, tgt, fw)


def _swiglu_fwd(gt, up):
    S, F = gt.shape
    tr = _row_tile(S, F)

    def fn(g, u):
        return (g * jax.nn.sigmoid(g) * u,)

    row = pl.BlockSpec((tr, F), lambda i: (i, 0))
    return _ew("swiglu_fwd", fn, [(gt, row), (up, row)], [(jax.ShapeDtypeStruct((S, F), BF16), row)], (S // tr,))[0]


def _swiglu_bwd(dff, gt, up):
    S, F = gt.shape
    tr = _row_tile(S, F)

    def fn(d, g, u):
        sg = jax.nn.sigmoid(g)
        return d * u * (sg * (1.0 + g * (1.0 - sg))), d * (g * sg)

    row = pl.BlockSpec((tr, F), lambda i: (i, 0))
    o = jax.ShapeDtypeStruct((S, F), BF16)
    return _ew("swiglu_bwd", fn, [(dff, row), (gt, row), (up, row)], [(o, row), (o, row)], (S // tr,))


def _shift_down(x, d, head8):
    r = pltpu.roll(x, d, 0)
    rh = pltpu.roll(head8, d, 0)
    row8 = lax.broadcasted_iota(jnp.int32, head8.shape, 0)
    top = jnp.where(row8 < d, rh, r[0:8])
    return jnp.concatenate([top, r[8:]], axis=0)


def _shift_up(x, d, tail8):
    n = x.shape[0]
    r = pltpu.roll(x, n - d, 0)
    rt = pltpu.roll(tail8, 8 - d, 0)
    row8 = lax.broadcasted_iota(jnp.int32, tail8.shape, 0)
    bot = jnp.where(row8 + d >= 8, rt, r[n - 8:n])
    return jnp.concatenate([r[:n - 8], bot], axis=0)


def _log_sigmoid(lam):
    z = jnp.exp(-jnp.abs(lam))
    u = 1.0 + z
    log1p = jnp.where(u == 1.0, z, jnp.log(u) * (z / jnp.where(u == 1.0, 1.0, u - 1.0)))
    return jnp.minimum(lam, 0.0) - log1p


def _neg_expm1(z):
    series = -z * (1.0 + z * (0.5 + z * (1.0 / 6.0 + z * (1.0 / 24.0 + z * (1.0 / 120.0)))))
    return jnp.where(z > -0.05, series, 1.0 - jnp.exp(z))


_GELU_C = 0.7978845608028654


def _gelu(x):
    t = jnp.tanh(_GELU_C * (x + 0.044715 * (x * x * x)))
    return x * (0.5 * (1.0 + t)), t


def _gelu_grad(x, t):
    return 0.5 * (1.0 + t) + 0.5 * x * (1.0 - t * t) * (_GELU_C * (1.0 + 3.0 * 0.044715 * (x * x)))


def _lru_gates(lx, head8, cw, cb, wa_ref, ba, wx_ref, bx, ls):
    nb = wa_ref.shape[0]
    sh = [lx] + [_shift_down(lx, d, head8) for d in (1, 2, 3)]
    cx = cb + sh[3] * cw[0:1]
    cx = cx + sh[2] * cw[1:2]
    cx = cx + sh[1] * cw[2:3]
    cx = cx + sh[0] * cw[3:4]
    cxb = cx.astype(BF16)
    ra = jnp.concatenate([_dot(cxb[:, n * HEAD_DIM:(n + 1) * HEAD_DIM], wa_ref[n]) for n in range(nb)], axis=1) + ba
    ia = jnp.concatenate([_dot(cxb[:, n * HEAD_DIM:(n + 1) * HEAD_DIM], wx_ref[n]) for n in range(nb)], axis=1) + bx
    r = jax.nn.sigmoid(ra)
    ig = jax.nn.sigmoid(ia)
    log_a = LRU_C * r * ls
    a = jnp.exp(log_a)
    mult = jnp.sqrt(_neg_expm1(2.0 * log_a))
    return sh, cx, cxb, r, ig, a, mult


def _lru_specs(tl, DL):
    nb = DL // HEAD_DIM
    vec = pl.BlockSpec((1, DL), lambda i: (0, 0))
    return [pl.BlockSpec((CONV_W, DL), lambda i: (0, 0)), vec,
            pl.BlockSpec((nb, HEAD_DIM, HEAD_DIM), lambda i: (0, 0, 0)), vec,
            pl.BlockSpec((nb, HEAD_DIM, HEAD_DIM), lambda i: (0, 0, 0)), vec, vec]


def _lru_fwd(proj, cw, cb, wa, ba, wx, bx, lam, tl, d_mix):
    S = proj.shape[0]
    DL = cb.shape[1]
    tl = _tile(S, tl)

    def body(lx_ref, lg_ref, cw_ref, cb_ref, wa_ref, ba_ref, wx_ref, bx_ref, lam_ref, h_ref, y_ref, prev8, hc, a_s, b_s):
        i = pl.program_id(0)

        @pl.when(i == 0)
        def _():
            prev8[...] = jnp.zeros_like(prev8)
            hc[...] = jnp.zeros_like(hc)

        lx = lx_ref[...]
        ls = _log_sigmoid(lam_ref[...])
        _, cx, _, _, ig, a, mult = _lru_gates(lx, prev8[...], cw_ref[...], cb_ref[...], wa_ref, ba_ref[...],
                                              wx_ref, bx_ref[...], ls)
        b = mult * (ig * cx)
        row = lax.broadcasted_iota(jnp.int32, a.shape, 0) & 7
        for d in (1, 2, 4):
            a_sh = pltpu.roll(a, d, 0)
            b_sh = pltpu.roll(b, d, 0)
            m = row >= d
            b = jnp.where(m, a * b_sh + b, b)
            a = jnp.where(m, a * a_sh, a)
        a_s[...] = a
        b_s[...] = b

        def step(g, hprev):
            sl = pl.ds(pl.multiple_of(g * 8, 8), 8)
            hh = a_s[sl, :] * hprev + b_s[sl, :]
            h_ref[sl, :] = hh
            return hh[7:8, :]

        hc[0:1, :] = lax.fori_loop(0, tl // 8, step, hc[0:1, :])
        prev8[...] = lx[tl - 8:tl]
        g, _ = _gelu(lg_ref[...])
        y_ref[...] = (h_ref[...] * g).astype(BF16)

    return pl.pallas_call(
        body, grid=(S // tl,),
        in_specs=[pl.BlockSpec((tl, DL), lambda i: (i, 0)), pl.BlockSpec((tl, DL), lambda i: (i, 1))] + _lru_specs(tl, DL),
        out_specs=[pl.BlockSpec((tl, DL), lambda i: (i, 0)), pl.BlockSpec((tl, DL), lambda i: (i, 0))],
        out_shape=[jax.ShapeDtypeStruct((S, DL), F32), jax.ShapeDtypeStruct((S, d_mix), BF16)],
        scratch_shapes=[pltpu.VMEM((8, DL), F32), pltpu.VMEM((8, DL), F32), pltpu.VMEM((tl, DL), F32), pltpu.VMEM((tl, DL), F32)],
        name="lru_fwd", compiler_params=_params(("arbitrary",)))(proj, proj, cw, cb, wa, ba, wx, bx, lam)


def _lru_bwd(proj, h, dy, cw, cb, wa, ba, wx, bx, lam, tl):
    S = proj.shape[0]
    DL = cb.shape[1]
    nb = DL // HEAD_DIM
    tl = _tile(S, tl)
    nt = S // tl
    ng = tl // 8
    t8 = tl // 8

    def body(lx_ref, lxp_ref, lg_ref, h_ref, hp_ref, dy_ref, cw_ref, cb_ref, wa_ref, ba_ref, wx_ref, bx_ref, lam_ref,
             dlxg_ref, dcw_ref, dcb_ref, dwa_ref, dba_ref, dwx_ref, dbx_ref, dlam_ref,
             a_next, g_carry, dcx_next, an_s, dh_s, g_s):
        i = pl.program_id(0)

        @pl.when(i == 0)
        def _():
            for ref in (dcw_ref, dcb_ref, dwa_ref, dba_ref, dwx_ref, dbx_ref, dlam_ref, a_next, g_carry, dcx_next):
                ref[...] = jnp.zeros_like(ref)

        first = i == nt - 1
        lx = lx_ref[...]
        hv = h_ref[...]
        lg = lg_ref[...]
        dyv = dy_ref[...]
        head8 = jnp.where(first, 0.0, lxp_ref[...])
        hhead8 = jnp.where(first, 0.0, hp_ref[...])
        lamv = lam_ref[...]
        ls = _log_sigmoid(lamv)
        cwv = cw_ref[...]
        sh, cx, cxb, r, ig, a, mult = _lru_gates(lx, head8, cwv, cb_ref[...], wa_ref, ba_ref[...], wx_ref, bx_ref[...], ls)
        hprev = _shift_down(hv, 1, hhead8)
        g, t = _gelu(lg)
        dlg = dyv * hv * _gelu_grad(lg, t)
        dh = dyv * g
        an = _shift_up(a, 1, a_next[...])
        row = lax.broadcasted_iota(jnp.int32, a.shape, 0) & 7
        for d in (1, 2, 4):
            an_sh = pltpu.roll(an, tl - d, 0)
            dh_sh = pltpu.roll(dh, tl - d, 0)
            m = row + d < 8
            dh = jnp.where(m, an * dh_sh + dh, dh)
            an = jnp.where(m, an * an_sh, an)
        an_s[...] = an
        dh_s[...] = dh

        def step(k, gc):
            sl = pl.ds(pl.multiple_of((ng - 1 - k) * 8, 8), 8)
            gg = an_s[sl, :] * gc + dh_s[sl, :]
            g_s[sl, :] = gg
            return gg[0:1, :]

        g_carry[0:1, :] = lax.fori_loop(0, ng, step, g_carry[0:1, :])
        a_next[...] = a[0:8]
        G = g_s[...]
        da = G * hprev
        icx = ig * cx
        dmult = G * icx
        dicx = G * mult
        di = dicx * cx
        dcx = dicx * ig
        dlog = da * a - dmult * (a * a) / mult
        dr = dlog * (LRU_C * ls)
        dlam_ref[...] += jnp.sum(dlog * (LRU_C * r), axis=0, keepdims=True)
        dra = dr * r * (1.0 - r)
        dia = di * ig * (1.0 - ig)
        dba_ref[...] += jnp.sum(dra, axis=0, keepdims=True)
        dbx_ref[...] += jnp.sum(dia, axis=0, keepdims=True)
        drab = dra.astype(BF16)
        diab = dia.astype(BF16)
        back = []
        for n in range(nb):
            cs = slice(n * HEAD_DIM, (n + 1) * HEAD_DIM)
            dwa_ref[n] += _dot(cxb[:, cs], drab[:, cs], TN)
            dwx_ref[n] += _dot(cxb[:, cs], diab[:, cs], TN)
            back.append(_dot(drab[:, cs], wa_ref[n], NT) + _dot(diab[:, cs], wx_ref[n], NT))
        dcx = dcx + jnp.concatenate(back, axis=1)
        dcb_ref[...] += jnp.sum(dcx, axis=0, keepdims=True)
        for tap in range(CONV_W):
            dcw_ref[tap:tap + 1, :] += jnp.sum(dcx * sh[CONV_W - 1 - tap], axis=0, keepdims=True)
        tail = dcx_next[...]
        dlx = dcx * cwv[3:4]
        for d in (1, 2, 3):
            dlx = dlx + _shift_up(dcx, d, tail) * cwv[3 - d:4 - d]
        dcx_next[...] = dcx[0:8]
        dlxg_ref[:, 0:DL] = dlx.astype(BF16)
        dlxg_ref[:, DL:2 * DL] = dlg.astype(BF16)

        @pl.when(i == nt - 1)
        def _():
            dlam_ref[...] = dlam_ref[...] * (1.0 - jax.nn.sigmoid(lamv))

    rev = lambda i: nt - 1 - i
    prev8_map = lambda i: (jnp.maximum((nt - 1 - i) * t8 - 1, 0), 0)
    vec = pl.BlockSpec((1, DL), lambda i: (0, 0))
    mat = pl.BlockSpec((nb, HEAD_DIM, HEAD_DIM), lambda i: (0, 0, 0))
    return pl.pallas_call(
        body, grid=(nt,),
        in_specs=[pl.BlockSpec((tl, DL), lambda i: (rev(i), 0)), pl.BlockSpec((8, DL), prev8_map),
                  pl.BlockSpec((tl, DL), lambda i: (rev(i), 1)),
                  pl.BlockSpec((tl, DL), lambda i: (rev(i), 0)), pl.BlockSpec((8, DL), prev8_map),
                  pl.BlockSpec((tl, DL), lambda i: (rev(i), 0))] + _lru_specs(tl, DL),
        out_specs=[pl.BlockSpec((tl, 2 * DL), lambda i: (rev(i), 0)), pl.BlockSpec((CONV_W, DL), lambda i: (0, 0)), vec,
                   mat, vec, mat, vec, vec],
        out_shape=[jax.ShapeDtypeStruct((S, 2 * DL), BF16), jax.ShapeDtypeStruct((CONV_W, DL), F32),
                   jax.ShapeDtypeStruct((1, DL), F32), jax.ShapeDtypeStruct((nb, HEAD_DIM, HEAD_DIM), F32),
                   jax.ShapeDtypeStruct((1, DL), F32), jax.ShapeDtypeStruct((nb, HEAD_DIM, HEAD_DIM), F32),
                   jax.ShapeDtypeStruct((1, DL), F32), jax.ShapeDtypeStruct((1, DL), F32)],
        scratch_shapes=[pltpu.VMEM((8, DL), F32), pltpu.VMEM((8, DL), F32), pltpu.VMEM((8, DL), F32),
                        pltpu.VMEM((tl, DL), F32), pltpu.VMEM((tl, DL), F32), pltpu.VMEM((tl, DL), F32)],
        name="lru_bwd", compiler_params=_params(("arbitrary",)))(proj, proj, proj, h, h, dy, cw, cb, wa, ba, wx, bx, lam)


def _ret_tables(S, H):
    pos = jnp.arange(S, dtype=F32)
    inv_freq = ROPE_BASE ** (-jnp.arange(0, HEAD_DIM, 2, dtype=F32) / HEAD_DIM)
    ang = pos[:, None] * inv_freq[None, :]
    cos, sin = jnp.cos(ang), jnp.sin(ang)
    cosf = jnp.concatenate([cos, cos], axis=1)
    sins = jnp.concatenate([-sin, sin], axis=1)
    log_gamma = jnp.log1p(-jnp.exp2(-5.0 - jnp.arange(H, dtype=F32)))
    idx = jnp.arange(CHUNK)
    diff = idx[:, None] - idx[None, :]
    causal = diff >= 0
    decay = jnp.where(causal[None], jnp.exp(log_gamma[:, None, None] * jnp.where(causal, diff, 0)[None].astype(F32)), 0.0)
    zeta = jnp.exp(log_gamma[:, None] * (CHUNK - 1 - idx).astype(F32)[None, :])
    xi = jnp.exp(log_gamma[:, None] * (idx + 1).astype(F32)[None, :])
    gc = jnp.exp(log_gamma * CHUNK)
    lanes = (H, CHUNK, HEAD_DIM)
    return (cosf, sins, decay, jnp.broadcast_to(zeta[:, :, None], lanes), jnp.broadcast_to(xi[:, :, None], lanes),
            jnp.broadcast_to(gc[:, None, None], lanes))


def _rope(t, cos, sin_signed):
    return t * cos + pltpu.roll(t, HEAD_DIM // 2, 1) * sin_signed


def _rope_t(d, cos, sin_signed):
    return d * cos + pltpu.roll(d * sin_signed, HEAD_DIM // 2, 1)


def _ret_const_specs(H, DR):
    full = pl.BlockSpec((H, CHUNK, HEAD_DIM), lambda i: (0, 0, 0))
    return [full, full, full, full, pl.BlockSpec((1, DR), lambda i: (0, 0))]


def _ret_fwd(proj, y, tables, gnw, tb):
    S = proj.shape[0]
    DR = gnw.shape[1]
    H = DR // HEAD_DIM
    tb = _tile(S, tb, CHUNK)
    nc = tb // CHUNK
    cosf, sins, dm, zeta, xi, gc = tables
    scale = HEAD_DIM ** -0.5

    def body(qk_ref, vg_ref, cos_ref, sin_ref, dm_ref, zeta_ref, xi_ref, gc_ref, gnw_ref, y_in, y_ref, rprev_ref, r_s):
        del y_in
        i = pl.program_id(0)

        @pl.when(i == 0)
        def _():
            r_s[...] = jnp.zeros_like(r_s)

        def chunk(c, carry):
            rows = pl.ds(pl.multiple_of(c * CHUNK, CHUNK), CHUNK)
            cos = cos_ref[rows, :]
            sin = sin_ref[rows, :]
            for h in range(H):
                c0 = slice(h * HEAD_DIM, (h + 1) * HEAD_DIM)
                c1 = slice(DR + h * HEAD_DIM, DR + (h + 1) * HEAD_DIM)
                qh = _rope(qk_ref[rows, c0], cos, sin)
                kh = _rope(qk_ref[rows, c1], cos, sin) * scale
                vb = vg_ref[rows, c0].astype(BF16)
                gate = vg_ref[rows, c1]
                s = _dot(qh.astype(BF16), kh.astype(BF16), NT) * dm_ref[h]
                rp = r_s[h]
                rpb = rp.astype(BF16)
                rprev_ref[c, h] = rpb
                o = _dot(s.astype(BF16), vb) + _dot((qh * xi_ref[h]).astype(BF16), rpb)
                r_s[h] = rp * gc_ref[h] + _dot((kh * zeta_ref[h]).astype(BF16), vb, TN)
                mu = jnp.mean(o, axis=-1, keepdims=True)
                oc = o - mu
                var = jnp.mean(oc * oc, axis=-1, keepdims=True)
                on = oc * lax.rsqrt(var + EPS) * gnw_ref[:, c0]
                y_ref[rows, c0] = (gate * jax.nn.sigmoid(gate) * on).astype(BF16)
            return carry

        lax.fori_loop(0, nc, chunk, 0)

    return pl.pallas_call(
        body, grid=(S // tb,),
        in_specs=[pl.BlockSpec((tb, 2 * DR), lambda i: (i, 1)), pl.BlockSpec((tb, 2 * DR), lambda i: (i, 2)),
                  pl.BlockSpec((tb, HEAD_DIM), lambda i: (i, 0)), pl.BlockSpec((tb, HEAD_DIM), lambda i: (i, 0))]
        + _ret_const_specs(H, DR) + [pl.BlockSpec(memory_space=pl.ANY)],
        out_specs=[pl.BlockSpec((tb, DR), lambda i: (i, 1)),
                   pl.BlockSpec((nc, H, CHUNK, HEAD_DIM), lambda i: (i, 0, 0, 0))],
        out_shape=[jax.ShapeDtypeStruct(y.shape, BF16), jax.ShapeDtypeStruct((S // CHUNK, H, CHUNK, HEAD_DIM), BF16)],
        scratch_shapes=[pltpu.VMEM((H, CHUNK, HEAD_DIM), F32)],
        input_output_aliases={9: 0},
        name="ret_fwd", compiler_params=_params(("arbitrary",)))(proj, proj, cosf, sins, dm, zeta, xi, gc, gnw, y)


def _ret_bwd(proj, rprev, dy, tables, gnw, tb):
    S = proj.shape[0]
    DR = gnw.shape[1]
    H = DR // HEAD_DIM
    tb = _tile(S, tb, CHUNK)
    nc = tb // CHUNK
    nt = S // tb
    cosf, sins, dm, zeta, xi, gc = tables
    scale = HEAD_DIM ** -0.5

    def body(qk_ref, vg_ref, cos_ref, sin_ref, dm_ref, zeta_ref, xi_ref, gc_ref, gnw_ref, rprev_ref, dy_ref,
             dp_ref, dgn_ref, dr_s):
        i = pl.program_id(0)

        @pl.when(i == 0)
        def _():
            dr_s[...] = jnp.zeros_like(dr_s)
            dgn_ref[...] = jnp.zeros_like(dgn_ref)

        def chunk(cc, carry):
            c = nc - 1 - cc
            rows = pl.ds(pl.multiple_of(c * CHUNK, CHUNK), CHUNK)
            cos = cos_ref[rows, :]
            sin = sin_ref[rows, :]
            for h in range(H):
                c0 = slice(h * HEAD_DIM, (h + 1) * HEAD_DIM)
                c1 = slice(DR + h * HEAD_DIM, DR + (h + 1) * HEAD_DIM)
                c2 = slice(2 * DR + h * HEAD_DIM, 2 * DR + (h + 1) * HEAD_DIM)
                c3 = slice(3 * DR + h * HEAD_DIM, 3 * DR + (h + 1) * HEAD_DIM)
                qh = _rope(qk_ref[rows, c0], cos, sin)
                kh = _rope(qk_ref[rows, c1], cos, sin) * scale
                qb = qh.astype(BF16)
                kb = kh.astype(BF16)
                vb = vg_ref[rows, c0].astype(BF16)
                gate = vg_ref[rows, c1]
                dmh = dm_ref[h]
                xih = xi_ref[h]
                zetah = zeta_ref[h]
                sb = (_dot(qb, kb, NT) * dmh).astype(BF16)
                rpb = rprev_ref[c, h]
                qx = (qh * xih).astype(BF16)
                kz = (kh * zetah).astype(BF16)
                o = _dot(sb, vb) + _dot(qx, rpb)
                mu = jnp.mean(o, axis=-1, keepdims=True)
                oc = o - mu
                rstd = lax.rsqrt(jnp.mean(oc * oc, axis=-1, keepdims=True) + EPS)
                ohat = oc * rstd
                gw = gnw_ref[:, c0]
                sg = jax.nn.sigmoid(gate)
                dyv = dy_ref[rows, c0]
                dgate = dyv * (ohat * gw) * (sg * (1.0 + gate * (1.0 - sg)))
                don = dyv * (gate * sg)
                dgn_ref[:, c0] += jnp.sum(don * ohat, axis=0, keepdims=True)
                dohat = don * gw
                do = rstd * (dohat - jnp.mean(dohat, axis=-1, keepdims=True)
                             - ohat * jnp.mean(dohat * ohat, axis=-1, keepdims=True))
                dob = do.astype(BF16)
                drh = dr_s[h]
                drb = drh.astype(BF16)
                dv = _dot(sb, dob, TN) + _dot(kz, drb)
                dsb = (_dot(dob, vb, NT) * dmh).astype(BF16)
                dqh = _dot(dsb, kb) + _dot(dob, rpb, NT) * xih
                dkh = _dot(dsb, qb, TN) + _dot(vb, drb, NT) * zetah
                dr_s[h] = drh * gc_ref[h] + _dot(qx, dob, TN)
                dp_ref[rows, c0] = _rope_t(dqh, cos, sin).astype(BF16)
                dp_ref[rows, c1] = _rope_t(dkh * scale, cos, sin).astype(BF16)
                dp_ref[rows, c2] = dv.astype(BF16)
                dp_ref[rows, c3] = dgate.astype(BF16)
            return carry

        lax.fori_loop(0, nc, chunk, 0)

    rev = lambda i: nt - 1 - i
    return pl.pallas_call(
        body, grid=(nt,),
        in_specs=[pl.BlockSpec((tb, 2 * DR), lambda i: (rev(i), 1)), pl.BlockSpec((tb, 2 * DR), lambda i: (rev(i), 2)),
                  pl.BlockSpec((tb, HEAD_DIM), lambda i: (rev(i), 0)), pl.BlockSpec((tb, HEAD_DIM), lambda i: (rev(i), 0))]
        + _ret_const_specs(H, DR)
        + [pl.BlockSpec((nc, H, CHUNK, HEAD_DIM), lambda i: (rev(i), 0, 0, 0)), pl.BlockSpec((tb, DR), lambda i: (rev(i), 1))],
        out_specs=[pl.BlockSpec((tb, 4 * DR), lambda i: (rev(i), 0)), pl.BlockSpec((1, DR), lambda i: (0, 0))],
        out_shape=[jax.ShapeDtypeStruct((S, 4 * DR), BF16), jax.ShapeDtypeStruct((1, DR), F32)],
        scratch_shapes=[pltpu.VMEM((H, CHUNK, HEAD_DIM), F32)],
        name="ret_bwd", compiler_params=_params(("arbitrary",)))(proj, proj, cosf, sins, dm, zeta, xi, gc, gnw, rprev, dy)


def _layer_step(x, tgt, ln1_w, w_in_st, conv_w, conv_b, wa, ba, wx, bx, lam, gnw, w_out, ln2_w, wg_st, wu_st, wd, fw):
    S, D = x.shape
    DL = conv_b.shape[1]
    DR = gnw.shape[1]
    assert DL == DR and DL % HEAD_DIM == 0 and S % CHUNK == 0
    H = DR // HEAD_DIM
    d_mix = DL + DR
    n_in = w_in_st.shape[2]
    n_ff = wg_st.shape[2]
    tables = _ret_tables(S, H)
    wab, wxb = wa.astype(BF16), wx.astype(BF16)
    TM = 512

    u1 = _rms_fwd("rms1", x, ln1_w, TM)
    proj = _mm_nn_stacked("proj", u1, w_in_st, F32, TM)
    hs, y = _lru_fwd(proj, conv_w, conv_b, wab, ba, wxb, bx, lam, 128, d_mix)
    y, rprev = _ret_fwd(proj, y, tables, gnw, 256)
    h1 = _mm_nn("out_proj", y, w_out, x, F32, TM, 1024, d_mix)
    u2 = _rms_fwd("rms2", h1, ln2_w, TM)
    gt = _mm_nn_stacked("ffn_gate", u2, wg_st, F32, TM)
    up = _mm_nn_stacked("ffn_up", u2, wu_st, F32, TM)
    ff = _swiglu_fwd(gt, up)
    h2 = _mm_nn("ffn_down", ff, wd, h1, F32, TM, 1024, n_ff)
    dh2, dh2b, d_fw, loss = _loss_head(h2, tgt, fw.reshape(1, D), TM)

    dff = _mm_nt("d_ff", dh2b, wd, F32, TM, n_ff)
    g_wd = _mm_tn("g_w_down", ff, dh2b, n_ff, 1024, 512)
    dgt, dup = _swiglu_bwd(dff, gt, up)
    g_wg = _mm_tn("g_w_gate", u2, dgt, 1024, None, 512, stacked_cols=n_ff)
    g_wu = _mm_tn("g_w_up", u2, dup, 1024, None, 512, stacked_cols=n_ff)
    du2 = _mm_nt_stacked("d_u2", [dgt, dup], [wg_st, wu_st], F32, TM, 1024)
    dh1, dh1b, d_ln2 = _rms_bwd("rms2_bwd", h1, ln2_w, du2, dh2, TM, True)
    dy = _mm_nt("d_y", dh1b, w_out, F32, TM, 1024)
    g_wout = _mm_tn("g_w_out", y, dh1b, 1024, 1024, 512)
    dpa, d_cw, d_cb, d_wa, d_ba, d_wx, d_bx, d_lam = _lru_bwd(proj, hs, dy, conv_w, conv_b, wab, ba, wxb, bx, lam, 128)
    dpb, d_gn = _ret_bwd(proj, rprev, dy, tables, gnw, 256)
    dproj = jnp.concatenate([dpa, dpb], axis=1)
    g_win = _mm_tn("g_w_in", u1, dproj, 1024, None, 512, stacked_cols=n_in)
    du1 = _mm_nt_stacked("d_u1", [dproj], [w_in_st], F32, TM, 1024)
    gx, d_ln1 = _rms_bwd("rms1_bwd", x, ln1_w, du1, dh1, TM, False)

    big = dict(w_in=g_win, w_out=g_wout.reshape(4, d_mix // 4, D), w_ffn_gate=g_wg, w_ffn_up=g_wu,
               w_ffn_down=g_wd.reshape(4, n_ff, D))
    small = dict(ln1_w=d_ln1, conv_w=d_cw, conv_b=d_cb, gate_a_w=d_wa, gate_a_b=d_ba, gate_x_w=d_wx, gate_x_b=d_bx,
                 lru_lambda=d_lam, ret_gn_w=d_gn, ln2_w=d_ln2, final_norm_w=d_fw)
    return loss, gx, big, small


HBM_SPEC = pl.BlockSpec(memory_space=pl.ANY)


def _place():
    x, y, c = lax.axis_index("x"), lax.axis_index("y"), lax.axis_index("c")
    chips = [(1 - x, y), (x, 1 - y), (1 - x, 1 - y)]
    return x, y, c, chips


def _gather_weights(shards, conv_w):
    nw = len(shards)

    def body(*refs):
        ins, cin = refs[:nw], refs[nw]
        outs, cout = refs[nw + 1:2 * nw + 1], refs[2 * nw + 1]
        send_sems, recv_sems, local_sems = refs[2 * nw + 2:]
        x, y, c, chips = _place()
        me = 2 * x + y
        sibling = (x, y, 1 - c)

        def piece(w, chip, half):
            r2 = shards[w].shape[0] // 2
            return outs[w].at[chip, pl.ds(half * r2, r2), :]

        def remote(k, src, dst, to):
            return pltpu.make_async_remote_copy(src_ref=src, dst_ref=dst, send_sem=send_sems.at[k], recv_sem=recv_sems.at[k],
                                                device_id=to, device_id_type=MESH)

        local = [pltpu.make_async_copy(ins[w], outs[w].at[me], local_sems.at[w]) for w in range(nw)]
        local.append(pltpu.make_async_copy(cin, cout.at[me], local_sems.at[nw]))
        for cp in local:
            cp.start()
        started = []
        for w in range(nw):
            r2 = shards[w].shape[0] // 2
            for j, chip in enumerate(chips):
                started.append(remote(6 * w + j, ins[w].at[pl.ds(c * r2, r2), :], piece(w, me, c), (*chip, c)))
                started[-1].start()
        for j, chip in enumerate(chips):
            started.append(remote(6 * nw + j, cin, cout.at[me], (*chip, c)))
            started[-1].start()
        for w in range(nw):
            for j, (cx, cy) in enumerate(chips):
                got = piece(w, 2 * cx + cy, c)
                remote(6 * w + j, got, got, sibling).wait_recv()
                started.append(remote(6 * w + 3 + j, got, got, sibling))
                started[-1].start()
        for w in range(nw):
            for j, (cx, cy) in enumerate(chips):
                got = piece(w, 2 * cx + cy, 1 - c)
                remote(6 * w + 3 + j, got, got, sibling).wait_recv()
        for j, (cx, cy) in enumerate(chips):
            got = cout.at[2 * cx + cy]
            remote(6 * nw + j, got, got, sibling).wait_recv()
        for cp in started:
            cp.wait_send()
        for cp in local:
            cp.wait()

    out_shape = [jax.ShapeDtypeStruct((4,) + s.shape, s.dtype) for s in shards] + \
                [jax.ShapeDtypeStruct((4,) + conv_w.shape, conv_w.dtype)]
    n_sem = 6 * nw + 3
    return pl.pallas_call(
        body, in_specs=[HBM_SPEC] * (nw + 1), out_specs=[HBM_SPEC] * (nw + 1), out_shape=out_shape,
        scratch_shapes=[pltpu.SemaphoreType.DMA((n_sem,)), pltpu.SemaphoreType.DMA((n_sem,)), pltpu.SemaphoreType.DMA((nw + 1,))],
        name="gather_weights")(*shards, conv_w)


def _pair_exchange(grads):
    nw = len(grads)

    def body(*refs):
        ins, outs = refs[:nw], refs[nw:2 * nw]
        send_sems, recv_sems = refs[2 * nw:]
        x, y, c, _ = _place()
        cps = []
        for w in range(nw):
            r2 = grads[w].shape[1] // 2
            cps.append(pltpu.make_async_remote_copy(
                src_ref=ins[w].at[:, pl.ds((1 - c) * r2, r2), :], dst_ref=outs[w], send_sem=send_sems.at[w],
                recv_sem=recv_sems.at[w], device_id=(x, y, 1 - c), device_id_type=MESH))
            cps[-1].start()
        for cp in cps:
            cp.wait()

    out_shape = [jax.ShapeDtypeStruct((g.shape[0], g.shape[1] // 2, g.shape[2]), g.dtype) for g in grads]
    return pl.pallas_call(
        body, in_specs=[HBM_SPEC] * nw, out_specs=[HBM_SPEC] * nw, out_shape=out_shape,
        scratch_shapes=[pltpu.SemaphoreType.DMA((nw,)), pltpu.SemaphoreType.DMA((nw,))], name="rs_pair_exchange")(*grads)


def _chip_exchange(parts):
    nw = len(parts)

    def body(*refs):
        ins, outs = refs[:nw], refs[nw:2 * nw]
        send_sems, recv_sems = refs[2 * nw:]
        x, y, c, chips = _place()
        cps = []
        for w in range(nw):
            for j, (cx, cy) in enumerate(chips):
                cps.append(pltpu.make_async_remote_copy(
                    src_ref=ins[w].at[2 * cx + cy], dst_ref=outs[w].at[j], send_sem=send_sems.at[3 * w + j],
                    recv_sem=recv_sems.at[3 * w + j], device_id=(cx, cy, c), device_id_type=MESH))
                cps[-1].start()
        for cp in cps:
            cp.wait()

    out_shape = [jax.ShapeDtypeStruct((3,) + p.shape[1:], p.dtype) for p in parts]
    return pl.pallas_call(
        body, in_specs=[HBM_SPEC] * nw, out_specs=[HBM_SPEC] * nw, out_shape=out_shape,
        scratch_shapes=[pltpu.SemaphoreType.DMA((3 * nw,)), pltpu.SemaphoreType.DMA((3 * nw,))], name="rs_chip_exchange")(*parts)


def _pair_share(halves):
    nw = len(halves)

    def body(*refs):
        ins, outs = refs[:nw], refs[nw:2 * nw]
        send_sems, recv_sems, local_sems = refs[2 * nw:]
        x, y, c, _ = _place()
        local, cps = [], []
        for w in range(nw):
            local.append(pltpu.make_async_copy(ins[w], outs[w].at[c], local_sems.at[w]))
            local[-1].start()
            cps.append(pltpu.make_async_remote_copy(
                src_ref=ins[w], dst_ref=outs[w].at[c], send_sem=send_sems.at[w], recv_sem=recv_sems.at[w],
                device_id=(x, y, 1 - c), device_id_type=MESH))
            cps[-1].start()
        for w in range(nw):
            cps[w].wait_send()
            pltpu.make_async_remote_copy(
                src_ref=ins[w], dst_ref=outs[w].at[1 - c], send_sem=send_sems.at[w], recv_sem=recv_sems.at[w],
                device_id=(x, y, 1 - c), device_id_type=MESH).wait_recv()
            local[w].wait()

    out_shape = [jax.ShapeDtypeStruct((2,) + h.shape, h.dtype) for h in halves]
    return pl.pallas_call(
        body, in_specs=[HBM_SPEC] * nw, out_specs=[HBM_SPEC] * nw, out_shape=out_shape,
        scratch_shapes=[pltpu.SemaphoreType.DMA((nw,)), pltpu.SemaphoreType.DMA((nw,)), pltpu.SemaphoreType.DMA((nw,))],
        name="rs_pair_share")(*halves)


def _gather_small(sm):
    flips = [(fx, fy, fc) for fx in (0, 1) for fy in (0, 1) for fc in (0, 1)][1:]

    def body(in_ref, out_ref, send_sems, recv_sems, local_sem):
        x, y, c, _ = _place()
        me = 4 * x + 2 * y + c
        local = pltpu.make_async_copy(in_ref, out_ref.at[me], local_sem)
        local.start()
        peers = [(1 - x if fx else x, 1 - y if fy else y, 1 - c if fc else c) for fx, fy, fc in flips]
        cps = []
        for k, peer in enumerate(peers):
            cps.append(pltpu.make_async_remote_copy(src_ref=in_ref, dst_ref=out_ref.at[me], send_sem=send_sems.at[k],
                                                    recv_sem=recv_sems.at[k], device_id=peer, device_id_type=MESH))
            cps[-1].start()
        for k, (px, py, pc) in enumerate(peers):
            cps[k].wait_send()
            pltpu.make_async_remote_copy(src_ref=in_ref, dst_ref=out_ref.at[4 * px + 2 * py + pc], send_sem=send_sems.at[k],
                                         recv_sem=recv_sems.at[k], device_id=(px, py, pc), device_id_type=MESH).wait_recv()
        local.wait()

    return pl.pallas_call(
        body, in_specs=[HBM_SPEC], out_specs=HBM_SPEC, out_shape=jax.ShapeDtypeStruct((8,) + sm.shape, sm.dtype),
        scratch_shapes=[pltpu.SemaphoreType.DMA((7,)), pltpu.SemaphoreType.DMA((7,)), pltpu.SemaphoreType.DMA],
        name="gather_small")(sm)


def _adamw(w, g, m, v):
    m = ADAM_B1 * m + (1.0 - ADAM_B1) * g
    v = ADAM_B2 * v + (1.0 - ADAM_B2) * (g * g)
    m_hat = m / (1.0 - ADAM_B1 ** ADAM_STEP)
    v_hat = v / (1.0 - ADAM_B2 ** ADAM_STEP)
    delta = -ADAM_LR * (m_hat / (jnp.sqrt(v_hat) + ADAM_EPS) + ADAM_WD * w)
    return delta, m, v


def _adamw_call(name, w, g, m, v):
    R, C = w.shape
    tr = _row_tile(R, C, 1024 * 1024)
    row = pl.BlockSpec((tr, C), lambda i: (i, 0))
    o = jax.ShapeDtypeStruct((R, C), F32)
    return _ew(name, _adamw, [(w, row), (g, row), (m, row), (v, row)], [(o, row), (o, row), (o, row)], (R // tr,))


def _reduce_big(grads, place):
    names = list(grads)
    gl = [grads[n] for n in names]
    got_pair = _pair_exchange(gl)
    parts, halves = [], []
    for n, g, ra in zip(names, gl, got_pair):
        _, R, C = g.shape
        r2 = R // 2
        tr = _row_tile(r2, C)
        nb = r2 // tr
        own = pl.BlockSpec((None, tr, C), lambda j, i, p: (j, p[0] * nb + i, 0))
        blk = pl.BlockSpec((None, tr, C), lambda j, i, p: (j, i, 0))
        parts.append(_ew("rs_pair_sum_" + n, lambda a, b: (a + b,), [(g, own), (ra, blk)],
                         [(jax.ShapeDtypeStruct((4, r2, C), BF16), blk)], (4, nb), sp=place)[0])
    got_chip = _chip_exchange(parts)
    for n, g, ra, rb in zip(names, gl, got_pair, got_chip):
        _, R, C = g.shape
        r2 = R // 2
        tr = _row_tile(r2, C)
        nb = r2 // tr
        own = pl.BlockSpec((None, tr, C), lambda i, p: (p[1], p[0] * nb + i, 0))
        mine = pl.BlockSpec((None, tr, C), lambda i, p: (p[1], i, 0))
        src = [pl.BlockSpec((None, tr, C), functools.partial(lambda i, p, j: (j, i, 0), j=j)) for j in range(3)]
        out = pl.BlockSpec((tr, C), lambda i, p: (i, 0))

        def total(a, b, r0, r1, r2_):
            return ((((a + b) + r0.astype(F32)) + r1.astype(F32)) + r2_.astype(F32),)

        halves.append(_ew("rs_chip_sum_" + n, total, [(g, own), (ra, mine), (rb, src[0]), (rb, src[1]), (rb, src[2])],
                          [(jax.ShapeDtypeStruct((r2, C), F32), out)], (nb,), sp=place)[0])
    full = _pair_share(halves)
    return {n: f.reshape(f.shape[0] * f.shape[1], f.shape[2]) for n, f in zip(names, full)}


def _pack(arrays):
    rows, offs, pos = [], [], 0
    for a in arrays:
        flat = a.reshape(-1)
        n = -(-flat.shape[0] // (8 * LANES)) * (8 * LANES)
        if n != flat.shape[0]:
            flat = jnp.pad(flat, (0, n - flat.shape[0]))
        rows.append(flat.reshape(-1, LANES))
        offs.append(pos)
        pos += n // LANES
    return jnp.concatenate(rows, axis=0), offs


def _unpack(packed, offs, shapes):
    out = []
    for off, shp in zip(offs, shapes):
        n = 1
        for s in shp:
            n *= s
        out.append(packed[off:off + -(-n // LANES)].reshape(-1)[:n].reshape(shp))
    return out


def _sum8(gathered):
    _, R, C = gathered.shape
    tr = _row_tile(R, C, 256 * 1024)
    specs = [pl.BlockSpec((None, tr, C), functools.partial(lambda i, d: (d, i, 0), d=d)) for d in range(8)]

    def fn(*parts):
        t = parts[0]
        for p in parts[1:]:
            t = t + p
        return (t,)

    return _ew("small_sum", fn, [(gathered, s) for s in specs],
               [(jax.ShapeDtypeStruct((R, C), F32), pl.BlockSpec((tr, C), lambda i: (i, 0)))], (R // tr,))[0]


BIG = ("w_in", "w_out", "w_ffn_gate", "w_ffn_up", "w_ffn_down")
SMALL = ("ln1_w", "conv_w", "conv_b", "gate_a_w", "gate_a_b", "gate_x_w", "gate_x_b", "lru_lambda", "ret_gn_w", "ln2_w",
         "final_norm_w")
WEIGHTS = ("ln1_w", "w_in", "conv_w", "conv_b", "gate_a_w", "gate_a_b", "gate_x_w", "gate_x_b", "lru_lambda", "ret_gn_w",
           "w_out", "ln2_w", "w_ffn_gate", "w_ffn_up", "w_ffn_down", "final_norm_w")


def kernel(x, ln1_w, w_in, conv_w, conv_b, gate_a_w, gate_a_b, gate_x_w, gate_x_b, lru_lambda, ret_gn_w, w_out, ln2_w, w_ffn_gate, w_ffn_up, w_ffn_down, final_norm_w, loss_target, m_ln1_w, m_w_in, m_conv_w, m_conv_b, m_gate_a_w, m_gate_a_b, m_gate_x_w, m_gate_x_b, m_lru_lambda, m_ret_gn_w, m_w_out, m_ln2_w, m_w_ffn_gate, m_w_ffn_up, m_w_ffn_down, m_final_norm_w, v_ln1_w, v_w_in, v_conv_w, v_conv_b, v_gate_a_w, v_gate_a_b, v_gate_x_w, v_gate_x_b, v_lru_lambda, v_ret_gn_w, v_w_out, v_ln2_w, v_w_ffn_gate, v_w_ffn_up, v_w_ffn_down, v_final_norm_w):
    w = dict(ln1_w=ln1_w, w_in=w_in, conv_w=conv_w, conv_b=conv_b, gate_a_w=gate_a_w, gate_a_b=gate_a_b, gate_x_w=gate_x_w,
             gate_x_b=gate_x_b, lru_lambda=lru_lambda, ret_gn_w=ret_gn_w, w_out=w_out, ln2_w=ln2_w, w_ffn_gate=w_ffn_gate,
             w_ffn_up=w_ffn_up, w_ffn_down=w_ffn_down, final_norm_w=final_norm_w)
    m = dict(ln1_w=m_ln1_w, w_in=m_w_in, conv_w=m_conv_w, conv_b=m_conv_b, gate_a_w=m_gate_a_w, gate_a_b=m_gate_a_b,
             gate_x_w=m_gate_x_w, gate_x_b=m_gate_x_b, lru_lambda=m_lru_lambda, ret_gn_w=m_ret_gn_w, w_out=m_w_out,
             ln2_w=m_ln2_w, w_ffn_gate=m_w_ffn_gate, w_ffn_up=m_w_ffn_up, w_ffn_down=m_w_ffn_down,
             final_norm_w=m_final_norm_w)
    v = dict(ln1_w=v_ln1_w, w_in=v_w_in, conv_w=v_conv_w, conv_b=v_conv_b, gate_a_w=v_gate_a_w, gate_a_b=v_gate_a_b,
             gate_x_w=v_gate_x_w, gate_x_b=v_gate_x_b, lru_lambda=v_lru_lambda, ret_gn_w=v_ret_gn_w, w_out=v_w_out,
             ln2_w=v_ln2_w, w_ffn_gate=v_w_ffn_gate, w_ffn_up=v_w_ffn_up, w_ffn_down=v_w_ffn_down,
             final_norm_w=v_final_norm_w)
    S, D = x.shape[1], x.shape[2]
    chip = 2 * lax.axis_index("x") + lax.axis_index("y")
    place = jnp.stack([lax.axis_index("c"), chip]).astype(jnp.int32)

    shards = [w[n][0].astype(BF16) for n in BIG]
    *stk, conv_st = _gather_weights(shards, w["conv_w"][0])
    w_in_st, w_out_st, wg_st, wu_st, wd_st = stk
    cw_cols = conv_st.shape[2]
    conv_full = jnp.transpose(conv_st, (1, 0, 2)).reshape(CONV_W, 4 * cw_cols)
    d_mix = w_out_st.shape[0] * w_out_st.shape[1]
    n_ffs = wd_st.shape[1]

    loss, gx, big, small = _layer_step(
        x[0], loss_target[0], ln1_w, w_in_st, conv_full, conv_b, gate_a_w[0], gate_a_b, gate_x_w[0], gate_x_b, lru_lambda,
        ret_gn_w, w_out_st.reshape(d_mix, D), ln2_w, wg_st, wu_st, wd_st.reshape(4 * n_ffs, D), final_norm_w)

    g_big = _reduce_big(big, place)
    packed, offs = _pack([small[n] for n in SMALL] + [loss])
    total = _sum8(_gather_small(packed))
    red = _unpack(total, offs, [small[n].shape for n in SMALL] + [(1, LANES)])
    g = dict(zip(SMALL, red[:-1]))
    loss_out = red[-1][0, 0]
    g["conv_w"] = lax.dynamic_slice(g["conv_w"], (0, chip * cw_cols), (CONV_W, cw_cols))

    grad, delta, new_m, new_v = {}, {}, {}, {}
    for n in BIG:
        shp = w[n].shape
        w2, m2, v2 = (t[n].reshape(shp[1], shp[2]) for t in (w, m, v))
        d_, m_, v_ = _adamw_call("adamw_" + n, w2, g_big[n], m2, v2)
        grad[n], delta[n], new_m[n], new_v[n] = (t.reshape(shp) for t in (g_big[n], d_, m_, v_))
    packs = [_pack([t[n] for n in SMALL])[0] for t in (w, m, v)]
    gp, offs2 = _pack([g[n] for n in SMALL])
    outs = _adamw_call("adamw_small", packs[0], gp, packs[1], packs[2])
    shapes = [w[n].shape for n in SMALL]
    for dst, arr in zip((delta, new_m, new_v), outs):
        dst.update(zip(SMALL, _unpack(arr, offs2, shapes)))
    for n in SMALL:
        grad[n] = g[n].reshape(w[n].shape)

    return (loss_out, gx.reshape(x.shape), *[grad[n] for n in WEIGHTS], *[delta[n] for n in WEIGHTS],
            *[new_m[n] for n in WEIGHTS], *[new_v[n] for n in WEIGHTS])
```

```python
import functools

import jax
import jax.numpy as jnp
from jax import lax
from jax.experimental import pallas as pl
from jax.experimental.pallas import tpu as pltpu

F32 = jnp.float32
BF16 = jnp.bfloat16
MESH = pl.DeviceIdType.MESH

EPS = 1e-6
LRU_C = 8.0
ROPE_BASE = 10000.0
CHUNK = 128
HEAD_DIM = 128
CONV_W = 4
ADAM_LR = 0.001
ADAM_B1 = 0.9
ADAM_B2 = 0.999
ADAM_EPS = 1e-08
ADAM_WD = 0.01
ADAM_STEP = 10

V7X_VMEM_BYTES = 64 * 1024 * 1024
VMEM_LIMIT = V7X_VMEM_BYTES - 8 * 1024 * 1024
LANES = 128
SUBLANES_16BIT = 16

NN = (((1,), (0,)), ((), ()))
NT = (((1,), (1,)), ((), ()))
TN = (((0,), (0,)), ((), ()))


def _dot(a, b, dims=NN):
    return lax.dot_general(a, b, dims, preferred_element_type=F32)


def _tile(n, pref, mult=SUBLANES_16BIT):
    best = None
    t = mult
    while t <= min(n, pref):
        if n % t == 0:
            best = t
        t += mult
    return best if best is not None else n


def _row_tile(rows, cols, budget_bytes=2 * 1024 * 1024):
    return _tile(rows, max(SUBLANES_16BIT, budget_bytes // (cols * 4)))


def _params(sem):
    return pltpu.CompilerParams(dimension_semantics=sem, vmem_limit_bytes=VMEM_LIMIT)


def _ew(name, fn, ins, outs, grid, sp=None):
    n_in = len(ins)

    def body(*refs):
        if sp is not None:
            refs = refs[1:]
        vals = [r[...] for r in refs[:n_in]]
        res = fn(*vals)
        for o_ref, v in zip(refs[n_in:], res):
            o_ref[...] = v.astype(o_ref.dtype)

    in_specs = [s for _, s in ins]
    out_specs = [s for _, s in outs]
    out_shape = [s for s, _ in outs]
    sem = ("arbitrary",) * len(grid)
    if sp is None:
        return pl.pallas_call(body, grid=grid, in_specs=in_specs, out_specs=out_specs, out_shape=out_shape,
                              name=name, compiler_params=_params(sem))(*[a for a, _ in ins])
    gs = pltpu.PrefetchScalarGridSpec(num_scalar_prefetch=1, grid=grid, in_specs=in_specs, out_specs=out_specs)
    return pl.pallas_call(body, grid_spec=gs, out_shape=out_shape, name=name,
                          compiler_params=_params(sem))(sp, *[a for a, _ in ins])


def _matmul(name, pairs, dims, grid, out_shape, out_spec, acc_shape, res=None):
    n = len(pairs)
    nk = grid[2]

    def body(*refs):
        ab = refs[:2 * n]
        pos = 2 * n
        res_ref = None
        if res is not None:
            res_ref = refs[pos]
            pos += 1
        o_ref = refs[pos]
        acc_ref = refs[pos + 1] if nk > 1 else None

        def partial():
            t = None
            for p in range(n):
                d = _dot(ab[2 * p][...], ab[2 * p + 1][...], dims)
                t = d if t is None else t + d
            return t

        def finish(t):
            if res_ref is not None:
                t = t + res_ref[...]
            o_ref[...] = t.astype(o_ref.dtype)

        if nk == 1:
            finish(partial())
        else:
            k = pl.program_id(2)

            @pl.when(k == 0)
            def _():
                acc_ref[...] = partial()

            @pl.when(k > 0)
            def _():
                acc_ref[...] += partial()

            @pl.when(k == nk - 1)
            def _():
                finish(acc_ref[...])

    operands, in_specs = [], []
    for a, a_spec, b, b_spec in pairs:
        operands += [a, b]
        in_specs += [a_spec, b_spec]
    if res is not None:
        operands.append(res[0])
        in_specs.append(res[1])
    scratch = [pltpu.VMEM(acc_shape, F32)] if nk > 1 else []
    return pl.pallas_call(body, grid=grid, in_specs=in_specs, out_specs=out_spec, out_shape=out_shape,
                          scratch_shapes=scratch, name=name,
                          compiler_params=_params(("arbitrary", "arbitrary", "arbitrary")))(*operands)


def _mm_nn_stacked(name, a, b_st, out_dtype, tm):
    M, K = a.shape
    J, _, Nj = b_st.shape
    tm = _tile(M, tm)
    return _matmul(
        name, [(a, pl.BlockSpec((tm, K), lambda j, i, k: (i, 0)), b_st, pl.BlockSpec((None, K, Nj), lambda j, i, k: (j, 0, 0)))],
        NN, (J, M // tm, 1), jax.ShapeDtypeStruct((M, J * Nj), out_dtype), pl.BlockSpec((tm, Nj), lambda j, i, k: (i, j)), None)


def _mm_nn(name, a, b, res, out_dtype, tm, tn, tk):
    M, K = a.shape
    N = b.shape[1]
    tm, tn, tk = _tile(M, tm), _tile(N, tn, LANES), _tile(K, tk, LANES)
    return _matmul(
        name, [(a, pl.BlockSpec((tm, tk), lambda j, i, k: (i, k)), b, pl.BlockSpec((tk, tn), lambda j, i, k: (k, j)))],
        NN, (N // tn, M // tm, K // tk), jax.ShapeDtypeStruct((M, N), out_dtype), pl.BlockSpec((tm, tn), lambda j, i, k: (i, j)),
        (tm, tn), res=(res, pl.BlockSpec((tm, tn), lambda j, i, k: (i, j))))


def _mm_nt(name, a, b, out_dtype, tm, tn):
    M, K = a.shape
    N = b.shape[0]
    tm, tn = _tile(M, tm), _tile(N, tn, LANES)
    return _matmul(
        name, [(a, pl.BlockSpec((tm, K), lambda j, i, k: (i, 0)), b, pl.BlockSpec((tn, K), lambda j, i, k: (j, 0)))],
        NT, (N // tn, M // tm, 1), jax.ShapeDtypeStruct((M, N), out_dtype), pl.BlockSpec((tm, tn), lambda j, i, k: (i, j)), None)


def _mm_nt_stacked(name, a_list, b_list, out_dtype, tm, tn):
    M = a_list[0].shape[0]
    J, N, Nj = b_list[0].shape
    tm, tn = _tile(M, tm), _tile(N, tn, LANES)
    pairs = [(a, pl.BlockSpec((tm, Nj), lambda j, i, k: (i, k)), b, pl.BlockSpec((None, tn, Nj), lambda j, i, k: (k, j, 0)))
             for a, b in zip(a_list, b_list)]
    return _matmul(name, pairs, NT, (N // tn, M // tm, J), jax.ShapeDtypeStruct((M, N), out_dtype),
                   pl.BlockSpec((tm, tn), lambda j, i, k: (i, j)), (tm, tn))


def _mm_tn(name, a, b, tmo, tn, tk, stacked_cols=None):
    S, Mo = a.shape
    N = b.shape[1]
    tmo, tk = _tile(Mo, tmo, LANES), _tile(S, tk)
    if stacked_cols is None:
        tn = _tile(N, tn, LANES)
        out_shape = jax.ShapeDtypeStruct((Mo, N), F32)
        out_spec = pl.BlockSpec((tmo, tn), lambda i, j, k: (i, j))
    else:
        tn = stacked_cols
        out_shape = jax.ShapeDtypeStruct((N // tn, Mo, tn), F32)
        out_spec = pl.BlockSpec((None, tmo, tn), lambda i, j, k: (j, i, 0))
    return _matmul(
        name, [(a, pl.BlockSpec((tk, tmo), lambda i, j, k: (k, i)), b, pl.BlockSpec((tk, tn), lambda i, j, k: (k, j)))],
        TN, (Mo // tmo, N // tn, S // tk), out_shape, out_spec, (tmo, tn))


def _rms_fwd(name, x, w, tm):
    S, D = x.shape
    tm = _tile(S, tm)

    def fn(xv, wv):
        r = lax.rsqrt(jnp.mean(xv * xv, axis=-1, keepdims=True) + EPS)
        return ((xv * r) * wv,)

    row = pl.BlockSpec((tm, D), lambda i: (i, 0))
    return _ew(name, fn, [(x, row), (w, pl.BlockSpec((1, D), lambda i: (0, 0)))],
               [(jax.ShapeDtypeStruct((S, D), BF16), row)], (S // tm,))[0]


def _rms_bwd(name, x, w, dy, dres, tm, want_bf16):
    S, D = x.shape
    tm = _tile(S, tm)

    def body(x_ref, w_ref, dy_ref, dres_ref, dx_ref, *rest):
        dw_ref = rest[-1]
        i = pl.program_id(0)

        @pl.when(i == 0)
        def _():
            dw_ref[...] = jnp.zeros_like(dw_ref)

        xv = x_ref[...]
        r = lax.rsqrt(jnp.mean(xv * xv, axis=-1, keepdims=True) + EPS)
        nv = xv * r
        dyv = dy_ref[...]
        dn = dyv * w_ref[...]
        dw_ref[...] += jnp.sum(dyv * nv, axis=0, keepdims=True)
        dx = dres_ref[...] + r * (dn - nv * jnp.mean(dn * nv, axis=-1, keepdims=True))
        dx_ref[...] = dx
        if want_bf16:
            rest[0][...] = dx.astype(BF16)

    row = pl.BlockSpec((tm, D), lambda i: (i, 0))
    vec = pl.BlockSpec((1, D), lambda i: (0, 0))
    out_shape = [jax.ShapeDtypeStruct((S, D), F32)] + ([jax.ShapeDtypeStruct((S, D), BF16)] if want_bf16 else []) + \
                [jax.ShapeDtypeStruct((1, D), F32)]
    out_specs = [row] + ([row] if want_bf16 else []) + [vec]
    return pl.pallas_call(body, grid=(S // tm,), in_specs=[row, vec, row, row], out_specs=out_specs, out_shape=out_shape,
                          name=name, compiler_params=_params(("arbitrary",)))(x, w, dy, dres)


def _loss_head(h2, tgt, fw, tm):
    S, D = h2.shape
    tm = _tile(S, tm)

    def body(h_ref, t_ref, w_ref, dh_ref, dhb_ref, dw_ref, loss_ref):
        i = pl.program_id(0)

        @pl.when(i == 0)
        def _():
            dw_ref[...] = jnp.zeros_like(dw_ref)
            loss_ref[...] = jnp.zeros_like(loss_ref)

        hv = h_ref[...]
        wv = w_ref[...]
        r = lax.rsqrt(jnp.mean(hv * hv, axis=-1, keepdims=True) + EPS)
        nv = hv * r
        err = nv * wv - t_ref[...]
        row_loss = jnp.mean(err * err, axis=-1, keepdims=True)
        loss_ref[...] += 0.5 * jnp.sum(row_loss, axis=0, keepdims=True)
        dyo = err * (1.0 / D)
        dn = dyo * wv
        dw_ref[...] += jnp.sum(dyo * nv, axis=0, keepdims=True)
        dh = r * (dn - nv * jnp.mean(dn * nv, axis=-1, keepdims=True))
        dh_ref[...] = dh
        dhb_ref[...] = dh.astype(BF16)

    row = pl.BlockSpec((tm, D), lambda i: (i, 0))
    vec = pl.BlockSpec((1, D), lambda i: (0, 0))
    return pl.pallas_call(
        body, grid=(S // tm,), in_specs=[row, row, vec],
        out_specs=[row, row, vec, pl.BlockSpec((1, LANES), lambda i: (0, 0))],
        out_shape=[jax.ShapeDtypeStruct((S, D), F32), jax.ShapeDtypeStruct((S, D), BF16),
                   jax.ShapeDtypeStruct((1, D), F32), jax.ShapeDtypeStruct((1, LANES), F32)],
        name="loss_head", compiler_params=_params(("arbitrary",)))(h2, tgt, fw)


def _swiglu_fwd(gt, up):
    S, F = gt.shape
    tr = _row_tile(S, F)

    def fn(g, u):
        return (g * jax.nn.sigmoid(g) * u,)

    row = pl.BlockSpec((tr, F), lambda i: (i, 0))
    return _ew("swiglu_fwd", fn, [(gt, row), (up, row)], [(jax.ShapeDtypeStruct((S, F), BF16), row)], (S // tr,))[0]


def _swiglu_bwd(dff, gt, up):
    S, F = gt.shape
    tr = _row_tile(S, F)

    def fn(d, g, u):
        sg = jax.nn.sigmoid(g)
        return d * u * (sg * (1.0 + g * (1.0 - sg))), d * (g * sg)

    row = pl.BlockSpec((tr, F), lambda i: (i, 0))
    o = jax.ShapeDtypeStruct((S, F), BF16)
    return _ew("swiglu_bwd", fn, [(dff, row), (gt, row), (up, row)], [(o, row), (o, row)], (S // tr,))


def _shift_down(x, d, head8):
    r = pltpu.roll(x, d, 0)
    rh = pltpu.roll(head8, d, 0)
    row8 = lax.broadcasted_iota(jnp.int32, head8.shape, 0)
    top = jnp.where(row8 < d, rh, r[0:8])
    return jnp.concatenate([top, r[8:]], axis=0)


def _shift_up(x, d, tail8):
    n = x.shape[0]
    r = pltpu.roll(x, n - d, 0)
    rt = pltpu.roll(tail8, 8 - d, 0)
    row8 = lax.broadcasted_iota(jnp.int32, tail8.shape, 0)
    bot = jnp.where(row8 + d >= 8, rt, r[n - 8:n])
    return jnp.concatenate([r[:n - 8], bot], axis=0)


def _log_sigmoid(lam):
    z = jnp.exp(-jnp.abs(lam))
    u = 1.0 + z
    log1p = jnp.where(u == 1.0, z, jnp.log(u) * (z / jnp.where(u == 1.0, 1.0, u - 1.0)))
    return jnp.minimum(lam, 0.0) - log1p


def _neg_expm1(z):
    series = -z * (1.0 + z * (0.5 + z * (1.0 / 6.0 + z * (1.0 / 24.0 + z * (1.0 / 120.0)))))
    return jnp.where(z > -0.05, series, 1.0 - jnp.exp(z))


_GELU_C = 0.7978845608028654


def _gelu(x):
    t = jnp.tanh(_GELU_C * (x + 0.044715 * (x * x * x)))
    return x * (0.5 * (1.0 + t)), t


def _gelu_grad(x, t):
    return 0.5 * (1.0 + t) + 0.5 * x * (1.0 - t * t) * (_GELU_C * (1.0 + 3.0 * 0.044715 * (x * x)))


def _lru_gates(lx, head8, cw, cb, wa_ref, ba, wx_ref, bx, ls):
    nb = wa_ref.shape[0]
    sh = [lx] + [_shift_down(lx, d, head8) for d in (1, 2, 3)]
    cx = cb + sh[3] * cw[0:1]
    cx = cx + sh[2] * cw[1:2]
    cx = cx + sh[1] * cw[2:3]
    cx = cx + sh[0] * cw[3:4]
    cxb = cx.astype(BF16)
    ra = jnp.concatenate([_dot(cxb[:, n * HEAD_DIM:(n + 1) * HEAD_DIM], wa_ref[n]) for n in range(nb)], axis=1) + ba
    ia = jnp.concatenate([_dot(cxb[:, n * HEAD_DIM:(n + 1) * HEAD_DIM], wx_ref[n]) for n in range(nb)], axis=1) + bx
    r = jax.nn.sigmoid(ra)
    ig = jax.nn.sigmoid(ia)
    log_a = LRU_C * r * ls
    a = jnp.exp(log_a)
    mult = jnp.sqrt(_neg_expm1(2.0 * log_a))
    return sh, cx, cxb, r, ig, a, mult


def _lru_specs(tl, DL):
    nb = DL // HEAD_DIM
    vec = pl.BlockSpec((1, DL), lambda i: (0, 0))
    return [pl.BlockSpec((CONV_W, DL), lambda i: (0, 0)), vec,
            pl.BlockSpec((nb, HEAD_DIM, HEAD_DIM), lambda i: (0, 0, 0)), vec,
            pl.BlockSpec((nb, HEAD_DIM, HEAD_DIM), lambda i: (0, 0, 0)), vec, vec]


def _lru_fwd(proj, cw, cb, wa, ba, wx, bx, lam, tl, d_mix):
    S = proj.shape[0]
    DL = cb.shape[1]
    tl = _tile(S, tl)

    def body(lx_ref, lg_ref, cw_ref, cb_ref, wa_ref, ba_ref, wx_ref, bx_ref, lam_ref, h_ref, y_ref, prev8, hc, a_s, b_s):
        i = pl.program_id(0)

        @pl.when(i == 0)
        def _():
            prev8[...] = jnp.zeros_like(prev8)
            hc[...] = jnp.zeros_like(hc)

        lx = lx_ref[...]
        ls = _log_sigmoid(lam_ref[...])
        _, cx, _, _, ig, a, mult = _lru_gates(lx, prev8[...], cw_ref[...], cb_ref[...], wa_ref, ba_ref[...],
                                              wx_ref, bx_ref[...], ls)
        b = mult * (ig * cx)
        row = lax.broadcasted_iota(jnp.int32, a.shape, 0) & 7
        for d in (1, 2, 4):
            a_sh = pltpu.roll(a, d, 0)
            b_sh = pltpu.roll(b, d, 0)
            m = row >= d
            b = jnp.where(m, a * b_sh + b, b)
            a = jnp.where(m, a * a_sh, a)
        a_s[...] = a
        b_s[...] = b

        def step(g, hprev):
            sl = pl.ds(pl.multiple_of(g * 8, 8), 8)
            hh = a_s[sl, :] * hprev + b_s[sl, :]
            h_ref[sl, :] = hh
            return hh[7:8, :]

        hc[0:1, :] = lax.fori_loop(0, tl // 8, step, hc[0:1, :])
        prev8[...] = lx[tl - 8:tl]
        g, _ = _gelu(lg_ref[...])
        y_ref[...] = (h_ref[...] * g).astype(BF16)

    return pl.pallas_call(
        body, grid=(S // tl,),
        in_specs=[pl.BlockSpec((tl, DL), lambda i: (i, 0)), pl.BlockSpec((tl, DL), lambda i: (i, 1))] + _lru_specs(tl, DL),
        out_specs=[pl.BlockSpec((tl, DL), lambda i: (i, 0)), pl.BlockSpec((tl, DL), lambda i: (i, 0))],
        out_shape=[jax.ShapeDtypeStruct((S, DL), F32), jax.ShapeDtypeStruct((S, d_mix), BF16)],
        scratch_shapes=[pltpu.VMEM((8, DL), F32), pltpu.VMEM((8, DL), F32), pltpu.VMEM((tl, DL), F32), pltpu.VMEM((tl, DL), F32)],
        name="lru_fwd", compiler_params=_params(("arbitrary",)))(proj, proj, cw, cb, wa, ba, wx, bx, lam)


def _lru_bwd(proj, h, dy, cw, cb, wa, ba, wx, bx, lam, tl):
    S = proj.shape[0]
    DL = cb.shape[1]
    nb = DL // HEAD_DIM
    tl = _tile(S, tl)
    nt = S // tl
    ng = tl // 8
    t8 = tl // 8

    def body(lx_ref, lxp_ref, lg_ref, h_ref, hp_ref, dy_ref, cw_ref, cb_ref, wa_ref, ba_ref, wx_ref, bx_ref, lam_ref,
             dlxg_ref, dcw_ref, dcb_ref, dwa_ref, dba_ref, dwx_ref, dbx_ref, dlam_ref,
             a_next, g_carry, dcx_next, an_s, dh_s, g_s):
        i = pl.program_id(0)

        @pl.when(i == 0)
        def _():
            for ref in (dcw_ref, dcb_ref, dwa_ref, dba_ref, dwx_ref, dbx_ref, dlam_ref, a_next, g_carry, dcx_next):
                ref[...] = jnp.zeros_like(ref)

        first = i == nt - 1
        lx = lx_ref[...]
        hv = h_ref[...]
        lg = lg_ref[...]
        dyv = dy_ref[...]
        head8 = jnp.where(first, 0.0, lxp_ref[...])
        hhead8 = jnp.where(first, 0.0, hp_ref[...])
        lamv = lam_ref[...]
        ls = _log_sigmoid(lamv)
        cwv = cw_ref[...]
        sh, cx, cxb, r, ig, a, mult = _lru_gates(lx, head8, cwv, cb_ref[...], wa_ref, ba_ref[...], wx_ref, bx_ref[...], ls)
        hprev = _shift_down(hv, 1, hhead8)
        g, t = _gelu(lg)
        dlg = dyv * hv * _gelu_grad(lg, t)
        dh = dyv * g
        an = _shift_up(a, 1, a_next[...])
        row = lax.broadcasted_iota(jnp.int32, a.shape, 0) & 7
        for d in (1, 2, 4):
            an_sh = pltpu.roll(an, tl - d, 0)
            dh_sh = pltpu.roll(dh, tl - d, 0)
            m = row + d < 8
            dh = jnp.where(m, an * dh_sh + dh, dh)
            an = jnp.where(m, an * an_sh, an)
        an_s[...] = an
        dh_s[...] = dh

        def step(k, gc):
            sl = pl.ds(pl.multiple_of((ng - 1 - k) * 8, 8), 8)
            gg = an_s[sl, :] * gc + dh_s[sl, :]
            g_s[sl, :] = gg
            return gg[0:1, :]

        g_carry[0:1, :] = lax.fori_loop(0, ng, step, g_carry[0:1, :])
        a_next[...] = a[0:8]
        G = g_s[...]
        da = G * hprev
        icx = ig * cx
        dmult = G * icx
        dicx = G * mult
        di = dicx * cx
        dcx = dicx * ig
        dlog = da * a - dmult * (a * a) / mult
        dr = dlog * (LRU_C * ls)
        dlam_ref[...] += jnp.sum(dlog * (LRU_C * r), axis=0, keepdims=True)
        dra = dr * r * (1.0 - r)
        dia = di * ig * (1.0 - ig)
        dba_ref[...] += jnp.sum(dra, axis=0, keepdims=True)
        dbx_ref[...] += jnp.sum(dia, axis=0, keepdims=True)
        drab = dra.astype(BF16)
        diab = dia.astype(BF16)
        back = []
        for n in range(nb):
            cs = slice(n * HEAD_DIM, (n + 1) * HEAD_DIM)
            dwa_ref[n] += _dot(cxb[:, cs], drab[:, cs], TN)
            dwx_ref[n] += _dot(cxb[:, cs], diab[:, cs], TN)
            back.append(_dot(drab[:, cs], wa_ref[n], NT) + _dot(diab[:, cs], wx_ref[n], NT))
        dcx = dcx + jnp.concatenate(back, axis=1)
        dcb_ref[...] += jnp.sum(dcx, axis=0, keepdims=True)
        for tap in range(CONV_W):
            dcw_ref[tap:tap + 1, :] += jnp.sum(dcx * sh[CONV_W - 1 - tap], axis=0, keepdims=True)
        tail = dcx_next[...]
        dlx = dcx * cwv[3:4]
        for d in (1, 2, 3):
            dlx = dlx + _shift_up(dcx, d, tail) * cwv[3 - d:4 - d]
        dcx_next[...] = dcx[0:8]
        dlxg_ref[:, 0:DL] = dlx.astype(BF16)
        dlxg_ref[:, DL:2 * DL] = dlg.astype(BF16)

        @pl.when(i == nt - 1)
        def _():
            dlam_ref[...] = dlam_ref[...] * (1.0 - jax.nn.sigmoid(lamv))

    rev = lambda i: nt - 1 - i
    prev8_map = lambda i: (jnp.maximum((nt - 1 - i) * t8 - 1, 0), 0)
    vec = pl.BlockSpec((1, DL), lambda i: (0, 0))
    mat = pl.BlockSpec((nb, HEAD_DIM, HEAD_DIM), lambda i: (0, 0, 0))
    return pl.pallas_call(
        body, grid=(nt,),
        in_specs=[pl.BlockSpec((tl, DL), lambda i: (rev(i), 0)), pl.BlockSpec((8, DL), prev8_map),
                  pl.BlockSpec((tl, DL), lambda i: (rev(i), 1)),
                  pl.BlockSpec((tl, DL), lambda i: (rev(i), 0)), pl.BlockSpec((8, DL), prev8_map),
                  pl.BlockSpec((tl, DL), lambda i: (rev(i), 0))] + _lru_specs(tl, DL),
        out_specs=[pl.BlockSpec((tl, 2 * DL), lambda i: (rev(i), 0)), pl.BlockSpec((CONV_W, DL), lambda i: (0, 0)), vec,
                   mat, vec, mat, vec, vec],
        out_shape=[jax.ShapeDtypeStruct((S, 2 * DL), BF16), jax.ShapeDtypeStruct((CONV_W, DL), F32),
                   jax.ShapeDtypeStruct((1, DL), F32), jax.ShapeDtypeStruct((nb, HEAD_DIM, HEAD_DIM), F32),
                   jax.ShapeDtypeStruct((1, DL), F32), jax.ShapeDtypeStruct((nb, HEAD_DIM, HEAD_DIM), F32),
                   jax.ShapeDtypeStruct((1, DL), F32), jax.ShapeDtypeStruct((1, DL), F32)],
        scratch_shapes=[pltpu.VMEM((8, DL), F32), pltpu.VMEM((8, DL), F32), pltpu.VMEM((8, DL), F32),
                        pltpu.VMEM((tl, DL), F32), pltpu.VMEM((tl, DL), F32), pltpu.VMEM((tl, DL), F32)],
        name="lru_bwd", compiler_params=_params(("arbitrary",)))(proj, proj, proj, h, h, dy, cw, cb, wa, ba, wx, bx, lam)


def _ret_tables(S, H):
    pos = jnp.arange(S, dtype=F32)
    inv_freq = ROPE_BASE ** (-jnp.arange(0, HEAD_DIM, 2, dtype=F32) / HEAD_DIM)
    ang = pos[:, None] * inv_freq[None, :]
    cos, sin = jnp.cos(ang), jnp.sin(ang)
    cosf = jnp.concatenate([cos, cos], axis=1)
    sins = jnp.concatenate([-sin, sin], axis=1)
    log_gamma = jnp.log1p(-jnp.exp2(-5.0 - jnp.arange(H, dtype=F32)))
    idx = jnp.arange(CHUNK)
    diff = idx[:, None] - idx[None, :]
    causal = diff >= 0
    decay = jnp.where(causal[None], jnp.exp(log_gamma[:, None, None] * jnp.where(causal, diff, 0)[None].astype(F32)), 0.0)
    zeta = jnp.exp(log_gamma[:, None] * (CHUNK - 1 - idx).astype(F32)[None, :])
    xi = jnp.exp(log_gamma[:, None] * (idx + 1).astype(F32)[None, :])
    gc = jnp.exp(log_gamma * CHUNK)
    lanes = (H, CHUNK, HEAD_DIM)
    return (cosf, sins, decay, jnp.broadcast_to(zeta[:, :, None], lanes), jnp.broadcast_to(xi[:, :, None], lanes),
            jnp.broadcast_to(gc[:, None, None], lanes))


def _rope(t, cos, sin_signed):
    return t * cos + pltpu.roll(t, HEAD_DIM // 2, 1) * sin_signed


def _rope_t(d, cos, sin_signed):
    return d * cos + pltpu.roll(d * sin_signed, HEAD_DIM // 2, 1)


def _ret_const_specs(H, DR):
    full = pl.BlockSpec((H, CHUNK, HEAD_DIM), lambda i: (0, 0, 0))
    return [full, full, full, full, pl.BlockSpec((1, DR), lambda i: (0, 0))]


def _ret_fwd(proj, y, tables, gnw, tb):
    S = proj.shape[0]
    DR = gnw.shape[1]
    H = DR // HEAD_DIM
    tb = _tile(S, tb, CHUNK)
    nc = tb // CHUNK
    cosf, sins, dm, zeta, xi, gc = tables
    scale = HEAD_DIM ** -0.5

    def body(qk_ref, vg_ref, cos_ref, sin_ref, dm_ref, zeta_ref, xi_ref, gc_ref, gnw_ref, y_in, y_ref, rprev_ref, r_s):
        del y_in
        i = pl.program_id(0)

        @pl.when(i == 0)
        def _():
            r_s[...] = jnp.zeros_like(r_s)

        def chunk(c, carry):
            rows = pl.ds(pl.multiple_of(c * CHUNK, CHUNK), CHUNK)
            cos = cos_ref[rows, :]
            sin = sin_ref[rows, :]
            for h in range(H):
                c0 = slice(h * HEAD_DIM, (h + 1) * HEAD_DIM)
                c1 = slice(DR + h * HEAD_DIM, DR + (h + 1) * HEAD_DIM)
                qh = _rope(qk_ref[rows, c0], cos, sin)
                kh = _rope(qk_ref[rows, c1], cos, sin) * scale
                vb = vg_ref[rows, c0].astype(BF16)
                gate = vg_ref[rows, c1]
                s = _dot(qh.astype(BF16), kh.astype(BF16), NT) * dm_ref[h]
                rp = r_s[h]
                rpb = rp.astype(BF16)
                rprev_ref[c, h] = rpb
                o = _dot(s.astype(BF16), vb) + _dot((qh * xi_ref[h]).astype(BF16), rpb)
                r_s[h] = rp * gc_ref[h] + _dot((kh * zeta_ref[h]).astype(BF16), vb, TN)
                mu = jnp.mean(o, axis=-1, keepdims=True)
                oc = o - mu
                var = jnp.mean(oc * oc, axis=-1, keepdims=True)
                on = oc * lax.rsqrt(var + EPS) * gnw_ref[:, c0]
                y_ref[rows, c0] = (gate * jax.nn.sigmoid(gate) * on).astype(BF16)
            return carry

        lax.fori_loop(0, nc, chunk, 0)

    return pl.pallas_call(
        body, grid=(S // tb,),
        in_specs=[pl.BlockSpec((tb, 2 * DR), lambda i: (i, 1)), pl.BlockSpec((tb, 2 * DR), lambda i: (i, 2)),
                  pl.BlockSpec((tb, HEAD_DIM), lambda i: (i, 0)), pl.BlockSpec((tb, HEAD_DIM), lambda i: (i, 0))]
        + _ret_const_specs(H, DR) + [pl.BlockSpec(memory_space=pl.ANY)],
        out_specs=[pl.BlockSpec((tb, DR), lambda i: (i, 1)),
                   pl.BlockSpec((nc, H, CHUNK, HEAD_DIM), lambda i: (i, 0, 0, 0))],
        out_shape=[jax.ShapeDtypeStruct(y.shape, BF16), jax.ShapeDtypeStruct((S // CHUNK, H, CHUNK, HEAD_DIM), BF16)],
        scratch_shapes=[pltpu.VMEM((H, CHUNK, HEAD_DIM), F32)],
        input_output_aliases={9: 0},
        name="ret_fwd", compiler_params=_params(("arbitrary",)))(proj, proj, cosf, sins, dm, zeta, xi, gc, gnw, y)


def _ret_bwd(proj, rprev, dy, tables, gnw, tb):
    S = proj.shape[0]
    DR = gnw.shape[1]
    H = DR // HEAD_DIM
    tb = _tile(S, tb, CHUNK)
    nc = tb // CHUNK
    nt = S // tb
    cosf, sins, dm, zeta, xi, gc = tables
    scale = HEAD_DIM ** -0.5

    def body(qk_ref, vg_ref, cos_ref, sin_ref, dm_ref, zeta_ref, xi_ref, gc_ref, gnw_ref, rprev_ref, dy_ref,
             dp_ref, dgn_ref, dr_s):
        i = pl.program_id(0)

        @pl.when(i == 0)
        def _():
            dr_s[...] = jnp.zeros_like(dr_s)
            dgn_ref[...] = jnp.zeros_like(dgn_ref)

        def chunk(cc, carry):
            c = nc - 1 - cc
            rows = pl.ds(pl.multiple_of(c * CHUNK, CHUNK), CHUNK)
            cos = cos_ref[rows, :]
            sin = sin_ref[rows, :]
            for h in range(H):
                c0 = slice(h * HEAD_DIM, (h + 1) * HEAD_DIM)
                c1 = slice(DR + h * HEAD_DIM, DR + (h + 1) * HEAD_DIM)
                c2 = slice(2 * DR + h * HEAD_DIM, 2 * DR + (h + 1) * HEAD_DIM)
                c3 = slice(3 * DR + h * HEAD_DIM, 3 * DR + (h + 1) * HEAD_DIM)
                qh = _rope(qk_ref[rows, c0], cos, sin)
                kh = _rope(qk_ref[rows, c1], cos, sin) * scale
                qb = qh.astype(BF16)
                kb = kh.astype(BF16)
                vb = vg_ref[rows, c0].astype(BF16)
                gate = vg_ref[rows, c1]
                dmh = dm_ref[h]
                xih = xi_ref[h]
                zetah = zeta_ref[h]
                sb = (_dot(qb, kb, NT) * dmh).astype(BF16)
                rpb = rprev_ref[c, h]
                qx = (qh * xih).astype(BF16)
                kz = (kh * zetah).astype(BF16)
                o = _dot(sb, vb) + _dot(qx, rpb)
                mu = jnp.mean(o, axis=-1, keepdims=True)
                oc = o - mu
                rstd = lax.rsqrt(jnp.mean(oc * oc, axis=-1, keepdims=True) + EPS)
                ohat = oc * rstd
                gw = gnw_ref[:, c0]
                sg = jax.nn.sigmoid(gate)
                dyv = dy_ref[rows, c0]
                dgate = dyv * (ohat * gw) * (sg * (1.0 + gate * (1.0 - sg)))
                don = dyv * (gate * sg)
                dgn_ref[:, c0] += jnp.sum(don * ohat, axis=0, keepdims=True)
                dohat = don * gw
                do = rstd * (dohat - jnp.mean(dohat, axis=-1, keepdims=True)
                             - ohat * jnp.mean(dohat * ohat, axis=-1, keepdims=True))
                dob = do.astype(BF16)
                drh = dr_s[h]
                drb = drh.astype(BF16)
                dv = _dot(sb, dob, TN) + _dot(kz, drb)
                dsb = (_dot(dob, vb, NT) * dmh).astype(BF16)
                dqh = _dot(dsb, kb) + _dot(dob, rpb, NT) * xih
                dkh = _dot(dsb, qb, TN) + _dot(vb, drb, NT) * zetah
                dr_s[h] = drh * gc_ref[h] + _dot(qx, dob, TN)
                dp_ref[rows, c0] = _rope_t(dqh, cos, sin).astype(BF16)
                dp_ref[rows, c1] = _rope_t(dkh * scale, cos, sin).astype(BF16)
                dp_ref[rows, c2] = dv.astype(BF16)
                dp_ref[rows, c3] = dgate.astype(BF16)
            return carry

        lax.fori_loop(0, nc, chunk, 0)

    rev = lambda i: nt - 1 - i
    return pl.pallas_call(
        body, grid=(nt,),
        in_specs=[pl.BlockSpec((tb, 2 * DR), lambda i: (rev(i), 1)), pl.BlockSpec((tb, 2 * DR), lambda i: (rev(i), 2)),
                  pl.BlockSpec((tb, HEAD_DIM), lambda i: (rev(i), 0)), pl.BlockSpec((tb, HEAD_DIM), lambda i: (rev(i), 0))]
        + _ret_const_specs(H, DR)
        + [pl.BlockSpec((nc, H, CHUNK, HEAD_DIM), lambda i: (rev(i), 0, 0, 0)), pl.BlockSpec((tb, DR), lambda i: (rev(i), 1))],
        out_specs=[pl.BlockSpec((tb, 4 * DR), lambda i: (rev(i), 0)), pl.BlockSpec((1, DR), lambda i: (0, 0))],
        out_shape=[jax.ShapeDtypeStruct((S, 4 * DR), BF16), jax.ShapeDtypeStruct((1, DR), F32)],
        scratch_shapes=[pltpu.VMEM((H, CHUNK, HEAD_DIM), F32)],
        name="ret_bwd", compiler_params=_params(("arbitrary",)))(proj, proj, cosf, sins, dm, zeta, xi, gc, gnw, rprev, dy)


def _layer_step(x, tgt, ln1_w, w_in_st, conv_w, conv_b, wa, ba, wx, bx, lam, gnw, w_out, ln2_w, wg_st, wu_st, wd, fw):
    S, D = x.shape
    DL = conv_b.shape[1]
    DR = gnw.shape[1]
    assert DL == DR and DL % HEAD_DIM == 0 and S % CHUNK == 0
    H = DR // HEAD_DIM
    d_mix = DL + DR
    n_in = w_in_st.shape[2]
    n_ff = wg_st.shape[2]
    tables = _ret_tables(S, H)
    wab, wxb = wa.astype(BF16), wx.astype(BF16)
    TM = 512

    u1 = _rms_fwd("rms1", x, ln1_w, TM)
    proj = _mm_nn_stacked("proj", u1, w_in_st, F32, TM)
    hs, y = _lru_fwd(proj, conv_w, conv_b, wab, ba, wxb, bx, lam, 128, d_mix)
    y, rprev = _ret_fwd(proj, y, tables, gnw, 256)
    h1 = _mm_nn("out_proj", y, w_out, x, F32, TM, 1024, d_mix)
    u2 = _rms_fwd("rms2", h1, ln2_w, TM)
    gt = _mm_nn_stacked("ffn_gate", u2, wg_st, F32, TM)
    up = _mm_nn_stacked("ffn_up", u2, wu_st, F32, TM)
    ff = _swiglu_fwd(gt, up)
    h2 = _mm_nn("ffn_down", ff, wd, h1, F32, TM, 1024, n_ff)
    dh2, dh2b, d_fw, loss = _loss_head(h2, tgt, fw.reshape(1, D), TM)

    dff = _mm_nt("d_ff", dh2b, wd, F32, TM, n_ff)
    g_wd = _mm_tn("g_w_down", ff, dh2b, n_ff, 1024, 512)
    dgt, dup = _swiglu_bwd(dff, gt, up)
    g_wg = _mm_tn("g_w_gate", u2, dgt, 1024, None, 512, stacked_cols=n_ff)
    g_wu = _mm_tn("g_w_up", u2, dup, 1024, None, 512, stacked_cols=n_ff)
    du2 = _mm_nt_stacked("d_u2", [dgt, dup], [wg_st, wu_st], F32, TM, 1024)
    dh1, dh1b, d_ln2 = _rms_bwd("rms2_bwd", h1, ln2_w, du2, dh2, TM, True)
    dy = _mm_nt("d_y", dh1b, w_out, F32, TM, 1024)
    g_wout = _mm_tn("g_w_out", y, dh1b, 1024, 1024, 512)
    dpa, d_cw, d_cb, d_wa, d_ba, d_wx, d_bx, d_lam = _lru_bwd(proj, hs, dy, conv_w, conv_b, wab, ba, wxb, bx, lam, 128)
    dpb, d_gn = _ret_bwd(proj, rprev, dy, tables, gnw, 256)
    dproj = jnp.concatenate([dpa, dpb], axis=1)
    g_win = _mm_tn("g_w_in", u1, dproj, 1024, None, 512, stacked_cols=n_in)
    du1 = _mm_nt_stacked("d_u1", [dproj], [w_in_st], F32, TM, 1024)
    gx, d_ln1 = _rms_bwd("rms1_bwd", x, ln1_w, du1, dh1, TM, False)

    big = dict(w_in=g_win, w_out=g_wout.reshape(4, d_mix // 4, D), w_ffn_gate=g_wg, w_ffn_up=g_wu,
               w_ffn_down=g_wd.reshape(4, n_ff, D))
    small = dict(ln1_w=d_ln1, conv_w=d_cw, conv_b=d_cb, gate_a_w=d_wa, gate_a_b=d_ba, gate_x_w=d_wx, gate_x_b=d_bx,
                 lru_lambda=d_lam, ret_gn_w=d_gn, ln2_w=d_ln2, final_norm_w=d_fw)
    return loss, gx, big, small


HBM_SPEC = pl.BlockSpec(memory_space=pl.ANY)


def _place():
    x, y, c = lax.axis_index("x"), lax.axis_index("y"), lax.axis_index("c")
    chips = [(1 - x, y), (x, 1 - y), (1 - x, 1 - y)]
    return x, y, c, chips


def _own_slab(name, shard, place):
    R, C = shard.shape
    tr = _row_tile(R, C)
    return _ew("cast_" + name, lambda a: (a,), [(shard, pl.BlockSpec((tr, C), lambda i, p: (i, 0)))],
               [(jax.ShapeDtypeStruct((4, R, C), BF16), pl.BlockSpec((None, tr, C), lambda i, p: (p[1], i, 0)))],
               (R // tr,), sp=place)[0]


def _gather_weights(stacked, conv_w):
    nw = len(stacked)

    def body(*refs):
        cin = refs[nw]
        outs, cout = refs[nw + 1:2 * nw + 1], refs[2 * nw + 1]
        send_sems, recv_sems, local_sem = refs[2 * nw + 2:]
        x, y, c, chips = _place()
        me = 2 * x + y
        sibling = (x, y, 1 - c)

        def piece(w, chip, half):
            r2 = stacked[w].shape[1] // 2
            return outs[w].at[chip, pl.ds(half * r2, r2), :]

        def remote(k, src, dst, to):
            return pltpu.make_async_remote_copy(src_ref=src, dst_ref=dst, send_sem=send_sems.at[k], recv_sem=recv_sems.at[k],
                                                device_id=to, device_id_type=MESH)

        local = pltpu.make_async_copy(cin, cout.at[me], local_sem)
        local.start()
        started = []
        for w in range(nw):
            for j, chip in enumerate(chips):
                started.append(remote(6 * w + j, piece(w, me, c), piece(w, me, c), (*chip, c)))
                started[-1].start()
        for j, chip in enumerate(chips):
            started.append(remote(6 * nw + j, cin, cout.at[me], (*chip, c)))
            started[-1].start()
        for w in range(nw):
            for j, (cx, cy) in enumerate(chips):
                got = piece(w, 2 * cx + cy, c)
                remote(6 * w + j, got, got, sibling).wait_recv()
                started.append(remote(6 * w + 3 + j, got, got, sibling))
                started[-1].start()
        for w in range(nw):
            for j, (cx, cy) in enumerate(chips):
                got = piece(w, 2 * cx + cy, 1 - c)
                remote(6 * w + 3 + j, got, got, sibling).wait_recv()
        for j, (cx, cy) in enumerate(chips):
            got = cout.at[2 * cx + cy]
            remote(6 * nw + j, got, got, sibling).wait_recv()
        for cp in started:
            cp.wait_send()
        local.wait()

    out_shape = [jax.ShapeDtypeStruct(s.shape, s.dtype) for s in stacked] + \
                [jax.ShapeDtypeStruct((4,) + conv_w.shape, conv_w.dtype)]
    n_sem = 6 * nw + 3
    return pl.pallas_call(
        body, in_specs=[HBM_SPEC] * (nw + 1), out_specs=[HBM_SPEC] * (nw + 1), out_shape=out_shape,
        scratch_shapes=[pltpu.SemaphoreType.DMA((n_sem,)), pltpu.SemaphoreType.DMA((n_sem,)), pltpu.SemaphoreType.DMA],
        input_output_aliases={w: w for w in range(nw)}, name="gather_weights")(*stacked, conv_w)


def _pair_exchange(grads):
    nw = len(grads)

    def body(*refs):
        ins, outs = refs[:nw], refs[nw:2 * nw]
        send_sems, recv_sems = refs[2 * nw:]
        x, y, c, _ = _place()
        cps = []
        for w in range(nw):
            r2 = grads[w].shape[1] // 2
            cps.append(pltpu.make_async_remote_copy(
                src_ref=ins[w].at[:, pl.ds((1 - c) * r2, r2), :], dst_ref=outs[w], send_sem=send_sems.at[w],
                recv_sem=recv_sems.at[w], device_id=(x, y, 1 - c), device_id_type=MESH))
            cps[-1].start()
        for cp in cps:
            cp.wait()

    out_shape = [jax.ShapeDtypeStruct((g.shape[0], g.shape[1] // 2, g.shape[2]), g.dtype) for g in grads]
    return pl.pallas_call(
        body, in_specs=[HBM_SPEC] * nw, out_specs=[HBM_SPEC] * nw, out_shape=out_shape,
        scratch_shapes=[pltpu.SemaphoreType.DMA((nw,)), pltpu.SemaphoreType.DMA((nw,))], name="rs_pair_exchange")(*grads)


def _chip_exchange(parts):
    nw = len(parts)

    def body(*refs):
        ins, outs = refs[:nw], refs[nw:2 * nw]
        send_sems, recv_sems = refs[2 * nw:]
        x, y, c, chips = _place()
        cps = []
        for w in range(nw):
            for j, (cx, cy) in enumerate(chips):
                cps.append(pltpu.make_async_remote_copy(
                    src_ref=ins[w].at[2 * cx + cy], dst_ref=outs[w].at[j], send_sem=send_sems.at[3 * w + j],
                    recv_sem=recv_sems.at[3 * w + j], device_id=(cx, cy, c), device_id_type=MESH))
                cps[-1].start()
        for cp in cps:
            cp.wait()

    out_shape = [jax.ShapeDtypeStruct((3,) + p.shape[1:], p.dtype) for p in parts]
    return pl.pallas_call(
        body, in_specs=[HBM_SPEC] * nw, out_specs=[HBM_SPEC] * nw, out_shape=out_shape,
        scratch_shapes=[pltpu.SemaphoreType.DMA((3 * nw,)), pltpu.SemaphoreType.DMA((3 * nw,))], name="rs_chip_exchange")(*parts)


def _pair_share(slots):
    nw = len(slots)

    def body(*refs):
        outs = refs[nw:2 * nw]
        send_sems, recv_sems = refs[2 * nw:]
        x, y, c, _ = _place()
        cps = []
        for w in range(nw):
            cps.append(pltpu.make_async_remote_copy(
                src_ref=outs[w].at[c], dst_ref=outs[w].at[c], send_sem=send_sems.at[w], recv_sem=recv_sems.at[w],
                device_id=(x, y, 1 - c), device_id_type=MESH))
            cps[-1].start()
        for w in range(nw):
            cps[w].wait_send()
            pltpu.make_async_remote_copy(
                src_ref=outs[w].at[1 - c], dst_ref=outs[w].at[1 - c], send_sem=send_sems.at[w], recv_sem=recv_sems.at[w],
                device_id=(x, y, 1 - c), device_id_type=MESH).wait_recv()

    out_shape = [jax.ShapeDtypeStruct(h.shape, h.dtype) for h in slots]
    return pl.pallas_call(
        body, in_specs=[HBM_SPEC] * nw, out_specs=[HBM_SPEC] * nw, out_shape=out_shape,
        scratch_shapes=[pltpu.SemaphoreType.DMA((nw,)), pltpu.SemaphoreType.DMA((nw,))],
        input_output_aliases={w: w for w in range(nw)}, name="rs_pair_share")(*slots)


def _gather_small(sm):
    flips = [(fx, fy, fc) for fx in (0, 1) for fy in (0, 1) for fc in (0, 1)][1:]

    def body(in_ref, out_ref, send_sems, recv_sems, local_sem):
        x, y, c, _ = _place()
        me = 4 * x + 2 * y + c
        local = pltpu.make_async_copy(in_ref, out_ref.at[me], local_sem)
        local.start()
        peers = [(1 - x if fx else x, 1 - y if fy else y, 1 - c if fc else c) for fx, fy, fc in flips]
        cps = []
        for k, peer in enumerate(peers):
            cps.append(pltpu.make_async_remote_copy(src_ref=in_ref, dst_ref=out_ref.at[me], send_sem=send_sems.at[k],
                                                    recv_sem=recv_sems.at[k], device_id=peer, device_id_type=MESH))
            cps[-1].start()
        for k, (px, py, pc) in enumerate(peers):
            cps[k].wait_send()
            pltpu.make_async_remote_copy(src_ref=in_ref, dst_ref=out_ref.at[4 * px + 2 * py + pc], send_sem=send_sems.at[k],
                                         recv_sem=recv_sems.at[k], device_id=(px, py, pc), device_id_type=MESH).wait_recv()
        local.wait()

    return pl.pallas_call(
        body, in_specs=[HBM_SPEC], out_specs=HBM_SPEC, out_shape=jax.ShapeDtypeStruct((8,) + sm.shape, sm.dtype),
        scratch_shapes=[pltpu.SemaphoreType.DMA((7,)), pltpu.SemaphoreType.DMA((7,)), pltpu.SemaphoreType.DMA],
        name="gather_small")(sm)


def _adamw(w, g, m, v):
    m = ADAM_B1 * m + (1.0 - ADAM_B1) * g
    v = ADAM_B2 * v + (1.0 - ADAM_B2) * (g * g)
    m_hat = m / (1.0 - ADAM_B1 ** ADAM_STEP)
    v_hat = v / (1.0 - ADAM_B2 ** ADAM_STEP)
    delta = -ADAM_LR * (m_hat / (jnp.sqrt(v_hat) + ADAM_EPS) + ADAM_WD * w)
    return delta, m, v


def _adamw_call(name, w, g, m, v):
    R, C = w.shape
    tr = _row_tile(R, C, 1024 * 1024)
    row = pl.BlockSpec((tr, C), lambda i: (i, 0))
    o = jax.ShapeDtypeStruct((R, C), F32)
    return _ew(name, _adamw, [(w, row), (g, row), (m, row), (v, row)], [(o, row), (o, row), (o, row)], (R // tr,))


def _reduce_big(grads, place):
    names = list(grads)
    gl = [grads[n] for n in names]
    got_pair = _pair_exchange(gl)
    parts, halves = [], []
    for n, g, ra in zip(names, gl, got_pair):
        _, R, C = g.shape
        r2 = R // 2
        tr = _row_tile(r2, C)
        nb = r2 // tr
        own = pl.BlockSpec((None, tr, C), lambda j, i, p: (j, p[0] * nb + i, 0))
        blk = pl.BlockSpec((None, tr, C), lambda j, i, p: (j, i, 0))
        parts.append(_ew("rs_pair_sum_" + n, lambda a, b: (a + b,), [(g, own), (ra, blk)],
                         [(jax.ShapeDtypeStruct((4, r2, C), BF16), blk)], (4, nb), sp=place)[0])
    got_chip = _chip_exchange(parts)
    for n, g, ra, rb in zip(names, gl, got_pair, got_chip):
        _, R, C = g.shape
        r2 = R // 2
        tr = _row_tile(r2, C)
        nb = r2 // tr
        own = pl.BlockSpec((None, tr, C), lambda i, p: (p[1], p[0] * nb + i, 0))
        mine = pl.BlockSpec((None, tr, C), lambda i, p: (p[1], i, 0))
        src = [pl.BlockSpec((None, tr, C), functools.partial(lambda i, p, j: (j, i, 0), j=j)) for j in range(3)]
        out = pl.BlockSpec((None, tr, C), lambda i, p: (p[0], i, 0))

        def total(a, b, r0, r1, r2_):
            return ((((a + b) + r0.astype(F32)) + r1.astype(F32)) + r2_.astype(F32),)

        halves.append(_ew("rs_chip_sum_" + n, total, [(g, own), (ra, mine), (rb, src[0]), (rb, src[1]), (rb, src[2])],
                          [(jax.ShapeDtypeStruct((2, r2, C), F32), out)], (nb,), sp=place)[0])
    full = _pair_share(halves)
    return {n: f.reshape(f.shape[0] * f.shape[1], f.shape[2]) for n, f in zip(names, full)}


def _pack(arrays):
    rows, offs, pos = [], [], 0
    for a in arrays:
        flat = a.reshape(-1)
        n = -(-flat.shape[0] // (8 * LANES)) * (8 * LANES)
        if n != flat.shape[0]:
            flat = jnp.pad(flat, (0, n - flat.shape[0]))
        rows.append(flat.reshape(-1, LANES))
        offs.append(pos)
        pos += n // LANES
    return jnp.concatenate(rows, axis=0), offs


def _unpack(packed, offs, shapes):
    out = []
    for off, shp in zip(offs, shapes):
        n = 1
        for s in shp:
            n *= s
        out.append(packed[off:off + -(-n // LANES)].reshape(-1)[:n].reshape(shp))
    return out


def _sum8(gathered):
    _, R, C = gathered.shape
    tr = _row_tile(R, C, 256 * 1024)
    specs = [pl.BlockSpec((None, tr, C), functools.partial(lambda i, d: (d, i, 0), d=d)) for d in range(8)]

    def fn(*parts):
        t = parts[0]
        for p in parts[1:]:
            t = t + p
        return (t,)

    return _ew("small_sum", fn, [(gathered, s) for s in specs],
               [(jax.ShapeDtypeStruct((R, C), F32), pl.BlockSpec((tr, C), lambda i: (i, 0)))], (R // tr,))[0]


BIG = ("w_in", "w_out", "w_ffn_gate", "w_ffn_up", "w_ffn_down")
SMALL = ("ln1_w", "conv_w", "conv_b", "gate_a_w", "gate_a_b", "gate_x_w", "gate_x_b", "lru_lambda", "ret_gn_w", "ln2_w",
         "final_norm_w")
WEIGHTS = ("ln1_w", "w_in", "conv_w", "conv_b", "gate_a_w", "gate_a_b", "gate_x_w", "gate_x_b", "lru_lambda", "ret_gn_w",
           "w_out", "ln2_w", "w_ffn_gate", "w_ffn_up", "w_ffn_down", "final_norm_w")


def kernel(x, ln1_w, w_in, conv_w, conv_b, gate_a_w, gate_a_b, gate_x_w, gate_x_b, lru_lambda, ret_gn_w, w_out, ln2_w, w_ffn_gate, w_ffn_up, w_ffn_down, final_norm_w, loss_target, m_ln1_w, m_w_in, m_conv_w, m_conv_b, m_gate_a_w, m_gate_a_b, m_gate_x_w, m_gate_x_b, m_lru_lambda, m_ret_gn_w, m_w_out, m_ln2_w, m_w_ffn_gate, m_w_ffn_up, m_w_ffn_down, m_final_norm_w, v_ln1_w, v_w_in, v_conv_w, v_conv_b, v_gate_a_w, v_gate_a_b, v_gate_x_w, v_gate_x_b, v_lru_lambda, v_ret_gn_w, v_w_out, v_ln2_w, v_w_ffn_gate, v_w_ffn_up, v_w_ffn_down, v_final_norm_w):
    w = dict(ln1_w=ln1_w, w_in=w_in, conv_w=conv_w, conv_b=conv_b, gate_a_w=gate_a_w, gate_a_b=gate_a_b, gate_x_w=gate_x_w,
             gate_x_b=gate_x_b, lru_lambda=lru_lambda, ret_gn_w=ret_gn_w, w_out=w_out, ln2_w=ln2_w, w_ffn_gate=w_ffn_gate,
             w_ffn_up=w_ffn_up, w_ffn_down=w_ffn_down, final_norm_w=final_norm_w)
    m = dict(ln1_w=m_ln1_w, w_in=m_w_in, conv_w=m_conv_w, conv_b=m_conv_b, gate_a_w=m_gate_a_w, gate_a_b=m_gate_a_b,
             gate_x_w=m_gate_x_w, gate_x_b=m_gate_x_b, lru_lambda=m_lru_lambda, ret_gn_w=m_ret_gn_w, w_out=m_w_out,
             ln2_w=m_ln2_w, w_ffn_gate=m_w_ffn_gate, w_ffn_up=m_w_ffn_up, w_ffn_down=m_w_ffn_down,
             final_norm_w=m_final_norm_w)
    v = dict(ln1_w=v_ln1_w, w_in=v_w_in, conv_w=v_conv_w, conv_b=v_conv_b, gate_a_w=v_gate_a_w, gate_a_b=v_gate_a_b,
             gate_x_w=v_gate_x_w, gate_x_b=v_gate_x_b, lru_lambda=v_lru_lambda, ret_gn_w=v_ret_gn_w, w_out=v_w_out,
             ln2_w=v_ln2_w, w_ffn_gate=v_w_ffn_gate, w_ffn_up=v_w_ffn_up, w_ffn_down=v_w_ffn_down,
             final_norm_w=v_final_norm_w)
    S, D = x.shape[1], x.shape[2]
    chip = 2 * lax.axis_index("x") + lax.axis_index("y")
    place = jnp.stack([lax.axis_index("c"), chip]).astype(jnp.int32)

    *stk, conv_st = _gather_weights([_own_slab(n, w[n][0], place) for n in BIG], w["conv_w"][0])
    w_in_st, w_out_st, wg_st, wu_st, wd_st = stk
    cw_cols = conv_st.shape[2]
    conv_full = jnp.transpose(conv_st, (1, 0, 2)).reshape(CONV_W, 4 * cw_cols)
    d_mix = w_out_st.shape[0] * w_out_st.shape[1]
    n_ffs = wd_st.shape[1]

    loss, gx, big, small = _layer_step(
        x[0], loss_target[0], ln1_w, w_in_st, conv_full, conv_b, gate_a_w[0], gate_a_b, gate_x_w[0], gate_x_b, lru_lambda,
        ret_gn_w, w_out_st.reshape(d_mix, D), ln2_w, wg_st, wu_st, wd_st.reshape(4 * n_ffs, D), final_norm_w)

    g_big = _reduce_big(big, place)
    packed, offs = _pack([small[n] for n in SMALL] + [loss])
    total = _sum8(_gather_small(packed))
    red = _unpack(total, offs, [small[n].shape for n in SMALL] + [(1, LANES)])
    g = dict(zip(SMALL, red[:-1]))
    loss_out = red[-1][0, 0]
    g["conv_w"] = lax.dynamic_slice(g["conv_w"], (0, chip * cw_cols), (CONV_W, cw_cols))

    grad, delta, new_m, new_v = {}, {}, {}, {}
    for n in BIG:
        shp = w[n].shape
        w2, m2, v2 = (t[n].reshape(shp[1], shp[2]) for t in (w, m, v))
        d_, m_, v_ = _adamw_call("adamw_" + n, w2, g_big[n], m2, v2)
        grad[n], delta[n], new_m[n], new_v[n] = (t.reshape(shp) for t in (g_big[n], d_, m_, v_))
    packs = [_pack([t[n] for n in SMALL])[0] for t in (w, m, v)]
    gp, offs2 = _pack([g[n] for n in SMALL])
    outs = _adamw_call("adamw_small", packs[0], gp, packs[1], packs[2])
    shapes = [w[n].shape for n in SMALL]
    for dst, arr in zip((delta, new_m, new_v), outs):
        dst.update(zip(SMALL, _unpack(arr, offs2, shapes)))
    for n in SMALL:
        grad[n] = g[n].reshape(w[n].shape)

    return (loss_out, gx.reshape(x.shape), *[grad[n] for n in WEIGHTS], *[delta[n] for n in WEIGHTS],
            *[new_m[n] for n in WEIGHTS], *[new_v[n] for n in WEIGHTS])
```

```python
import functools

import jax
import jax.numpy as jnp
from jax import lax
from jax.experimental import pallas as pl
from jax.experimental.pallas import tpu as pltpu

F32 = jnp.float32
BF16 = jnp.bfloat16
MESH = pl.DeviceIdType.MESH

EPS = 1e-6
LRU_C = 8.0
ROPE_BASE = 10000.0
CHUNK = 128
HEAD_DIM = 128
CONV_W = 4
ADAM_LR = 0.001
ADAM_B1 = 0.9
ADAM_B2 = 0.999
ADAM_EPS = 1e-08
ADAM_WD = 0.01
ADAM_STEP = 10

V7X_VMEM_BYTES = 64 * 1024 * 1024
VMEM_LIMIT = V7X_VMEM_BYTES - 8 * 1024 * 1024
LANES = 128
SUBLANES_16BIT = 16

NN = (((1,), (0,)), ((), ()))
NT = (((1,), (1,)), ((), ()))
TN = (((0,), (0,)), ((), ()))


def _dot(a, b, dims=NN):
    return lax.dot_general(a, b, dims, preferred_element_type=F32)


def _tile(n, pref, mult=SUBLANES_16BIT):
    best = None
    t = mult
    while t <= min(n, pref):
        if n % t == 0:
            best = t
        t += mult
    return best if best is not None else n


def _row_tile(rows, cols, budget_bytes=2 * 1024 * 1024):
    return _tile(rows, max(SUBLANES_16BIT, budget_bytes // (cols * 4)))


def _params(sem):
    return pltpu.CompilerParams(dimension_semantics=sem, vmem_limit_bytes=VMEM_LIMIT)


def _ew(name, fn, ins, outs, grid, sp=None):
    n_in = len(ins)

    def body(*refs):
        if sp is not None:
            refs = refs[1:]
        vals = [r[...] for r in refs[:n_in]]
        res = fn(*vals)
        for o_ref, v in zip(refs[n_in:], res):
            o_ref[...] = v.astype(o_ref.dtype)

    in_specs = [s for _, s in ins]
    out_specs = [s for _, s in outs]
    out_shape = [s for s, _ in outs]
    sem = ("arbitrary",) * len(grid)
    if sp is None:
        return pl.pallas_call(body, grid=grid, in_specs=in_specs, out_specs=out_specs, out_shape=out_shape,
                              name=name, compiler_params=_params(sem))(*[a for a, _ in ins])
    gs = pltpu.PrefetchScalarGridSpec(num_scalar_prefetch=1, grid=grid, in_specs=in_specs, out_specs=out_specs)
    return pl.pallas_call(body, grid_spec=gs, out_shape=out_shape, name=name,
                          compiler_params=_params(sem))(sp, *[a for a, _ in ins])


def _matmul(name, pairs, dims, grid, out_shape, out_spec, acc_shape, res=None):
    n = len(pairs)
    nk = grid[2]

    def body(*refs):
        ab = refs[:2 * n]
        pos = 2 * n
        res_ref = None
        if res is not None:
            res_ref = refs[pos]
            pos += 1
        o_ref = refs[pos]
        acc_ref = refs[pos + 1] if nk > 1 else None

        def partial():
            t = None
            for p in range(n):
                d = _dot(ab[2 * p][...], ab[2 * p + 1][...], dims)
                t = d if t is None else t + d
            return t

        def finish(t):
            if res_ref is not None:
                t = t + res_ref[...]
            o_ref[...] = t.astype(o_ref.dtype)

        if nk == 1:
            finish(partial())
        else:
            k = pl.program_id(2)

            @pl.when(k == 0)
            def _():
                acc_ref[...] = partial()

            @pl.when(k > 0)
            def _():
                acc_ref[...] += partial()

            @pl.when(k == nk - 1)
            def _():
                finish(acc_ref[...])

    operands, in_specs = [], []
    for a, a_spec, b, b_spec in pairs:
        operands += [a, b]
        in_specs += [a_spec, b_spec]
    if res is not None:
        operands.append(res[0])
        in_specs.append(res[1])
    scratch = [pltpu.VMEM(acc_shape, F32)] if nk > 1 else []
    return pl.pallas_call(body, grid=grid, in_specs=in_specs, out_specs=out_spec, out_shape=out_shape,
                          scratch_shapes=scratch, name=name,
                          compiler_params=_params(("arbitrary", "arbitrary", "arbitrary")))(*operands)


def _mm_nn_stacked(name, a, b_st, out_dtype, tm):
    M, K = a.shape
    J, _, Nj = b_st.shape
    tm = _tile(M, tm)
    return _matmul(
        name, [(a, pl.BlockSpec((tm, K), lambda j, i, k: (i, 0)), b_st, pl.BlockSpec((None, K, Nj), lambda j, i, k: (j, 0, 0)))],
        NN, (J, M // tm, 1), jax.ShapeDtypeStruct((M, J * Nj), out_dtype), pl.BlockSpec((tm, Nj), lambda j, i, k: (i, j)), None)


def _mm_nn(name, a, b, res, out_dtype, tm, tn, tk):
    M, K = a.shape
    N = b.shape[1]
    tm, tn, tk = _tile(M, tm), _tile(N, tn, LANES), _tile(K, tk, LANES)
    return _matmul(
        name, [(a, pl.BlockSpec((tm, tk), lambda j, i, k: (i, k)), b, pl.BlockSpec((tk, tn), lambda j, i, k: (k, j)))],
        NN, (N // tn, M // tm, K // tk), jax.ShapeDtypeStruct((M, N), out_dtype), pl.BlockSpec((tm, tn), lambda j, i, k: (i, j)),
        (tm, tn), res=(res, pl.BlockSpec((tm, tn), lambda j, i, k: (i, j))))


def _mm_nt(name, a, b, out_dtype, tm, tn):
    M, K = a.shape
    N = b.shape[0]
    tm, tn = _tile(M, tm), _tile(N, tn, LANES)
    return _matmul(
        name, [(a, pl.BlockSpec((tm, K), lambda j, i, k: (i, 0)), b, pl.BlockSpec((tn, K), lambda j, i, k: (j, 0)))],
        NT, (N // tn, M // tm, 1), jax.ShapeDtypeStruct((M, N), out_dtype), pl.BlockSpec((tm, tn), lambda j, i, k: (i, j)), None)


def _mm_nt_stacked(name, a_list, b_list, out_dtype, tm, tn):
    n = len(a_list)
    M = a_list[0].shape[0]
    J, N, Nj = b_list[0].shape
    tm, tn = _tile(M, tm), _tile(N, tn, LANES)

    def body(*refs):
        o_ref = refs[2 * n]
        t = None
        for p in range(n):
            a_ref, b_ref = refs[p], refs[n + p]
            for s in range(J):
                d = _dot(a_ref[:, s * Nj:(s + 1) * Nj], b_ref[s], NT)
                t = d if t is None else t + d
        o_ref[...] = t.astype(o_ref.dtype)

    a_spec = pl.BlockSpec((tm, J * Nj), lambda j, i: (i, 0))
    b_spec = pl.BlockSpec((J, tn, Nj), lambda j, i: (0, j, 0), pipeline_mode=pl.Buffered(1))
    return pl.pallas_call(
        body, grid=(N // tn, M // tm), in_specs=[a_spec] * n + [b_spec] * n,
        out_specs=pl.BlockSpec((tm, tn), lambda j, i: (i, j)), out_shape=jax.ShapeDtypeStruct((M, N), out_dtype),
        name=name, compiler_params=_params(("arbitrary", "arbitrary")))(*a_list, *b_list)


def _ffn_gate_up(u2, wg_st, wu_st, tm):
    S, D = u2.shape
    J, _, Nj = wg_st.shape
    tm = _tile(S, tm)

    def body(a_ref, wg_ref, wu_ref, gt_ref, up_ref, ff_ref):
        a = a_ref[...]
        g = _dot(a, wg_ref[...])
        u = _dot(a, wu_ref[...])
        gt_ref[...] = g.astype(BF16)
        up_ref[...] = u.astype(BF16)
        ff_ref[...] = (g * jax.nn.sigmoid(g) * u).astype(BF16)

    w_spec = pl.BlockSpec((None, D, Nj), lambda j, i: (j, 0, 0))
    o_spec = pl.BlockSpec((tm, Nj), lambda j, i: (i, j))
    o = jax.ShapeDtypeStruct((S, J * Nj), BF16)
    return pl.pallas_call(
        body, grid=(J, S // tm), in_specs=[pl.BlockSpec((tm, D), lambda j, i: (i, 0)), w_spec, w_spec],
        out_specs=[o_spec, o_spec, o_spec], out_shape=[o, o, o], name="ffn_gate_up",
        compiler_params=_params(("arbitrary", "arbitrary")))(u2, wg_st, wu_st)


def _ffn_gate_up_bwd(dh2b, wd, gt, up, tm, tn):
    S, D = dh2b.shape
    F = wd.shape[0]
    tm, tn = _tile(S, tm), _tile(F, tn, LANES)

    def body(a_ref, wd_ref, gt_ref, up_ref, dgt_ref, dup_ref):
        d = _dot(a_ref[...], wd_ref[...], NT)
        g = gt_ref[...].astype(F32)
        u = up_ref[...].astype(F32)
        sg = jax.nn.sigmoid(g)
        dgt_ref[...] = (d * u * (sg * (1.0 + g * (1.0 - sg)))).astype(BF16)
        dup_ref[...] = (d * (g * sg)).astype(BF16)

    blk = pl.BlockSpec((tm, tn), lambda j, i: (i, j))
    o = jax.ShapeDtypeStruct((S, F), BF16)
    return pl.pallas_call(
        body, grid=(F // tn, S // tm),
        in_specs=[pl.BlockSpec((tm, D), lambda j, i: (i, 0)), pl.BlockSpec((tn, D), lambda j, i: (j, 0)), blk, blk],
        out_specs=[blk, blk], out_shape=[o, o], name="ffn_gate_up_bwd",
        compiler_params=_params(("arbitrary", "arbitrary")))(dh2b, wd, gt, up)


def _mm_tn(name, a, b, tmo, tn, tk, stacked_cols=None):
    S, Mo = a.shape
    N = b.shape[1]
    tmo, tk = _tile(Mo, tmo, LANES), _tile(S, tk)
    if stacked_cols is None:
        tn = _tile(N, tn, LANES)
        out_shape = jax.ShapeDtypeStruct((Mo, N), F32)
        out_spec = pl.BlockSpec((tmo, tn), lambda i, j, k: (i, j))
    else:
        tn = stacked_cols
        out_shape = jax.ShapeDtypeStruct((N // tn, Mo, tn), F32)
        out_spec = pl.BlockSpec((None, tmo, tn), lambda i, j, k: (j, i, 0))
    return _matmul(
        name, [(a, pl.BlockSpec((tk, tmo), lambda i, j, k: (k, i)), b, pl.BlockSpec((tk, tn), lambda i, j, k: (k, j)))],
        TN, (Mo // tmo, N // tn, S // tk), out_shape, out_spec, (tmo, tn))


def _rms_fwd(name, x, w, tm):
    S, D = x.shape
    tm = _tile(S, tm)

    def fn(xv, wv):
        r = lax.rsqrt(jnp.mean(xv * xv, axis=-1, keepdims=True) + EPS)
        return ((xv * r) * wv,)

    row = pl.BlockSpec((tm, D), lambda i: (i, 0))
    return _ew(name, fn, [(x, row), (w, pl.BlockSpec((1, D), lambda i: (0, 0)))],
               [(jax.ShapeDtypeStruct((S, D), BF16), row)], (S // tm,))[0]


def _rms_bwd(name, x, w, dy, dres, tm, want_bf16):
    S, D = x.shape
    tm = _tile(S, tm)

    def body(x_ref, w_ref, dy_ref, dres_ref, dx_ref, *rest):
        dw_ref = rest[-1]
        i = pl.program_id(0)

        @pl.when(i == 0)
        def _():
            dw_ref[...] = jnp.zeros_like(dw_ref)

        xv = x_ref[...]
        r = lax.rsqrt(jnp.mean(xv * xv, axis=-1, keepdims=True) + EPS)
        nv = xv * r
        dyv = dy_ref[...]
        dn = dyv * w_ref[...]
        dw_ref[...] += jnp.sum(dyv * nv, axis=0, keepdims=True)
        dx = dres_ref[...] + r * (dn - nv * jnp.mean(dn * nv, axis=-1, keepdims=True))
        dx_ref[...] = dx
        if want_bf16:
            rest[0][...] = dx.astype(BF16)

    row = pl.BlockSpec((tm, D), lambda i: (i, 0))
    vec = pl.BlockSpec((1, D), lambda i: (0, 0))
    out_shape = [jax.ShapeDtypeStruct((S, D), F32)] + ([jax.ShapeDtypeStruct((S, D), BF16)] if want_bf16 else []) + \
                [jax.ShapeDtypeStruct((1, D), F32)]
    out_specs = [row] + ([row] if want_bf16 else []) + [vec]
    return pl.pallas_call(body, grid=(S // tm,), in_specs=[row, vec, row, row], out_specs=out_specs, out_shape=out_shape,
                          name=name, compiler_params=_params(("arbitrary",)))(x, w, dy, dres)


def _loss_head(h2, tgt, fw, tm):
    S, D = h2.shape
    tm = _tile(S, tm)

    def body(h_ref, t_ref, w_ref, dh_ref, dhb_ref, dw_ref, loss_ref):
        i = pl.program_id(0)

        @pl.when(i == 0)
        def _():
            dw_ref[...] = jnp.zeros_like(dw_ref)
            loss_ref[...] = jnp.zeros_like(loss_ref)

        hv = h_ref[...]
        wv = w_ref[...]
        r = lax.rsqrt(jnp.mean(hv * hv, axis=-1, keepdims=True) + EPS)
        nv = hv * r
        err = nv * wv - t_ref[...]
        row_loss = jnp.mean(err * err, axis=-1, keepdims=True)
        loss_ref[...] += 0.5 * jnp.sum(row_loss, axis=0, keepdims=True)
        dyo = err * (1.0 / D)
        dn = dyo * wv
        dw_ref[...] += jnp.sum(dyo * nv, axis=0, keepdims=True)
        dh = r * (dn - nv * jnp.mean(dn * nv, axis=-1, keepdims=True))
        dh_ref[...] = dh
        dhb_ref[...] = dh.astype(BF16)

    row = pl.BlockSpec((tm, D), lambda i: (i, 0))
    vec = pl.BlockSpec((1, D), lambda i: (0, 0))
    return pl.pallas_call(
        body, grid=(S // tm,), in_specs=[row, row, vec],
        out_specs=[row, row, vec, pl.BlockSpec((1, LANES), lambda i: (0, 0))],
        out_shape=[jax.ShapeDtypeStruct((S, D), F32), jax.ShapeDtypeStruct((S, D), BF16),
                   jax.ShapeDtypeStruct((1, D), F32), jax.ShapeDtypeStruct((1, LANES), F32)],
        name="loss_head", compiler_params=_params(("arbitrary",)))(h2, tgt, fw)


def _shift_down(x, d, head8):
    r = pltpu.roll(x, d, 0)
    rh = pltpu.roll(head8, d, 0)
    row8 = lax.broadcasted_iota(jnp.int32, head8.shape, 0)
    top = jnp.where(row8 < d, rh, r[0:8])
    return jnp.concatenate([top, r[8:]], axis=0)


def _shift_up(x, d, tail8):
    n = x.shape[0]
    r = pltpu.roll(x, n - d, 0)
    rt = pltpu.roll(tail8, 8 - d, 0)
    row8 = lax.broadcasted_iota(jnp.int32, tail8.shape, 0)
    bot = jnp.where(row8 + d >= 8, rt, r[n - 8:n])
    return jnp.concatenate([r[:n - 8], bot], axis=0)


def _log_sigmoid(lam):
    z = jnp.exp(-jnp.abs(lam))
    u = 1.0 + z
    log1p = jnp.where(u == 1.0, z, jnp.log(u) * (z / jnp.where(u == 1.0, 1.0, u - 1.0)))
    return jnp.minimum(lam, 0.0) - log1p


def _neg_expm1(z):
    series = -z * (1.0 + z * (0.5 + z * (1.0 / 6.0 + z * (1.0 / 24.0 + z * (1.0 / 120.0)))))
    return jnp.where(z > -0.05, series, 1.0 - jnp.exp(z))


_GELU_C = 0.7978845608028654


def _gelu(x):
    t = jnp.tanh(_GELU_C * (x + 0.044715 * (x * x * x)))
    return x * (0.5 * (1.0 + t)), t


def _gelu_grad(x, t):
    return 0.5 * (1.0 + t) + 0.5 * x * (1.0 - t * t) * (_GELU_C * (1.0 + 3.0 * 0.044715 * (x * x)))


def _lru_gates(lx, head8, cw, cb, wa_ref, ba, wx_ref, bx, ls):
    nb = wa_ref.shape[0]
    sh = [lx] + [_shift_down(lx, d, head8) for d in (1, 2, 3)]
    cx = cb + sh[3] * cw[0:1]
    cx = cx + sh[2] * cw[1:2]
    cx = cx + sh[1] * cw[2:3]
    cx = cx + sh[0] * cw[3:4]
    cxb = cx.astype(BF16)
    ra = jnp.concatenate([_dot(cxb[:, n * HEAD_DIM:(n + 1) * HEAD_DIM], wa_ref[n]) for n in range(nb)], axis=1) + ba
    ia = jnp.concatenate([_dot(cxb[:, n * HEAD_DIM:(n + 1) * HEAD_DIM], wx_ref[n]) for n in range(nb)], axis=1) + bx
    r = jax.nn.sigmoid(ra)
    ig = jax.nn.sigmoid(ia)
    log_a = LRU_C * r * ls
    a = jnp.exp(log_a)
    mult = jnp.sqrt(_neg_expm1(2.0 * log_a))
    return sh, cx, cxb, r, ig, a, mult


def _lru_specs(tl, DL):
    nb = DL // HEAD_DIM
    vec = pl.BlockSpec((1, DL), lambda i: (0, 0))
    return [pl.BlockSpec((CONV_W, DL), lambda i: (0, 0)), vec,
            pl.BlockSpec((nb, HEAD_DIM, HEAD_DIM), lambda i: (0, 0, 0)), vec,
            pl.BlockSpec((nb, HEAD_DIM, HEAD_DIM), lambda i: (0, 0, 0)), vec, vec]


def _lru_fwd(proj, cw, cb, wa, ba, wx, bx, lam, tl, d_mix):
    S = proj.shape[0]
    DL = cb.shape[1]
    tl = _tile(S, tl)

    def body(lx_ref, lg_ref, cw_ref, cb_ref, wa_ref, ba_ref, wx_ref, bx_ref, lam_ref, h_ref, y_ref, prev8, hc, a_s, b_s):
        i = pl.program_id(0)

        @pl.when(i == 0)
        def _():
            prev8[...] = jnp.zeros_like(prev8)
            hc[...] = jnp.zeros_like(hc)

        lx = lx_ref[...]
        ls = _log_sigmoid(lam_ref[...])
        _, cx, _, _, ig, a, mult = _lru_gates(lx, prev8[...], cw_ref[...], cb_ref[...], wa_ref, ba_ref[...],
                                              wx_ref, bx_ref[...], ls)
        b = mult * (ig * cx)
        row = lax.broadcasted_iota(jnp.int32, a.shape, 0) & 7
        for d in (1, 2, 4):
            a_sh = pltpu.roll(a, d, 0)
            b_sh = pltpu.roll(b, d, 0)
            m = row >= d
            b = jnp.where(m, a * b_sh + b, b)
            a = jnp.where(m, a * a_sh, a)
        a_s[...] = a
        b_s[...] = b

        def step(g, hprev):
            sl = pl.ds(pl.multiple_of(g * 8, 8), 8)
            hh = a_s[sl, :] * hprev + b_s[sl, :]
            h_ref[sl, :] = hh
            return hh[7:8, :]

        hc[0:1, :] = lax.fori_loop(0, tl // 8, step, hc[0:1, :])
        prev8[...] = lx[tl - 8:tl]
        g, _ = _gelu(lg_ref[...])
        y_ref[...] = (h_ref[...] * g).astype(BF16)

    return pl.pallas_call(
        body, grid=(S // tl,),
        in_specs=[pl.BlockSpec((tl, DL), lambda i: (i, 0)), pl.BlockSpec((tl, DL), lambda i: (i, 1))] + _lru_specs(tl, DL),
        out_specs=[pl.BlockSpec((tl, DL), lambda i: (i, 0)), pl.BlockSpec((tl, DL), lambda i: (i, 0))],
        out_shape=[jax.ShapeDtypeStruct((S, DL), F32), jax.ShapeDtypeStruct((S, d_mix), BF16)],
        scratch_shapes=[pltpu.VMEM((8, DL), F32), pltpu.VMEM((8, DL), F32), pltpu.VMEM((tl, DL), F32), pltpu.VMEM((tl, DL), F32)],
        name="lru_fwd", compiler_params=_params(("arbitrary",)))(proj, proj, cw, cb, wa, ba, wx, bx, lam)


def _lru_bwd(proj, h, dy, cw, cb, wa, ba, wx, bx, lam, tl):
    S = proj.shape[0]
    DL = cb.shape[1]
    nb = DL // HEAD_DIM
    tl = _tile(S, tl)
    nt = S // tl
    ng = tl // 8
    t8 = tl // 8

    def body(lx_ref, lxp_ref, lg_ref, h_ref, hp_ref, dy_ref, cw_ref, cb_ref, wa_ref, ba_ref, wx_ref, bx_ref, lam_ref,
             dlxg_ref, dcw_ref, dcb_ref, dwa_ref, dba_ref, dwx_ref, dbx_ref, dlam_ref,
             a_next, g_carry, dcx_next, an_s, dh_s, g_s):
        i = pl.program_id(0)

        @pl.when(i == 0)
        def _():
            for ref in (dcw_ref, dcb_ref, dwa_ref, dba_ref, dwx_ref, dbx_ref, dlam_ref, a_next, g_carry, dcx_next):
                ref[...] = jnp.zeros_like(ref)

        first = i == nt - 1
        lx = lx_ref[...]
        hv = h_ref[...]
        lg = lg_ref[...]
        dyv = dy_ref[...]
        head8 = jnp.where(first, 0.0, lxp_ref[...])
        hhead8 = jnp.where(first, 0.0, hp_ref[...])
        lamv = lam_ref[...]
        ls = _log_sigmoid(lamv)
        cwv = cw_ref[...]
        sh, cx, cxb, r, ig, a, mult = _lru_gates(lx, head8, cwv, cb_ref[...], wa_ref, ba_ref[...], wx_ref, bx_ref[...], ls)
        hprev = _shift_down(hv, 1, hhead8)
        g, t = _gelu(lg)
        dlg = dyv * hv * _gelu_grad(lg, t)
        dh = dyv * g
        an = _shift_up(a, 1, a_next[...])
        row = lax.broadcasted_iota(jnp.int32, a.shape, 0) & 7
        for d in (1, 2, 4):
            an_sh = pltpu.roll(an, tl - d, 0)
            dh_sh = pltpu.roll(dh, tl - d, 0)
            m = row + d < 8
            dh = jnp.where(m, an * dh_sh + dh, dh)
            an = jnp.where(m, an * an_sh, an)
        an_s[...] = an
        dh_s[...] = dh

        def step(k, gc):
            sl = pl.ds(pl.multiple_of((ng - 1 - k) * 8, 8), 8)
            gg = an_s[sl, :] * gc + dh_s[sl, :]
            g_s[sl, :] = gg
            return gg[0:1, :]

        g_carry[0:1, :] = lax.fori_loop(0, ng, step, g_carry[0:1, :])
        a_next[...] = a[0:8]
        G = g_s[...]
        da = G * hprev
        icx = ig * cx
        dmult = G * icx
        dicx = G * mult
        di = dicx * cx
        dcx = dicx * ig
        dlog = da * a - dmult * (a * a) / mult
        dr = dlog * (LRU_C * ls)
        dlam_ref[...] += jnp.sum(dlog * (LRU_C * r), axis=0, keepdims=True)
        dra = dr * r * (1.0 - r)
        dia = di * ig * (1.0 - ig)
        dba_ref[...] += jnp.sum(dra, axis=0, keepdims=True)
        dbx_ref[...] += jnp.sum(dia, axis=0, keepdims=True)
        drab = dra.astype(BF16)
        diab = dia.astype(BF16)
        back = []
        for n in range(nb):
            cs = slice(n * HEAD_DIM, (n + 1) * HEAD_DIM)
            dwa_ref[n] += _dot(cxb[:, cs], drab[:, cs], TN)
            dwx_ref[n] += _dot(cxb[:, cs], diab[:, cs], TN)
            back.append(_dot(drab[:, cs], wa_ref[n], NT) + _dot(diab[:, cs], wx_ref[n], NT))
        dcx = dcx + jnp.concatenate(back, axis=1)
        dcb_ref[...] += jnp.sum(dcx, axis=0, keepdims=True)
        for tap in range(CONV_W):
            dcw_ref[tap:tap + 1, :] += jnp.sum(dcx * sh[CONV_W - 1 - tap], axis=0, keepdims=True)
        tail = dcx_next[...]
        dlx = dcx * cwv[3:4]
        for d in (1, 2, 3):
            dlx = dlx + _shift_up(dcx, d, tail) * cwv[3 - d:4 - d]
        dcx_next[...] = dcx[0:8]
        dlxg_ref[:, 0:DL] = dlx.astype(BF16)
        dlxg_ref[:, DL:2 * DL] = dlg.astype(BF16)

        @pl.when(i == nt - 1)
        def _():
            dlam_ref[...] = dlam_ref[...] * (1.0 - jax.nn.sigmoid(lamv))

    rev = lambda i: nt - 1 - i
    prev8_map = lambda i: (jnp.maximum((nt - 1 - i) * t8 - 1, 0), 0)
    vec = pl.BlockSpec((1, DL), lambda i: (0, 0))
    mat = pl.BlockSpec((nb, HEAD_DIM, HEAD_DIM), lambda i: (0, 0, 0))
    return pl.pallas_call(
        body, grid=(nt,),
        in_specs=[pl.BlockSpec((tl, DL), lambda i: (rev(i), 0)), pl.BlockSpec((8, DL), prev8_map),
                  pl.BlockSpec((tl, DL), lambda i: (rev(i), 1)),
                  pl.BlockSpec((tl, DL), lambda i: (rev(i), 0)), pl.BlockSpec((8, DL), prev8_map),
                  pl.BlockSpec((tl, DL), lambda i: (rev(i), 0))] + _lru_specs(tl, DL),
        out_specs=[pl.BlockSpec((tl, 2 * DL), lambda i: (rev(i), 0)), pl.BlockSpec((CONV_W, DL), lambda i: (0, 0)), vec,
                   mat, vec, mat, vec, vec],
        out_shape=[jax.ShapeDtypeStruct(proj.shape, BF16), jax.ShapeDtypeStruct((CONV_W, DL), F32),
                   jax.ShapeDtypeStruct((1, DL), F32), jax.ShapeDtypeStruct((nb, HEAD_DIM, HEAD_DIM), F32),
                   jax.ShapeDtypeStruct((1, DL), F32), jax.ShapeDtypeStruct((nb, HEAD_DIM, HEAD_DIM), F32),
                   jax.ShapeDtypeStruct((1, DL), F32), jax.ShapeDtypeStruct((1, DL), F32)],
        scratch_shapes=[pltpu.VMEM((8, DL), F32), pltpu.VMEM((8, DL), F32), pltpu.VMEM((8, DL), F32),
                        pltpu.VMEM((tl, DL), F32), pltpu.VMEM((tl, DL), F32), pltpu.VMEM((tl, DL), F32)],
        name="lru_bwd", compiler_params=_params(("arbitrary",)))(proj, proj, proj, h, h, dy, cw, cb, wa, ba, wx, bx, lam)


def _ret_tables(S, H):
    pos = jnp.arange(S, dtype=F32)
    inv_freq = ROPE_BASE ** (-jnp.arange(0, HEAD_DIM, 2, dtype=F32) / HEAD_DIM)
    ang = pos[:, None] * inv_freq[None, :]
    cos, sin = jnp.cos(ang), jnp.sin(ang)
    cosf = jnp.concatenate([cos, cos], axis=1)
    sins = jnp.concatenate([-sin, sin], axis=1)
    log_gamma = jnp.log1p(-jnp.exp2(-5.0 - jnp.arange(H, dtype=F32)))
    idx = jnp.arange(CHUNK)
    diff = idx[:, None] - idx[None, :]
    causal = diff >= 0
    decay = jnp.where(causal[None], jnp.exp(log_gamma[:, None, None] * jnp.where(causal, diff, 0)[None].astype(F32)), 0.0)
    zeta = jnp.exp(log_gamma[:, None] * (CHUNK - 1 - idx).astype(F32)[None, :])
    xi = jnp.exp(log_gamma[:, None] * (idx + 1).astype(F32)[None, :])
    gc = jnp.exp(log_gamma * CHUNK)
    lanes = (H, CHUNK, HEAD_DIM)
    return (cosf, sins, decay, jnp.broadcast_to(zeta[:, :, None], lanes), jnp.broadcast_to(xi[:, :, None], lanes),
            jnp.broadcast_to(gc[:, None, None], lanes))


def _rope(t, cos, sin_signed):
    return t * cos + pltpu.roll(t, HEAD_DIM // 2, 1) * sin_signed


def _rope_t(d, cos, sin_signed):
    return d * cos + pltpu.roll(d * sin_signed, HEAD_DIM // 2, 1)


def _ret_const_specs(H, DR):
    full = pl.BlockSpec((H, CHUNK, HEAD_DIM), lambda *_: (0, 0, 0))
    return [full, full, full, full, pl.BlockSpec((1, DR), lambda *_: (0, 0))]


def _ret_fwd(proj, y, tables, gnw, tb):
    S = proj.shape[0]
    DR = gnw.shape[1]
    H = DR // HEAD_DIM
    tb = _tile(S, tb, CHUNK)
    nc = tb // CHUNK
    cosf, sins, dm, zeta, xi, gc = tables
    scale = HEAD_DIM ** -0.5

    def body(qk_ref, vg_ref, cos_ref, sin_ref, dm_ref, zeta_ref, xi_ref, gc_ref, gnw_ref, y_in, y_ref, rprev_ref, r_s):
        del y_in
        i = pl.program_id(0)

        @pl.when(i == 0)
        def _():
            r_s[...] = jnp.zeros_like(r_s)

        def chunk(c, carry):
            rows = pl.ds(pl.multiple_of(c * CHUNK, CHUNK), CHUNK)
            cos = cos_ref[rows, :]
            sin = sin_ref[rows, :]
            for h in range(H):
                c0 = slice(h * HEAD_DIM, (h + 1) * HEAD_DIM)
                c1 = slice(DR + h * HEAD_DIM, DR + (h + 1) * HEAD_DIM)
                qh = _rope(qk_ref[rows, c0], cos, sin)
                kh = _rope(qk_ref[rows, c1], cos, sin) * scale
                vb = vg_ref[rows, c0].astype(BF16)
                gate = vg_ref[rows, c1]
                s = _dot(qh.astype(BF16), kh.astype(BF16), NT) * dm_ref[h]
                rp = r_s[h]
                rpb = rp.astype(BF16)
                rprev_ref[c, h] = rpb
                o = _dot(s.astype(BF16), vb) + _dot((qh * xi_ref[h]).astype(BF16), rpb)
                r_s[h] = rp * gc_ref[h] + _dot((kh * zeta_ref[h]).astype(BF16), vb, TN)
                mu = jnp.mean(o, axis=-1, keepdims=True)
                oc = o - mu
                var = jnp.mean(oc * oc, axis=-1, keepdims=True)
                on = oc * lax.rsqrt(var + EPS) * gnw_ref[:, c0]
                y_ref[rows, c0] = (gate * jax.nn.sigmoid(gate) * on).astype(BF16)
            return carry

        lax.fori_loop(0, nc, chunk, 0)

    return pl.pallas_call(
        body, grid=(S // tb,),
        in_specs=[pl.BlockSpec((tb, 2 * DR), lambda i: (i, 1)), pl.BlockSpec((tb, 2 * DR), lambda i: (i, 2)),
                  pl.BlockSpec((tb, HEAD_DIM), lambda i: (i, 0)), pl.BlockSpec((tb, HEAD_DIM), lambda i: (i, 0))]
        + _ret_const_specs(H, DR) + [pl.BlockSpec(memory_space=pl.ANY)],
        out_specs=[pl.BlockSpec((tb, DR), lambda i: (i, 1)),
                   pl.BlockSpec((nc, H, CHUNK, HEAD_DIM), lambda i: (i, 0, 0, 0))],
        out_shape=[jax.ShapeDtypeStruct(y.shape, BF16), jax.ShapeDtypeStruct((S // CHUNK, H, CHUNK, HEAD_DIM), BF16)],
        scratch_shapes=[pltpu.VMEM((H, CHUNK, HEAD_DIM), F32)],
        input_output_aliases={9: 0},
        name="ret_fwd", compiler_params=_params(("arbitrary",)))(proj, proj, cosf, sins, dm, zeta, xi, gc, gnw, y)


def _ret_bwd(proj, rprev, dy, dproj, tables, gnw, tb):
    S = proj.shape[0]
    DR = gnw.shape[1]
    H = DR // HEAD_DIM
    tb = _tile(S, tb, CHUNK)
    nc = tb // CHUNK
    nt = S // tb
    cosf, sins, dm, zeta, xi, gc = tables
    scale = HEAD_DIM ** -0.5

    def body(qk_ref, vg_ref, cos_ref, sin_ref, dm_ref, zeta_ref, xi_ref, gc_ref, gnw_ref, rprev_ref, dy_ref, dp_in,
             dp_ref, dgn_ref, dr_s, dvg_s):
        del dp_in
        i = pl.program_id(0)
        second = pl.program_id(1) == 1

        @pl.when(jnp.logical_and(i == 0, jnp.logical_not(second)))
        def _():
            dr_s[...] = jnp.zeros_like(dr_s)
            dgn_ref[...] = jnp.zeros_like(dgn_ref)

        @pl.when(second)
        def _():
            dp_ref[...] = dvg_s[...]

        def chunk(cc, carry):
            c = nc - 1 - cc
            rows = pl.ds(pl.multiple_of(c * CHUNK, CHUNK), CHUNK)
            cos = cos_ref[rows, :]
            sin = sin_ref[rows, :]
            for h in range(H):
                c0 = slice(h * HEAD_DIM, (h + 1) * HEAD_DIM)
                c1 = slice(DR + h * HEAD_DIM, DR + (h + 1) * HEAD_DIM)
                qh = _rope(qk_ref[rows, c0], cos, sin)
                kh = _rope(qk_ref[rows, c1], cos, sin) * scale
                qb = qh.astype(BF16)
                kb = kh.astype(BF16)
                vb = vg_ref[rows, c0].astype(BF16)
                gate = vg_ref[rows, c1]
                dmh = dm_ref[h]
                xih = xi_ref[h]
                zetah = zeta_ref[h]
                sb = (_dot(qb, kb, NT) * dmh).astype(BF16)
                rpb = rprev_ref[c, h]
                qx = (qh * xih).astype(BF16)
                kz = (kh * zetah).astype(BF16)
                o = _dot(sb, vb) + _dot(qx, rpb)
                mu = jnp.mean(o, axis=-1, keepdims=True)
                oc = o - mu
                rstd = lax.rsqrt(jnp.mean(oc * oc, axis=-1, keepdims=True) + EPS)
                ohat = oc * rstd
                gw = gnw_ref[:, c0]
                sg = jax.nn.sigmoid(gate)
                dyv = dy_ref[rows, c0]
                dgate = dyv * (ohat * gw) * (sg * (1.0 + gate * (1.0 - sg)))
                don = dyv * (gate * sg)
                dgn_ref[:, c0] += jnp.sum(don * ohat, axis=0, keepdims=True)
                dohat = don * gw
                do = rstd * (dohat - jnp.mean(dohat, axis=-1, keepdims=True)
                             - ohat * jnp.mean(dohat * ohat, axis=-1, keepdims=True))
                dob = do.astype(BF16)
                drh = dr_s[h]
                drb = drh.astype(BF16)
                dv = _dot(sb, dob, TN) + _dot(kz, drb)
                dsb = (_dot(dob, vb, NT) * dmh).astype(BF16)
                dqh = _dot(dsb, kb) + _dot(dob, rpb, NT) * xih
                dkh = _dot(dsb, qb, TN) + _dot(vb, drb, NT) * zetah
                dr_s[h] = drh * gc_ref[h] + _dot(qx, dob, TN)
                dp_ref[rows, c0] = _rope_t(dqh, cos, sin).astype(BF16)
                dp_ref[rows, c1] = _rope_t(dkh * scale, cos, sin).astype(BF16)
                dvg_s[rows, c0] = dv.astype(BF16)
                dvg_s[rows, c1] = dgate.astype(BF16)
            return carry

        @pl.when(jnp.logical_not(second))
        def _():
            lax.fori_loop(0, nc, chunk, 0)

    rev = lambda i: nt - 1 - i
    return pl.pallas_call(
        body, grid=(nt, 2),
        in_specs=[pl.BlockSpec((tb, 2 * DR), lambda i, j: (rev(i), 1)), pl.BlockSpec((tb, 2 * DR), lambda i, j: (rev(i), 2)),
                  pl.BlockSpec((tb, HEAD_DIM), lambda i, j: (rev(i), 0)), pl.BlockSpec((tb, HEAD_DIM), lambda i, j: (rev(i), 0))]
        + _ret_const_specs(H, DR)
        + [pl.BlockSpec((nc, H, CHUNK, HEAD_DIM), lambda i, j: (rev(i), 0, 0, 0)),
           pl.BlockSpec((tb, DR), lambda i, j: (rev(i), 1)), pl.BlockSpec(memory_space=pl.ANY)],
        out_specs=[pl.BlockSpec((tb, 2 * DR), lambda i, j: (rev(i), 1 + j)), pl.BlockSpec((1, DR), lambda i, j: (0, 0))],
        out_shape=[jax.ShapeDtypeStruct(dproj.shape, BF16), jax.ShapeDtypeStruct((1, DR), F32)],
        scratch_shapes=[pltpu.VMEM((H, CHUNK, HEAD_DIM), F32), pltpu.VMEM((tb, 2 * DR), BF16)],
        input_output_aliases={11: 0},
        name="ret_bwd", compiler_params=_params(("arbitrary", "arbitrary")))(
            proj, proj, cosf, sins, dm, zeta, xi, gc, gnw, rprev, dy, dproj)


def _layer_step(x, tgt, ln1_w, w_in_st, conv_w, conv_b, wa, ba, wx, bx, lam, gnw, w_out, ln2_w, wg_st, wu_st, wd, fw):
    S, D = x.shape
    DL = conv_b.shape[1]
    DR = gnw.shape[1]
    assert DL == DR and DL % HEAD_DIM == 0 and S % CHUNK == 0
    H = DR // HEAD_DIM
    d_mix = DL + DR
    n_in = w_in_st.shape[2]
    n_ff = wg_st.shape[2]
    tables = _ret_tables(S, H)
    wab, wxb = wa.astype(BF16), wx.astype(BF16)
    TM = 512

    u1 = _rms_fwd("rms1", x, ln1_w, TM)
    proj = _mm_nn_stacked("proj", u1, w_in_st, F32, TM)
    hs, y = _lru_fwd(proj, conv_w, conv_b, wab, ba, wxb, bx, lam, 128, d_mix)
    y, rprev = _ret_fwd(proj, y, tables, gnw, 256)
    h1 = _mm_nn("out_proj", y, w_out, x, F32, TM, 1024, d_mix)
    u2 = _rms_fwd("rms2", h1, ln2_w, TM)
    gt, up, ff = _ffn_gate_up(u2, wg_st, wu_st, TM)
    h2 = _mm_nn("ffn_down", ff, wd, h1, F32, TM, 1024, 4 * n_ff)
    dh2, dh2b, d_fw, loss = _loss_head(h2, tgt, fw.reshape(1, D), TM)

    TK = 1024
    g_wd = _mm_tn("g_w_down", ff, dh2b, n_ff, 1024, TK)
    dgt, dup = _ffn_gate_up_bwd(dh2b, wd, gt, up, TM, n_ff)
    g_wg = _mm_tn("g_w_gate", u2, dgt, 1024, None, TK, stacked_cols=n_ff)
    g_wu = _mm_tn("g_w_up", u2, dup, 1024, None, TK, stacked_cols=n_ff)
    du2 = _mm_nt_stacked("d_u2", [dgt, dup], [wg_st, wu_st], F32, 256, 1024)
    dh1, dh1b, d_ln2 = _rms_bwd("rms2_bwd", h1, ln2_w, du2, dh2, TM, True)
    dy = _mm_nt("d_y", dh1b, w_out, F32, TM, 1024)
    g_wout = _mm_tn("g_w_out", y, dh1b, 1024, 1024, TK)
    dproj, d_cw, d_cb, d_wa, d_ba, d_wx, d_bx, d_lam = _lru_bwd(proj, hs, dy, conv_w, conv_b, wab, ba, wxb, bx, lam, 128)
    dproj, d_gn = _ret_bwd(proj, rprev, dy, dproj, tables, gnw, 256)
    g_win = _mm_tn("g_w_in", u1, dproj, 1024, None, TK, stacked_cols=n_in)
    du1 = _mm_nt_stacked("d_u1", [dproj], [w_in_st], F32, 256, D)
    gx, d_ln1 = _rms_bwd("rms1_bwd", x, ln1_w, du1, dh1, TM, False)

    big = dict(w_in=g_win, w_out=g_wout.reshape(4, d_mix // 4, D), w_ffn_gate=g_wg, w_ffn_up=g_wu,
               w_ffn_down=g_wd.reshape(4, n_ff, D))
    small = dict(ln1_w=d_ln1, conv_w=d_cw, conv_b=d_cb, gate_a_w=d_wa, gate_a_b=d_ba, gate_x_w=d_wx, gate_x_b=d_bx,
                 lru_lambda=d_lam, ret_gn_w=d_gn, ln2_w=d_ln2, final_norm_w=d_fw)
    return loss, gx, big, small


HBM_SPEC = pl.BlockSpec(memory_space=pl.ANY)


def _place():
    x, y, c = lax.axis_index("x"), lax.axis_index("y"), lax.axis_index("c")
    chips = [(1 - x, y), (x, 1 - y), (1 - x, 1 - y)]
    return x, y, c, chips


def _own_slab(name, shard, place):
    R, C = shard.shape
    tr = _row_tile(R, C)
    return _ew("cast_" + name, lambda a: (a,), [(shard, pl.BlockSpec((tr, C), lambda i, p: (i, 0)))],
               [(jax.ShapeDtypeStruct((4, R, C), BF16), pl.BlockSpec((None, tr, C), lambda i, p: (p[1], i, 0)))],
               (R // tr,), sp=place)[0]


def _gather_weights(stacked, conv_w):
    nw = len(stacked)

    def body(*refs):
        cin = refs[nw]
        outs, cout = refs[nw + 1:2 * nw + 1], refs[2 * nw + 1]
        send_sems, recv_sems, local_sem = refs[2 * nw + 2:]
        x, y, c, chips = _place()
        me = 2 * x + y
        sibling = (x, y, 1 - c)

        def piece(w, chip, half):
            r2 = stacked[w].shape[1] // 2
            return outs[w].at[chip, pl.ds(half * r2, r2), :]

        def remote(k, src, dst, to):
            return pltpu.make_async_remote_copy(src_ref=src, dst_ref=dst, send_sem=send_sems.at[k], recv_sem=recv_sems.at[k],
                                                device_id=to, device_id_type=MESH)

        local = pltpu.make_async_copy(cin, cout.at[me], local_sem)
        local.start()
        started = []
        for w in range(nw):
            for j, chip in enumerate(chips):
                started.append(remote(6 * w + j, piece(w, me, c), piece(w, me, c), (*chip, c)))
                started[-1].start()
        for j, chip in enumerate(chips):
            started.append(remote(6 * nw + j, cin, cout.at[me], (*chip, c)))
            started[-1].start()
        for w in range(nw):
            for j, (cx, cy) in enumerate(chips):
                got = piece(w, 2 * cx + cy, c)
                remote(6 * w + j, got, got, sibling).wait_recv()
                started.append(remote(6 * w + 3 + j, got, got, sibling))
                started[-1].start()
        for w in range(nw):
            for j, (cx, cy) in enumerate(chips):
                got = piece(w, 2 * cx + cy, 1 - c)
                remote(6 * w + 3 + j, got, got, sibling).wait_recv()
        for j, (cx, cy) in enumerate(chips):
            got = cout.at[2 * cx + cy]
            remote(6 * nw + j, got, got, sibling).wait_recv()
        for cp in started:
            cp.wait_send()
        local.wait()

    out_shape = [jax.ShapeDtypeStruct(s.shape, s.dtype) for s in stacked] + \
                [jax.ShapeDtypeStruct((4,) + conv_w.shape, conv_w.dtype)]
    n_sem = 6 * nw + 3
    return pl.pallas_call(
        body, in_specs=[HBM_SPEC] * (nw + 1), out_specs=[HBM_SPEC] * (nw + 1), out_shape=out_shape,
        scratch_shapes=[pltpu.SemaphoreType.DMA((n_sem,)), pltpu.SemaphoreType.DMA((n_sem,)), pltpu.SemaphoreType.DMA],
        input_output_aliases={w: w for w in range(nw)}, name="gather_weights")(*stacked, conv_w)


def _pair_exchange(grads):
    nw = len(grads)

    def body(*refs):
        ins, outs = refs[:nw], refs[nw:2 * nw]
        send_sems, recv_sems = refs[2 * nw:]
        x, y, c, _ = _place()
        cps = []
        for w in range(nw):
            r2 = grads[w].shape[1] // 2
            cps.append(pltpu.make_async_remote_copy(
                src_ref=ins[w].at[:, pl.ds((1 - c) * r2, r2), :], dst_ref=outs[w], send_sem=send_sems.at[w],
                recv_sem=recv_sems.at[w], device_id=(x, y, 1 - c), device_id_type=MESH))
            cps[-1].start()
        for cp in cps:
            cp.wait()

    out_shape = [jax.ShapeDtypeStruct((g.shape[0], g.shape[1] // 2, g.shape[2]), g.dtype) for g in grads]
    return pl.pallas_call(
        body, in_specs=[HBM_SPEC] * nw, out_specs=[HBM_SPEC] * nw, out_shape=out_shape,
        scratch_shapes=[pltpu.SemaphoreType.DMA((nw,)), pltpu.SemaphoreType.DMA((nw,))], name="rs_pair_exchange")(*grads)


def _chip_exchange(parts):
    nw = len(parts)

    def body(*refs):
        ins, outs = refs[:nw], refs[nw:2 * nw]
        send_sems, recv_sems = refs[2 * nw:]
        x, y, c, chips = _place()
        cps = []
        for w in range(nw):
            for j, (cx, cy) in enumerate(chips):
                cps.append(pltpu.make_async_remote_copy(
                    src_ref=ins[w].at[2 * cx + cy], dst_ref=outs[w].at[j], send_sem=send_sems.at[3 * w + j],
                    recv_sem=recv_sems.at[3 * w + j], device_id=(cx, cy, c), device_id_type=MESH))
                cps[-1].start()
        for cp in cps:
            cp.wait()

    out_shape = [jax.ShapeDtypeStruct((3,) + p.shape[1:], p.dtype) for p in parts]
    return pl.pallas_call(
        body, in_specs=[HBM_SPEC] * nw, out_specs=[HBM_SPEC] * nw, out_shape=out_shape,
        scratch_shapes=[pltpu.SemaphoreType.DMA((3 * nw,)), pltpu.SemaphoreType.DMA((3 * nw,))], name="rs_chip_exchange")(*parts)


def _pair_share(slots):
    nw = len(slots)

    def body(*refs):
        outs = refs[nw:2 * nw]
        send_sems, recv_sems = refs[2 * nw:]
        x, y, c, _ = _place()
        cps = []
        for w in range(nw):
            cps.append(pltpu.make_async_remote_copy(
                src_ref=outs[w].at[c], dst_ref=outs[w].at[c], send_sem=send_sems.at[w], recv_sem=recv_sems.at[w],
                device_id=(x, y, 1 - c), device_id_type=MESH))
            cps[-1].start()
        for w in range(nw):
            cps[w].wait_send()
            pltpu.make_async_remote_copy(
                src_ref=outs[w].at[1 - c], dst_ref=outs[w].at[1 - c], send_sem=send_sems.at[w], recv_sem=recv_sems.at[w],
                device_id=(x, y, 1 - c), device_id_type=MESH).wait_recv()

    out_shape = [jax.ShapeDtypeStruct(h.shape, h.dtype) for h in slots]
    return pl.pallas_call(
        body, in_specs=[HBM_SPEC] * nw, out_specs=[HBM_SPEC] * nw, out_shape=out_shape,
        scratch_shapes=[pltpu.SemaphoreType.DMA((nw,)), pltpu.SemaphoreType.DMA((nw,))],
        input_output_aliases={w: w for w in range(nw)}, name="rs_pair_share")(*slots)


def _gather_small(sm):
    flips = [(fx, fy, fc) for fx in (0, 1) for fy in (0, 1) for fc in (0, 1)][1:]

    def body(in_ref, out_ref, send_sems, recv_sems, local_sem):
        x, y, c, _ = _place()
        me = 4 * x + 2 * y + c
        local = pltpu.make_async_copy(in_ref, out_ref.at[me], local_sem)
        local.start()
        peers = [(1 - x if fx else x, 1 - y if fy else y, 1 - c if fc else c) for fx, fy, fc in flips]
        cps = []
        for k, peer in enumerate(peers):
            cps.append(pltpu.make_async_remote_copy(src_ref=in_ref, dst_ref=out_ref.at[me], send_sem=send_sems.at[k],
                                                    recv_sem=recv_sems.at[k], device_id=peer, device_id_type=MESH))
            cps[-1].start()
        for k, (px, py, pc) in enumerate(peers):
            cps[k].wait_send()
            pltpu.make_async_remote_copy(src_ref=in_ref, dst_ref=out_ref.at[4 * px + 2 * py + pc], send_sem=send_sems.at[k],
                                         recv_sem=recv_sems.at[k], device_id=(px, py, pc), device_id_type=MESH).wait_recv()
        local.wait()

    return pl.pallas_call(
        body, in_specs=[HBM_SPEC], out_specs=HBM_SPEC, out_shape=jax.ShapeDtypeStruct((8,) + sm.shape, sm.dtype),
        scratch_shapes=[pltpu.SemaphoreType.DMA((7,)), pltpu.SemaphoreType.DMA((7,)), pltpu.SemaphoreType.DMA],
        name="gather_small")(sm)


def _adamw(w, g, m, v):
    m = ADAM_B1 * m + (1.0 - ADAM_B1) * g
    v = ADAM_B2 * v + (1.0 - ADAM_B2) * (g * g)
    m_hat = m / (1.0 - ADAM_B1 ** ADAM_STEP)
    v_hat = v / (1.0 - ADAM_B2 ** ADAM_STEP)
    delta = -ADAM_LR * (m_hat / (jnp.sqrt(v_hat) + ADAM_EPS) + ADAM_WD * w)
    return delta, m, v


def _adamw_call(name, w, g, m, v):
    R, C = w.shape
    tr = _row_tile(R, C, 1024 * 1024)
    row = pl.BlockSpec((tr, C), lambda i: (i, 0))
    o = jax.ShapeDtypeStruct((R, C), F32)
    return _ew(name, _adamw, [(w, row), (g, row), (m, row), (v, row)], [(o, row), (o, row), (o, row)], (R // tr,))


def _reduce_big(grads, place):
    names = list(grads)
    gl = [grads[n] for n in names]
    got_pair = _pair_exchange(gl)
    parts, halves = [], []
    for n, g, ra in zip(names, gl, got_pair):
        _, R, C = g.shape
        r2 = R // 2
        tr = _row_tile(r2, C)
        nb = r2 // tr
        own = pl.BlockSpec((None, tr, C), lambda j, i, p: (j, p[0] * nb + i, 0))
        blk = pl.BlockSpec((None, tr, C), lambda j, i, p: (j, i, 0))
        parts.append(_ew("rs_pair_sum_" + n, lambda a, b: (a + b,), [(g, own), (ra, blk)],
                         [(jax.ShapeDtypeStruct((4, r2, C), BF16), blk)], (4, nb), sp=place)[0])
    got_chip = _chip_exchange(parts)
    for n, g, ra, rb in zip(names, gl, got_pair, got_chip):
        _, R, C = g.shape
        r2 = R // 2
        tr = _row_tile(r2, C)
        nb = r2 // tr
        own = pl.BlockSpec((None, tr, C), lambda i, p: (p[1], p[0] * nb + i, 0))
        mine = pl.BlockSpec((None, tr, C), lambda i, p: (p[1], i, 0))
        src = [pl.BlockSpec((None, tr, C), functools.partial(lambda i, p, j: (j, i, 0), j=j)) for j in range(3)]
        out = pl.BlockSpec((None, tr, C), lambda i, p: (p[0], i, 0))

        def total(a, b, r0, r1, r2_):
            return ((((a + b) + r0.astype(F32)) + r1.astype(F32)) + r2_.astype(F32),)

        halves.append(_ew("rs_chip_sum_" + n, total, [(g, own), (ra, mine), (rb, src[0]), (rb, src[1]), (rb, src[2])],
                          [(jax.ShapeDtypeStruct((2, r2, C), F32), out)], (nb,), sp=place)[0])
    full = _pair_share(halves)
    return {n: f.reshape(f.shape[0] * f.shape[1], f.shape[2]) for n, f in zip(names, full)}


def _pack(arrays):
    rows, offs, pos = [], [], 0
    for a in arrays:
        flat = a.reshape(-1)
        n = -(-flat.shape[0] // (8 * LANES)) * (8 * LANES)
        if n != flat.shape[0]:
            flat = jnp.pad(flat, (0, n - flat.shape[0]))
        rows.append(flat.reshape(-1, LANES))
        offs.append(pos)
        pos += n // LANES
    return jnp.concatenate(rows, axis=0), offs


def _unpack(packed, offs, shapes):
    out = []
    for off, shp in zip(offs, shapes):
        n = 1
        for s in shp:
            n *= s
        out.append(packed[off:off + -(-n // LANES)].reshape(-1)[:n].reshape(shp))
    return out


def _sum8(gathered):
    _, R, C = gathered.shape
    tr = _row_tile(R, C, 256 * 1024)
    specs = [pl.BlockSpec((None, tr, C), functools.partial(lambda i, d: (d, i, 0), d=d)) for d in range(8)]

    def fn(*parts):
        t = parts[0]
        for p in parts[1:]:
            t = t + p
        return (t,)

    return _ew("small_sum", fn, [(gathered, s) for s in specs],
               [(jax.ShapeDtypeStruct((R, C), F32), pl.BlockSpec((tr, C), lambda i: (i, 0)))], (R // tr,))[0]


BIG = ("w_in", "w_out", "w_ffn_gate", "w_ffn_up", "w_ffn_down")
SMALL = ("ln1_w", "conv_w", "conv_b", "gate_a_w", "gate_a_b", "gate_x_w", "gate_x_b", "lru_lambda", "ret_gn_w", "ln2_w",
         "final_norm_w")
WEIGHTS = ("ln1_w", "w_in", "conv_w", "conv_b", "gate_a_w", "gate_a_b", "gate_x_w", "gate_x_b", "lru_lambda", "ret_gn_w",
           "w_out", "ln2_w", "w_ffn_gate", "w_ffn_up", "w_ffn_down", "final_norm_w")


def kernel(x, ln1_w, w_in, conv_w, conv_b, gate_a_w, gate_a_b, gate_x_w, gate_x_b, lru_lambda, ret_gn_w, w_out, ln2_w, w_ffn_gate, w_ffn_up, w_ffn_down, final_norm_w, loss_target, m_ln1_w, m_w_in, m_conv_w, m_conv_b, m_gate_a_w, m_gate_a_b, m_gate_x_w, m_gate_x_b, m_lru_lambda, m_ret_gn_w, m_w_out, m_ln2_w, m_w_ffn_gate, m_w_ffn_up, m_w_ffn_down, m_final_norm_w, v_ln1_w, v_w_in, v_conv_w, v_conv_b, v_gate_a_w, v_gate_a_b, v_gate_x_w, v_gate_x_b, v_lru_lambda, v_ret_gn_w, v_w_out, v_ln2_w, v_w_ffn_gate, v_w_ffn_up, v_w_ffn_down, v_final_norm_w):
    w = dict(ln1_w=ln1_w, w_in=w_in, conv_w=conv_w, conv_b=conv_b, gate_a_w=gate_a_w, gate_a_b=gate_a_b, gate_x_w=gate_x_w,
             gate_x_b=gate_x_b, lru_lambda=lru_lambda, ret_gn_w=ret_gn_w, w_out=w_out, ln2_w=ln2_w, w_ffn_gate=w_ffn_gate,
             w_ffn_up=w_ffn_up, w_ffn_down=w_ffn_down, final_norm_w=final_norm_w)
    m = dict(ln1_w=m_ln1_w, w_in=m_w_in, conv_w=m_conv_w, conv_b=m_conv_b, gate_a_w=m_gate_a_w, gate_a_b=m_gate_a_b,
             gate_x_w=m_gate_x_w, gate_x_b=m_gate_x_b, lru_lambda=m_lru_lambda, ret_gn_w=m_ret_gn_w, w_out=m_w_out,
             ln2_w=m_ln2_w, w_ffn_gate=m_w_ffn_gate, w_ffn_up=m_w_ffn_up, w_ffn_down=m_w_ffn_down,
             final_norm_w=m_final_norm_w)
    v = dict(ln1_w=v_ln1_w, w_in=v_w_in, conv_w=v_conv_w, conv_b=v_conv_b, gate_a_w=v_gate_a_w, gate_a_b=v_gate_a_b,
             gate_x_w=v_gate_x_w, gate_x_b=v_gate_x_b, lru_lambda=v_lru_lambda, ret_gn_w=v_ret_gn_w, w_out=v_w_out,
             ln2_w=v_ln2_w, w_ffn_gate=v_w_ffn_gate, w_ffn_up=v_w_ffn_up, w_ffn_down=v_w_ffn_down,
             final_norm_w=v_final_norm_w)
    S, D = x.shape[1], x.shape[2]
    chip = 2 * lax.axis_index("x") + lax.axis_index("y")
    place = jnp.stack([lax.axis_index("c"), chip]).astype(jnp.int32)

    *stk, conv_st = _gather_weights([_own_slab(n, w[n][0], place) for n in BIG], w["conv_w"][0])
    w_in_st, w_out_st, wg_st, wu_st, wd_st = stk
    cw_cols = conv_st.shape[2]
    conv_full = jnp.transpose(conv_st, (1, 0, 2)).reshape(CONV_W, 4 * cw_cols)
    d_mix = w_out_st.shape[0] * w_out_st.shape[1]
    n_ffs = wd_st.shape[1]

    loss, gx, big, small = _layer_step(
        x[0], loss_target[0], ln1_w, w_in_st, conv_full, conv_b, gate_a_w[0], gate_a_b, gate_x_w[0], gate_x_b, lru_lambda,
        ret_gn_w, w_out_st.reshape(d_mix, D), ln2_w, wg_st, wu_st, wd_st.reshape(4 * n_ffs, D), final_norm_w)

    g_big = _reduce_big(big, place)
    packed, offs = _pack([small[n] for n in SMALL] + [loss])
    total = _sum8(_gather_small(packed))
    red = _unpack(total, offs, [small[n].shape for n in SMALL] + [(1, LANES)])
    g = dict(zip(SMALL, red[:-1]))
    loss_out = red[-1][0, 0]
    g["conv_w"] = lax.dynamic_slice(g["conv_w"], (0, chip * cw_cols), (CONV_W, cw_cols))

    grad, delta, new_m, new_v = {}, {}, {}, {}
    for n in BIG:
        shp = w[n].shape
        w2, m2, v2 = (t[n].reshape(shp[1], shp[2]) for t in (w, m, v))
        d_, m_, v_ = _adamw_call("adamw_" + n, w2, g_big[n], m2, v2)
        grad[n], delta[n], new_m[n], new_v[n] = (t.reshape(shp) for t in (g_big[n], d_, m_, v_))
    packs = [_pack([t[n] for n in SMALL])[0] for t in (w, m, v)]
    gp, offs2 = _pack([g[n] for n in SMALL])
    outs = _adamw_call("adamw_small", packs[0], gp, packs[1], packs[2])
    shapes = [w[n].shape for n in SMALL]
    for dst, arr in zip((delta, new_m, new_v), outs):
        dst.update(zip(SMALL, _unpack(arr, offs2, shapes)))
    for n in SMALL:
        grad[n] = g[n].reshape(w[n].shape)

    return (loss_out, gx.reshape(x.shape), *[grad[n] for n in WEIGHTS], *[delta[n] for n in WEIGHTS],
            *[new_m[n] for n in WEIGHTS], *[new_v[n] for n in WEIGHTS])
```

```python
import functools

import jax
import jax.numpy as jnp
from jax import lax
from jax.experimental import pallas as pl
from jax.experimental.pallas import tpu as pltpu

F32 = jnp.float32
BF16 = jnp.bfloat16
MESH = pl.DeviceIdType.MESH

EPS = 1e-6
LRU_C = 8.0
ROPE_BASE = 10000.0
CHUNK = 128
HEAD_DIM = 128
CONV_W = 4
ADAM_LR = 0.001
ADAM_B1 = 0.9
ADAM_B2 = 0.999
ADAM_EPS = 1e-08
ADAM_WD = 0.01
ADAM_STEP = 10

V7X_VMEM_BYTES = 64 * 1024 * 1024
VMEM_LIMIT = V7X_VMEM_BYTES - 8 * 1024 * 1024
LANES = 128
SUBLANES_16BIT = 16

NN = (((1,), (0,)), ((), ()))
NT = (((1,), (1,)), ((), ()))
TN = (((0,), (0,)), ((), ()))


def _dot(a, b, dims=NN):
    return lax.dot_general(a, b, dims, preferred_element_type=F32)


def _tile(n, pref, mult=SUBLANES_16BIT):
    best = None
    t = mult
    while t <= min(n, pref):
        if n % t == 0:
            best = t
        t += mult
    return best if best is not None else n


def _row_tile(rows, cols, budget_bytes=2 * 1024 * 1024):
    return _tile(rows, max(SUBLANES_16BIT, budget_bytes // (cols * 4)))


def _params(sem):
    return pltpu.CompilerParams(dimension_semantics=sem, vmem_limit_bytes=VMEM_LIMIT)


HBM_SPEC = pl.BlockSpec(memory_space=pl.ANY)


class _Comm:
    def __init__(self, ins, outs, aliases, n_sem, start, finish):
        self.ins, self.outs, self.aliases, self.n_sem, self.start, self.finish = ins, outs, aliases, n_sem, start, finish


def _merge(tasks):
    ins, outs, aliases, plans, n_sem = [], [], {}, [], 0
    for t in tasks:
        i0, o0 = len(ins), len(outs)
        plans.append((t, i0, o0, n_sem))
        ins += t.ins
        outs += t.outs
        aliases.update({i0 + a: o0 + b for a, b in t.aliases.items()})
        n_sem += t.n_sem

    def run(which):
        def go(cins, couts, ssem, rsem, base):
            for t, i0, o0, s0 in plans:
                getattr(t, which)(cins[i0:i0 + len(t.ins)], couts[o0:o0 + len(t.outs)], ssem, rsem, base + s0)
        return go

    return _Comm(ins, outs, aliases, n_sem, run("start"), run("finish"))


def _pcall(body, *, name, grid, in_specs, out_specs, out_shape, operands, scratch_shapes=(), aliases=None, comm=None):
    n_in, n_out, n_scr = len(operands), len(out_shape), len(scratch_shapes)
    aliases = dict(aliases or {})
    params = _params(("arbitrary",) * len(grid))
    if comm is None:
        return pl.pallas_call(body, grid=grid, in_specs=list(in_specs), out_specs=list(out_specs), out_shape=list(out_shape),
                              scratch_shapes=list(scratch_shapes), input_output_aliases=aliases, name=name,
                              compiler_params=params)(*operands)
    nci, nco = len(comm.ins), len(comm.outs)

    def wrapped(*refs):
        ins, cins = refs[:n_in], refs[n_in:n_in + nci]
        o0 = n_in + nci
        outs, couts = refs[o0:o0 + n_out], refs[o0 + n_out:o0 + n_out + nco]
        s0 = o0 + n_out + nco
        scr, (ssem, rsem) = refs[s0:s0 + n_scr], refs[s0 + n_scr:]
        ids = [pl.program_id(a) for a in range(len(grid))]
        first = functools.reduce(jnp.logical_and, [i == 0 for i in ids])
        last = functools.reduce(jnp.logical_and, [i == g - 1 for i, g in zip(ids, grid)])

        @pl.when(first)
        def _():
            comm.start(cins, couts, ssem, rsem, 0)

        body(*ins, *outs, *scr)

        @pl.when(last)
        def _():
            comm.finish(cins, couts, ssem, rsem, 0)

    aliases.update({n_in + a: n_out + b for a, b in comm.aliases.items()})
    res = pl.pallas_call(
        wrapped, grid=grid, in_specs=list(in_specs) + [HBM_SPEC] * nci, out_specs=list(out_specs) + [HBM_SPEC] * nco,
        out_shape=list(out_shape) + list(comm.outs),
        scratch_shapes=list(scratch_shapes) + [pltpu.SemaphoreType.DMA((comm.n_sem,)), pltpu.SemaphoreType.DMA((comm.n_sem,))],
        input_output_aliases=aliases, name=name, compiler_params=params)(*operands, *comm.ins)
    return res[:n_out], res[n_out:]


def _ew(name, fn, ins, outs, grid, sp=None):
    n_in = len(ins)

    def body(*refs):
        if sp is not None:
            refs = refs[1:]
        vals = [r[...] for r in refs[:n_in]]
        res = fn(*vals)
        for o_ref, v in zip(refs[n_in:], res):
            o_ref[...] = v.astype(o_ref.dtype)

    in_specs = [s for _, s in ins]
    out_specs = [s for _, s in outs]
    out_shape = [s for s, _ in outs]
    sem = ("arbitrary",) * len(grid)
    if sp is None:
        return pl.pallas_call(body, grid=grid, in_specs=in_specs, out_specs=out_specs, out_shape=out_shape,
                              name=name, compiler_params=_params(sem))(*[a for a, _ in ins])
    gs = pltpu.PrefetchScalarGridSpec(num_scalar_prefetch=1, grid=grid, in_specs=in_specs, out_specs=out_specs)
    return pl.pallas_call(body, grid_spec=gs, out_shape=out_shape, name=name,
                          compiler_params=_params(sem))(sp, *[a for a, _ in ins])


def _matmul(name, pairs, dims, grid, out_shape, out_spec, acc_shape, res=None, comm=None):
    n = len(pairs)
    nk = grid[2]

    def body(*refs):
        ab = refs[:2 * n]
        pos = 2 * n
        res_ref = None
        if res is not None:
            res_ref = refs[pos]
            pos += 1
        o_ref = refs[pos]
        acc_ref = refs[pos + 1] if nk > 1 else None

        def partial():
            t = None
            for p in range(n):
                d = _dot(ab[2 * p][...], ab[2 * p + 1][...], dims)
                t = d if t is None else t + d
            return t

        def finish(t):
            if res_ref is not None:
                t = t + res_ref[...]
            o_ref[...] = t.astype(o_ref.dtype)

        if nk == 1:
            finish(partial())
        else:
            k = pl.program_id(2)

            @pl.when(k == 0)
            def _():
                acc_ref[...] = partial()

            @pl.when(k > 0)
            def _():
                acc_ref[...] += partial()

            @pl.when(k == nk - 1)
            def _():
                finish(acc_ref[...])

    operands, in_specs = [], []
    for a, a_spec, b, b_spec in pairs:
        operands += [a, b]
        in_specs += [a_spec, b_spec]
    if res is not None:
        operands.append(res[0])
        in_specs.append(res[1])
    scratch = [pltpu.VMEM(acc_shape, F32)] if nk > 1 else []
    res = _pcall(body, name=name, grid=grid, in_specs=in_specs, out_specs=[out_spec], out_shape=[out_shape],
                 operands=operands, scratch_shapes=scratch, comm=comm)
    return res[0] if comm is None else (res[0][0], res[1])


def _mm_nn_stacked(name, a, b_st, out_dtype, tm, comm=None):
    M, K = a.shape
    J, _, Nj = b_st.shape
    tm = _tile(M, tm)
    return _matmul(
        name, [(a, pl.BlockSpec((tm, K), lambda j, i, k: (i, 0)), b_st, pl.BlockSpec((None, K, Nj), lambda j, i, k: (j, 0, 0)))],
        NN, (J, M // tm, 1), jax.ShapeDtypeStruct((M, J * Nj), out_dtype), pl.BlockSpec((tm, Nj), lambda j, i, k: (i, j)), None,
        comm=comm)


def _mm_nn(name, a, b, res, out_dtype, tm, tn, tk, comm=None):
    M, K = a.shape
    N = b.shape[1]
    tm, tn, tk = _tile(M, tm), _tile(N, tn, LANES), _tile(K, tk, LANES)
    return _matmul(
        name, [(a, pl.BlockSpec((tm, tk), lambda j, i, k: (i, k)), b, pl.BlockSpec((tk, tn), lambda j, i, k: (k, j)))],
        NN, (N // tn, M // tm, K // tk), jax.ShapeDtypeStruct((M, N), out_dtype), pl.BlockSpec((tm, tn), lambda j, i, k: (i, j)),
        (tm, tn), res=(res, pl.BlockSpec((tm, tn), lambda j, i, k: (i, j))), comm=comm)


def _mm_nt(name, a, b, out_dtype, tm, tn):
    M, K = a.shape
    N = b.shape[0]
    tm, tn = _tile(M, tm), _tile(N, tn, LANES)
    return _matmul(
        name, [(a, pl.BlockSpec((tm, K), lambda j, i, k: (i, 0)), b, pl.BlockSpec((tn, K), lambda j, i, k: (j, 0)))],
        NT, (N // tn, M // tm, 1), jax.ShapeDtypeStruct((M, N), out_dtype), pl.BlockSpec((tm, tn), lambda j, i, k: (i, j)), None)


def _mm_nt_stacked(name, a_list, b_list, out_dtype, tm, tn, comm=None):
    n = len(a_list)
    M = a_list[0].shape[0]
    J, N, Nj = b_list[0].shape
    tm, tn = _tile(M, tm), _tile(N, tn, LANES)

    def body(*refs):
        o_ref = refs[2 * n]
        t = None
        for p in range(n):
            a_ref, b_ref = refs[p], refs[n + p]
            for s in range(J):
                d = _dot(a_ref[:, s * Nj:(s + 1) * Nj], b_ref[s], NT)
                t = d if t is None else t + d
        o_ref[...] = t.astype(o_ref.dtype)

    a_spec = pl.BlockSpec((tm, J * Nj), lambda j, i: (i, 0))
    b_spec = pl.BlockSpec((J, tn, Nj), lambda j, i: (0, j, 0), pipeline_mode=pl.Buffered(1))
    res = _pcall(body, name=name, grid=(N // tn, M // tm), in_specs=[a_spec] * n + [b_spec] * n,
                 out_specs=[pl.BlockSpec((tm, tn), lambda j, i: (i, j))], out_shape=[jax.ShapeDtypeStruct((M, N), out_dtype)],
                 operands=[*a_list, *b_list], comm=comm)
    return res[0] if comm is None else (res[0][0], res[1])


def _ffn_gate_up(u2, wg_st, wu_st, tm):
    S, D = u2.shape
    J, _, Nj = wg_st.shape
    tm = _tile(S, tm)

    def body(a_ref, wg_ref, wu_ref, gt_ref, up_ref, ff_ref):
        a = a_ref[...]
        g = _dot(a, wg_ref[...])
        u = _dot(a, wu_ref[...])
        gt_ref[...] = g.astype(BF16)
        up_ref[...] = u.astype(BF16)
        ff_ref[...] = (g * jax.nn.sigmoid(g) * u).astype(BF16)

    w_spec = pl.BlockSpec((None, D, Nj), lambda j, i: (j, 0, 0))
    o_spec = pl.BlockSpec((tm, Nj), lambda j, i: (i, j))
    o = jax.ShapeDtypeStruct((S, J * Nj), BF16)
    return _pcall(body, name="ffn_gate_up", grid=(J, S // tm),
                  in_specs=[pl.BlockSpec((tm, D), lambda j, i: (i, 0)), w_spec, w_spec],
                  out_specs=[o_spec, o_spec, o_spec], out_shape=[o, o, o], operands=[u2, wg_st, wu_st])


def _ffn_gate_up_bwd(dh2b, wd, gt, up, tm, tn, comm=None):
    S, D = dh2b.shape
    F = wd.shape[0]
    tm, tn = _tile(S, tm), _tile(F, tn, LANES)

    def body(a_ref, wd_ref, gt_ref, up_ref, dgt_ref, dup_ref):
        d = _dot(a_ref[...], wd_ref[...], NT)
        g = gt_ref[...].astype(F32)
        u = up_ref[...].astype(F32)
        sg = jax.nn.sigmoid(g)
        dgt_ref[...] = (d * u * (sg * (1.0 + g * (1.0 - sg)))).astype(BF16)
        dup_ref[...] = (d * (g * sg)).astype(BF16)

    blk = pl.BlockSpec((tm, tn), lambda j, i: (i, j))
    o = jax.ShapeDtypeStruct((S, F), BF16)
    return _pcall(body, name="ffn_gate_up_bwd", grid=(F // tn, S // tm),
                  in_specs=[pl.BlockSpec((tm, D), lambda j, i: (i, 0)), pl.BlockSpec((tn, D), lambda j, i: (j, 0)), blk, blk],
                  out_specs=[blk, blk], out_shape=[o, o], operands=[dh2b, wd, gt, up], comm=comm)


def _mm_tn(name, a, b, tmo, tn, tk, stacked_cols=None, comm=None):
    S, Mo = a.shape
    N = b.shape[1]
    tmo, tk = _tile(Mo, tmo, LANES), _tile(S, tk)
    if stacked_cols is None:
        tn = _tile(N, tn, LANES)
        out_shape = jax.ShapeDtypeStruct((Mo, N), F32)
        out_spec = pl.BlockSpec((tmo, tn), lambda i, j, k: (i, j))
    else:
        tn = stacked_cols
        out_shape = jax.ShapeDtypeStruct((N // tn, Mo, tn), F32)
        out_spec = pl.BlockSpec((None, tmo, tn), lambda i, j, k: (j, i, 0))
    return _matmul(
        name, [(a, pl.BlockSpec((tk, tmo), lambda i, j, k: (k, i)), b, pl.BlockSpec((tk, tn), lambda i, j, k: (k, j)))],
        TN, (Mo // tmo, N // tn, S // tk), out_shape, out_spec, (tmo, tn), comm=comm)


def _rms_fwd(name, x, w, tm):
    S, D = x.shape
    tm = _tile(S, tm)

    def fn(xv, wv):
        r = lax.rsqrt(jnp.mean(xv * xv, axis=-1, keepdims=True) + EPS)
        return ((xv * r) * wv,)

    row = pl.BlockSpec((tm, D), lambda i: (i, 0))
    return _ew(name, fn, [(x, row), (w, pl.BlockSpec((1, D), lambda i: (0, 0)))],
               [(jax.ShapeDtypeStruct((S, D), BF16), row)], (S // tm,))[0]


def _rms_bwd(name, x, w, dy, dres, tm, want_bf16, comm=None):
    S, D = x.shape
    tm = _tile(S, tm)

    def body(x_ref, w_ref, dy_ref, dres_ref, dx_ref, *rest):
        dw_ref = rest[-1]
        i = pl.program_id(0)

        @pl.when(i == 0)
        def _():
            dw_ref[...] = jnp.zeros_like(dw_ref)

        xv = x_ref[...]
        r = lax.rsqrt(jnp.mean(xv * xv, axis=-1, keepdims=True) + EPS)
        nv = xv * r
        dyv = dy_ref[...]
        dn = dyv * w_ref[...]
        dw_ref[...] += jnp.sum(dyv * nv, axis=0, keepdims=True)
        dx = dres_ref[...] + r * (dn - nv * jnp.mean(dn * nv, axis=-1, keepdims=True))
        dx_ref[...] = dx
        if want_bf16:
            rest[0][...] = dx.astype(BF16)

    row = pl.BlockSpec((tm, D), lambda i: (i, 0))
    vec = pl.BlockSpec((1, D), lambda i: (0, 0))
    out_shape = [jax.ShapeDtypeStruct((S, D), F32)] + ([jax.ShapeDtypeStruct((S, D), BF16)] if want_bf16 else []) + \
                [jax.ShapeDtypeStruct((1, D), F32)]
    out_specs = [row] + ([row] if want_bf16 else []) + [vec]
    return _pcall(body, name=name, grid=(S // tm,), in_specs=[row, vec, row, row], out_specs=out_specs, out_shape=out_shape,
                  operands=[x, w, dy, dres], comm=comm)


def _loss_head(h2, tgt, fw, tm):
    S, D = h2.shape
    tm = _tile(S, tm)

    def body(h_ref, t_ref, w_ref, dh_ref, dhb_ref, dw_ref, loss_ref):
        i = pl.program_id(0)

        @pl.when(i == 0)
        def _():
            dw_ref[...] = jnp.zeros_like(dw_ref)
            loss_ref[...] = jnp.zeros_like(loss_ref)

        hv = h_ref[...]
        wv = w_ref[...]
        r = lax.rsqrt(jnp.mean(hv * hv, axis=-1, keepdims=True) + EPS)
        nv = hv * r
        err = nv * wv - t_ref[...]
        row_loss = jnp.mean(err * err, axis=-1, keepdims=True)
        loss_ref[...] += 0.5 * jnp.sum(row_loss, axis=0, keepdims=True)
        dyo = err * (1.0 / D)
        dn = dyo * wv
        dw_ref[...] += jnp.sum(dyo * nv, axis=0, keepdims=True)
        dh = r * (dn - nv * jnp.mean(dn * nv, axis=-1, keepdims=True))
        dh_ref[...] = dh
        dhb_ref[...] = dh.astype(BF16)

    row = pl.BlockSpec((tm, D), lambda i: (i, 0))
    vec = pl.BlockSpec((1, D), lambda i: (0, 0))
    return _pcall(
        body, name="loss_head", grid=(S // tm,), in_specs=[row, row, vec],
        out_specs=[row, row, vec, pl.BlockSpec((1, LANES), lambda i: (0, 0))],
        out_shape=[jax.ShapeDtypeStruct((S, D), F32), jax.ShapeDtypeStruct((S, D), BF16),
                   jax.ShapeDtypeStruct((1, D), F32), jax.ShapeDtypeStruct((1, LANES), F32)],
        operands=[h2, tgt, fw])


def _shift_down(x, d, head8):
    r = pltpu.roll(x, d, 0)
    rh = pltpu.roll(head8, d, 0)
    row8 = lax.broadcasted_iota(jnp.int32, head8.shape, 0)
    top = jnp.where(row8 < d, rh, r[0:8])
    return jnp.concatenate([top, r[8:]], axis=0)


def _shift_up(x, d, tail8):
    n = x.shape[0]
    r = pltpu.roll(x, n - d, 0)
    rt = pltpu.roll(tail8, 8 - d, 0)
    row8 = lax.broadcasted_iota(jnp.int32, tail8.shape, 0)
    bot = jnp.where(row8 + d >= 8, rt, r[n - 8:n])
    return jnp.concatenate([r[:n - 8], bot], axis=0)


def _log_sigmoid(lam):
    z = jnp.exp(-jnp.abs(lam))
    u = 1.0 + z
    log1p = jnp.where(u == 1.0, z, jnp.log(u) * (z / jnp.where(u == 1.0, 1.0, u - 1.0)))
    return jnp.minimum(lam, 0.0) - log1p


def _neg_expm1(z):
    series = -z * (1.0 + z * (0.5 + z * (1.0 / 6.0 + z * (1.0 / 24.0 + z * (1.0 / 120.0)))))
    return jnp.where(z > -0.05, series, 1.0 - jnp.exp(z))


_GELU_C = 0.7978845608028654


def _gelu(x):
    t = jnp.tanh(_GELU_C * (x + 0.044715 * (x * x * x)))
    return x * (0.5 * (1.0 + t)), t


def _gelu_grad(x, t):
    return 0.5 * (1.0 + t) + 0.5 * x * (1.0 - t * t) * (_GELU_C * (1.0 + 3.0 * 0.044715 * (x * x)))


def _lru_gates(lx, head8, cw, cb, wa_ref, ba, wx_ref, bx, ls):
    nb = wa_ref.shape[0]
    sh = [lx] + [_shift_down(lx, d, head8) for d in (1, 2, 3)]
    cx = cb + sh[3] * cw[0:1]
    cx = cx + sh[2] * cw[1:2]
    cx = cx + sh[1] * cw[2:3]
    cx = cx + sh[0] * cw[3:4]
    cxb = cx.astype(BF16)
    ra = jnp.concatenate([_dot(cxb[:, n * HEAD_DIM:(n + 1) * HEAD_DIM], wa_ref[n]) for n in range(nb)], axis=1) + ba
    ia = jnp.concatenate([_dot(cxb[:, n * HEAD_DIM:(n + 1) * HEAD_DIM], wx_ref[n]) for n in range(nb)], axis=1) + bx
    r = jax.nn.sigmoid(ra)
    ig = jax.nn.sigmoid(ia)
    log_a = LRU_C * r * ls
    a = jnp.exp(log_a)
    mult = jnp.sqrt(_neg_expm1(2.0 * log_a))
    return sh, cx, cxb, r, ig, a, mult


def _lru_specs(tl, DL):
    nb = DL // HEAD_DIM
    vec = pl.BlockSpec((1, DL), lambda i: (0, 0))
    return [pl.BlockSpec((CONV_W, DL), lambda i: (0, 0)), vec,
            pl.BlockSpec((nb, HEAD_DIM, HEAD_DIM), lambda i: (0, 0, 0)), vec,
            pl.BlockSpec((nb, HEAD_DIM, HEAD_DIM), lambda i: (0, 0, 0)), vec, vec]


def _lru_fwd(proj, cw, cb, wa, ba, wx, bx, lam, tl, d_mix, comm=None):
    S = proj.shape[0]
    DL = cb.shape[1]
    tl = _tile(S, tl)

    def body(lx_ref, lg_ref, cw_ref, cb_ref, wa_ref, ba_ref, wx_ref, bx_ref, lam_ref, h_ref, y_ref, prev8, hc, a_s, b_s):
        i = pl.program_id(0)

        @pl.when(i == 0)
        def _():
            prev8[...] = jnp.zeros_like(prev8)
            hc[...] = jnp.zeros_like(hc)

        lx = lx_ref[...]
        ls = _log_sigmoid(lam_ref[...])
        _, cx, _, _, ig, a, mult = _lru_gates(lx, prev8[...], cw_ref[...], cb_ref[...], wa_ref, ba_ref[...],
                                              wx_ref, bx_ref[...], ls)
        b = mult * (ig * cx)
        row = lax.broadcasted_iota(jnp.int32, a.shape, 0) & 7
        for d in (1, 2, 4):
            a_sh = pltpu.roll(a, d, 0)
            b_sh = pltpu.roll(b, d, 0)
            m = row >= d
            b = jnp.where(m, a * b_sh + b, b)
            a = jnp.where(m, a * a_sh, a)
        a_s[...] = a
        b_s[...] = b

        def step(g, hprev):
            sl = pl.ds(pl.multiple_of(g * 8, 8), 8)
            hh = a_s[sl, :] * hprev + b_s[sl, :]
            h_ref[sl, :] = hh
            return hh[7:8, :]

        hc[0:1, :] = lax.fori_loop(0, tl // 8, step, hc[0:1, :])
        prev8[...] = lx[tl - 8:tl]
        g, _ = _gelu(lg_ref[...])
        y_ref[...] = (h_ref[...] * g).astype(BF16)

    return _pcall(
        body, name="lru_fwd", grid=(S // tl,),
        in_specs=[pl.BlockSpec((tl, DL), lambda i: (i, 0)), pl.BlockSpec((tl, DL), lambda i: (i, 1))] + _lru_specs(tl, DL),
        out_specs=[pl.BlockSpec((tl, DL), lambda i: (i, 0)), pl.BlockSpec((tl, DL), lambda i: (i, 0))],
        out_shape=[jax.ShapeDtypeStruct((S, DL), F32), jax.ShapeDtypeStruct((S, d_mix), BF16)],
        scratch_shapes=[pltpu.VMEM((8, DL), F32), pltpu.VMEM((8, DL), F32), pltpu.VMEM((tl, DL), F32), pltpu.VMEM((tl, DL), F32)],
        operands=[proj, proj, cw, cb, wa, ba, wx, bx, lam], comm=comm)


def _lru_bwd(proj, h, dy, cw, cb, wa, ba, wx, bx, lam, tl, comm=None):
    S = proj.shape[0]
    DL = cb.shape[1]
    nb = DL // HEAD_DIM
    tl = _tile(S, tl)
    nt = S // tl
    ng = tl // 8
    t8 = tl // 8

    def body(lx_ref, lxp_ref, lg_ref, h_ref, hp_ref, dy_ref, cw_ref, cb_ref, wa_ref, ba_ref, wx_ref, bx_ref, lam_ref,
             dlxg_ref, dcw_ref, dcb_ref, dwa_ref, dba_ref, dwx_ref, dbx_ref, dlam_ref,
             a_next, g_carry, dcx_next, an_s, dh_s, g_s):
        i = pl.program_id(0)

        @pl.when(i == 0)
        def _():
            for ref in (dcw_ref, dcb_ref, dwa_ref, dba_ref, dwx_ref, dbx_ref, dlam_ref, a_next, g_carry, dcx_next):
                ref[...] = jnp.zeros_like(ref)

        first = i == nt - 1
        lx = lx_ref[...]
        hv = h_ref[...]
        lg = lg_ref[...]
        dyv = dy_ref[...]
        head8 = jnp.where(first, 0.0, lxp_ref[...])
        hhead8 = jnp.where(first, 0.0, hp_ref[...])
        lamv = lam_ref[...]
        ls = _log_sigmoid(lamv)
        cwv = cw_ref[...]
        sh, cx, cxb, r, ig, a, mult = _lru_gates(lx, head8, cwv, cb_ref[...], wa_ref, ba_ref[...], wx_ref, bx_ref[...], ls)
        hprev = _shift_down(hv, 1, hhead8)
        g, t = _gelu(lg)
        dlg = dyv * hv * _gelu_grad(lg, t)
        dh = dyv * g
        an = _shift_up(a, 1, a_next[...])
        row = lax.broadcasted_iota(jnp.int32, a.shape, 0) & 7
        for d in (1, 2, 4):
            an_sh = pltpu.roll(an, tl - d, 0)
            dh_sh = pltpu.roll(dh, tl - d, 0)
            m = row + d < 8
            dh = jnp.where(m, an * dh_sh + dh, dh)
            an = jnp.where(m, an * an_sh, an)
        an_s[...] = an
        dh_s[...] = dh

        def step(k, gc):
            sl = pl.ds(pl.multiple_of((ng - 1 - k) * 8, 8), 8)
            gg = an_s[sl, :] * gc + dh_s[sl, :]
            g_s[sl, :] = gg
            return gg[0:1, :]

        g_carry[0:1, :] = lax.fori_loop(0, ng, step, g_carry[0:1, :])
        a_next[...] = a[0:8]
        G = g_s[...]
        da = G * hprev
        icx = ig * cx
        dmult = G * icx
        dicx = G * mult
        di = dicx * cx
        dcx = dicx * ig
        dlog = da * a - dmult * (a * a) / mult
        dr = dlog * (LRU_C * ls)
        dlam_ref[...] += jnp.sum(dlog * (LRU_C * r), axis=0, keepdims=True)
        dra = dr * r * (1.0 - r)
        dia = di * ig * (1.0 - ig)
        dba_ref[...] += jnp.sum(dra, axis=0, keepdims=True)
        dbx_ref[...] += jnp.sum(dia, axis=0, keepdims=True)
        drab = dra.astype(BF16)
        diab = dia.astype(BF16)
        back = []
        for n in range(nb):
            cs = slice(n * HEAD_DIM, (n + 1) * HEAD_DIM)
            dwa_ref[n] += _dot(cxb[:, cs], drab[:, cs], TN)
            dwx_ref[n] += _dot(cxb[:, cs], diab[:, cs], TN)
            back.append(_dot(drab[:, cs], wa_ref[n], NT) + _dot(diab[:, cs], wx_ref[n], NT))
        dcx = dcx + jnp.concatenate(back, axis=1)
        dcb_ref[...] += jnp.sum(dcx, axis=0, keepdims=True)
        for tap in range(CONV_W):
            dcw_ref[tap:tap + 1, :] += jnp.sum(dcx * sh[CONV_W - 1 - tap], axis=0, keepdims=True)
        tail = dcx_next[...]
        dlx = dcx * cwv[3:4]
        for d in (1, 2, 3):
            dlx = dlx + _shift_up(dcx, d, tail) * cwv[3 - d:4 - d]
        dcx_next[...] = dcx[0:8]
        dlxg_ref[:, 0:DL] = dlx.astype(BF16)
        dlxg_ref[:, DL:2 * DL] = dlg.astype(BF16)

        @pl.when(i == nt - 1)
        def _():
            dlam_ref[...] = dlam_ref[...] * (1.0 - jax.nn.sigmoid(lamv))

    rev = lambda i: nt - 1 - i
    prev8_map = lambda i: (jnp.maximum((nt - 1 - i) * t8 - 1, 0), 0)
    vec = pl.BlockSpec((1, DL), lambda i: (0, 0))
    mat = pl.BlockSpec((nb, HEAD_DIM, HEAD_DIM), lambda i: (0, 0, 0))
    return _pcall(
        body, name="lru_bwd", grid=(nt,), operands=[proj, proj, proj, h, h, dy, cw, cb, wa, ba, wx, bx, lam], comm=comm,
        in_specs=[pl.BlockSpec((tl, DL), lambda i: (rev(i), 0)), pl.BlockSpec((8, DL), prev8_map),
                  pl.BlockSpec((tl, DL), lambda i: (rev(i), 1)),
                  pl.BlockSpec((tl, DL), lambda i: (rev(i), 0)), pl.BlockSpec((8, DL), prev8_map),
                  pl.BlockSpec((tl, DL), lambda i: (rev(i), 0))] + _lru_specs(tl, DL),
        out_specs=[pl.BlockSpec((tl, 2 * DL), lambda i: (rev(i), 0)), pl.BlockSpec((CONV_W, DL), lambda i: (0, 0)), vec,
                   mat, vec, mat, vec, vec],
        out_shape=[jax.ShapeDtypeStruct(proj.shape, BF16), jax.ShapeDtypeStruct((CONV_W, DL), F32),
                   jax.ShapeDtypeStruct((1, DL), F32), jax.ShapeDtypeStruct((nb, HEAD_DIM, HEAD_DIM), F32),
                   jax.ShapeDtypeStruct((1, DL), F32), jax.ShapeDtypeStruct((nb, HEAD_DIM, HEAD_DIM), F32),
                   jax.ShapeDtypeStruct((1, DL), F32), jax.ShapeDtypeStruct((1, DL), F32)],
        scratch_shapes=[pltpu.VMEM((8, DL), F32), pltpu.VMEM((8, DL), F32), pltpu.VMEM((8, DL), F32),
                        pltpu.VMEM((tl, DL), F32), pltpu.VMEM((tl, DL), F32), pltpu.VMEM((tl, DL), F32)])


def _ret_tables(S, H):
    pos = jnp.arange(S, dtype=F32)
    inv_freq = ROPE_BASE ** (-jnp.arange(0, HEAD_DIM, 2, dtype=F32) / HEAD_DIM)
    ang = pos[:, None] * inv_freq[None, :]
    cos, sin = jnp.cos(ang), jnp.sin(ang)
    cosf = jnp.concatenate([cos, cos], axis=1)
    sins = jnp.concatenate([-sin, sin], axis=1)
    log_gamma = jnp.log1p(-jnp.exp2(-5.0 - jnp.arange(H, dtype=F32)))
    idx = jnp.arange(CHUNK)
    diff = idx[:, None] - idx[None, :]
    causal = diff >= 0
    decay = jnp.where(causal[None], jnp.exp(log_gamma[:, None, None] * jnp.where(causal, diff, 0)[None].astype(F32)), 0.0)
    zeta = jnp.exp(log_gamma[:, None] * (CHUNK - 1 - idx).astype(F32)[None, :])
    xi = jnp.exp(log_gamma[:, None] * (idx + 1).astype(F32)[None, :])
    gc = jnp.exp(log_gamma * CHUNK)
    lanes = (H, CHUNK, HEAD_DIM)
    return (cosf, sins, decay, jnp.broadcast_to(zeta[:, :, None], lanes), jnp.broadcast_to(xi[:, :, None], lanes),
            jnp.broadcast_to(gc[:, None, None], lanes))


def _rope(t, cos, sin_signed):
    return t * cos + pltpu.roll(t, HEAD_DIM // 2, 1) * sin_signed


def _rope_t(d, cos, sin_signed):
    return d * cos + pltpu.roll(d * sin_signed, HEAD_DIM // 2, 1)


def _ret_const_specs(H, DR):
    full = pl.BlockSpec((H, CHUNK, HEAD_DIM), lambda *_: (0, 0, 0))
    return [full, full, full, full, pl.BlockSpec((1, DR), lambda *_: (0, 0))]


def _ret_fwd(proj, y, tables, gnw, tb, comm=None):
    S = proj.shape[0]
    DR = gnw.shape[1]
    H = DR // HEAD_DIM
    tb = _tile(S, tb, CHUNK)
    nc = tb // CHUNK
    cosf, sins, dm, zeta, xi, gc = tables
    scale = HEAD_DIM ** -0.5

    def body(qk_ref, vg_ref, cos_ref, sin_ref, dm_ref, zeta_ref, xi_ref, gc_ref, gnw_ref, y_in, y_ref, rprev_ref, r_s):
        del y_in
        i = pl.program_id(0)

        @pl.when(i == 0)
        def _():
            r_s[...] = jnp.zeros_like(r_s)

        def chunk(c, carry):
            rows = pl.ds(pl.multiple_of(c * CHUNK, CHUNK), CHUNK)
            cos = cos_ref[rows, :]
            sin = sin_ref[rows, :]
            for h in range(H):
                c0 = slice(h * HEAD_DIM, (h + 1) * HEAD_DIM)
                c1 = slice(DR + h * HEAD_DIM, DR + (h + 1) * HEAD_DIM)
                qh = _rope(qk_ref[rows, c0], cos, sin)
                kh = _rope(qk_ref[rows, c1], cos, sin) * scale
                vb = vg_ref[rows, c0].astype(BF16)
                gate = vg_ref[rows, c1]
                s = _dot(qh.astype(BF16), kh.astype(BF16), NT) * dm_ref[h]
                rp = r_s[h]
                rpb = rp.astype(BF16)
                rprev_ref[c, h] = rpb
                o = _dot(s.astype(BF16), vb) + _dot((qh * xi_ref[h]).astype(BF16), rpb)
                r_s[h] = rp * gc_ref[h] + _dot((kh * zeta_ref[h]).astype(BF16), vb, TN)
                mu = jnp.mean(o, axis=-1, keepdims=True)
                oc = o - mu
                var = jnp.mean(oc * oc, axis=-1, keepdims=True)
                on = oc * lax.rsqrt(var + EPS) * gnw_ref[:, c0]
                y_ref[rows, c0] = (gate * jax.nn.sigmoid(gate) * on).astype(BF16)
            return carry

        lax.fori_loop(0, nc, chunk, 0)

    return _pcall(
        body, name="ret_fwd", grid=(S // tb,),
        in_specs=[pl.BlockSpec((tb, 2 * DR), lambda i: (i, 1)), pl.BlockSpec((tb, 2 * DR), lambda i: (i, 2)),
                  pl.BlockSpec((tb, HEAD_DIM), lambda i: (i, 0)), pl.BlockSpec((tb, HEAD_DIM), lambda i: (i, 0))]
        + _ret_const_specs(H, DR) + [HBM_SPEC],
        out_specs=[pl.BlockSpec((tb, DR), lambda i: (i, 1)),
                   pl.BlockSpec((nc, H, CHUNK, HEAD_DIM), lambda i: (i, 0, 0, 0))],
        out_shape=[jax.ShapeDtypeStruct(y.shape, BF16), jax.ShapeDtypeStruct((S // CHUNK, H, CHUNK, HEAD_DIM), BF16)],
        scratch_shapes=[pltpu.VMEM((H, CHUNK, HEAD_DIM), F32)], aliases={9: 0},
        operands=[proj, proj, cosf, sins, dm, zeta, xi, gc, gnw, y], comm=comm)


def _ret_bwd(proj, rprev, dy, dproj, tables, gnw, tb, comm=None):
    S = proj.shape[0]
    DR = gnw.shape[1]
    H = DR // HEAD_DIM
    tb = _tile(S, tb, CHUNK)
    nc = tb // CHUNK
    nt = S // tb
    cosf, sins, dm, zeta, xi, gc = tables
    scale = HEAD_DIM ** -0.5

    def body(qk_ref, vg_ref, cos_ref, sin_ref, dm_ref, zeta_ref, xi_ref, gc_ref, gnw_ref, rprev_ref, dy_ref, dp_in,
             dp_ref, dgn_ref, dr_s, dvg_s):
        del dp_in
        i = pl.program_id(0)
        second = pl.program_id(1) == 1

        @pl.when(jnp.logical_and(i == 0, jnp.logical_not(second)))
        def _():
            dr_s[...] = jnp.zeros_like(dr_s)
            dgn_ref[...] = jnp.zeros_like(dgn_ref)

        @pl.when(second)
        def _():
            dp_ref[...] = dvg_s[...]

        def chunk(cc, carry):
            c = nc - 1 - cc
            rows = pl.ds(pl.multiple_of(c * CHUNK, CHUNK), CHUNK)
            cos = cos_ref[rows, :]
            sin = sin_ref[rows, :]
            for h in range(H):
                c0 = slice(h * HEAD_DIM, (h + 1) * HEAD_DIM)
                c1 = slice(DR + h * HEAD_DIM, DR + (h + 1) * HEAD_DIM)
                qh = _rope(qk_ref[rows, c0], cos, sin)
                kh = _rope(qk_ref[rows, c1], cos, sin) * scale
                qb = qh.astype(BF16)
                kb = kh.astype(BF16)
                vb = vg_ref[rows, c0].astype(BF16)
                gate = vg_ref[rows, c1]
                dmh = dm_ref[h]
                xih = xi_ref[h]
                zetah = zeta_ref[h]
                sb = (_dot(qb, kb, NT) * dmh).astype(BF16)
                rpb = rprev_ref[c, h]
                qx = (qh * xih).astype(BF16)
                kz = (kh * zetah).astype(BF16)
                o = _dot(sb, vb) + _dot(qx, rpb)
                mu = jnp.mean(o, axis=-1, keepdims=True)
                oc = o - mu
                rstd = lax.rsqrt(jnp.mean(oc * oc, axis=-1, keepdims=True) + EPS)
                ohat = oc * rstd
                gw = gnw_ref[:, c0]
                sg = jax.nn.sigmoid(gate)
                dyv = dy_ref[rows, c0]
                dgate = dyv * (ohat * gw) * (sg * (1.0 + gate * (1.0 - sg)))
                don = dyv * (gate * sg)
                dgn_ref[:, c0] += jnp.sum(don * ohat, axis=0, keepdims=True)
                dohat = don * gw
                do = rstd * (dohat - jnp.mean(dohat, axis=-1, keepdims=True)
                             - ohat * jnp.mean(dohat * ohat, axis=-1, keepdims=True))
                dob = do.astype(BF16)
                drh = dr_s[h]
                drb = drh.astype(BF16)
                dv = _dot(sb, dob, TN) + _dot(kz, drb)
                dsb = (_dot(dob, vb, NT) * dmh).astype(BF16)
                dqh = _dot(dsb, kb) + _dot(dob, rpb, NT) * xih
                dkh = _dot(dsb, qb, TN) + _dot(vb, drb, NT) * zetah
                dr_s[h] = drh * gc_ref[h] + _dot(qx, dob, TN)
                dp_ref[rows, c0] = _rope_t(dqh, cos, sin).astype(BF16)
                dp_ref[rows, c1] = _rope_t(dkh * scale, cos, sin).astype(BF16)
                dvg_s[rows, c0] = dv.astype(BF16)
                dvg_s[rows, c1] = dgate.astype(BF16)
            return carry

        @pl.when(jnp.logical_not(second))
        def _():
            lax.fori_loop(0, nc, chunk, 0)

    rev = lambda i: nt - 1 - i
    return _pcall(
        body, name="ret_bwd", grid=(nt, 2), aliases={11: 0}, comm=comm,
        operands=[proj, proj, cosf, sins, dm, zeta, xi, gc, gnw, rprev, dy, dproj],
        in_specs=[pl.BlockSpec((tb, 2 * DR), lambda i, j: (rev(i), 1)), pl.BlockSpec((tb, 2 * DR), lambda i, j: (rev(i), 2)),
                  pl.BlockSpec((tb, HEAD_DIM), lambda i, j: (rev(i), 0)), pl.BlockSpec((tb, HEAD_DIM), lambda i, j: (rev(i), 0))]
        + _ret_const_specs(H, DR)
        + [pl.BlockSpec((nc, H, CHUNK, HEAD_DIM), lambda i, j: (rev(i), 0, 0, 0)),
           pl.BlockSpec((tb, DR), lambda i, j: (rev(i), 1)), HBM_SPEC],
        out_specs=[pl.BlockSpec((tb, 2 * DR), lambda i, j: (rev(i), 1 + j)), pl.BlockSpec((1, DR), lambda i, j: (0, 0))],
        out_shape=[jax.ShapeDtypeStruct(dproj.shape, BF16), jax.ShapeDtypeStruct((1, DR), F32)],
        scratch_shapes=[pltpu.VMEM((H, CHUNK, HEAD_DIM), F32), pltpu.VMEM((tb, 2 * DR), BF16)])


def _place():
    x, y, c = lax.axis_index("x"), lax.axis_index("y"), lax.axis_index("c")
    chips = [(1 - x, y), (x, 1 - y), (1 - x, 1 - y)]
    return x, y, c, chips


def _own_slab(name, shard, place):
    R, C = shard.shape
    tr = _row_tile(R, C)
    return _ew("cast_" + name, lambda a: (a,), [(shard, pl.BlockSpec((tr, C), lambda i, p: (i, 0)))],
               [(jax.ShapeDtypeStruct((4, R, C), BF16), pl.BlockSpec((None, tr, C), lambda i, p: (p[1], i, 0)))],
               (R // tr,), sp=place)[0]


class _remote:
    def __init__(self, src, dst, ssem, rsem, k, to):
        self.args = dict(src_ref=src, dst_ref=dst, send_sem=ssem.at[k], recv_sem=rsem.at[k], device_id=to,
                         device_id_type=MESH)

    def start(self):
        pltpu.make_async_remote_copy(**self.args).start()

    def wait_send(self):
        pltpu.make_async_remote_copy(**self.args).wait_send()

    def wait_recv(self):
        pltpu.make_async_remote_copy(**self.args).wait_recv()


def _task_fns(copies):
    def start(cins, couts, ssem, rsem, base):
        for cp in copies(cins, couts, ssem, rsem, base)[0]:
            cp.start()

    def finish(cins, couts, ssem, rsem, base):
        sends, recvs = copies(cins, couts, ssem, rsem, base)
        for cp in sends:
            cp.wait_send()
        for cp in recvs:
            cp.wait_recv()

    return start, finish


def _gather_ici(st):
    r2 = st.shape[1] // 2

    def copies(cins, couts, ssem, rsem, base):
        x, y, c, chips = _place()
        out = couts[0]
        mine = out.at[2 * x + y, pl.ds(c * r2, r2), :]
        sends = [_remote(mine, mine, ssem, rsem, base + j, (*chip, c)) for j, chip in enumerate(chips)]
        recvs = []
        for j, (cx, cy) in enumerate(chips):
            got = out.at[2 * cx + cy, pl.ds(c * r2, r2), :]
            recvs.append(_remote(got, got, ssem, rsem, base + j, (x, y, c)))
        return sends, recvs

    start, finish = _task_fns(copies)
    return _Comm([st], [jax.ShapeDtypeStruct(st.shape, st.dtype)], {0: 0}, 3, start, finish)


def _gather_d2d(st):
    r2 = st.shape[1] // 2

    def copies(cins, couts, ssem, rsem, base):
        x, y, c, chips = _place()
        out = couts[0]
        sends, recvs = [], []
        for j, (cx, cy) in enumerate(chips):
            have = out.at[2 * cx + cy, pl.ds(c * r2, r2), :]
            want = out.at[2 * cx + cy, pl.ds((1 - c) * r2, r2), :]
            sends.append(_remote(have, have, ssem, rsem, base + j, (x, y, 1 - c)))
            recvs.append(_remote(want, want, ssem, rsem, base + j, (x, y, c)))
        return sends, recvs

    start, finish = _task_fns(copies)
    return _Comm([st], [jax.ShapeDtypeStruct(st.shape, st.dtype)], {0: 0}, 3, start, finish)


def _gather_conv(conv_w):
    def copies(cins, couts, ssem, rsem, base):
        x, y, c, chips = _place()
        src, out = cins[0], couts[0]
        sends = [_remote(src, out.at[2 * x + y], ssem, rsem, base + j, (*chip, c)) for j, chip in enumerate(chips)]
        recvs = [_remote(src, out.at[2 * cx + cy], ssem, rsem, base + j, (x, y, c)) for j, (cx, cy) in enumerate(chips)]
        return sends, recvs

    start, finish = _task_fns(copies)
    return _Comm([conv_w], [jax.ShapeDtypeStruct((4,) + conv_w.shape, conv_w.dtype)], {}, 3, start, finish)


def _pair_exchange(g):
    r2 = g.shape[1] // 2

    def copies(cins, couts, ssem, rsem, base):
        x, y, c, _ = _place()
        cp = _remote(cins[0].at[:, pl.ds((1 - c) * r2, r2), :], couts[0], ssem, rsem, base, (x, y, 1 - c))
        return [cp], [cp]

    start, finish = _task_fns(copies)
    return _Comm([g], [jax.ShapeDtypeStruct((g.shape[0], r2, g.shape[2]), g.dtype)], {}, 1, start, finish)


def _chip_exchange(part):
    def copies(cins, couts, ssem, rsem, base):
        x, y, c, chips = _place()
        cps = [_remote(cins[0].at[2 * cx + cy], couts[0].at[j], ssem, rsem, base + j, (cx, cy, c))
               for j, (cx, cy) in enumerate(chips)]
        return cps, cps

    start, finish = _task_fns(copies)
    return _Comm([part], [jax.ShapeDtypeStruct((3,) + part.shape[1:], part.dtype)], {}, 3, start, finish)


def _pair_share(slot):
    def copies(cins, couts, ssem, rsem, base):
        x, y, c, _ = _place()
        out = couts[0]
        return ([_remote(out.at[c], out.at[c], ssem, rsem, base, (x, y, 1 - c))],
                [_remote(out.at[1 - c], out.at[1 - c], ssem, rsem, base, (x, y, c))])

    start, finish = _task_fns(copies)
    return _Comm([slot], [jax.ShapeDtypeStruct(slot.shape, slot.dtype)], {0: 0}, 1, start, finish)


def _gather_small(sm):
    flips = [(fx, fy, fc) for fx in (0, 1) for fy in (0, 1) for fc in (0, 1)][1:]

    def copies(cins, couts, ssem, rsem, base):
        x, y, c, _ = _place()
        src, out = cins[0], couts[0]
        peers = [(1 - x if fx else x, 1 - y if fy else y, 1 - c if fc else c) for fx, fy, fc in flips]
        sends = [_remote(src, out.at[4 * x + 2 * y + c], ssem, rsem, base + k, peer) for k, peer in enumerate(peers)]
        recvs = [_remote(src, out.at[4 * px + 2 * py + pc], ssem, rsem, base + k, (x, y, c))
                 for k, (px, py, pc) in enumerate(peers)]
        return sends, recvs

    start, finish = _task_fns(copies)
    return _Comm([sm], [jax.ShapeDtypeStruct((8,) + sm.shape, sm.dtype)], {}, 7, start, finish)


def _gather_both(st):
    ici, d2d = _gather_ici(st), _gather_d2d(st)

    def finish(cins, couts, ssem, rsem, base):
        ici.finish(cins, couts, ssem, rsem, base)
        d2d.start(cins, couts, ssem, rsem, base + ici.n_sem)
        d2d.finish(cins, couts, ssem, rsem, base + ici.n_sem)

    return _Comm(ici.ins, ici.outs, ici.aliases, ici.n_sem + d2d.n_sem, ici.start, finish)


def _comm_call(name, tasks):
    task = _merge(tasks)
    nci = len(task.ins)

    def body(*refs):
        cins, couts, (ssem, rsem) = refs[:nci], refs[nci:nci + len(task.outs)], refs[nci + len(task.outs):]
        task.start(cins, couts, ssem, rsem, 0)
        task.finish(cins, couts, ssem, rsem, 0)

    return pl.pallas_call(
        body, in_specs=[HBM_SPEC] * nci, out_specs=[HBM_SPEC] * len(task.outs), out_shape=list(task.outs),
        scratch_shapes=[pltpu.SemaphoreType.DMA((task.n_sem,)), pltpu.SemaphoreType.DMA((task.n_sem,))],
        input_output_aliases=task.aliases, name=name)(*task.ins)


def _adamw(w, g, m, v):
    m = ADAM_B1 * m + (1.0 - ADAM_B1) * g
    v = ADAM_B2 * v + (1.0 - ADAM_B2) * (g * g)
    m_hat = m / (1.0 - ADAM_B1 ** ADAM_STEP)
    v_hat = v / (1.0 - ADAM_B2 ** ADAM_STEP)
    delta = -ADAM_LR * (m_hat / (jnp.sqrt(v_hat) + ADAM_EPS) + ADAM_WD * w)
    return delta, m, v


def _adamw_call(name, w, g, m, v):
    R, C = w.shape
    tr = _row_tile(R, C, 1024 * 1024)
    row = pl.BlockSpec((tr, C), lambda i: (i, 0))
    o = jax.ShapeDtypeStruct((R, C), F32)
    return _ew(name, _adamw, [(w, row), (g, row), (m, row), (v, row)], [(o, row), (o, row), (o, row)], (R // tr,))


def _pair_sum(name, g, ra, place):
    _, R, C = g.shape
    r2 = R // 2
    tr = _row_tile(r2, C)
    nb = r2 // tr
    own = pl.BlockSpec((None, tr, C), lambda j, i, p: (j, p[0] * nb + i, 0))
    blk = pl.BlockSpec((None, tr, C), lambda j, i, p: (j, i, 0))
    return _ew("rs_pair_sum_" + name, lambda a, b: (a + b,), [(g, own), (ra, blk)],
               [(jax.ShapeDtypeStruct((4, r2, C), BF16), blk)], (4, nb), sp=place)[0]


def _chip_sum(name, g, ra, rb, place):
    _, R, C = g.shape
    r2 = R // 2
    tr = _row_tile(r2, C)
    nb = r2 // tr
    own = pl.BlockSpec((None, tr, C), lambda i, p: (p[1], p[0] * nb + i, 0))
    mine = pl.BlockSpec((None, tr, C), lambda i, p: (p[1], i, 0))
    src = [pl.BlockSpec((None, tr, C), functools.partial(lambda i, p, j: (j, i, 0), j=j)) for j in range(3)]
    out = pl.BlockSpec((None, tr, C), lambda i, p: (p[0], i, 0))

    def total(a, b, r0, r1, r2_):
        return ((((a + b) + r0.astype(F32)) + r1.astype(F32)) + r2_.astype(F32),)

    return _ew("rs_chip_sum_" + name, total, [(g, own), (ra, mine), (rb, src[0]), (rb, src[1]), (rb, src[2])],
               [(jax.ShapeDtypeStruct((2, r2, C), F32), out)], (nb,), sp=place)[0]


def _pack(arrays):
    rows, offs, pos = [], [], 0
    for a in arrays:
        flat = a.reshape(-1)
        n = -(-flat.shape[0] // (8 * LANES)) * (8 * LANES)
        if n != flat.shape[0]:
            flat = jnp.pad(flat, (0, n - flat.shape[0]))
        rows.append(flat.reshape(-1, LANES))
        offs.append(pos)
        pos += n // LANES
    return jnp.concatenate(rows, axis=0), offs


def _unpack(packed, offs, shapes):
    out = []
    for off, shp in zip(offs, shapes):
        n = 1
        for s in shp:
            n *= s
        out.append(packed[off:off + -(-n // LANES)].reshape(-1)[:n].reshape(shp))
    return out


def _sum8(gathered):
    _, R, C = gathered.shape
    tr = _row_tile(R, C, 256 * 1024)
    specs = [pl.BlockSpec((None, tr, C), functools.partial(lambda i, d: (d, i, 0), d=d)) for d in range(8)]

    def fn(*parts):
        t = parts[0]
        for p in parts[1:]:
            t = t + p
        return (t,)

    return _ew("small_sum", fn, [(gathered, s) for s in specs],
               [(jax.ShapeDtypeStruct((R, C), F32), pl.BlockSpec((tr, C), lambda i: (i, 0)))], (R // tr,))[0]


BIG = ("w_in", "w_out", "w_ffn_gate", "w_ffn_up", "w_ffn_down")
SMALL = ("ln1_w", "conv_w", "conv_b", "gate_a_w", "gate_a_b", "gate_x_w", "gate_x_b", "lru_lambda", "ret_gn_w", "ln2_w",
         "final_norm_w")
WEIGHTS = ("ln1_w", "w_in", "conv_w", "conv_b", "gate_a_w", "gate_a_b", "gate_x_w", "gate_x_b", "lru_lambda", "ret_gn_w",
           "w_out", "ln2_w", "w_ffn_gate", "w_ffn_up", "w_ffn_down", "final_norm_w")


def kernel(x, ln1_w, w_in, conv_w, conv_b, gate_a_w, gate_a_b, gate_x_w, gate_x_b, lru_lambda, ret_gn_w, w_out, ln2_w, w_ffn_gate, w_ffn_up, w_ffn_down, final_norm_w, loss_target, m_ln1_w, m_w_in, m_conv_w, m_conv_b, m_gate_a_w, m_gate_a_b, m_gate_x_w, m_gate_x_b, m_lru_lambda, m_ret_gn_w, m_w_out, m_ln2_w, m_w_ffn_gate, m_w_ffn_up, m_w_ffn_down, m_final_norm_w, v_ln1_w, v_w_in, v_conv_w, v_conv_b, v_gate_a_w, v_gate_a_b, v_gate_x_w, v_gate_x_b, v_lru_lambda, v_ret_gn_w, v_w_out, v_ln2_w, v_w_ffn_gate, v_w_ffn_up, v_w_ffn_down, v_final_norm_w):
    w = dict(ln1_w=ln1_w, w_in=w_in, conv_w=conv_w, conv_b=conv_b, gate_a_w=gate_a_w, gate_a_b=gate_a_b, gate_x_w=gate_x_w,
             gate_x_b=gate_x_b, lru_lambda=lru_lambda, ret_gn_w=ret_gn_w, w_out=w_out, ln2_w=ln2_w, w_ffn_gate=w_ffn_gate,
             w_ffn_up=w_ffn_up, w_ffn_down=w_ffn_down, final_norm_w=final_norm_w)
    m = dict(ln1_w=m_ln1_w, w_in=m_w_in, conv_w=m_conv_w, conv_b=m_conv_b, gate_a_w=m_gate_a_w, gate_a_b=m_gate_a_b,
             gate_x_w=m_gate_x_w, gate_x_b=m_gate_x_b, lru_lambda=m_lru_lambda, ret_gn_w=m_ret_gn_w, w_out=m_w_out,
             ln2_w=m_ln2_w, w_ffn_gate=m_w_ffn_gate, w_ffn_up=m_w_ffn_up, w_ffn_down=m_w_ffn_down,
             final_norm_w=m_final_norm_w)
    v = dict(ln1_w=v_ln1_w, w_in=v_w_in, conv_w=v_conv_w, conv_b=v_conv_b, gate_a_w=v_gate_a_w, gate_a_b=v_gate_a_b,
             gate_x_w=v_gate_x_w, gate_x_b=v_gate_x_b, lru_lambda=v_lru_lambda, ret_gn_w=v_ret_gn_w, w_out=v_w_out,
             ln2_w=v_ln2_w, w_ffn_gate=v_w_ffn_gate, w_ffn_up=v_w_ffn_up, w_ffn_down=v_w_ffn_down,
             final_norm_w=v_final_norm_w)
    xs, tgt = x[0], loss_target[0]
    S, D = xs.shape
    DL, DR = conv_b.shape[1], ret_gn_w.shape[1]
    assert DL == DR and DL % HEAD_DIM == 0 and S % CHUNK == 0
    d_mix = DL + DR
    cx, cy, cc = lax.axis_index("x"), lax.axis_index("y"), lax.axis_index("c")
    chip = 2 * cx + cy
    place = jnp.stack([cc, chip]).astype(jnp.int32)
    grad, delta, new_m, new_v = {}, {}, {}, {}

    def finish_big(n, full):
        shp = w[n].shape
        g2 = full.reshape(shp[1], shp[2])
        w2, m2, v2 = (t[n].reshape(shp[1], shp[2]) for t in (w, m, v))
        d_, m_, v_ = _adamw_call("adamw_" + n, w2, g2, m2, v2)
        grad[n], delta[n], new_m[n], new_v[n] = (t.reshape(shp) for t in (g2, d_, m_, v_))

    def all_sum(gathered, own):
        return _sum8(lax.dynamic_update_slice(gathered, own[None], (4 * cx + 2 * cy + cc, 0, 0)))

    st = {n: _own_slab(n, w[n][0], place) for n in BIG}
    w_in_st, conv_st = _comm_call("gather_w_in", [_gather_both(st["w_in"]), _gather_conv(conv_w[0])])
    conv_st = lax.dynamic_update_slice(conv_st, conv_w, (chip, 0, 0))
    cw_cols = conv_st.shape[2]
    conv_full = jnp.transpose(conv_st, (1, 0, 2)).reshape(CONV_W, 4 * cw_cols)
    n_in, n_ff = w_in_st.shape[2], st["w_ffn_gate"].shape[2]
    tables = _ret_tables(S, DR // HEAD_DIM)
    wab, wxb = gate_a_w[0].astype(BF16), gate_x_w[0].astype(BF16)
    lru_w = (conv_full, conv_b, wab, gate_a_b, wxb, gate_x_b, lru_lambda)
    TM, TK = 512, 1024

    u1 = _rms_fwd("rms1", xs, ln1_w, TM)
    proj, (w_out_st, wg_st) = _mm_nn_stacked("proj", u1, w_in_st, F32, TM,
                                             comm=_merge([_gather_ici(st["w_out"]), _gather_ici(st["w_ffn_gate"])]))
    (hs, y), (w_out_st, wg_st, wu_st) = _lru_fwd(
        proj, *lru_w, 128, d_mix, comm=_merge([_gather_d2d(w_out_st), _gather_d2d(wg_st), _gather_ici(st["w_ffn_up"])]))
    (y, rprev), (wu_st, wd_st) = _ret_fwd(proj, y, tables, ret_gn_w, 256,
                                          comm=_merge([_gather_d2d(wu_st), _gather_ici(st["w_ffn_down"])]))
    w_out_f = w_out_st.reshape(d_mix, D)
    h1, (wd_st,) = _mm_nn("out_proj", y, w_out_f, xs, F32, TM, 1024, d_mix, comm=_gather_d2d(wd_st))
    wd_f = wd_st.reshape(4 * n_ff, D)
    u2 = _rms_fwd("rms2", h1, ln2_w, TM)
    gt, up, ff = _ffn_gate_up(u2, wg_st, wu_st, TM)
    h2 = _mm_nn("ffn_down", ff, wd_f, h1, F32, TM, 1024, 4 * n_ff)
    dh2, dh2b, d_fw, loss = _loss_head(h2, tgt, final_norm_w.reshape(1, D), TM)

    g_wd = _mm_tn("g_w_down", ff, dh2b, n_ff, 1024, TK).reshape(4, n_ff, D)
    (dgt, dup), (ra_wd,) = _ffn_gate_up_bwd(dh2b, wd_f, gt, up, TM, n_ff, comm=_pair_exchange(g_wd))
    pb_wd = _pair_sum("w_ffn_down", g_wd, ra_wd, place)
    g_wg, (rb_wd,) = _mm_tn("g_w_gate", u2, dgt, 1024, None, TK, stacked_cols=n_ff, comm=_chip_exchange(pb_wd))
    slot_wd = _chip_sum("w_ffn_down", g_wd, ra_wd, rb_wd, place)
    g_wu, (full_wd, ra_wg) = _mm_tn("g_w_up", u2, dup, 1024, None, TK, stacked_cols=n_ff,
                                    comm=_merge([_pair_share(slot_wd), _pair_exchange(g_wg)]))
    finish_big("w_ffn_down", full_wd)
    pb_wg = _pair_sum("w_ffn_gate", g_wg, ra_wg, place)
    du2, (rb_wg, ra_wu) = _mm_nt_stacked("d_u2", [dgt, dup], [wg_st, wu_st], F32, 256, 1024,
                                         comm=_merge([_chip_exchange(pb_wg), _pair_exchange(g_wu)]))
    slot_wg = _chip_sum("w_ffn_gate", g_wg, ra_wg, rb_wg, place)
    pb_wu = _pair_sum("w_ffn_up", g_wu, ra_wu, place)
    (dh1, dh1b, d_ln2), (full_wg,) = _rms_bwd("rms2_bwd", h1, ln2_w, du2, dh2, TM, True, comm=_pair_share(slot_wg))
    finish_big("w_ffn_gate", full_wg)
    dy = _mm_nt("d_y", dh1b, w_out_f, F32, TM, 1024)
    g_wout = _mm_tn("g_w_out", y, dh1b, 1024, 1024, TK).reshape(4, d_mix // 4, D)
    (dproj, d_cw, d_cb, d_wa, d_ba, d_wx, d_bx, d_lam), (rb_wu, ra_wout) = _lru_bwd(
        proj, hs, dy, *lru_w, 128, comm=_merge([_chip_exchange(pb_wu), _pair_exchange(g_wout)]))
    slot_wu = _chip_sum("w_ffn_up", g_wu, ra_wu, rb_wu, place)
    pb_wout = _pair_sum("w_out", g_wout, ra_wout, place)
    (dproj, d_gn), (full_wu, rb_wout) = _ret_bwd(proj, rprev, dy, dproj, tables, ret_gn_w, 256,
                                                 comm=_merge([_pair_share(slot_wu), _chip_exchange(pb_wout)]))
    finish_big("w_ffn_up", full_wu)
    slot_wout = _chip_sum("w_out", g_wout, ra_wout, rb_wout, place)
    small = dict(conv_w=d_cw, conv_b=d_cb, gate_a_w=d_wa, gate_a_b=d_ba, gate_x_w=d_wx, gate_x_b=d_bx, lru_lambda=d_lam,
                 ret_gn_w=d_gn, ln2_w=d_ln2, final_norm_w=d_fw)
    packed, offs = _pack([small[n] for n in SMALL[1:]] + [loss])
    g_win, (full_wout, got_small) = _mm_tn("g_w_in", u1, dproj, 1024, None, TK, stacked_cols=n_in,
                                           comm=_merge([_pair_share(slot_wout), _gather_small(packed)]))
    finish_big("w_out", full_wout)
    du1, (ra_win,) = _mm_nt_stacked("d_u1", [dproj], [w_in_st], F32, 256, D, comm=_pair_exchange(g_win))
    pb_win = _pair_sum("w_in", g_win, ra_win, place)
    (gx, d_ln1), (rb_win,) = _rms_bwd("rms1_bwd", xs, ln1_w, du1, dh1, TM, False, comm=_chip_exchange(pb_win))
    slot_win = _chip_sum("w_in", g_win, ra_win, rb_win, place)
    packed1, _ = _pack([d_ln1])
    full_win, got_ln1 = _comm_call("reduce_tail", [_pair_share(slot_win), _gather_small(packed1)])
    finish_big("w_in", full_win)

    red = _unpack(all_sum(got_small, packed), offs, [small[n].shape for n in SMALL[1:]] + [(1, LANES)])
    g = dict(zip(SMALL[1:], red[:-1]))
    g["ln1_w"] = all_sum(got_ln1, packed1)[:-(-D // LANES)].reshape(1, D)
    loss_out = red[-1][0, 0]
    g["conv_w"] = lax.dynamic_slice(g["conv_w"], (0, chip * cw_cols), (CONV_W, cw_cols))
    packs = [_pack([t[n] for n in SMALL])[0] for t in (w, m, v)]
    gp, offs2 = _pack([g[n] for n in SMALL])
    outs = _adamw_call("adamw_small", packs[0], gp, packs[1], packs[2])
    shapes = [w[n].shape for n in SMALL]
    for dst, arr in zip((delta, new_m, new_v), outs):
        dst.update(zip(SMALL, _unpack(arr, offs2, shapes)))
    for n in SMALL:
        grad[n] = g[n].reshape(w[n].shape)

    return (loss_out, gx.reshape(x.shape), *[grad[n] for n in WEIGHTS], *[delta[n] for n in WEIGHTS],
            *[new_m[n] for n in WEIGHTS], *[new_v[n] for n in WEIGHTS])
```

```python
import functools

import jax
import jax.numpy as jnp
from jax import lax
from jax.experimental import pallas as pl
from jax.experimental.pallas import tpu as pltpu

F32 = jnp.float32
BF16 = jnp.bfloat16
MESH = pl.DeviceIdType.MESH

EPS = 1e-6
LRU_C = 8.0
ROPE_BASE = 10000.0
CHUNK = 128
HEAD_DIM = 128
CONV_W = 4
ADAM_LR = 0.001
ADAM_B1 = 0.9
ADAM_B2 = 0.999
ADAM_EPS = 1e-08
ADAM_WD = 0.01
ADAM_STEP = 10

V7X_VMEM_BYTES = 64 * 1024 * 1024
VMEM_LIMIT = V7X_VMEM_BYTES - 8 * 1024 * 1024
LANES = 128
SUBLANES_16BIT = 16

NN = (((1,), (0,)), ((), ()))
NT = (((1,), (1,)), ((), ()))
TN = (((0,), (0,)), ((), ()))


def _dot(a, b, dims=NN):
    return lax.dot_general(a, b, dims, preferred_element_type=F32)


def _tile(n, pref, mult=SUBLANES_16BIT):
    best = None
    t = mult
    while t <= min(n, pref):
        if n % t == 0:
            best = t
        t += mult
    return best if best is not None else n


def _row_tile(rows, cols, budget_bytes=2 * 1024 * 1024):
    return _tile(rows, max(SUBLANES_16BIT, budget_bytes // (cols * 4)))


def _params(sem):
    return pltpu.CompilerParams(dimension_semantics=sem, vmem_limit_bytes=VMEM_LIMIT)


HBM_SPEC = pl.BlockSpec(memory_space=pl.ANY)


class _Comm:
    def __init__(self, ins, outs, aliases, n_sem, start, finish):
        self.ins, self.outs, self.aliases, self.n_sem, self.start, self.finish = ins, outs, aliases, n_sem, start, finish


def _merge(tasks):
    ins, outs, aliases, plans, n_sem = [], [], {}, [], 0
    for t in tasks:
        i0, o0 = len(ins), len(outs)
        plans.append((t, i0, o0, n_sem))
        ins += t.ins
        outs += t.outs
        aliases.update({i0 + a: o0 + b for a, b in t.aliases.items()})
        n_sem += t.n_sem

    def run(which):
        def go(cins, couts, ssem, rsem, base):
            for t, i0, o0, s0 in plans:
                getattr(t, which)(cins[i0:i0 + len(t.ins)], couts[o0:o0 + len(t.outs)], ssem, rsem, base + s0)
        return go

    return _Comm(ins, outs, aliases, n_sem, run("start"), run("finish"))


def _pcall(body, *, name, grid, in_specs, out_specs, out_shape, operands, scratch_shapes=(), aliases=None, comm=None):
    n_in, n_out, n_scr = len(operands), len(out_shape), len(scratch_shapes)
    aliases = dict(aliases or {})
    params = _params(("arbitrary",) * len(grid))
    if comm is None:
        return pl.pallas_call(body, grid=grid, in_specs=list(in_specs), out_specs=list(out_specs), out_shape=list(out_shape),
                              scratch_shapes=list(scratch_shapes), input_output_aliases=aliases, name=name,
                              compiler_params=params)(*operands)
    nci, nco = len(comm.ins), len(comm.outs)

    def wrapped(*refs):
        ins, cins = refs[:n_in], refs[n_in:n_in + nci]
        o0 = n_in + nci
        outs, couts = refs[o0:o0 + n_out], refs[o0 + n_out:o0 + n_out + nco]
        s0 = o0 + n_out + nco
        scr, (ssem, rsem) = refs[s0:s0 + n_scr], refs[s0 + n_scr:]
        ids = [pl.program_id(a) for a in range(len(grid))]
        first = functools.reduce(jnp.logical_and, [i == 0 for i in ids])
        last = functools.reduce(jnp.logical_and, [i == g - 1 for i, g in zip(ids, grid)])

        @pl.when(first)
        def _():
            comm.start(cins, couts, ssem, rsem, 0)

        body(*ins, *outs, *scr)

        @pl.when(last)
        def _():
            comm.finish(cins, couts, ssem, rsem, 0)

    aliases.update({n_in + a: n_out + b for a, b in comm.aliases.items()})
    res = pl.pallas_call(
        wrapped, grid=grid, in_specs=list(in_specs) + [HBM_SPEC] * nci, out_specs=list(out_specs) + [HBM_SPEC] * nco,
        out_shape=list(out_shape) + list(comm.outs),
        scratch_shapes=list(scratch_shapes) + [pltpu.SemaphoreType.DMA((comm.n_sem,)), pltpu.SemaphoreType.DMA((comm.n_sem,))],
        input_output_aliases=aliases, name=name, compiler_params=params)(*operands, *comm.ins)
    return res[:n_out], res[n_out:]


def _ew(name, fn, ins, outs, grid, sp=None):
    n_in = len(ins)

    def body(*refs):
        if sp is not None:
            refs = refs[1:]
        vals = [r[...] for r in refs[:n_in]]
        res = fn(*vals)
        for o_ref, v in zip(refs[n_in:], res):
            o_ref[...] = v.astype(o_ref.dtype)

    in_specs = [s for _, s in ins]
    out_specs = [s for _, s in outs]
    out_shape = [s for s, _ in outs]
    sem = ("arbitrary",) * len(grid)
    if sp is None:
        return pl.pallas_call(body, grid=grid, in_specs=in_specs, out_specs=out_specs, out_shape=out_shape,
                              name=name, compiler_params=_params(sem))(*[a for a, _ in ins])
    gs = pltpu.PrefetchScalarGridSpec(num_scalar_prefetch=1, grid=grid, in_specs=in_specs, out_specs=out_specs)
    return pl.pallas_call(body, grid_spec=gs, out_shape=out_shape, name=name,
                          compiler_params=_params(sem))(sp, *[a for a, _ in ins])


def _matmul(name, pairs, dims, grid, out_shape, out_spec, acc_shape, res=None, comm=None):
    n = len(pairs)
    nk = grid[2]

    def body(*refs):
        ab = refs[:2 * n]
        pos = 2 * n
        res_ref = None
        if res is not None:
            res_ref = refs[pos]
            pos += 1
        o_ref = refs[pos]
        acc_ref = refs[pos + 1] if nk > 1 else None

        def partial():
            t = None
            for p in range(n):
                d = _dot(ab[2 * p][...], ab[2 * p + 1][...], dims)
                t = d if t is None else t + d
            return t

        def finish(t):
            if res_ref is not None:
                t = t + res_ref[...]
            o_ref[...] = t.astype(o_ref.dtype)

        if nk == 1:
            finish(partial())
        else:
            k = pl.program_id(2)

            @pl.when(k == 0)
            def _():
                acc_ref[...] = partial()

            @pl.when(k > 0)
            def _():
                acc_ref[...] += partial()

            @pl.when(k == nk - 1)
            def _():
                finish(acc_ref[...])

    operands, in_specs = [], []
    for a, a_spec, b, b_spec in pairs:
        operands += [a, b]
        in_specs += [a_spec, b_spec]
    if res is not None:
        operands.append(res[0])
        in_specs.append(res[1])
    scratch = [pltpu.VMEM(acc_shape, F32)] if nk > 1 else []
    res = _pcall(body, name=name, grid=grid, in_specs=in_specs, out_specs=[out_spec], out_shape=[out_shape],
                 operands=operands, scratch_shapes=scratch, comm=comm)
    return res[0] if comm is None else (res[0][0], res[1])


def _mm_nn_stacked(name, a, b_st, out_dtype, tm, comm=None):
    M, K = a.shape
    J, _, Nj = b_st.shape
    tm = _tile(M, tm)
    return _matmul(
        name, [(a, pl.BlockSpec((tm, K), lambda j, i, k: (i, 0)), b_st, pl.BlockSpec((None, K, Nj), lambda j, i, k: (j, 0, 0)))],
        NN, (J, M // tm, 1), jax.ShapeDtypeStruct((M, J * Nj), out_dtype), pl.BlockSpec((tm, Nj), lambda j, i, k: (i, j)), None,
        comm=comm)


def _mm_nn(name, a, b, res, out_dtype, tm, tn, tk, comm=None):
    M, K = a.shape
    N = b.shape[1]
    tm, tn, tk = _tile(M, tm), _tile(N, tn, LANES), _tile(K, tk, LANES)
    return _matmul(
        name, [(a, pl.BlockSpec((tm, tk), lambda j, i, k: (i, k)), b, pl.BlockSpec((tk, tn), lambda j, i, k: (k, j)))],
        NN, (N // tn, M // tm, K // tk), jax.ShapeDtypeStruct((M, N), out_dtype), pl.BlockSpec((tm, tn), lambda j, i, k: (i, j)),
        (tm, tn), res=(res, pl.BlockSpec((tm, tn), lambda j, i, k: (i, j))), comm=comm)


def _mm_nt(name, a, b, out_dtype, tm, tn):
    M, K = a.shape
    N = b.shape[0]
    tm, tn = _tile(M, tm), _tile(N, tn, LANES)
    return _matmul(
        name, [(a, pl.BlockSpec((tm, K), lambda j, i, k: (i, 0)), b, pl.BlockSpec((tn, K), lambda j, i, k: (j, 0)))],
        NT, (N // tn, M // tm, 1), jax.ShapeDtypeStruct((M, N), out_dtype), pl.BlockSpec((tm, tn), lambda j, i, k: (i, j)), None)


def _mm_nt_stacked(name, a_list, b_list, out_dtype, tm, tn, comm=None):
    n = len(a_list)
    M = a_list[0].shape[0]
    J, N, Nj = b_list[0].shape
    tm, tn = _tile(M, tm), _tile(N, tn, LANES)

    def body(*refs):
        o_ref = refs[2 * n]
        t = None
        for p in range(n):
            a_ref, b_ref = refs[p], refs[n + p]
            for s in range(J):
                d = _dot(a_ref[:, s * Nj:(s + 1) * Nj], b_ref[s], NT)
                t = d if t is None else t + d
        o_ref[...] = t.astype(o_ref.dtype)

    a_spec = pl.BlockSpec((tm, J * Nj), lambda j, i: (i, 0))
    b_spec = pl.BlockSpec((J, tn, Nj), lambda j, i: (0, j, 0), pipeline_mode=pl.Buffered(1))
    res = _pcall(body, name=name, grid=(N // tn, M // tm), in_specs=[a_spec] * n + [b_spec] * n,
                 out_specs=[pl.BlockSpec((tm, tn), lambda j, i: (i, j))], out_shape=[jax.ShapeDtypeStruct((M, N), out_dtype)],
                 operands=[*a_list, *b_list], comm=comm)
    return res[0] if comm is None else (res[0][0], res[1])


def _ffn_gate_up(u2, wg_st, wu_st, tm, comm=None):
    S, D = u2.shape
    J, _, Nj = wg_st.shape
    tm = _tile(S, tm)

    def body(a_ref, wg_ref, wu_ref, gt_ref, up_ref, ff_ref):
        a = a_ref[...]
        g = _dot(a, wg_ref[...])
        u = _dot(a, wu_ref[...])
        gt_ref[...] = g.astype(BF16)
        up_ref[...] = u.astype(BF16)
        ff_ref[...] = (g * jax.nn.sigmoid(g) * u).astype(BF16)

    w_spec = pl.BlockSpec((None, D, Nj), lambda j, i: (j, 0, 0))
    o_spec = pl.BlockSpec((tm, Nj), lambda j, i: (i, j))
    o = jax.ShapeDtypeStruct((S, J * Nj), BF16)
    return _pcall(body, name="ffn_gate_up", grid=(J, S // tm),
                  in_specs=[pl.BlockSpec((tm, D), lambda j, i: (i, 0)), w_spec, w_spec],
                  out_specs=[o_spec, o_spec, o_spec], out_shape=[o, o, o], operands=[u2, wg_st, wu_st], comm=comm)


def _ffn_gate_up_bwd(dh2b, wd, gt, up, tm, tn, comm=None):
    S, D = dh2b.shape
    F = wd.shape[0]
    tm, tn = _tile(S, tm), _tile(F, tn, LANES)

    def body(a_ref, wd_ref, gt_ref, up_ref, dgt_ref, dup_ref):
        d = _dot(a_ref[...], wd_ref[...], NT)
        g = gt_ref[...].astype(F32)
        u = up_ref[...].astype(F32)
        sg = jax.nn.sigmoid(g)
        dgt_ref[...] = (d * u * (sg * (1.0 + g * (1.0 - sg)))).astype(BF16)
        dup_ref[...] = (d * (g * sg)).astype(BF16)

    blk = pl.BlockSpec((tm, tn), lambda j, i: (i, j))
    o = jax.ShapeDtypeStruct((S, F), BF16)
    return _pcall(body, name="ffn_gate_up_bwd", grid=(F // tn, S // tm),
                  in_specs=[pl.BlockSpec((tm, D), lambda j, i: (i, 0)), pl.BlockSpec((tn, D), lambda j, i: (j, 0)), blk, blk],
                  out_specs=[blk, blk], out_shape=[o, o], operands=[dh2b, wd, gt, up], comm=comm)


def _mm_tn(name, a, b, tmo, tn, tk, stacked_cols=None, comm=None):
    S, Mo = a.shape
    N = b.shape[1]
    tmo, tk = _tile(Mo, tmo, LANES), _tile(S, tk)
    if stacked_cols is None:
        tn = _tile(N, tn, LANES)
        out_shape = jax.ShapeDtypeStruct((Mo, N), F32)
        out_spec = pl.BlockSpec((tmo, tn), lambda i, j, k: (i, j))
    else:
        tn = stacked_cols
        out_shape = jax.ShapeDtypeStruct((N // tn, Mo, tn), F32)
        out_spec = pl.BlockSpec((None, tmo, tn), lambda i, j, k: (j, i, 0))
    return _matmul(
        name, [(a, pl.BlockSpec((tk, tmo), lambda i, j, k: (k, i)), b, pl.BlockSpec((tk, tn), lambda i, j, k: (k, j)))],
        TN, (Mo // tmo, N // tn, S // tk), out_shape, out_spec, (tmo, tn), comm=comm)


def _rms_fwd(name, x, w, tm):
    S, D = x.shape
    tm = _tile(S, tm)

    def fn(xv, wv):
        r = lax.rsqrt(jnp.mean(xv * xv, axis=-1, keepdims=True) + EPS)
        return ((xv * r) * wv,)

    row = pl.BlockSpec((tm, D), lambda i: (i, 0))
    return _ew(name, fn, [(x, row), (w, pl.BlockSpec((1, D), lambda i: (0, 0)))],
               [(jax.ShapeDtypeStruct((S, D), BF16), row)], (S // tm,))[0]


def _rms_bwd(name, x, w, dy, dres, tm, want_bf16, comm=None):
    S, D = x.shape
    tm = _tile(S, tm)

    def body(x_ref, w_ref, dy_ref, dres_ref, dx_ref, *rest):
        dw_ref = rest[-1]
        i = pl.program_id(0)

        @pl.when(i == 0)
        def _():
            dw_ref[...] = jnp.zeros_like(dw_ref)

        xv = x_ref[...]
        r = lax.rsqrt(jnp.mean(xv * xv, axis=-1, keepdims=True) + EPS)
        nv = xv * r
        dyv = dy_ref[...]
        dn = dyv * w_ref[...]
        dw_ref[...] += jnp.sum(dyv * nv, axis=0, keepdims=True)
        dx = dres_ref[...] + r * (dn - nv * jnp.mean(dn * nv, axis=-1, keepdims=True))
        dx_ref[...] = dx
        if want_bf16:
            rest[0][...] = dx.astype(BF16)

    row = pl.BlockSpec((tm, D), lambda i: (i, 0))
    vec = pl.BlockSpec((1, D), lambda i: (0, 0))
    out_shape = [jax.ShapeDtypeStruct((S, D), F32)] + ([jax.ShapeDtypeStruct((S, D), BF16)] if want_bf16 else []) + \
                [jax.ShapeDtypeStruct((1, D), F32)]
    out_specs = [row] + ([row] if want_bf16 else []) + [vec]
    return _pcall(body, name=name, grid=(S // tm,), in_specs=[row, vec, row, row], out_specs=out_specs, out_shape=out_shape,
                  operands=[x, w, dy, dres], comm=comm)


def _loss_head(h2, tgt, fw, tm):
    S, D = h2.shape
    tm = _tile(S, tm)

    def body(h_ref, t_ref, w_ref, dh_ref, dhb_ref, dw_ref, loss_ref):
        i = pl.program_id(0)

        @pl.when(i == 0)
        def _():
            dw_ref[...] = jnp.zeros_like(dw_ref)
            loss_ref[...] = jnp.zeros_like(loss_ref)

        hv = h_ref[...]
        wv = w_ref[...]
        r = lax.rsqrt(jnp.mean(hv * hv, axis=-1, keepdims=True) + EPS)
        nv = hv * r
        err = nv * wv - t_ref[...]
        row_loss = jnp.mean(err * err, axis=-1, keepdims=True)
        loss_ref[...] += 0.5 * jnp.sum(row_loss, axis=0, keepdims=True)
        dyo = err * (1.0 / D)
        dn = dyo * wv
        dw_ref[...] += jnp.sum(dyo * nv, axis=0, keepdims=True)
        dh = r * (dn - nv * jnp.mean(dn * nv, axis=-1, keepdims=True))
        dh_ref[...] = dh
        dhb_ref[...] = dh.astype(BF16)

    row = pl.BlockSpec((tm, D), lambda i: (i, 0))
    vec = pl.BlockSpec((1, D), lambda i: (0, 0))
    return _pcall(
        body, name="loss_head", grid=(S // tm,), in_specs=[row, row, vec],
        out_specs=[row, row, vec, pl.BlockSpec((1, LANES), lambda i: (0, 0))],
        out_shape=[jax.ShapeDtypeStruct((S, D), F32), jax.ShapeDtypeStruct((S, D), BF16),
                   jax.ShapeDtypeStruct((1, D), F32), jax.ShapeDtypeStruct((1, LANES), F32)],
        operands=[h2, tgt, fw])


def _shift_down(x, d, head8):
    r = pltpu.roll(x, d, 0)
    rh = pltpu.roll(head8, d, 0)
    row8 = lax.broadcasted_iota(jnp.int32, head8.shape, 0)
    top = jnp.where(row8 < d, rh, r[0:8])
    return jnp.concatenate([top, r[8:]], axis=0)


def _shift_up(x, d, tail8):
    n = x.shape[0]
    r = pltpu.roll(x, n - d, 0)
    rt = pltpu.roll(tail8, 8 - d, 0)
    row8 = lax.broadcasted_iota(jnp.int32, tail8.shape, 0)
    bot = jnp.where(row8 + d >= 8, rt, r[n - 8:n])
    return jnp.concatenate([r[:n - 8], bot], axis=0)


def _log_sigmoid(lam):
    z = jnp.exp(-jnp.abs(lam))
    u = 1.0 + z
    log1p = jnp.where(u == 1.0, z, jnp.log(u) * (z / jnp.where(u == 1.0, 1.0, u - 1.0)))
    return jnp.minimum(lam, 0.0) - log1p


def _neg_expm1(z):
    series = -z * (1.0 + z * (0.5 + z * (1.0 / 6.0 + z * (1.0 / 24.0 + z * (1.0 / 120.0)))))
    return jnp.where(z > -0.05, series, 1.0 - jnp.exp(z))


_GELU_C = 0.7978845608028654


def _gelu(x):
    t = jnp.tanh(_GELU_C * (x + 0.044715 * (x * x * x)))
    return x * (0.5 * (1.0 + t)), t


def _gelu_grad(x, t):
    return 0.5 * (1.0 + t) + 0.5 * x * (1.0 - t * t) * (_GELU_C * (1.0 + 3.0 * 0.044715 * (x * x)))


def _lru_gates(lx, head8, cw, cb, wa_ref, ba, wx_ref, bx, ls):
    nb = wa_ref.shape[0]
    sh = [lx] + [_shift_down(lx, d, head8) for d in (1, 2, 3)]
    cx = cb + sh[3] * cw[0:1]
    cx = cx + sh[2] * cw[1:2]
    cx = cx + sh[1] * cw[2:3]
    cx = cx + sh[0] * cw[3:4]
    cxb = cx.astype(BF16)
    ra = jnp.concatenate([_dot(cxb[:, n * HEAD_DIM:(n + 1) * HEAD_DIM], wa_ref[n]) for n in range(nb)], axis=1) + ba
    ia = jnp.concatenate([_dot(cxb[:, n * HEAD_DIM:(n + 1) * HEAD_DIM], wx_ref[n]) for n in range(nb)], axis=1) + bx
    r = jax.nn.sigmoid(ra)
    ig = jax.nn.sigmoid(ia)
    log_a = LRU_C * r * ls
    a = jnp.exp(log_a)
    mult = jnp.sqrt(_neg_expm1(2.0 * log_a))
    return sh, cx, cxb, r, ig, a, mult


def _lru_specs(tl, DL):
    nb = DL // HEAD_DIM
    vec = pl.BlockSpec((1, DL), lambda i: (0, 0))
    return [pl.BlockSpec((CONV_W, DL), lambda i: (0, 0)), vec,
            pl.BlockSpec((nb, HEAD_DIM, HEAD_DIM), lambda i: (0, 0, 0)), vec,
            pl.BlockSpec((nb, HEAD_DIM, HEAD_DIM), lambda i: (0, 0, 0)), vec, vec]


def _lru_fwd(proj, cw, cb, wa, ba, wx, bx, lam, tl, d_mix, comm=None):
    S = proj.shape[0]
    DL = cb.shape[1]
    tl = _tile(S, tl)

    def body(lx_ref, lg_ref, cw_ref, cb_ref, wa_ref, ba_ref, wx_ref, bx_ref, lam_ref, h_ref, y_ref, prev8, hc, a_s, b_s):
        i = pl.program_id(0)

        @pl.when(i == 0)
        def _():
            prev8[...] = jnp.zeros_like(prev8)
            hc[...] = jnp.zeros_like(hc)

        lx = lx_ref[...]
        ls = _log_sigmoid(lam_ref[...])
        _, cx, _, _, ig, a, mult = _lru_gates(lx, prev8[...], cw_ref[...], cb_ref[...], wa_ref, ba_ref[...],
                                              wx_ref, bx_ref[...], ls)
        b = mult * (ig * cx)
        row = lax.broadcasted_iota(jnp.int32, a.shape, 0) & 7
        for d in (1, 2, 4):
            a_sh = pltpu.roll(a, d, 0)
            b_sh = pltpu.roll(b, d, 0)
            m = row >= d
            b = jnp.where(m, a * b_sh + b, b)
            a = jnp.where(m, a * a_sh, a)
        a_s[...] = a
        b_s[...] = b

        def step(g, hprev):
            sl = pl.ds(pl.multiple_of(g * 8, 8), 8)
            hh = a_s[sl, :] * hprev + b_s[sl, :]
            h_ref[sl, :] = hh
            return hh[7:8, :]

        hc[0:1, :] = lax.fori_loop(0, tl // 8, step, hc[0:1, :])
        prev8[...] = lx[tl - 8:tl]
        g, _ = _gelu(lg_ref[...])
        y_ref[...] = (h_ref[...] * g).astype(BF16)

    return _pcall(
        body, name="lru_fwd", grid=(S // tl,),
        in_specs=[pl.BlockSpec((tl, DL), lambda i: (i, 0)), pl.BlockSpec((tl, DL), lambda i: (i, 1))] + _lru_specs(tl, DL),
        out_specs=[pl.BlockSpec((tl, DL), lambda i: (i, 0)), pl.BlockSpec((tl, DL), lambda i: (i, 0))],
        out_shape=[jax.ShapeDtypeStruct((S, DL), F32), jax.ShapeDtypeStruct((S, d_mix), BF16)],
        scratch_shapes=[pltpu.VMEM((8, DL), F32), pltpu.VMEM((8, DL), F32), pltpu.VMEM((tl, DL), F32), pltpu.VMEM((tl, DL), F32)],
        operands=[proj, proj, cw, cb, wa, ba, wx, bx, lam], comm=comm)


def _lru_bwd(proj, h, dy, cw, cb, wa, ba, wx, bx, lam, tl, comm=None):
    S = proj.shape[0]
    DL = cb.shape[1]
    nb = DL // HEAD_DIM
    tl = _tile(S, tl)
    nt = S // tl
    ng = tl // 8
    t8 = tl // 8

    def body(lx_ref, lxp_ref, lg_ref, h_ref, hp_ref, dy_ref, cw_ref, cb_ref, wa_ref, ba_ref, wx_ref, bx_ref, lam_ref,
             dlxg_ref, dcw_ref, dcb_ref, dwa_ref, dba_ref, dwx_ref, dbx_ref, dlam_ref,
             a_next, g_carry, dcx_next, an_s, dh_s, g_s):
        i = pl.program_id(0)

        @pl.when(i == 0)
        def _():
            for ref in (dcw_ref, dcb_ref, dwa_ref, dba_ref, dwx_ref, dbx_ref, dlam_ref, a_next, g_carry, dcx_next):
                ref[...] = jnp.zeros_like(ref)

        first = i == nt - 1
        lx = lx_ref[...]
        hv = h_ref[...]
        lg = lg_ref[...]
        dyv = dy_ref[...]
        head8 = jnp.where(first, 0.0, lxp_ref[...])
        hhead8 = jnp.where(first, 0.0, hp_ref[...])
        lamv = lam_ref[...]
        ls = _log_sigmoid(lamv)
        cwv = cw_ref[...]
        sh, cx, cxb, r, ig, a, mult = _lru_gates(lx, head8, cwv, cb_ref[...], wa_ref, ba_ref[...], wx_ref, bx_ref[...], ls)
        hprev = _shift_down(hv, 1, hhead8)
        g, t = _gelu(lg)
        dlg = dyv * hv * _gelu_grad(lg, t)
        dh = dyv * g
        an = _shift_up(a, 1, a_next[...])
        row = lax.broadcasted_iota(jnp.int32, a.shape, 0) & 7
        for d in (1, 2, 4):
            an_sh = pltpu.roll(an, tl - d, 0)
            dh_sh = pltpu.roll(dh, tl - d, 0)
            m = row + d < 8
            dh = jnp.where(m, an * dh_sh + dh, dh)
            an = jnp.where(m, an * an_sh, an)
        an_s[...] = an
        dh_s[...] = dh

        def step(k, gc):
            sl = pl.ds(pl.multiple_of((ng - 1 - k) * 8, 8), 8)
            gg = an_s[sl, :] * gc + dh_s[sl, :]
            g_s[sl, :] = gg
            return gg[0:1, :]

        g_carry[0:1, :] = lax.fori_loop(0, ng, step, g_carry[0:1, :])
        a_next[...] = a[0:8]
        G = g_s[...]
        da = G * hprev
        icx = ig * cx
        dmult = G * icx
        dicx = G * mult
        di = dicx * cx
        dcx = dicx * ig
        dlog = da * a - dmult * (a * a) / mult
        dr = dlog * (LRU_C * ls)
        dlam_ref[...] += jnp.sum(dlog * (LRU_C * r), axis=0, keepdims=True)
        dra = dr * r * (1.0 - r)
        dia = di * ig * (1.0 - ig)
        dba_ref[...] += jnp.sum(dra, axis=0, keepdims=True)
        dbx_ref[...] += jnp.sum(dia, axis=0, keepdims=True)
        drab = dra.astype(BF16)
        diab = dia.astype(BF16)
        back = []
        for n in range(nb):
            cs = slice(n * HEAD_DIM, (n + 1) * HEAD_DIM)
            dwa_ref[n] += _dot(cxb[:, cs], drab[:, cs], TN)
            dwx_ref[n] += _dot(cxb[:, cs], diab[:, cs], TN)
            back.append(_dot(drab[:, cs], wa_ref[n], NT) + _dot(diab[:, cs], wx_ref[n], NT))
        dcx = dcx + jnp.concatenate(back, axis=1)
        dcb_ref[...] += jnp.sum(dcx, axis=0, keepdims=True)
        for tap in range(CONV_W):
            dcw_ref[tap:tap + 1, :] += jnp.sum(dcx * sh[CONV_W - 1 - tap], axis=0, keepdims=True)
        tail = dcx_next[...]
        dlx = dcx * cwv[3:4]
        for d in (1, 2, 3):
            dlx = dlx + _shift_up(dcx, d, tail) * cwv[3 - d:4 - d]
        dcx_next[...] = dcx[0:8]
        dlxg_ref[:, 0:DL] = dlx.astype(BF16)
        dlxg_ref[:, DL:2 * DL] = dlg.astype(BF16)

        @pl.when(i == nt - 1)
        def _():
            dlam_ref[...] = dlam_ref[...] * (1.0 - jax.nn.sigmoid(lamv))

    rev = lambda i: nt - 1 - i
    prev8_map = lambda i: (jnp.maximum((nt - 1 - i) * t8 - 1, 0), 0)
    vec = pl.BlockSpec((1, DL), lambda i: (0, 0))
    mat = pl.BlockSpec((nb, HEAD_DIM, HEAD_DIM), lambda i: (0, 0, 0))
    return _pcall(
        body, name="lru_bwd", grid=(nt,), operands=[proj, proj, proj, h, h, dy, cw, cb, wa, ba, wx, bx, lam], comm=comm,
        in_specs=[pl.BlockSpec((tl, DL), lambda i: (rev(i), 0)), pl.BlockSpec((8, DL), prev8_map),
                  pl.BlockSpec((tl, DL), lambda i: (rev(i), 1)),
                  pl.BlockSpec((tl, DL), lambda i: (rev(i), 0)), pl.BlockSpec((8, DL), prev8_map),
                  pl.BlockSpec((tl, DL), lambda i: (rev(i), 0))] + _lru_specs(tl, DL),
        out_specs=[pl.BlockSpec((tl, 2 * DL), lambda i: (rev(i), 0)), pl.BlockSpec((CONV_W, DL), lambda i: (0, 0)), vec,
                   mat, vec, mat, vec, vec],
        out_shape=[jax.ShapeDtypeStruct(proj.shape, BF16), jax.ShapeDtypeStruct((CONV_W, DL), F32),
                   jax.ShapeDtypeStruct((1, DL), F32), jax.ShapeDtypeStruct((nb, HEAD_DIM, HEAD_DIM), F32),
                   jax.ShapeDtypeStruct((1, DL), F32), jax.ShapeDtypeStruct((nb, HEAD_DIM, HEAD_DIM), F32),
                   jax.ShapeDtypeStruct((1, DL), F32), jax.ShapeDtypeStruct((1, DL), F32)],
        scratch_shapes=[pltpu.VMEM((8, DL), F32), pltpu.VMEM((8, DL), F32), pltpu.VMEM((8, DL), F32),
                        pltpu.VMEM((tl, DL), F32), pltpu.VMEM((tl, DL), F32), pltpu.VMEM((tl, DL), F32)])


def _ret_tables(S, H):
    pos = jnp.arange(S, dtype=F32)
    inv_freq = ROPE_BASE ** (-jnp.arange(0, HEAD_DIM, 2, dtype=F32) / HEAD_DIM)
    ang = pos[:, None] * inv_freq[None, :]
    cos, sin = jnp.cos(ang), jnp.sin(ang)
    cosf = jnp.concatenate([cos, cos], axis=1)
    sins = jnp.concatenate([-sin, sin], axis=1)
    log_gamma = jnp.log1p(-jnp.exp2(-5.0 - jnp.arange(H, dtype=F32)))
    idx = jnp.arange(CHUNK)
    diff = idx[:, None] - idx[None, :]
    causal = diff >= 0
    decay = jnp.where(causal[None], jnp.exp(log_gamma[:, None, None] * jnp.where(causal, diff, 0)[None].astype(F32)), 0.0)
    zeta = jnp.exp(log_gamma[:, None] * (CHUNK - 1 - idx).astype(F32)[None, :])
    xi = jnp.exp(log_gamma[:, None] * (idx + 1).astype(F32)[None, :])
    gc = jnp.exp(log_gamma * CHUNK)
    lanes = (H, CHUNK, HEAD_DIM)
    return (cosf, sins, decay, jnp.broadcast_to(zeta[:, :, None], lanes), jnp.broadcast_to(xi[:, :, None], lanes),
            jnp.broadcast_to(gc[:, None, None], lanes))


def _rope(t, cos, sin_signed):
    return t * cos + pltpu.roll(t, HEAD_DIM // 2, 1) * sin_signed


def _rope_t(d, cos, sin_signed):
    return d * cos + pltpu.roll(d * sin_signed, HEAD_DIM // 2, 1)


def _ret_const_specs(H, DR):
    full = pl.BlockSpec((H, CHUNK, HEAD_DIM), lambda *_: (0, 0, 0))
    return [full, full, full, full, pl.BlockSpec((1, DR), lambda *_: (0, 0))]


def _ret_fwd(proj, y, tables, gnw, tb, comm=None):
    S = proj.shape[0]
    DR = gnw.shape[1]
    H = DR // HEAD_DIM
    tb = _tile(S, tb, CHUNK)
    nc = tb // CHUNK
    cosf, sins, dm, zeta, xi, gc = tables
    scale = HEAD_DIM ** -0.5

    def body(qk_ref, vg_ref, cos_ref, sin_ref, dm_ref, zeta_ref, xi_ref, gc_ref, gnw_ref, y_in, y_ref, rprev_ref, r_s):
        del y_in
        i = pl.program_id(0)

        @pl.when(i == 0)
        def _():
            r_s[...] = jnp.zeros_like(r_s)

        def chunk(c, carry):
            rows = pl.ds(pl.multiple_of(c * CHUNK, CHUNK), CHUNK)
            cos = cos_ref[rows, :]
            sin = sin_ref[rows, :]
            heads = range(H)
            c0 = [slice(h * HEAD_DIM, (h + 1) * HEAD_DIM) for h in heads]
            c1 = [slice(DR + h * HEAD_DIM, DR + (h + 1) * HEAD_DIM) for h in heads]
            qh = [_rope(qk_ref[rows, c0[h]], cos, sin) for h in heads]
            kh = [_rope(qk_ref[rows, c1[h]], cos, sin) * scale for h in heads]
            vb = [vg_ref[rows, c0[h]].astype(BF16) for h in heads]
            rp = [r_s[h] for h in heads]
            rpb = [rp[h].astype(BF16) for h in heads]
            s = [_dot(qh[h].astype(BF16), kh[h].astype(BF16), NT) for h in heads]
            kv = [_dot((kh[h] * zeta_ref[h]).astype(BF16), vb[h], TN) for h in heads]
            cross = [_dot((qh[h] * xi_ref[h]).astype(BF16), rpb[h]) for h in heads]
            o = [_dot((s[h] * dm_ref[h]).astype(BF16), vb[h]) + cross[h] for h in heads]
            for h in heads:
                rprev_ref[c, h] = rpb[h]
                r_s[h] = rp[h] * gc_ref[h] + kv[h]
                mu = jnp.mean(o[h], axis=-1, keepdims=True)
                oc = o[h] - mu
                var = jnp.mean(oc * oc, axis=-1, keepdims=True)
                on = oc * lax.rsqrt(var + EPS) * gnw_ref[:, c0[h]]
                gate = vg_ref[rows, c1[h]]
                y_ref[rows, c0[h]] = (gate * jax.nn.sigmoid(gate) * on).astype(BF16)
            return carry

        lax.fori_loop(0, nc, chunk, 0)

    return _pcall(
        body, name="ret_fwd", grid=(S // tb,),
        in_specs=[pl.BlockSpec((tb, 2 * DR), lambda i: (i, 1)), pl.BlockSpec((tb, 2 * DR), lambda i: (i, 2)),
                  pl.BlockSpec((tb, HEAD_DIM), lambda i: (i, 0)), pl.BlockSpec((tb, HEAD_DIM), lambda i: (i, 0))]
        + _ret_const_specs(H, DR) + [HBM_SPEC],
        out_specs=[pl.BlockSpec((tb, DR), lambda i: (i, 1)),
                   pl.BlockSpec((nc, H, CHUNK, HEAD_DIM), lambda i: (i, 0, 0, 0))],
        out_shape=[jax.ShapeDtypeStruct(y.shape, BF16), jax.ShapeDtypeStruct((S // CHUNK, H, CHUNK, HEAD_DIM), BF16)],
        scratch_shapes=[pltpu.VMEM((H, CHUNK, HEAD_DIM), F32)], aliases={9: 0},
        operands=[proj, proj, cosf, sins, dm, zeta, xi, gc, gnw, y], comm=comm)


def _ret_bwd(proj, rprev, dy, dproj, tables, gnw, tb, comm=None):
    S = proj.shape[0]
    DR = gnw.shape[1]
    H = DR // HEAD_DIM
    tb = _tile(S, tb, CHUNK)
    nc = tb // CHUNK
    nt = S // tb
    cosf, sins, dm, zeta, xi, gc = tables
    scale = HEAD_DIM ** -0.5

    def body(qk_ref, vg_ref, cos_ref, sin_ref, dm_ref, zeta_ref, xi_ref, gc_ref, gnw_ref, rprev_ref, dy_ref, dp_in,
             dp_ref, dgn_ref, dr_s, dvg_s):
        del dp_in
        i = pl.program_id(0)
        second = pl.program_id(1) == 1

        @pl.when(jnp.logical_and(i == 0, jnp.logical_not(second)))
        def _():
            dr_s[...] = jnp.zeros_like(dr_s)
            dgn_ref[...] = jnp.zeros_like(dgn_ref)

        @pl.when(second)
        def _():
            dp_ref[...] = dvg_s[...]

        def chunk(cc, carry):
            c = nc - 1 - cc
            rows = pl.ds(pl.multiple_of(c * CHUNK, CHUNK), CHUNK)
            cos = cos_ref[rows, :]
            sin = sin_ref[rows, :]
            heads = range(H)
            c0 = [slice(h * HEAD_DIM, (h + 1) * HEAD_DIM) for h in heads]
            c1 = [slice(DR + h * HEAD_DIM, DR + (h + 1) * HEAD_DIM) for h in heads]
            qh = [_rope(qk_ref[rows, c0[h]], cos, sin) for h in heads]
            kh = [_rope(qk_ref[rows, c1[h]], cos, sin) * scale for h in heads]
            qb = [t.astype(BF16) for t in qh]
            kb = [t.astype(BF16) for t in kh]
            vb = [vg_ref[rows, c0[h]].astype(BF16) for h in heads]
            rpb = [rprev_ref[c, h] for h in heads]
            qx = [(qh[h] * xi_ref[h]).astype(BF16) for h in heads]
            kz = [(kh[h] * zeta_ref[h]).astype(BF16) for h in heads]
            drh = [dr_s[h] for h in heads]
            drb = [t.astype(BF16) for t in drh]
            s = [_dot(qb[h], kb[h], NT) for h in heads]
            cross = [_dot(qx[h], rpb[h]) for h in heads]
            dv_state = [_dot(kz[h], drb[h]) for h in heads]
            dk_state = [_dot(vb[h], drb[h], NT) for h in heads]
            sb = [(s[h] * dm_ref[h]).astype(BF16) for h in heads]
            o = [_dot(sb[h], vb[h]) + cross[h] for h in heads]
            dob = []
            for h in heads:
                mu = jnp.mean(o[h], axis=-1, keepdims=True)
                oc = o[h] - mu
                rstd = lax.rsqrt(jnp.mean(oc * oc, axis=-1, keepdims=True) + EPS)
                ohat = oc * rstd
                gw = gnw_ref[:, c0[h]]
                gate = vg_ref[rows, c1[h]]
                sg = jax.nn.sigmoid(gate)
                dyv = dy_ref[rows, c0[h]]
                dvg_s[rows, c1[h]] = (dyv * (ohat * gw) * (sg * (1.0 + gate * (1.0 - sg)))).astype(BF16)
                don = dyv * (gate * sg)
                dgn_ref[:, c0[h]] += jnp.sum(don * ohat, axis=0, keepdims=True)
                dohat = don * gw
                do = rstd * (dohat - jnp.mean(dohat, axis=-1, keepdims=True)
                             - ohat * jnp.mean(dohat * ohat, axis=-1, keepdims=True))
                dob.append(do.astype(BF16))
            ds = [_dot(dob[h], vb[h], NT) for h in heads]
            dq_state = [_dot(dob[h], rpb[h], NT) for h in heads]
            dv = [_dot(sb[h], dob[h], TN) + dv_state[h] for h in heads]
            dr_new = [_dot(qx[h], dob[h], TN) for h in heads]
            dsb = [(ds[h] * dm_ref[h]).astype(BF16) for h in heads]
            dqh = [_dot(dsb[h], kb[h]) + dq_state[h] * xi_ref[h] for h in heads]
            dkh = [_dot(dsb[h], qb[h], TN) + dk_state[h] * zeta_ref[h] for h in heads]
            for h in heads:
                dr_s[h] = drh[h] * gc_ref[h] + dr_new[h]
                dp_ref[rows, c0[h]] = _rope_t(dqh[h], cos, sin).astype(BF16)
                dp_ref[rows, c1[h]] = _rope_t(dkh[h] * scale, cos, sin).astype(BF16)
                dvg_s[rows, c0[h]] = dv[h].astype(BF16)
            return carry

        @pl.when(jnp.logical_not(second))
        def _():
            lax.fori_loop(0, nc, chunk, 0)

    rev = lambda i: nt - 1 - i
    return _pcall(
        body, name="ret_bwd", grid=(nt, 2), aliases={11: 0}, comm=comm,
        operands=[proj, proj, cosf, sins, dm, zeta, xi, gc, gnw, rprev, dy, dproj],
        in_specs=[pl.BlockSpec((tb, 2 * DR), lambda i, j: (rev(i), 1)), pl.BlockSpec((tb, 2 * DR), lambda i, j: (rev(i), 2)),
                  pl.BlockSpec((tb, HEAD_DIM), lambda i, j: (rev(i), 0)), pl.BlockSpec((tb, HEAD_DIM), lambda i, j: (rev(i), 0))]
        + _ret_const_specs(H, DR)
        + [pl.BlockSpec((nc, H, CHUNK, HEAD_DIM), lambda i, j: (rev(i), 0, 0, 0)),
           pl.BlockSpec((tb, DR), lambda i, j: (rev(i), 1)), HBM_SPEC],
        out_specs=[pl.BlockSpec((tb, 2 * DR), lambda i, j: (rev(i), 1 + j)), pl.BlockSpec((1, DR), lambda i, j: (0, 0))],
        out_shape=[jax.ShapeDtypeStruct(dproj.shape, BF16), jax.ShapeDtypeStruct((1, DR), F32)],
        scratch_shapes=[pltpu.VMEM((H, CHUNK, HEAD_DIM), F32), pltpu.VMEM((tb, 2 * DR), BF16)])


def _place():
    x, y, c = lax.axis_index("x"), lax.axis_index("y"), lax.axis_index("c")
    chips = [(1 - x, y), (x, 1 - y), (1 - x, 1 - y)]
    return x, y, c, chips


def _own_slab(name, shard, place):
    R, C = shard.shape
    tr = _row_tile(R, C)
    return _ew("cast_" + name, lambda a: (a,), [(shard, pl.BlockSpec((tr, C), lambda i, p: (i, 0)))],
               [(jax.ShapeDtypeStruct((4, R, C), BF16), pl.BlockSpec((None, tr, C), lambda i, p: (p[1], i, 0)))],
               (R // tr,), sp=place)[0]


class _remote:
    def __init__(self, src, dst, ssem, rsem, k, to):
        self.args = dict(src_ref=src, dst_ref=dst, send_sem=ssem.at[k], recv_sem=rsem.at[k], device_id=to,
                         device_id_type=MESH)

    def start(self):
        pltpu.make_async_remote_copy(**self.args).start()

    def wait_send(self):
        pltpu.make_async_remote_copy(**self.args).wait_send()

    def wait_recv(self):
        pltpu.make_async_remote_copy(**self.args).wait_recv()


def _task_fns(copies):
    def start(cins, couts, ssem, rsem, base):
        for cp in copies(cins, couts, ssem, rsem, base)[0]:
            cp.start()

    def finish(cins, couts, ssem, rsem, base):
        sends, recvs = copies(cins, couts, ssem, rsem, base)
        for cp in sends:
            cp.wait_send()
        for cp in recvs:
            cp.wait_recv()

    return start, finish


def _gather_ici(st):
    r2 = st.shape[1] // 2

    def copies(cins, couts, ssem, rsem, base):
        x, y, c, chips = _place()
        out = couts[0]
        mine = out.at[2 * x + y, pl.ds(c * r2, r2), :]
        sends = [_remote(mine, mine, ssem, rsem, base + j, (*chip, c)) for j, chip in enumerate(chips)]
        recvs = []
        for j, (cx, cy) in enumerate(chips):
            got = out.at[2 * cx + cy, pl.ds(c * r2, r2), :]
            recvs.append(_remote(got, got, ssem, rsem, base + j, (x, y, c)))
        return sends, recvs

    start, finish = _task_fns(copies)
    return _Comm([st], [jax.ShapeDtypeStruct(st.shape, st.dtype)], {0: 0}, 3, start, finish)


def _gather_d2d(st):
    r2 = st.shape[1] // 2

    def copies(cins, couts, ssem, rsem, base):
        x, y, c, chips = _place()
        out = couts[0]
        sends, recvs = [], []
        for j, (cx, cy) in enumerate(chips):
            have = out.at[2 * cx + cy, pl.ds(c * r2, r2), :]
            want = out.at[2 * cx + cy, pl.ds((1 - c) * r2, r2), :]
            sends.append(_remote(have, have, ssem, rsem, base + j, (x, y, 1 - c)))
            recvs.append(_remote(want, want, ssem, rsem, base + j, (x, y, c)))
        return sends, recvs

    start, finish = _task_fns(copies)
    return _Comm([st], [jax.ShapeDtypeStruct(st.shape, st.dtype)], {0: 0}, 3, start, finish)


def _gather_conv(conv_w):
    def copies(cins, couts, ssem, rsem, base):
        x, y, c, chips = _place()
        src, out = cins[0], couts[0]
        sends = [_remote(src, out.at[2 * x + y], ssem, rsem, base + j, (*chip, c)) for j, chip in enumerate(chips)]
        recvs = [_remote(src, out.at[2 * cx + cy], ssem, rsem, base + j, (x, y, c)) for j, (cx, cy) in enumerate(chips)]
        return sends, recvs

    start, finish = _task_fns(copies)
    return _Comm([conv_w], [jax.ShapeDtypeStruct((4,) + conv_w.shape, conv_w.dtype)], {}, 3, start, finish)


def _pair_exchange(g):
    r2 = g.shape[1] // 2

    def copies(cins, couts, ssem, rsem, base):
        x, y, c, _ = _place()
        cp = _remote(cins[0].at[:, pl.ds((1 - c) * r2, r2), :], couts[0], ssem, rsem, base, (x, y, 1 - c))
        return [cp], [cp]

    start, finish = _task_fns(copies)
    return _Comm([g], [jax.ShapeDtypeStruct((g.shape[0], r2, g.shape[2]), g.dtype)], {}, 1, start, finish)


def _chip_exchange(part):
    def copies(cins, couts, ssem, rsem, base):
        x, y, c, chips = _place()
        cps = [_remote(cins[0].at[2 * cx + cy], couts[0].at[j], ssem, rsem, base + j, (cx, cy, c))
               for j, (cx, cy) in enumerate(chips)]
        return cps, cps

    start, finish = _task_fns(copies)
    return _Comm([part], [jax.ShapeDtypeStruct((3,) + part.shape[1:], part.dtype)], {}, 3, start, finish)


def _pair_share(slot):
    def copies(cins, couts, ssem, rsem, base):
        x, y, c, _ = _place()
        out = couts[0]
        return ([_remote(out.at[c], out.at[c], ssem, rsem, base, (x, y, 1 - c))],
                [_remote(out.at[1 - c], out.at[1 - c], ssem, rsem, base, (x, y, c))])

    start, finish = _task_fns(copies)
    return _Comm([slot], [jax.ShapeDtypeStruct(slot.shape, slot.dtype)], {0: 0}, 1, start, finish)


def _gather_small(sm):
    flips = [(fx, fy, fc) for fx in (0, 1) for fy in (0, 1) for fc in (0, 1)][1:]

    def copies(cins, couts, ssem, rsem, base):
        x, y, c, _ = _place()
        src, out = cins[0], couts[0]
        peers = [(1 - x if fx else x, 1 - y if fy else y, 1 - c if fc else c) for fx, fy, fc in flips]
        sends = [_remote(src, out.at[4 * x + 2 * y + c], ssem, rsem, base + k, peer) for k, peer in enumerate(peers)]
        recvs = [_remote(src, out.at[4 * px + 2 * py + pc], ssem, rsem, base + k, (x, y, c))
                 for k, (px, py, pc) in enumerate(peers)]
        return sends, recvs

    start, finish = _task_fns(copies)
    return _Comm([sm], [jax.ShapeDtypeStruct((8,) + sm.shape, sm.dtype)], {}, 7, start, finish)


def _gather_both(st):
    ici, d2d = _gather_ici(st), _gather_d2d(st)

    def finish(cins, couts, ssem, rsem, base):
        ici.finish(cins, couts, ssem, rsem, base)
        d2d.start(cins, couts, ssem, rsem, base + ici.n_sem)
        d2d.finish(cins, couts, ssem, rsem, base + ici.n_sem)

    return _Comm(ici.ins, ici.outs, ici.aliases, ici.n_sem + d2d.n_sem, ici.start, finish)


def _comm_call(name, tasks):
    task = _merge(tasks)
    nci = len(task.ins)

    def body(*refs):
        cins, couts, (ssem, rsem) = refs[:nci], refs[nci:nci + len(task.outs)], refs[nci + len(task.outs):]
        task.start(cins, couts, ssem, rsem, 0)
        task.finish(cins, couts, ssem, rsem, 0)

    return pl.pallas_call(
        body, in_specs=[HBM_SPEC] * nci, out_specs=[HBM_SPEC] * len(task.outs), out_shape=list(task.outs),
        scratch_shapes=[pltpu.SemaphoreType.DMA((task.n_sem,)), pltpu.SemaphoreType.DMA((task.n_sem,))],
        input_output_aliases=task.aliases, name=name)(*task.ins)


def _adamw(w, g, m, v):
    m = ADAM_B1 * m + (1.0 - ADAM_B1) * g
    v = ADAM_B2 * v + (1.0 - ADAM_B2) * (g * g)
    m_hat = m / (1.0 - ADAM_B1 ** ADAM_STEP)
    v_hat = v / (1.0 - ADAM_B2 ** ADAM_STEP)
    delta = -ADAM_LR * (m_hat / (jnp.sqrt(v_hat) + ADAM_EPS) + ADAM_WD * w)
    return delta, m, v


def _adamw_call(name, w, g, m, v):
    R, C = w.shape
    tr = _row_tile(R, C, 1024 * 1024)
    row = pl.BlockSpec((tr, C), lambda i: (i, 0))
    o = jax.ShapeDtypeStruct((R, C), F32)
    return _ew(name, lambda w_, g_, m_, v_: (*_adamw(w_, g_, m_, v_), g_), [(w, row), (g, row), (m, row), (v, row)],
               [(o, row), (o, row), (o, row), (o, row)], (R // tr,))


def _pair_sum(name, g, ra, place):
    _, R, C = g.shape
    r2 = R // 2
    tr = _row_tile(r2, C)
    nb = r2 // tr
    own = pl.BlockSpec((None, tr, C), lambda j, i, p: (j, p[0] * nb + i, 0))
    blk = pl.BlockSpec((None, tr, C), lambda j, i, p: (j, i, 0))
    return _ew("rs_pair_sum_" + name, lambda a, b: (a + b,), [(g, own), (ra, blk)],
               [(jax.ShapeDtypeStruct((4, r2, C), BF16), blk)], (4, nb), sp=place)[0]


def _chip_sum(name, g, ra, rb, place):
    _, R, C = g.shape
    r2 = R // 2
    tr = _row_tile(r2, C)
    nb = r2 // tr
    own = pl.BlockSpec((None, tr, C), lambda i, p: (p[1], p[0] * nb + i, 0))
    mine = pl.BlockSpec((None, tr, C), lambda i, p: (p[1], i, 0))
    src = [pl.BlockSpec((None, tr, C), functools.partial(lambda i, p, j: (j, i, 0), j=j)) for j in range(3)]
    out = pl.BlockSpec((None, tr, C), lambda i, p: (p[0], i, 0))

    def total(a, b, r0, r1, r2_):
        return ((((a + b) + r0.astype(F32)) + r1.astype(F32)) + r2_.astype(F32),)

    return _ew("rs_chip_sum_" + name, total, [(g, own), (ra, mine), (rb, src[0]), (rb, src[1]), (rb, src[2])],
               [(jax.ShapeDtypeStruct((2, r2, C), F32), out)], (nb,), sp=place)[0]


def _pack(arrays):
    rows, offs, pos = [], [], 0
    for a in arrays:
        flat = a.reshape(-1)
        n = -(-flat.shape[0] // (8 * LANES)) * (8 * LANES)
        if n != flat.shape[0]:
            flat = jnp.pad(flat, (0, n - flat.shape[0]))
        rows.append(flat.reshape(-1, LANES))
        offs.append(pos)
        pos += n // LANES
    return jnp.concatenate(rows, axis=0), offs


def _unpack(packed, offs, shapes):
    out = []
    for off, shp in zip(offs, shapes):
        n = 1
        for s in shp:
            n *= s
        out.append(packed[off:off + -(-n // LANES)].reshape(-1)[:n].reshape(shp))
    return out


def _sum8(gathered):
    _, R, C = gathered.shape
    tr = _row_tile(R, C, 256 * 1024)
    specs = [pl.BlockSpec((None, tr, C), functools.partial(lambda i, d: (d, i, 0), d=d)) for d in range(8)]

    def fn(*parts):
        t = parts[0]
        for p in parts[1:]:
            t = t + p
        return (t,)

    return _ew("small_sum", fn, [(gathered, s) for s in specs],
               [(jax.ShapeDtypeStruct((R, C), F32), pl.BlockSpec((tr, C), lambda i: (i, 0)))], (R // tr,))[0]


BIG = ("w_in", "w_out", "w_ffn_gate", "w_ffn_up", "w_ffn_down")
SMALL = ("ln1_w", "conv_w", "conv_b", "gate_a_w", "gate_a_b", "gate_x_w", "gate_x_b", "lru_lambda", "ret_gn_w", "ln2_w",
         "final_norm_w")
WEIGHTS = ("ln1_w", "w_in", "conv_w", "conv_b", "gate_a_w", "gate_a_b", "gate_x_w", "gate_x_b", "lru_lambda", "ret_gn_w",
           "w_out", "ln2_w", "w_ffn_gate", "w_ffn_up", "w_ffn_down", "final_norm_w")


def kernel(x, ln1_w, w_in, conv_w, conv_b, gate_a_w, gate_a_b, gate_x_w, gate_x_b, lru_lambda, ret_gn_w, w_out, ln2_w, w_ffn_gate, w_ffn_up, w_ffn_down, final_norm_w, loss_target, m_ln1_w, m_w_in, m_conv_w, m_conv_b, m_gate_a_w, m_gate_a_b, m_gate_x_w, m_gate_x_b, m_lru_lambda, m_ret_gn_w, m_w_out, m_ln2_w, m_w_ffn_gate, m_w_ffn_up, m_w_ffn_down, m_final_norm_w, v_ln1_w, v_w_in, v_conv_w, v_conv_b, v_gate_a_w, v_gate_a_b, v_gate_x_w, v_gate_x_b, v_lru_lambda, v_ret_gn_w, v_w_out, v_ln2_w, v_w_ffn_gate, v_w_ffn_up, v_w_ffn_down, v_final_norm_w):
    w = dict(ln1_w=ln1_w, w_in=w_in, conv_w=conv_w, conv_b=conv_b, gate_a_w=gate_a_w, gate_a_b=gate_a_b, gate_x_w=gate_x_w,
             gate_x_b=gate_x_b, lru_lambda=lru_lambda, ret_gn_w=ret_gn_w, w_out=w_out, ln2_w=ln2_w, w_ffn_gate=w_ffn_gate,
             w_ffn_up=w_ffn_up, w_ffn_down=w_ffn_down, final_norm_w=final_norm_w)
    m = dict(ln1_w=m_ln1_w, w_in=m_w_in, conv_w=m_conv_w, conv_b=m_conv_b, gate_a_w=m_gate_a_w, gate_a_b=m_gate_a_b,
             gate_x_w=m_gate_x_w, gate_x_b=m_gate_x_b, lru_lambda=m_lru_lambda, ret_gn_w=m_ret_gn_w, w_out=m_w_out,
             ln2_w=m_ln2_w, w_ffn_gate=m_w_ffn_gate, w_ffn_up=m_w_ffn_up, w_ffn_down=m_w_ffn_down,
             final_norm_w=m_final_norm_w)
    v = dict(ln1_w=v_ln1_w, w_in=v_w_in, conv_w=v_conv_w, conv_b=v_conv_b, gate_a_w=v_gate_a_w, gate_a_b=v_gate_a_b,
             gate_x_w=v_gate_x_w, gate_x_b=v_gate_x_b, lru_lambda=v_lru_lambda, ret_gn_w=v_ret_gn_w, w_out=v_w_out,
             ln2_w=v_ln2_w, w_ffn_gate=v_w_ffn_gate, w_ffn_up=v_w_ffn_up, w_ffn_down=v_w_ffn_down,
             final_norm_w=v_final_norm_w)
    xs, tgt = x[0], loss_target[0]
    S, D = xs.shape
    DL, DR = conv_b.shape[1], ret_gn_w.shape[1]
    assert DL == DR and DL % HEAD_DIM == 0 and S % CHUNK == 0
    d_mix = DL + DR
    cx, cy, cc = lax.axis_index("x"), lax.axis_index("y"), lax.axis_index("c")
    chip = 2 * cx + cy
    place = jnp.stack([cc, chip]).astype(jnp.int32)
    grad, delta, new_m, new_v = {}, {}, {}, {}

    def finish_big(n, full):
        shp = w[n].shape
        g2 = full.reshape(shp[1], shp[2])
        w2, m2, v2 = (t[n].reshape(shp[1], shp[2]) for t in (w, m, v))
        d_, m_, v_, g_ = _adamw_call("adamw_" + n, w2, g2, m2, v2)
        grad[n], delta[n], new_m[n], new_v[n] = (t.reshape(shp) for t in (g_, d_, m_, v_))

    def all_sum(gathered, own):
        return _sum8(lax.dynamic_update_slice(gathered, own[None], (4 * cx + 2 * cy + cc, 0, 0)))

    st = {n: _own_slab(n, w[n][0], place) for n in BIG}
    w_in_st, conv_st = _comm_call("gather_w_in", [_gather_both(st["w_in"]), _gather_conv(conv_w[0])])
    conv_st = lax.dynamic_update_slice(conv_st, conv_w, (chip, 0, 0))
    cw_cols = conv_st.shape[2]
    conv_full = jnp.transpose(conv_st, (1, 0, 2)).reshape(CONV_W, 4 * cw_cols)
    n_in, n_ff = w_in_st.shape[2], st["w_ffn_gate"].shape[2]
    tables = _ret_tables(S, DR // HEAD_DIM)
    wab, wxb = gate_a_w[0].astype(BF16), gate_x_w[0].astype(BF16)
    lru_w = (conv_full, conv_b, wab, gate_a_b, wxb, gate_x_b, lru_lambda)
    TM, TK = 512, 1024

    u1 = _rms_fwd("rms1", xs, ln1_w, TM)
    proj, (w_out_st, wg_st) = _mm_nn_stacked("proj", u1, w_in_st, F32, TM,
                                             comm=_merge([_gather_ici(st["w_out"]), _gather_ici(st["w_ffn_gate"])]))
    (hs, y), (w_out_st, wg_st, wu_st) = _lru_fwd(
        proj, *lru_w, 128, d_mix, comm=_merge([_gather_d2d(w_out_st), _gather_d2d(wg_st), _gather_ici(st["w_ffn_up"])]))
    (y, rprev), (wu_st,) = _ret_fwd(proj, y, tables, ret_gn_w, 256, comm=_gather_d2d(wu_st))
    w_out_f = w_out_st.reshape(d_mix, D)
    h1, (wd_st,) = _mm_nn("out_proj", y, w_out_f, xs, F32, TM, 1024, d_mix, comm=_gather_ici(st["w_ffn_down"]))
    u2 = _rms_fwd("rms2", h1, ln2_w, TM)
    (gt, up, ff), (wd_st,) = _ffn_gate_up(u2, wg_st, wu_st, TM, comm=_gather_d2d(wd_st))
    wd_f = wd_st.reshape(4 * n_ff, D)
    h2 = _mm_nn("ffn_down", ff, wd_f, h1, F32, TM, 1024, 4 * n_ff)
    dh2, dh2b, d_fw, loss = _loss_head(h2, tgt, final_norm_w.reshape(1, D), TM)

    g_wd = _mm_tn("g_w_down", ff, dh2b, n_ff, 1024, TK).reshape(4, n_ff, D)
    (dgt, dup), (ra_wd,) = _ffn_gate_up_bwd(dh2b, wd_f, gt, up, TM, n_ff, comm=_pair_exchange(g_wd))
    pb_wd = _pair_sum("w_ffn_down", g_wd, ra_wd, place)
    g_wg, (rb_wd,) = _mm_tn("g_w_gate", u2, dgt, 1024, None, TK, stacked_cols=n_ff, comm=_chip_exchange(pb_wd))
    slot_wd = _chip_sum("w_ffn_down", g_wd, ra_wd, rb_wd, place)
    g_wu, (full_wd, ra_wg) = _mm_tn("g_w_up", u2, dup, 1024, None, TK, stacked_cols=n_ff,
                                    comm=_merge([_pair_share(slot_wd), _pair_exchange(g_wg)]))
    finish_big("w_ffn_down", full_wd)
    pb_wg = _pair_sum("w_ffn_gate", g_wg, ra_wg, place)
    du2, (rb_wg, ra_wu) = _mm_nt_stacked("d_u2", [dgt, dup], [wg_st, wu_st], F32, 256, 1024,
                                         comm=_merge([_chip_exchange(pb_wg), _pair_exchange(g_wu)]))
    slot_wg = _chip_sum("w_ffn_gate", g_wg, ra_wg, rb_wg, place)
    pb_wu = _pair_sum("w_ffn_up", g_wu, ra_wu, place)
    (dh1, dh1b, d_ln2), (full_wg,) = _rms_bwd("rms2_bwd", h1, ln2_w, du2, dh2, TM, True, comm=_pair_share(slot_wg))
    finish_big("w_ffn_gate", full_wg)
    dy = _mm_nt("d_y", dh1b, w_out_f, F32, TM, 1024)
    g_wout = _mm_tn("g_w_out", y, dh1b, 1024, 1024, TK).reshape(4, d_mix // 4, D)
    (dproj, d_cw, d_cb, d_wa, d_ba, d_wx, d_bx, d_lam), (rb_wu, ra_wout) = _lru_bwd(
        proj, hs, dy, *lru_w, 128, comm=_merge([_chip_exchange(pb_wu), _pair_exchange(g_wout)]))
    slot_wu = _chip_sum("w_ffn_up", g_wu, ra_wu, rb_wu, place)
    pb_wout = _pair_sum("w_out", g_wout, ra_wout, place)
    (dproj, d_gn), (full_wu, rb_wout) = _ret_bwd(proj, rprev, dy, dproj, tables, ret_gn_w, 256,
                                                 comm=_merge([_pair_share(slot_wu), _chip_exchange(pb_wout)]))
    finish_big("w_ffn_up", full_wu)
    slot_wout = _chip_sum("w_out", g_wout, ra_wout, rb_wout, place)
    small = dict(conv_w=d_cw, conv_b=d_cb, gate_a_w=d_wa, gate_a_b=d_ba, gate_x_w=d_wx, gate_x_b=d_bx, lru_lambda=d_lam,
                 ret_gn_w=d_gn, ln2_w=d_ln2, final_norm_w=d_fw)
    packed, offs = _pack([small[n] for n in SMALL[1:]] + [loss])
    g_win, (full_wout, got_small) = _mm_tn("g_w_in", u1, dproj, 1024, None, TK, stacked_cols=n_in,
                                           comm=_merge([_pair_share(slot_wout), _gather_small(packed)]))
    finish_big("w_out", full_wout)
    du1, (ra_win,) = _mm_nt_stacked("d_u1", [dproj], [w_in_st], F32, 256, D, comm=_pair_exchange(g_win))
    pb_win = _pair_sum("w_in", g_win, ra_win, place)
    (gx, d_ln1), (rb_win,) = _rms_bwd("rms1_bwd", xs, ln1_w, du1, dh1, TM, False, comm=_chip_exchange(pb_win))
    slot_win = _chip_sum("w_in", g_win, ra_win, rb_win, place)
    packed1, _ = _pack([d_ln1])
    full_win, got_ln1 = _comm_call("reduce_tail", [_pair_share(slot_win), _gather_small(packed1)])
    finish_big("w_in", full_win)

    red = _unpack(all_sum(got_small, packed), offs, [small[n].shape for n in SMALL[1:]] + [(1, LANES)])
    g = dict(zip(SMALL[1:], red[:-1]))
    g["ln1_w"] = all_sum(got_ln1, packed1)[:-(-D // LANES)].reshape(1, D)
    loss_out = red[-1][0, 0]
    g["conv_w"] = lax.dynamic_slice(g["conv_w"], (0, chip * cw_cols), (CONV_W, cw_cols))
    packs = [_pack([t[n] for n in SMALL])[0] for t in (w, m, v)]
    gp, offs2 = _pack([g[n] for n in SMALL])
    outs = _adamw_call("adamw_small", packs[0], gp, packs[1], packs[2])
    shapes = [w[n].shape for n in SMALL]
    for dst, arr in zip((delta, new_m, new_v), outs):
        dst.update(zip(SMALL, _unpack(arr, offs2, shapes)))
    for n in SMALL:
        grad[n] = g[n].reshape(w[n].shape)

    return (loss_out, gx.reshape(x.shape), *[grad[n] for n in WEIGHTS], *[delta[n] for n in WEIGHTS],
            *[new_m[n] for n in WEIGHTS], *[new_v[n] for n in WEIGHTS])
```

```python
import functools

import jax
import jax.numpy as jnp
from jax import lax
from jax.experimental import pallas as pl
from jax.experimental.pallas import tpu as pltpu

F32 = jnp.float32
BF16 = jnp.bfloat16
MESH = pl.DeviceIdType.MESH

EPS = 1e-6
LRU_C = 8.0
ROPE_BASE = 10000.0
CHUNK = 128
HEAD_DIM = 128
CONV_W = 4
ADAM_LR = 0.001
ADAM_B1 = 0.9
ADAM_B2 = 0.999
ADAM_EPS = 1e-08
ADAM_WD = 0.01
ADAM_STEP = 10

V7X_VMEM_BYTES = 64 * 1024 * 1024
VMEM_LIMIT = V7X_VMEM_BYTES - 8 * 1024 * 1024
LANES = 128
SUBLANES_16BIT = 16

NN = (((1,), (0,)), ((), ()))
NT = (((1,), (1,)), ((), ()))
TN = (((0,), (0,)), ((), ()))


def _dot(a, b, dims=NN):
    return lax.dot_general(a, b, dims, preferred_element_type=F32)


def _tile(n, pref, mult=SUBLANES_16BIT):
    best = None
    t = mult
    while t <= min(n, pref):
        if n % t == 0:
            best = t
        t += mult
    return best if best is not None else n


def _row_tile(rows, cols, budget_bytes=2 * 1024 * 1024):
    return _tile(rows, max(SUBLANES_16BIT, budget_bytes // (cols * 4)))


def _params(sem):
    return pltpu.CompilerParams(dimension_semantics=sem, vmem_limit_bytes=VMEM_LIMIT)


HBM_SPEC = pl.BlockSpec(memory_space=pl.ANY)


class _Comm:
    def __init__(self, ins, outs, aliases, n_sem, start, finish):
        self.ins, self.outs, self.aliases, self.n_sem, self.start, self.finish = ins, outs, aliases, n_sem, start, finish


def _merge(tasks):
    ins, outs, aliases, plans, n_sem = [], [], {}, [], 0
    for t in tasks:
        i0, o0 = len(ins), len(outs)
        plans.append((t, i0, o0, n_sem))
        ins += t.ins
        outs += t.outs
        aliases.update({i0 + a: o0 + b for a, b in t.aliases.items()})
        n_sem += t.n_sem

    def run(which):
        def go(cins, couts, ssem, rsem, base):
            for t, i0, o0, s0 in plans:
                getattr(t, which)(cins[i0:i0 + len(t.ins)], couts[o0:o0 + len(t.outs)], ssem, rsem, base + s0)
        return go

    return _Comm(ins, outs, aliases, n_sem, run("start"), run("finish"))


def _pcall(body, *, name, grid, in_specs, out_specs, out_shape, operands, scratch_shapes=(), aliases=None, comm=None):
    n_in, n_out, n_scr = len(operands), len(out_shape), len(scratch_shapes)
    aliases = dict(aliases or {})
    params = _params(("arbitrary",) * len(grid))
    if comm is None:
        return pl.pallas_call(body, grid=grid, in_specs=list(in_specs), out_specs=list(out_specs), out_shape=list(out_shape),
                              scratch_shapes=list(scratch_shapes), input_output_aliases=aliases, name=name,
                              compiler_params=params)(*operands)
    nci, nco = len(comm.ins), len(comm.outs)

    def wrapped(*refs):
        ins, cins = refs[:n_in], refs[n_in:n_in + nci]
        o0 = n_in + nci
        outs, couts = refs[o0:o0 + n_out], refs[o0 + n_out:o0 + n_out + nco]
        s0 = o0 + n_out + nco
        scr, (ssem, rsem) = refs[s0:s0 + n_scr], refs[s0 + n_scr:]
        ids = [pl.program_id(a) for a in range(len(grid))]
        first = functools.reduce(jnp.logical_and, [i == 0 for i in ids])
        last = functools.reduce(jnp.logical_and, [i == g - 1 for i, g in zip(ids, grid)])

        @pl.when(first)
        def _():
            comm.start(cins, couts, ssem, rsem, 0)

        body(*ins, *outs, *scr)

        @pl.when(last)
        def _():
            comm.finish(cins, couts, ssem, rsem, 0)

    aliases.update({n_in + a: n_out + b for a, b in comm.aliases.items()})
    res = pl.pallas_call(
        wrapped, grid=grid, in_specs=list(in_specs) + [HBM_SPEC] * nci, out_specs=list(out_specs) + [HBM_SPEC] * nco,
        out_shape=list(out_shape) + list(comm.outs),
        scratch_shapes=list(scratch_shapes) + [pltpu.SemaphoreType.DMA((comm.n_sem,)), pltpu.SemaphoreType.DMA((comm.n_sem,))],
        input_output_aliases=aliases, name=name, compiler_params=params)(*operands, *comm.ins)
    return res[:n_out], res[n_out:]


def _ew(name, fn, ins, outs, grid, sp=None):
    n_in = len(ins)

    def body(*refs):
        if sp is not None:
            refs = refs[1:]
        vals = [r[...] for r in refs[:n_in]]
        res = fn(*vals)
        for o_ref, v in zip(refs[n_in:], res):
            o_ref[...] = v.astype(o_ref.dtype)

    in_specs = [s for _, s in ins]
    out_specs = [s for _, s in outs]
    out_shape = [s for s, _ in outs]
    sem = ("arbitrary",) * len(grid)
    if sp is None:
        return pl.pallas_call(body, grid=grid, in_specs=in_specs, out_specs=out_specs, out_shape=out_shape,
                              name=name, compiler_params=_params(sem))(*[a for a, _ in ins])
    gs = pltpu.PrefetchScalarGridSpec(num_scalar_prefetch=1, grid=grid, in_specs=in_specs, out_specs=out_specs)
    return pl.pallas_call(body, grid_spec=gs, out_shape=out_shape, name=name,
                          compiler_params=_params(sem))(sp, *[a for a, _ in ins])


def _matmul(name, pairs, dims, grid, out_shape, out_spec, acc_shape, res=None, comm=None):
    n = len(pairs)
    nk = grid[2]

    def body(*refs):
        ab = refs[:2 * n]
        pos = 2 * n
        res_ref = None
        if res is not None:
            res_ref = refs[pos]
            pos += 1
        o_ref = refs[pos]
        acc_ref = refs[pos + 1] if nk > 1 else None

        def partial():
            t = None
            for p in range(n):
                d = _dot(ab[2 * p][...], ab[2 * p + 1][...], dims)
                t = d if t is None else t + d
            return t

        def finish(t):
            if res_ref is not None:
                t = t + res_ref[...]
            o_ref[...] = t.astype(o_ref.dtype)

        if nk == 1:
            finish(partial())
        else:
            k = pl.program_id(2)

            @pl.when(k == 0)
            def _():
                acc_ref[...] = partial()

            @pl.when(k > 0)
            def _():
                acc_ref[...] += partial()

            @pl.when(k == nk - 1)
            def _():
                finish(acc_ref[...])

    operands, in_specs = [], []
    for a, a_spec, b, b_spec in pairs:
        operands += [a, b]
        in_specs += [a_spec, b_spec]
    if res is not None:
        operands.append(res[0])
        in_specs.append(res[1])
    scratch = [pltpu.VMEM(acc_shape, F32)] if nk > 1 else []
    res = _pcall(body, name=name, grid=grid, in_specs=in_specs, out_specs=[out_spec], out_shape=[out_shape],
                 operands=operands, scratch_shapes=scratch, comm=comm)
    return res[0] if comm is None else (res[0][0], res[1])


def _mm_nn_stacked(name, a, b_st, out_dtype, tm, comm=None):
    M, K = a.shape
    J, _, Nj = b_st.shape
    tm = _tile(M, tm)
    return _matmul(
        name, [(a, pl.BlockSpec((tm, K), lambda j, i, k: (i, 0)), b_st, pl.BlockSpec((None, K, Nj), lambda j, i, k: (j, 0, 0)))],
        NN, (J, M // tm, 1), jax.ShapeDtypeStruct((M, J * Nj), out_dtype), pl.BlockSpec((tm, Nj), lambda j, i, k: (i, j)), None,
        comm=comm)


def _mm_nt(name, a, b, out_dtype, tm, tn):
    M, K = a.shape
    N = b.shape[0]
    tm, tn = _tile(M, tm), _tile(N, tn, LANES)
    return _matmul(
        name, [(a, pl.BlockSpec((tm, K), lambda j, i, k: (i, 0)), b, pl.BlockSpec((tn, K), lambda j, i, k: (j, 0)))],
        NT, (N // tn, M // tm, 1), jax.ShapeDtypeStruct((M, N), out_dtype), pl.BlockSpec((tm, tn), lambda j, i, k: (i, j)), None)


def _mm_nt_stacked(name, a_list, b_list, out_dtype, tm, tn, comm=None):
    n = len(a_list)
    M = a_list[0].shape[0]
    J, N, Nj = b_list[0].shape
    tm, tn = _tile(M, tm), _tile(N, tn, LANES)

    def body(*refs):
        o_ref = refs[2 * n]
        t = None
        for p in range(n):
            a_ref, b_ref = refs[p], refs[n + p]
            for s in range(J):
                d = _dot(a_ref[:, s * Nj:(s + 1) * Nj], b_ref[s], NT)
                t = d if t is None else t + d
        o_ref[...] = t.astype(o_ref.dtype)

    a_spec = pl.BlockSpec((tm, J * Nj), lambda j, i: (i, 0))
    b_spec = pl.BlockSpec((J, tn, Nj), lambda j, i: (0, j, 0), pipeline_mode=pl.Buffered(1))
    res = _pcall(body, name=name, grid=(N // tn, M // tm), in_specs=[a_spec] * n + [b_spec] * n,
                 out_specs=[pl.BlockSpec((tm, tn), lambda j, i: (i, j))], out_shape=[jax.ShapeDtypeStruct((M, N), out_dtype)],
                 operands=[*a_list, *b_list], comm=comm)
    return res[0] if comm is None else (res[0][0], res[1])


MXU_COLUMNS = 256


def _col_blocks(n):
    return [slice(s, min(s + MXU_COLUMNS, n)) for s in range(0, n, MXU_COLUMNS)]


def _ffn_gate_up(u2, wg_st, wu_st, tm, comm=None):
    S, D = u2.shape
    J, _, Nj = wg_st.shape
    tm = _tile(S, tm)

    def body(a_ref, wg_ref, wu_ref, gt_ref, up_ref, ff_ref):
        a = a_ref[...]
        for cols in _col_blocks(Nj):
            g = _dot(a, wg_ref[:, cols])
            u = _dot(a, wu_ref[:, cols])
            gt_ref[:, cols] = g.astype(BF16)
            up_ref[:, cols] = u.astype(BF16)
            ff_ref[:, cols] = (g * jax.nn.sigmoid(g) * u).astype(BF16)

    w_spec = pl.BlockSpec((None, D, Nj), lambda j, i: (j, 0, 0))
    o_spec = pl.BlockSpec((tm, Nj), lambda j, i: (i, j))
    o = jax.ShapeDtypeStruct((S, J * Nj), BF16)
    return _pcall(body, name="ffn_gate_up", grid=(J, S // tm),
                  in_specs=[pl.BlockSpec((tm, D), lambda j, i: (i, 0)), w_spec, w_spec],
                  out_specs=[o_spec, o_spec, o_spec], out_shape=[o, o, o], operands=[u2, wg_st, wu_st], comm=comm)


def _ffn_gate_up_bwd(dh2b, wd, gt, up, tm, tn, comm=None):
    S, D = dh2b.shape
    F = wd.shape[0]
    tm, tn = _tile(S, tm), _tile(F, tn, LANES)

    def body(a_ref, wd_ref, gt_ref, up_ref, dgt_ref, dup_ref):
        a = a_ref[...]
        for cols in _col_blocks(tn):
            d = _dot(a, wd_ref[cols, :], NT)
            g = gt_ref[:, cols].astype(F32)
            u = up_ref[:, cols].astype(F32)
            sg = jax.nn.sigmoid(g)
            dgt_ref[:, cols] = (d * u * (sg * (1.0 + g * (1.0 - sg)))).astype(BF16)
            dup_ref[:, cols] = (d * (g * sg)).astype(BF16)

    blk = pl.BlockSpec((tm, tn), lambda j, i: (i, j))
    o = jax.ShapeDtypeStruct((S, F), BF16)
    return _pcall(body, name="ffn_gate_up_bwd", grid=(F // tn, S // tm),
                  in_specs=[pl.BlockSpec((tm, D), lambda j, i: (i, 0)), pl.BlockSpec((tn, D), lambda j, i: (j, 0)), blk, blk],
                  out_specs=[blk, blk], out_shape=[o, o], operands=[dh2b, wd, gt, up], comm=comm)


def _mm_tn(name, a, b, tmo, tn, tk, stacked_cols=None, comm=None):
    S, Mo = a.shape
    N = b.shape[1]
    tmo, tk = _tile(Mo, tmo, LANES), _tile(S, tk)
    if stacked_cols is None:
        tn = _tile(N, tn, LANES)
        out_shape = jax.ShapeDtypeStruct((Mo, N), F32)
        out_spec = pl.BlockSpec((tmo, tn), lambda i, j, k: (i, j))
    else:
        tn = stacked_cols
        out_shape = jax.ShapeDtypeStruct((N // tn, Mo, tn), F32)
        out_spec = pl.BlockSpec((None, tmo, tn), lambda i, j, k: (j, i, 0))
    return _matmul(
        name, [(a, pl.BlockSpec((tk, tmo), lambda i, j, k: (k, i)), b, pl.BlockSpec((tk, tn), lambda i, j, k: (k, j)))],
        TN, (Mo // tmo, N // tn, S // tk), out_shape, out_spec, (tmo, tn), comm=comm)


def _rms_fwd(name, x, w, tm):
    S, D = x.shape
    tm = _tile(S, tm)

    def fn(xv, wv):
        r = lax.rsqrt(jnp.mean(xv * xv, axis=-1, keepdims=True) + EPS)
        return ((xv * r) * wv,)

    row = pl.BlockSpec((tm, D), lambda i: (i, 0))
    return _ew(name, fn, [(x, row), (w, pl.BlockSpec((1, D), lambda i: (0, 0)))],
               [(jax.ShapeDtypeStruct((S, D), BF16), row)], (S // tm,))[0]


def _rms_bwd(name, x, w, dy, dres, tm, want_bf16, comm=None):
    S, D = x.shape
    tm = _tile(S, tm)

    def body(x_ref, w_ref, dy_ref, dres_ref, dx_ref, *rest):
        dw_ref = rest[-1]
        i = pl.program_id(0)

        @pl.when(i == 0)
        def _():
            dw_ref[...] = jnp.zeros_like(dw_ref)

        xv = x_ref[...]
        r = lax.rsqrt(jnp.mean(xv * xv, axis=-1, keepdims=True) + EPS)
        nv = xv * r
        dyv = dy_ref[...]
        dn = dyv * w_ref[...]
        dw_ref[...] += jnp.sum(dyv * nv, axis=0, keepdims=True)
        dx = dres_ref[...] + r * (dn - nv * jnp.mean(dn * nv, axis=-1, keepdims=True))
        dx_ref[...] = dx
        if want_bf16:
            rest[0][...] = dx.astype(BF16)

    row = pl.BlockSpec((tm, D), lambda i: (i, 0))
    vec = pl.BlockSpec((1, D), lambda i: (0, 0))
    out_shape = [jax.ShapeDtypeStruct((S, D), F32)] + ([jax.ShapeDtypeStruct((S, D), BF16)] if want_bf16 else []) + \
                [jax.ShapeDtypeStruct((1, D), F32)]
    out_specs = [row] + ([row] if want_bf16 else []) + [vec]
    return _pcall(body, name=name, grid=(S // tm,), in_specs=[row, vec, row, row], out_specs=out_specs, out_shape=out_shape,
                  operands=[x, w, dy, dres], comm=comm)


def _out_proj_rms(y, w_out, x, ln_w, tm, comm=None):
    S, K = y.shape
    D = w_out.shape[1]
    tm = _tile(S, tm)

    def body(a_ref, w_ref, x_ref, lw_ref, h_ref, u_ref):
        hv = _dot(a_ref[...], w_ref[...]) + x_ref[...]
        h_ref[...] = hv
        r = lax.rsqrt(jnp.mean(hv * hv, axis=-1, keepdims=True) + EPS)
        u_ref[...] = ((hv * r) * lw_ref[...]).astype(BF16)

    row = pl.BlockSpec((tm, D), lambda i: (i, 0))
    return _pcall(
        body, name="out_proj", grid=(S // tm,),
        in_specs=[pl.BlockSpec((tm, K), lambda i: (i, 0)),
                  pl.BlockSpec((K, D), lambda i: (0, 0), pipeline_mode=pl.Buffered(1)), row,
                  pl.BlockSpec((1, D), lambda i: (0, 0))],
        out_specs=[row, row], out_shape=[jax.ShapeDtypeStruct((S, D), F32), jax.ShapeDtypeStruct((S, D), BF16)],
        operands=[y, w_out, x, ln_w], comm=comm)


def _ffn_down_loss(ff, wd, h1, tgt, fw, tm):
    S, K = ff.shape
    D = wd.shape[1]
    tm = _tile(S, tm)

    def body(a_ref, wd_ref, h1_ref, t_ref, w_ref, dh_ref, dhb_ref, dw_ref, loss_ref):
        i = pl.program_id(0)

        @pl.when(i == 0)
        def _():
            dw_ref[...] = jnp.zeros_like(dw_ref)
            loss_ref[...] = jnp.zeros_like(loss_ref)

        hv = _dot(a_ref[...], wd_ref[...]) + h1_ref[...]
        wv = w_ref[...]
        r = lax.rsqrt(jnp.mean(hv * hv, axis=-1, keepdims=True) + EPS)
        nv = hv * r
        err = nv * wv - t_ref[...]
        row_loss = jnp.mean(err * err, axis=-1, keepdims=True)
        loss_ref[...] += 0.5 * jnp.sum(row_loss, axis=0, keepdims=True)
        dyo = err * (1.0 / D)
        dn = dyo * wv
        dw_ref[...] += jnp.sum(dyo * nv, axis=0, keepdims=True)
        dh = r * (dn - nv * jnp.mean(dn * nv, axis=-1, keepdims=True))
        dh_ref[...] = dh
        dhb_ref[...] = dh.astype(BF16)

    row = pl.BlockSpec((tm, D), lambda i: (i, 0))
    vec = pl.BlockSpec((1, D), lambda i: (0, 0))
    return _pcall(
        body, name="ffn_down_loss", grid=(S // tm,),
        in_specs=[pl.BlockSpec((tm, K), lambda i: (i, 0)),
                  pl.BlockSpec((K, D), lambda i: (0, 0), pipeline_mode=pl.Buffered(1)), row, row, vec],
        out_specs=[row, row, vec, pl.BlockSpec((1, LANES), lambda i: (0, 0))],
        out_shape=[jax.ShapeDtypeStruct((S, D), F32), jax.ShapeDtypeStruct((S, D), BF16),
                   jax.ShapeDtypeStruct((1, D), F32), jax.ShapeDtypeStruct((1, LANES), F32)],
        operands=[ff, wd, h1, tgt, fw])


def _shift_down(x, d, head8):
    r = pltpu.roll(x, d, 0)
    rh = pltpu.roll(head8, d, 0)
    row8 = lax.broadcasted_iota(jnp.int32, head8.shape, 0)
    top = jnp.where(row8 < d, rh, r[0:8])
    return jnp.concatenate([top, r[8:]], axis=0)


def _shift_up(x, d, tail8):
    n = x.shape[0]
    r = pltpu.roll(x, n - d, 0)
    rt = pltpu.roll(tail8, 8 - d, 0)
    row8 = lax.broadcasted_iota(jnp.int32, tail8.shape, 0)
    bot = jnp.where(row8 + d >= 8, rt, r[n - 8:n])
    return jnp.concatenate([r[:n - 8], bot], axis=0)


def _log_sigmoid(lam):
    z = jnp.exp(-jnp.abs(lam))
    u = 1.0 + z
    log1p = jnp.where(u == 1.0, z, jnp.log(u) * (z / jnp.where(u == 1.0, 1.0, u - 1.0)))
    return jnp.minimum(lam, 0.0) - log1p


def _neg_expm1(z, exp_z):
    series = -z * (1.0 + z * (0.5 + z * (1.0 / 6.0)))
    return jnp.where(z > -0.02, series, 1.0 - exp_z)


_GELU_C = 0.7978845608028654


def _gelu(x):
    t = jnp.tanh(_GELU_C * (x + 0.044715 * (x * x * x)))
    return x * (0.5 * (1.0 + t)), t


def _gelu_grad(x, t):
    return 0.5 * (1.0 + t) + 0.5 * x * (1.0 - t * t) * (_GELU_C * (1.0 + 3.0 * 0.044715 * (x * x)))


def _lru_gates(lx, head8, cw, cb, wa_ref, ba, wx_ref, bx, ls):
    nb = wa_ref.shape[0]
    sh = [lx] + [_shift_down(lx, d, head8) for d in (1, 2, 3)]
    cx = cb + sh[3] * cw[0:1]
    cx = cx + sh[2] * cw[1:2]
    cx = cx + sh[1] * cw[2:3]
    cx = cx + sh[0] * cw[3:4]
    cxb = cx.astype(BF16)
    ra = jnp.concatenate([_dot(cxb[:, n * HEAD_DIM:(n + 1) * HEAD_DIM], wa_ref[n]) for n in range(nb)], axis=1) + ba
    ia = jnp.concatenate([_dot(cxb[:, n * HEAD_DIM:(n + 1) * HEAD_DIM], wx_ref[n]) for n in range(nb)], axis=1) + bx
    r = jax.nn.sigmoid(ra)
    ig = jax.nn.sigmoid(ia)
    log_a = LRU_C * r * ls
    a = jnp.exp(log_a)
    m2 = _neg_expm1(2.0 * log_a, a * a)
    return sh, cx, cxb, r, ig, a, m2, jnp.sqrt(m2)


def _lru_specs(tl, DL):
    nb = DL // HEAD_DIM
    vec = pl.BlockSpec((1, DL), lambda i: (0, 0))
    return [pl.BlockSpec((CONV_W, DL), lambda i: (0, 0)), vec,
            pl.BlockSpec((nb, HEAD_DIM, HEAD_DIM), lambda i: (0, 0, 0)), vec,
            pl.BlockSpec((nb, HEAD_DIM, HEAD_DIM), lambda i: (0, 0, 0)), vec, vec]


def _lru_fwd(proj, cw, cb, wa, ba, wx, bx, lam, tl, d_mix, comm=None):
    S = proj.shape[0]
    DL = cb.shape[1]
    tl = _tile(S, tl)

    def body(lx_ref, lg_ref, cw_ref, cb_ref, wa_ref, ba_ref, wx_ref, bx_ref, lam_ref, h_ref, y_ref, prev8, hc, a_s, b_s):
        i = pl.program_id(0)

        @pl.when(i == 0)
        def _():
            prev8[...] = jnp.zeros_like(prev8)
            hc[...] = jnp.zeros_like(hc)

        lx = lx_ref[...]
        ls = _log_sigmoid(lam_ref[...])
        _, cx, _, _, ig, a, _, mult = _lru_gates(lx, prev8[...], cw_ref[...], cb_ref[...], wa_ref, ba_ref[...],
                                                 wx_ref, bx_ref[...], ls)
        b = mult * (ig * cx)
        row = lax.broadcasted_iota(jnp.int32, a.shape, 0) & 7
        for d in (1, 2, 4):
            a_sh = pltpu.roll(a, d, 0)
            b_sh = pltpu.roll(b, d, 0)
            m = row >= d
            b = jnp.where(m, a * b_sh + b, b)
            a = jnp.where(m, a * a_sh, a)
        a_s[...] = a
        b_s[...] = b

        def step(g, hprev):
            sl = pl.ds(pl.multiple_of(g * 8, 8), 8)
            hh = a_s[sl, :] * hprev + b_s[sl, :]
            h_ref[sl, :] = hh
            return hh[7:8, :]

        hc[0:1, :] = lax.fori_loop(0, tl // 8, step, hc[0:1, :])
        prev8[...] = lx[tl - 8:tl]
        g, _ = _gelu(lg_ref[...])
        y_ref[...] = (h_ref[...] * g).astype(BF16)

    return _pcall(
        body, name="lru_fwd", grid=(S // tl,),
        in_specs=[pl.BlockSpec((tl, DL), lambda i: (i, 0)), pl.BlockSpec((tl, DL), lambda i: (i, 1))] + _lru_specs(tl, DL),
        out_specs=[pl.BlockSpec((tl, DL), lambda i: (i, 0)), pl.BlockSpec((tl, DL), lambda i: (i, 0))],
        out_shape=[jax.ShapeDtypeStruct((S, DL), F32), jax.ShapeDtypeStruct((S, d_mix), BF16)],
        scratch_shapes=[pltpu.VMEM((8, DL), F32), pltpu.VMEM((8, DL), F32), pltpu.VMEM((tl, DL), F32), pltpu.VMEM((tl, DL), F32)],
        operands=[proj, proj, cw, cb, wa, ba, wx, bx, lam], comm=comm)


def _lru_bwd(proj, h, dy, cw, cb, wa, ba, wx, bx, lam, tl, comm=None):
    S = proj.shape[0]
    DL = cb.shape[1]
    nb = DL // HEAD_DIM
    tl = _tile(S, tl)
    nt = S // tl
    ng = tl // 8
    t8 = tl // 8

    def body(lx_ref, lxp_ref, lg_ref, h_ref, hp_ref, dy_ref, cw_ref, cb_ref, wa_ref, ba_ref, wx_ref, bx_ref, lam_ref,
             dlxg_ref, dcw_ref, dcb_ref, dwa_ref, dba_ref, dwx_ref, dbx_ref, dlam_ref,
             a_next, g_carry, dcx_next, an_s, dh_s, g_s):
        i = pl.program_id(0)

        @pl.when(i == 0)
        def _():
            for ref in (dcw_ref, dcb_ref, dwa_ref, dba_ref, dwx_ref, dbx_ref, dlam_ref, a_next, g_carry, dcx_next):
                ref[...] = jnp.zeros_like(ref)

        first = i == nt - 1
        lx = lx_ref[...]
        hv = h_ref[...]
        lg = lg_ref[...]
        dyv = dy_ref[...]
        head8 = jnp.where(first, 0.0, lxp_ref[...])
        hhead8 = jnp.where(first, 0.0, hp_ref[...])
        lamv = lam_ref[...]
        ls = _log_sigmoid(lamv)
        cwv = cw_ref[...]
        sh, cx, cxb, r, ig, a, m2, mult = _lru_gates(lx, head8, cwv, cb_ref[...], wa_ref, ba_ref[...], wx_ref, bx_ref[...],
                                                     ls)
        hprev = _shift_down(hv, 1, hhead8)
        g, t = _gelu(lg)
        dlg = dyv * hv * _gelu_grad(lg, t)
        dh = dyv * g
        an = _shift_up(a, 1, a_next[...])
        row = lax.broadcasted_iota(jnp.int32, a.shape, 0) & 7
        for d in (1, 2, 4):
            an_sh = pltpu.roll(an, tl - d, 0)
            dh_sh = pltpu.roll(dh, tl - d, 0)
            m = row + d < 8
            dh = jnp.where(m, an * dh_sh + dh, dh)
            an = jnp.where(m, an * an_sh, an)
        an_s[...] = an
        dh_s[...] = dh

        def step(k, gc):
            sl = pl.ds(pl.multiple_of((ng - 1 - k) * 8, 8), 8)
            gg = an_s[sl, :] * gc + dh_s[sl, :]
            g_s[sl, :] = gg
            return gg[0:1, :]

        g_carry[0:1, :] = lax.fori_loop(0, ng, step, g_carry[0:1, :])
        a_next[...] = a[0:8]
        G = g_s[...]
        da = G * hprev
        icx = ig * cx
        dmult = G * icx
        dicx = G * mult
        di = dicx * cx
        dcx = dicx * ig
        dlog = da * a - dmult * ((a * a) * lax.rsqrt(m2))
        dr = dlog * (LRU_C * ls)
        dlam_ref[...] += jnp.sum(dlog * (LRU_C * r), axis=0, keepdims=True)
        dra = dr * r * (1.0 - r)
        dia = di * ig * (1.0 - ig)
        dba_ref[...] += jnp.sum(dra, axis=0, keepdims=True)
        dbx_ref[...] += jnp.sum(dia, axis=0, keepdims=True)
        drab = dra.astype(BF16)
        diab = dia.astype(BF16)
        back = []
        for n in range(nb):
            cs = slice(n * HEAD_DIM, (n + 1) * HEAD_DIM)
            dwa_ref[n] += _dot(cxb[:, cs], drab[:, cs], TN)
            dwx_ref[n] += _dot(cxb[:, cs], diab[:, cs], TN)
            back.append(_dot(drab[:, cs], wa_ref[n], NT) + _dot(diab[:, cs], wx_ref[n], NT))
        dcx = dcx + jnp.concatenate(back, axis=1)
        dcb_ref[...] += jnp.sum(dcx, axis=0, keepdims=True)
        for tap in range(CONV_W):
            dcw_ref[tap:tap + 1, :] += jnp.sum(dcx * sh[CONV_W - 1 - tap], axis=0, keepdims=True)
        tail = dcx_next[...]
        dlx = dcx * cwv[3:4]
        for d in (1, 2, 3):
            dlx = dlx + _shift_up(dcx, d, tail) * cwv[3 - d:4 - d]
        dcx_next[...] = dcx[0:8]
        dlxg_ref[:, 0:DL] = dlx.astype(BF16)
        dlxg_ref[:, DL:2 * DL] = dlg.astype(BF16)

        @pl.when(i == nt - 1)
        def _():
            dlam_ref[...] = dlam_ref[...] * (1.0 - jax.nn.sigmoid(lamv))

    rev = lambda i: nt - 1 - i
    prev8_map = lambda i: (jnp.maximum((nt - 1 - i) * t8 - 1, 0), 0)
    vec = pl.BlockSpec((1, DL), lambda i: (0, 0))
    mat = pl.BlockSpec((nb, HEAD_DIM, HEAD_DIM), lambda i: (0, 0, 0))
    return _pcall(
        body, name="lru_bwd", grid=(nt,), operands=[proj, proj, proj, h, h, dy, cw, cb, wa, ba, wx, bx, lam], comm=comm,
        in_specs=[pl.BlockSpec((tl, DL), lambda i: (rev(i), 0)), pl.BlockSpec((8, DL), prev8_map),
                  pl.BlockSpec((tl, DL), lambda i: (rev(i), 1)),
                  pl.BlockSpec((tl, DL), lambda i: (rev(i), 0)), pl.BlockSpec((8, DL), prev8_map),
                  pl.BlockSpec((tl, DL), lambda i: (rev(i), 0))] + _lru_specs(tl, DL),
        out_specs=[pl.BlockSpec((tl, 2 * DL), lambda i: (rev(i), 0)), pl.BlockSpec((CONV_W, DL), lambda i: (0, 0)), vec,
                   mat, vec, mat, vec, vec],
        out_shape=[jax.ShapeDtypeStruct(proj.shape, BF16), jax.ShapeDtypeStruct((CONV_W, DL), F32),
                   jax.ShapeDtypeStruct((1, DL), F32), jax.ShapeDtypeStruct((nb, HEAD_DIM, HEAD_DIM), F32),
                   jax.ShapeDtypeStruct((1, DL), F32), jax.ShapeDtypeStruct((nb, HEAD_DIM, HEAD_DIM), F32),
                   jax.ShapeDtypeStruct((1, DL), F32), jax.ShapeDtypeStruct((1, DL), F32)],
        scratch_shapes=[pltpu.VMEM((8, DL), F32), pltpu.VMEM((8, DL), F32), pltpu.VMEM((8, DL), F32),
                        pltpu.VMEM((tl, DL), F32), pltpu.VMEM((tl, DL), F32), pltpu.VMEM((tl, DL), F32)])


def _ret_tables(S, H):
    pos = jnp.arange(S, dtype=F32)
    inv_freq = ROPE_BASE ** (-jnp.arange(0, HEAD_DIM, 2, dtype=F32) / HEAD_DIM)
    ang = pos[:, None] * inv_freq[None, :]
    cos, sin = jnp.cos(ang), jnp.sin(ang)
    cosf = jnp.concatenate([cos, cos], axis=1)
    sins = jnp.concatenate([-sin, sin], axis=1)
    log_gamma = jnp.log1p(-jnp.exp2(-5.0 - jnp.arange(H, dtype=F32)))
    idx = jnp.arange(CHUNK)
    diff = idx[:, None] - idx[None, :]
    causal = diff >= 0
    decay = jnp.where(causal[None], jnp.exp(log_gamma[:, None, None] * jnp.where(causal, diff, 0)[None].astype(F32)), 0.0)
    zeta = jnp.exp(log_gamma[:, None] * (CHUNK - 1 - idx).astype(F32)[None, :])
    xi = jnp.exp(log_gamma[:, None] * (idx + 1).astype(F32)[None, :])
    gc = jnp.exp(log_gamma * CHUNK)
    lanes = (H, CHUNK, HEAD_DIM)
    return (cosf, sins, decay, jnp.broadcast_to(zeta[:, :, None], lanes), jnp.broadcast_to(xi[:, :, None], lanes),
            jnp.broadcast_to(gc[:, None, None], lanes))


def _rope(t, cos, sin_signed):
    return t * cos + pltpu.roll(t, HEAD_DIM // 2, 1) * sin_signed


def _rope_t(d, cos, sin_signed):
    return d * cos + pltpu.roll(d * sin_signed, HEAD_DIM // 2, 1)


def _ret_const_specs(H, DR):
    full = pl.BlockSpec((H, CHUNK, HEAD_DIM), lambda *_: (0, 0, 0))
    return [full, full, full, full, pl.BlockSpec((1, DR), lambda *_: (0, 0))]


def _ret_fwd(proj, y, tables, gnw, tb, comm=None):
    S = proj.shape[0]
    DR = gnw.shape[1]
    H = DR // HEAD_DIM
    tb = _tile(S, tb, CHUNK)
    nc = tb // CHUNK
    cosf, sins, dm, zeta, xi, gc = tables
    scale = HEAD_DIM ** -0.5

    def body(qk_ref, vg_ref, cos_ref, sin_ref, dm_ref, zeta_ref, xi_ref, gc_ref, gnw_ref, y_in, y_ref, rprev_ref, r_s):
        del y_in
        i = pl.program_id(0)

        @pl.when(i == 0)
        def _():
            r_s[...] = jnp.zeros_like(r_s)

        def chunk(c, carry):
            rows = pl.ds(pl.multiple_of(c * CHUNK, CHUNK), CHUNK)
            cos = cos_ref[rows, :]
            sin = sin_ref[rows, :]
            heads = range(H)
            c0 = [slice(h * HEAD_DIM, (h + 1) * HEAD_DIM) for h in heads]
            c1 = [slice(DR + h * HEAD_DIM, DR + (h + 1) * HEAD_DIM) for h in heads]
            qh = [_rope(qk_ref[rows, c0[h]], cos, sin) for h in heads]
            kh = [_rope(qk_ref[rows, c1[h]], cos, sin) * scale for h in heads]
            vb = [vg_ref[rows, c0[h]].astype(BF16) for h in heads]
            rp = [r_s[h] for h in heads]
            rpb = [rp[h].astype(BF16) for h in heads]
            s = [_dot(qh[h].astype(BF16), kh[h].astype(BF16), NT) for h in heads]
            kv = [_dot((kh[h] * zeta_ref[h]).astype(BF16), vb[h], TN) for h in heads]
            cross = [_dot((qh[h] * xi_ref[h]).astype(BF16), rpb[h]) for h in heads]
            o = [_dot((s[h] * dm_ref[h]).astype(BF16), vb[h]) + cross[h] for h in heads]
            for h in heads:
                rprev_ref[c, h] = rpb[h]
                r_s[h] = rp[h] * gc_ref[h] + kv[h]
                mu = jnp.mean(o[h], axis=-1, keepdims=True)
                oc = o[h] - mu
                var = jnp.mean(oc * oc, axis=-1, keepdims=True)
                on = oc * lax.rsqrt(var + EPS) * gnw_ref[:, c0[h]]
                gate = vg_ref[rows, c1[h]]
                y_ref[rows, c0[h]] = (gate * jax.nn.sigmoid(gate) * on).astype(BF16)
            return carry

        lax.fori_loop(0, nc, chunk, 0)

    return _pcall(
        body, name="ret_fwd", grid=(S // tb,),
        in_specs=[pl.BlockSpec((tb, 2 * DR), lambda i: (i, 1)), pl.BlockSpec((tb, 2 * DR), lambda i: (i, 2)),
                  pl.BlockSpec((tb, HEAD_DIM), lambda i: (i, 0)), pl.BlockSpec((tb, HEAD_DIM), lambda i: (i, 0))]
        + _ret_const_specs(H, DR) + [HBM_SPEC],
        out_specs=[pl.BlockSpec((tb, DR), lambda i: (i, 1)),
                   pl.BlockSpec((nc, H, CHUNK, HEAD_DIM), lambda i: (i, 0, 0, 0))],
        out_shape=[jax.ShapeDtypeStruct(y.shape, BF16), jax.ShapeDtypeStruct((S // CHUNK, H, CHUNK, HEAD_DIM), BF16)],
        scratch_shapes=[pltpu.VMEM((H, CHUNK, HEAD_DIM), F32)], aliases={9: 0},
        operands=[proj, proj, cosf, sins, dm, zeta, xi, gc, gnw, y], comm=comm)


def _ret_bwd(proj, rprev, dy, dproj, tables, gnw, tb, comm=None):
    S = proj.shape[0]
    DR = gnw.shape[1]
    H = DR // HEAD_DIM
    tb = _tile(S, tb, CHUNK)
    nc = tb // CHUNK
    nt = S // tb
    cosf, sins, dm, zeta, xi, gc = tables
    scale = HEAD_DIM ** -0.5

    def body(qk_ref, vg_ref, cos_ref, sin_ref, dm_ref, zeta_ref, xi_ref, gc_ref, gnw_ref, rprev_ref, dy_ref, dp_in,
             dp_ref, dgn_ref, dr_s, dvg_s):
        del dp_in
        i = pl.program_id(0)
        second = pl.program_id(1) == 1

        @pl.when(jnp.logical_and(i == 0, jnp.logical_not(second)))
        def _():
            dr_s[...] = jnp.zeros_like(dr_s)
            dgn_ref[...] = jnp.zeros_like(dgn_ref)

        @pl.when(second)
        def _():
            dp_ref[...] = dvg_s[...]

        def chunk(cc, carry):
            c = nc - 1 - cc
            rows = pl.ds(pl.multiple_of(c * CHUNK, CHUNK), CHUNK)
            cos = cos_ref[rows, :]
            sin = sin_ref[rows, :]
            heads = range(H)
            c0 = [slice(h * HEAD_DIM, (h + 1) * HEAD_DIM) for h in heads]
            c1 = [slice(DR + h * HEAD_DIM, DR + (h + 1) * HEAD_DIM) for h in heads]
            qh = [_rope(qk_ref[rows, c0[h]], cos, sin) for h in heads]
            kh = [_rope(qk_ref[rows, c1[h]], cos, sin) * scale for h in heads]
            qb = [t.astype(BF16) for t in qh]
            kb = [t.astype(BF16) for t in kh]
            vb = [vg_ref[rows, c0[h]].astype(BF16) for h in heads]
            rpb = [rprev_ref[c, h] for h in heads]
            qx = [(qh[h] * xi_ref[h]).astype(BF16) for h in heads]
            kz = [(kh[h] * zeta_ref[h]).astype(BF16) for h in heads]
            drh = [dr_s[h] for h in heads]
            drb = [t.astype(BF16) for t in drh]
            s = [_dot(qb[h], kb[h], NT) for h in heads]
            cross = [_dot(qx[h], rpb[h]) for h in heads]
            dv_state = [_dot(kz[h], drb[h]) for h in heads]
            dk_state = [_dot(vb[h], drb[h], NT) for h in heads]
            sb = [(s[h] * dm_ref[h]).astype(BF16) for h in heads]
            o = [_dot(sb[h], vb[h]) + cross[h] for h in heads]
            dob = []
            for h in heads:
                mu = jnp.mean(o[h], axis=-1, keepdims=True)
                oc = o[h] - mu
                rstd = lax.rsqrt(jnp.mean(oc * oc, axis=-1, keepdims=True) + EPS)
                ohat = oc * rstd
                gw = gnw_ref[:, c0[h]]
                gate = vg_ref[rows, c1[h]]
                sg = jax.nn.sigmoid(gate)
                dyv = dy_ref[rows, c0[h]]
                dvg_s[rows, c1[h]] = (dyv * (ohat * gw) * (sg * (1.0 + gate * (1.0 - sg)))).astype(BF16)
                don = dyv * (gate * sg)
                dgn_ref[:, c0[h]] += jnp.sum(don * ohat, axis=0, keepdims=True)
                dohat = don * gw
                do = rstd * (dohat - jnp.mean(dohat, axis=-1, keepdims=True)
                             - ohat * jnp.mean(dohat * ohat, axis=-1, keepdims=True))
                dob.append(do.astype(BF16))
            ds = [_dot(dob[h], vb[h], NT) for h in heads]
            dq_state = [_dot(dob[h], rpb[h], NT) for h in heads]
            dv = [_dot(sb[h], dob[h], TN) + dv_state[h] for h in heads]
            dr_new = [_dot(qx[h], dob[h], TN) for h in heads]
            dsb = [(ds[h] * dm_ref[h]).astype(BF16) for h in heads]
            dqh = [_dot(dsb[h], kb[h]) + dq_state[h] * xi_ref[h] for h in heads]
            dkh = [_dot(dsb[h], qb[h], TN) + dk_state[h] * zeta_ref[h] for h in heads]
            for h in heads:
                dr_s[h] = drh[h] * gc_ref[h] + dr_new[h]
                dp_ref[rows, c0[h]] = _rope_t(dqh[h], cos, sin).astype(BF16)
                dp_ref[rows, c1[h]] = _rope_t(dkh[h] * scale, cos, sin).astype(BF16)
                dvg_s[rows, c0[h]] = dv[h].astype(BF16)
            return carry

        @pl.when(jnp.logical_not(second))
        def _():
            lax.fori_loop(0, nc, chunk, 0)

    rev = lambda i: nt - 1 - i
    return _pcall(
        body, name="ret_bwd", grid=(nt, 2), aliases={11: 0}, comm=comm,
        operands=[proj, proj, cosf, sins, dm, zeta, xi, gc, gnw, rprev, dy, dproj],
        in_specs=[pl.BlockSpec((tb, 2 * DR), lambda i, j: (rev(i), 1)), pl.BlockSpec((tb, 2 * DR), lambda i, j: (rev(i), 2)),
                  pl.BlockSpec((tb, HEAD_DIM), lambda i, j: (rev(i), 0)), pl.BlockSpec((tb, HEAD_DIM), lambda i, j: (rev(i), 0))]
        + _ret_const_specs(H, DR)
        + [pl.BlockSpec((nc, H, CHUNK, HEAD_DIM), lambda i, j: (rev(i), 0, 0, 0)),
           pl.BlockSpec((tb, DR), lambda i, j: (rev(i), 1)), HBM_SPEC],
        out_specs=[pl.BlockSpec((tb, 2 * DR), lambda i, j: (rev(i), 1 + j)), pl.BlockSpec((1, DR), lambda i, j: (0, 0))],
        out_shape=[jax.ShapeDtypeStruct(dproj.shape, BF16), jax.ShapeDtypeStruct((1, DR), F32)],
        scratch_shapes=[pltpu.VMEM((H, CHUNK, HEAD_DIM), F32), pltpu.VMEM((tb, 2 * DR), BF16)])


def _place():
    x, y, c = lax.axis_index("x"), lax.axis_index("y"), lax.axis_index("c")
    chips = [(1 - x, y), (x, 1 - y), (1 - x, 1 - y)]
    return x, y, c, chips


def _own_slab(name, shard, place):
    R, C = shard.shape
    tr = _row_tile(R, C)
    return _ew("cast_" + name, lambda a: (a,), [(shard, pl.BlockSpec((tr, C), lambda i, p: (i, 0)))],
               [(jax.ShapeDtypeStruct((4, R, C), BF16), pl.BlockSpec((None, tr, C), lambda i, p: (p[1], i, 0)))],
               (R // tr,), sp=place)[0]


class _remote:
    def __init__(self, src, dst, ssem, rsem, k, to):
        self.args = dict(src_ref=src, dst_ref=dst, send_sem=ssem.at[k], recv_sem=rsem.at[k], device_id=to,
                         device_id_type=MESH)

    def start(self):
        pltpu.make_async_remote_copy(**self.args).start()

    def wait_send(self):
        pltpu.make_async_remote_copy(**self.args).wait_send()

    def wait_recv(self):
        pltpu.make_async_remote_copy(**self.args).wait_recv()


def _task_fns(copies):
    def start(cins, couts, ssem, rsem, base):
        for cp in copies(cins, couts, ssem, rsem, base)[0]:
            cp.start()

    def finish(cins, couts, ssem, rsem, base):
        sends, recvs = copies(cins, couts, ssem, rsem, base)
        for cp in sends:
            cp.wait_send()
        for cp in recvs:
            cp.wait_recv()

    return start, finish


def _gather_ici(st):
    r2 = st.shape[1] // 2

    def copies(cins, couts, ssem, rsem, base):
        x, y, c, chips = _place()
        out = couts[0]
        mine = out.at[2 * x + y, pl.ds(c * r2, r2), :]
        sends = [_remote(mine, mine, ssem, rsem, base + j, (*chip, c)) for j, chip in enumerate(chips)]
        recvs = []
        for j, (cx, cy) in enumerate(chips):
            got = out.at[2 * cx + cy, pl.ds(c * r2, r2), :]
            recvs.append(_remote(got, got, ssem, rsem, base + j, (x, y, c)))
        return sends, recvs

    start, finish = _task_fns(copies)
    return _Comm([st], [jax.ShapeDtypeStruct(st.shape, st.dtype)], {0: 0}, 3, start, finish)


def _gather_d2d(st):
    r2 = st.shape[1] // 2

    def copies(cins, couts, ssem, rsem, base):
        x, y, c, chips = _place()
        out = couts[0]
        sends, recvs = [], []
        for j, (cx, cy) in enumerate(chips):
            have = out.at[2 * cx + cy, pl.ds(c * r2, r2), :]
            want = out.at[2 * cx + cy, pl.ds((1 - c) * r2, r2), :]
            sends.append(_remote(have, have, ssem, rsem, base + j, (x, y, 1 - c)))
            recvs.append(_remote(want, want, ssem, rsem, base + j, (x, y, c)))
        return sends, recvs

    start, finish = _task_fns(copies)
    return _Comm([st], [jax.ShapeDtypeStruct(st.shape, st.dtype)], {0: 0}, 3, start, finish)


def _gather_conv(conv_w):
    def copies(cins, couts, ssem, rsem, base):
        x, y, c, chips = _place()
        src, out = cins[0], couts[0]
        sends = [_remote(src, out.at[2 * x + y], ssem, rsem, base + j, (*chip, c)) for j, chip in enumerate(chips)]
        recvs = [_remote(src, out.at[2 * cx + cy], ssem, rsem, base + j, (x, y, c)) for j, (cx, cy) in enumerate(chips)]
        return sends, recvs

    start, finish = _task_fns(copies)
    return _Comm([conv_w], [jax.ShapeDtypeStruct((4,) + conv_w.shape, conv_w.dtype)], {}, 3, start, finish)


def _pair_exchange(g):
    r2 = g.shape[1] // 2

    def copies(cins, couts, ssem, rsem, base):
        x, y, c, _ = _place()
        cp = _remote(cins[0].at[:, pl.ds((1 - c) * r2, r2), :], couts[0], ssem, rsem, base, (x, y, 1 - c))
        return [cp], [cp]

    start, finish = _task_fns(copies)
    return _Comm([g], [jax.ShapeDtypeStruct((g.shape[0], r2, g.shape[2]), g.dtype)], {}, 1, start, finish)


def _chip_exchange(part):
    def copies(cins, couts, ssem, rsem, base):
        x, y, c, chips = _place()
        cps = [_remote(cins[0].at[2 * cx + cy], couts[0].at[j], ssem, rsem, base + j, (cx, cy, c))
               for j, (cx, cy) in enumerate(chips)]
        return cps, cps

    start, finish = _task_fns(copies)
    return _Comm([part], [jax.ShapeDtypeStruct((3,) + part.shape[1:], part.dtype)], {}, 3, start, finish)


def _pair_share(slot):
    def copies(cins, couts, ssem, rsem, base):
        x, y, c, _ = _place()
        out = couts[0]
        return ([_remote(out.at[c], out.at[c], ssem, rsem, base, (x, y, 1 - c))],
                [_remote(out.at[1 - c], out.at[1 - c], ssem, rsem, base, (x, y, c))])

    start, finish = _task_fns(copies)
    return _Comm([slot], [jax.ShapeDtypeStruct(slot.shape, slot.dtype)], {0: 0}, 1, start, finish)


def _gather_small(sm):
    flips = [(fx, fy, fc) for fx in (0, 1) for fy in (0, 1) for fc in (0, 1)][1:]

    def copies(cins, couts, ssem, rsem, base):
        x, y, c, _ = _place()
        src, out = cins[0], couts[0]
        peers = [(1 - x if fx else x, 1 - y if fy else y, 1 - c if fc else c) for fx, fy, fc in flips]
        sends = [_remote(src, out.at[4 * x + 2 * y + c], ssem, rsem, base + k, peer) for k, peer in enumerate(peers)]
        recvs = [_remote(src, out.at[4 * px + 2 * py + pc], ssem, rsem, base + k, (x, y, c))
                 for k, (px, py, pc) in enumerate(peers)]
        return sends, recvs

    start, finish = _task_fns(copies)
    return _Comm([sm], [jax.ShapeDtypeStruct((8,) + sm.shape, sm.dtype)], {}, 7, start, finish)


def _gather_both(st):
    ici, d2d = _gather_ici(st), _gather_d2d(st)

    def finish(cins, couts, ssem, rsem, base):
        ici.finish(cins, couts, ssem, rsem, base)
        d2d.start(cins, couts, ssem, rsem, base + ici.n_sem)
        d2d.finish(cins, couts, ssem, rsem, base + ici.n_sem)

    return _Comm(ici.ins, ici.outs, ici.aliases, ici.n_sem + d2d.n_sem, ici.start, finish)


def _comm_call(name, tasks):
    task = _merge(tasks)
    nci = len(task.ins)

    def body(*refs):
        cins, couts, (ssem, rsem) = refs[:nci], refs[nci:nci + len(task.outs)], refs[nci + len(task.outs):]
        task.start(cins, couts, ssem, rsem, 0)
        task.finish(cins, couts, ssem, rsem, 0)

    return pl.pallas_call(
        body, in_specs=[HBM_SPEC] * nci, out_specs=[HBM_SPEC] * len(task.outs), out_shape=list(task.outs),
        scratch_shapes=[pltpu.SemaphoreType.DMA((task.n_sem,)), pltpu.SemaphoreType.DMA((task.n_sem,))],
        input_output_aliases=task.aliases, name=name)(*task.ins)


def _adamw(w, g, m, v):
    m = ADAM_B1 * m + (1.0 - ADAM_B1) * g
    v = ADAM_B2 * v + (1.0 - ADAM_B2) * (g * g)
    m_hat = m / (1.0 - ADAM_B1 ** ADAM_STEP)
    v_hat = v / (1.0 - ADAM_B2 ** ADAM_STEP)
    delta = -ADAM_LR * (m_hat / (jnp.sqrt(v_hat) + ADAM_EPS) + ADAM_WD * w)
    return delta, m, v


def _adamw_call(name, w, g, m, v):
    R, C = w.shape
    tr = _row_tile(R, C, 1024 * 1024)
    row = pl.BlockSpec((tr, C), lambda i: (i, 0))
    o = jax.ShapeDtypeStruct((R, C), F32)
    return _ew(name, lambda w_, g_, m_, v_: (*_adamw(w_, g_, m_, v_), g_), [(w, row), (g, row), (m, row), (v, row)],
               [(o, row), (o, row), (o, row), (o, row)], (R // tr,))


def _pair_sum(name, g, ra, place):
    _, R, C = g.shape
    r2 = R // 2
    tr = _row_tile(r2, C)
    nb = r2 // tr
    own = pl.BlockSpec((None, tr, C), lambda j, i, p: (j, p[0] * nb + i, 0))
    blk = pl.BlockSpec((None, tr, C), lambda j, i, p: (j, i, 0))
    return _ew("rs_pair_sum_" + name, lambda a, b: (a + b,), [(g, own), (ra, blk)],
               [(jax.ShapeDtypeStruct((4, r2, C), BF16), blk)], (4, nb), sp=place)[0]


def _chip_sum(name, g, ra, rb, place):
    _, R, C = g.shape
    r2 = R // 2
    tr = _row_tile(r2, C)
    nb = r2 // tr
    own = pl.BlockSpec((None, tr, C), lambda i, p: (p[1], p[0] * nb + i, 0))
    mine = pl.BlockSpec((None, tr, C), lambda i, p: (p[1], i, 0))
    src = [pl.BlockSpec((None, tr, C), functools.partial(lambda i, p, j: (j, i, 0), j=j)) for j in range(3)]
    out = pl.BlockSpec((None, tr, C), lambda i, p: (p[0], i, 0))

    def total(a, b, r0, r1, r2_):
        return ((((a + b) + r0.astype(F32)) + r1.astype(F32)) + r2_.astype(F32),)

    return _ew("rs_chip_sum_" + name, total, [(g, own), (ra, mine), (rb, src[0]), (rb, src[1]), (rb, src[2])],
               [(jax.ShapeDtypeStruct((2, r2, C), F32), out)], (nb,), sp=place)[0]


def _pack(arrays):
    rows, offs, pos = [], [], 0
    for a in arrays:
        flat = a.reshape(-1)
        n = -(-flat.shape[0] // (8 * LANES)) * (8 * LANES)
        if n != flat.shape[0]:
            flat = jnp.pad(flat, (0, n - flat.shape[0]))
        rows.append(flat.reshape(-1, LANES))
        offs.append(pos)
        pos += n // LANES
    return jnp.concatenate(rows, axis=0), offs


def _unpack(packed, offs, shapes):
    out = []
    for off, shp in zip(offs, shapes):
        n = 1
        for s in shp:
            n *= s
        out.append(packed[off:off + -(-n // LANES)].reshape(-1)[:n].reshape(shp))
    return out


def _sum8(gathered):
    _, R, C = gathered.shape
    tr = _row_tile(R, C, 256 * 1024)
    specs = [pl.BlockSpec((None, tr, C), functools.partial(lambda i, d: (d, i, 0), d=d)) for d in range(8)]

    def fn(*parts):
        t = parts[0]
        for p in parts[1:]:
            t = t + p
        return (t,)

    return _ew("small_sum", fn, [(gathered, s) for s in specs],
               [(jax.ShapeDtypeStruct((R, C), F32), pl.BlockSpec((tr, C), lambda i: (i, 0)))], (R // tr,))[0]


BIG = ("w_in", "w_out", "w_ffn_gate", "w_ffn_up", "w_ffn_down")
SMALL = ("ln1_w", "conv_w", "conv_b", "gate_a_w", "gate_a_b", "gate_x_w", "gate_x_b", "lru_lambda", "ret_gn_w", "ln2_w",
         "final_norm_w")
WEIGHTS = ("ln1_w", "w_in", "conv_w", "conv_b", "gate_a_w", "gate_a_b", "gate_x_w", "gate_x_b", "lru_lambda", "ret_gn_w",
           "w_out", "ln2_w", "w_ffn_gate", "w_ffn_up", "w_ffn_down", "final_norm_w")


def kernel(x, ln1_w, w_in, conv_w, conv_b, gate_a_w, gate_a_b, gate_x_w, gate_x_b, lru_lambda, ret_gn_w, w_out, ln2_w, w_ffn_gate, w_ffn_up, w_ffn_down, final_norm_w, loss_target, m_ln1_w, m_w_in, m_conv_w, m_conv_b, m_gate_a_w, m_gate_a_b, m_gate_x_w, m_gate_x_b, m_lru_lambda, m_ret_gn_w, m_w_out, m_ln2_w, m_w_ffn_gate, m_w_ffn_up, m_w_ffn_down, m_final_norm_w, v_ln1_w, v_w_in, v_conv_w, v_conv_b, v_gate_a_w, v_gate_a_b, v_gate_x_w, v_gate_x_b, v_lru_lambda, v_ret_gn_w, v_w_out, v_ln2_w, v_w_ffn_gate, v_w_ffn_up, v_w_ffn_down, v_final_norm_w):
    w = dict(ln1_w=ln1_w, w_in=w_in, conv_w=conv_w, conv_b=conv_b, gate_a_w=gate_a_w, gate_a_b=gate_a_b, gate_x_w=gate_x_w,
             gate_x_b=gate_x_b, lru_lambda=lru_lambda, ret_gn_w=ret_gn_w, w_out=w_out, ln2_w=ln2_w, w_ffn_gate=w_ffn_gate,
             w_ffn_up=w_ffn_up, w_ffn_down=w_ffn_down, final_norm_w=final_norm_w)
    m = dict(ln1_w=m_ln1_w, w_in=m_w_in, conv_w=m_conv_w, conv_b=m_conv_b, gate_a_w=m_gate_a_w, gate_a_b=m_gate_a_b,
             gate_x_w=m_gate_x_w, gate_x_b=m_gate_x_b, lru_lambda=m_lru_lambda, ret_gn_w=m_ret_gn_w, w_out=m_w_out,
             ln2_w=m_ln2_w, w_ffn_gate=m_w_ffn_gate, w_ffn_up=m_w_ffn_up, w_ffn_down=m_w_ffn_down,
             final_norm_w=m_final_norm_w)
    v = dict(ln1_w=v_ln1_w, w_in=v_w_in, conv_w=v_conv_w, conv_b=v_conv_b, gate_a_w=v_gate_a_w, gate_a_b=v_gate_a_b,
             gate_x_w=v_gate_x_w, gate_x_b=v_gate_x_b, lru_lambda=v_lru_lambda, ret_gn_w=v_ret_gn_w, w_out=v_w_out,
             ln2_w=v_ln2_w, w_ffn_gate=v_w_ffn_gate, w_ffn_up=v_w_ffn_up, w_ffn_down=v_w_ffn_down,
             final_norm_w=v_final_norm_w)
    xs, tgt = x[0], loss_target[0]
    S, D = xs.shape
    DL, DR = conv_b.shape[1], ret_gn_w.shape[1]
    assert DL == DR and DL % HEAD_DIM == 0 and S % CHUNK == 0
    d_mix = DL + DR
    cx, cy, cc = lax.axis_index("x"), lax.axis_index("y"), lax.axis_index("c")
    chip = 2 * cx + cy
    place = jnp.stack([cc, chip]).astype(jnp.int32)
    grad, delta, new_m, new_v = {}, {}, {}, {}

    def finish_big(n, full):
        shp = w[n].shape
        g2 = full.reshape(shp[1], shp[2])
        w2, m2, v2 = (t[n].reshape(shp[1], shp[2]) for t in (w, m, v))
        d_, m_, v_, g_ = _adamw_call("adamw_" + n, w2, g2, m2, v2)
        grad[n], delta[n], new_m[n], new_v[n] = (t.reshape(shp) for t in (g_, d_, m_, v_))

    def all_sum(gathered, own):
        return _sum8(lax.dynamic_update_slice(gathered, own[None], (4 * cx + 2 * cy + cc, 0, 0)))

    st = {n: _own_slab(n, w[n][0], place) for n in BIG}
    w_in_st, conv_st = _comm_call("gather_w_in", [_gather_both(st["w_in"]), _gather_conv(conv_w[0])])
    conv_st = lax.dynamic_update_slice(conv_st, conv_w, (chip, 0, 0))
    cw_cols = conv_st.shape[2]
    conv_full = jnp.transpose(conv_st, (1, 0, 2)).reshape(CONV_W, 4 * cw_cols)
    n_in, n_ff = w_in_st.shape[2], st["w_ffn_gate"].shape[2]
    tables = _ret_tables(S, DR // HEAD_DIM)
    wab, wxb = gate_a_w[0].astype(BF16), gate_x_w[0].astype(BF16)
    lru_w = (conv_full, conv_b, wab, gate_a_b, wxb, gate_x_b, lru_lambda)
    TM, TK = 512, 1024

    u1 = _rms_fwd("rms1", xs, ln1_w, TM)
    proj, (w_out_st, wg_st) = _mm_nn_stacked("proj", u1, w_in_st, F32, TM,
                                             comm=_merge([_gather_ici(st["w_out"]), _gather_ici(st["w_ffn_gate"])]))
    (hs, y), (w_out_st, wg_st, wu_st) = _lru_fwd(
        proj, *lru_w, 128, d_mix, comm=_merge([_gather_d2d(w_out_st), _gather_d2d(wg_st), _gather_ici(st["w_ffn_up"])]))
    (y, rprev), (wu_st,) = _ret_fwd(proj, y, tables, ret_gn_w, 256, comm=_gather_d2d(wu_st))
    w_out_f = w_out_st.reshape(d_mix, D)
    (h1, u2), (wd_st,) = _out_proj_rms(y, w_out_f, xs, ln2_w, TM, comm=_gather_ici(st["w_ffn_down"]))
    (gt, up, ff), (wd_st,) = _ffn_gate_up(u2, wg_st, wu_st, TM, comm=_gather_d2d(wd_st))
    wd_f = wd_st.reshape(4 * n_ff, D)
    dh2, dh2b, d_fw, loss = _ffn_down_loss(ff, wd_f, h1, tgt, final_norm_w.reshape(1, D), 256)

    g_wd = _mm_tn("g_w_down", ff, dh2b, n_ff, 1024, TK).reshape(4, n_ff, D)
    (dgt, dup), (ra_wd,) = _ffn_gate_up_bwd(dh2b, wd_f, gt, up, TM, n_ff, comm=_pair_exchange(g_wd))
    pb_wd = _pair_sum("w_ffn_down", g_wd, ra_wd, place)
    g_wg, (rb_wd,) = _mm_tn("g_w_gate", u2, dgt, 1024, None, TK, stacked_cols=n_ff, comm=_chip_exchange(pb_wd))
    slot_wd = _chip_sum("w_ffn_down", g_wd, ra_wd, rb_wd, place)
    g_wu, (full_wd, ra_wg) = _mm_tn("g_w_up", u2, dup, 1024, None, TK, stacked_cols=n_ff,
                                    comm=_merge([_pair_share(slot_wd), _pair_exchange(g_wg)]))
    finish_big("w_ffn_down", full_wd)
    pb_wg = _pair_sum("w_ffn_gate", g_wg, ra_wg, place)
    du2, (rb_wg, ra_wu) = _mm_nt_stacked("d_u2", [dgt, dup], [wg_st, wu_st], F32, 256, 1024,
                                         comm=_merge([_chip_exchange(pb_wg), _pair_exchange(g_wu)]))
    slot_wg = _chip_sum("w_ffn_gate", g_wg, ra_wg, rb_wg, place)
    pb_wu = _pair_sum("w_ffn_up", g_wu, ra_wu, place)
    (dh1, dh1b, d_ln2), (full_wg,) = _rms_bwd("rms2_bwd", h1, ln2_w, du2, dh2, TM, True, comm=_pair_share(slot_wg))
    finish_big("w_ffn_gate", full_wg)
    dy = _mm_nt("d_y", dh1b, w_out_f, F32, TM, 1024)
    g_wout = _mm_tn("g_w_out", y, dh1b, 1024, 1024, TK).reshape(4, d_mix // 4, D)
    (dproj, d_cw, d_cb, d_wa, d_ba, d_wx, d_bx, d_lam), (rb_wu, ra_wout) = _lru_bwd(
        proj, hs, dy, *lru_w, 128, comm=_merge([_chip_exchange(pb_wu), _pair_exchange(g_wout)]))
    slot_wu = _chip_sum("w_ffn_up", g_wu, ra_wu, rb_wu, place)
    pb_wout = _pair_sum("w_out", g_wout, ra_wout, place)
    (dproj, d_gn), (full_wu, rb_wout) = _ret_bwd(proj, rprev, dy, dproj, tables, ret_gn_w, 256,
                                                 comm=_merge([_pair_share(slot_wu), _chip_exchange(pb_wout)]))
    finish_big("w_ffn_up", full_wu)
    slot_wout = _chip_sum("w_out", g_wout, ra_wout, rb_wout, place)
    small = dict(conv_w=d_cw, conv_b=d_cb, gate_a_w=d_wa, gate_a_b=d_ba, gate_x_w=d_wx, gate_x_b=d_bx, lru_lambda=d_lam,
                 ret_gn_w=d_gn, ln2_w=d_ln2, final_norm_w=d_fw)
    packed, offs = _pack([small[n] for n in SMALL[1:]] + [loss])
    g_win, (full_wout, got_small) = _mm_tn("g_w_in", u1, dproj, 1024, None, TK, stacked_cols=n_in,
                                           comm=_merge([_pair_share(slot_wout), _gather_small(packed)]))
    finish_big("w_out", full_wout)
    du1, (ra_win,) = _mm_nt_stacked("d_u1", [dproj], [w_in_st], F32, 256, D, comm=_pair_exchange(g_win))
    pb_win = _pair_sum("w_in", g_win, ra_win, place)
    (gx, d_ln1), (rb_win,) = _rms_bwd("rms1_bwd", xs, ln1_w, du1, dh1, TM, False, comm=_chip_exchange(pb_win))
    slot_win = _chip_sum("w_in", g_win, ra_win, rb_win, place)
    packed1, _ = _pack([d_ln1])
    full_win, got_ln1 = _comm_call("reduce_tail", [_pair_share(slot_win), _gather_small(packed1)])
    finish_big("w_in", full_win)

    red = _unpack(all_sum(got_small, packed), offs, [small[n].shape for n in SMALL[1:]] + [(1, LANES)])
    g = dict(zip(SMALL[1:], red[:-1]))
    g["ln1_w"] = all_sum(got_ln1, packed1)[:-(-D // LANES)].reshape(1, D)
    loss_out = red[-1][0, 0]
    g["conv_w"] = lax.dynamic_slice(g["conv_w"], (0, chip * cw_cols), (CONV_W, cw_cols))
    packs = [_pack([t[n] for n in SMALL])[0] for t in (w, m, v)]
    gp, offs2 = _pack([g[n] for n in SMALL])
    outs = _adamw_call("adamw_small", packs[0], gp, packs[1], packs[2])
    shapes = [w[n].shape for n in SMALL]
    for dst, arr in zip((delta, new_m, new_v), outs):
        dst.update(zip(SMALL, _unpack(arr, offs2, shapes)))
    for n in SMALL:
        grad[n] = g[n].reshape(w[n].shape)

    return (loss_out, gx.reshape(x.shape), *[grad[n] for n in WEIGHTS], *[delta[n] for n in WEIGHTS],
            *[new_m[n] for n in WEIGHTS], *[new_v[n] for n in WEIGHTS])
```

```python
import functools

import jax
import jax.numpy as jnp
from jax import lax
from jax.experimental import pallas as pl
from jax.experimental.pallas import tpu as pltpu

F32 = jnp.float32
BF16 = jnp.bfloat16
MESH = pl.DeviceIdType.MESH

EPS = 1e-6
LRU_C = 8.0
ROPE_BASE = 10000.0
CHUNK = 128
HEAD_DIM = 128
CONV_W = 4
ADAM_LR = 0.001
ADAM_B1 = 0.9
ADAM_B2 = 0.999
ADAM_EPS = 1e-08
ADAM_WD = 0.01
ADAM_STEP = 10

V7X_VMEM_BYTES = 64 * 1024 * 1024
VMEM_LIMIT = V7X_VMEM_BYTES - 8 * 1024 * 1024
LANES = 128
SUBLANES_16BIT = 16

NN = (((1,), (0,)), ((), ()))
NT = (((1,), (1,)), ((), ()))
TN = (((0,), (0,)), ((), ()))


def _dot(a, b, dims=NN):
    return lax.dot_general(a, b, dims, preferred_element_type=F32)


def _tile(n, pref, mult=SUBLANES_16BIT):
    best = None
    t = mult
    while t <= min(n, pref):
        if n % t == 0:
            best = t
        t += mult
    return best if best is not None else n


def _row_tile(rows, cols, budget_bytes=2 * 1024 * 1024):
    return _tile(rows, max(SUBLANES_16BIT, budget_bytes // (cols * 4)))


def _params(sem):
    return pltpu.CompilerParams(dimension_semantics=sem, vmem_limit_bytes=VMEM_LIMIT)


HBM_SPEC = pl.BlockSpec(memory_space=pl.ANY)


class _Comm:
    def __init__(self, ins, outs, aliases, n_sem, start, finish):
        self.ins, self.outs, self.aliases, self.n_sem, self.start, self.finish = ins, outs, aliases, n_sem, start, finish


def _merge(tasks):
    ins, outs, aliases, plans, n_sem = [], [], {}, [], 0
    for t in tasks:
        i0, o0 = len(ins), len(outs)
        plans.append((t, i0, o0, n_sem))
        ins += t.ins
        outs += t.outs
        aliases.update({i0 + a: o0 + b for a, b in t.aliases.items()})
        n_sem += t.n_sem

    def run(which):
        def go(cins, couts, ssem, rsem, base):
            for t, i0, o0, s0 in plans:
                getattr(t, which)(cins[i0:i0 + len(t.ins)], couts[o0:o0 + len(t.outs)], ssem, rsem, base + s0)
        return go

    return _Comm(ins, outs, aliases, n_sem, run("start"), run("finish"))


def _pcall(body, *, name, grid, in_specs, out_specs, out_shape, operands, scratch_shapes=(), aliases=None, comm=None):
    n_in, n_out, n_scr = len(operands), len(out_shape), len(scratch_shapes)
    aliases = dict(aliases or {})
    params = _params(("arbitrary",) * len(grid))
    if comm is None:
        return pl.pallas_call(body, grid=grid, in_specs=list(in_specs), out_specs=list(out_specs), out_shape=list(out_shape),
                              scratch_shapes=list(scratch_shapes), input_output_aliases=aliases, name=name,
                              compiler_params=params)(*operands)
    nci, nco = len(comm.ins), len(comm.outs)

    def wrapped(*refs):
        ins, cins = refs[:n_in], refs[n_in:n_in + nci]
        o0 = n_in + nci
        outs, couts = refs[o0:o0 + n_out], refs[o0 + n_out:o0 + n_out + nco]
        s0 = o0 + n_out + nco
        scr, (ssem, rsem) = refs[s0:s0 + n_scr], refs[s0 + n_scr:]
        ids = [pl.program_id(a) for a in range(len(grid))]
        first = functools.reduce(jnp.logical_and, [i == 0 for i in ids])
        last = functools.reduce(jnp.logical_and, [i == g - 1 for i, g in zip(ids, grid)])

        @pl.when(first)
        def _():
            comm.start(cins, couts, ssem, rsem, 0)

        body(*ins, *outs, *scr)

        @pl.when(last)
        def _():
            comm.finish(cins, couts, ssem, rsem, 0)

    aliases.update({n_in + a: n_out + b for a, b in comm.aliases.items()})
    res = pl.pallas_call(
        wrapped, grid=grid, in_specs=list(in_specs) + [HBM_SPEC] * nci, out_specs=list(out_specs) + [HBM_SPEC] * nco,
        out_shape=list(out_shape) + list(comm.outs),
        scratch_shapes=list(scratch_shapes) + [pltpu.SemaphoreType.DMA((comm.n_sem,)), pltpu.SemaphoreType.DMA((comm.n_sem,))],
        input_output_aliases=aliases, name=name, compiler_params=params)(*operands, *comm.ins)
    return res[:n_out], res[n_out:]


def _ew(name, fn, ins, outs, grid, sp=None):
    n_in = len(ins)

    def body(*refs):
        if sp is not None:
            refs = refs[1:]
        vals = [r[...] for r in refs[:n_in]]
        res = fn(*vals)
        for o_ref, v in zip(refs[n_in:], res):
            o_ref[...] = v.astype(o_ref.dtype)

    in_specs = [s for _, s in ins]
    out_specs = [s for _, s in outs]
    out_shape = [s for s, _ in outs]
    sem = ("arbitrary",) * len(grid)
    if sp is None:
        return pl.pallas_call(body, grid=grid, in_specs=in_specs, out_specs=out_specs, out_shape=out_shape,
                              name=name, compiler_params=_params(sem))(*[a for a, _ in ins])
    gs = pltpu.PrefetchScalarGridSpec(num_scalar_prefetch=1, grid=grid, in_specs=in_specs, out_specs=out_specs)
    return pl.pallas_call(body, grid_spec=gs, out_shape=out_shape, name=name,
                          compiler_params=_params(sem))(sp, *[a for a, _ in ins])


def _matmul(name, pairs, dims, grid, out_shape, out_spec, acc_shape, res=None, comm=None):
    n = len(pairs)
    nk = grid[2]

    def body(*refs):
        ab = refs[:2 * n]
        pos = 2 * n
        res_ref = None
        if res is not None:
            res_ref = refs[pos]
            pos += 1
        o_ref = refs[pos]
        acc_ref = refs[pos + 1] if nk > 1 else None

        def partial():
            t = None
            for p in range(n):
                d = _dot(ab[2 * p][...], ab[2 * p + 1][...], dims)
                t = d if t is None else t + d
            return t

        def finish(t):
            if res_ref is not None:
                t = t + res_ref[...]
            o_ref[...] = t.astype(o_ref.dtype)

        if nk == 1:
            finish(partial())
        else:
            k = pl.program_id(2)

            @pl.when(k == 0)
            def _():
                acc_ref[...] = partial()

            @pl.when(k > 0)
            def _():
                acc_ref[...] += partial()

            @pl.when(k == nk - 1)
            def _():
                finish(acc_ref[...])

    operands, in_specs = [], []
    for a, a_spec, b, b_spec in pairs:
        operands += [a, b]
        in_specs += [a_spec, b_spec]
    if res is not None:
        operands.append(res[0])
        in_specs.append(res[1])
    scratch = [pltpu.VMEM(acc_shape, F32)] if nk > 1 else []
    res = _pcall(body, name=name, grid=grid, in_specs=in_specs, out_specs=[out_spec], out_shape=[out_shape],
                 operands=operands, scratch_shapes=scratch, comm=comm)
    return res[0] if comm is None else (res[0][0], res[1])


def _mm_nn_stacked(name, a, b_st, out_dtype, tm, comm=None):
    M, K = a.shape
    J, _, Nj = b_st.shape
    tm = _tile(M, tm)
    return _matmul(
        name, [(a, pl.BlockSpec((tm, K), lambda j, i, k: (i, 0)), b_st, pl.BlockSpec((None, K, Nj), lambda j, i, k: (j, 0, 0)))],
        NN, (J, M // tm, 1), jax.ShapeDtypeStruct((M, J * Nj), out_dtype), pl.BlockSpec((tm, Nj), lambda j, i, k: (i, j)), None,
        comm=comm)


def _mm_nt_stacked(name, a_list, b_list, out_dtype, tm, tn, comm=None):
    n = len(a_list)
    M = a_list[0].shape[0]
    J, N, Nj = b_list[0].shape
    tm, tn = _tile(M, tm), _tile(N, tn, LANES)

    def body(*refs):
        o_ref = refs[2 * n]
        t = None
        for p in range(n):
            a_ref, b_ref = refs[p], refs[n + p]
            for s in range(J):
                d = _dot(a_ref[:, s * Nj:(s + 1) * Nj], b_ref[s], NT)
                t = d if t is None else t + d
        o_ref[...] = t.astype(o_ref.dtype)

    a_spec = pl.BlockSpec((tm, J * Nj), lambda j, i: (i, 0))
    b_spec = pl.BlockSpec((J, tn, Nj), lambda j, i: (0, j, 0), pipeline_mode=pl.Buffered(1))
    res = _pcall(body, name=name, grid=(N // tn, M // tm), in_specs=[a_spec] * n + [b_spec] * n,
                 out_specs=[pl.BlockSpec((tm, tn), lambda j, i: (i, j))], out_shape=[jax.ShapeDtypeStruct((M, N), out_dtype)],
                 operands=[*a_list, *b_list], comm=comm)
    return res[0] if comm is None else (res[0][0], res[1])


MXU_COLUMNS = 256


def _col_blocks(n):
    return [slice(s, min(s + MXU_COLUMNS, n)) for s in range(0, n, MXU_COLUMNS)]


def _ffn_gate_up(u2, wg_st, wu_st, tm, comm=None):
    S, D = u2.shape
    J, _, Nj = wg_st.shape
    tm = _tile(S, tm)

    def body(a_ref, wg_ref, wu_ref, gt_ref, up_ref, ff_ref):
        a = a_ref[...]
        blocks = _col_blocks(Nj)
        ahead = (_dot(a, wg_ref[:, blocks[0]]), _dot(a, wu_ref[:, blocks[0]]))
        for j, cols in enumerate(blocks):
            g, u = ahead
            if j + 1 < len(blocks):
                ahead = (_dot(a, wg_ref[:, blocks[j + 1]]), _dot(a, wu_ref[:, blocks[j + 1]]))
            gt_ref[:, cols] = g.astype(BF16)
            up_ref[:, cols] = u.astype(BF16)
            ff_ref[:, cols] = (g * jax.nn.sigmoid(g) * u).astype(BF16)

    w_spec = pl.BlockSpec((None, D, Nj), lambda j, i: (j, 0, 0))
    o_spec = pl.BlockSpec((tm, Nj), lambda j, i: (i, j))
    o = jax.ShapeDtypeStruct((S, J * Nj), BF16)
    return _pcall(body, name="ffn_gate_up", grid=(J, S // tm),
                  in_specs=[pl.BlockSpec((tm, D), lambda j, i: (i, 0)), w_spec, w_spec],
                  out_specs=[o_spec, o_spec, o_spec], out_shape=[o, o, o], operands=[u2, wg_st, wu_st], comm=comm)


def _ffn_gate_up_bwd(dh2b, wd, gt, up, tm, tn, comm=None):
    S, D = dh2b.shape
    F = wd.shape[0]
    tm, tn = _tile(S, tm), _tile(F, tn, LANES)

    def body(a_ref, wd_ref, gt_ref, up_ref, dgt_ref, dup_ref):
        a = a_ref[...]
        blocks = _col_blocks(tn)
        ahead = _dot(a, wd_ref[blocks[0], :], NT)
        for j, cols in enumerate(blocks):
            d = ahead
            if j + 1 < len(blocks):
                ahead = _dot(a, wd_ref[blocks[j + 1], :], NT)
            g = gt_ref[:, cols].astype(F32)
            u = up_ref[:, cols].astype(F32)
            sg = jax.nn.sigmoid(g)
            dgt_ref[:, cols] = (d * u * (sg * (1.0 + g * (1.0 - sg)))).astype(BF16)
            dup_ref[:, cols] = (d * (g * sg)).astype(BF16)

    blk = pl.BlockSpec((tm, tn), lambda j, i: (i, j))
    o = jax.ShapeDtypeStruct((S, F), BF16)
    return _pcall(body, name="ffn_gate_up_bwd", grid=(F // tn, S // tm),
                  in_specs=[pl.BlockSpec((tm, D), lambda j, i: (i, 0)), pl.BlockSpec((tn, D), lambda j, i: (j, 0)), blk, blk],
                  out_specs=[blk, blk], out_shape=[o, o], operands=[dh2b, wd, gt, up], comm=comm)


def _mm_tn(name, a, b, tmo, tn, tk, stacked_cols=None, comm=None):
    S, Mo = a.shape
    N = b.shape[1]
    tmo, tk = _tile(Mo, tmo, LANES), _tile(S, tk)
    if stacked_cols is None:
        tn = _tile(N, tn, LANES)
        out_shape = jax.ShapeDtypeStruct((Mo, N), F32)
        out_spec = pl.BlockSpec((tmo, tn), lambda i, j, k: (i, j))
    else:
        tn = stacked_cols
        out_shape = jax.ShapeDtypeStruct((N // tn, Mo, tn), F32)
        out_spec = pl.BlockSpec((None, tmo, tn), lambda i, j, k: (j, i, 0))
    return _matmul(
        name, [(a, pl.BlockSpec((tk, tmo), lambda i, j, k: (k, i)), b, pl.BlockSpec((tk, tn), lambda i, j, k: (k, j)))],
        TN, (Mo // tmo, N // tn, S // tk), out_shape, out_spec, (tmo, tn), comm=comm)


def _rms_fwd(name, x, w, tm):
    S, D = x.shape
    tm = _tile(S, tm)

    def fn(xv, wv):
        r = lax.rsqrt(jnp.mean(xv * xv, axis=-1, keepdims=True) + EPS)
        return ((xv * r) * wv,)

    row = pl.BlockSpec((tm, D), lambda i: (i, 0))
    return _ew(name, fn, [(x, row), (w, pl.BlockSpec((1, D), lambda i: (0, 0)))],
               [(jax.ShapeDtypeStruct((S, D), BF16), row)], (S // tm,))[0]


def _rms_bwd(name, x, w, dy, dres, tm, comm=None):
    S, D = x.shape
    tm = _tile(S, tm)

    def body(x_ref, w_ref, dy_ref, dres_ref, dx_ref, dw_ref):
        i = pl.program_id(0)

        @pl.when(i == 0)
        def _():
            dw_ref[...] = jnp.zeros_like(dw_ref)

        xv = x_ref[...]
        r = lax.rsqrt(jnp.mean(xv * xv, axis=-1, keepdims=True) + EPS)
        nv = xv * r
        dyv = dy_ref[...]
        dn = dyv * w_ref[...]
        dw_ref[...] += jnp.sum(dyv * nv, axis=0, keepdims=True)
        dx = dres_ref[...] + r * (dn - nv * jnp.mean(dn * nv, axis=-1, keepdims=True))
        dx_ref[...] = dx

    row = pl.BlockSpec((tm, D), lambda i: (i, 0))
    vec = pl.BlockSpec((1, D), lambda i: (0, 0))
    return _pcall(body, name=name, grid=(S // tm,), in_specs=[row, vec, row, row], out_specs=[row, vec],
                  out_shape=[jax.ShapeDtypeStruct((S, D), F32), jax.ShapeDtypeStruct((1, D), F32)],
                  operands=[x, w, dy, dres], comm=comm)


def _rms_bwd_dy(h1, w, du2, dh2, w_out, tm, comm=None):
    S, D = h1.shape
    d_mix = w_out.shape[0]
    tm = _tile(S, tm)

    def body(x_ref, w_ref, dy_ref, dres_ref, wo_ref, dx_ref, dxb_ref, out_ref, dw_ref):
        i = pl.program_id(0)

        @pl.when(i == 0)
        def _():
            dw_ref[...] = jnp.zeros_like(dw_ref)

        xv = x_ref[...]
        r = lax.rsqrt(jnp.mean(xv * xv, axis=-1, keepdims=True) + EPS)
        nv = xv * r
        dyv = dy_ref[...]
        dn = dyv * w_ref[...]
        dw_ref[...] += jnp.sum(dyv * nv, axis=0, keepdims=True)
        dx = dres_ref[...] + r * (dn - nv * jnp.mean(dn * nv, axis=-1, keepdims=True))
        dx_ref[...] = dx
        dxb = dx.astype(BF16)
        dxb_ref[...] = dxb
        out_ref[...] = _dot(dxb, wo_ref[...], NT)

    row = pl.BlockSpec((tm, D), lambda i: (i, 0))
    vec = pl.BlockSpec((1, D), lambda i: (0, 0))
    return _pcall(
        body, name="rms2_bwd_dy", grid=(S // tm,),
        in_specs=[row, vec, row, row, pl.BlockSpec((d_mix, D), lambda i: (0, 0), pipeline_mode=pl.Buffered(1))],
        out_specs=[row, row, pl.BlockSpec((tm, d_mix), lambda i: (i, 0)), vec],
        out_shape=[jax.ShapeDtypeStruct((S, D), F32), jax.ShapeDtypeStruct((S, D), BF16),
                   jax.ShapeDtypeStruct((S, d_mix), F32), jax.ShapeDtypeStruct((1, D), F32)],
        operands=[h1, w, du2, dh2, w_out], comm=comm)


def _out_proj_rms(y, w_out, x, ln_w, tm, comm=None):
    S, K = y.shape
    D = w_out.shape[1]
    tm = _tile(S, tm)

    def body(a_ref, w_ref, x_ref, lw_ref, h_ref, u_ref):
        hv = _dot(a_ref[...], w_ref[...]) + x_ref[...]
        h_ref[...] = hv
        r = lax.rsqrt(jnp.mean(hv * hv, axis=-1, keepdims=True) + EPS)
        u_ref[...] = ((hv * r) * lw_ref[...]).astype(BF16)

    row = pl.BlockSpec((tm, D), lambda i: (i, 0))
    return _pcall(
        body, name="out_proj", grid=(S // tm,),
        in_specs=[pl.BlockSpec((tm, K), lambda i: (i, 0)),
                  pl.BlockSpec((K, D), lambda i: (0, 0), pipeline_mode=pl.Buffered(1)), row,
                  pl.BlockSpec((1, D), lambda i: (0, 0))],
        out_specs=[row, row], out_shape=[jax.ShapeDtypeStruct((S, D), F32), jax.ShapeDtypeStruct((S, D), BF16)],
        operands=[y, w_out, x, ln_w], comm=comm)


def _ffn_down_loss(ff, wd, h1, tgt, fw, tm):
    S, K = ff.shape
    D = wd.shape[1]
    tm = _tile(S, tm)

    def body(a_ref, wd_ref, h1_ref, t_ref, w_ref, dh_ref, dhb_ref, dw_ref, loss_ref):
        i = pl.program_id(0)

        @pl.when(i == 0)
        def _():
            dw_ref[...] = jnp.zeros_like(dw_ref)
            loss_ref[...] = jnp.zeros_like(loss_ref)

        hv = _dot(a_ref[...], wd_ref[...]) + h1_ref[...]
        wv = w_ref[...]
        r = lax.rsqrt(jnp.mean(hv * hv, axis=-1, keepdims=True) + EPS)
        nv = hv * r
        err = nv * wv - t_ref[...]
        row_loss = jnp.mean(err * err, axis=-1, keepdims=True)
        loss_ref[...] += 0.5 * jnp.sum(row_loss, axis=0, keepdims=True)
        dyo = err * (1.0 / D)
        dn = dyo * wv
        dw_ref[...] += jnp.sum(dyo * nv, axis=0, keepdims=True)
        dh = r * (dn - nv * jnp.mean(dn * nv, axis=-1, keepdims=True))
        dh_ref[...] = dh
        dhb_ref[...] = dh.astype(BF16)

    row = pl.BlockSpec((tm, D), lambda i: (i, 0))
    vec = pl.BlockSpec((1, D), lambda i: (0, 0))
    return _pcall(
        body, name="ffn_down_loss", grid=(S // tm,),
        in_specs=[pl.BlockSpec((tm, K), lambda i: (i, 0)),
                  pl.BlockSpec((K, D), lambda i: (0, 0), pipeline_mode=pl.Buffered(1)), row, row, vec],
        out_specs=[row, row, vec, pl.BlockSpec((1, LANES), lambda i: (0, 0))],
        out_shape=[jax.ShapeDtypeStruct((S, D), F32), jax.ShapeDtypeStruct((S, D), BF16),
                   jax.ShapeDtypeStruct((1, D), F32), jax.ShapeDtypeStruct((1, LANES), F32)],
        operands=[ff, wd, h1, tgt, fw])


def _shift_down(x, d, head8):
    r = pltpu.roll(x, d, 0)
    rh = pltpu.roll(head8, d, 0)
    row8 = lax.broadcasted_iota(jnp.int32, head8.shape, 0)
    top = jnp.where(row8 < d, rh, r[0:8])
    return jnp.concatenate([top, r[8:]], axis=0)


def _shift_up(x, d, tail8):
    n = x.shape[0]
    r = pltpu.roll(x, n - d, 0)
    rt = pltpu.roll(tail8, 8 - d, 0)
    row8 = lax.broadcasted_iota(jnp.int32, tail8.shape, 0)
    bot = jnp.where(row8 + d >= 8, rt, r[n - 8:n])
    return jnp.concatenate([r[:n - 8], bot], axis=0)


def _log_sigmoid(lam):
    z = jnp.exp(-jnp.abs(lam))
    u = 1.0 + z
    log1p = jnp.where(u == 1.0, z, jnp.log(u) * (z / jnp.where(u == 1.0, 1.0, u - 1.0)))
    return jnp.minimum(lam, 0.0) - log1p


def _neg_expm1(z, exp_z):
    series = -z * (1.0 + z * (0.5 + z * (1.0 / 6.0)))
    return jnp.where(z > -0.02, series, 1.0 - exp_z)


_GELU_C = 0.7978845608028654


def _gelu(x):
    t = jnp.tanh(_GELU_C * (x + 0.044715 * (x * x * x)))
    return x * (0.5 * (1.0 + t)), t


def _gelu_grad(x, t):
    return 0.5 * (1.0 + t) + 0.5 * x * (1.0 - t * t) * (_GELU_C * (1.0 + 3.0 * 0.044715 * (x * x)))


def _lru_gates(lx, head8, cw, cb, wa_ref, ba, wx_ref, bx, ls):
    nb = wa_ref.shape[0]
    sh = [lx] + [_shift_down(lx, d, head8) for d in (1, 2, 3)]
    cx = cb + sh[3] * cw[0:1]
    cx = cx + sh[2] * cw[1:2]
    cx = cx + sh[1] * cw[2:3]
    cx = cx + sh[0] * cw[3:4]
    cxb = cx.astype(BF16)
    ra = jnp.concatenate([_dot(cxb[:, n * HEAD_DIM:(n + 1) * HEAD_DIM], wa_ref[n]) for n in range(nb)], axis=1) + ba
    ia = jnp.concatenate([_dot(cxb[:, n * HEAD_DIM:(n + 1) * HEAD_DIM], wx_ref[n]) for n in range(nb)], axis=1) + bx
    r = jax.nn.sigmoid(ra)
    ig = jax.nn.sigmoid(ia)
    log_a = LRU_C * r * ls
    a = jnp.exp(log_a)
    m2 = _neg_expm1(2.0 * log_a, a * a)
    return sh, cx, cxb, r, ig, a, m2, jnp.sqrt(m2)


def _lru_specs(tl, DL):
    nb = DL // HEAD_DIM
    vec = pl.BlockSpec((1, DL), lambda i: (0, 0))
    return [pl.BlockSpec((CONV_W, DL), lambda i: (0, 0)), vec,
            pl.BlockSpec((nb, HEAD_DIM, HEAD_DIM), lambda i: (0, 0, 0)), vec,
            pl.BlockSpec((nb, HEAD_DIM, HEAD_DIM), lambda i: (0, 0, 0)), vec, vec]


def _lru_fwd(proj, cw, cb, wa, ba, wx, bx, lam, tl, d_mix, comm=None):
    S = proj.shape[0]
    DL = cb.shape[1]
    tl = _tile(S, tl)

    def body(lx_ref, lg_ref, cw_ref, cb_ref, wa_ref, ba_ref, wx_ref, bx_ref, lam_ref, h_ref, y_ref, prev8, hc, a_s, b_s):
        i = pl.program_id(0)

        @pl.when(i == 0)
        def _():
            prev8[...] = jnp.zeros_like(prev8)
            hc[...] = jnp.zeros_like(hc)

        lx = lx_ref[...]
        ls = _log_sigmoid(lam_ref[...])
        _, cx, _, _, ig, a, _, mult = _lru_gates(lx, prev8[...], cw_ref[...], cb_ref[...], wa_ref, ba_ref[...],
                                                 wx_ref, bx_ref[...], ls)
        b = mult * (ig * cx)
        row = lax.broadcasted_iota(jnp.int32, a.shape, 0) & 7
        for d in (1, 2, 4):
            a_sh = pltpu.roll(a, d, 0)
            b_sh = pltpu.roll(b, d, 0)
            m = row >= d
            b = jnp.where(m, a * b_sh + b, b)
            a = jnp.where(m, a * a_sh, a)
        a_s[...] = a
        b_s[...] = b

        def step(g, hprev):
            sl = pl.ds(pl.multiple_of(g * 8, 8), 8)
            hh = a_s[sl, :] * hprev + b_s[sl, :]
            h_ref[sl, :] = hh
            return hh[7:8, :]

        hc[0:1, :] = lax.fori_loop(0, tl // 8, step, hc[0:1, :])
        prev8[...] = lx[tl - 8:tl]
        g, _ = _gelu(lg_ref[...])
        y_ref[...] = (h_ref[...] * g).astype(BF16)

    return _pcall(
        body, name="lru_fwd", grid=(S // tl,),
        in_specs=[pl.BlockSpec((tl, DL), lambda i: (i, 0)), pl.BlockSpec((tl, DL), lambda i: (i, 1))] + _lru_specs(tl, DL),
        out_specs=[pl.BlockSpec((tl, DL), lambda i: (i, 0)), pl.BlockSpec((tl, DL), lambda i: (i, 0))],
        out_shape=[jax.ShapeDtypeStruct((S, DL), F32), jax.ShapeDtypeStruct((S, d_mix), BF16)],
        scratch_shapes=[pltpu.VMEM((8, DL), F32), pltpu.VMEM((8, DL), F32), pltpu.VMEM((tl, DL), F32), pltpu.VMEM((tl, DL), F32)],
        operands=[proj, proj, cw, cb, wa, ba, wx, bx, lam], comm=comm)


def _lru_bwd(proj, h, dy, cw, cb, wa, ba, wx, bx, lam, tl, comm=None):
    S = proj.shape[0]
    DL = cb.shape[1]
    nb = DL // HEAD_DIM
    tl = _tile(S, tl)
    nt = S // tl
    ng = tl // 8
    t8 = tl // 8

    def body(lx_ref, lxp_ref, lg_ref, h_ref, hp_ref, dy_ref, cw_ref, cb_ref, wa_ref, ba_ref, wx_ref, bx_ref, lam_ref,
             dlxg_ref, dcw_ref, dcb_ref, dwa_ref, dba_ref, dwx_ref, dbx_ref, dlam_ref,
             a_next, g_carry, dcx_next, an_s, dh_s, g_s):
        i = pl.program_id(0)

        @pl.when(i == 0)
        def _():
            for ref in (dcw_ref, dcb_ref, dwa_ref, dba_ref, dwx_ref, dbx_ref, dlam_ref, a_next, g_carry, dcx_next):
                ref[...] = jnp.zeros_like(ref)

        first = i == nt - 1
        lx = lx_ref[...]
        hv = h_ref[...]
        lg = lg_ref[...]
        dyv = dy_ref[...]
        head8 = jnp.where(first, 0.0, lxp_ref[...])
        hhead8 = jnp.where(first, 0.0, hp_ref[...])
        lamv = lam_ref[...]
        ls = _log_sigmoid(lamv)
        cwv = cw_ref[...]
        sh, cx, cxb, r, ig, a, m2, mult = _lru_gates(lx, head8, cwv, cb_ref[...], wa_ref, ba_ref[...], wx_ref, bx_ref[...],
                                                     ls)
        hprev = _shift_down(hv, 1, hhead8)
        g, t = _gelu(lg)
        dlg = dyv * hv * _gelu_grad(lg, t)
        dh = dyv * g
        an = _shift_up(a, 1, a_next[...])
        row = lax.broadcasted_iota(jnp.int32, a.shape, 0) & 7
        for d in (1, 2, 4):
            an_sh = pltpu.roll(an, tl - d, 0)
            dh_sh = pltpu.roll(dh, tl - d, 0)
            m = row + d < 8
            dh = jnp.where(m, an * dh_sh + dh, dh)
            an = jnp.where(m, an * an_sh, an)
        an_s[...] = an
        dh_s[...] = dh

        def step(k, gc):
            sl = pl.ds(pl.multiple_of((ng - 1 - k) * 8, 8), 8)
            gg = an_s[sl, :] * gc + dh_s[sl, :]
            g_s[sl, :] = gg
            return gg[0:1, :]

        g_carry[0:1, :] = lax.fori_loop(0, ng, step, g_carry[0:1, :])
        a_next[...] = a[0:8]
        G = g_s[...]
        da = G * hprev
        icx = ig * cx
        dmult = G * icx
        dicx = G * mult
        di = dicx * cx
        dcx = dicx * ig
        dlog = da * a - dmult * ((a * a) * lax.rsqrt(m2))
        dr = dlog * (LRU_C * ls)
        dlam_ref[...] += jnp.sum(dlog * (LRU_C * r), axis=0, keepdims=True)
        dra = dr * r * (1.0 - r)
        dia = di * ig * (1.0 - ig)
        dba_ref[...] += jnp.sum(dra, axis=0, keepdims=True)
        dbx_ref[...] += jnp.sum(dia, axis=0, keepdims=True)
        drab = dra.astype(BF16)
        diab = dia.astype(BF16)
        back = []
        for n in range(nb):
            cs = slice(n * HEAD_DIM, (n + 1) * HEAD_DIM)
            dwa_ref[n] += _dot(cxb[:, cs], drab[:, cs], TN)
            dwx_ref[n] += _dot(cxb[:, cs], diab[:, cs], TN)
            back.append(_dot(drab[:, cs], wa_ref[n], NT) + _dot(diab[:, cs], wx_ref[n], NT))
        dcx = dcx + jnp.concatenate(back, axis=1)
        dcb_ref[...] += jnp.sum(dcx, axis=0, keepdims=True)
        for tap in range(CONV_W):
            dcw_ref[tap:tap + 1, :] += jnp.sum(dcx * sh[CONV_W - 1 - tap], axis=0, keepdims=True)
        tail = dcx_next[...]
        dlx = dcx * cwv[3:4]
        for d in (1, 2, 3):
            dlx = dlx + _shift_up(dcx, d, tail) * cwv[3 - d:4 - d]
        dcx_next[...] = dcx[0:8]
        dlxg_ref[:, 0:DL] = dlx.astype(BF16)
        dlxg_ref[:, DL:2 * DL] = dlg.astype(BF16)

        @pl.when(i == nt - 1)
        def _():
            dlam_ref[...] = dlam_ref[...] * (1.0 - jax.nn.sigmoid(lamv))

    rev = lambda i: nt - 1 - i
    prev8_map = lambda i: (jnp.maximum((nt - 1 - i) * t8 - 1, 0), 0)
    vec = pl.BlockSpec((1, DL), lambda i: (0, 0))
    mat = pl.BlockSpec((nb, HEAD_DIM, HEAD_DIM), lambda i: (0, 0, 0))
    return _pcall(
        body, name="lru_bwd", grid=(nt,), operands=[proj, proj, proj, h, h, dy, cw, cb, wa, ba, wx, bx, lam], comm=comm,
        in_specs=[pl.BlockSpec((tl, DL), lambda i: (rev(i), 0)), pl.BlockSpec((8, DL), prev8_map),
                  pl.BlockSpec((tl, DL), lambda i: (rev(i), 1)),
                  pl.BlockSpec((tl, DL), lambda i: (rev(i), 0)), pl.BlockSpec((8, DL), prev8_map),
                  pl.BlockSpec((tl, DL), lambda i: (rev(i), 0))] + _lru_specs(tl, DL),
        out_specs=[pl.BlockSpec((tl, 2 * DL), lambda i: (rev(i), 0)), pl.BlockSpec((CONV_W, DL), lambda i: (0, 0)), vec,
                   mat, vec, mat, vec, vec],
        out_shape=[jax.ShapeDtypeStruct(proj.shape, BF16), jax.ShapeDtypeStruct((CONV_W, DL), F32),
                   jax.ShapeDtypeStruct((1, DL), F32), jax.ShapeDtypeStruct((nb, HEAD_DIM, HEAD_DIM), F32),
                   jax.ShapeDtypeStruct((1, DL), F32), jax.ShapeDtypeStruct((nb, HEAD_DIM, HEAD_DIM), F32),
                   jax.ShapeDtypeStruct((1, DL), F32), jax.ShapeDtypeStruct((1, DL), F32)],
        scratch_shapes=[pltpu.VMEM((8, DL), F32), pltpu.VMEM((8, DL), F32), pltpu.VMEM((8, DL), F32),
                        pltpu.VMEM((tl, DL), F32), pltpu.VMEM((tl, DL), F32), pltpu.VMEM((tl, DL), F32)])


def _ret_tables(S, H):
    pos = jnp.arange(S, dtype=F32)
    inv_freq = ROPE_BASE ** (-jnp.arange(0, HEAD_DIM, 2, dtype=F32) / HEAD_DIM)
    ang = pos[:, None] * inv_freq[None, :]
    cos, sin = jnp.cos(ang), jnp.sin(ang)
    cosf = jnp.concatenate([cos, cos], axis=1)
    sins = jnp.concatenate([-sin, sin], axis=1)
    log_gamma = jnp.log1p(-jnp.exp2(-5.0 - jnp.arange(H, dtype=F32)))
    idx = jnp.arange(CHUNK)
    diff = idx[:, None] - idx[None, :]
    causal = diff >= 0
    decay = jnp.where(causal[None], jnp.exp(log_gamma[:, None, None] * jnp.where(causal, diff, 0)[None].astype(F32)), 0.0)
    zeta = jnp.exp(log_gamma[:, None] * (CHUNK - 1 - idx).astype(F32)[None, :])
    xi = jnp.exp(log_gamma[:, None] * (idx + 1).astype(F32)[None, :])
    gc = jnp.exp(log_gamma * CHUNK)
    lanes = (H, CHUNK, HEAD_DIM)
    return (cosf, sins, decay, jnp.broadcast_to(zeta[:, :, None], lanes), jnp.broadcast_to(xi[:, :, None], lanes),
            jnp.broadcast_to(gc[:, None, None], lanes))


def _rope(t, cos, sin_signed):
    return t * cos + pltpu.roll(t, HEAD_DIM // 2, 1) * sin_signed


def _rope_t(d, cos, sin_signed):
    return d * cos + pltpu.roll(d * sin_signed, HEAD_DIM // 2, 1)


def _ret_const_specs(H, DR):
    full = pl.BlockSpec((H, CHUNK, HEAD_DIM), lambda *_: (0, 0, 0))
    return [full, full, full, full, pl.BlockSpec((1, DR), lambda *_: (0, 0))]


def _ret_fwd(proj, y, tables, gnw, tb, comm=None):
    S = proj.shape[0]
    DR = gnw.shape[1]
    H = DR // HEAD_DIM
    tb = _tile(S, tb, CHUNK)
    nc = tb // CHUNK
    cosf, sins, dm, zeta, xi, gc = tables
    scale = HEAD_DIM ** -0.5

    def body(qk_ref, vg_ref, cos_ref, sin_ref, dm_ref, zeta_ref, xi_ref, gc_ref, gnw_ref, y_in, y_ref, rprev_ref, r_s):
        del y_in
        i = pl.program_id(0)

        @pl.when(i == 0)
        def _():
            r_s[...] = jnp.zeros_like(r_s)

        def chunk(c, carry):
            rows = pl.ds(pl.multiple_of(c * CHUNK, CHUNK), CHUNK)
            cos = cos_ref[rows, :]
            sin = sin_ref[rows, :]
            heads = range(H)
            c0 = [slice(h * HEAD_DIM, (h + 1) * HEAD_DIM) for h in heads]
            c1 = [slice(DR + h * HEAD_DIM, DR + (h + 1) * HEAD_DIM) for h in heads]
            qh = [_rope(qk_ref[rows, c0[h]], cos, sin) for h in heads]
            kh = [_rope(qk_ref[rows, c1[h]], cos, sin) * scale for h in heads]
            vb = [vg_ref[rows, c0[h]].astype(BF16) for h in heads]
            rp = [r_s[h] for h in heads]
            rpb = [rp[h].astype(BF16) for h in heads]
            s = [_dot(qh[h].astype(BF16), kh[h].astype(BF16), NT) for h in heads]
            kv = [_dot((kh[h] * zeta_ref[h]).astype(BF16), vb[h], TN) for h in heads]
            cross = [_dot((qh[h] * xi_ref[h]).astype(BF16), rpb[h]) for h in heads]
            o = [_dot((s[h] * dm_ref[h]).astype(BF16), vb[h]) + cross[h] for h in heads]
            for h in heads:
                rprev_ref[c, h] = rpb[h]
                r_s[h] = rp[h] * gc_ref[h] + kv[h]
                mu = jnp.mean(o[h], axis=-1, keepdims=True)
                oc = o[h] - mu
                var = jnp.mean(oc * oc, axis=-1, keepdims=True)
                on = oc * lax.rsqrt(var + EPS) * gnw_ref[:, c0[h]]
                gate = vg_ref[rows, c1[h]]
                y_ref[rows, c0[h]] = (gate * jax.nn.sigmoid(gate) * on).astype(BF16)
            return carry

        lax.fori_loop(0, nc, chunk, 0)

    return _pcall(
        body, name="ret_fwd", grid=(S // tb,),
        in_specs=[pl.BlockSpec((tb, 2 * DR), lambda i: (i, 1)), pl.BlockSpec((tb, 2 * DR), lambda i: (i, 2)),
                  pl.BlockSpec((tb, HEAD_DIM), lambda i: (i, 0)), pl.BlockSpec((tb, HEAD_DIM), lambda i: (i, 0))]
        + _ret_const_specs(H, DR) + [HBM_SPEC],
        out_specs=[pl.BlockSpec((tb, DR), lambda i: (i, 1)),
                   pl.BlockSpec((nc, H, CHUNK, HEAD_DIM), lambda i: (i, 0, 0, 0))],
        out_shape=[jax.ShapeDtypeStruct(y.shape, BF16), jax.ShapeDtypeStruct((S // CHUNK, H, CHUNK, HEAD_DIM), BF16)],
        scratch_shapes=[pltpu.VMEM((H, CHUNK, HEAD_DIM), F32)], aliases={9: 0},
        operands=[proj, proj, cosf, sins, dm, zeta, xi, gc, gnw, y], comm=comm)


def _ret_bwd(proj, rprev, dy, dproj, tables, gnw, tb, comm=None):
    S = proj.shape[0]
    DR = gnw.shape[1]
    H = DR // HEAD_DIM
    tb = _tile(S, tb, CHUNK)
    nc = tb // CHUNK
    nt = S // tb
    cosf, sins, dm, zeta, xi, gc = tables
    scale = HEAD_DIM ** -0.5

    def body(qk_ref, vg_ref, cos_ref, sin_ref, dm_ref, zeta_ref, xi_ref, gc_ref, gnw_ref, rprev_ref, dy_ref, dp_in,
             dp_ref, dgn_ref, dr_s, dqk_s, dvg_s, out_sems):
        del dp_in
        i = pl.program_id(0)
        slot = i % 2

        def out_copies(step, sl):
            rows = pl.ds(pl.multiple_of((nt - 1 - step) * tb, tb), tb)
            return (pltpu.make_async_copy(dqk_s.at[sl], dp_ref.at[rows, pl.ds(2 * DR, 2 * DR)], out_sems.at[sl, 0]),
                    pltpu.make_async_copy(dvg_s.at[sl], dp_ref.at[rows, pl.ds(4 * DR, 2 * DR)], out_sems.at[sl, 1]))

        @pl.when(i == 0)
        def _():
            dr_s[...] = jnp.zeros_like(dr_s)
            dgn_ref[...] = jnp.zeros_like(dgn_ref)

        @pl.when(i >= 2)
        def _():
            for cp in out_copies(i - 2, slot):
                cp.wait()

        def chunk(cc, carry):
            c = nc - 1 - cc
            rows = pl.ds(pl.multiple_of(c * CHUNK, CHUNK), CHUNK)
            cos = cos_ref[rows, :]
            sin = sin_ref[rows, :]
            heads = range(H)
            c0 = [slice(h * HEAD_DIM, (h + 1) * HEAD_DIM) for h in heads]
            c1 = [slice(DR + h * HEAD_DIM, DR + (h + 1) * HEAD_DIM) for h in heads]
            qh = [_rope(qk_ref[rows, c0[h]], cos, sin) for h in heads]
            kh = [_rope(qk_ref[rows, c1[h]], cos, sin) * scale for h in heads]
            qb = [t.astype(BF16) for t in qh]
            kb = [t.astype(BF16) for t in kh]
            vb = [vg_ref[rows, c0[h]].astype(BF16) for h in heads]
            rpb = [rprev_ref[c, h] for h in heads]
            qx = [(qh[h] * xi_ref[h]).astype(BF16) for h in heads]
            kz = [(kh[h] * zeta_ref[h]).astype(BF16) for h in heads]
            drh = [dr_s[h] for h in heads]
            drb = [t.astype(BF16) for t in drh]
            s = [_dot(qb[h], kb[h], NT) for h in heads]
            cross = [_dot(qx[h], rpb[h]) for h in heads]
            dv_state = [_dot(kz[h], drb[h]) for h in heads]
            dk_state = [_dot(vb[h], drb[h], NT) for h in heads]
            sb = [(s[h] * dm_ref[h]).astype(BF16) for h in heads]
            o = [_dot(sb[h], vb[h]) + cross[h] for h in heads]
            dob = []
            for h in heads:
                mu = jnp.mean(o[h], axis=-1, keepdims=True)
                oc = o[h] - mu
                rstd = lax.rsqrt(jnp.mean(oc * oc, axis=-1, keepdims=True) + EPS)
                ohat = oc * rstd
                gw = gnw_ref[:, c0[h]]
                gate = vg_ref[rows, c1[h]]
                sg = jax.nn.sigmoid(gate)
                dyv = dy_ref[rows, c0[h]]
                dvg_s[slot, rows, c1[h]] = (dyv * (ohat * gw) * (sg * (1.0 + gate * (1.0 - sg)))).astype(BF16)
                don = dyv * (gate * sg)
                dgn_ref[:, c0[h]] += jnp.sum(don * ohat, axis=0, keepdims=True)
                dohat = don * gw
                do = rstd * (dohat - jnp.mean(dohat, axis=-1, keepdims=True)
                             - ohat * jnp.mean(dohat * ohat, axis=-1, keepdims=True))
                dob.append(do.astype(BF16))
            ds = [_dot(dob[h], vb[h], NT) for h in heads]
            dq_state = [_dot(dob[h], rpb[h], NT) for h in heads]
            dv = [_dot(sb[h], dob[h], TN) + dv_state[h] for h in heads]
            dr_new = [_dot(qx[h], dob[h], TN) for h in heads]
            dsb = [(ds[h] * dm_ref[h]).astype(BF16) for h in heads]
            dqh = [_dot(dsb[h], kb[h]) + dq_state[h] * xi_ref[h] for h in heads]
            dkh = [_dot(dsb[h], qb[h], TN) + dk_state[h] * zeta_ref[h] for h in heads]
            for h in heads:
                dr_s[h] = drh[h] * gc_ref[h] + dr_new[h]
                dqk_s[slot, rows, c0[h]] = _rope_t(dqh[h], cos, sin).astype(BF16)
                dqk_s[slot, rows, c1[h]] = _rope_t(dkh[h] * scale, cos, sin).astype(BF16)
                dvg_s[slot, rows, c0[h]] = dv[h].astype(BF16)
            return carry

        lax.fori_loop(0, nc, chunk, 0)
        for cp in out_copies(i, slot):
            cp.start()

        @pl.when(i == nt - 1)
        def _():
            if nt >= 2:
                for cp in out_copies(i - 1, 1 - slot):
                    cp.wait()
            for cp in out_copies(i, slot):
                cp.wait()

    rev = lambda i: nt - 1 - i
    return _pcall(
        body, name="ret_bwd", grid=(nt,), aliases={11: 0}, comm=comm,
        operands=[proj, proj, cosf, sins, dm, zeta, xi, gc, gnw, rprev, dy, dproj],
        in_specs=[pl.BlockSpec((tb, 2 * DR), lambda i: (rev(i), 1)), pl.BlockSpec((tb, 2 * DR), lambda i: (rev(i), 2)),
                  pl.BlockSpec((tb, HEAD_DIM), lambda i: (rev(i), 0)), pl.BlockSpec((tb, HEAD_DIM), lambda i: (rev(i), 0))]
        + _ret_const_specs(H, DR)
        + [pl.BlockSpec((nc, H, CHUNK, HEAD_DIM), lambda i: (rev(i), 0, 0, 0)),
           pl.BlockSpec((tb, DR), lambda i: (rev(i), 1)), HBM_SPEC],
        out_specs=[HBM_SPEC, pl.BlockSpec((1, DR), lambda i: (0, 0))],
        out_shape=[jax.ShapeDtypeStruct(dproj.shape, BF16), jax.ShapeDtypeStruct((1, DR), F32)],
        scratch_shapes=[pltpu.VMEM((H, CHUNK, HEAD_DIM), F32), pltpu.VMEM((2, tb, 2 * DR), BF16),
                        pltpu.VMEM((2, tb, 2 * DR), BF16), pltpu.SemaphoreType.DMA((2, 2))])


def _place():
    x, y, c = lax.axis_index("x"), lax.axis_index("y"), lax.axis_index("c")
    chips = [(1 - x, y), (x, 1 - y), (1 - x, 1 - y)]
    return x, y, c, chips


def _own_slab(name, shard, place):
    R, C = shard.shape
    tr = _row_tile(R, C)
    return _ew("cast_" + name, lambda a: (a,), [(shard, pl.BlockSpec((tr, C), lambda i, p: (i, 0)))],
               [(jax.ShapeDtypeStruct((4, R, C), BF16), pl.BlockSpec((None, tr, C), lambda i, p: (p[1], i, 0)))],
               (R // tr,), sp=place)[0]


class _remote:
    def __init__(self, src, dst, ssem, rsem, k, to):
        self.args = dict(src_ref=src, dst_ref=dst, send_sem=ssem.at[k], recv_sem=rsem.at[k], device_id=to,
                         device_id_type=MESH)

    def start(self):
        pltpu.make_async_remote_copy(**self.args).start()

    def wait_send(self):
        pltpu.make_async_remote_copy(**self.args).wait_send()

    def wait_recv(self):
        pltpu.make_async_remote_copy(**self.args).wait_recv()


def _task_fns(copies):
    def start(cins, couts, ssem, rsem, base):
        for cp in copies(cins, couts, ssem, rsem, base)[0]:
            cp.start()

    def finish(cins, couts, ssem, rsem, base):
        sends, recvs = copies(cins, couts, ssem, rsem, base)
        for cp in sends:
            cp.wait_send()
        for cp in recvs:
            cp.wait_recv()

    return start, finish


def _gather_ici(st):
    r2 = st.shape[1] // 2

    def copies(cins, couts, ssem, rsem, base):
        x, y, c, chips = _place()
        out = couts[0]
        mine = out.at[2 * x + y, pl.ds(c * r2, r2), :]
        sends = [_remote(mine, mine, ssem, rsem, base + j, (*chip, c)) for j, chip in enumerate(chips)]
        recvs = []
        for j, (cx, cy) in enumerate(chips):
            got = out.at[2 * cx + cy, pl.ds(c * r2, r2), :]
            recvs.append(_remote(got, got, ssem, rsem, base + j, (x, y, c)))
        return sends, recvs

    start, finish = _task_fns(copies)
    return _Comm([st], [jax.ShapeDtypeStruct(st.shape, st.dtype)], {0: 0}, 3, start, finish)


def _gather_d2d(st):
    r2 = st.shape[1] // 2

    def copies(cins, couts, ssem, rsem, base):
        x, y, c, chips = _place()
        out = couts[0]
        sends, recvs = [], []
        for j, (cx, cy) in enumerate(chips):
            have = out.at[2 * cx + cy, pl.ds(c * r2, r2), :]
            want = out.at[2 * cx + cy, pl.ds((1 - c) * r2, r2), :]
            sends.append(_remote(have, have, ssem, rsem, base + j, (x, y, 1 - c)))
            recvs.append(_remote(want, want, ssem, rsem, base + j, (x, y, c)))
        return sends, recvs

    start, finish = _task_fns(copies)
    return _Comm([st], [jax.ShapeDtypeStruct(st.shape, st.dtype)], {0: 0}, 3, start, finish)


def _gather_conv(conv_w):
    def copies(cins, couts, ssem, rsem, base):
        x, y, c, chips = _place()
        src, out = cins[0], couts[0]
        sends = [_remote(src, out.at[2 * x + y], ssem, rsem, base + j, (*chip, c)) for j, chip in enumerate(chips)]
        recvs = [_remote(src, out.at[2 * cx + cy], ssem, rsem, base + j, (x, y, c)) for j, (cx, cy) in enumerate(chips)]
        return sends, recvs

    start, finish = _task_fns(copies)
    return _Comm([conv_w], [jax.ShapeDtypeStruct((4,) + conv_w.shape, conv_w.dtype)], {}, 3, start, finish)


def _pair_exchange(g):
    r2 = g.shape[1] // 2

    def copies(cins, couts, ssem, rsem, base):
        x, y, c, _ = _place()
        cp = _remote(cins[0].at[:, pl.ds((1 - c) * r2, r2), :], couts[0], ssem, rsem, base, (x, y, 1 - c))
        return [cp], [cp]

    start, finish = _task_fns(copies)
    return _Comm([g], [jax.ShapeDtypeStruct((g.shape[0], r2, g.shape[2]), g.dtype)], {}, 1, start, finish)


def _chip_exchange(part):
    def copies(cins, couts, ssem, rsem, base):
        x, y, c, chips = _place()
        cps = [_remote(cins[0].at[2 * cx + cy], couts[0].at[j], ssem, rsem, base + j, (cx, cy, c))
               for j, (cx, cy) in enumerate(chips)]
        return cps, cps

    start, finish = _task_fns(copies)
    return _Comm([part], [jax.ShapeDtypeStruct((3,) + part.shape[1:], part.dtype)], {}, 3, start, finish)


def _pair_share(slot):
    def copies(cins, couts, ssem, rsem, base):
        x, y, c, _ = _place()
        out = couts[0]
        return ([_remote(out.at[c], out.at[c], ssem, rsem, base, (x, y, 1 - c))],
                [_remote(out.at[1 - c], out.at[1 - c], ssem, rsem, base, (x, y, c))])

    start, finish = _task_fns(copies)
    return _Comm([slot], [jax.ShapeDtypeStruct(slot.shape, slot.dtype)], {0: 0}, 1, start, finish)


def _gather_small(sm):
    flips = [(fx, fy, fc) for fx in (0, 1) for fy in (0, 1) for fc in (0, 1)][1:]

    def copies(cins, couts, ssem, rsem, base):
        x, y, c, _ = _place()
        src, out = cins[0], couts[0]
        peers = [(1 - x if fx else x, 1 - y if fy else y, 1 - c if fc else c) for fx, fy, fc in flips]
        sends = [_remote(src, out.at[4 * x + 2 * y + c], ssem, rsem, base + k, peer) for k, peer in enumerate(peers)]
        recvs = [_remote(src, out.at[4 * px + 2 * py + pc], ssem, rsem, base + k, (x, y, c))
                 for k, (px, py, pc) in enumerate(peers)]
        return sends, recvs

    start, finish = _task_fns(copies)
    return _Comm([sm], [jax.ShapeDtypeStruct((8,) + sm.shape, sm.dtype)], {}, 7, start, finish)


def _gather_both(st):
    ici, d2d = _gather_ici(st), _gather_d2d(st)

    def finish(cins, couts, ssem, rsem, base):
        ici.finish(cins, couts, ssem, rsem, base)
        d2d.start(cins, couts, ssem, rsem, base + ici.n_sem)
        d2d.finish(cins, couts, ssem, rsem, base + ici.n_sem)

    return _Comm(ici.ins, ici.outs, ici.aliases, ici.n_sem + d2d.n_sem, ici.start, finish)


def _comm_call(name, tasks):
    task = _merge(tasks)
    nci = len(task.ins)

    def body(*refs):
        cins, couts, (ssem, rsem) = refs[:nci], refs[nci:nci + len(task.outs)], refs[nci + len(task.outs):]
        task.start(cins, couts, ssem, rsem, 0)
        task.finish(cins, couts, ssem, rsem, 0)

    return pl.pallas_call(
        body, in_specs=[HBM_SPEC] * nci, out_specs=[HBM_SPEC] * len(task.outs), out_shape=list(task.outs),
        scratch_shapes=[pltpu.SemaphoreType.DMA((task.n_sem,)), pltpu.SemaphoreType.DMA((task.n_sem,))],
        input_output_aliases=task.aliases, name=name)(*task.ins)


def _adamw(w, g, m, v):
    m = ADAM_B1 * m + (1.0 - ADAM_B1) * g
    v = ADAM_B2 * v + (1.0 - ADAM_B2) * (g * g)
    m_hat = m / (1.0 - ADAM_B1 ** ADAM_STEP)
    v_hat = v / (1.0 - ADAM_B2 ** ADAM_STEP)
    delta = -ADAM_LR * (m_hat / (jnp.sqrt(v_hat) + ADAM_EPS) + ADAM_WD * w)
    return delta, m, v


def _adamw_call(name, w, g, m, v):
    R, C = w.shape
    tr = _row_tile(R, C, 1024 * 1024)
    row = pl.BlockSpec((tr, C), lambda i: (i, 0))
    o = jax.ShapeDtypeStruct((R, C), F32)
    return _ew(name, lambda w_, g_, m_, v_: (*_adamw(w_, g_, m_, v_), g_), [(w, row), (g, row), (m, row), (v, row)],
               [(o, row), (o, row), (o, row), (o, row)], (R // tr,))


def _pair_sum(name, g, ra, place):
    _, R, C = g.shape
    r2 = R // 2
    tr = _row_tile(r2, C)
    nb = r2 // tr
    own = pl.BlockSpec((None, tr, C), lambda j, i, p: (j, p[0] * nb + i, 0))
    blk = pl.BlockSpec((None, tr, C), lambda j, i, p: (j, i, 0))
    return _ew("rs_pair_sum_" + name, lambda a, b: (a + b,), [(g, own), (ra, blk)],
               [(jax.ShapeDtypeStruct((4, r2, C), BF16), blk)], (4, nb), sp=place)[0]


def _chip_sum(name, g, ra, rb, place):
    _, R, C = g.shape
    r2 = R // 2
    tr = _row_tile(r2, C)
    nb = r2 // tr
    own = pl.BlockSpec((None, tr, C), lambda i, p: (p[1], p[0] * nb + i, 0))
    mine = pl.BlockSpec((None, tr, C), lambda i, p: (p[1], i, 0))
    src = [pl.BlockSpec((None, tr, C), functools.partial(lambda i, p, j: (j, i, 0), j=j)) for j in range(3)]
    out = pl.BlockSpec((None, tr, C), lambda i, p: (p[0], i, 0))

    def total(a, b, r0, r1, r2_):
        return ((((a + b) + r0.astype(F32)) + r1.astype(F32)) + r2_.astype(F32),)

    return _ew("rs_chip_sum_" + name, total, [(g, own), (ra, mine), (rb, src[0]), (rb, src[1]), (rb, src[2])],
               [(jax.ShapeDtypeStruct((2, r2, C), F32), out)], (nb,), sp=place)[0]


def _pack(arrays):
    rows, offs, pos = [], [], 0
    for a in arrays:
        flat = a.reshape(-1)
        n = -(-flat.shape[0] // (8 * LANES)) * (8 * LANES)
        if n != flat.shape[0]:
            flat = jnp.pad(flat, (0, n - flat.shape[0]))
        rows.append(flat.reshape(-1, LANES))
        offs.append(pos)
        pos += n // LANES
    return jnp.concatenate(rows, axis=0), offs


def _unpack(packed, offs, shapes):
    out = []
    for off, shp in zip(offs, shapes):
        n = 1
        for s in shp:
            n *= s
        out.append(packed[off:off + -(-n // LANES)].reshape(-1)[:n].reshape(shp))
    return out


def _sum8(gathered):
    _, R, C = gathered.shape
    tr = _row_tile(R, C, 256 * 1024)
    specs = [pl.BlockSpec((None, tr, C), functools.partial(lambda i, d: (d, i, 0), d=d)) for d in range(8)]

    def fn(*parts):
        t = parts[0]
        for p in parts[1:]:
            t = t + p
        return (t,)

    return _ew("small_sum", fn, [(gathered, s) for s in specs],
               [(jax.ShapeDtypeStruct((R, C), F32), pl.BlockSpec((tr, C), lambda i: (i, 0)))], (R // tr,))[0]


BIG = ("w_in", "w_out", "w_ffn_gate", "w_ffn_up", "w_ffn_down")
SMALL = ("ln1_w", "conv_w", "conv_b", "gate_a_w", "gate_a_b", "gate_x_w", "gate_x_b", "lru_lambda", "ret_gn_w", "ln2_w",
         "final_norm_w")
WEIGHTS = ("ln1_w", "w_in", "conv_w", "conv_b", "gate_a_w", "gate_a_b", "gate_x_w", "gate_x_b", "lru_lambda", "ret_gn_w",
           "w_out", "ln2_w", "w_ffn_gate", "w_ffn_up", "w_ffn_down", "final_norm_w")


def kernel(x, ln1_w, w_in, conv_w, conv_b, gate_a_w, gate_a_b, gate_x_w, gate_x_b, lru_lambda, ret_gn_w, w_out, ln2_w, w_ffn_gate, w_ffn_up, w_ffn_down, final_norm_w, loss_target, m_ln1_w, m_w_in, m_conv_w, m_conv_b, m_gate_a_w, m_gate_a_b, m_gate_x_w, m_gate_x_b, m_lru_lambda, m_ret_gn_w, m_w_out, m_ln2_w, m_w_ffn_gate, m_w_ffn_up, m_w_ffn_down, m_final_norm_w, v_ln1_w, v_w_in, v_conv_w, v_conv_b, v_gate_a_w, v_gate_a_b, v_gate_x_w, v_gate_x_b, v_lru_lambda, v_ret_gn_w, v_w_out, v_ln2_w, v_w_ffn_gate, v_w_ffn_up, v_w_ffn_down, v_final_norm_w):
    w = dict(ln1_w=ln1_w, w_in=w_in, conv_w=conv_w, conv_b=conv_b, gate_a_w=gate_a_w, gate_a_b=gate_a_b, gate_x_w=gate_x_w,
             gate_x_b=gate_x_b, lru_lambda=lru_lambda, ret_gn_w=ret_gn_w, w_out=w_out, ln2_w=ln2_w, w_ffn_gate=w_ffn_gate,
             w_ffn_up=w_ffn_up, w_ffn_down=w_ffn_down, final_norm_w=final_norm_w)
    m = dict(ln1_w=m_ln1_w, w_in=m_w_in, conv_w=m_conv_w, conv_b=m_conv_b, gate_a_w=m_gate_a_w, gate_a_b=m_gate_a_b,
             gate_x_w=m_gate_x_w, gate_x_b=m_gate_x_b, lru_lambda=m_lru_lambda, ret_gn_w=m_ret_gn_w, w_out=m_w_out,
             ln2_w=m_ln2_w, w_ffn_gate=m_w_ffn_gate, w_ffn_up=m_w_ffn_up, w_ffn_down=m_w_ffn_down,
             final_norm_w=m_final_norm_w)
    v = dict(ln1_w=v_ln1_w, w_in=v_w_in, conv_w=v_conv_w, conv_b=v_conv_b, gate_a_w=v_gate_a_w, gate_a_b=v_gate_a_b,
             gate_x_w=v_gate_x_w, gate_x_b=v_gate_x_b, lru_lambda=v_lru_lambda, ret_gn_w=v_ret_gn_w, w_out=v_w_out,
             ln2_w=v_ln2_w, w_ffn_gate=v_w_ffn_gate, w_ffn_up=v_w_ffn_up, w_ffn_down=v_w_ffn_down,
             final_norm_w=v_final_norm_w)
    xs, tgt = x[0], loss_target[0]
    S, D = xs.shape
    DL, DR = conv_b.shape[1], ret_gn_w.shape[1]
    assert DL == DR and DL % HEAD_DIM == 0 and S % CHUNK == 0
    d_mix = DL + DR
    cx, cy, cc = lax.axis_index("x"), lax.axis_index("y"), lax.axis_index("c")
    chip = 2 * cx + cy
    place = jnp.stack([cc, chip]).astype(jnp.int32)
    grad, delta, new_m, new_v = {}, {}, {}, {}

    def finish_big(n, full):
        shp = w[n].shape
        g2 = full.reshape(shp[1], shp[2])
        w2, m2, v2 = (t[n].reshape(shp[1], shp[2]) for t in (w, m, v))
        d_, m_, v_, g_ = _adamw_call("adamw_" + n, w2, g2, m2, v2)
        grad[n], delta[n], new_m[n], new_v[n] = (t.reshape(shp) for t in (g_, d_, m_, v_))

    def all_sum(gathered, own):
        return _sum8(lax.dynamic_update_slice(gathered, own[None], (4 * cx + 2 * cy + cc, 0, 0)))

    st = {n: _own_slab(n, w[n][0], place) for n in BIG}
    w_in_st, conv_st = _comm_call("gather_w_in", [_gather_both(st["w_in"]), _gather_conv(conv_w[0])])
    conv_st = lax.dynamic_update_slice(conv_st, conv_w, (chip, 0, 0))
    cw_cols = conv_st.shape[2]
    conv_full = jnp.transpose(conv_st, (1, 0, 2)).reshape(CONV_W, 4 * cw_cols)
    n_in, n_ff = w_in_st.shape[2], st["w_ffn_gate"].shape[2]
    tables = _ret_tables(S, DR // HEAD_DIM)
    wab, wxb = gate_a_w[0].astype(BF16), gate_x_w[0].astype(BF16)
    lru_w = (conv_full, conv_b, wab, gate_a_b, wxb, gate_x_b, lru_lambda)
    TM, TK = 512, 1024

    u1 = _rms_fwd("rms1", xs, ln1_w, TM)
    proj, (w_out_st, wg_st) = _mm_nn_stacked("proj", u1, w_in_st, F32, TM,
                                             comm=_merge([_gather_ici(st["w_out"]), _gather_ici(st["w_ffn_gate"])]))
    (hs, y), (w_out_st, wg_st, wu_st) = _lru_fwd(
        proj, *lru_w, 128, d_mix, comm=_merge([_gather_d2d(w_out_st), _gather_d2d(wg_st), _gather_ici(st["w_ffn_up"])]))
    (y, rprev), (wu_st,) = _ret_fwd(proj, y, tables, ret_gn_w, 256, comm=_gather_d2d(wu_st))
    w_out_f = w_out_st.reshape(d_mix, D)
    (h1, u2), (wd_st,) = _out_proj_rms(y, w_out_f, xs, ln2_w, TM, comm=_gather_ici(st["w_ffn_down"]))
    (gt, up, ff), (wd_st,) = _ffn_gate_up(u2, wg_st, wu_st, TM, comm=_gather_d2d(wd_st))
    wd_f = wd_st.reshape(4 * n_ff, D)
    dh2, dh2b, d_fw, loss = _ffn_down_loss(ff, wd_f, h1, tgt, final_norm_w.reshape(1, D), 256)

    g_wd = _mm_tn("g_w_down", ff, dh2b, n_ff, 1024, TK).reshape(4, n_ff, D)
    (dgt, dup), (ra_wd,) = _ffn_gate_up_bwd(dh2b, wd_f, gt, up, TM, n_ff, comm=_pair_exchange(g_wd))
    pb_wd = _pair_sum("w_ffn_down", g_wd, ra_wd, place)
    g_wg, (rb_wd,) = _mm_tn("g_w_gate", u2, dgt, 1024, None, TK, stacked_cols=n_ff, comm=_chip_exchange(pb_wd))
    slot_wd = _chip_sum("w_ffn_down", g_wd, ra_wd, rb_wd, place)
    g_wu, (full_wd, ra_wg) = _mm_tn("g_w_up", u2, dup, 1024, None, TK, stacked_cols=n_ff,
                                    comm=_merge([_pair_share(slot_wd), _pair_exchange(g_wg)]))
    finish_big("w_ffn_down", full_wd)
    pb_wg = _pair_sum("w_ffn_gate", g_wg, ra_wg, place)
    du2, (rb_wg, ra_wu) = _mm_nt_stacked("d_u2", [dgt, dup], [wg_st, wu_st], F32, 256, 1024,
                                         comm=_merge([_chip_exchange(pb_wg), _pair_exchange(g_wu)]))
    slot_wg = _chip_sum("w_ffn_gate", g_wg, ra_wg, rb_wg, place)
    pb_wu = _pair_sum("w_ffn_up", g_wu, ra_wu, place)
    (dh1, dh1b, dy, d_ln2), (full_wg,) = _rms_bwd_dy(h1, ln2_w, du2, dh2, w_out_f, 256, comm=_pair_share(slot_wg))
    finish_big("w_ffn_gate", full_wg)
    g_wout = _mm_tn("g_w_out", y, dh1b, 1024, 1024, TK).reshape(4, d_mix // 4, D)
    (dproj, d_cw, d_cb, d_wa, d_ba, d_wx, d_bx, d_lam), (rb_wu, ra_wout) = _lru_bwd(
        proj, hs, dy, *lru_w, 128, comm=_merge([_chip_exchange(pb_wu), _pair_exchange(g_wout)]))
    slot_wu = _chip_sum("w_ffn_up", g_wu, ra_wu, rb_wu, place)
    pb_wout = _pair_sum("w_out", g_wout, ra_wout, place)
    (dproj, d_gn), (full_wu, rb_wout) = _ret_bwd(proj, rprev, dy, dproj, tables, ret_gn_w, 256,
                                                 comm=_merge([_pair_share(slot_wu), _chip_exchange(pb_wout)]))
    finish_big("w_ffn_up", full_wu)
    slot_wout = _chip_sum("w_out", g_wout, ra_wout, rb_wout, place)
    small = dict(conv_w=d_cw, conv_b=d_cb, gate_a_w=d_wa, gate_a_b=d_ba, gate_x_w=d_wx, gate_x_b=d_bx, lru_lambda=d_lam,
                 ret_gn_w=d_gn, ln2_w=d_ln2, final_norm_w=d_fw)
    packed, offs = _pack([small[n] for n in SMALL[1:]] + [loss])
    g_win, (full_wout, got_small) = _mm_tn("g_w_in", u1, dproj, 1024, None, TK, stacked_cols=n_in,
                                           comm=_merge([_pair_share(slot_wout), _gather_small(packed)]))
    finish_big("w_out", full_wout)
    du1, (ra_win,) = _mm_nt_stacked("d_u1", [dproj], [w_in_st], F32, 256, D, comm=_pair_exchange(g_win))
    pb_win = _pair_sum("w_in", g_win, ra_win, place)
    (gx, d_ln1), (rb_win,) = _rms_bwd("rms1_bwd", xs, ln1_w, du1, dh1, TM, comm=_chip_exchange(pb_win))
    slot_win = _chip_sum("w_in", g_win, ra_win, rb_win, place)
    packed1, _ = _pack([d_ln1])
    full_win, got_ln1 = _comm_call("reduce_tail", [_pair_share(slot_win), _gather_small(packed1)])
    finish_big("w_in", full_win)

    red = _unpack(all_sum(got_small, packed), offs, [small[n].shape for n in SMALL[1:]] + [(1, LANES)])
    g = dict(zip(SMALL[1:], red[:-1]))
    g["ln1_w"] = all_sum(got_ln1, packed1)[:-(-D // LANES)].reshape(1, D)
    loss_out = red[-1][0, 0]
    g["conv_w"] = lax.dynamic_slice(g["conv_w"], (0, chip * cw_cols), (CONV_W, cw_cols))
    packs = [_pack([t[n] for n in SMALL])[0] for t in (w, m, v)]
    gp, offs2 = _pack([g[n] for n in SMALL])
    outs = _adamw_call("adamw_small", packs[0], gp, packs[1], packs[2])
    shapes = [w[n].shape for n in SMALL]
    for dst, arr in zip((delta, new_m, new_v), outs):
        dst.update(zip(SMALL, _unpack(arr, offs2, shapes)))
    for n in SMALL:
        grad[n] = g[n].reshape(w[n].shape)

    return (loss_out, gx.reshape(x.shape), *[grad[n] for n in WEIGHTS], *[delta[n] for n in WEIGHTS],
            *[new_m[n] for n in WEIGHTS], *[new_v[n] for n in WEIGHTS])
```

```python
import functools

import jax
import jax.numpy as jnp
from jax import lax
from jax.experimental import pallas as pl
from jax.experimental.pallas import tpu as pltpu

F32 = jnp.float32
BF16 = jnp.bfloat16
MESH = pl.DeviceIdType.MESH

EPS = 1e-6
LRU_C = 8.0
ROPE_BASE = 10000.0
CHUNK = 128
HEAD_DIM = 128
CONV_W = 4
ADAM_LR = 0.001
ADAM_B1 = 0.9
ADAM_B2 = 0.999
ADAM_EPS = 1e-08
ADAM_WD = 0.01
ADAM_STEP = 10

V7X_VMEM_BYTES = 64 * 1024 * 1024
VMEM_LIMIT = V7X_VMEM_BYTES - 8 * 1024 * 1024
LANES = 128
SUBLANES_16BIT = 16

NN = (((1,), (0,)), ((), ()))
NT = (((1,), (1,)), ((), ()))
TN = (((0,), (0,)), ((), ()))


def _dot(a, b, dims=NN):
    return lax.dot_general(a, b, dims, preferred_element_type=F32)


def _tile(n, pref, mult=SUBLANES_16BIT):
    best = None
    t = mult
    while t <= min(n, pref):
        if n % t == 0:
            best = t
        t += mult
    return best if best is not None else n


def _row_tile(rows, cols, budget_bytes=2 * 1024 * 1024):
    return _tile(rows, max(SUBLANES_16BIT, budget_bytes // (cols * 4)))


def _params(sem):
    return pltpu.CompilerParams(dimension_semantics=sem, vmem_limit_bytes=VMEM_LIMIT)


HBM_SPEC = pl.BlockSpec(memory_space=pl.ANY)


class _Comm:
    def __init__(self, ins, outs, aliases, n_sem, start, finish):
        self.ins, self.outs, self.aliases, self.n_sem, self.start, self.finish = ins, outs, aliases, n_sem, start, finish


def _merge(tasks):
    ins, outs, aliases, plans, n_sem = [], [], {}, [], 0
    for t in tasks:
        i0, o0 = len(ins), len(outs)
        plans.append((t, i0, o0, n_sem))
        ins += t.ins
        outs += t.outs
        aliases.update({i0 + a: o0 + b for a, b in t.aliases.items()})
        n_sem += t.n_sem

    def run(which):
        def go(cins, couts, ssem, rsem, base):
            for t, i0, o0, s0 in plans:
                getattr(t, which)(cins[i0:i0 + len(t.ins)], couts[o0:o0 + len(t.outs)], ssem, rsem, base + s0)
        return go

    return _Comm(ins, outs, aliases, n_sem, run("start"), run("finish"))


def _pcall(body, *, name, grid, in_specs, out_specs, out_shape, operands, scratch_shapes=(), aliases=None, comm=None):
    n_in, n_out, n_scr = len(operands), len(out_shape), len(scratch_shapes)
    aliases = dict(aliases or {})
    params = _params(("arbitrary",) * len(grid))
    if comm is None:
        return pl.pallas_call(body, grid=grid, in_specs=list(in_specs), out_specs=list(out_specs), out_shape=list(out_shape),
                              scratch_shapes=list(scratch_shapes), input_output_aliases=aliases, name=name,
                              compiler_params=params)(*operands)
    nci, nco = len(comm.ins), len(comm.outs)

    def wrapped(*refs):
        ins, cins = refs[:n_in], refs[n_in:n_in + nci]
        o0 = n_in + nci
        outs, couts = refs[o0:o0 + n_out], refs[o0 + n_out:o0 + n_out + nco]
        s0 = o0 + n_out + nco
        scr, (ssem, rsem) = refs[s0:s0 + n_scr], refs[s0 + n_scr:]
        ids = [pl.program_id(a) for a in range(len(grid))]
        first = functools.reduce(jnp.logical_and, [i == 0 for i in ids])
        last = functools.reduce(jnp.logical_and, [i == g - 1 for i, g in zip(ids, grid)])

        @pl.when(first)
        def _():
            comm.start(cins, couts, ssem, rsem, 0)

        body(*ins, *outs, *scr)

        @pl.when(last)
        def _():
            comm.finish(cins, couts, ssem, rsem, 0)

    aliases.update({n_in + a: n_out + b for a, b in comm.aliases.items()})
    res = pl.pallas_call(
        wrapped, grid=grid, in_specs=list(in_specs) + [HBM_SPEC] * nci, out_specs=list(out_specs) + [HBM_SPEC] * nco,
        out_shape=list(out_shape) + list(comm.outs),
        scratch_shapes=list(scratch_shapes) + [pltpu.SemaphoreType.DMA((comm.n_sem,)), pltpu.SemaphoreType.DMA((comm.n_sem,))],
        input_output_aliases=aliases, name=name, compiler_params=params)(*operands, *comm.ins)
    return res[:n_out], res[n_out:]


def _ew(name, fn, ins, outs, grid, sp=None):
    n_in = len(ins)

    def body(*refs):
        if sp is not None:
            refs = refs[1:]
        vals = [r[...] for r in refs[:n_in]]
        res = fn(*vals)
        for o_ref, v in zip(refs[n_in:], res):
            o_ref[...] = v.astype(o_ref.dtype)

    in_specs = [s for _, s in ins]
    out_specs = [s for _, s in outs]
    out_shape = [s for s, _ in outs]
    sem = ("arbitrary",) * len(grid)
    if sp is None:
        return pl.pallas_call(body, grid=grid, in_specs=in_specs, out_specs=out_specs, out_shape=out_shape,
                              name=name, compiler_params=_params(sem))(*[a for a, _ in ins])
    gs = pltpu.PrefetchScalarGridSpec(num_scalar_prefetch=1, grid=grid, in_specs=in_specs, out_specs=out_specs)
    return pl.pallas_call(body, grid_spec=gs, out_shape=out_shape, name=name,
                          compiler_params=_params(sem))(sp, *[a for a, _ in ins])


def _matmul(name, pairs, dims, grid, out_shape, out_spec, acc_shape, res=None, comm=None):
    n = len(pairs)
    nk = grid[2]

    def body(*refs):
        ab = refs[:2 * n]
        pos = 2 * n
        res_ref = None
        if res is not None:
            res_ref = refs[pos]
            pos += 1
        o_ref = refs[pos]
        acc_ref = refs[pos + 1] if nk > 1 else None

        def partial():
            t = None
            for p in range(n):
                d = _dot(ab[2 * p][...], ab[2 * p + 1][...], dims)
                t = d if t is None else t + d
            return t

        def finish(t):
            if res_ref is not None:
                t = t + res_ref[...]
            o_ref[...] = t.astype(o_ref.dtype)

        if nk == 1:
            finish(partial())
        else:
            k = pl.program_id(2)

            @pl.when(k == 0)
            def _():
                acc_ref[...] = partial()

            @pl.when(k > 0)
            def _():
                acc_ref[...] += partial()

            @pl.when(k == nk - 1)
            def _():
                finish(acc_ref[...])

    operands, in_specs = [], []
    for a, a_spec, b, b_spec in pairs:
        operands += [a, b]
        in_specs += [a_spec, b_spec]
    if res is not None:
        operands.append(res[0])
        in_specs.append(res[1])
    scratch = [pltpu.VMEM(acc_shape, F32)] if nk > 1 else []
    res = _pcall(body, name=name, grid=grid, in_specs=in_specs, out_specs=[out_spec], out_shape=[out_shape],
                 operands=operands, scratch_shapes=scratch, comm=comm)
    return res[0] if comm is None else (res[0][0], res[1])


def _mm_nn_stacked(name, a, b_st, out_dtype, tm, comm=None):
    M, K = a.shape
    J, _, Nj = b_st.shape
    tm = _tile(M, tm)
    return _matmul(
        name, [(a, pl.BlockSpec((tm, K), lambda j, i, k: (i, 0)), b_st, pl.BlockSpec((None, K, Nj), lambda j, i, k: (j, 0, 0)))],
        NN, (J, M // tm, 1), jax.ShapeDtypeStruct((M, J * Nj), out_dtype), pl.BlockSpec((tm, Nj), lambda j, i, k: (i, j)), None,
        comm=comm)


def _mm_nt_stacked(name, a_list, b_list, out_dtype, tm, tn, comm=None):
    n = len(a_list)
    M = a_list[0].shape[0]
    J, N, Nj = b_list[0].shape
    tm, tn = _tile(M, tm), _tile(N, tn, LANES)

    def body(*refs):
        o_ref = refs[2 * n]
        t = None
        for p in range(n):
            a_ref, b_ref = refs[p], refs[n + p]
            for s in range(J):
                d = _dot(a_ref[:, s * Nj:(s + 1) * Nj], b_ref[s], NT)
                t = d if t is None else t + d
        o_ref[...] = t.astype(o_ref.dtype)

    a_spec = pl.BlockSpec((tm, J * Nj), lambda j, i: (i, 0))
    b_spec = pl.BlockSpec((J, tn, Nj), lambda j, i: (0, j, 0), pipeline_mode=pl.Buffered(1))
    res = _pcall(body, name=name, grid=(N // tn, M // tm), in_specs=[a_spec] * n + [b_spec] * n,
                 out_specs=[pl.BlockSpec((tm, tn), lambda j, i: (i, j))], out_shape=[jax.ShapeDtypeStruct((M, N), out_dtype)],
                 operands=[*a_list, *b_list], comm=comm)
    return res[0] if comm is None else (res[0][0], res[1])


MXU_COLUMNS = 256


def _col_blocks(n):
    return [slice(s, min(s + MXU_COLUMNS, n)) for s in range(0, n, MXU_COLUMNS)]


def _ffn_gate_up(u2, wg_st, wu_st, tm, comm=None):
    S, D = u2.shape
    J, _, Nj = wg_st.shape
    tm = _tile(S, tm)

    def body(a_ref, wg_ref, wu_ref, dg_ref, du_ref, ff_ref):
        a = a_ref[...]
        blocks = _col_blocks(Nj)
        ahead = (_dot(a, wg_ref[:, blocks[0]]), _dot(a, wu_ref[:, blocks[0]]))
        for j, cols in enumerate(blocks):
            g, u = ahead
            if j + 1 < len(blocks):
                ahead = (_dot(a, wg_ref[:, blocks[j + 1]]), _dot(a, wu_ref[:, blocks[j + 1]]))
            sg = jax.nn.sigmoid(g)
            silu = g * sg
            dg_ref[:, cols] = (u * (sg * (1.0 + g * (1.0 - sg)))).astype(BF16)
            du_ref[:, cols] = silu.astype(BF16)
            ff_ref[:, cols] = (silu * u).astype(BF16)

    w_spec = pl.BlockSpec((None, D, Nj), lambda j, i: (j, 0, 0))
    o_spec = pl.BlockSpec((tm, Nj), lambda j, i: (i, j))
    o = jax.ShapeDtypeStruct((S, J * Nj), BF16)
    return _pcall(body, name="ffn_gate_up", grid=(J, S // tm),
                  in_specs=[pl.BlockSpec((tm, D), lambda j, i: (i, 0)), w_spec, w_spec],
                  out_specs=[o_spec, o_spec, o_spec], out_shape=[o, o, o], operands=[u2, wg_st, wu_st], comm=comm)


def _ffn_gate_up_bwd(dh2b, wd, dg_fac, du_fac, tm, tn, comm=None):
    S, D = dh2b.shape
    F = wd.shape[0]
    tm, tn = _tile(S, tm), _tile(F, tn, LANES)

    def body(a_ref, wd_ref, dg_ref, du_ref, dgt_ref, dup_ref):
        a = a_ref[...]
        for cols in _col_blocks(tn):
            d = _dot(a, wd_ref[cols, :], NT)
            dgt_ref[:, cols] = (d * dg_ref[:, cols].astype(F32)).astype(BF16)
            dup_ref[:, cols] = (d * du_ref[:, cols].astype(F32)).astype(BF16)

    blk = pl.BlockSpec((tm, tn), lambda j, i: (i, j))
    o = jax.ShapeDtypeStruct((S, F), BF16)
    return _pcall(body, name="ffn_gate_up_bwd", grid=(F // tn, S // tm),
                  in_specs=[pl.BlockSpec((tm, D), lambda j, i: (i, 0)), pl.BlockSpec((tn, D), lambda j, i: (j, 0)), blk, blk],
                  out_specs=[blk, blk], out_shape=[o, o], operands=[dh2b, wd, dg_fac, du_fac], comm=comm)


def _mm_tn(name, a, b, tmo, tn, tk, stacked_cols=None, comm=None):
    S, Mo = a.shape
    N = b.shape[1]
    tmo, tk = _tile(Mo, tmo, LANES), _tile(S, tk)
    if stacked_cols is None:
        tn = _tile(N, tn, LANES)
        out_shape = jax.ShapeDtypeStruct((Mo, N), F32)
        out_spec = pl.BlockSpec((tmo, tn), lambda i, j, k: (i, j))
    else:
        tn = stacked_cols
        out_shape = jax.ShapeDtypeStruct((N // tn, Mo, tn), F32)
        out_spec = pl.BlockSpec((None, tmo, tn), lambda i, j, k: (j, i, 0))
    return _matmul(
        name, [(a, pl.BlockSpec((tk, tmo), lambda i, j, k: (k, i)), b, pl.BlockSpec((tk, tn), lambda i, j, k: (k, j)))],
        TN, (Mo // tmo, N // tn, S // tk), out_shape, out_spec, (tmo, tn), comm=comm)


def _rms_fwd(name, x, w, tm, comm=None):
    S, D = x.shape
    tm = _tile(S, tm)

    def body(x_ref, w_ref, o_ref):
        xv = x_ref[...]
        r = lax.rsqrt(jnp.mean(xv * xv, axis=-1, keepdims=True) + EPS)
        o_ref[...] = ((xv * r) * w_ref[...]).astype(BF16)

    row = pl.BlockSpec((tm, D), lambda i: (i, 0))
    return _pcall(body, name=name, grid=(S // tm,), in_specs=[row, pl.BlockSpec((1, D), lambda i: (0, 0))], out_specs=[row],
                  out_shape=[jax.ShapeDtypeStruct((S, D), BF16)], operands=[x, w], comm=comm)


def _rms_bwd(name, x, w, dy, dres, tm, comm=None):
    S, D = x.shape
    tm = _tile(S, tm)

    def body(x_ref, w_ref, dy_ref, dres_ref, dx_ref, dw_ref):
        i = pl.program_id(0)

        @pl.when(i == 0)
        def _():
            dw_ref[...] = jnp.zeros_like(dw_ref)

        xv = x_ref[...]
        r = lax.rsqrt(jnp.mean(xv * xv, axis=-1, keepdims=True) + EPS)
        nv = xv * r
        dyv = dy_ref[...]
        dn = dyv * w_ref[...]
        dw_ref[...] += jnp.sum(dyv * nv, axis=0, keepdims=True)
        dx = dres_ref[...] + r * (dn - nv * jnp.mean(dn * nv, axis=-1, keepdims=True))
        dx_ref[...] = dx

    row = pl.BlockSpec((tm, D), lambda i: (i, 0))
    vec = pl.BlockSpec((1, D), lambda i: (0, 0))
    return _pcall(body, name=name, grid=(S // tm,), in_specs=[row, vec, row, row], out_specs=[row, vec],
                  out_shape=[jax.ShapeDtypeStruct((S, D), F32), jax.ShapeDtypeStruct((1, D), F32)],
                  operands=[x, w, dy, dres], comm=comm)


def _rms_bwd_dy(h1, w, du2, dh2, w_out, tm, comm=None):
    S, D = h1.shape
    d_mix = w_out.shape[0]
    tm = _tile(S, tm)

    def body(x_ref, w_ref, dy_ref, dres_ref, wo_ref, dx_ref, dxb_ref, out_ref, dw_ref):
        i = pl.program_id(0)

        @pl.when(i == 0)
        def _():
            dw_ref[...] = jnp.zeros_like(dw_ref)

        xv = x_ref[...]
        r = lax.rsqrt(jnp.mean(xv * xv, axis=-1, keepdims=True) + EPS)
        nv = xv * r
        dyv = dy_ref[...]
        dn = dyv * w_ref[...]
        dw_ref[...] += jnp.sum(dyv * nv, axis=0, keepdims=True)
        dx = dres_ref[...] + r * (dn - nv * jnp.mean(dn * nv, axis=-1, keepdims=True))
        dx_ref[...] = dx
        dxb = dx.astype(BF16)
        dxb_ref[...] = dxb
        out_ref[...] = _dot(dxb, wo_ref[...], NT)

    row = pl.BlockSpec((tm, D), lambda i: (i, 0))
    vec = pl.BlockSpec((1, D), lambda i: (0, 0))
    return _pcall(
        body, name="rms2_bwd_dy", grid=(S // tm,),
        in_specs=[row, vec, row, row, pl.BlockSpec((d_mix, D), lambda i: (0, 0), pipeline_mode=pl.Buffered(1))],
        out_specs=[row, row, pl.BlockSpec((tm, d_mix), lambda i: (i, 0)), vec],
        out_shape=[jax.ShapeDtypeStruct((S, D), F32), jax.ShapeDtypeStruct((S, D), BF16),
                   jax.ShapeDtypeStruct((S, d_mix), F32), jax.ShapeDtypeStruct((1, D), F32)],
        operands=[h1, w, du2, dh2, w_out], comm=comm)


def _out_proj_rms(y, w_out, x, ln_w, tm, comm=None):
    S, K = y.shape
    D = w_out.shape[1]
    tm = _tile(S, tm)

    def body(a_ref, w_ref, x_ref, lw_ref, h_ref, u_ref):
        hv = _dot(a_ref[...], w_ref[...]) + x_ref[...]
        h_ref[...] = hv
        r = lax.rsqrt(jnp.mean(hv * hv, axis=-1, keepdims=True) + EPS)
        u_ref[...] = ((hv * r) * lw_ref[...]).astype(BF16)

    row = pl.BlockSpec((tm, D), lambda i: (i, 0))
    return _pcall(
        body, name="out_proj", grid=(S // tm,),
        in_specs=[pl.BlockSpec((tm, K), lambda i: (i, 0)),
                  pl.BlockSpec((K, D), lambda i: (0, 0), pipeline_mode=pl.Buffered(1)), row,
                  pl.BlockSpec((1, D), lambda i: (0, 0))],
        out_specs=[row, row], out_shape=[jax.ShapeDtypeStruct((S, D), F32), jax.ShapeDtypeStruct((S, D), BF16)],
        operands=[y, w_out, x, ln_w], comm=comm)


def _ffn_down_loss(ff, wd, h1, tgt, fw, tm):
    S, K = ff.shape
    D = wd.shape[1]
    tm = _tile(S, tm)

    def body(a_ref, wd_ref, h1_ref, t_ref, w_ref, dh_ref, dhb_ref, dw_ref, loss_ref):
        i = pl.program_id(0)

        @pl.when(i == 0)
        def _():
            dw_ref[...] = jnp.zeros_like(dw_ref)
            loss_ref[...] = jnp.zeros_like(loss_ref)

        hv = _dot(a_ref[...], wd_ref[...]) + h1_ref[...]
        wv = w_ref[...]
        r = lax.rsqrt(jnp.mean(hv * hv, axis=-1, keepdims=True) + EPS)
        nv = hv * r
        err = nv * wv - t_ref[...]
        row_loss = jnp.mean(err * err, axis=-1, keepdims=True)
        loss_ref[...] += 0.5 * jnp.sum(row_loss, axis=0, keepdims=True)
        dyo = err * (1.0 / D)
        dn = dyo * wv
        dw_ref[...] += jnp.sum(dyo * nv, axis=0, keepdims=True)
        dh = r * (dn - nv * jnp.mean(dn * nv, axis=-1, keepdims=True))
        dh_ref[...] = dh
        dhb_ref[...] = dh.astype(BF16)

    row = pl.BlockSpec((tm, D), lambda i: (i, 0))
    vec = pl.BlockSpec((1, D), lambda i: (0, 0))
    return _pcall(
        body, name="ffn_down_loss", grid=(S // tm,),
        in_specs=[pl.BlockSpec((tm, K), lambda i: (i, 0)),
                  pl.BlockSpec((K, D), lambda i: (0, 0), pipeline_mode=pl.Buffered(1)), row, row, vec],
        out_specs=[row, row, vec, pl.BlockSpec((1, LANES), lambda i: (0, 0))],
        out_shape=[jax.ShapeDtypeStruct((S, D), F32), jax.ShapeDtypeStruct((S, D), BF16),
                   jax.ShapeDtypeStruct((1, D), F32), jax.ShapeDtypeStruct((1, LANES), F32)],
        operands=[ff, wd, h1, tgt, fw])


def _shift_down(x, d, head8):
    r = pltpu.roll(x, d, 0)
    rh = pltpu.roll(head8, d, 0)
    row8 = lax.broadcasted_iota(jnp.int32, head8.shape, 0)
    top = jnp.where(row8 < d, rh, r[0:8])
    return jnp.concatenate([top, r[8:]], axis=0)


def _shift_up(x, d, tail8):
    n = x.shape[0]
    r = pltpu.roll(x, n - d, 0)
    rt = pltpu.roll(tail8, 8 - d, 0)
    row8 = lax.broadcasted_iota(jnp.int32, tail8.shape, 0)
    bot = jnp.where(row8 + d >= 8, rt, r[n - 8:n])
    return jnp.concatenate([r[:n - 8], bot], axis=0)


def _log_sigmoid(lam):
    z = jnp.exp(-jnp.abs(lam))
    u = 1.0 + z
    log1p = jnp.where(u == 1.0, z, jnp.log(u) * (z / jnp.where(u == 1.0, 1.0, u - 1.0)))
    return jnp.minimum(lam, 0.0) - log1p


def _neg_expm1(z, exp_z):
    series = -z * (1.0 + z * (0.5 + z * (1.0 / 6.0)))
    return jnp.where(z > -0.02, series, 1.0 - exp_z)


_GELU_C = 0.7978845608028654


def _gelu(x):
    t = jnp.tanh(_GELU_C * (x + 0.044715 * (x * x * x)))
    return x * (0.5 * (1.0 + t)), t


def _gelu_grad(x, t):
    return 0.5 * (1.0 + t) + 0.5 * x * (1.0 - t * t) * (_GELU_C * (1.0 + 3.0 * 0.044715 * (x * x)))


def _lru_gates(lx, head8, cw, cb, wa_ref, ba, wx_ref, bx, ls):
    nb = wa_ref.shape[0]
    sh = [lx] + [_shift_down(lx, d, head8) for d in (1, 2, 3)]
    cx = cb + sh[3] * cw[0:1]
    cx = cx + sh[2] * cw[1:2]
    cx = cx + sh[1] * cw[2:3]
    cx = cx + sh[0] * cw[3:4]
    cxb = cx.astype(BF16)
    ra = jnp.concatenate([_dot(cxb[:, n * HEAD_DIM:(n + 1) * HEAD_DIM], wa_ref[n]) for n in range(nb)], axis=1) + ba
    ia = jnp.concatenate([_dot(cxb[:, n * HEAD_DIM:(n + 1) * HEAD_DIM], wx_ref[n]) for n in range(nb)], axis=1) + bx
    r = jax.nn.sigmoid(ra)
    ig = jax.nn.sigmoid(ia)
    log_a = LRU_C * r * ls
    a = jnp.exp(log_a)
    m2 = _neg_expm1(2.0 * log_a, a * a)
    return sh, cx, cxb, r, ig, a, m2, jnp.sqrt(m2)


def _lru_specs(tl, DL):
    nb = DL // HEAD_DIM
    vec = pl.BlockSpec((1, DL), lambda i: (0, 0))
    return [pl.BlockSpec((CONV_W, DL), lambda i: (0, 0)), vec,
            pl.BlockSpec((nb, HEAD_DIM, HEAD_DIM), lambda i: (0, 0, 0)), vec,
            pl.BlockSpec((nb, HEAD_DIM, HEAD_DIM), lambda i: (0, 0, 0)), vec, vec]


def _lru_fwd(proj, cw, cb, wa, ba, wx, bx, lam, tl, d_mix, comm=None):
    S = proj.shape[0]
    DL = cb.shape[1]
    tl = _tile(S, tl)

    def body(lx_ref, lg_ref, cw_ref, cb_ref, wa_ref, ba_ref, wx_ref, bx_ref, lam_ref, h_ref, y_ref, prev8, hc, a_s, b_s):
        i = pl.program_id(0)

        @pl.when(i == 0)
        def _():
            prev8[...] = jnp.zeros_like(prev8)
            hc[...] = jnp.zeros_like(hc)

        lx = lx_ref[...]
        ls = _log_sigmoid(lam_ref[...])
        _, cx, _, _, ig, a, _, mult = _lru_gates(lx, prev8[...], cw_ref[...], cb_ref[...], wa_ref, ba_ref[...],
                                                 wx_ref, bx_ref[...], ls)
        b = mult * (ig * cx)
        row = lax.broadcasted_iota(jnp.int32, a.shape, 0) & 7
        for d in (1, 2, 4):
            a_sh = pltpu.roll(a, d, 0)
            b_sh = pltpu.roll(b, d, 0)
            m = row >= d
            b = jnp.where(m, a * b_sh + b, b)
            a = jnp.where(m, a * a_sh, a)
        a_s[...] = a
        b_s[...] = b

        def step(g, hprev):
            sl = pl.ds(pl.multiple_of(g * 8, 8), 8)
            hh = a_s[sl, :] * hprev + b_s[sl, :]
            h_ref[sl, :] = hh
            return hh[7:8, :]

        hc[0:1, :] = lax.fori_loop(0, tl // 8, step, hc[0:1, :])
        prev8[...] = lx[tl - 8:tl]
        g, _ = _gelu(lg_ref[...])
        y_ref[...] = (h_ref[...] * g).astype(BF16)

    return _pcall(
        body, name="lru_fwd", grid=(S // tl,),
        in_specs=[pl.BlockSpec((tl, DL), lambda i: (i, 0)), pl.BlockSpec((tl, DL), lambda i: (i, 1))] + _lru_specs(tl, DL),
        out_specs=[pl.BlockSpec((tl, DL), lambda i: (i, 0)), pl.BlockSpec((tl, DL), lambda i: (i, 0))],
        out_shape=[jax.ShapeDtypeStruct((S, DL), F32), jax.ShapeDtypeStruct((S, d_mix), BF16)],
        scratch_shapes=[pltpu.VMEM((8, DL), F32), pltpu.VMEM((8, DL), F32), pltpu.VMEM((tl, DL), F32), pltpu.VMEM((tl, DL), F32)],
        operands=[proj, proj, cw, cb, wa, ba, wx, bx, lam], comm=comm)


def _lru_bwd(proj, h, dy, cw, cb, wa, ba, wx, bx, lam, tl, comm=None):
    S = proj.shape[0]
    DL = cb.shape[1]
    nb = DL // HEAD_DIM
    tl = _tile(S, tl)
    nt = S // tl
    ng = tl // 8
    t8 = tl // 8

    def body(lx_ref, lxp_ref, lg_ref, h_ref, hp_ref, dy_ref, cw_ref, cb_ref, wa_ref, ba_ref, wx_ref, bx_ref, lam_ref,
             dlxg_ref, dcw_ref, dcb_ref, dwa_ref, dba_ref, dwx_ref, dbx_ref, dlam_ref,
             a_next, g_carry, dcx_next, an_s, dh_s, g_s):
        i = pl.program_id(0)

        @pl.when(i == 0)
        def _():
            for ref in (dcw_ref, dcb_ref, dwa_ref, dba_ref, dwx_ref, dbx_ref, dlam_ref, a_next, g_carry, dcx_next):
                ref[...] = jnp.zeros_like(ref)

        first = i == nt - 1
        lx = lx_ref[...]
        hv = h_ref[...]
        lg = lg_ref[...]
        dyv = dy_ref[...]
        head8 = jnp.where(first, 0.0, lxp_ref[...])
        hhead8 = jnp.where(first, 0.0, hp_ref[...])
        lamv = lam_ref[...]
        ls = _log_sigmoid(lamv)
        cwv = cw_ref[...]
        sh, cx, cxb, r, ig, a, m2, mult = _lru_gates(lx, head8, cwv, cb_ref[...], wa_ref, ba_ref[...], wx_ref, bx_ref[...],
                                                     ls)
        hprev = _shift_down(hv, 1, hhead8)
        g, t = _gelu(lg)
        dlg = dyv * hv * _gelu_grad(lg, t)
        dh = dyv * g
        an = _shift_up(a, 1, a_next[...])
        row = lax.broadcasted_iota(jnp.int32, a.shape, 0) & 7
        for d in (1, 2, 4):
            an_sh = pltpu.roll(an, tl - d, 0)
            dh_sh = pltpu.roll(dh, tl - d, 0)
            m = row + d < 8
            dh = jnp.where(m, an * dh_sh + dh, dh)
            an = jnp.where(m, an * an_sh, an)
        an_s[...] = an
        dh_s[...] = dh

        def step(k, gc):
            sl = pl.ds(pl.multiple_of((ng - 1 - k) * 8, 8), 8)
            gg = an_s[sl, :] * gc + dh_s[sl, :]
            g_s[sl, :] = gg
            return gg[0:1, :]

        g_carry[0:1, :] = lax.fori_loop(0, ng, step, g_carry[0:1, :])
        a_next[...] = a[0:8]
        G = g_s[...]
        da = G * hprev
        icx = ig * cx
        dmult = G * icx
        dicx = G * mult
        di = dicx * cx
        dcx = dicx * ig
        dlog = da * a - dmult * ((a * a) * lax.rsqrt(m2))
        dr = dlog * (LRU_C * ls)
        dlam_ref[...] += jnp.sum(dlog * (LRU_C * r), axis=0, keepdims=True)
        dra = dr * r * (1.0 - r)
        dia = di * ig * (1.0 - ig)
        dba_ref[...] += jnp.sum(dra, axis=0, keepdims=True)
        dbx_ref[...] += jnp.sum(dia, axis=0, keepdims=True)
        drab = dra.astype(BF16)
        diab = dia.astype(BF16)
        back = []
        for n in range(nb):
            cs = slice(n * HEAD_DIM, (n + 1) * HEAD_DIM)
            dwa_ref[n] += _dot(cxb[:, cs], drab[:, cs], TN)
            dwx_ref[n] += _dot(cxb[:, cs], diab[:, cs], TN)
            back.append(_dot(drab[:, cs], wa_ref[n], NT) + _dot(diab[:, cs], wx_ref[n], NT))
        dcx = dcx + jnp.concatenate(back, axis=1)
        dcb_ref[...] += jnp.sum(dcx, axis=0, keepdims=True)
        for tap in range(CONV_W):
            dcw_ref[tap:tap + 1, :] += jnp.sum(dcx * sh[CONV_W - 1 - tap], axis=0, keepdims=True)
        tail = dcx_next[...]
        dlx = dcx * cwv[3:4]
        for d in (1, 2, 3):
            dlx = dlx + _shift_up(dcx, d, tail) * cwv[3 - d:4 - d]
        dcx_next[...] = dcx[0:8]
        dlxg_ref[:, 0:DL] = dlx.astype(BF16)
        dlxg_ref[:, DL:2 * DL] = dlg.astype(BF16)

        @pl.when(i == nt - 1)
        def _():
            dlam_ref[...] = dlam_ref[...] * (1.0 - jax.nn.sigmoid(lamv))

    rev = lambda i: nt - 1 - i
    prev8_map = lambda i: (jnp.maximum((nt - 1 - i) * t8 - 1, 0), 0)
    vec = pl.BlockSpec((1, DL), lambda i: (0, 0))
    mat = pl.BlockSpec((nb, HEAD_DIM, HEAD_DIM), lambda i: (0, 0, 0))
    return _pcall(
        body, name="lru_bwd", grid=(nt,), operands=[proj, proj, proj, h, h, dy, cw, cb, wa, ba, wx, bx, lam], comm=comm,
        in_specs=[pl.BlockSpec((tl, DL), lambda i: (rev(i), 0)), pl.BlockSpec((8, DL), prev8_map),
                  pl.BlockSpec((tl, DL), lambda i: (rev(i), 1)),
                  pl.BlockSpec((tl, DL), lambda i: (rev(i), 0)), pl.BlockSpec((8, DL), prev8_map),
                  pl.BlockSpec((tl, DL), lambda i: (rev(i), 0))] + _lru_specs(tl, DL),
        out_specs=[pl.BlockSpec((tl, 2 * DL), lambda i: (rev(i), 0)), pl.BlockSpec((CONV_W, DL), lambda i: (0, 0)), vec,
                   mat, vec, mat, vec, vec],
        out_shape=[jax.ShapeDtypeStruct(proj.shape, BF16), jax.ShapeDtypeStruct((CONV_W, DL), F32),
                   jax.ShapeDtypeStruct((1, DL), F32), jax.ShapeDtypeStruct((nb, HEAD_DIM, HEAD_DIM), F32),
                   jax.ShapeDtypeStruct((1, DL), F32), jax.ShapeDtypeStruct((nb, HEAD_DIM, HEAD_DIM), F32),
                   jax.ShapeDtypeStruct((1, DL), F32), jax.ShapeDtypeStruct((1, DL), F32)],
        scratch_shapes=[pltpu.VMEM((8, DL), F32), pltpu.VMEM((8, DL), F32), pltpu.VMEM((8, DL), F32),
                        pltpu.VMEM((tl, DL), F32), pltpu.VMEM((tl, DL), F32), pltpu.VMEM((tl, DL), F32)])


def _ret_tables(S, H):
    pos = jnp.arange(S, dtype=F32)
    inv_freq = ROPE_BASE ** (-jnp.arange(0, HEAD_DIM, 2, dtype=F32) / HEAD_DIM)
    ang = pos[:, None] * inv_freq[None, :]
    cos, sin = jnp.cos(ang), jnp.sin(ang)
    cosf = jnp.concatenate([cos, cos], axis=1)
    sins = jnp.concatenate([-sin, sin], axis=1)
    log_gamma = jnp.log1p(-jnp.exp2(-5.0 - jnp.arange(H, dtype=F32)))
    idx = jnp.arange(CHUNK)
    diff = idx[:, None] - idx[None, :]
    causal = diff >= 0
    decay = jnp.where(causal[None], jnp.exp(log_gamma[:, None, None] * jnp.where(causal, diff, 0)[None].astype(F32)), 0.0)
    zeta = jnp.exp(log_gamma[:, None] * (CHUNK - 1 - idx).astype(F32)[None, :])
    xi = jnp.exp(log_gamma[:, None] * (idx + 1).astype(F32)[None, :])
    gc = jnp.exp(log_gamma * CHUNK)
    lanes = (H, CHUNK, HEAD_DIM)
    return (cosf, sins, decay, jnp.broadcast_to(zeta[:, :, None], lanes), jnp.broadcast_to(xi[:, :, None], lanes),
            jnp.broadcast_to(gc[:, None, None], lanes))


def _rope(t, cos, sin_signed):
    return t * cos + pltpu.roll(t, HEAD_DIM // 2, 1) * sin_signed


def _rope_t(d, cos, sin_signed):
    return d * cos + pltpu.roll(d * sin_signed, HEAD_DIM // 2, 1)


def _ret_const_specs(H, DR):
    full = pl.BlockSpec((H, CHUNK, HEAD_DIM), lambda *_: (0, 0, 0))
    return [full, full, full, full, pl.BlockSpec((1, DR), lambda *_: (0, 0))]


def _ret_fwd(proj, y, tables, gnw, tb, comm=None):
    S = proj.shape[0]
    DR = gnw.shape[1]
    H = DR // HEAD_DIM
    tb = _tile(S, tb, CHUNK)
    nc = tb // CHUNK
    cosf, sins, dm, zeta, xi, gc = tables
    scale = HEAD_DIM ** -0.5

    def body(qk_ref, vg_ref, cos_ref, sin_ref, dm_ref, zeta_ref, xi_ref, gc_ref, gnw_ref, y_in, y_ref, rprev_ref, r_s):
        del y_in
        i = pl.program_id(0)

        @pl.when(i == 0)
        def _():
            r_s[...] = jnp.zeros_like(r_s)

        def chunk(c, carry):
            rows = pl.ds(pl.multiple_of(c * CHUNK, CHUNK), CHUNK)
            cos = cos_ref[rows, :]
            sin = sin_ref[rows, :]
            heads = range(H)
            c0 = [slice(h * HEAD_DIM, (h + 1) * HEAD_DIM) for h in heads]
            c1 = [slice(DR + h * HEAD_DIM, DR + (h + 1) * HEAD_DIM) for h in heads]
            qh = [_rope(qk_ref[rows, c0[h]], cos, sin) for h in heads]
            kh = [_rope(qk_ref[rows, c1[h]], cos, sin) * scale for h in heads]
            vb = [vg_ref[rows, c0[h]].astype(BF16) for h in heads]
            rp = [r_s[h] for h in heads]
            rpb = [rp[h].astype(BF16) for h in heads]
            s = [_dot(qh[h].astype(BF16), kh[h].astype(BF16), NT) for h in heads]
            kv = [_dot((kh[h] * zeta_ref[h]).astype(BF16), vb[h], TN) for h in heads]
            cross = [_dot((qh[h] * xi_ref[h]).astype(BF16), rpb[h]) for h in heads]
            o = [_dot((s[h] * dm_ref[h]).astype(BF16), vb[h]) + cross[h] for h in heads]
            for h in heads:
                rprev_ref[c, h] = rpb[h]
                r_s[h] = rp[h] * gc_ref[h] + kv[h]
                mu = jnp.mean(o[h], axis=-1, keepdims=True)
                oc = o[h] - mu
                var = jnp.mean(oc * oc, axis=-1, keepdims=True)
                on = oc * lax.rsqrt(var + EPS) * gnw_ref[:, c0[h]]
                gate = vg_ref[rows, c1[h]]
                y_ref[rows, c0[h]] = (gate * jax.nn.sigmoid(gate) * on).astype(BF16)
            return carry

        lax.fori_loop(0, nc, chunk, 0)

    return _pcall(
        body, name="ret_fwd", grid=(S // tb,),
        in_specs=[pl.BlockSpec((tb, 2 * DR), lambda i: (i, 1)), pl.BlockSpec((tb, 2 * DR), lambda i: (i, 2)),
                  pl.BlockSpec((tb, HEAD_DIM), lambda i: (i, 0)), pl.BlockSpec((tb, HEAD_DIM), lambda i: (i, 0))]
        + _ret_const_specs(H, DR) + [HBM_SPEC],
        out_specs=[pl.BlockSpec((tb, DR), lambda i: (i, 1)),
                   pl.BlockSpec((nc, H, CHUNK, HEAD_DIM), lambda i: (i, 0, 0, 0))],
        out_shape=[jax.ShapeDtypeStruct(y.shape, BF16), jax.ShapeDtypeStruct((S // CHUNK, H, CHUNK, HEAD_DIM), BF16)],
        scratch_shapes=[pltpu.VMEM((H, CHUNK, HEAD_DIM), F32)], aliases={9: 0},
        operands=[proj, proj, cosf, sins, dm, zeta, xi, gc, gnw, y], comm=comm)


def _ret_bwd(proj, rprev, dy, dproj, tables, gnw, tb, comm=None):
    S = proj.shape[0]
    DR = gnw.shape[1]
    H = DR // HEAD_DIM
    tb = _tile(S, tb, CHUNK)
    nc = tb // CHUNK
    nt = S // tb
    cosf, sins, dm, zeta, xi, gc = tables
    scale = HEAD_DIM ** -0.5

    def body(qk_ref, vg_ref, cos_ref, sin_ref, dm_ref, zeta_ref, xi_ref, gc_ref, gnw_ref, rprev_ref, dy_ref, dp_in,
             dp_ref, dgn_ref, dr_s, dqk_s, dvg_s, out_sems):
        del dp_in
        i = pl.program_id(0)
        slot = i % 2

        def out_copies(step, sl):
            rows = pl.ds(pl.multiple_of((nt - 1 - step) * tb, tb), tb)
            return (pltpu.make_async_copy(dqk_s.at[sl], dp_ref.at[rows, pl.ds(2 * DR, 2 * DR)], out_sems.at[sl, 0]),
                    pltpu.make_async_copy(dvg_s.at[sl], dp_ref.at[rows, pl.ds(4 * DR, 2 * DR)], out_sems.at[sl, 1]))

        @pl.when(i == 0)
        def _():
            dr_s[...] = jnp.zeros_like(dr_s)
            dgn_ref[...] = jnp.zeros_like(dgn_ref)

        @pl.when(i >= 2)
        def _():
            for cp in out_copies(i - 2, slot):
                cp.wait()

        def chunk(cc, carry):
            c = nc - 1 - cc
            rows = pl.ds(pl.multiple_of(c * CHUNK, CHUNK), CHUNK)
            cos = cos_ref[rows, :]
            sin = sin_ref[rows, :]
            heads = range(H)
            c0 = [slice(h * HEAD_DIM, (h + 1) * HEAD_DIM) for h in heads]
            c1 = [slice(DR + h * HEAD_DIM, DR + (h + 1) * HEAD_DIM) for h in heads]
            qh = [_rope(qk_ref[rows, c0[h]], cos, sin) for h in heads]
            kh = [_rope(qk_ref[rows, c1[h]], cos, sin) * scale for h in heads]
            qb = [t.astype(BF16) for t in qh]
            kb = [t.astype(BF16) for t in kh]
            vb = [vg_ref[rows, c0[h]].astype(BF16) for h in heads]
            rpb = [rprev_ref[c, h] for h in heads]
            qx = [(qh[h] * xi_ref[h]).astype(BF16) for h in heads]
            kz = [(kh[h] * zeta_ref[h]).astype(BF16) for h in heads]
            drh = [dr_s[h] for h in heads]
            drb = [t.astype(BF16) for t in drh]
            s = [_dot(qb[h], kb[h], NT) for h in heads]
            cross = [_dot(qx[h], rpb[h]) for h in heads]
            dv_state = [_dot(kz[h], drb[h]) for h in heads]
            dk_state = [_dot(vb[h], drb[h], NT) for h in heads]
            sb = [(s[h] * dm_ref[h]).astype(BF16) for h in heads]
            o = [_dot(sb[h], vb[h]) + cross[h] for h in heads]
            dob = []
            for h in heads:
                mu = jnp.mean(o[h], axis=-1, keepdims=True)
                oc = o[h] - mu
                rstd = lax.rsqrt(jnp.mean(oc * oc, axis=-1, keepdims=True) + EPS)
                ohat = oc * rstd
                gw = gnw_ref[:, c0[h]]
                gate = vg_ref[rows, c1[h]]
                sg = jax.nn.sigmoid(gate)
                dyv = dy_ref[rows, c0[h]]
                dvg_s[slot, rows, c1[h]] = (dyv * (ohat * gw) * (sg * (1.0 + gate * (1.0 - sg)))).astype(BF16)
                don = dyv * (gate * sg)
                dgn_ref[:, c0[h]] += jnp.sum(don * ohat, axis=0, keepdims=True)
                dohat = don * gw
                do = rstd * (dohat - jnp.mean(dohat, axis=-1, keepdims=True)
                             - ohat * jnp.mean(dohat * ohat, axis=-1, keepdims=True))
                dob.append(do.astype(BF16))
            ds = [_dot(dob[h], vb[h], NT) for h in heads]
            dq_state = [_dot(dob[h], rpb[h], NT) for h in heads]
            dv = [_dot(sb[h], dob[h], TN) + dv_state[h] for h in heads]
            dr_new = [_dot(qx[h], dob[h], TN) for h in heads]
            dsb = [(ds[h] * dm_ref[h]).astype(BF16) for h in heads]
            dqh = [_dot(dsb[h], kb[h]) + dq_state[h] * xi_ref[h] for h in heads]
            dkh = [_dot(dsb[h], qb[h], TN) + dk_state[h] * zeta_ref[h] for h in heads]
            for h in heads:
                dr_s[h] = drh[h] * gc_ref[h] + dr_new[h]
                dqk_s[slot, rows, c0[h]] = _rope_t(dqh[h], cos, sin).astype(BF16)
                dqk_s[slot, rows, c1[h]] = _rope_t(dkh[h] * scale, cos, sin).astype(BF16)
                dvg_s[slot, rows, c0[h]] = dv[h].astype(BF16)
            return carry

        lax.fori_loop(0, nc, chunk, 0)
        for cp in out_copies(i, slot):
            cp.start()

        @pl.when(i == nt - 1)
        def _():
            if nt >= 2:
                for cp in out_copies(i - 1, 1 - slot):
                    cp.wait()
            for cp in out_copies(i, slot):
                cp.wait()

    rev = lambda i: nt - 1 - i
    return _pcall(
        body, name="ret_bwd", grid=(nt,), aliases={11: 0}, comm=comm,
        operands=[proj, proj, cosf, sins, dm, zeta, xi, gc, gnw, rprev, dy, dproj],
        in_specs=[pl.BlockSpec((tb, 2 * DR), lambda i: (rev(i), 1)), pl.BlockSpec((tb, 2 * DR), lambda i: (rev(i), 2)),
                  pl.BlockSpec((tb, HEAD_DIM), lambda i: (rev(i), 0)), pl.BlockSpec((tb, HEAD_DIM), lambda i: (rev(i), 0))]
        + _ret_const_specs(H, DR)
        + [pl.BlockSpec((nc, H, CHUNK, HEAD_DIM), lambda i: (rev(i), 0, 0, 0)),
           pl.BlockSpec((tb, DR), lambda i: (rev(i), 1)), HBM_SPEC],
        out_specs=[HBM_SPEC, pl.BlockSpec((1, DR), lambda i: (0, 0))],
        out_shape=[jax.ShapeDtypeStruct(dproj.shape, BF16), jax.ShapeDtypeStruct((1, DR), F32)],
        scratch_shapes=[pltpu.VMEM((H, CHUNK, HEAD_DIM), F32), pltpu.VMEM((2, tb, 2 * DR), BF16),
                        pltpu.VMEM((2, tb, 2 * DR), BF16), pltpu.SemaphoreType.DMA((2, 2))])


def _place():
    x, y, c = lax.axis_index("x"), lax.axis_index("y"), lax.axis_index("c")
    chips = [(1 - x, y), (x, 1 - y), (1 - x, 1 - y)]
    return x, y, c, chips


def _own_slab(name, shard, place):
    R, C = shard.shape
    tr = _row_tile(R, C)
    return _ew("cast_" + name, lambda a: (a,), [(shard, pl.BlockSpec((tr, C), lambda i, p: (i, 0)))],
               [(jax.ShapeDtypeStruct((4, R, C), BF16), pl.BlockSpec((None, tr, C), lambda i, p: (p[1], i, 0)))],
               (R // tr,), sp=place)[0]


class _remote:
    def __init__(self, src, dst, ssem, rsem, k, to):
        self.args = dict(src_ref=src, dst_ref=dst, send_sem=ssem.at[k], recv_sem=rsem.at[k], device_id=to,
                         device_id_type=MESH)

    def start(self):
        pltpu.make_async_remote_copy(**self.args).start()

    def wait_send(self):
        pltpu.make_async_remote_copy(**self.args).wait_send()

    def wait_recv(self):
        pltpu.make_async_remote_copy(**self.args).wait_recv()


def _task_fns(copies):
    def start(cins, couts, ssem, rsem, base):
        for cp in copies(cins, couts, ssem, rsem, base)[0]:
            cp.start()

    def finish(cins, couts, ssem, rsem, base):
        sends, recvs = copies(cins, couts, ssem, rsem, base)
        for cp in sends:
            cp.wait_send()
        for cp in recvs:
            cp.wait_recv()

    return start, finish


def _gather_ici(st):
    r2 = st.shape[1] // 2

    def copies(cins, couts, ssem, rsem, base):
        x, y, c, chips = _place()
        out = couts[0]
        mine = out.at[2 * x + y, pl.ds(c * r2, r2), :]
        sends = [_remote(mine, mine, ssem, rsem, base + j, (*chip, c)) for j, chip in enumerate(chips)]
        recvs = []
        for j, (cx, cy) in enumerate(chips):
            got = out.at[2 * cx + cy, pl.ds(c * r2, r2), :]
            recvs.append(_remote(got, got, ssem, rsem, base + j, (x, y, c)))
        return sends, recvs

    start, finish = _task_fns(copies)
    return _Comm([st], [jax.ShapeDtypeStruct(st.shape, st.dtype)], {0: 0}, 3, start, finish)


def _gather_d2d(st):
    r2 = st.shape[1] // 2

    def copies(cins, couts, ssem, rsem, base):
        x, y, c, chips = _place()
        out = couts[0]
        sends, recvs = [], []
        for j, (cx, cy) in enumerate(chips):
            have = out.at[2 * cx + cy, pl.ds(c * r2, r2), :]
            want = out.at[2 * cx + cy, pl.ds((1 - c) * r2, r2), :]
            sends.append(_remote(have, have, ssem, rsem, base + j, (x, y, 1 - c)))
            recvs.append(_remote(want, want, ssem, rsem, base + j, (x, y, c)))
        return sends, recvs

    start, finish = _task_fns(copies)
    return _Comm([st], [jax.ShapeDtypeStruct(st.shape, st.dtype)], {0: 0}, 3, start, finish)


def _gather_conv(conv_w):
    def copies(cins, couts, ssem, rsem, base):
        x, y, c, chips = _place()
        src, out = cins[0], couts[0]
        sends = [_remote(src, out.at[2 * x + y], ssem, rsem, base + j, (*chip, c)) for j, chip in enumerate(chips)]
        recvs = [_remote(src, out.at[2 * cx + cy], ssem, rsem, base + j, (x, y, c)) for j, (cx, cy) in enumerate(chips)]
        return sends, recvs

    start, finish = _task_fns(copies)
    return _Comm([conv_w], [jax.ShapeDtypeStruct((4,) + conv_w.shape, conv_w.dtype)], {}, 3, start, finish)


def _pair_exchange(g):
    r2 = g.shape[1] // 2

    def copies(cins, couts, ssem, rsem, base):
        x, y, c, _ = _place()
        cp = _remote(cins[0].at[:, pl.ds((1 - c) * r2, r2), :], couts[0], ssem, rsem, base, (x, y, 1 - c))
        return [cp], [cp]

    start, finish = _task_fns(copies)
    return _Comm([g], [jax.ShapeDtypeStruct((g.shape[0], r2, g.shape[2]), g.dtype)], {}, 1, start, finish)


def _chip_exchange(part):
    def copies(cins, couts, ssem, rsem, base):
        x, y, c, chips = _place()
        cps = [_remote(cins[0].at[2 * cx + cy], couts[0].at[j], ssem, rsem, base + j, (cx, cy, c))
               for j, (cx, cy) in enumerate(chips)]
        return cps, cps

    start, finish = _task_fns(copies)
    return _Comm([part], [jax.ShapeDtypeStruct((3,) + part.shape[1:], part.dtype)], {}, 3, start, finish)


def _pair_share(slot):
    def copies(cins, couts, ssem, rsem, base):
        x, y, c, _ = _place()
        out = couts[0]
        return ([_remote(out.at[c], out.at[c], ssem, rsem, base, (x, y, 1 - c))],
                [_remote(out.at[1 - c], out.at[1 - c], ssem, rsem, base, (x, y, c))])

    start, finish = _task_fns(copies)
    return _Comm([slot], [jax.ShapeDtypeStruct(slot.shape, slot.dtype)], {0: 0}, 1, start, finish)


def _gather_small(sm):
    flips = [(fx, fy, fc) for fx in (0, 1) for fy in (0, 1) for fc in (0, 1)][1:]

    def copies(cins, couts, ssem, rsem, base):
        x, y, c, _ = _place()
        src, out = cins[0], couts[0]
        peers = [(1 - x if fx else x, 1 - y if fy else y, 1 - c if fc else c) for fx, fy, fc in flips]
        sends = [_remote(src, out.at[4 * x + 2 * y + c], ssem, rsem, base + k, peer) for k, peer in enumerate(peers)]
        recvs = [_remote(src, out.at[4 * px + 2 * py + pc], ssem, rsem, base + k, (x, y, c))
                 for k, (px, py, pc) in enumerate(peers)]
        return sends, recvs

    start, finish = _task_fns(copies)
    return _Comm([sm], [jax.ShapeDtypeStruct((8,) + sm.shape, sm.dtype)], {}, 7, start, finish)


def _comm_call(name, tasks):
    task = _merge(tasks)
    nci = len(task.ins)

    def body(*refs):
        cins, couts, (ssem, rsem) = refs[:nci], refs[nci:nci + len(task.outs)], refs[nci + len(task.outs):]
        task.start(cins, couts, ssem, rsem, 0)
        task.finish(cins, couts, ssem, rsem, 0)

    return pl.pallas_call(
        body, in_specs=[HBM_SPEC] * nci, out_specs=[HBM_SPEC] * len(task.outs), out_shape=list(task.outs),
        scratch_shapes=[pltpu.SemaphoreType.DMA((task.n_sem,)), pltpu.SemaphoreType.DMA((task.n_sem,))],
        input_output_aliases=task.aliases, name=name)(*task.ins)


def _adamw(w, g, m, v):
    m = ADAM_B1 * m + (1.0 - ADAM_B1) * g
    v = ADAM_B2 * v + (1.0 - ADAM_B2) * (g * g)
    m_hat = m / (1.0 - ADAM_B1 ** ADAM_STEP)
    v_hat = v / (1.0 - ADAM_B2 ** ADAM_STEP)
    delta = -ADAM_LR * (m_hat / (jnp.sqrt(v_hat) + ADAM_EPS) + ADAM_WD * w)
    return delta, m, v


def _adamw_call(name, w, g, m, v):
    R, C = w.shape
    tr = _row_tile(R, C, 1024 * 1024)
    row = pl.BlockSpec((tr, C), lambda i: (i, 0))
    o = jax.ShapeDtypeStruct((R, C), F32)
    return _ew(name, lambda w_, g_, m_, v_: (*_adamw(w_, g_, m_, v_), g_), [(w, row), (g, row), (m, row), (v, row)],
               [(o, row), (o, row), (o, row), (o, row)], (R // tr,))


def _pair_sum(name, g, ra, place):
    _, R, C = g.shape
    r2 = R // 2
    tr = _row_tile(r2, C)
    nb = r2 // tr
    own = pl.BlockSpec((None, tr, C), lambda j, i, p: (j, p[0] * nb + i, 0))
    blk = pl.BlockSpec((None, tr, C), lambda j, i, p: (j, i, 0))
    return _ew("rs_pair_sum_" + name, lambda a, b: (a + b,), [(g, own), (ra, blk)],
               [(jax.ShapeDtypeStruct((4, r2, C), BF16), blk)], (4, nb), sp=place)[0]


def _chip_sum(name, g, ra, rb, place):
    _, R, C = g.shape
    r2 = R // 2
    tr = _row_tile(r2, C)
    nb = r2 // tr
    own = pl.BlockSpec((None, tr, C), lambda i, p: (p[1], p[0] * nb + i, 0))
    mine = pl.BlockSpec((None, tr, C), lambda i, p: (p[1], i, 0))
    src = [pl.BlockSpec((None, tr, C), functools.partial(lambda i, p, j: (j, i, 0), j=j)) for j in range(3)]
    out = pl.BlockSpec((None, tr, C), lambda i, p: (p[0], i, 0))

    def total(a, b, r0, r1, r2_):
        return ((((a + b) + r0.astype(F32)) + r1.astype(F32)) + r2_.astype(F32),)

    return _ew("rs_chip_sum_" + name, total, [(g, own), (ra, mine), (rb, src[0]), (rb, src[1]), (rb, src[2])],
               [(jax.ShapeDtypeStruct((2, r2, C), F32), out)], (nb,), sp=place)[0]


def _pack(arrays):
    rows, offs, pos = [], [], 0
    for a in arrays:
        flat = a.reshape(-1)
        n = -(-flat.shape[0] // (8 * LANES)) * (8 * LANES)
        if n != flat.shape[0]:
            flat = jnp.pad(flat, (0, n - flat.shape[0]))
        rows.append(flat.reshape(-1, LANES))
        offs.append(pos)
        pos += n // LANES
    return jnp.concatenate(rows, axis=0), offs


def _unpack(packed, offs, shapes):
    out = []
    for off, shp in zip(offs, shapes):
        n = 1
        for s in shp:
            n *= s
        out.append(packed[off:off + -(-n // LANES)].reshape(-1)[:n].reshape(shp))
    return out


def _sum8(gathered):
    _, R, C = gathered.shape
    tr = _row_tile(R, C, 256 * 1024)
    specs = [pl.BlockSpec((None, tr, C), functools.partial(lambda i, d: (d, i, 0), d=d)) for d in range(8)]

    def fn(*parts):
        t = parts[0]
        for p in parts[1:]:
            t = t + p
        return (t,)

    return _ew("small_sum", fn, [(gathered, s) for s in specs],
               [(jax.ShapeDtypeStruct((R, C), F32), pl.BlockSpec((tr, C), lambda i: (i, 0)))], (R // tr,))[0]


BIG = ("w_in", "w_out", "w_ffn_gate", "w_ffn_up", "w_ffn_down")
SMALL = ("ln1_w", "conv_w", "conv_b", "gate_a_w", "gate_a_b", "gate_x_w", "gate_x_b", "lru_lambda", "ret_gn_w", "ln2_w",
         "final_norm_w")
WEIGHTS = ("ln1_w", "w_in", "conv_w", "conv_b", "gate_a_w", "gate_a_b", "gate_x_w", "gate_x_b", "lru_lambda", "ret_gn_w",
           "w_out", "ln2_w", "w_ffn_gate", "w_ffn_up", "w_ffn_down", "final_norm_w")


def kernel(x, ln1_w, w_in, conv_w, conv_b, gate_a_w, gate_a_b, gate_x_w, gate_x_b, lru_lambda, ret_gn_w, w_out, ln2_w, w_ffn_gate, w_ffn_up, w_ffn_down, final_norm_w, loss_target, m_ln1_w, m_w_in, m_conv_w, m_conv_b, m_gate_a_w, m_gate_a_b, m_gate_x_w, m_gate_x_b, m_lru_lambda, m_ret_gn_w, m_w_out, m_ln2_w, m_w_ffn_gate, m_w_ffn_up, m_w_ffn_down, m_final_norm_w, v_ln1_w, v_w_in, v_conv_w, v_conv_b, v_gate_a_w, v_gate_a_b, v_gate_x_w, v_gate_x_b, v_lru_lambda, v_ret_gn_w, v_w_out, v_ln2_w, v_w_ffn_gate, v_w_ffn_up, v_w_ffn_down, v_final_norm_w):
    w = dict(ln1_w=ln1_w, w_in=w_in, conv_w=conv_w, conv_b=conv_b, gate_a_w=gate_a_w, gate_a_b=gate_a_b, gate_x_w=gate_x_w,
             gate_x_b=gate_x_b, lru_lambda=lru_lambda, ret_gn_w=ret_gn_w, w_out=w_out, ln2_w=ln2_w, w_ffn_gate=w_ffn_gate,
             w_ffn_up=w_ffn_up, w_ffn_down=w_ffn_down, final_norm_w=final_norm_w)
    m = dict(ln1_w=m_ln1_w, w_in=m_w_in, conv_w=m_conv_w, conv_b=m_conv_b, gate_a_w=m_gate_a_w, gate_a_b=m_gate_a_b,
             gate_x_w=m_gate_x_w, gate_x_b=m_gate_x_b, lru_lambda=m_lru_lambda, ret_gn_w=m_ret_gn_w, w_out=m_w_out,
             ln2_w=m_ln2_w, w_ffn_gate=m_w_ffn_gate, w_ffn_up=m_w_ffn_up, w_ffn_down=m_w_ffn_down,
             final_norm_w=m_final_norm_w)
    v = dict(ln1_w=v_ln1_w, w_in=v_w_in, conv_w=v_conv_w, conv_b=v_conv_b, gate_a_w=v_gate_a_w, gate_a_b=v_gate_a_b,
             gate_x_w=v_gate_x_w, gate_x_b=v_gate_x_b, lru_lambda=v_lru_lambda, ret_gn_w=v_ret_gn_w, w_out=v_w_out,
             ln2_w=v_ln2_w, w_ffn_gate=v_w_ffn_gate, w_ffn_up=v_w_ffn_up, w_ffn_down=v_w_ffn_down,
             final_norm_w=v_final_norm_w)
    xs, tgt = x[0], loss_target[0]
    S, D = xs.shape
    DL, DR = conv_b.shape[1], ret_gn_w.shape[1]
    assert DL == DR and DL % HEAD_DIM == 0 and S % CHUNK == 0
    d_mix = DL + DR
    cx, cy, cc = lax.axis_index("x"), lax.axis_index("y"), lax.axis_index("c")
    chip = 2 * cx + cy
    place = jnp.stack([cc, chip]).astype(jnp.int32)
    grad, delta, new_m, new_v = {}, {}, {}, {}

    def finish_big(n, full):
        shp = w[n].shape
        g2 = full.reshape(shp[1], shp[2])
        w2, m2, v2 = (t[n].reshape(shp[1], shp[2]) for t in (w, m, v))
        d_, m_, v_, g_ = _adamw_call("adamw_" + n, w2, g2, m2, v2)
        grad[n], delta[n], new_m[n], new_v[n] = (t.reshape(shp) for t in (g_, d_, m_, v_))

    def all_sum(gathered, own):
        return _sum8(lax.dynamic_update_slice(gathered, own[None], (4 * cx + 2 * cy + cc, 0, 0)))

    st = {n: _own_slab(n, w[n][0], place) for n in BIG}
    TM, TK = 512, 1024
    (u1,), (w_in_st,) = _rms_fwd("rms1", xs, ln1_w, TM, comm=_gather_ici(st["w_in"]))
    w_in_st, conv_st = _comm_call("gather_w_in", [_gather_d2d(w_in_st), _gather_conv(conv_w[0])])
    conv_st = lax.dynamic_update_slice(conv_st, conv_w, (chip, 0, 0))
    cw_cols = conv_st.shape[2]
    conv_full = jnp.transpose(conv_st, (1, 0, 2)).reshape(CONV_W, 4 * cw_cols)
    n_in, n_ff = w_in_st.shape[2], st["w_ffn_gate"].shape[2]
    tables = _ret_tables(S, DR // HEAD_DIM)
    wab, wxb = gate_a_w[0].astype(BF16), gate_x_w[0].astype(BF16)
    lru_w = (conv_full, conv_b, wab, gate_a_b, wxb, gate_x_b, lru_lambda)

    proj, (w_out_st, wg_st) = _mm_nn_stacked("proj", u1, w_in_st, F32, TM,
                                             comm=_merge([_gather_ici(st["w_out"]), _gather_ici(st["w_ffn_gate"])]))
    (hs, y), (w_out_st, wg_st, wu_st) = _lru_fwd(
        proj, *lru_w, 128, d_mix, comm=_merge([_gather_d2d(w_out_st), _gather_d2d(wg_st), _gather_ici(st["w_ffn_up"])]))
    (y, rprev), (wu_st,) = _ret_fwd(proj, y, tables, ret_gn_w, 256, comm=_gather_d2d(wu_st))
    w_out_f = w_out_st.reshape(d_mix, D)
    (h1, u2), (wd_st,) = _out_proj_rms(y, w_out_f, xs, ln2_w, TM, comm=_gather_ici(st["w_ffn_down"]))
    (dg_fac, du_fac, ff), (wd_st,) = _ffn_gate_up(u2, wg_st, wu_st, TM, comm=_gather_d2d(wd_st))
    wd_f = wd_st.reshape(4 * n_ff, D)
    dh2, dh2b, d_fw, loss = _ffn_down_loss(ff, wd_f, h1, tgt, final_norm_w.reshape(1, D), 256)

    g_wd = _mm_tn("g_w_down", ff, dh2b, n_ff, 1024, TK).reshape(4, n_ff, D)
    (dgt, dup), (ra_wd,) = _ffn_gate_up_bwd(dh2b, wd_f, dg_fac, du_fac, TM, n_ff, comm=_pair_exchange(g_wd))
    pb_wd = _pair_sum("w_ffn_down", g_wd, ra_wd, place)
    g_wg, (rb_wd,) = _mm_tn("g_w_gate", u2, dgt, 1024, None, TK, stacked_cols=n_ff, comm=_chip_exchange(pb_wd))
    slot_wd = _chip_sum("w_ffn_down", g_wd, ra_wd, rb_wd, place)
    g_wu, (full_wd, ra_wg) = _mm_tn("g_w_up", u2, dup, 1024, None, TK, stacked_cols=n_ff,
                                    comm=_merge([_pair_share(slot_wd), _pair_exchange(g_wg)]))
    finish_big("w_ffn_down", full_wd)
    pb_wg = _pair_sum("w_ffn_gate", g_wg, ra_wg, place)
    du2, (rb_wg, ra_wu) = _mm_nt_stacked("d_u2", [dgt, dup], [wg_st, wu_st], F32, 256, 1024,
                                         comm=_merge([_chip_exchange(pb_wg), _pair_exchange(g_wu)]))
    slot_wg = _chip_sum("w_ffn_gate", g_wg, ra_wg, rb_wg, place)
    pb_wu = _pair_sum("w_ffn_up", g_wu, ra_wu, place)
    (dh1, dh1b, dy, d_ln2), (full_wg,) = _rms_bwd_dy(h1, ln2_w, du2, dh2, w_out_f, 256, comm=_pair_share(slot_wg))
    finish_big("w_ffn_gate", full_wg)
    g_wout = _mm_tn("g_w_out", y, dh1b, 1024, 1024, TK).reshape(4, d_mix // 4, D)
    (dproj, d_cw, d_cb, d_wa, d_ba, d_wx, d_bx, d_lam), (rb_wu, ra_wout) = _lru_bwd(
        proj, hs, dy, *lru_w, 128, comm=_merge([_chip_exchange(pb_wu), _pair_exchange(g_wout)]))
    slot_wu = _chip_sum("w_ffn_up", g_wu, ra_wu, rb_wu, place)
    pb_wout = _pair_sum("w_out", g_wout, ra_wout, place)
    (dproj, d_gn), (full_wu, rb_wout) = _ret_bwd(proj, rprev, dy, dproj, tables, ret_gn_w, 256,
                                                 comm=_merge([_pair_share(slot_wu), _chip_exchange(pb_wout)]))
    finish_big("w_ffn_up", full_wu)
    slot_wout = _chip_sum("w_out", g_wout, ra_wout, rb_wout, place)
    small = dict(conv_w=d_cw, conv_b=d_cb, gate_a_w=d_wa, gate_a_b=d_ba, gate_x_w=d_wx, gate_x_b=d_bx, lru_lambda=d_lam,
                 ret_gn_w=d_gn, ln2_w=d_ln2, final_norm_w=d_fw)
    packed, offs = _pack([small[n] for n in SMALL[1:]] + [loss])
    g_win, (full_wout, got_small) = _mm_tn("g_w_in", u1, dproj, 1024, None, TK, stacked_cols=n_in,
                                           comm=_merge([_pair_share(slot_wout), _gather_small(packed)]))
    finish_big("w_out", full_wout)
    (ra_win,) = _comm_call("rs_pair_w_in", [_pair_exchange(g_win)])
    pb_win = _pair_sum("w_in", g_win, ra_win, place)
    du1, (rb_win,) = _mm_nt_stacked("d_u1", [dproj], [w_in_st], F32, 256, D, comm=_chip_exchange(pb_win))
    slot_win = _chip_sum("w_in", g_win, ra_win, rb_win, place)
    gx, d_ln1 = _rms_bwd("rms1_bwd", xs, ln1_w, du1, dh1, TM)
    packed1, _ = _pack([d_ln1])
    full_win, got_ln1 = _comm_call("reduce_tail", [_pair_share(slot_win), _gather_small(packed1)])
    finish_big("w_in", full_win)

    red = _unpack(all_sum(got_small, packed), offs, [small[n].shape for n in SMALL[1:]] + [(1, LANES)])
    g = dict(zip(SMALL[1:], red[:-1]))
    g["ln1_w"] = all_sum(got_ln1, packed1)[:-(-D // LANES)].reshape(1, D)
    loss_out = red[-1][0, 0]
    g["conv_w"] = lax.dynamic_slice(g["conv_w"], (0, chip * cw_cols), (CONV_W, cw_cols))
    packs = [_pack([t[n] for n in SMALL])[0] for t in (w, m, v)]
    gp, offs2 = _pack([g[n] for n in SMALL])
    outs = _adamw_call("adamw_small", packs[0], gp, packs[1], packs[2])
    shapes = [w[n].shape for n in SMALL]
    for dst, arr in zip((delta, new_m, new_v), outs):
        dst.update(zip(SMALL, _unpack(arr, offs2, shapes)))
    for n in SMALL:
        grad[n] = g[n].reshape(w[n].shape)

    return (loss_out, gx.reshape(x.shape), *[grad[n] for n in WEIGHTS], *[delta[n] for n in WEIGHTS],
            *[new_m[n] for n in WEIGHTS], *[new_v[n] for n in WEIGHTS])
```

```python
import functools

import jax
import jax.numpy as jnp
from jax import lax
from jax.experimental import pallas as pl
from jax.experimental.pallas import tpu as pltpu

F32 = jnp.float32
BF16 = jnp.bfloat16
MESH = pl.DeviceIdType.MESH

EPS = 1e-6
LRU_C = 8.0
ROPE_BASE = 10000.0
CHUNK = 128
HEAD_DIM = 128
CONV_W = 4
ADAM_LR = 0.001
ADAM_B1 = 0.9
ADAM_B2 = 0.999
ADAM_EPS = 1e-08
ADAM_WD = 0.01
ADAM_STEP = 10

V7X_VMEM_BYTES = 64 * 1024 * 1024
VMEM_LIMIT = V7X_VMEM_BYTES - 8 * 1024 * 1024
LANES = 128
SUBLANES_16BIT = 16

NN = (((1,), (0,)), ((), ()))
NT = (((1,), (1,)), ((), ()))
TN = (((0,), (0,)), ((), ()))


def _dot(a, b, dims=NN):
    return lax.dot_general(a, b, dims, preferred_element_type=F32)


def _tile(n, pref, mult=SUBLANES_16BIT):
    best = None
    t = mult
    while t <= min(n, pref):
        if n % t == 0:
            best = t
        t += mult
    return best if best is not None else n


def _row_tile(rows, cols, budget_bytes=2 * 1024 * 1024):
    return _tile(rows, max(SUBLANES_16BIT, budget_bytes // (cols * 4)))


def _params(sem):
    return pltpu.CompilerParams(dimension_semantics=sem, vmem_limit_bytes=VMEM_LIMIT)


HBM_SPEC = pl.BlockSpec(memory_space=pl.ANY)


class _Comm:
    def __init__(self, ins, outs, aliases, n_sem, start, finish):
        self.ins, self.outs, self.aliases, self.n_sem, self.start, self.finish = ins, outs, aliases, n_sem, start, finish


def _merge(tasks):
    ins, outs, aliases, plans, n_sem = [], [], {}, [], 0
    for t in tasks:
        i0, o0 = len(ins), len(outs)
        plans.append((t, i0, o0, n_sem))
        ins += t.ins
        outs += t.outs
        aliases.update({i0 + a: o0 + b for a, b in t.aliases.items()})
        n_sem += t.n_sem

    def run(which):
        def go(cins, couts, ssem, rsem, base):
            for t, i0, o0, s0 in plans:
                getattr(t, which)(cins[i0:i0 + len(t.ins)], couts[o0:o0 + len(t.outs)], ssem, rsem, base + s0)
        return go

    return _Comm(ins, outs, aliases, n_sem, run("start"), run("finish"))


def _pcall(body, *, name, grid, in_specs, out_specs, out_shape, operands, scratch_shapes=(), aliases=None, comm=None):
    n_in, n_out, n_scr = len(operands), len(out_shape), len(scratch_shapes)
    aliases = dict(aliases or {})
    params = _params(("arbitrary",) * len(grid))
    if comm is None:
        return pl.pallas_call(body, grid=grid, in_specs=list(in_specs), out_specs=list(out_specs), out_shape=list(out_shape),
                              scratch_shapes=list(scratch_shapes), input_output_aliases=aliases, name=name,
                              compiler_params=params)(*operands)
    nci, nco = len(comm.ins), len(comm.outs)

    def wrapped(*refs):
        ins, cins = refs[:n_in], refs[n_in:n_in + nci]
        o0 = n_in + nci
        outs, couts = refs[o0:o0 + n_out], refs[o0 + n_out:o0 + n_out + nco]
        s0 = o0 + n_out + nco
        scr, (ssem, rsem) = refs[s0:s0 + n_scr], refs[s0 + n_scr:]
        ids = [pl.program_id(a) for a in range(len(grid))]
        first = functools.reduce(jnp.logical_and, [i == 0 for i in ids])
        last = functools.reduce(jnp.logical_and, [i == g - 1 for i, g in zip(ids, grid)])

        @pl.when(first)
        def _():
            comm.start(cins, couts, ssem, rsem, 0)

        body(*ins, *outs, *scr)

        @pl.when(last)
        def _():
            comm.finish(cins, couts, ssem, rsem, 0)

    aliases.update({n_in + a: n_out + b for a, b in comm.aliases.items()})
    res = pl.pallas_call(
        wrapped, grid=grid, in_specs=list(in_specs) + [HBM_SPEC] * nci, out_specs=list(out_specs) + [HBM_SPEC] * nco,
        out_shape=list(out_shape) + list(comm.outs),
        scratch_shapes=list(scratch_shapes) + [pltpu.SemaphoreType.DMA((comm.n_sem,)), pltpu.SemaphoreType.DMA((comm.n_sem,))],
        input_output_aliases=aliases, name=name, compiler_params=params)(*operands, *comm.ins)
    return res[:n_out], res[n_out:]


def _ew(name, fn, ins, outs, grid, sp=None):
    n_in = len(ins)

    def body(*refs):
        if sp is not None:
            refs = refs[1:]
        vals = [r[...] for r in refs[:n_in]]
        res = fn(*vals)
        for o_ref, v in zip(refs[n_in:], res):
            o_ref[...] = v.astype(o_ref.dtype)

    in_specs = [s for _, s in ins]
    out_specs = [s for _, s in outs]
    out_shape = [s for s, _ in outs]
    sem = ("arbitrary",) * len(grid)
    if sp is None:
        return pl.pallas_call(body, grid=grid, in_specs=in_specs, out_specs=out_specs, out_shape=out_shape,
                              name=name, compiler_params=_params(sem))(*[a for a, _ in ins])
    gs = pltpu.PrefetchScalarGridSpec(num_scalar_prefetch=1, grid=grid, in_specs=in_specs, out_specs=out_specs)
    return pl.pallas_call(body, grid_spec=gs, out_shape=out_shape, name=name,
                          compiler_params=_params(sem))(sp, *[a for a, _ in ins])


def _matmul(name, pairs, dims, grid, out_shape, out_spec, acc_shape, res=None, comm=None):
    n = len(pairs)
    nk = grid[2]

    def body(*refs):
        ab = refs[:2 * n]
        pos = 2 * n
        res_ref = None
        if res is not None:
            res_ref = refs[pos]
            pos += 1
        o_ref = refs[pos]
        acc_ref = refs[pos + 1] if nk > 1 else None

        def partial():
            t = None
            for p in range(n):
                d = _dot(ab[2 * p][...], ab[2 * p + 1][...], dims)
                t = d if t is None else t + d
            return t

        def finish(t):
            if res_ref is not None:
                t = t + res_ref[...]
            o_ref[...] = t.astype(o_ref.dtype)

        if nk == 1:
            finish(partial())
        else:
            k = pl.program_id(2)

            @pl.when(k == 0)
            def _():
                acc_ref[...] = partial()

            @pl.when(k > 0)
            def _():
                acc_ref[...] += partial()

            @pl.when(k == nk - 1)
            def _():
                finish(acc_ref[...])

    operands, in_specs = [], []
    for a, a_spec, b, b_spec in pairs:
        operands += [a, b]
        in_specs += [a_spec, b_spec]
    if res is not None:
        operands.append(res[0])
        in_specs.append(res[1])
    scratch = [pltpu.VMEM(acc_shape, F32)] if nk > 1 else []
    res = _pcall(body, name=name, grid=grid, in_specs=in_specs, out_specs=[out_spec], out_shape=[out_shape],
                 operands=operands, scratch_shapes=scratch, comm=comm)
    return res[0] if comm is None else (res[0][0], res[1])


def _mm_nn_stacked(name, a, b_st, out_dtype, tm, comm=None):
    M, K = a.shape
    J, _, Nj = b_st.shape
    tm = _tile(M, tm)
    return _matmul(
        name, [(a, pl.BlockSpec((tm, K), lambda j, i, k: (i, 0)), b_st, pl.BlockSpec((None, K, Nj), lambda j, i, k: (j, 0, 0)))],
        NN, (J, M // tm, 1), jax.ShapeDtypeStruct((M, J * Nj), out_dtype), pl.BlockSpec((tm, Nj), lambda j, i, k: (i, j)), None,
        comm=comm)


def _mm_nt_stacked(name, a, b_st, tm, res=None, comm=None):
    M = a.shape[0]
    J, N, Nj = b_st.shape
    tm = _tile(M, tm)

    def body(a_ref, b_ref, *rest):
        o_ref = rest[-1]
        t = None if res is None else rest[0][...]
        for s in range(J):
            d = _dot(a_ref[:, s * Nj:(s + 1) * Nj], b_ref[s], NT)
            t = d if t is None else t + d
        o_ref[...] = t

    row = pl.BlockSpec((tm, N), lambda i: (i, 0))
    out = _pcall(body, name=name, grid=(M // tm,),
                 in_specs=[pl.BlockSpec((tm, J * Nj), lambda i: (i, 0)),
                           pl.BlockSpec((J, N, Nj), lambda i: (0, 0, 0), pipeline_mode=pl.Buffered(1))] + [row] * (res is not None),
                 out_specs=[row], out_shape=[jax.ShapeDtypeStruct((M, N), F32)],
                 operands=[a, b_st] + [res] * (res is not None), comm=comm)
    return out[0] if comm is None else (out[0][0], out[1])


MXU_COLUMNS = 256


def _col_blocks(n):
    return [slice(s, min(s + MXU_COLUMNS, n)) for s in range(0, n, MXU_COLUMNS)]


def _ffn_gate_up(u2, wg_st, wu_st, tm, comm=None):
    S, D = u2.shape
    J, _, Nj = wg_st.shape
    tm = _tile(S, tm)

    def body(a_ref, wg_ref, wu_ref, dg_ref, du_ref, ff_ref):
        a = a_ref[...]
        blocks = _col_blocks(Nj)
        ahead = (_dot(a, wg_ref[:, blocks[0]]), _dot(a, wu_ref[:, blocks[0]]))
        for j, cols in enumerate(blocks):
            g, u = ahead
            if j + 1 < len(blocks):
                ahead = (_dot(a, wg_ref[:, blocks[j + 1]]), _dot(a, wu_ref[:, blocks[j + 1]]))
            sg = jax.nn.sigmoid(g)
            silu = g * sg
            dg_ref[:, cols] = (u * (sg * (1.0 + g * (1.0 - sg)))).astype(BF16)
            du_ref[:, cols] = silu.astype(BF16)
            ff_ref[:, cols] = (silu * u).astype(BF16)

    w_spec = pl.BlockSpec((None, D, Nj), lambda j, i: (j, 0, 0))
    o_spec = pl.BlockSpec((tm, Nj), lambda j, i: (i, j))
    o = jax.ShapeDtypeStruct((S, J * Nj), BF16)
    return _pcall(body, name="ffn_gate_up", grid=(J, S // tm),
                  in_specs=[pl.BlockSpec((tm, D), lambda j, i: (i, 0)), w_spec, w_spec],
                  out_specs=[o_spec, o_spec, o_spec], out_shape=[o, o, o], operands=[u2, wg_st, wu_st], comm=comm)


def _ffn_gate_up_bwd(dh2b, wd, dg_fac, du_fac, tm, tn, comm=None):
    S, D = dh2b.shape
    F = wd.shape[0]
    tm, tn = _tile(S, tm), _tile(F, tn, LANES)

    def body(a_ref, wd_ref, dg_ref, du_ref, dgt_ref, dup_ref):
        a = a_ref[...]
        for cols in _col_blocks(tn):
            d = _dot(a, wd_ref[cols, :], NT)
            dgt_ref[:, cols] = (d * dg_ref[:, cols].astype(F32)).astype(BF16)
            dup_ref[:, cols] = (d * du_ref[:, cols].astype(F32)).astype(BF16)

    blk = pl.BlockSpec((tm, tn), lambda j, i: (i, j))
    o = jax.ShapeDtypeStruct((S, F), BF16)
    return _pcall(body, name="ffn_gate_up_bwd", grid=(F // tn, S // tm),
                  in_specs=[pl.BlockSpec((tm, D), lambda j, i: (i, 0)), pl.BlockSpec((tn, D), lambda j, i: (j, 0)), blk, blk],
                  out_specs=[blk, blk], out_shape=[o, o], operands=[dh2b, wd, dg_fac, du_fac], comm=comm)


def _mm_tn(name, a, b, tmo, tn, tk, stacked_cols=None, comm=None):
    S, Mo = a.shape
    N = b.shape[1]
    tmo, tk = _tile(Mo, tmo, LANES), _tile(S, tk)
    if stacked_cols is None:
        tn = _tile(N, tn, LANES)
        out_shape = jax.ShapeDtypeStruct((Mo, N), F32)
        out_spec = pl.BlockSpec((tmo, tn), lambda i, j, k: (i, j))
    else:
        tn = stacked_cols
        out_shape = jax.ShapeDtypeStruct((N // tn, Mo, tn), F32)
        out_spec = pl.BlockSpec((None, tmo, tn), lambda i, j, k: (j, i, 0))
    return _matmul(
        name, [(a, pl.BlockSpec((tk, tmo), lambda i, j, k: (k, i)), b, pl.BlockSpec((tk, tn), lambda i, j, k: (k, j)))],
        TN, (Mo // tmo, N // tn, S // tk), out_shape, out_spec, (tmo, tn), comm=comm)


def _rms_fwd(name, x, w, tm, comm=None):
    S, D = x.shape
    tm = _tile(S, tm)

    def body(x_ref, w_ref, o_ref):
        xv = x_ref[...]
        r = lax.rsqrt(jnp.mean(xv * xv, axis=-1, keepdims=True) + EPS)
        o_ref[...] = ((xv * r) * w_ref[...]).astype(BF16)

    row = pl.BlockSpec((tm, D), lambda i: (i, 0))
    return _pcall(body, name=name, grid=(S // tm,), in_specs=[row, pl.BlockSpec((1, D), lambda i: (0, 0))], out_specs=[row],
                  out_shape=[jax.ShapeDtypeStruct((S, D), BF16)], operands=[x, w], comm=comm)


def _rms_bwd(name, x, w, dy, dres, tm, comm=None):
    S, D = x.shape
    tm = _tile(S, tm)

    def body(x_ref, w_ref, dy_ref, dres_ref, dx_ref, dw_ref):
        i = pl.program_id(0)

        @pl.when(i == 0)
        def _():
            dw_ref[...] = jnp.zeros_like(dw_ref)

        xv = x_ref[...]
        r = lax.rsqrt(jnp.mean(xv * xv, axis=-1, keepdims=True) + EPS)
        nv = xv * r
        dyv = dy_ref[...]
        dn = dyv * w_ref[...]
        dw_ref[...] += jnp.sum(dyv * nv, axis=0, keepdims=True)
        dx = dres_ref[...] + r * (dn - nv * jnp.mean(dn * nv, axis=-1, keepdims=True))
        dx_ref[...] = dx

    row = pl.BlockSpec((tm, D), lambda i: (i, 0))
    vec = pl.BlockSpec((1, D), lambda i: (0, 0))
    return _pcall(body, name=name, grid=(S // tm,), in_specs=[row, vec, row, row], out_specs=[row, vec],
                  out_shape=[jax.ShapeDtypeStruct((S, D), F32), jax.ShapeDtypeStruct((1, D), F32)],
                  operands=[x, w, dy, dres], comm=comm)


def _rms_bwd_dy(h1, w, du2, dh2, w_out, tm, comm=None):
    S, D = h1.shape
    d_mix = w_out.shape[0]
    tm = _tile(S, tm)

    def body(x_ref, w_ref, dy_ref, dres_ref, wo_ref, dx_ref, dxb_ref, out_ref, dw_ref):
        i = pl.program_id(0)

        @pl.when(i == 0)
        def _():
            dw_ref[...] = jnp.zeros_like(dw_ref)

        xv = x_ref[...]
        r = lax.rsqrt(jnp.mean(xv * xv, axis=-1, keepdims=True) + EPS)
        nv = xv * r
        dyv = dy_ref[...]
        dn = dyv * w_ref[...]
        dw_ref[...] += jnp.sum(dyv * nv, axis=0, keepdims=True)
        dx = dres_ref[...] + r * (dn - nv * jnp.mean(dn * nv, axis=-1, keepdims=True))
        dx_ref[...] = dx
        dxb = dx.astype(BF16)
        dxb_ref[...] = dxb
        out_ref[...] = _dot(dxb, wo_ref[...], NT)

    row = pl.BlockSpec((tm, D), lambda i: (i, 0))
    vec = pl.BlockSpec((1, D), lambda i: (0, 0))
    return _pcall(
        body, name="rms2_bwd_dy", grid=(S // tm,),
        in_specs=[row, vec, row, row, pl.BlockSpec((d_mix, D), lambda i: (0, 0), pipeline_mode=pl.Buffered(1))],
        out_specs=[row, row, pl.BlockSpec((tm, d_mix), lambda i: (i, 0)), vec],
        out_shape=[jax.ShapeDtypeStruct((S, D), F32), jax.ShapeDtypeStruct((S, D), BF16),
                   jax.ShapeDtypeStruct((S, d_mix), F32), jax.ShapeDtypeStruct((1, D), F32)],
        operands=[h1, w, du2, dh2, w_out], comm=comm)


def _out_proj_rms(y, w_out, x, ln_w, tm, comm=None):
    S, K = y.shape
    D = w_out.shape[1]
    tm = _tile(S, tm)

    def body(a_ref, w_ref, x_ref, lw_ref, h_ref, u_ref):
        hv = _dot(a_ref[...], w_ref[...]) + x_ref[...]
        h_ref[...] = hv
        r = lax.rsqrt(jnp.mean(hv * hv, axis=-1, keepdims=True) + EPS)
        u_ref[...] = ((hv * r) * lw_ref[...]).astype(BF16)

    row = pl.BlockSpec((tm, D), lambda i: (i, 0))
    return _pcall(
        body, name="out_proj", grid=(S // tm,),
        in_specs=[pl.BlockSpec((tm, K), lambda i: (i, 0)),
                  pl.BlockSpec((K, D), lambda i: (0, 0), pipeline_mode=pl.Buffered(1)), row,
                  pl.BlockSpec((1, D), lambda i: (0, 0))],
        out_specs=[row, row], out_shape=[jax.ShapeDtypeStruct((S, D), F32), jax.ShapeDtypeStruct((S, D), BF16)],
        operands=[y, w_out, x, ln_w], comm=comm)


def _ffn_down_loss(ff, wd, h1, tgt, fw, tm):
    S, K = ff.shape
    D = wd.shape[1]
    tm = _tile(S, tm)

    def body(a_ref, wd_ref, h1_ref, t_ref, w_ref, dh_ref, dhb_ref, dw_ref, loss_ref):
        i = pl.program_id(0)

        @pl.when(i == 0)
        def _():
            dw_ref[...] = jnp.zeros_like(dw_ref)
            loss_ref[...] = jnp.zeros_like(loss_ref)

        hv = _dot(a_ref[...], wd_ref[...]) + h1_ref[...]
        wv = w_ref[...]
        r = lax.rsqrt(jnp.mean(hv * hv, axis=-1, keepdims=True) + EPS)
        nv = hv * r
        err = nv * wv - t_ref[...]
        row_loss = jnp.mean(err * err, axis=-1, keepdims=True)
        loss_ref[...] += 0.5 * jnp.sum(row_loss, axis=0, keepdims=True)
        dyo = err * (1.0 / D)
        dn = dyo * wv
        dw_ref[...] += jnp.sum(dyo * nv, axis=0, keepdims=True)
        dh = r * (dn - nv * jnp.mean(dn * nv, axis=-1, keepdims=True))
        dh_ref[...] = dh
        dhb_ref[...] = dh.astype(BF16)

    row = pl.BlockSpec((tm, D), lambda i: (i, 0))
    vec = pl.BlockSpec((1, D), lambda i: (0, 0))
    return _pcall(
        body, name="ffn_down_loss", grid=(S // tm,),
        in_specs=[pl.BlockSpec((tm, K), lambda i: (i, 0)),
                  pl.BlockSpec((K, D), lambda i: (0, 0), pipeline_mode=pl.Buffered(1)), row, row, vec],
        out_specs=[row, row, vec, pl.BlockSpec((1, LANES), lambda i: (0, 0))],
        out_shape=[jax.ShapeDtypeStruct((S, D), F32), jax.ShapeDtypeStruct((S, D), BF16),
                   jax.ShapeDtypeStruct((1, D), F32), jax.ShapeDtypeStruct((1, LANES), F32)],
        operands=[ff, wd, h1, tgt, fw])


def _shift_down(x, d, head8):
    r = pltpu.roll(x, d, 0)
    rh = pltpu.roll(head8, d, 0)
    row8 = lax.broadcasted_iota(jnp.int32, head8.shape, 0)
    top = jnp.where(row8 < d, rh, r[0:8])
    return jnp.concatenate([top, r[8:]], axis=0)


def _shift_up(x, d, tail8):
    n = x.shape[0]
    r = pltpu.roll(x, n - d, 0)
    rt = pltpu.roll(tail8, 8 - d, 0)
    row8 = lax.broadcasted_iota(jnp.int32, tail8.shape, 0)
    bot = jnp.where(row8 + d >= 8, rt, r[n - 8:n])
    return jnp.concatenate([r[:n - 8], bot], axis=0)


def _log_sigmoid(lam):
    z = jnp.exp(-jnp.abs(lam))
    u = 1.0 + z
    log1p = jnp.where(u == 1.0, z, jnp.log(u) * (z / jnp.where(u == 1.0, 1.0, u - 1.0)))
    return jnp.minimum(lam, 0.0) - log1p


def _neg_expm1(z, exp_z):
    series = -z * (1.0 + z * (0.5 + z * (1.0 / 6.0)))
    return jnp.where(z > -0.02, series, 1.0 - exp_z)


_GELU_C = 0.7978845608028654


def _gelu(x):
    t = jnp.tanh(_GELU_C * (x + 0.044715 * (x * x * x)))
    return x * (0.5 * (1.0 + t)), t


def _gelu_grad(x, t):
    return 0.5 * (1.0 + t) + 0.5 * x * (1.0 - t * t) * (_GELU_C * (1.0 + 3.0 * 0.044715 * (x * x)))


def _lru_gates(lx, head8, cw, cb, wa_ref, ba, wx_ref, bx, ls):
    nb = wa_ref.shape[0]
    sh = [lx] + [_shift_down(lx, d, head8) for d in (1, 2, 3)]
    cx = cb + sh[3] * cw[0:1]
    cx = cx + sh[2] * cw[1:2]
    cx = cx + sh[1] * cw[2:3]
    cx = cx + sh[0] * cw[3:4]
    cxb = cx.astype(BF16)
    ra = jnp.concatenate([_dot(cxb[:, n * HEAD_DIM:(n + 1) * HEAD_DIM], wa_ref[n]) for n in range(nb)], axis=1) + ba
    ia = jnp.concatenate([_dot(cxb[:, n * HEAD_DIM:(n + 1) * HEAD_DIM], wx_ref[n]) for n in range(nb)], axis=1) + bx
    r = jax.nn.sigmoid(ra)
    ig = jax.nn.sigmoid(ia)
    log_a = LRU_C * r * ls
    a = jnp.exp(log_a)
    m2 = _neg_expm1(2.0 * log_a, a * a)
    return sh, cx, cxb, r, ig, a, m2, jnp.sqrt(m2)


def _lru_specs(tl, DL):
    nb = DL // HEAD_DIM
    vec = pl.BlockSpec((1, DL), lambda i: (0, 0))
    return [pl.BlockSpec((CONV_W, DL), lambda i: (0, 0)), vec,
            pl.BlockSpec((nb, HEAD_DIM, HEAD_DIM), lambda i: (0, 0, 0)), vec,
            pl.BlockSpec((nb, HEAD_DIM, HEAD_DIM), lambda i: (0, 0, 0)), vec, vec]


def _lru_fwd(proj, cw, cb, wa, ba, wx, bx, lam, tl, d_mix, comm=None):
    S = proj.shape[0]
    DL = cb.shape[1]
    tl = _tile(S, tl)

    def body(lx_ref, lg_ref, cw_ref, cb_ref, wa_ref, ba_ref, wx_ref, bx_ref, lam_ref, h_ref, y_ref, prev8, hc, a_s, b_s):
        i = pl.program_id(0)

        @pl.when(i == 0)
        def _():
            prev8[...] = jnp.zeros_like(prev8)
            hc[...] = jnp.zeros_like(hc)

        lx = lx_ref[...]
        ls = _log_sigmoid(lam_ref[...])
        _, cx, _, _, ig, a, _, mult = _lru_gates(lx, prev8[...], cw_ref[...], cb_ref[...], wa_ref, ba_ref[...],
                                                 wx_ref, bx_ref[...], ls)
        b = mult * (ig * cx)
        row = lax.broadcasted_iota(jnp.int32, a.shape, 0) & 7
        for d in (1, 2, 4):
            a_sh = pltpu.roll(a, d, 0)
            b_sh = pltpu.roll(b, d, 0)
            m = row >= d
            b = jnp.where(m, a * b_sh + b, b)
            a = jnp.where(m, a * a_sh, a)
        a_s[...] = a
        b_s[...] = b

        def step(g, hprev):
            sl = pl.ds(pl.multiple_of(g * 8, 8), 8)
            hh = a_s[sl, :] * hprev + b_s[sl, :]
            h_ref[sl, :] = hh
            return hh[7:8, :]

        hc[0:1, :] = lax.fori_loop(0, tl // 8, step, hc[0:1, :])
        prev8[...] = lx[tl - 8:tl]
        g, _ = _gelu(lg_ref[...])
        y_ref[...] = (h_ref[...] * g).astype(BF16)

    return _pcall(
        body, name="lru_fwd", grid=(S // tl,),
        in_specs=[pl.BlockSpec((tl, DL), lambda i: (i, 0)), pl.BlockSpec((tl, DL), lambda i: (i, 1))] + _lru_specs(tl, DL),
        out_specs=[pl.BlockSpec((tl, DL), lambda i: (i, 0)), pl.BlockSpec((tl, DL), lambda i: (i, 0))],
        out_shape=[jax.ShapeDtypeStruct((S, DL), F32), jax.ShapeDtypeStruct((S, d_mix), BF16)],
        scratch_shapes=[pltpu.VMEM((8, DL), F32), pltpu.VMEM((8, DL), F32), pltpu.VMEM((tl, DL), F32), pltpu.VMEM((tl, DL), F32)],
        operands=[proj, proj, cw, cb, wa, ba, wx, bx, lam], comm=comm)


def _lru_bwd(proj, h, dy, cw, cb, wa, ba, wx, bx, lam, tl, comm=None):
    S = proj.shape[0]
    DL = cb.shape[1]
    nb = DL // HEAD_DIM
    tl = _tile(S, tl)
    nt = S // tl
    ng = tl // 8
    t8 = tl // 8

    def body(lx_ref, lxp_ref, lg_ref, h_ref, hp_ref, dy_ref, cw_ref, cb_ref, wa_ref, ba_ref, wx_ref, bx_ref, lam_ref,
             dlxg_ref, dcw_ref, dcb_ref, dwa_ref, dba_ref, dwx_ref, dbx_ref, dlam_ref,
             a_next, g_carry, dcx_next, an_s, dh_s, g_s):
        i = pl.program_id(0)

        @pl.when(i == 0)
        def _():
            for ref in (dcw_ref, dcb_ref, dwa_ref, dba_ref, dwx_ref, dbx_ref, dlam_ref, a_next, g_carry, dcx_next):
                ref[...] = jnp.zeros_like(ref)

        first = i == nt - 1
        lx = lx_ref[...]
        hv = h_ref[...]
        lg = lg_ref[...]
        dyv = dy_ref[...]
        head8 = jnp.where(first, 0.0, lxp_ref[...])
        hhead8 = jnp.where(first, 0.0, hp_ref[...])
        lamv = lam_ref[...]
        ls = _log_sigmoid(lamv)
        cwv = cw_ref[...]
        sh, cx, cxb, r, ig, a, m2, mult = _lru_gates(lx, head8, cwv, cb_ref[...], wa_ref, ba_ref[...], wx_ref, bx_ref[...],
                                                     ls)
        hprev = _shift_down(hv, 1, hhead8)
        g, t = _gelu(lg)
        dlg = dyv * hv * _gelu_grad(lg, t)
        dh = dyv * g
        an = _shift_up(a, 1, a_next[...])
        row = lax.broadcasted_iota(jnp.int32, a.shape, 0) & 7
        for d in (1, 2, 4):
            an_sh = pltpu.roll(an, tl - d, 0)
            dh_sh = pltpu.roll(dh, tl - d, 0)
            m = row + d < 8
            dh = jnp.where(m, an * dh_sh + dh, dh)
            an = jnp.where(m, an * an_sh, an)
        an_s[...] = an
        dh_s[...] = dh

        def step(k, gc):
            sl = pl.ds(pl.multiple_of((ng - 1 - k) * 8, 8), 8)
            gg = an_s[sl, :] * gc + dh_s[sl, :]
            g_s[sl, :] = gg
            return gg[0:1, :]

        g_carry[0:1, :] = lax.fori_loop(0, ng, step, g_carry[0:1, :])
        a_next[...] = a[0:8]
        G = g_s[...]
        da = G * hprev
        icx = ig * cx
        dmult = G * icx
        dicx = G * mult
        di = dicx * cx
        dcx = dicx * ig
        dlog = da * a - dmult * ((a * a) * lax.rsqrt(m2))
        dr = dlog * (LRU_C * ls)
        dlam_ref[...] += jnp.sum(dlog * (LRU_C * r), axis=0, keepdims=True)
        dra = dr * r * (1.0 - r)
        dia = di * ig * (1.0 - ig)
        dba_ref[...] += jnp.sum(dra, axis=0, keepdims=True)
        dbx_ref[...] += jnp.sum(dia, axis=0, keepdims=True)
        drab = dra.astype(BF16)
        diab = dia.astype(BF16)
        back = []
        for n in range(nb):
            cs = slice(n * HEAD_DIM, (n + 1) * HEAD_DIM)
            dwa_ref[n] += _dot(cxb[:, cs], drab[:, cs], TN)
            dwx_ref[n] += _dot(cxb[:, cs], diab[:, cs], TN)
            back.append(_dot(drab[:, cs], wa_ref[n], NT) + _dot(diab[:, cs], wx_ref[n], NT))
        dcx = dcx + jnp.concatenate(back, axis=1)
        dcb_ref[...] += jnp.sum(dcx, axis=0, keepdims=True)
        for tap in range(CONV_W):
            dcw_ref[tap:tap + 1, :] += jnp.sum(dcx * sh[CONV_W - 1 - tap], axis=0, keepdims=True)
        tail = dcx_next[...]
        dlx = dcx * cwv[3:4]
        for d in (1, 2, 3):
            dlx = dlx + _shift_up(dcx, d, tail) * cwv[3 - d:4 - d]
        dcx_next[...] = dcx[0:8]
        dlxg_ref[:, 0:DL] = dlx.astype(BF16)
        dlxg_ref[:, DL:2 * DL] = dlg.astype(BF16)

        @pl.when(i == nt - 1)
        def _():
            dlam_ref[...] = dlam_ref[...] * (1.0 - jax.nn.sigmoid(lamv))

    rev = lambda i: nt - 1 - i
    prev8_map = lambda i: (jnp.maximum((nt - 1 - i) * t8 - 1, 0), 0)
    vec = pl.BlockSpec((1, DL), lambda i: (0, 0))
    mat = pl.BlockSpec((nb, HEAD_DIM, HEAD_DIM), lambda i: (0, 0, 0))
    return _pcall(
        body, name="lru_bwd", grid=(nt,), operands=[proj, proj, proj, h, h, dy, cw, cb, wa, ba, wx, bx, lam], comm=comm,
        in_specs=[pl.BlockSpec((tl, DL), lambda i: (rev(i), 0)), pl.BlockSpec((8, DL), prev8_map),
                  pl.BlockSpec((tl, DL), lambda i: (rev(i), 1)),
                  pl.BlockSpec((tl, DL), lambda i: (rev(i), 0)), pl.BlockSpec((8, DL), prev8_map),
                  pl.BlockSpec((tl, DL), lambda i: (rev(i), 0))] + _lru_specs(tl, DL),
        out_specs=[pl.BlockSpec((tl, 2 * DL), lambda i: (rev(i), 0)), pl.BlockSpec((CONV_W, DL), lambda i: (0, 0)), vec,
                   mat, vec, mat, vec, vec],
        out_shape=[jax.ShapeDtypeStruct(proj.shape, BF16), jax.ShapeDtypeStruct((CONV_W, DL), F32),
                   jax.ShapeDtypeStruct((1, DL), F32), jax.ShapeDtypeStruct((nb, HEAD_DIM, HEAD_DIM), F32),
                   jax.ShapeDtypeStruct((1, DL), F32), jax.ShapeDtypeStruct((nb, HEAD_DIM, HEAD_DIM), F32),
                   jax.ShapeDtypeStruct((1, DL), F32), jax.ShapeDtypeStruct((1, DL), F32)],
        scratch_shapes=[pltpu.VMEM((8, DL), F32), pltpu.VMEM((8, DL), F32), pltpu.VMEM((8, DL), F32),
                        pltpu.VMEM((tl, DL), F32), pltpu.VMEM((tl, DL), F32), pltpu.VMEM((tl, DL), F32)])


def _ret_tables(S, H):
    pos = jnp.arange(S, dtype=F32)
    inv_freq = ROPE_BASE ** (-jnp.arange(0, HEAD_DIM, 2, dtype=F32) / HEAD_DIM)
    ang = pos[:, None] * inv_freq[None, :]
    cos, sin = jnp.cos(ang), jnp.sin(ang)
    cosf = jnp.concatenate([cos, cos], axis=1)
    sins = jnp.concatenate([-sin, sin], axis=1)
    log_gamma = jnp.log1p(-jnp.exp2(-5.0 - jnp.arange(H, dtype=F32)))
    idx = jnp.arange(CHUNK)
    diff = idx[:, None] - idx[None, :]
    causal = diff >= 0
    decay = jnp.where(causal[None], jnp.exp(log_gamma[:, None, None] * jnp.where(causal, diff, 0)[None].astype(F32)), 0.0)
    zeta = jnp.exp(log_gamma[:, None] * (CHUNK - 1 - idx).astype(F32)[None, :])
    xi = jnp.exp(log_gamma[:, None] * (idx + 1).astype(F32)[None, :])
    gc = jnp.exp(log_gamma * CHUNK)
    lanes = (H, CHUNK, HEAD_DIM)
    return (cosf, sins, decay, jnp.broadcast_to(zeta[:, :, None], lanes), jnp.broadcast_to(xi[:, :, None], lanes),
            jnp.broadcast_to(gc[:, None, None], lanes))


def _rope(t, cos, sin_signed):
    return t * cos + pltpu.roll(t, HEAD_DIM // 2, 1) * sin_signed


def _rope_t(d, cos, sin_signed):
    return d * cos + pltpu.roll(d * sin_signed, HEAD_DIM // 2, 1)


def _ret_const_specs(H, DR):
    full = pl.BlockSpec((H, CHUNK, HEAD_DIM), lambda *_: (0, 0, 0))
    return [full, full, full, full, pl.BlockSpec((1, DR), lambda *_: (0, 0))]


def _ret_fwd(proj, y, tables, gnw, tb, comm=None):
    S = proj.shape[0]
    DR = gnw.shape[1]
    H = DR // HEAD_DIM
    tb = _tile(S, tb, CHUNK)
    nc = tb // CHUNK
    cosf, sins, dm, zeta, xi, gc = tables
    scale = HEAD_DIM ** -0.5

    def body(qk_ref, vg_ref, cos_ref, sin_ref, dm_ref, zeta_ref, xi_ref, gc_ref, gnw_ref, y_in, y_ref, rprev_ref, r_s):
        del y_in
        i = pl.program_id(0)

        @pl.when(i == 0)
        def _():
            r_s[...] = jnp.zeros_like(r_s)

        def chunk(c, carry):
            rows = pl.ds(pl.multiple_of(c * CHUNK, CHUNK), CHUNK)
            cos = cos_ref[rows, :]
            sin = sin_ref[rows, :]
            heads = range(H)
            c0 = [slice(h * HEAD_DIM, (h + 1) * HEAD_DIM) for h in heads]
            c1 = [slice(DR + h * HEAD_DIM, DR + (h + 1) * HEAD_DIM) for h in heads]
            qh = [_rope(qk_ref[rows, c0[h]], cos, sin) for h in heads]
            kh = [_rope(qk_ref[rows, c1[h]], cos, sin) * scale for h in heads]
            vb = [vg_ref[rows, c0[h]].astype(BF16) for h in heads]
            rp = [r_s[h] for h in heads]
            rpb = [rp[h].astype(BF16) for h in heads]
            s = [_dot(qh[h].astype(BF16), kh[h].astype(BF16), NT) for h in heads]
            kv = [_dot((kh[h] * zeta_ref[h]).astype(BF16), vb[h], TN) for h in heads]
            cross = [_dot((qh[h] * xi_ref[h]).astype(BF16), rpb[h]) for h in heads]
            o = [_dot((s[h] * dm_ref[h]).astype(BF16), vb[h]) + cross[h] for h in heads]
            for h in heads:
                rprev_ref[c, h] = rpb[h]
                r_s[h] = rp[h] * gc_ref[h] + kv[h]
                mu = jnp.mean(o[h], axis=-1, keepdims=True)
                oc = o[h] - mu
                var = jnp.mean(oc * oc, axis=-1, keepdims=True)
                on = oc * lax.rsqrt(var + EPS) * gnw_ref[:, c0[h]]
                gate = vg_ref[rows, c1[h]]
                y_ref[rows, c0[h]] = (gate * jax.nn.sigmoid(gate) * on).astype(BF16)
            return carry

        lax.fori_loop(0, nc, chunk, 0)

    return _pcall(
        body, name="ret_fwd", grid=(S // tb,),
        in_specs=[pl.BlockSpec((tb, 2 * DR), lambda i: (i, 1)), pl.BlockSpec((tb, 2 * DR), lambda i: (i, 2)),
                  pl.BlockSpec((tb, HEAD_DIM), lambda i: (i, 0)), pl.BlockSpec((tb, HEAD_DIM), lambda i: (i, 0))]
        + _ret_const_specs(H, DR) + [HBM_SPEC],
        out_specs=[pl.BlockSpec((tb, DR), lambda i: (i, 1)),
                   pl.BlockSpec((nc, H, CHUNK, HEAD_DIM), lambda i: (i, 0, 0, 0))],
        out_shape=[jax.ShapeDtypeStruct(y.shape, BF16), jax.ShapeDtypeStruct((S // CHUNK, H, CHUNK, HEAD_DIM), BF16)],
        scratch_shapes=[pltpu.VMEM((H, CHUNK, HEAD_DIM), F32)], aliases={9: 0},
        operands=[proj, proj, cosf, sins, dm, zeta, xi, gc, gnw, y], comm=comm)


def _ret_bwd(proj, rprev, dy, dproj, tables, gnw, tb, comm=None):
    S = proj.shape[0]
    DR = gnw.shape[1]
    H = DR // HEAD_DIM
    tb = _tile(S, tb, CHUNK)
    nc = tb // CHUNK
    nt = S // tb
    cosf, sins, dm, zeta, xi, gc = tables
    scale = HEAD_DIM ** -0.5

    def body(qk_ref, vg_ref, cos_ref, sin_ref, dm_ref, zeta_ref, xi_ref, gc_ref, gnw_ref, rprev_ref, dy_ref, dp_in,
             dp_ref, dgn_ref, dr_s, dqk_s, dvg_s, out_sems):
        del dp_in
        i = pl.program_id(0)
        slot = i % 2

        def out_copies(step, sl):
            rows = pl.ds(pl.multiple_of((nt - 1 - step) * tb, tb), tb)
            return (pltpu.make_async_copy(dqk_s.at[sl], dp_ref.at[rows, pl.ds(2 * DR, 2 * DR)], out_sems.at[sl, 0]),
                    pltpu.make_async_copy(dvg_s.at[sl], dp_ref.at[rows, pl.ds(4 * DR, 2 * DR)], out_sems.at[sl, 1]))

        @pl.when(i == 0)
        def _():
            dr_s[...] = jnp.zeros_like(dr_s)
            dgn_ref[...] = jnp.zeros_like(dgn_ref)

        @pl.when(i >= 2)
        def _():
            for cp in out_copies(i - 2, slot):
                cp.wait()

        def chunk(cc, carry):
            c = nc - 1 - cc
            rows = pl.ds(pl.multiple_of(c * CHUNK, CHUNK), CHUNK)
            cos = cos_ref[rows, :]
            sin = sin_ref[rows, :]
            heads = range(H)
            c0 = [slice(h * HEAD_DIM, (h + 1) * HEAD_DIM) for h in heads]
            c1 = [slice(DR + h * HEAD_DIM, DR + (h + 1) * HEAD_DIM) for h in heads]
            qh = [_rope(qk_ref[rows, c0[h]], cos, sin) for h in heads]
            kh = [_rope(qk_ref[rows, c1[h]], cos, sin) * scale for h in heads]
            qb = [t.astype(BF16) for t in qh]
            kb = [t.astype(BF16) for t in kh]
            vb = [vg_ref[rows, c0[h]].astype(BF16) for h in heads]
            rpb = [rprev_ref[c, h] for h in heads]
            qx = [(qh[h] * xi_ref[h]).astype(BF16) for h in heads]
            kz = [(kh[h] * zeta_ref[h]).astype(BF16) for h in heads]
            drh = [dr_s[h] for h in heads]
            drb = [t.astype(BF16) for t in drh]
            s = [_dot(qb[h], kb[h], NT) for h in heads]
            cross = [_dot(qx[h], rpb[h]) for h in heads]
            dv_state = [_dot(kz[h], drb[h]) for h in heads]
            dk_state = [_dot(vb[h], drb[h], NT) for h in heads]
            sb = [(s[h] * dm_ref[h]).astype(BF16) for h in heads]
            o = [_dot(sb[h], vb[h]) + cross[h] for h in heads]
            dob = []
            for h in heads:
                mu = jnp.mean(o[h], axis=-1, keepdims=True)
                oc = o[h] - mu
                rstd = lax.rsqrt(jnp.mean(oc * oc, axis=-1, keepdims=True) + EPS)
                ohat = oc * rstd
                gw = gnw_ref[:, c0[h]]
                gate = vg_ref[rows, c1[h]]
                sg = jax.nn.sigmoid(gate)
                dyv = dy_ref[rows, c0[h]]
                dvg_s[slot, rows, c1[h]] = (dyv * (ohat * gw) * (sg * (1.0 + gate * (1.0 - sg)))).astype(BF16)
                don = dyv * (gate * sg)
                dgn_ref[:, c0[h]] += jnp.sum(don * ohat, axis=0, keepdims=True)
                dohat = don * gw
                do = rstd * (dohat - jnp.mean(dohat, axis=-1, keepdims=True)
                             - ohat * jnp.mean(dohat * ohat, axis=-1, keepdims=True))
                dob.append(do.astype(BF16))
            ds = [_dot(dob[h], vb[h], NT) for h in heads]
            dq_state = [_dot(dob[h], rpb[h], NT) for h in heads]
            dv = [_dot(sb[h], dob[h], TN) + dv_state[h] for h in heads]
            dr_new = [_dot(qx[h], dob[h], TN) for h in heads]
            dsb = [(ds[h] * dm_ref[h]).astype(BF16) for h in heads]
            dqh = [_dot(dsb[h], kb[h]) + dq_state[h] * xi_ref[h] for h in heads]
            dkh = [_dot(dsb[h], qb[h], TN) + dk_state[h] * zeta_ref[h] for h in heads]
            for h in heads:
                dr_s[h] = drh[h] * gc_ref[h] + dr_new[h]
                dqk_s[slot, rows, c0[h]] = _rope_t(dqh[h], cos, sin).astype(BF16)
                dqk_s[slot, rows, c1[h]] = _rope_t(dkh[h] * scale, cos, sin).astype(BF16)
                dvg_s[slot, rows, c0[h]] = dv[h].astype(BF16)
            return carry

        lax.fori_loop(0, nc, chunk, 0)
        for cp in out_copies(i, slot):
            cp.start()

        @pl.when(i == nt - 1)
        def _():
            if nt >= 2:
                for cp in out_copies(i - 1, 1 - slot):
                    cp.wait()
            for cp in out_copies(i, slot):
                cp.wait()

    rev = lambda i: nt - 1 - i
    return _pcall(
        body, name="ret_bwd", grid=(nt,), aliases={11: 0}, comm=comm,
        operands=[proj, proj, cosf, sins, dm, zeta, xi, gc, gnw, rprev, dy, dproj],
        in_specs=[pl.BlockSpec((tb, 2 * DR), lambda i: (rev(i), 1)), pl.BlockSpec((tb, 2 * DR), lambda i: (rev(i), 2)),
                  pl.BlockSpec((tb, HEAD_DIM), lambda i: (rev(i), 0)), pl.BlockSpec((tb, HEAD_DIM), lambda i: (rev(i), 0))]
        + _ret_const_specs(H, DR)
        + [pl.BlockSpec((nc, H, CHUNK, HEAD_DIM), lambda i: (rev(i), 0, 0, 0)),
           pl.BlockSpec((tb, DR), lambda i: (rev(i), 1)), HBM_SPEC],
        out_specs=[HBM_SPEC, pl.BlockSpec((1, DR), lambda i: (0, 0))],
        out_shape=[jax.ShapeDtypeStruct(dproj.shape, BF16), jax.ShapeDtypeStruct((1, DR), F32)],
        scratch_shapes=[pltpu.VMEM((H, CHUNK, HEAD_DIM), F32), pltpu.VMEM((2, tb, 2 * DR), BF16),
                        pltpu.VMEM((2, tb, 2 * DR), BF16), pltpu.SemaphoreType.DMA((2, 2))])


def _place():
    x, y, c = lax.axis_index("x"), lax.axis_index("y"), lax.axis_index("c")
    chips = [(1 - x, y), (x, 1 - y), (1 - x, 1 - y)]
    return x, y, c, chips


def _own_slab(name, shard, place):
    R, C = shard.shape
    tr = _row_tile(R, C)
    return _ew("cast_" + name, lambda a: (a,), [(shard, pl.BlockSpec((tr, C), lambda i, p: (i, 0)))],
               [(jax.ShapeDtypeStruct((4, R, C), BF16), pl.BlockSpec((None, tr, C), lambda i, p: (p[1], i, 0)))],
               (R // tr,), sp=place)[0]


class _remote:
    def __init__(self, src, dst, ssem, rsem, k, to):
        self.args = dict(src_ref=src, dst_ref=dst, send_sem=ssem.at[k], recv_sem=rsem.at[k], device_id=to,
                         device_id_type=MESH)

    def start(self):
        pltpu.make_async_remote_copy(**self.args).start()

    def wait_send(self):
        pltpu.make_async_remote_copy(**self.args).wait_send()

    def wait_recv(self):
        pltpu.make_async_remote_copy(**self.args).wait_recv()


def _task_fns(copies):
    def start(cins, couts, ssem, rsem, base):
        for cp in copies(cins, couts, ssem, rsem, base)[0]:
            cp.start()

    def finish(cins, couts, ssem, rsem, base):
        sends, recvs = copies(cins, couts, ssem, rsem, base)
        for cp in sends:
            cp.wait_send()
        for cp in recvs:
            cp.wait_recv()

    return start, finish


NEIGHBOURS, DIAGONAL = (0, 1), (2,)


def _gather_ici(st, which=NEIGHBOURS + DIAGONAL):
    r2 = st.shape[1] // 2

    def copies(cins, couts, ssem, rsem, base):
        x, y, c, chips = _place()
        out = couts[0]
        mine = out.at[2 * x + y, pl.ds(c * r2, r2), :]
        sends, recvs = [], []
        for k, j in enumerate(which):
            cx, cy = chips[j]
            got = out.at[2 * cx + cy, pl.ds(c * r2, r2), :]
            sends.append(_remote(mine, mine, ssem, rsem, base + k, (cx, cy, c)))
            recvs.append(_remote(got, got, ssem, rsem, base + k, (x, y, c)))
        return sends, recvs

    start, finish = _task_fns(copies)
    return _Comm([st], [jax.ShapeDtypeStruct(st.shape, st.dtype)], {0: 0}, len(which), start, finish)


def _gather_d2d(st):
    r2 = st.shape[1] // 2

    def copies(cins, couts, ssem, rsem, base):
        x, y, c, chips = _place()
        out = couts[0]
        sends, recvs = [], []
        for j, (cx, cy) in enumerate(chips):
            have = out.at[2 * cx + cy, pl.ds(c * r2, r2), :]
            want = out.at[2 * cx + cy, pl.ds((1 - c) * r2, r2), :]
            sends.append(_remote(have, have, ssem, rsem, base + j, (x, y, 1 - c)))
            recvs.append(_remote(want, want, ssem, rsem, base + j, (x, y, c)))
        return sends, recvs

    start, finish = _task_fns(copies)
    return _Comm([st], [jax.ShapeDtypeStruct(st.shape, st.dtype)], {0: 0}, 3, start, finish)


def _gather_conv(conv_w):
    def copies(cins, couts, ssem, rsem, base):
        x, y, c, chips = _place()
        src, out = cins[0], couts[0]
        sends = [_remote(src, out.at[2 * x + y], ssem, rsem, base + j, (*chip, c)) for j, chip in enumerate(chips)]
        recvs = [_remote(src, out.at[2 * cx + cy], ssem, rsem, base + j, (x, y, c)) for j, (cx, cy) in enumerate(chips)]
        return sends, recvs

    start, finish = _task_fns(copies)
    return _Comm([conv_w], [jax.ShapeDtypeStruct((4,) + conv_w.shape, conv_w.dtype)], {}, 3, start, finish)


def _pair_exchange(g):
    r2 = g.shape[1] // 2

    def copies(cins, couts, ssem, rsem, base):
        x, y, c, _ = _place()
        cp = _remote(cins[0].at[:, pl.ds((1 - c) * r2, r2), :], couts[0], ssem, rsem, base, (x, y, 1 - c))
        return [cp], [cp]

    start, finish = _task_fns(copies)
    return _Comm([g], [jax.ShapeDtypeStruct((g.shape[0], r2, g.shape[2]), g.dtype)], {}, 1, start, finish)


def _chip_exchange(part):
    def copies(cins, couts, ssem, rsem, base):
        x, y, c, chips = _place()
        cps = [_remote(cins[0].at[2 * cx + cy], couts[0].at[j], ssem, rsem, base + j, (cx, cy, c))
               for j, (cx, cy) in enumerate(chips)]
        return cps, cps

    start, finish = _task_fns(copies)
    return _Comm([part], [jax.ShapeDtypeStruct((3,) + part.shape[1:], part.dtype)], {}, 3, start, finish)


def _pair_share(slot):
    def copies(cins, couts, ssem, rsem, base):
        x, y, c, _ = _place()
        out = couts[0]
        return ([_remote(out.at[c], out.at[c], ssem, rsem, base, (x, y, 1 - c))],
                [_remote(out.at[1 - c], out.at[1 - c], ssem, rsem, base, (x, y, c))])

    start, finish = _task_fns(copies)
    return _Comm([slot], [jax.ShapeDtypeStruct(slot.shape, slot.dtype)], {0: 0}, 1, start, finish)


def _gather_small(sm):
    flips = [(fx, fy, fc) for fx in (0, 1) for fy in (0, 1) for fc in (0, 1)][1:]

    def copies(cins, couts, ssem, rsem, base):
        x, y, c, _ = _place()
        src, out = cins[0], couts[0]
        peers = [(1 - x if fx else x, 1 - y if fy else y, 1 - c if fc else c) for fx, fy, fc in flips]
        sends = [_remote(src, out.at[4 * x + 2 * y + c], ssem, rsem, base + k, peer) for k, peer in enumerate(peers)]
        recvs = [_remote(src, out.at[4 * px + 2 * py + pc], ssem, rsem, base + k, (x, y, c))
                 for k, (px, py, pc) in enumerate(peers)]
        return sends, recvs

    start, finish = _task_fns(copies)
    return _Comm([sm], [jax.ShapeDtypeStruct((8,) + sm.shape, sm.dtype)], {}, 7, start, finish)


def _comm_call(name, tasks):
    task = _merge(tasks)
    nci = len(task.ins)

    def body(*refs):
        cins, couts, (ssem, rsem) = refs[:nci], refs[nci:nci + len(task.outs)], refs[nci + len(task.outs):]
        task.start(cins, couts, ssem, rsem, 0)
        task.finish(cins, couts, ssem, rsem, 0)

    return pl.pallas_call(
        body, in_specs=[HBM_SPEC] * nci, out_specs=[HBM_SPEC] * len(task.outs), out_shape=list(task.outs),
        scratch_shapes=[pltpu.SemaphoreType.DMA((task.n_sem,)), pltpu.SemaphoreType.DMA((task.n_sem,))],
        input_output_aliases=task.aliases, name=name)(*task.ins)


def _adamw(w, g, m, v):
    m = ADAM_B1 * m + (1.0 - ADAM_B1) * g
    v = ADAM_B2 * v + (1.0 - ADAM_B2) * (g * g)
    m_hat = m / (1.0 - ADAM_B1 ** ADAM_STEP)
    v_hat = v / (1.0 - ADAM_B2 ** ADAM_STEP)
    delta = -ADAM_LR * (m_hat / (jnp.sqrt(v_hat) + ADAM_EPS) + ADAM_WD * w)
    return delta, m, v


def _adamw_call(name, w, g, m, v):
    R, C = w.shape
    tr = _row_tile(R, C, 1024 * 1024)
    row = pl.BlockSpec((tr, C), lambda i: (i, 0))
    o = jax.ShapeDtypeStruct((R, C), F32)
    return _ew(name, lambda w_, g_, m_, v_: (*_adamw(w_, g_, m_, v_), g_), [(w, row), (g, row), (m, row), (v, row)],
               [(o, row), (o, row), (o, row), (o, row)], (R // tr,))


def _pair_sum(name, g, ra, place):
    _, R, C = g.shape
    r2 = R // 2
    tr = _row_tile(r2, C)
    nb = r2 // tr
    own = pl.BlockSpec((None, tr, C), lambda j, i, p: (j, p[0] * nb + i, 0))
    blk = pl.BlockSpec((None, tr, C), lambda j, i, p: (j, i, 0))
    return _ew("rs_pair_sum_" + name, lambda a, b: (a + b,), [(g, own), (ra, blk)],
               [(jax.ShapeDtypeStruct((4, r2, C), BF16), blk)], (4, nb), sp=place)[0]


def _chip_sum(name, g, ra, rb, place):
    _, R, C = g.shape
    r2 = R // 2
    tr = _row_tile(r2, C)
    nb = r2 // tr
    own = pl.BlockSpec((None, tr, C), lambda i, p: (p[1], p[0] * nb + i, 0))
    mine = pl.BlockSpec((None, tr, C), lambda i, p: (p[1], i, 0))
    src = [pl.BlockSpec((None, tr, C), functools.partial(lambda i, p, j: (j, i, 0), j=j)) for j in range(3)]
    out = pl.BlockSpec((None, tr, C), lambda i, p: (p[0], i, 0))

    def total(a, b, r0, r1, r2_):
        return ((((a + b) + r0.astype(F32)) + r1.astype(F32)) + r2_.astype(F32),)

    return _ew("rs_chip_sum_" + name, total, [(g, own), (ra, mine), (rb, src[0]), (rb, src[1]), (rb, src[2])],
               [(jax.ShapeDtypeStruct((2, r2, C), F32), out)], (nb,), sp=place)[0]


def _pack(arrays):
    rows, offs, pos = [], [], 0
    for a in arrays:
        flat = a.reshape(-1)
        n = -(-flat.shape[0] // (8 * LANES)) * (8 * LANES)
        if n != flat.shape[0]:
            flat = jnp.pad(flat, (0, n - flat.shape[0]))
        rows.append(flat.reshape(-1, LANES))
        offs.append(pos)
        pos += n // LANES
    return jnp.concatenate(rows, axis=0), offs


def _unpack(packed, offs, shapes):
    out = []
    for off, shp in zip(offs, shapes):
        n = 1
        for s in shp:
            n *= s
        out.append(packed[off:off + -(-n // LANES)].reshape(-1)[:n].reshape(shp))
    return out


def _sum8(gathered):
    _, R, C = gathered.shape
    tr = _row_tile(R, C, 256 * 1024)
    specs = [pl.BlockSpec((None, tr, C), functools.partial(lambda i, d: (d, i, 0), d=d)) for d in range(8)]

    def fn(*parts):
        t = parts[0]
        for p in parts[1:]:
            t = t + p
        return (t,)

    return _ew("small_sum", fn, [(gathered, s) for s in specs],
               [(jax.ShapeDtypeStruct((R, C), F32), pl.BlockSpec((tr, C), lambda i: (i, 0)))], (R // tr,))[0]


BIG = ("w_in", "w_out", "w_ffn_gate", "w_ffn_up", "w_ffn_down")
SMALL = ("ln1_w", "conv_w", "conv_b", "gate_a_w", "gate_a_b", "gate_x_w", "gate_x_b", "lru_lambda", "ret_gn_w", "ln2_w",
         "final_norm_w")
WEIGHTS = ("ln1_w", "w_in", "conv_w", "conv_b", "gate_a_w", "gate_a_b", "gate_x_w", "gate_x_b", "lru_lambda", "ret_gn_w",
           "w_out", "ln2_w", "w_ffn_gate", "w_ffn_up", "w_ffn_down", "final_norm_w")


def kernel(x, ln1_w, w_in, conv_w, conv_b, gate_a_w, gate_a_b, gate_x_w, gate_x_b, lru_lambda, ret_gn_w, w_out, ln2_w, w_ffn_gate, w_ffn_up, w_ffn_down, final_norm_w, loss_target, m_ln1_w, m_w_in, m_conv_w, m_conv_b, m_gate_a_w, m_gate_a_b, m_gate_x_w, m_gate_x_b, m_lru_lambda, m_ret_gn_w, m_w_out, m_ln2_w, m_w_ffn_gate, m_w_ffn_up, m_w_ffn_down, m_final_norm_w, v_ln1_w, v_w_in, v_conv_w, v_conv_b, v_gate_a_w, v_gate_a_b, v_gate_x_w, v_gate_x_b, v_lru_lambda, v_ret_gn_w, v_w_out, v_ln2_w, v_w_ffn_gate, v_w_ffn_up, v_w_ffn_down, v_final_norm_w):
    w = dict(ln1_w=ln1_w, w_in=w_in, conv_w=conv_w, conv_b=conv_b, gate_a_w=gate_a_w, gate_a_b=gate_a_b, gate_x_w=gate_x_w,
             gate_x_b=gate_x_b, lru_lambda=lru_lambda, ret_gn_w=ret_gn_w, w_out=w_out, ln2_w=ln2_w, w_ffn_gate=w_ffn_gate,
             w_ffn_up=w_ffn_up, w_ffn_down=w_ffn_down, final_norm_w=final_norm_w)
    m = dict(ln1_w=m_ln1_w, w_in=m_w_in, conv_w=m_conv_w, conv_b=m_conv_b, gate_a_w=m_gate_a_w, gate_a_b=m_gate_a_b,
             gate_x_w=m_gate_x_w, gate_x_b=m_gate_x_b, lru_lambda=m_lru_lambda, ret_gn_w=m_ret_gn_w, w_out=m_w_out,
             ln2_w=m_ln2_w, w_ffn_gate=m_w_ffn_gate, w_ffn_up=m_w_ffn_up, w_ffn_down=m_w_ffn_down,
             final_norm_w=m_final_norm_w)
    v = dict(ln1_w=v_ln1_w, w_in=v_w_in, conv_w=v_conv_w, conv_b=v_conv_b, gate_a_w=v_gate_a_w, gate_a_b=v_gate_a_b,
             gate_x_w=v_gate_x_w, gate_x_b=v_gate_x_b, lru_lambda=v_lru_lambda, ret_gn_w=v_ret_gn_w, w_out=v_w_out,
             ln2_w=v_ln2_w, w_ffn_gate=v_w_ffn_gate, w_ffn_up=v_w_ffn_up, w_ffn_down=v_w_ffn_down,
             final_norm_w=v_final_norm_w)
    xs, tgt = x[0], loss_target[0]
    S, D = xs.shape
    DL, DR = conv_b.shape[1], ret_gn_w.shape[1]
    assert DL == DR and DL % HEAD_DIM == 0 and S % CHUNK == 0
    d_mix = DL + DR
    cx, cy, cc = lax.axis_index("x"), lax.axis_index("y"), lax.axis_index("c")
    chip = 2 * cx + cy
    place = jnp.stack([cc, chip]).astype(jnp.int32)
    grad, delta, new_m, new_v = {}, {}, {}, {}

    def finish_big(n, full):
        shp = w[n].shape
        g2 = full.reshape(shp[1], shp[2])
        w2, m2, v2 = (t[n].reshape(shp[1], shp[2]) for t in (w, m, v))
        d_, m_, v_, g_ = _adamw_call("adamw_" + n, w2, g2, m2, v2)
        grad[n], delta[n], new_m[n], new_v[n] = (t.reshape(shp) for t in (g_, d_, m_, v_))

    def all_sum(gathered, own):
        return _sum8(lax.dynamic_update_slice(gathered, own[None], (4 * cx + 2 * cy + cc, 0, 0)))

    st = {n: _own_slab(n, w[n][0], place) for n in BIG}
    TM, TK = 512, 2048
    (u1,), (w_in_st,) = _rms_fwd("rms1", xs, ln1_w, TM, comm=_gather_ici(st["w_in"]))
    w_in_st, conv_st = _comm_call("gather_w_in", [_gather_d2d(w_in_st), _gather_conv(conv_w[0])])
    conv_st = lax.dynamic_update_slice(conv_st, conv_w, (chip, 0, 0))
    cw_cols = conv_st.shape[2]
    conv_full = jnp.transpose(conv_st, (1, 0, 2)).reshape(CONV_W, 4 * cw_cols)
    n_in, n_ff = w_in_st.shape[2], st["w_ffn_gate"].shape[2]
    tables = _ret_tables(S, DR // HEAD_DIM)
    wab, wxb = gate_a_w[0].astype(BF16), gate_x_w[0].astype(BF16)
    lru_w = (conv_full, conv_b, wab, gate_a_b, wxb, gate_x_b, lru_lambda)

    proj, (w_out_st, wg_st) = _mm_nn_stacked("proj", u1, w_in_st, F32, TM,
                                             comm=_merge([_gather_ici(st["w_out"]), _gather_ici(st["w_ffn_gate"])]))
    (hs, y), (w_out_st, wg_st, wu_st) = _lru_fwd(
        proj, *lru_w, 128, d_mix, comm=_merge([_gather_d2d(w_out_st), _gather_d2d(wg_st), _gather_ici(st["w_ffn_up"])]))
    (y, rprev), (wu_st, wd_st) = _ret_fwd(proj, y, tables, ret_gn_w, 256,
                                          comm=_merge([_gather_d2d(wu_st), _gather_ici(st["w_ffn_down"], NEIGHBOURS)]))
    w_out_f = w_out_st.reshape(d_mix, D)
    (h1, u2), (wd_st,) = _out_proj_rms(y, w_out_f, xs, ln2_w, TM, comm=_gather_ici(wd_st, DIAGONAL))
    (dg_fac, du_fac, ff), (wd_st,) = _ffn_gate_up(u2, wg_st, wu_st, TM, comm=_gather_d2d(wd_st))
    wd_f = wd_st.reshape(4 * n_ff, D)
    dh2, dh2b, d_fw, loss = _ffn_down_loss(ff, wd_f, h1, tgt, final_norm_w.reshape(1, D), 256)

    g_wd = _mm_tn("g_w_down", ff, dh2b, n_ff, 1024, TK).reshape(4, n_ff, D)
    (dgt, dup), (ra_wd,) = _ffn_gate_up_bwd(dh2b, wd_f, dg_fac, du_fac, TM, n_ff, comm=_pair_exchange(g_wd))
    pb_wd = _pair_sum("w_ffn_down", g_wd, ra_wd, place)
    g_wg, (rb_wd,) = _mm_tn("g_w_gate", u2, dgt, 1024, None, TK, stacked_cols=n_ff, comm=_chip_exchange(pb_wd))
    slot_wd = _chip_sum("w_ffn_down", g_wd, ra_wd, rb_wd, place)
    g_wu, (full_wd, ra_wg) = _mm_tn("g_w_up", u2, dup, 1024, None, TK, stacked_cols=n_ff,
                                    comm=_merge([_pair_share(slot_wd), _pair_exchange(g_wg)]))
    finish_big("w_ffn_down", full_wd)
    pb_wg = _pair_sum("w_ffn_gate", g_wg, ra_wg, place)
    du2, (rb_wg,) = _mm_nt_stacked("d_u2_gate", dgt, wg_st, 256, comm=_chip_exchange(pb_wg))
    du2, (ra_wu,) = _mm_nt_stacked("d_u2_up", dup, wu_st, 256, res=du2, comm=_pair_exchange(g_wu))
    slot_wg = _chip_sum("w_ffn_gate", g_wg, ra_wg, rb_wg, place)
    pb_wu = _pair_sum("w_ffn_up", g_wu, ra_wu, place)
    (dh1, dh1b, dy, d_ln2), (full_wg,) = _rms_bwd_dy(h1, ln2_w, du2, dh2, w_out_f, 256, comm=_pair_share(slot_wg))
    finish_big("w_ffn_gate", full_wg)
    g_wout = _mm_tn("g_w_out", y, dh1b, 1024, 1024, TK).reshape(4, d_mix // 4, D)
    (dproj, d_cw, d_cb, d_wa, d_ba, d_wx, d_bx, d_lam), (rb_wu, ra_wout) = _lru_bwd(
        proj, hs, dy, *lru_w, 128, comm=_merge([_chip_exchange(pb_wu), _pair_exchange(g_wout)]))
    slot_wu = _chip_sum("w_ffn_up", g_wu, ra_wu, rb_wu, place)
    pb_wout = _pair_sum("w_out", g_wout, ra_wout, place)
    (dproj, d_gn), (full_wu, rb_wout) = _ret_bwd(proj, rprev, dy, dproj, tables, ret_gn_w, 256,
                                                 comm=_merge([_pair_share(slot_wu), _chip_exchange(pb_wout)]))
    finish_big("w_ffn_up", full_wu)
    slot_wout = _chip_sum("w_out", g_wout, ra_wout, rb_wout, place)
    small = dict(conv_w=d_cw, conv_b=d_cb, gate_a_w=d_wa, gate_a_b=d_ba, gate_x_w=d_wx, gate_x_b=d_bx, lru_lambda=d_lam,
                 ret_gn_w=d_gn, ln2_w=d_ln2, final_norm_w=d_fw)
    packed, offs = _pack([small[n] for n in SMALL[1:]] + [loss])
    g_win, (full_wout, got_small) = _mm_tn("g_w_in", u1, dproj, 1024, None, TK, stacked_cols=n_in,
                                           comm=_merge([_pair_share(slot_wout), _gather_small(packed)]))
    finish_big("w_out", full_wout)
    (ra_win,) = _comm_call("rs_pair_w_in", [_pair_exchange(g_win)])
    pb_win = _pair_sum("w_in", g_win, ra_win, place)
    du1, (rb_win,) = _mm_nt_stacked("d_u1", dproj, w_in_st, 256, comm=_chip_exchange(pb_win))
    slot_win = _chip_sum("w_in", g_win, ra_win, rb_win, place)
    gx, d_ln1 = _rms_bwd("rms1_bwd", xs, ln1_w, du1, dh1, TM)
    packed1, _ = _pack([d_ln1])
    full_win, got_ln1 = _comm_call("reduce_tail", [_pair_share(slot_win), _gather_small(packed1)])
    finish_big("w_in", full_win)

    red = _unpack(all_sum(got_small, packed), offs, [small[n].shape for n in SMALL[1:]] + [(1, LANES)])
    g = dict(zip(SMALL[1:], red[:-1]))
    g["ln1_w"] = all_sum(got_ln1, packed1)[:-(-D // LANES)].reshape(1, D)
    loss_out = red[-1][0, 0]
    g["conv_w"] = lax.dynamic_slice(g["conv_w"], (0, chip * cw_cols), (CONV_W, cw_cols))
    packs = [_pack([t[n] for n in SMALL])[0] for t in (w, m, v)]
    gp, offs2 = _pack([g[n] for n in SMALL])
    outs = _adamw_call("adamw_small", packs[0], gp, packs[1], packs[2])
    shapes = [w[n].shape for n in SMALL]
    for dst, arr in zip((delta, new_m, new_v), outs):
        dst.update(zip(SMALL, _unpack(arr, offs2, shapes)))
    for n in SMALL:
        grad[n] = g[n].reshape(w[n].shape)

    return (loss_out, gx.reshape(x.shape), *[grad[n] for n in WEIGHTS], *[delta[n] for n in WEIGHTS],
            *[new_m[n] for n in WEIGHTS], *[new_v[n] for n in WEIGHTS])
```

```python
import functools

import jax
import jax.numpy as jnp
from jax import lax
from jax.experimental import pallas as pl
from jax.experimental.pallas import tpu as pltpu

F32 = jnp.float32
BF16 = jnp.bfloat16
MESH = pl.DeviceIdType.MESH

EPS = 1e-6
LRU_C = 8.0
ROPE_BASE = 10000.0
CHUNK = 128
HEAD_DIM = 128
CONV_W = 4
ADAM_LR = 0.001
ADAM_B1 = 0.9
ADAM_B2 = 0.999
ADAM_EPS = 1e-08
ADAM_WD = 0.01
ADAM_STEP = 10

V7X_VMEM_BYTES = 64 * 1024 * 1024
VMEM_LIMIT = V7X_VMEM_BYTES - 8 * 1024 * 1024
LANES = 128
SUBLANES_16BIT = 16

NN = (((1,), (0,)), ((), ()))
NT = (((1,), (1,)), ((), ()))
TN = (((0,), (0,)), ((), ()))


def _dot(a, b, dims=NN):
    return lax.dot_general(a, b, dims, preferred_element_type=F32)


def _tile(n, pref, mult=SUBLANES_16BIT):
    best = None
    t = mult
    while t <= min(n, pref):
        if n % t == 0:
            best = t
        t += mult
    return best if best is not None else n


def _row_tile(rows, cols, budget_bytes=2 * 1024 * 1024):
    return _tile(rows, max(SUBLANES_16BIT, budget_bytes // (cols * 4)))


def _params(sem):
    return pltpu.CompilerParams(dimension_semantics=sem, vmem_limit_bytes=VMEM_LIMIT)


HBM_SPEC = pl.BlockSpec(memory_space=pl.ANY)


class _Comm:
    def __init__(self, ins, outs, aliases, n_sem, start, finish):
        self.ins, self.outs, self.aliases, self.n_sem, self.start, self.finish = ins, outs, aliases, n_sem, start, finish


def _merge(tasks):
    ins, outs, aliases, plans, n_sem = [], [], {}, [], 0
    for t in tasks:
        i0, o0 = len(ins), len(outs)
        plans.append((t, i0, o0, n_sem))
        ins += t.ins
        outs += t.outs
        aliases.update({i0 + a: o0 + b for a, b in t.aliases.items()})
        n_sem += t.n_sem

    def run(which):
        def go(cins, couts, ssem, rsem, base):
            for t, i0, o0, s0 in plans:
                getattr(t, which)(cins[i0:i0 + len(t.ins)], couts[o0:o0 + len(t.outs)], ssem, rsem, base + s0)
        return go

    return _Comm(ins, outs, aliases, n_sem, run("start"), run("finish"))


def _pcall(body, *, name, grid, in_specs, out_specs, out_shape, operands, scratch_shapes=(), aliases=None, comm=None):
    n_in, n_out, n_scr = len(operands), len(out_shape), len(scratch_shapes)
    aliases = dict(aliases or {})
    params = _params(("arbitrary",) * len(grid))
    if comm is None:
        return pl.pallas_call(body, grid=grid, in_specs=list(in_specs), out_specs=list(out_specs), out_shape=list(out_shape),
                              scratch_shapes=list(scratch_shapes), input_output_aliases=aliases, name=name,
                              compiler_params=params)(*operands)
    nci, nco = len(comm.ins), len(comm.outs)

    def wrapped(*refs):
        ins, cins = refs[:n_in], refs[n_in:n_in + nci]
        o0 = n_in + nci
        outs, couts = refs[o0:o0 + n_out], refs[o0 + n_out:o0 + n_out + nco]
        s0 = o0 + n_out + nco
        scr, (ssem, rsem) = refs[s0:s0 + n_scr], refs[s0 + n_scr:]
        ids = [pl.program_id(a) for a in range(len(grid))]
        first = functools.reduce(jnp.logical_and, [i == 0 for i in ids])
        last = functools.reduce(jnp.logical_and, [i == g - 1 for i, g in zip(ids, grid)])

        @pl.when(first)
        def _():
            comm.start(cins, couts, ssem, rsem, 0)

        body(*ins, *outs, *scr)

        @pl.when(last)
        def _():
            comm.finish(cins, couts, ssem, rsem, 0)

    aliases.update({n_in + a: n_out + b for a, b in comm.aliases.items()})
    res = pl.pallas_call(
        wrapped, grid=grid, in_specs=list(in_specs) + [HBM_SPEC] * nci, out_specs=list(out_specs) + [HBM_SPEC] * nco,
        out_shape=list(out_shape) + list(comm.outs),
        scratch_shapes=list(scratch_shapes) + [pltpu.SemaphoreType.DMA((comm.n_sem,)), pltpu.SemaphoreType.DMA((comm.n_sem,))],
        input_output_aliases=aliases, name=name, compiler_params=params)(*operands, *comm.ins)
    return res[:n_out], res[n_out:]


def _ew(name, fn, ins, outs, grid, sp=None):
    n_in = len(ins)

    def body(*refs):
        if sp is not None:
            refs = refs[1:]
        vals = [r[...] for r in refs[:n_in]]
        res = fn(*vals)
        for o_ref, v in zip(refs[n_in:], res):
            o_ref[...] = v.astype(o_ref.dtype)

    in_specs = [s for _, s in ins]
    out_specs = [s for _, s in outs]
    out_shape = [s for s, _ in outs]
    sem = ("arbitrary",) * len(grid)
    if sp is None:
        return pl.pallas_call(body, grid=grid, in_specs=in_specs, out_specs=out_specs, out_shape=out_shape,
                              name=name, compiler_params=_params(sem))(*[a for a, _ in ins])
    gs = pltpu.PrefetchScalarGridSpec(num_scalar_prefetch=1, grid=grid, in_specs=in_specs, out_specs=out_specs)
    return pl.pallas_call(body, grid_spec=gs, out_shape=out_shape, name=name,
                          compiler_params=_params(sem))(sp, *[a for a, _ in ins])


def _matmul(name, pairs, dims, grid, out_shape, out_spec, acc_shape, res=None, comm=None):
    n = len(pairs)
    nk = grid[2]

    def body(*refs):
        ab = refs[:2 * n]
        pos = 2 * n
        res_ref = None
        if res is not None:
            res_ref = refs[pos]
            pos += 1
        o_ref = refs[pos]
        acc_ref = refs[pos + 1] if nk > 1 else None

        def partial():
            t = None
            for p in range(n):
                d = _dot(ab[2 * p][...], ab[2 * p + 1][...], dims)
                t = d if t is None else t + d
            return t

        def finish(t):
            if res_ref is not None:
                t = t + res_ref[...]
            o_ref[...] = t.astype(o_ref.dtype)

        if nk == 1:
            finish(partial())
        else:
            k = pl.program_id(2)

            @pl.when(k == 0)
            def _():
                acc_ref[...] = partial()

            @pl.when(k > 0)
            def _():
                acc_ref[...] += partial()

            @pl.when(k == nk - 1)
            def _():
                finish(acc_ref[...])

    operands, in_specs = [], []
    for a, a_spec, b, b_spec in pairs:
        operands += [a, b]
        in_specs += [a_spec, b_spec]
    if res is not None:
        operands.append(res[0])
        in_specs.append(res[1])
    scratch = [pltpu.VMEM(acc_shape, F32)] if nk > 1 else []
    res = _pcall(body, name=name, grid=grid, in_specs=in_specs, out_specs=[out_spec], out_shape=[out_shape],
                 operands=operands, scratch_shapes=scratch, comm=comm)
    return res[0] if comm is None else (res[0][0], res[1])


def _mm_nn_stacked(name, a, b_st, out_dtype, tm, comm=None):
    M, K = a.shape
    J, _, Nj = b_st.shape
    tm = _tile(M, tm)
    return _matmul(
        name, [(a, pl.BlockSpec((tm, K), lambda j, i, k: (i, 0)), b_st, pl.BlockSpec((None, K, Nj), lambda j, i, k: (j, 0, 0)))],
        NN, (J, M // tm, 1), jax.ShapeDtypeStruct((M, J * Nj), out_dtype), pl.BlockSpec((tm, Nj), lambda j, i, k: (i, j)), None,
        comm=comm)


def _mm_nt_stacked(name, a, b_st, tm, res=None, comm=None):
    M = a.shape[0]
    J, N, Nj = b_st.shape
    tm = _tile(M, tm)

    def body(a_ref, b_ref, *rest):
        o_ref = rest[-1]
        t = None if res is None else rest[0][...]
        for s in range(J):
            d = _dot(a_ref[:, s * Nj:(s + 1) * Nj], b_ref[s], NT)
            t = d if t is None else t + d
        o_ref[...] = t

    row = pl.BlockSpec((tm, N), lambda i: (i, 0))
    out = _pcall(body, name=name, grid=(M // tm,),
                 in_specs=[pl.BlockSpec((tm, J * Nj), lambda i: (i, 0)),
                           pl.BlockSpec((J, N, Nj), lambda i: (0, 0, 0), pipeline_mode=pl.Buffered(1))] + [row] * (res is not None),
                 out_specs=[row], out_shape=[jax.ShapeDtypeStruct((M, N), F32)],
                 operands=[a, b_st] + [res] * (res is not None), comm=comm)
    return out[0] if comm is None else (out[0][0], out[1])


MXU_COLUMNS = 256


def _col_blocks(n):
    return [slice(s, min(s + MXU_COLUMNS, n)) for s in range(0, n, MXU_COLUMNS)]


def _ffn_gate_up(u2, wg_st, wu_st, tm, comm=None):
    S, D = u2.shape
    J, _, Nj = wg_st.shape
    tm = _tile(S, tm)

    def body(a_ref, wg_ref, wu_ref, dg_ref, du_ref, ff_ref):
        a = a_ref[...]
        blocks = _col_blocks(Nj)
        ahead = (_dot(a, wg_ref[:, blocks[0]]), _dot(a, wu_ref[:, blocks[0]]))
        for j, cols in enumerate(blocks):
            g, u = ahead
            if j + 1 < len(blocks):
                ahead = (_dot(a, wg_ref[:, blocks[j + 1]]), _dot(a, wu_ref[:, blocks[j + 1]]))
            sg = jax.nn.sigmoid(g)
            silu = g * sg
            dg_ref[:, cols] = (u * (sg * (1.0 + g * (1.0 - sg)))).astype(BF16)
            du_ref[:, cols] = silu.astype(BF16)
            ff_ref[:, cols] = (silu * u).astype(BF16)

    w_spec = pl.BlockSpec((None, D, Nj), lambda j, i: (j, 0, 0))
    o_spec = pl.BlockSpec((tm, Nj), lambda j, i: (i, j))
    o = jax.ShapeDtypeStruct((S, J * Nj), BF16)
    return _pcall(body, name="ffn_gate_up", grid=(J, S // tm),
                  in_specs=[pl.BlockSpec((tm, D), lambda j, i: (i, 0)), w_spec, w_spec],
                  out_specs=[o_spec, o_spec, o_spec], out_shape=[o, o, o], operands=[u2, wg_st, wu_st], comm=comm)


def _ffn_gate_up_bwd(dh2b, wd, dg_fac, du_fac, tm, tn, comm=None):
    S, D = dh2b.shape
    F = wd.shape[0]
    tm, tn = _tile(S, tm), _tile(F, tn, LANES)

    def body(a_ref, wd_ref, dg_ref, du_ref, dgt_ref, dup_ref):
        a = a_ref[...]
        for cols in _col_blocks(tn):
            d = _dot(a, wd_ref[cols, :], NT)
            dgt_ref[:, cols] = (d * dg_ref[:, cols].astype(F32)).astype(BF16)
            dup_ref[:, cols] = (d * du_ref[:, cols].astype(F32)).astype(BF16)

    blk = pl.BlockSpec((tm, tn), lambda j, i: (i, j))
    o = jax.ShapeDtypeStruct((S, F), BF16)
    return _pcall(body, name="ffn_gate_up_bwd", grid=(F // tn, S // tm),
                  in_specs=[pl.BlockSpec((tm, D), lambda j, i: (i, 0)), pl.BlockSpec((tn, D), lambda j, i: (j, 0)), blk, blk],
                  out_specs=[blk, blk], out_shape=[o, o], operands=[dh2b, wd, dg_fac, du_fac], comm=comm)


def _mm_tn(name, a, b, tmo, tn, tk, stacked_cols=None, comm=None):
    S, Mo = a.shape
    N = b.shape[1]
    tmo, tk = _tile(Mo, tmo, LANES), _tile(S, tk)
    if stacked_cols is None:
        tn = _tile(N, tn, LANES)
        out_shape = jax.ShapeDtypeStruct((Mo, N), F32)
        out_spec = pl.BlockSpec((tmo, tn), lambda i, j, k: (i, j))
    else:
        tn = stacked_cols
        out_shape = jax.ShapeDtypeStruct((N // tn, Mo, tn), F32)
        out_spec = pl.BlockSpec((None, tmo, tn), lambda i, j, k: (j, i, 0))
    return _matmul(
        name, [(a, pl.BlockSpec((tk, tmo), lambda i, j, k: (k, i)), b, pl.BlockSpec((tk, tn), lambda i, j, k: (k, j)))],
        TN, (Mo // tmo, N // tn, S // tk), out_shape, out_spec, (tmo, tn), comm=comm)


def _rms_fwd(name, x, w, tm, comm=None):
    S, D = x.shape
    tm = _tile(S, tm)

    def body(x_ref, w_ref, o_ref):
        xv = x_ref[...]
        r = lax.rsqrt(jnp.mean(xv * xv, axis=-1, keepdims=True) + EPS)
        o_ref[...] = ((xv * r) * w_ref[...]).astype(BF16)

    row = pl.BlockSpec((tm, D), lambda i: (i, 0))
    return _pcall(body, name=name, grid=(S // tm,), in_specs=[row, pl.BlockSpec((1, D), lambda i: (0, 0))], out_specs=[row],
                  out_shape=[jax.ShapeDtypeStruct((S, D), BF16)], operands=[x, w], comm=comm)


def _rms_bwd(name, x, w, dy, dres, tm, comm=None):
    S, D = x.shape
    tm = _tile(S, tm)

    def body(x_ref, w_ref, dy_ref, dres_ref, dx_ref, dw_ref):
        i = pl.program_id(0)

        @pl.when(i == 0)
        def _():
            dw_ref[...] = jnp.zeros_like(dw_ref)

        xv = x_ref[...]
        r = lax.rsqrt(jnp.mean(xv * xv, axis=-1, keepdims=True) + EPS)
        nv = xv * r
        dyv = dy_ref[...]
        dn = dyv * w_ref[...]
        dw_ref[...] += jnp.sum(dyv * nv, axis=0, keepdims=True)
        dx = dres_ref[...] + r * (dn - nv * jnp.mean(dn * nv, axis=-1, keepdims=True))
        dx_ref[...] = dx

    row = pl.BlockSpec((tm, D), lambda i: (i, 0))
    vec = pl.BlockSpec((1, D), lambda i: (0, 0))
    return _pcall(body, name=name, grid=(S // tm,), in_specs=[row, vec, row, row], out_specs=[row, vec],
                  out_shape=[jax.ShapeDtypeStruct((S, D), F32), jax.ShapeDtypeStruct((1, D), F32)],
                  operands=[x, w, dy, dres], comm=comm)


def _rms_bwd_dy(h1, w, du2, dh2, w_out, tm, comm=None):
    S, D = h1.shape
    d_mix = w_out.shape[0]
    tm = _tile(S, tm)

    def body(x_ref, w_ref, dy_ref, dres_ref, wo_ref, dx_ref, dxb_ref, out_ref, dw_ref):
        i = pl.program_id(0)

        @pl.when(i == 0)
        def _():
            dw_ref[...] = jnp.zeros_like(dw_ref)

        xv = x_ref[...]
        r = lax.rsqrt(jnp.mean(xv * xv, axis=-1, keepdims=True) + EPS)
        nv = xv * r
        dyv = dy_ref[...]
        dn = dyv * w_ref[...]
        dw_ref[...] += jnp.sum(dyv * nv, axis=0, keepdims=True)
        dx = dres_ref[...] + r * (dn - nv * jnp.mean(dn * nv, axis=-1, keepdims=True))
        dx_ref[...] = dx
        dxb = dx.astype(BF16)
        dxb_ref[...] = dxb
        out_ref[...] = _dot(dxb, wo_ref[...], NT)

    row = pl.BlockSpec((tm, D), lambda i: (i, 0))
    vec = pl.BlockSpec((1, D), lambda i: (0, 0))
    return _pcall(
        body, name="rms2_bwd_dy", grid=(S // tm,),
        in_specs=[row, vec, row, row, pl.BlockSpec((d_mix, D), lambda i: (0, 0), pipeline_mode=pl.Buffered(1))],
        out_specs=[row, row, pl.BlockSpec((tm, d_mix), lambda i: (i, 0)), vec],
        out_shape=[jax.ShapeDtypeStruct((S, D), F32), jax.ShapeDtypeStruct((S, D), BF16),
                   jax.ShapeDtypeStruct((S, d_mix), F32), jax.ShapeDtypeStruct((1, D), F32)],
        operands=[h1, w, du2, dh2, w_out], comm=comm)


def _out_proj_rms(y, w_out, x, ln_w, tm, comm=None):
    S, K = y.shape
    D = w_out.shape[1]
    tm = _tile(S, tm)

    def body(a_ref, w_ref, x_ref, lw_ref, h_ref, u_ref):
        hv = _dot(a_ref[...], w_ref[...]) + x_ref[...]
        h_ref[...] = hv
        r = lax.rsqrt(jnp.mean(hv * hv, axis=-1, keepdims=True) + EPS)
        u_ref[...] = ((hv * r) * lw_ref[...]).astype(BF16)

    row = pl.BlockSpec((tm, D), lambda i: (i, 0))
    return _pcall(
        body, name="out_proj", grid=(S // tm,),
        in_specs=[pl.BlockSpec((tm, K), lambda i: (i, 0)),
                  pl.BlockSpec((K, D), lambda i: (0, 0), pipeline_mode=pl.Buffered(1)), row,
                  pl.BlockSpec((1, D), lambda i: (0, 0))],
        out_specs=[row, row], out_shape=[jax.ShapeDtypeStruct((S, D), F32), jax.ShapeDtypeStruct((S, D), BF16)],
        operands=[y, w_out, x, ln_w], comm=comm)


def _ffn_down_loss(ff, wd, h1, tgt, fw, tm):
    S, K = ff.shape
    D = wd.shape[1]
    tm = _tile(S, tm)

    def body(a_ref, wd_ref, h1_ref, t_ref, w_ref, dh_ref, dhb_ref, dw_ref, loss_ref):
        i = pl.program_id(0)

        @pl.when(i == 0)
        def _():
            dw_ref[...] = jnp.zeros_like(dw_ref)
            loss_ref[...] = jnp.zeros_like(loss_ref)

        hv = _dot(a_ref[...], wd_ref[...]) + h1_ref[...]
        wv = w_ref[...]
        r = lax.rsqrt(jnp.mean(hv * hv, axis=-1, keepdims=True) + EPS)
        nv = hv * r
        err = nv * wv - t_ref[...]
        row_loss = jnp.mean(err * err, axis=-1, keepdims=True)
        loss_ref[...] += 0.5 * jnp.sum(row_loss, axis=0, keepdims=True)
        dyo = err * (1.0 / D)
        dn = dyo * wv
        dw_ref[...] += jnp.sum(dyo * nv, axis=0, keepdims=True)
        dh = r * (dn - nv * jnp.mean(dn * nv, axis=-1, keepdims=True))
        dh_ref[...] = dh
        dhb_ref[...] = dh.astype(BF16)

    row = pl.BlockSpec((tm, D), lambda i: (i, 0))
    vec = pl.BlockSpec((1, D), lambda i: (0, 0))
    return _pcall(
        body, name="ffn_down_loss", grid=(S // tm,),
        in_specs=[pl.BlockSpec((tm, K), lambda i: (i, 0)),
                  pl.BlockSpec((K, D), lambda i: (0, 0), pipeline_mode=pl.Buffered(1)), row, row, vec],
        out_specs=[row, row, vec, pl.BlockSpec((1, LANES), lambda i: (0, 0))],
        out_shape=[jax.ShapeDtypeStruct((S, D), F32), jax.ShapeDtypeStruct((S, D), BF16),
                   jax.ShapeDtypeStruct((1, D), F32), jax.ShapeDtypeStruct((1, LANES), F32)],
        operands=[ff, wd, h1, tgt, fw])


def _shift_down(x, d, head8):
    r = pltpu.roll(x, d, 0)
    rh = pltpu.roll(head8, d, 0)
    row8 = lax.broadcasted_iota(jnp.int32, head8.shape, 0)
    top = jnp.where(row8 < d, rh, r[0:8])
    return jnp.concatenate([top, r[8:]], axis=0)


def _shift_up(x, d, tail8):
    n = x.shape[0]
    r = pltpu.roll(x, n - d, 0)
    rt = pltpu.roll(tail8, 8 - d, 0)
    row8 = lax.broadcasted_iota(jnp.int32, tail8.shape, 0)
    bot = jnp.where(row8 + d >= 8, rt, r[n - 8:n])
    return jnp.concatenate([r[:n - 8], bot], axis=0)


def _log_sigmoid(lam):
    z = jnp.exp(-jnp.abs(lam))
    u = 1.0 + z
    log1p = jnp.where(u == 1.0, z, jnp.log(u) * (z / jnp.where(u == 1.0, 1.0, u - 1.0)))
    return jnp.minimum(lam, 0.0) - log1p


def _neg_expm1(z, exp_z):
    series = -z * (1.0 + z * (0.5 + z * (1.0 / 6.0)))
    return jnp.where(z > -0.02, series, 1.0 - exp_z)


_GELU_C = 0.7978845608028654


def _gelu(x):
    t = jnp.tanh(_GELU_C * (x + 0.044715 * (x * x * x)))
    return x * (0.5 * (1.0 + t)), t


def _gelu_grad(x, t):
    return 0.5 * (1.0 + t) + 0.5 * x * (1.0 - t * t) * (_GELU_C * (1.0 + 3.0 * 0.044715 * (x * x)))


def _lru_gates(lx, head8, cw, cb, wa_ref, ba, wx_ref, bx, ls):
    nb = wa_ref.shape[0]
    sh = [lx] + [_shift_down(lx, d, head8) for d in (1, 2, 3)]
    cx = cb + sh[3] * cw[0:1]
    cx = cx + sh[2] * cw[1:2]
    cx = cx + sh[1] * cw[2:3]
    cx = cx + sh[0] * cw[3:4]
    cxb = cx.astype(BF16)
    ra = jnp.concatenate([_dot(cxb[:, n * HEAD_DIM:(n + 1) * HEAD_DIM], wa_ref[n]) for n in range(nb)], axis=1) + ba
    ia = jnp.concatenate([_dot(cxb[:, n * HEAD_DIM:(n + 1) * HEAD_DIM], wx_ref[n]) for n in range(nb)], axis=1) + bx
    r = jax.nn.sigmoid(ra)
    ig = jax.nn.sigmoid(ia)
    log_a = LRU_C * r * ls
    a = jnp.exp(log_a)
    m2 = _neg_expm1(2.0 * log_a, a * a)
    return sh, cx, cxb, r, ig, a, m2, jnp.sqrt(m2)


def _lru_specs(tl, DL):
    nb = DL // HEAD_DIM
    vec = pl.BlockSpec((1, DL), lambda i: (0, 0))
    return [pl.BlockSpec((CONV_W, DL), lambda i: (0, 0)), vec,
            pl.BlockSpec((nb, HEAD_DIM, HEAD_DIM), lambda i: (0, 0, 0)), vec,
            pl.BlockSpec((nb, HEAD_DIM, HEAD_DIM), lambda i: (0, 0, 0)), vec, vec]


def _lru_fwd(proj, cw, cb, wa, ba, wx, bx, lam, tl, d_mix, comm=None):
    S = proj.shape[0]
    DL = cb.shape[1]
    tl = _tile(S, tl)

    def body(lx_ref, lg_ref, cw_ref, cb_ref, wa_ref, ba_ref, wx_ref, bx_ref, lam_ref, h_ref, y_ref, prev8, hc, a_s, b_s):
        i = pl.program_id(0)

        @pl.when(i == 0)
        def _():
            prev8[...] = jnp.zeros_like(prev8)
            hc[...] = jnp.zeros_like(hc)

        lx = lx_ref[...]
        ls = _log_sigmoid(lam_ref[...])
        _, cx, _, _, ig, a, _, mult = _lru_gates(lx, prev8[...], cw_ref[...], cb_ref[...], wa_ref, ba_ref[...],
                                                 wx_ref, bx_ref[...], ls)
        b = mult * (ig * cx)
        row = lax.broadcasted_iota(jnp.int32, a.shape, 0) & 7
        for d in (1, 2, 4):
            a_sh = pltpu.roll(a, d, 0)
            b_sh = pltpu.roll(b, d, 0)
            m = row >= d
            b = jnp.where(m, a * b_sh + b, b)
            a = jnp.where(m, a * a_sh, a)
        a_s[...] = a
        b_s[...] = b

        def step(g, hprev):
            sl = pl.ds(pl.multiple_of(g * 8, 8), 8)
            hh = a_s[sl, :] * hprev + b_s[sl, :]
            h_ref[sl, :] = hh
            return hh[7:8, :]

        hc[0:1, :] = lax.fori_loop(0, tl // 8, step, hc[0:1, :])
        prev8[...] = lx[tl - 8:tl]
        g, _ = _gelu(lg_ref[...])
        y_ref[...] = (h_ref[...] * g).astype(BF16)

    return _pcall(
        body, name="lru_fwd", grid=(S // tl,),
        in_specs=[pl.BlockSpec((tl, DL), lambda i: (i, 0)), pl.BlockSpec((tl, DL), lambda i: (i, 1))] + _lru_specs(tl, DL),
        out_specs=[pl.BlockSpec((tl, DL), lambda i: (i, 0)), pl.BlockSpec((tl, DL), lambda i: (i, 0))],
        out_shape=[jax.ShapeDtypeStruct((S, DL), F32), jax.ShapeDtypeStruct((S, d_mix), BF16)],
        scratch_shapes=[pltpu.VMEM((8, DL), F32), pltpu.VMEM((8, DL), F32), pltpu.VMEM((tl, DL), F32), pltpu.VMEM((tl, DL), F32)],
        operands=[proj, proj, cw, cb, wa, ba, wx, bx, lam], comm=comm)


def _side_tiles(nt, S, Mo, J):
    for tmo in (1024, 512, 256, 128):
        if Mo % tmo == 0 and nt % (J * (Mo // tmo)) == 0:
            nk = nt // (J * (Mo // tmo))
            if S % nk == 0 and (S // nk) % SUBLANES_16BIT == 0:
                return tmo, S // nk
    raise ValueError("no tiling of the side matmul fits the grid")


def _lru_bwd(proj, h, dy, cw, cb, wa, ba, wx, bx, lam, tl, side, comm=None):
    S = proj.shape[0]
    DL = cb.shape[1]
    nb = DL // HEAD_DIM
    tl = _tile(S, tl)
    nt = S // tl
    ng = tl // 8
    t8 = tl // 8
    side_a, side_b, Nj = side
    Mo, J = side_a.shape[1], side_b.shape[1] // Nj
    tmo, tk = _side_tiles(nt, S, Mo, J)
    nk, nmo = S // tk, Mo // tmo

    def body(lx_ref, lxp_ref, lg_ref, h_ref, hp_ref, dy_ref, cw_ref, cb_ref, wa_ref, ba_ref, wx_ref, bx_ref, lam_ref,
             sa_ref, sb_ref,
             dlxg_ref, dcw_ref, dcb_ref, dwa_ref, dba_ref, dwx_ref, dbx_ref, dlam_ref, so_ref,
             a_next, g_carry, dcx_next, an_s, dh_s, g_s, acc):
        i = pl.program_id(0)

        @pl.when(i == 0)
        def _():
            for ref in (dcw_ref, dcb_ref, dwa_ref, dba_ref, dwx_ref, dbx_ref, dlam_ref, a_next, g_carry, dcx_next):
                ref[...] = jnp.zeros_like(ref)

        @pl.when(i % nk == 0)
        def _():
            acc[...] = jnp.zeros_like(acc)

        first = i == nt - 1
        lx = lx_ref[...]
        hv = h_ref[...]
        lg = lg_ref[...]
        dyv = dy_ref[...]
        head8 = jnp.where(first, 0.0, lxp_ref[...])
        hhead8 = jnp.where(first, 0.0, hp_ref[...])
        lamv = lam_ref[...]
        ls = _log_sigmoid(lamv)
        cwv = cw_ref[...]
        sh, cx, cxb, r, ig, a, m2, mult = _lru_gates(lx, head8, cwv, cb_ref[...], wa_ref, ba_ref[...], wx_ref, bx_ref[...],
                                                     ls)
        hprev = _shift_down(hv, 1, hhead8)
        g, t = _gelu(lg)
        dlg = dyv * hv * _gelu_grad(lg, t)
        dh = dyv * g
        an = _shift_up(a, 1, a_next[...])
        row = lax.broadcasted_iota(jnp.int32, a.shape, 0) & 7
        for d in (1, 2, 4):
            an_sh = pltpu.roll(an, tl - d, 0)
            dh_sh = pltpu.roll(dh, tl - d, 0)
            m = row + d < 8
            dh = jnp.where(m, an * dh_sh + dh, dh)
            an = jnp.where(m, an * an_sh, an)
        an_s[...] = an
        dh_s[...] = dh

        def step(k, gc):
            sl = pl.ds(pl.multiple_of((ng - 1 - k) * 8, 8), 8)
            gg = an_s[sl, :] * gc + dh_s[sl, :]
            g_s[sl, :] = gg
            return gg[0:1, :]

        g_carry[0:1, :] = lax.fori_loop(0, ng, step, g_carry[0:1, :])
        a_next[...] = a[0:8]
        G = g_s[...]
        da = G * hprev
        icx = ig * cx
        dmult = G * icx
        dicx = G * mult
        di = dicx * cx
        dcx = dicx * ig
        dlog = da * a - dmult * ((a * a) * lax.rsqrt(m2))
        dr = dlog * (LRU_C * ls)
        dlam_ref[...] += jnp.sum(dlog * (LRU_C * r), axis=0, keepdims=True)
        dra = dr * r * (1.0 - r)
        dia = di * ig * (1.0 - ig)
        dba_ref[...] += jnp.sum(dra, axis=0, keepdims=True)
        dbx_ref[...] += jnp.sum(dia, axis=0, keepdims=True)
        drab = dra.astype(BF16)
        diab = dia.astype(BF16)
        back = []
        for n in range(nb):
            cs = slice(n * HEAD_DIM, (n + 1) * HEAD_DIM)
            dwa_ref[n] += _dot(cxb[:, cs], drab[:, cs], TN)
            dwx_ref[n] += _dot(cxb[:, cs], diab[:, cs], TN)
            back.append(_dot(drab[:, cs], wa_ref[n], NT) + _dot(diab[:, cs], wx_ref[n], NT))
        dcx = dcx + jnp.concatenate(back, axis=1)
        dcb_ref[...] += jnp.sum(dcx, axis=0, keepdims=True)
        for tap in range(CONV_W):
            dcw_ref[tap:tap + 1, :] += jnp.sum(dcx * sh[CONV_W - 1 - tap], axis=0, keepdims=True)
        tail = dcx_next[...]
        dlx = dcx * cwv[3:4]
        for d in (1, 2, 3):
            dlx = dlx + _shift_up(dcx, d, tail) * cwv[3 - d:4 - d]
        dcx_next[...] = dcx[0:8]
        dlxg_ref[:, 0:DL] = dlx.astype(BF16)
        dlxg_ref[:, DL:2 * DL] = dlg.astype(BF16)
        acc[...] += _dot(sa_ref[...], sb_ref[...], TN)

        @pl.when(i == nt - 1)
        def _():
            dlam_ref[...] = dlam_ref[...] * (1.0 - jax.nn.sigmoid(lamv))

        @pl.when(i % nk == nk - 1)
        def _():
            so_ref[...] = acc[...]

    rev = lambda i: nt - 1 - i
    prev8_map = lambda i: (jnp.maximum((nt - 1 - i) * t8 - 1, 0), 0)
    vec = pl.BlockSpec((1, DL), lambda i: (0, 0))
    mat = pl.BlockSpec((nb, HEAD_DIM, HEAD_DIM), lambda i: (0, 0, 0))
    return _pcall(
        body, name="lru_bwd", grid=(nt,), comm=comm,
        operands=[proj, proj, proj, h, h, dy, cw, cb, wa, ba, wx, bx, lam, side_a, side_b],
        in_specs=[pl.BlockSpec((tl, DL), lambda i: (rev(i), 0)), pl.BlockSpec((8, DL), prev8_map),
                  pl.BlockSpec((tl, DL), lambda i: (rev(i), 1)),
                  pl.BlockSpec((tl, DL), lambda i: (rev(i), 0)), pl.BlockSpec((8, DL), prev8_map),
                  pl.BlockSpec((tl, DL), lambda i: (rev(i), 0))] + _lru_specs(tl, DL)
        + [pl.BlockSpec((tk, tmo), lambda i: (i % nk, (i // nk) % nmo)),
           pl.BlockSpec((tk, Nj), lambda i: (i % nk, i // (nk * nmo)))],
        out_specs=[pl.BlockSpec((tl, 2 * DL), lambda i: (rev(i), 0)), pl.BlockSpec((CONV_W, DL), lambda i: (0, 0)), vec,
                   mat, vec, mat, vec, vec,
                   pl.BlockSpec((None, tmo, Nj), lambda i: (i // (nk * nmo), (i // nk) % nmo, 0))],
        out_shape=[jax.ShapeDtypeStruct(proj.shape, BF16), jax.ShapeDtypeStruct((CONV_W, DL), F32),
                   jax.ShapeDtypeStruct((1, DL), F32), jax.ShapeDtypeStruct((nb, HEAD_DIM, HEAD_DIM), F32),
                   jax.ShapeDtypeStruct((1, DL), F32), jax.ShapeDtypeStruct((nb, HEAD_DIM, HEAD_DIM), F32),
                   jax.ShapeDtypeStruct((1, DL), F32), jax.ShapeDtypeStruct((1, DL), F32),
                   jax.ShapeDtypeStruct((J, Mo, Nj), F32)],
        scratch_shapes=[pltpu.VMEM((8, DL), F32), pltpu.VMEM((8, DL), F32), pltpu.VMEM((8, DL), F32),
                        pltpu.VMEM((tl, DL), F32), pltpu.VMEM((tl, DL), F32), pltpu.VMEM((tl, DL), F32),
                        pltpu.VMEM((tmo, Nj), F32)])


def _ret_tables(S, H):
    pos = jnp.arange(S, dtype=F32)
    inv_freq = ROPE_BASE ** (-jnp.arange(0, HEAD_DIM, 2, dtype=F32) / HEAD_DIM)
    ang = pos[:, None] * inv_freq[None, :]
    cos, sin = jnp.cos(ang), jnp.sin(ang)
    cosf = jnp.concatenate([cos, cos], axis=1)
    sins = jnp.concatenate([-sin, sin], axis=1)
    log_gamma = jnp.log1p(-jnp.exp2(-5.0 - jnp.arange(H, dtype=F32)))
    idx = jnp.arange(CHUNK)
    diff = idx[:, None] - idx[None, :]
    causal = diff >= 0
    decay = jnp.where(causal[None], jnp.exp(log_gamma[:, None, None] * jnp.where(causal, diff, 0)[None].astype(F32)), 0.0)
    zeta = jnp.exp(log_gamma[:, None] * (CHUNK - 1 - idx).astype(F32)[None, :])
    xi = jnp.exp(log_gamma[:, None] * (idx + 1).astype(F32)[None, :])
    gc = jnp.exp(log_gamma * CHUNK)
    lanes = (H, CHUNK, HEAD_DIM)
    return (cosf, sins, decay, jnp.broadcast_to(zeta[:, :, None], lanes), jnp.broadcast_to(xi[:, :, None], lanes),
            jnp.broadcast_to(gc[:, None, None], lanes))


def _rope(t, cos, sin_signed):
    return t * cos + pltpu.roll(t, HEAD_DIM // 2, 1) * sin_signed


def _rope_t(d, cos, sin_signed):
    return d * cos + pltpu.roll(d * sin_signed, HEAD_DIM // 2, 1)


def _ret_const_specs(H, DR):
    full = pl.BlockSpec((H, CHUNK, HEAD_DIM), lambda *_: (0, 0, 0))
    return [full, full, full, full, pl.BlockSpec((1, DR), lambda *_: (0, 0))]


def _ret_fwd(proj, y, tables, gnw, tb, comm=None):
    S = proj.shape[0]
    DR = gnw.shape[1]
    H = DR // HEAD_DIM
    tb = _tile(S, tb, CHUNK)
    nc = tb // CHUNK
    cosf, sins, dm, zeta, xi, gc = tables
    scale = HEAD_DIM ** -0.5

    def body(qk_ref, vg_ref, cos_ref, sin_ref, dm_ref, zeta_ref, xi_ref, gc_ref, gnw_ref, y_in, y_ref, rprev_ref, r_s):
        del y_in
        i = pl.program_id(0)

        @pl.when(i == 0)
        def _():
            r_s[...] = jnp.zeros_like(r_s)

        def chunk(c, carry):
            rows = pl.ds(pl.multiple_of(c * CHUNK, CHUNK), CHUNK)
            cos = cos_ref[rows, :]
            sin = sin_ref[rows, :]
            heads = range(H)
            c0 = [slice(h * HEAD_DIM, (h + 1) * HEAD_DIM) for h in heads]
            c1 = [slice(DR + h * HEAD_DIM, DR + (h + 1) * HEAD_DIM) for h in heads]
            qh = [_rope(qk_ref[rows, c0[h]], cos, sin) for h in heads]
            kh = [_rope(qk_ref[rows, c1[h]], cos, sin) * scale for h in heads]
            vb = [vg_ref[rows, c0[h]].astype(BF16) for h in heads]
            rp = [r_s[h] for h in heads]
            rpb = [rp[h].astype(BF16) for h in heads]
            s = [_dot(qh[h].astype(BF16), kh[h].astype(BF16), NT) for h in heads]
            kv = [_dot((kh[h] * zeta_ref[h]).astype(BF16), vb[h], TN) for h in heads]
            cross = [_dot((qh[h] * xi_ref[h]).astype(BF16), rpb[h]) for h in heads]
            o = [_dot((s[h] * dm_ref[h]).astype(BF16), vb[h]) + cross[h] for h in heads]
            for h in heads:
                rprev_ref[c, h] = rpb[h]
                r_s[h] = rp[h] * gc_ref[h] + kv[h]
                mu = jnp.mean(o[h], axis=-1, keepdims=True)
                oc = o[h] - mu
                var = jnp.mean(oc * oc, axis=-1, keepdims=True)
                on = oc * lax.rsqrt(var + EPS) * gnw_ref[:, c0[h]]
                gate = vg_ref[rows, c1[h]]
                y_ref[rows, c0[h]] = (gate * jax.nn.sigmoid(gate) * on).astype(BF16)
            return carry

        lax.fori_loop(0, nc, chunk, 0)

    return _pcall(
        body, name="ret_fwd", grid=(S // tb,),
        in_specs=[pl.BlockSpec((tb, 2 * DR), lambda i: (i, 1)), pl.BlockSpec((tb, 2 * DR), lambda i: (i, 2)),
                  pl.BlockSpec((tb, HEAD_DIM), lambda i: (i, 0)), pl.BlockSpec((tb, HEAD_DIM), lambda i: (i, 0))]
        + _ret_const_specs(H, DR) + [HBM_SPEC],
        out_specs=[pl.BlockSpec((tb, DR), lambda i: (i, 1)),
                   pl.BlockSpec((nc, H, CHUNK, HEAD_DIM), lambda i: (i, 0, 0, 0))],
        out_shape=[jax.ShapeDtypeStruct(y.shape, BF16), jax.ShapeDtypeStruct((S // CHUNK, H, CHUNK, HEAD_DIM), BF16)],
        scratch_shapes=[pltpu.VMEM((H, CHUNK, HEAD_DIM), F32)], aliases={9: 0},
        operands=[proj, proj, cosf, sins, dm, zeta, xi, gc, gnw, y], comm=comm)


def _ret_bwd(proj, rprev, dy, dproj, tables, gnw, tb, comm=None):
    S = proj.shape[0]
    DR = gnw.shape[1]
    H = DR // HEAD_DIM
    tb = _tile(S, tb, CHUNK)
    nc = tb // CHUNK
    nt = S // tb
    cosf, sins, dm, zeta, xi, gc = tables
    scale = HEAD_DIM ** -0.5

    def body(qk_ref, vg_ref, cos_ref, sin_ref, dm_ref, zeta_ref, xi_ref, gc_ref, gnw_ref, rprev_ref, dy_ref, dp_in,
             dp_ref, dgn_ref, dr_s, dqk_s, dvg_s, out_sems):
        del dp_in
        i = pl.program_id(0)
        slot = i % 2

        def out_copies(step, sl):
            rows = pl.ds(pl.multiple_of((nt - 1 - step) * tb, tb), tb)
            return (pltpu.make_async_copy(dqk_s.at[sl], dp_ref.at[rows, pl.ds(2 * DR, 2 * DR)], out_sems.at[sl, 0]),
                    pltpu.make_async_copy(dvg_s.at[sl], dp_ref.at[rows, pl.ds(4 * DR, 2 * DR)], out_sems.at[sl, 1]))

        @pl.when(i == 0)
        def _():
            dr_s[...] = jnp.zeros_like(dr_s)
            dgn_ref[...] = jnp.zeros_like(dgn_ref)

        @pl.when(i >= 2)
        def _():
            for cp in out_copies(i - 2, slot):
                cp.wait()

        def chunk(cc, carry):
            c = nc - 1 - cc
            rows = pl.ds(pl.multiple_of(c * CHUNK, CHUNK), CHUNK)
            cos = cos_ref[rows, :]
            sin = sin_ref[rows, :]
            heads = range(H)
            c0 = [slice(h * HEAD_DIM, (h + 1) * HEAD_DIM) for h in heads]
            c1 = [slice(DR + h * HEAD_DIM, DR + (h + 1) * HEAD_DIM) for h in heads]
            qh = [_rope(qk_ref[rows, c0[h]], cos, sin) for h in heads]
            kh = [_rope(qk_ref[rows, c1[h]], cos, sin) * scale for h in heads]
            qb = [t.astype(BF16) for t in qh]
            kb = [t.astype(BF16) for t in kh]
            vb = [vg_ref[rows, c0[h]].astype(BF16) for h in heads]
            rpb = [rprev_ref[c, h] for h in heads]
            qx = [(qh[h] * xi_ref[h]).astype(BF16) for h in heads]
            kz = [(kh[h] * zeta_ref[h]).astype(BF16) for h in heads]
            drh = [dr_s[h] for h in heads]
            drb = [t.astype(BF16) for t in drh]
            s = [_dot(qb[h], kb[h], NT) for h in heads]
            cross = [_dot(qx[h], rpb[h]) for h in heads]
            dv_state = [_dot(kz[h], drb[h]) for h in heads]
            dk_state = [_dot(vb[h], drb[h], NT) for h in heads]
            sb = [(s[h] * dm_ref[h]).astype(BF16) for h in heads]
            o = [_dot(sb[h], vb[h]) + cross[h] for h in heads]
            dob = []
            for h in heads:
                mu = jnp.mean(o[h], axis=-1, keepdims=True)
                oc = o[h] - mu
                rstd = lax.rsqrt(jnp.mean(oc * oc, axis=-1, keepdims=True) + EPS)
                ohat = oc * rstd
                gw = gnw_ref[:, c0[h]]
                gate = vg_ref[rows, c1[h]]
                sg = jax.nn.sigmoid(gate)
                dyv = dy_ref[rows, c0[h]]
                dvg_s[slot, rows, c1[h]] = (dyv * (ohat * gw) * (sg * (1.0 + gate * (1.0 - sg)))).astype(BF16)
                don = dyv * (gate * sg)
                dgn_ref[:, c0[h]] += jnp.sum(don * ohat, axis=0, keepdims=True)
                dohat = don * gw
                do = rstd * (dohat - jnp.mean(dohat, axis=-1, keepdims=True)
                             - ohat * jnp.mean(dohat * ohat, axis=-1, keepdims=True))
                dob.append(do.astype(BF16))
            ds = [_dot(dob[h], vb[h], NT) for h in heads]
            dq_state = [_dot(dob[h], rpb[h], NT) for h in heads]
            dv = [_dot(sb[h], dob[h], TN) + dv_state[h] for h in heads]
            dr_new = [_dot(qx[h], dob[h], TN) for h in heads]
            dsb = [(ds[h] * dm_ref[h]).astype(BF16) for h in heads]
            dqh = [_dot(dsb[h], kb[h]) + dq_state[h] * xi_ref[h] for h in heads]
            dkh = [_dot(dsb[h], qb[h], TN) + dk_state[h] * zeta_ref[h] for h in heads]
            for h in heads:
                dr_s[h] = drh[h] * gc_ref[h] + dr_new[h]
                dqk_s[slot, rows, c0[h]] = _rope_t(dqh[h], cos, sin).astype(BF16)
                dqk_s[slot, rows, c1[h]] = _rope_t(dkh[h] * scale, cos, sin).astype(BF16)
                dvg_s[slot, rows, c0[h]] = dv[h].astype(BF16)
            return carry

        lax.fori_loop(0, nc, chunk, 0)
        for cp in out_copies(i, slot):
            cp.start()

        @pl.when(i == nt - 1)
        def _():
            if nt >= 2:
                for cp in out_copies(i - 1, 1 - slot):
                    cp.wait()
            for cp in out_copies(i, slot):
                cp.wait()

    rev = lambda i: nt - 1 - i
    return _pcall(
        body, name="ret_bwd", grid=(nt,), aliases={11: 0}, comm=comm,
        operands=[proj, proj, cosf, sins, dm, zeta, xi, gc, gnw, rprev, dy, dproj],
        in_specs=[pl.BlockSpec((tb, 2 * DR), lambda i: (rev(i), 1)), pl.BlockSpec((tb, 2 * DR), lambda i: (rev(i), 2)),
                  pl.BlockSpec((tb, HEAD_DIM), lambda i: (rev(i), 0)), pl.BlockSpec((tb, HEAD_DIM), lambda i: (rev(i), 0))]
        + _ret_const_specs(H, DR)
        + [pl.BlockSpec((nc, H, CHUNK, HEAD_DIM), lambda i: (rev(i), 0, 0, 0)),
           pl.BlockSpec((tb, DR), lambda i: (rev(i), 1)), HBM_SPEC],
        out_specs=[HBM_SPEC, pl.BlockSpec((1, DR), lambda i: (0, 0))],
        out_shape=[jax.ShapeDtypeStruct(dproj.shape, BF16), jax.ShapeDtypeStruct((1, DR), F32)],
        scratch_shapes=[pltpu.VMEM((H, CHUNK, HEAD_DIM), F32), pltpu.VMEM((2, tb, 2 * DR), BF16),
                        pltpu.VMEM((2, tb, 2 * DR), BF16), pltpu.SemaphoreType.DMA((2, 2))])


def _place():
    x, y, c = lax.axis_index("x"), lax.axis_index("y"), lax.axis_index("c")
    chips = [(1 - x, y), (x, 1 - y), (1 - x, 1 - y)]
    return x, y, c, chips


def _own_slab(name, shard, place):
    R, C = shard.shape
    tr = _row_tile(R, C)
    return _ew("cast_" + name, lambda a: (a,), [(shard, pl.BlockSpec((tr, C), lambda i, p: (i, 0)))],
               [(jax.ShapeDtypeStruct((4, R, C), BF16), pl.BlockSpec((None, tr, C), lambda i, p: (p[1], i, 0)))],
               (R // tr,), sp=place)[0]


class _remote:
    def __init__(self, src, dst, ssem, rsem, k, to):
        self.args = dict(src_ref=src, dst_ref=dst, send_sem=ssem.at[k], recv_sem=rsem.at[k], device_id=to,
                         device_id_type=MESH)

    def start(self):
        pltpu.make_async_remote_copy(**self.args).start()

    def wait_send(self):
        pltpu.make_async_remote_copy(**self.args).wait_send()

    def wait_recv(self):
        pltpu.make_async_remote_copy(**self.args).wait_recv()


def _task_fns(copies):
    def start(cins, couts, ssem, rsem, base):
        for cp in copies(cins, couts, ssem, rsem, base)[0]:
            cp.start()

    def finish(cins, couts, ssem, rsem, base):
        sends, recvs = copies(cins, couts, ssem, rsem, base)
        for cp in sends:
            cp.wait_send()
        for cp in recvs:
            cp.wait_recv()

    return start, finish


NEIGHBOURS, DIAGONAL = (0, 1), (2,)


def _gather_ici(st, which=NEIGHBOURS + DIAGONAL):
    r2 = st.shape[1] // 2

    def copies(cins, couts, ssem, rsem, base):
        x, y, c, chips = _place()
        out = couts[0]
        mine = out.at[2 * x + y, pl.ds(c * r2, r2), :]
        sends, recvs = [], []
        for k, j in enumerate(which):
            cx, cy = chips[j]
            got = out.at[2 * cx + cy, pl.ds(c * r2, r2), :]
            sends.append(_remote(mine, mine, ssem, rsem, base + k, (cx, cy, c)))
            recvs.append(_remote(got, got, ssem, rsem, base + k, (x, y, c)))
        return sends, recvs

    start, finish = _task_fns(copies)
    return _Comm([st], [jax.ShapeDtypeStruct(st.shape, st.dtype)], {0: 0}, len(which), start, finish)


def _gather_d2d(st):
    r2 = st.shape[1] // 2

    def copies(cins, couts, ssem, rsem, base):
        x, y, c, chips = _place()
        out = couts[0]
        sends, recvs = [], []
        for j, (cx, cy) in enumerate(chips):
            have = out.at[2 * cx + cy, pl.ds(c * r2, r2), :]
            want = out.at[2 * cx + cy, pl.ds((1 - c) * r2, r2), :]
            sends.append(_remote(have, have, ssem, rsem, base + j, (x, y, 1 - c)))
            recvs.append(_remote(want, want, ssem, rsem, base + j, (x, y, c)))
        return sends, recvs

    start, finish = _task_fns(copies)
    return _Comm([st], [jax.ShapeDtypeStruct(st.shape, st.dtype)], {0: 0}, 3, start, finish)


def _gather_conv(conv_w):
    def copies(cins, couts, ssem, rsem, base):
        x, y, c, chips = _place()
        src, out = cins[0], couts[0]
        sends = [_remote(src, out.at[2 * x + y], ssem, rsem, base + j, (*chip, c)) for j, chip in enumerate(chips)]
        recvs = [_remote(src, out.at[2 * cx + cy], ssem, rsem, base + j, (x, y, c)) for j, (cx, cy) in enumerate(chips)]
        return sends, recvs

    start, finish = _task_fns(copies)
    return _Comm([conv_w], [jax.ShapeDtypeStruct((4,) + conv_w.shape, conv_w.dtype)], {}, 3, start, finish)


def _pair_exchange(g):
    r2 = g.shape[1] // 2

    def copies(cins, couts, ssem, rsem, base):
        x, y, c, _ = _place()
        cp = _remote(cins[0].at[:, pl.ds((1 - c) * r2, r2), :], couts[0], ssem, rsem, base, (x, y, 1 - c))
        return [cp], [cp]

    start, finish = _task_fns(copies)
    return _Comm([g], [jax.ShapeDtypeStruct((g.shape[0], r2, g.shape[2]), g.dtype)], {}, 1, start, finish)


def _chip_exchange(part):
    def copies(cins, couts, ssem, rsem, base):
        x, y, c, chips = _place()
        cps = [_remote(cins[0].at[2 * cx + cy], couts[0].at[j], ssem, rsem, base + j, (cx, cy, c))
               for j, (cx, cy) in enumerate(chips)]
        return cps, cps

    start, finish = _task_fns(copies)
    return _Comm([part], [jax.ShapeDtypeStruct((3,) + part.shape[1:], part.dtype)], {}, 3, start, finish)


def _pair_share(slot):
    def copies(cins, couts, ssem, rsem, base):
        x, y, c, _ = _place()
        out = couts[0]
        return ([_remote(out.at[c], out.at[c], ssem, rsem, base, (x, y, 1 - c))],
                [_remote(out.at[1 - c], out.at[1 - c], ssem, rsem, base, (x, y, c))])

    start, finish = _task_fns(copies)
    return _Comm([slot], [jax.ShapeDtypeStruct(slot.shape, slot.dtype)], {0: 0}, 1, start, finish)


def _gather_small(sm):
    flips = [(fx, fy, fc) for fx in (0, 1) for fy in (0, 1) for fc in (0, 1)][1:]

    def copies(cins, couts, ssem, rsem, base):
        x, y, c, _ = _place()
        src, out = cins[0], couts[0]
        peers = [(1 - x if fx else x, 1 - y if fy else y, 1 - c if fc else c) for fx, fy, fc in flips]
        sends = [_remote(src, out.at[4 * x + 2 * y + c], ssem, rsem, base + k, peer) for k, peer in enumerate(peers)]
        recvs = [_remote(src, out.at[4 * px + 2 * py + pc], ssem, rsem, base + k, (x, y, c))
                 for k, (px, py, pc) in enumerate(peers)]
        return sends, recvs

    start, finish = _task_fns(copies)
    return _Comm([sm], [jax.ShapeDtypeStruct((8,) + sm.shape, sm.dtype)], {}, 7, start, finish)


def _comm_call(name, tasks):
    task = _merge(tasks)
    nci = len(task.ins)

    def body(*refs):
        cins, couts, (ssem, rsem) = refs[:nci], refs[nci:nci + len(task.outs)], refs[nci + len(task.outs):]
        task.start(cins, couts, ssem, rsem, 0)
        task.finish(cins, couts, ssem, rsem, 0)

    return pl.pallas_call(
        body, in_specs=[HBM_SPEC] * nci, out_specs=[HBM_SPEC] * len(task.outs), out_shape=list(task.outs),
        scratch_shapes=[pltpu.SemaphoreType.DMA((task.n_sem,)), pltpu.SemaphoreType.DMA((task.n_sem,))],
        input_output_aliases=task.aliases, name=name)(*task.ins)


def _adamw(w, g, m, v):
    m = ADAM_B1 * m + (1.0 - ADAM_B1) * g
    v = ADAM_B2 * v + (1.0 - ADAM_B2) * (g * g)
    m_hat = m / (1.0 - ADAM_B1 ** ADAM_STEP)
    v_hat = v / (1.0 - ADAM_B2 ** ADAM_STEP)
    delta = -ADAM_LR * (m_hat / (jnp.sqrt(v_hat) + ADAM_EPS) + ADAM_WD * w)
    return delta, m, v


def _adamw_call(name, w, g, m, v):
    R, C = w.shape
    tr = _row_tile(R, C, 1024 * 1024)
    row = pl.BlockSpec((tr, C), lambda i: (i, 0))
    o = jax.ShapeDtypeStruct((R, C), F32)
    return _ew(name, lambda w_, g_, m_, v_: (*_adamw(w_, g_, m_, v_), g_), [(w, row), (g, row), (m, row), (v, row)],
               [(o, row), (o, row), (o, row), (o, row)], (R // tr,))


def _pair_sum(name, g, ra, place):
    _, R, C = g.shape
    r2 = R // 2
    tr = _row_tile(r2, C)
    nb = r2 // tr
    own = pl.BlockSpec((None, tr, C), lambda j, i, p: (j, p[0] * nb + i, 0))
    blk = pl.BlockSpec((None, tr, C), lambda j, i, p: (j, i, 0))
    return _ew("rs_pair_sum_" + name, lambda a, b: (a + b,), [(g, own), (ra, blk)],
               [(jax.ShapeDtypeStruct((4, r2, C), BF16), blk)], (4, nb), sp=place)[0]


def _chip_sum(name, g, ra, rb, place):
    _, R, C = g.shape
    r2 = R // 2
    tr = _row_tile(r2, C)
    nb = r2 // tr
    own = pl.BlockSpec((None, tr, C), lambda i, p: (p[1], p[0] * nb + i, 0))
    mine = pl.BlockSpec((None, tr, C), lambda i, p: (p[1], i, 0))
    src = [pl.BlockSpec((None, tr, C), functools.partial(lambda i, p, j: (j, i, 0), j=j)) for j in range(3)]
    out = pl.BlockSpec((None, tr, C), lambda i, p: (p[0], i, 0))

    def total(a, b, r0, r1, r2_):
        return ((((a + b) + r0.astype(F32)) + r1.astype(F32)) + r2_.astype(F32),)

    return _ew("rs_chip_sum_" + name, total, [(g, own), (ra, mine), (rb, src[0]), (rb, src[1]), (rb, src[2])],
               [(jax.ShapeDtypeStruct((2, r2, C), F32), out)], (nb,), sp=place)[0]


def _pack(arrays):
    rows, offs, pos = [], [], 0
    for a in arrays:
        flat = a.reshape(-1)
        n = -(-flat.shape[0] // (8 * LANES)) * (8 * LANES)
        if n != flat.shape[0]:
            flat = jnp.pad(flat, (0, n - flat.shape[0]))
        rows.append(flat.reshape(-1, LANES))
        offs.append(pos)
        pos += n // LANES
    return jnp.concatenate(rows, axis=0), offs


def _unpack(packed, offs, shapes):
    out = []
    for off, shp in zip(offs, shapes):
        n = 1
        for s in shp:
            n *= s
        out.append(packed[off:off + -(-n // LANES)].reshape(-1)[:n].reshape(shp))
    return out


def _sum8(gathered):
    _, R, C = gathered.shape
    tr = _row_tile(R, C, 256 * 1024)
    specs = [pl.BlockSpec((None, tr, C), functools.partial(lambda i, d: (d, i, 0), d=d)) for d in range(8)]

    def fn(*parts):
        t = parts[0]
        for p in parts[1:]:
            t = t + p
        return (t,)

    return _ew("small_sum", fn, [(gathered, s) for s in specs],
               [(jax.ShapeDtypeStruct((R, C), F32), pl.BlockSpec((tr, C), lambda i: (i, 0)))], (R // tr,))[0]


BIG = ("w_in", "w_out", "w_ffn_gate", "w_ffn_up", "w_ffn_down")
SMALL = ("ln1_w", "conv_w", "conv_b", "gate_a_w", "gate_a_b", "gate_x_w", "gate_x_b", "lru_lambda", "ret_gn_w", "ln2_w",
         "final_norm_w")
WEIGHTS = ("ln1_w", "w_in", "conv_w", "conv_b", "gate_a_w", "gate_a_b", "gate_x_w", "gate_x_b", "lru_lambda", "ret_gn_w",
           "w_out", "ln2_w", "w_ffn_gate", "w_ffn_up", "w_ffn_down", "final_norm_w")


def kernel(x, ln1_w, w_in, conv_w, conv_b, gate_a_w, gate_a_b, gate_x_w, gate_x_b, lru_lambda, ret_gn_w, w_out, ln2_w, w_ffn_gate, w_ffn_up, w_ffn_down, final_norm_w, loss_target, m_ln1_w, m_w_in, m_conv_w, m_conv_b, m_gate_a_w, m_gate_a_b, m_gate_x_w, m_gate_x_b, m_lru_lambda, m_ret_gn_w, m_w_out, m_ln2_w, m_w_ffn_gate, m_w_ffn_up, m_w_ffn_down, m_final_norm_w, v_ln1_w, v_w_in, v_conv_w, v_conv_b, v_gate_a_w, v_gate_a_b, v_gate_x_w, v_gate_x_b, v_lru_lambda, v_ret_gn_w, v_w_out, v_ln2_w, v_w_ffn_gate, v_w_ffn_up, v_w_ffn_down, v_final_norm_w):
    w = dict(ln1_w=ln1_w, w_in=w_in, conv_w=conv_w, conv_b=conv_b, gate_a_w=gate_a_w, gate_a_b=gate_a_b, gate_x_w=gate_x_w,
             gate_x_b=gate_x_b, lru_lambda=lru_lambda, ret_gn_w=ret_gn_w, w_out=w_out, ln2_w=ln2_w, w_ffn_gate=w_ffn_gate,
             w_ffn_up=w_ffn_up, w_ffn_down=w_ffn_down, final_norm_w=final_norm_w)
    m = dict(ln1_w=m_ln1_w, w_in=m_w_in, conv_w=m_conv_w, conv_b=m_conv_b, gate_a_w=m_gate_a_w, gate_a_b=m_gate_a_b,
             gate_x_w=m_gate_x_w, gate_x_b=m_gate_x_b, lru_lambda=m_lru_lambda, ret_gn_w=m_ret_gn_w, w_out=m_w_out,
             ln2_w=m_ln2_w, w_ffn_gate=m_w_ffn_gate, w_ffn_up=m_w_ffn_up, w_ffn_down=m_w_ffn_down,
             final_norm_w=m_final_norm_w)
    v = dict(ln1_w=v_ln1_w, w_in=v_w_in, conv_w=v_conv_w, conv_b=v_conv_b, gate_a_w=v_gate_a_w, gate_a_b=v_gate_a_b,
             gate_x_w=v_gate_x_w, gate_x_b=v_gate_x_b, lru_lambda=v_lru_lambda, ret_gn_w=v_ret_gn_w, w_out=v_w_out,
             ln2_w=v_ln2_w, w_ffn_gate=v_w_ffn_gate, w_ffn_up=v_w_ffn_up, w_ffn_down=v_w_ffn_down,
             final_norm_w=v_final_norm_w)
    xs, tgt = x[0], loss_target[0]
    S, D = xs.shape
    DL, DR = conv_b.shape[1], ret_gn_w.shape[1]
    assert DL == DR and DL % HEAD_DIM == 0 and S % CHUNK == 0
    d_mix = DL + DR
    cx, cy, cc = lax.axis_index("x"), lax.axis_index("y"), lax.axis_index("c")
    chip = 2 * cx + cy
    place = jnp.stack([cc, chip]).astype(jnp.int32)
    grad, delta, new_m, new_v = {}, {}, {}, {}

    def finish_big(n, full):
        shp = w[n].shape
        g2 = full.reshape(shp[1], shp[2])
        w2, m2, v2 = (t[n].reshape(shp[1], shp[2]) for t in (w, m, v))
        d_, m_, v_, g_ = _adamw_call("adamw_" + n, w2, g2, m2, v2)
        grad[n], delta[n], new_m[n], new_v[n] = (t.reshape(shp) for t in (g_, d_, m_, v_))

    def all_sum(gathered, own):
        return _sum8(lax.dynamic_update_slice(gathered, own[None], (4 * cx + 2 * cy + cc, 0, 0)))

    st = {n: _own_slab(n, w[n][0], place) for n in BIG}
    TM, TK = 512, 2048
    (u1,), (w_in_st,) = _rms_fwd("rms1", xs, ln1_w, TM, comm=_gather_ici(st["w_in"]))
    w_in_st, conv_st = _comm_call("gather_w_in", [_gather_d2d(w_in_st), _gather_conv(conv_w[0])])
    conv_st = lax.dynamic_update_slice(conv_st, conv_w, (chip, 0, 0))
    cw_cols = conv_st.shape[2]
    conv_full = jnp.transpose(conv_st, (1, 0, 2)).reshape(CONV_W, 4 * cw_cols)
    n_in, n_ff = w_in_st.shape[2], st["w_ffn_gate"].shape[2]
    tables = _ret_tables(S, DR // HEAD_DIM)
    wab, wxb = gate_a_w[0].astype(BF16), gate_x_w[0].astype(BF16)
    lru_w = (conv_full, conv_b, wab, gate_a_b, wxb, gate_x_b, lru_lambda)

    proj, (w_out_st, wg_st) = _mm_nn_stacked("proj", u1, w_in_st, F32, TM,
                                             comm=_merge([_gather_ici(st["w_out"]), _gather_ici(st["w_ffn_gate"])]))
    (hs, y), (w_out_st, wg_st, wu_st) = _lru_fwd(
        proj, *lru_w, 128, d_mix, comm=_merge([_gather_d2d(w_out_st), _gather_d2d(wg_st), _gather_ici(st["w_ffn_up"])]))
    (y, rprev), (wu_st, wd_st) = _ret_fwd(proj, y, tables, ret_gn_w, 256,
                                          comm=_merge([_gather_d2d(wu_st), _gather_ici(st["w_ffn_down"], NEIGHBOURS)]))
    w_out_f = w_out_st.reshape(d_mix, D)
    (h1, u2), (wd_st,) = _out_proj_rms(y, w_out_f, xs, ln2_w, TM, comm=_gather_ici(wd_st, DIAGONAL))
    (dg_fac, du_fac, ff), (wd_st,) = _ffn_gate_up(u2, wg_st, wu_st, TM, comm=_gather_d2d(wd_st))
    wd_f = wd_st.reshape(4 * n_ff, D)
    dh2, dh2b, d_fw, loss = _ffn_down_loss(ff, wd_f, h1, tgt, final_norm_w.reshape(1, D), 256)

    g_wd = _mm_tn("g_w_down", ff, dh2b, n_ff, 1024, TK).reshape(4, n_ff, D)
    (dgt, dup), (ra_wd,) = _ffn_gate_up_bwd(dh2b, wd_f, dg_fac, du_fac, TM, n_ff, comm=_pair_exchange(g_wd))
    pb_wd = _pair_sum("w_ffn_down", g_wd, ra_wd, place)
    g_wg, (rb_wd,) = _mm_tn("g_w_gate", u2, dgt, 1024, None, TK, stacked_cols=n_ff, comm=_chip_exchange(pb_wd))
    slot_wd = _chip_sum("w_ffn_down", g_wd, ra_wd, rb_wd, place)
    du2, (full_wd, ra_wg) = _mm_nt_stacked("d_u2_gate", dgt, wg_st, 256,
                                           comm=_merge([_pair_share(slot_wd), _pair_exchange(g_wg)]))
    finish_big("w_ffn_down", full_wd)
    pb_wg = _pair_sum("w_ffn_gate", g_wg, ra_wg, place)
    du2, (rb_wg,) = _mm_nt_stacked("d_u2_up", dup, wu_st, 256, res=du2, comm=_chip_exchange(pb_wg))
    slot_wg = _chip_sum("w_ffn_gate", g_wg, ra_wg, rb_wg, place)
    (dh1, dh1b, dy, d_ln2), (full_wg,) = _rms_bwd_dy(h1, ln2_w, du2, dh2, w_out_f, 256, comm=_pair_share(slot_wg))
    finish_big("w_ffn_gate", full_wg)
    g_wout = _mm_tn("g_w_out", y, dh1b, 1024, 1024, TK).reshape(4, d_mix // 4, D)
    (dproj, d_cw, d_cb, d_wa, d_ba, d_wx, d_bx, d_lam, g_wu), (ra_wout,) = _lru_bwd(
        proj, hs, dy, *lru_w, 128, (u2, dup, n_ff), comm=_pair_exchange(g_wout))
    pb_wout = _pair_sum("w_out", g_wout, ra_wout, place)
    (dproj, d_gn), (rb_wout, ra_wu) = _ret_bwd(proj, rprev, dy, dproj, tables, ret_gn_w, 256,
                                               comm=_merge([_chip_exchange(pb_wout), _pair_exchange(g_wu)]))
    slot_wout = _chip_sum("w_out", g_wout, ra_wout, rb_wout, place)
    pb_wu = _pair_sum("w_ffn_up", g_wu, ra_wu, place)
    small = dict(conv_w=d_cw, conv_b=d_cb, gate_a_w=d_wa, gate_a_b=d_ba, gate_x_w=d_wx, gate_x_b=d_bx, lru_lambda=d_lam,
                 ret_gn_w=d_gn, ln2_w=d_ln2, final_norm_w=d_fw)
    packed, offs = _pack([small[n] for n in SMALL[1:]] + [loss])
    g_win, (full_wout, rb_wu, got_small) = _mm_tn(
        "g_w_in", u1, dproj, 1024, None, TK, stacked_cols=n_in,
        comm=_merge([_pair_share(slot_wout), _chip_exchange(pb_wu), _gather_small(packed)]))
    finish_big("w_out", full_wout)
    slot_wu = _chip_sum("w_ffn_up", g_wu, ra_wu, rb_wu, place)
    (ra_win,) = _comm_call("rs_pair_w_in", [_pair_exchange(g_win)])
    pb_win = _pair_sum("w_in", g_win, ra_win, place)
    du1, (rb_win, full_wu) = _mm_nt_stacked("d_u1", dproj, w_in_st, 256,
                                            comm=_merge([_chip_exchange(pb_win), _pair_share(slot_wu)]))
    finish_big("w_ffn_up", full_wu)
    slot_win = _chip_sum("w_in", g_win, ra_win, rb_win, place)
    gx, d_ln1 = _rms_bwd("rms1_bwd", xs, ln1_w, du1, dh1, TM)
    packed1, _ = _pack([d_ln1])
    full_win, got_ln1 = _comm_call("reduce_tail", [_pair_share(slot_win), _gather_small(packed1)])
    finish_big("w_in", full_win)

    red = _unpack(all_sum(got_small, packed), offs, [small[n].shape for n in SMALL[1:]] + [(1, LANES)])
    g = dict(zip(SMALL[1:], red[:-1]))
    g["ln1_w"] = all_sum(got_ln1, packed1)[:-(-D // LANES)].reshape(1, D)
    loss_out = red[-1][0, 0]
    g["conv_w"] = lax.dynamic_slice(g["conv_w"], (0, chip * cw_cols), (CONV_W, cw_cols))
    packs = [_pack([t[n] for n in SMALL])[0] for t in (w, m, v)]
    gp, offs2 = _pack([g[n] for n in SMALL])
    outs = _adamw_call("adamw_small", packs[0], gp, packs[1], packs[2])
    shapes = [w[n].shape for n in SMALL]
    for dst, arr in zip((delta, new_m, new_v), outs):
        dst.update(zip(SMALL, _unpack(arr, offs2, shapes)))
    for n in SMALL:
        grad[n] = g[n].reshape(w[n].shape)

    return (loss_out, gx.reshape(x.shape), *[grad[n] for n in WEIGHTS], *[delta[n] for n in WEIGHTS],
            *[new_m[n] for n in WEIGHTS], *[new_v[n] for n in WEIGHTS])
```

```python
import functools

import jax
import jax.numpy as jnp
from jax import lax
from jax.experimental import pallas as pl
from jax.experimental.pallas import tpu as pltpu

F32 = jnp.float32
BF16 = jnp.bfloat16
MESH = pl.DeviceIdType.MESH

EPS = 1e-6
LRU_C = 8.0
ROPE_BASE = 10000.0
CHUNK = 128
HEAD_DIM = 128
CONV_W = 4
ADAM_LR = 0.001
ADAM_B1 = 0.9
ADAM_B2 = 0.999
ADAM_EPS = 1e-08
ADAM_WD = 0.01
ADAM_STEP = 10

V7X_VMEM_BYTES = 64 * 1024 * 1024
VMEM_LIMIT = V7X_VMEM_BYTES - 8 * 1024 * 1024
LANES = 128
SUBLANES_16BIT = 16

NN = (((1,), (0,)), ((), ()))
NT = (((1,), (1,)), ((), ()))
TN = (((0,), (0,)), ((), ()))


def _dot(a, b, dims=NN):
    return lax.dot_general(a, b, dims, preferred_element_type=F32)


def _tile(n, pref, mult=SUBLANES_16BIT):
    best = None
    t = mult
    while t <= min(n, pref):
        if n % t == 0:
            best = t
        t += mult
    return best if best is not None else n


def _row_tile(rows, cols, budget_bytes=2 * 1024 * 1024):
    return _tile(rows, max(SUBLANES_16BIT, budget_bytes // (cols * 4)))


def _params(sem):
    return pltpu.CompilerParams(dimension_semantics=sem, vmem_limit_bytes=VMEM_LIMIT)


HBM_SPEC = pl.BlockSpec(memory_space=pl.ANY)


class _Comm:
    def __init__(self, ins, outs, aliases, n_sem, start, finish):
        self.ins, self.outs, self.aliases, self.n_sem, self.start, self.finish = ins, outs, aliases, n_sem, start, finish


def _merge(tasks):
    ins, outs, aliases, plans, n_sem = [], [], {}, [], 0
    for t in tasks:
        i0, o0 = len(ins), len(outs)
        plans.append((t, i0, o0, n_sem))
        ins += t.ins
        outs += t.outs
        aliases.update({i0 + a: o0 + b for a, b in t.aliases.items()})
        n_sem += t.n_sem

    def run(which):
        def go(cins, couts, ssem, rsem, base):
            for t, i0, o0, s0 in plans:
                getattr(t, which)(cins[i0:i0 + len(t.ins)], couts[o0:o0 + len(t.outs)], ssem, rsem, base + s0)
        return go

    return _Comm(ins, outs, aliases, n_sem, run("start"), run("finish"))


def _pcall(body, *, name, grid, in_specs, out_specs, out_shape, operands, scratch_shapes=(), aliases=None, comm=None):
    n_in, n_out, n_scr = len(operands), len(out_shape), len(scratch_shapes)
    aliases = dict(aliases or {})
    params = _params(("arbitrary",) * len(grid))
    if comm is None:
        return pl.pallas_call(body, grid=grid, in_specs=list(in_specs), out_specs=list(out_specs), out_shape=list(out_shape),
                              scratch_shapes=list(scratch_shapes), input_output_aliases=aliases, name=name,
                              compiler_params=params)(*operands)
    nci, nco = len(comm.ins), len(comm.outs)

    def wrapped(*refs):
        ins, cins = refs[:n_in], refs[n_in:n_in + nci]
        o0 = n_in + nci
        outs, couts = refs[o0:o0 + n_out], refs[o0 + n_out:o0 + n_out + nco]
        s0 = o0 + n_out + nco
        scr, (ssem, rsem) = refs[s0:s0 + n_scr], refs[s0 + n_scr:]
        ids = [pl.program_id(a) for a in range(len(grid))]
        first = functools.reduce(jnp.logical_and, [i == 0 for i in ids])
        last = functools.reduce(jnp.logical_and, [i == g - 1 for i, g in zip(ids, grid)])

        @pl.when(first)
        def _():
            comm.start(cins, couts, ssem, rsem, 0)

        body(*ins, *outs, *scr)

        @pl.when(last)
        def _():
            comm.finish(cins, couts, ssem, rsem, 0)

    aliases.update({n_in + a: n_out + b for a, b in comm.aliases.items()})
    res = pl.pallas_call(
        wrapped, grid=grid, in_specs=list(in_specs) + [HBM_SPEC] * nci, out_specs=list(out_specs) + [HBM_SPEC] * nco,
        out_shape=list(out_shape) + list(comm.outs),
        scratch_shapes=list(scratch_shapes) + [pltpu.SemaphoreType.DMA((comm.n_sem,)), pltpu.SemaphoreType.DMA((comm.n_sem,))],
        input_output_aliases=aliases, name=name, compiler_params=params)(*operands, *comm.ins)
    return res[:n_out], res[n_out:]


def _ew(name, fn, ins, outs, grid, sp=None):
    n_in = len(ins)

    def body(*refs):
        if sp is not None:
            refs = refs[1:]
        vals = [r[...] for r in refs[:n_in]]
        res = fn(*vals)
        for o_ref, v in zip(refs[n_in:], res):
            o_ref[...] = v.astype(o_ref.dtype)

    in_specs = [s for _, s in ins]
    out_specs = [s for _, s in outs]
    out_shape = [s for s, _ in outs]
    sem = ("arbitrary",) * len(grid)
    if sp is None:
        return pl.pallas_call(body, grid=grid, in_specs=in_specs, out_specs=out_specs, out_shape=out_shape,
                              name=name, compiler_params=_params(sem))(*[a for a, _ in ins])
    gs = pltpu.PrefetchScalarGridSpec(num_scalar_prefetch=1, grid=grid, in_specs=in_specs, out_specs=out_specs)
    return pl.pallas_call(body, grid_spec=gs, out_shape=out_shape, name=name,
                          compiler_params=_params(sem))(sp, *[a for a, _ in ins])


def _matmul(name, pairs, dims, grid, out_shape, out_spec, acc_shape, res=None, comm=None):
    n = len(pairs)
    nk = grid[2]

    def body(*refs):
        ab = refs[:2 * n]
        pos = 2 * n
        res_ref = None
        if res is not None:
            res_ref = refs[pos]
            pos += 1
        o_ref = refs[pos]
        acc_ref = refs[pos + 1] if nk > 1 else None

        def partial():
            t = None
            for p in range(n):
                d = _dot(ab[2 * p][...], ab[2 * p + 1][...], dims)
                t = d if t is None else t + d
            return t

        def finish(t):
            if res_ref is not None:
                t = t + res_ref[...]
            o_ref[...] = t.astype(o_ref.dtype)

        if nk == 1:
            finish(partial())
        else:
            k = pl.program_id(2)

            @pl.when(k == 0)
            def _():
                acc_ref[...] = partial()

            @pl.when(k > 0)
            def _():
                acc_ref[...] += partial()

            @pl.when(k == nk - 1)
            def _():
                finish(acc_ref[...])

    operands, in_specs = [], []
    for a, a_spec, b, b_spec in pairs:
        operands += [a, b]
        in_specs += [a_spec, b_spec]
    if res is not None:
        operands.append(res[0])
        in_specs.append(res[1])
    scratch = [pltpu.VMEM(acc_shape, F32)] if nk > 1 else []
    res = _pcall(body, name=name, grid=grid, in_specs=in_specs, out_specs=[out_spec], out_shape=[out_shape],
                 operands=operands, scratch_shapes=scratch, comm=comm)
    return res[0] if comm is None else (res[0][0], res[1])


def _mm_nn_stacked(name, a, b_st, out_dtype, tm, comm=None):
    M, K = a.shape
    J, _, Nj = b_st.shape
    tm = _tile(M, tm)
    return _matmul(
        name, [(a, pl.BlockSpec((tm, K), lambda j, i, k: (i, 0)), b_st, pl.BlockSpec((None, K, Nj), lambda j, i, k: (j, 0, 0)))],
        NN, (J, M // tm, 1), jax.ShapeDtypeStruct((M, J * Nj), out_dtype), pl.BlockSpec((tm, Nj), lambda j, i, k: (i, j)), None,
        comm=comm)


def _mm_nt_stacked(name, a, b_st, tm, res=None, comm=None):
    M = a.shape[0]
    J, N, Nj = b_st.shape
    tm = _tile(M, tm)

    def body(a_ref, b_ref, *rest):
        o_ref = rest[-1]
        t = None if res is None else rest[0][...]
        for s in range(J):
            d = _dot(a_ref[:, s * Nj:(s + 1) * Nj], b_ref[s], NT)
            t = d if t is None else t + d
        o_ref[...] = t

    row = pl.BlockSpec((tm, N), lambda i: (i, 0))
    out = _pcall(body, name=name, grid=(M // tm,),
                 in_specs=[pl.BlockSpec((tm, J * Nj), lambda i: (i, 0)),
                           pl.BlockSpec((J, N, Nj), lambda i: (0, 0, 0), pipeline_mode=pl.Buffered(1))] + [row] * (res is not None),
                 out_specs=[row], out_shape=[jax.ShapeDtypeStruct((M, N), F32)],
                 operands=[a, b_st] + [res] * (res is not None), comm=comm)
    return out[0] if comm is None else (out[0][0], out[1])


MXU_COLUMNS = 256


def _col_blocks(n):
    return [slice(s, min(s + MXU_COLUMNS, n)) for s in range(0, n, MXU_COLUMNS)]


def _ffn_gate_up(u2, wg_st, wu_st, tm, comm=None):
    S, D = u2.shape
    J, _, Nj = wg_st.shape
    tm = _tile(S, tm)

    def body(a_ref, wg_ref, wu_ref, dg_ref, du_ref, ff_ref):
        a = a_ref[...]
        blocks = _col_blocks(Nj)
        ahead = (_dot(a, wg_ref[:, blocks[0]]), _dot(a, wu_ref[:, blocks[0]]))
        for j, cols in enumerate(blocks):
            g, u = ahead
            if j + 1 < len(blocks):
                ahead = (_dot(a, wg_ref[:, blocks[j + 1]]), _dot(a, wu_ref[:, blocks[j + 1]]))
            sg = jax.nn.sigmoid(g)
            silu = g * sg
            dg_ref[:, cols] = (u * (sg * (1.0 + g * (1.0 - sg)))).astype(BF16)
            du_ref[:, cols] = silu.astype(BF16)
            ff_ref[:, cols] = (silu * u).astype(BF16)

    w_spec = pl.BlockSpec((None, D, Nj), lambda j, i: (j, 0, 0))
    o_spec = pl.BlockSpec((tm, Nj), lambda j, i: (i, j))
    o = jax.ShapeDtypeStruct((S, J * Nj), BF16)
    return _pcall(body, name="ffn_gate_up", grid=(J, S // tm),
                  in_specs=[pl.BlockSpec((tm, D), lambda j, i: (i, 0)), w_spec, w_spec],
                  out_specs=[o_spec, o_spec, o_spec], out_shape=[o, o, o], operands=[u2, wg_st, wu_st], comm=comm)


def _ffn_gate_up_bwd(dh2b, wd, dg_fac, du_fac, tm, tn, comm=None):
    S, D = dh2b.shape
    F = wd.shape[0]
    tm, tn = _tile(S, tm), _tile(F, tn, LANES)

    def body(a_ref, wd_ref, dg_ref, du_ref, dgt_ref, dup_ref):
        a = a_ref[...]
        for cols in _col_blocks(tn):
            d = _dot(a, wd_ref[cols, :], NT)
            dgt_ref[:, cols] = (d * dg_ref[:, cols].astype(F32)).astype(BF16)
            dup_ref[:, cols] = (d * du_ref[:, cols].astype(F32)).astype(BF16)

    blk = pl.BlockSpec((tm, tn), lambda j, i: (i, j))
    o = jax.ShapeDtypeStruct((S, F), BF16)
    return _pcall(body, name="ffn_gate_up_bwd", grid=(F // tn, S // tm),
                  in_specs=[pl.BlockSpec((tm, D), lambda j, i: (i, 0)), pl.BlockSpec((tn, D), lambda j, i: (j, 0)), blk, blk],
                  out_specs=[blk, blk], out_shape=[o, o], operands=[dh2b, wd, dg_fac, du_fac], comm=comm)


def _mm_tn(name, a, b, tmo, tn, tk, stacked_cols=None, comm=None):
    S, Mo = a.shape
    N = b.shape[1]
    tmo, tk = _tile(Mo, tmo, LANES), _tile(S, tk)
    if stacked_cols is None:
        tn = _tile(N, tn, LANES)
        out_shape = jax.ShapeDtypeStruct((Mo, N), F32)
        out_spec = pl.BlockSpec((tmo, tn), lambda i, j, k: (i, j))
    else:
        tn = stacked_cols
        out_shape = jax.ShapeDtypeStruct((N // tn, Mo, tn), F32)
        out_spec = pl.BlockSpec((None, tmo, tn), lambda i, j, k: (j, i, 0))
    return _matmul(
        name, [(a, pl.BlockSpec((tk, tmo), lambda i, j, k: (k, i)), b, pl.BlockSpec((tk, tn), lambda i, j, k: (k, j)))],
        TN, (Mo // tmo, N // tn, S // tk), out_shape, out_spec, (tmo, tn), comm=comm)


def _rms_fwd(name, x, w, tm, comm=None):
    S, D = x.shape
    tm = _tile(S, tm)

    def body(x_ref, w_ref, o_ref):
        xv = x_ref[...]
        r = lax.rsqrt(jnp.mean(xv * xv, axis=-1, keepdims=True) + EPS)
        o_ref[...] = ((xv * r) * w_ref[...]).astype(BF16)

    row = pl.BlockSpec((tm, D), lambda i: (i, 0))
    return _pcall(body, name=name, grid=(S // tm,), in_specs=[row, pl.BlockSpec((1, D), lambda i: (0, 0))], out_specs=[row],
                  out_shape=[jax.ShapeDtypeStruct((S, D), BF16)], operands=[x, w], comm=comm)


def _rms_bwd(name, x, w, dy, dres, tm, comm=None):
    S, D = x.shape
    tm = _tile(S, tm)

    def body(x_ref, w_ref, dy_ref, dres_ref, dx_ref, dw_ref):
        i = pl.program_id(0)

        @pl.when(i == 0)
        def _():
            dw_ref[...] = jnp.zeros_like(dw_ref)

        xv = x_ref[...]
        r = lax.rsqrt(jnp.mean(xv * xv, axis=-1, keepdims=True) + EPS)
        nv = xv * r
        dyv = dy_ref[...]
        dn = dyv * w_ref[...]
        dw_ref[...] += jnp.sum(dyv * nv, axis=0, keepdims=True)
        dx = dres_ref[...] + r * (dn - nv * jnp.mean(dn * nv, axis=-1, keepdims=True))
        dx_ref[...] = dx

    row = pl.BlockSpec((tm, D), lambda i: (i, 0))
    vec = pl.BlockSpec((1, D), lambda i: (0, 0))
    return _pcall(body, name=name, grid=(S // tm,), in_specs=[row, vec, row, row], out_specs=[row, vec],
                  out_shape=[jax.ShapeDtypeStruct((S, D), F32), jax.ShapeDtypeStruct((1, D), F32)],
                  operands=[x, w, dy, dres], comm=comm)


def _rms_bwd_dy(h1, w, du2, dh2, w_out, tm, comm=None):
    S, D = h1.shape
    d_mix = w_out.shape[0]
    tm = _tile(S, tm)

    def body(x_ref, w_ref, dy_ref, dres_ref, wo_ref, dx_ref, dxb_ref, out_ref, dw_ref):
        i = pl.program_id(0)

        @pl.when(i == 0)
        def _():
            dw_ref[...] = jnp.zeros_like(dw_ref)

        xv = x_ref[...]
        r = lax.rsqrt(jnp.mean(xv * xv, axis=-1, keepdims=True) + EPS)
        nv = xv * r
        dyv = dy_ref[...]
        dn = dyv * w_ref[...]
        dw_ref[...] += jnp.sum(dyv * nv, axis=0, keepdims=True)
        dx = dres_ref[...] + r * (dn - nv * jnp.mean(dn * nv, axis=-1, keepdims=True))
        dx_ref[...] = dx
        dxb = dx.astype(BF16)
        dxb_ref[...] = dxb
        out_ref[...] = _dot(dxb, wo_ref[...], NT)

    row = pl.BlockSpec((tm, D), lambda i: (i, 0))
    vec = pl.BlockSpec((1, D), lambda i: (0, 0))
    return _pcall(
        body, name="rms2_bwd_dy", grid=(S // tm,),
        in_specs=[row, vec, row, row, pl.BlockSpec((d_mix, D), lambda i: (0, 0), pipeline_mode=pl.Buffered(1))],
        out_specs=[row, row, pl.BlockSpec((tm, d_mix), lambda i: (i, 0)), vec],
        out_shape=[jax.ShapeDtypeStruct((S, D), F32), jax.ShapeDtypeStruct((S, D), BF16),
                   jax.ShapeDtypeStruct((S, d_mix), F32), jax.ShapeDtypeStruct((1, D), F32)],
        operands=[h1, w, du2, dh2, w_out], comm=comm)


def _out_proj_rms(y, w_out, x, ln_w, tm, comm=None):
    S, K = y.shape
    D = w_out.shape[1]
    tm = _tile(S, tm)

    def body(a_ref, w_ref, x_ref, lw_ref, h_ref, u_ref):
        hv = _dot(a_ref[...], w_ref[...]) + x_ref[...]
        h_ref[...] = hv
        r = lax.rsqrt(jnp.mean(hv * hv, axis=-1, keepdims=True) + EPS)
        u_ref[...] = ((hv * r) * lw_ref[...]).astype(BF16)

    row = pl.BlockSpec((tm, D), lambda i: (i, 0))
    return _pcall(
        body, name="out_proj", grid=(S // tm,),
        in_specs=[pl.BlockSpec((tm, K), lambda i: (i, 0)),
                  pl.BlockSpec((K, D), lambda i: (0, 0), pipeline_mode=pl.Buffered(1)), row,
                  pl.BlockSpec((1, D), lambda i: (0, 0))],
        out_specs=[row, row], out_shape=[jax.ShapeDtypeStruct((S, D), F32), jax.ShapeDtypeStruct((S, D), BF16)],
        operands=[y, w_out, x, ln_w], comm=comm)


def _ffn_down_loss(ff, wd, h1, tgt, fw, tm):
    S, K = ff.shape
    D = wd.shape[1]
    tm = _tile(S, tm)

    def body(a_ref, wd_ref, h1_ref, t_ref, w_ref, dh_ref, dhb_ref, dw_ref, loss_ref):
        i = pl.program_id(0)

        @pl.when(i == 0)
        def _():
            dw_ref[...] = jnp.zeros_like(dw_ref)
            loss_ref[...] = jnp.zeros_like(loss_ref)

        hv = _dot(a_ref[...], wd_ref[...]) + h1_ref[...]
        wv = w_ref[...]
        r = lax.rsqrt(jnp.mean(hv * hv, axis=-1, keepdims=True) + EPS)
        nv = hv * r
        err = nv * wv - t_ref[...]
        row_loss = jnp.mean(err * err, axis=-1, keepdims=True)
        loss_ref[...] += 0.5 * jnp.sum(row_loss, axis=0, keepdims=True)
        dyo = err * (1.0 / D)
        dn = dyo * wv
        dw_ref[...] += jnp.sum(dyo * nv, axis=0, keepdims=True)
        dh = r * (dn - nv * jnp.mean(dn * nv, axis=-1, keepdims=True))
        dh_ref[...] = dh
        dhb_ref[...] = dh.astype(BF16)

    row = pl.BlockSpec((tm, D), lambda i: (i, 0))
    vec = pl.BlockSpec((1, D), lambda i: (0, 0))
    return _pcall(
        body, name="ffn_down_loss", grid=(S // tm,),
        in_specs=[pl.BlockSpec((tm, K), lambda i: (i, 0)),
                  pl.BlockSpec((K, D), lambda i: (0, 0), pipeline_mode=pl.Buffered(1)), row, row, vec],
        out_specs=[row, row, vec, pl.BlockSpec((1, LANES), lambda i: (0, 0))],
        out_shape=[jax.ShapeDtypeStruct((S, D), F32), jax.ShapeDtypeStruct((S, D), BF16),
                   jax.ShapeDtypeStruct((1, D), F32), jax.ShapeDtypeStruct((1, LANES), F32)],
        operands=[ff, wd, h1, tgt, fw])


def _shift_down(x, d, head8):
    r = pltpu.roll(x, d, 0)
    rh = pltpu.roll(head8, d, 0)
    row8 = lax.broadcasted_iota(jnp.int32, head8.shape, 0)
    top = jnp.where(row8 < d, rh, r[0:8])
    return jnp.concatenate([top, r[8:]], axis=0)


def _shift_up(x, d, tail8):
    n = x.shape[0]
    r = pltpu.roll(x, n - d, 0)
    rt = pltpu.roll(tail8, 8 - d, 0)
    row8 = lax.broadcasted_iota(jnp.int32, tail8.shape, 0)
    bot = jnp.where(row8 + d >= 8, rt, r[n - 8:n])
    return jnp.concatenate([r[:n - 8], bot], axis=0)


def _roll_in_groups(x, d):
    n, c = x.shape
    return pltpu.roll(x.reshape(n // 8, 8, c), d, 1).reshape(n, c)


def _log_sigmoid(lam):
    z = jnp.exp(-jnp.abs(lam))
    u = 1.0 + z
    log1p = jnp.where(u == 1.0, z, jnp.log(u) * (z / jnp.where(u == 1.0, 1.0, u - 1.0)))
    return jnp.minimum(lam, 0.0) - log1p


def _neg_expm1(z, exp_z):
    series = -z * (1.0 + z * (0.5 + z * (1.0 / 6.0)))
    return jnp.where(z > -0.02, series, 1.0 - exp_z)


_GELU_C = 0.7978845608028654


def _gelu(x):
    t = jnp.tanh(_GELU_C * (x + 0.044715 * (x * x * x)))
    return x * (0.5 * (1.0 + t)), t


def _gelu_grad(x, t):
    return 0.5 * (1.0 + t) + 0.5 * x * (1.0 - t * t) * (_GELU_C * (1.0 + 3.0 * 0.044715 * (x * x)))


def _lru_gates(lx, head8, cw, cb, wa_ref, ba, wx_ref, bx, ls):
    nb = wa_ref.shape[0]
    sh = [lx] + [_shift_down(lx, d, head8) for d in (1, 2, 3)]
    cx = cb + sh[3] * cw[0:1]
    cx = cx + sh[2] * cw[1:2]
    cx = cx + sh[1] * cw[2:3]
    cx = cx + sh[0] * cw[3:4]
    cxb = cx.astype(BF16)
    ra = jnp.concatenate([_dot(cxb[:, n * HEAD_DIM:(n + 1) * HEAD_DIM], wa_ref[n]) for n in range(nb)], axis=1) + ba
    ia = jnp.concatenate([_dot(cxb[:, n * HEAD_DIM:(n + 1) * HEAD_DIM], wx_ref[n]) for n in range(nb)], axis=1) + bx
    r = jax.nn.sigmoid(ra)
    ig = jax.nn.sigmoid(ia)
    log_a = LRU_C * r * ls
    a = jnp.exp(log_a)
    m2 = _neg_expm1(2.0 * log_a, a * a)
    return sh, cx, cxb, r, ig, a, m2, jnp.sqrt(m2)


def _lru_specs(tl, DL):
    nb = DL // HEAD_DIM
    vec = pl.BlockSpec((1, DL), lambda i: (0, 0))
    return [pl.BlockSpec((CONV_W, DL), lambda i: (0, 0)), vec,
            pl.BlockSpec((nb, HEAD_DIM, HEAD_DIM), lambda i: (0, 0, 0)), vec,
            pl.BlockSpec((nb, HEAD_DIM, HEAD_DIM), lambda i: (0, 0, 0)), vec, vec]


def _lru_fwd(proj, cw, cb, wa, ba, wx, bx, lam, tl, d_mix, comm=None):
    S = proj.shape[0]
    DL = cb.shape[1]
    tl = _tile(S, tl)

    def body(lx_ref, lg_ref, cw_ref, cb_ref, wa_ref, ba_ref, wx_ref, bx_ref, lam_ref, h_ref, y_ref, prev8, hc, a_s, b_s):
        i = pl.program_id(0)

        @pl.when(i == 0)
        def _():
            prev8[...] = jnp.zeros_like(prev8)
            hc[...] = jnp.zeros_like(hc)

        lx = lx_ref[...]
        ls = _log_sigmoid(lam_ref[...])
        _, cx, _, _, ig, a, _, mult = _lru_gates(lx, prev8[...], cw_ref[...], cb_ref[...], wa_ref, ba_ref[...],
                                                 wx_ref, bx_ref[...], ls)
        b = mult * (ig * cx)
        row = lax.broadcasted_iota(jnp.int32, a.shape, 0) & 7
        for d in (1, 2, 4):
            a_sh = _roll_in_groups(a, d)
            b_sh = _roll_in_groups(b, d)
            m = row >= d
            b = jnp.where(m, a * b_sh + b, b)
            a = jnp.where(m, a * a_sh, a)
        a_s[...] = a
        b_s[...] = b

        def step(g, hprev):
            sl = pl.ds(pl.multiple_of(g * 8, 8), 8)
            hh = a_s[sl, :] * hprev + b_s[sl, :]
            h_ref[sl, :] = hh
            return hh[7:8, :]

        hc[0:1, :] = lax.fori_loop(0, tl // 8, step, hc[0:1, :])
        prev8[...] = lx[tl - 8:tl]
        g, _ = _gelu(lg_ref[...])
        y_ref[...] = (h_ref[...] * g).astype(BF16)

    return _pcall(
        body, name="lru_fwd", grid=(S // tl,),
        in_specs=[pl.BlockSpec((tl, DL), lambda i: (i, 0)), pl.BlockSpec((tl, DL), lambda i: (i, 1))] + _lru_specs(tl, DL),
        out_specs=[pl.BlockSpec((tl, DL), lambda i: (i, 0)), pl.BlockSpec((tl, DL), lambda i: (i, 0))],
        out_shape=[jax.ShapeDtypeStruct((S, DL), F32), jax.ShapeDtypeStruct((S, d_mix), BF16)],
        scratch_shapes=[pltpu.VMEM((8, DL), F32), pltpu.VMEM((8, DL), F32), pltpu.VMEM((tl, DL), F32), pltpu.VMEM((tl, DL), F32)],
        operands=[proj, proj, cw, cb, wa, ba, wx, bx, lam], comm=comm)


def _lru_bwd(proj, h, dy, cw, cb, wa, ba, wx, bx, lam, tl, comm=None):
    S = proj.shape[0]
    DL = cb.shape[1]
    nb = DL // HEAD_DIM
    tl = _tile(S, tl)
    nt = S // tl
    ng = tl // 8
    t8 = tl // 8

    def body(lx_ref, lxp_ref, lg_ref, h_ref, hp_ref, dy_ref, cw_ref, cb_ref, wa_ref, ba_ref, wx_ref, bx_ref, lam_ref,
             dlxg_ref, dcw_ref, dcb_ref, dwa_ref, dba_ref, dwx_ref, dbx_ref, dlam_ref,
             a_next, g_carry, dcx_next, an_s, dh_s, g_s):
        i = pl.program_id(0)

        @pl.when(i == 0)
        def _():
            for ref in (dcw_ref, dcb_ref, dwa_ref, dba_ref, dwx_ref, dbx_ref, dlam_ref, a_next, g_carry, dcx_next):
                ref[...] = jnp.zeros_like(ref)

        first = i == nt - 1
        lx = lx_ref[...]
        hv = h_ref[...]
        lg = lg_ref[...]
        dyv = dy_ref[...]
        head8 = jnp.where(first, 0.0, lxp_ref[...])
        hhead8 = jnp.where(first, 0.0, hp_ref[...])
        lamv = lam_ref[...]
        ls = _log_sigmoid(lamv)
        cwv = cw_ref[...]
        sh, cx, cxb, r, ig, a, m2, mult = _lru_gates(lx, head8, cwv, cb_ref[...], wa_ref, ba_ref[...], wx_ref, bx_ref[...],
                                                     ls)
        hprev = _shift_down(hv, 1, hhead8)
        g, t = _gelu(lg)
        dlg = dyv * hv * _gelu_grad(lg, t)
        dh = dyv * g
        an = _shift_up(a, 1, a_next[...])
        row = lax.broadcasted_iota(jnp.int32, a.shape, 0) & 7
        for d in (1, 2, 4):
            an_sh = _roll_in_groups(an, 8 - d)
            dh_sh = _roll_in_groups(dh, 8 - d)
            m = row + d < 8
            dh = jnp.where(m, an * dh_sh + dh, dh)
            an = jnp.where(m, an * an_sh, an)
        an_s[...] = an
        dh_s[...] = dh

        def step(k, gc):
            sl = pl.ds(pl.multiple_of((ng - 1 - k) * 8, 8), 8)
            gg = an_s[sl, :] * gc + dh_s[sl, :]
            g_s[sl, :] = gg
            return gg[0:1, :]

        g_carry[0:1, :] = lax.fori_loop(0, ng, step, g_carry[0:1, :])
        a_next[...] = a[0:8]
        G = g_s[...]
        da = G * hprev
        icx = ig * cx
        dmult = G * icx
        dicx = G * mult
        di = dicx * cx
        dcx = dicx * ig
        dlog = da * a - dmult * ((a * a) * lax.rsqrt(m2))
        dr = dlog * (LRU_C * ls)
        dlam_ref[...] += jnp.sum(dlog * (LRU_C * r), axis=0, keepdims=True)
        dra = dr * r * (1.0 - r)
        dia = di * ig * (1.0 - ig)
        dba_ref[...] += jnp.sum(dra, axis=0, keepdims=True)
        dbx_ref[...] += jnp.sum(dia, axis=0, keepdims=True)
        drab = dra.astype(BF16)
        diab = dia.astype(BF16)
        back = []
        for n in range(nb):
            cs = slice(n * HEAD_DIM, (n + 1) * HEAD_DIM)
            dwa_ref[n] += _dot(cxb[:, cs], drab[:, cs], TN)
            dwx_ref[n] += _dot(cxb[:, cs], diab[:, cs], TN)
            back.append(_dot(drab[:, cs], wa_ref[n], NT) + _dot(diab[:, cs], wx_ref[n], NT))
        dcx = dcx + jnp.concatenate(back, axis=1)
        dcb_ref[...] += jnp.sum(dcx, axis=0, keepdims=True)
        for tap in range(CONV_W):
            dcw_ref[tap:tap + 1, :] += jnp.sum(dcx * sh[CONV_W - 1 - tap], axis=0, keepdims=True)
        tail = dcx_next[...]
        dlx = dcx * cwv[3:4]
        for d in (1, 2, 3):
            dlx = dlx + _shift_up(dcx, d, tail) * cwv[3 - d:4 - d]
        dcx_next[...] = dcx[0:8]
        dlxg_ref[:, 0:DL] = dlx.astype(BF16)
        dlxg_ref[:, DL:2 * DL] = dlg.astype(BF16)

        @pl.when(i == nt - 1)
        def _():
            dlam_ref[...] = dlam_ref[...] * (1.0 - jax.nn.sigmoid(lamv))

    rev = lambda i: nt - 1 - i
    prev8_map = lambda i: (jnp.maximum((nt - 1 - i) * t8 - 1, 0), 0)
    vec = pl.BlockSpec((1, DL), lambda i: (0, 0))
    mat = pl.BlockSpec((nb, HEAD_DIM, HEAD_DIM), lambda i: (0, 0, 0))
    return _pcall(
        body, name="lru_bwd", grid=(nt,), operands=[proj, proj, proj, h, h, dy, cw, cb, wa, ba, wx, bx, lam], comm=comm,
        in_specs=[pl.BlockSpec((tl, DL), lambda i: (rev(i), 0)), pl.BlockSpec((8, DL), prev8_map),
                  pl.BlockSpec((tl, DL), lambda i: (rev(i), 1)),
                  pl.BlockSpec((tl, DL), lambda i: (rev(i), 0)), pl.BlockSpec((8, DL), prev8_map),
                  pl.BlockSpec((tl, DL), lambda i: (rev(i), 0))] + _lru_specs(tl, DL),
        out_specs=[pl.BlockSpec((tl, 2 * DL), lambda i: (rev(i), 0)), pl.BlockSpec((CONV_W, DL), lambda i: (0, 0)), vec,
                   mat, vec, mat, vec, vec],
        out_shape=[jax.ShapeDtypeStruct(proj.shape, BF16), jax.ShapeDtypeStruct((CONV_W, DL), F32),
                   jax.ShapeDtypeStruct((1, DL), F32), jax.ShapeDtypeStruct((nb, HEAD_DIM, HEAD_DIM), F32),
                   jax.ShapeDtypeStruct((1, DL), F32), jax.ShapeDtypeStruct((nb, HEAD_DIM, HEAD_DIM), F32),
                   jax.ShapeDtypeStruct((1, DL), F32), jax.ShapeDtypeStruct((1, DL), F32)],
        scratch_shapes=[pltpu.VMEM((8, DL), F32), pltpu.VMEM((8, DL), F32), pltpu.VMEM((8, DL), F32),
                        pltpu.VMEM((tl, DL), F32), pltpu.VMEM((tl, DL), F32), pltpu.VMEM((tl, DL), F32)])


def _ret_tables(S, H):
    pos = jnp.arange(S, dtype=F32)
    inv_freq = ROPE_BASE ** (-jnp.arange(0, HEAD_DIM, 2, dtype=F32) / HEAD_DIM)
    ang = pos[:, None] * inv_freq[None, :]
    cos, sin = jnp.cos(ang), jnp.sin(ang)
    cosf = jnp.concatenate([cos, cos], axis=1)
    sins = jnp.concatenate([-sin, sin], axis=1)
    log_gamma = jnp.log1p(-jnp.exp2(-5.0 - jnp.arange(H, dtype=F32)))
    idx = jnp.arange(CHUNK)
    diff = idx[:, None] - idx[None, :]
    causal = diff >= 0
    decay = jnp.where(causal[None], jnp.exp(log_gamma[:, None, None] * jnp.where(causal, diff, 0)[None].astype(F32)), 0.0)
    zeta = jnp.exp(log_gamma[:, None] * (CHUNK - 1 - idx).astype(F32)[None, :])
    xi = jnp.exp(log_gamma[:, None] * (idx + 1).astype(F32)[None, :])
    gc = jnp.exp(log_gamma * CHUNK)
    lanes = (H, CHUNK, HEAD_DIM)
    return (cosf, sins, decay, jnp.broadcast_to(zeta[:, :, None], lanes), jnp.broadcast_to(xi[:, :, None], lanes),
            jnp.broadcast_to(gc[:, None, None], lanes))


def _rope(t, cos, sin_signed):
    return t * cos + pltpu.roll(t, HEAD_DIM // 2, 1) * sin_signed


def _rope_t(d, cos, sin_signed):
    return d * cos + pltpu.roll(d * sin_signed, HEAD_DIM // 2, 1)


def _ret_const_specs(H, DR):
    full = pl.BlockSpec((H, CHUNK, HEAD_DIM), lambda *_: (0, 0, 0))
    return [full, full, full, full, pl.BlockSpec((1, DR), lambda *_: (0, 0))]


def _ret_fwd(proj, y, tables, gnw, tb, comm=None):
    S = proj.shape[0]
    DR = gnw.shape[1]
    H = DR // HEAD_DIM
    tb = _tile(S, tb, CHUNK)
    nc = tb // CHUNK
    cosf, sins, dm, zeta, xi, gc = tables
    scale = HEAD_DIM ** -0.5

    def body(qk_ref, vg_ref, cos_ref, sin_ref, dm_ref, zeta_ref, xi_ref, gc_ref, gnw_ref, y_in, y_ref, rprev_ref, r_s):
        del y_in
        i = pl.program_id(0)

        @pl.when(i == 0)
        def _():
            r_s[...] = jnp.zeros_like(r_s)

        def chunk(c, carry):
            rows = pl.ds(pl.multiple_of(c * CHUNK, CHUNK), CHUNK)
            cos = cos_ref[rows, :]
            sin = sin_ref[rows, :]
            heads = range(H)
            c0 = [slice(h * HEAD_DIM, (h + 1) * HEAD_DIM) for h in heads]
            c1 = [slice(DR + h * HEAD_DIM, DR + (h + 1) * HEAD_DIM) for h in heads]
            qh = [_rope(qk_ref[rows, c0[h]], cos, sin) for h in heads]
            kh = [_rope(qk_ref[rows, c1[h]], cos, sin) * scale for h in heads]
            vb = [vg_ref[rows, c0[h]].astype(BF16) for h in heads]
            rp = [r_s[h] for h in heads]
            rpb = [rp[h].astype(BF16) for h in heads]
            s = [_dot(qh[h].astype(BF16), kh[h].astype(BF16), NT) for h in heads]
            kv = [_dot((kh[h] * zeta_ref[h]).astype(BF16), vb[h], TN) for h in heads]
            cross = [_dot((qh[h] * xi_ref[h]).astype(BF16), rpb[h]) for h in heads]
            o = [_dot((s[h] * dm_ref[h]).astype(BF16), vb[h]) + cross[h] for h in heads]
            for h in heads:
                rprev_ref[c, h] = rpb[h]
                r_s[h] = rp[h] * gc_ref[h] + kv[h]
                mu = jnp.mean(o[h], axis=-1, keepdims=True)
                oc = o[h] - mu
                var = jnp.mean(oc * oc, axis=-1, keepdims=True)
                on = oc * lax.rsqrt(var + EPS) * gnw_ref[:, c0[h]]
                gate = vg_ref[rows, c1[h]]
                y_ref[rows, c0[h]] = (gate * jax.nn.sigmoid(gate) * on).astype(BF16)
            return carry

        lax.fori_loop(0, nc, chunk, 0)

    return _pcall(
        body, name="ret_fwd", grid=(S // tb,),
        in_specs=[pl.BlockSpec((tb, 2 * DR), lambda i: (i, 1)), pl.BlockSpec((tb, 2 * DR), lambda i: (i, 2)),
                  pl.BlockSpec((tb, HEAD_DIM), lambda i: (i, 0)), pl.BlockSpec((tb, HEAD_DIM), lambda i: (i, 0))]
        + _ret_const_specs(H, DR) + [HBM_SPEC],
        out_specs=[pl.BlockSpec((tb, DR), lambda i: (i, 1)),
                   pl.BlockSpec((nc, H, CHUNK, HEAD_DIM), lambda i: (i, 0, 0, 0))],
        out_shape=[jax.ShapeDtypeStruct(y.shape, BF16), jax.ShapeDtypeStruct((S // CHUNK, H, CHUNK, HEAD_DIM), BF16)],
        scratch_shapes=[pltpu.VMEM((H, CHUNK, HEAD_DIM), F32)], aliases={9: 0},
        operands=[proj, proj, cosf, sins, dm, zeta, xi, gc, gnw, y], comm=comm)


def _ret_bwd(proj, rprev, dy, dproj, tables, gnw, tb, comm=None):
    S = proj.shape[0]
    DR = gnw.shape[1]
    H = DR // HEAD_DIM
    tb = _tile(S, tb, CHUNK)
    nc = tb // CHUNK
    nt = S // tb
    cosf, sins, dm, zeta, xi, gc = tables
    scale = HEAD_DIM ** -0.5

    def body(qk_ref, vg_ref, cos_ref, sin_ref, dm_ref, zeta_ref, xi_ref, gc_ref, gnw_ref, rprev_ref, dy_ref, dp_in,
             dp_ref, dgn_ref, dr_s, dqk_s, dvg_s, out_sems):
        del dp_in
        i = pl.program_id(0)
        slot = i % 2

        def out_copies(step, sl):
            rows = pl.ds(pl.multiple_of((nt - 1 - step) * tb, tb), tb)
            return (pltpu.make_async_copy(dqk_s.at[sl], dp_ref.at[rows, pl.ds(2 * DR, 2 * DR)], out_sems.at[sl, 0]),
                    pltpu.make_async_copy(dvg_s.at[sl], dp_ref.at[rows, pl.ds(4 * DR, 2 * DR)], out_sems.at[sl, 1]))

        @pl.when(i == 0)
        def _():
            dr_s[...] = jnp.zeros_like(dr_s)
            dgn_ref[...] = jnp.zeros_like(dgn_ref)

        @pl.when(i >= 2)
        def _():
            for cp in out_copies(i - 2, slot):
                cp.wait()

        def chunk(cc, carry):
            c = nc - 1 - cc
            rows = pl.ds(pl.multiple_of(c * CHUNK, CHUNK), CHUNK)
            cos = cos_ref[rows, :]
            sin = sin_ref[rows, :]
            heads = range(H)
            c0 = [slice(h * HEAD_DIM, (h + 1) * HEAD_DIM) for h in heads]
            c1 = [slice(DR + h * HEAD_DIM, DR + (h + 1) * HEAD_DIM) for h in heads]
            qh = [_rope(qk_ref[rows, c0[h]], cos, sin) for h in heads]
            kh = [_rope(qk_ref[rows, c1[h]], cos, sin) * scale for h in heads]
            qb = [t.astype(BF16) for t in qh]
            kb = [t.astype(BF16) for t in kh]
            vb = [vg_ref[rows, c0[h]].astype(BF16) for h in heads]
            rpb = [rprev_ref[c, h] for h in heads]
            qx = [(qh[h] * xi_ref[h]).astype(BF16) for h in heads]
            kz = [(kh[h] * zeta_ref[h]).astype(BF16) for h in heads]
            drh = [dr_s[h] for h in heads]
            drb = [t.astype(BF16) for t in drh]
            s = [_dot(qb[h], kb[h], NT) for h in heads]
            cross = [_dot(qx[h], rpb[h]) for h in heads]
            dv_state = [_dot(kz[h], drb[h]) for h in heads]
            dk_state = [_dot(vb[h], drb[h], NT) for h in heads]
            sb = [(s[h] * dm_ref[h]).astype(BF16) for h in heads]
            o = [_dot(sb[h], vb[h]) + cross[h] for h in heads]
            dob = []
            for h in heads:
                mu = jnp.mean(o[h], axis=-1, keepdims=True)
                oc = o[h] - mu
                rstd = lax.rsqrt(jnp.mean(oc * oc, axis=-1, keepdims=True) + EPS)
                ohat = oc * rstd
                gw = gnw_ref[:, c0[h]]
                gate = vg_ref[rows, c1[h]]
                sg = jax.nn.sigmoid(gate)
                dyv = dy_ref[rows, c0[h]]
                dvg_s[slot, rows, c1[h]] = (dyv * (ohat * gw) * (sg * (1.0 + gate * (1.0 - sg)))).astype(BF16)
                don = dyv * (gate * sg)
                dgn_ref[:, c0[h]] += jnp.sum(don * ohat, axis=0, keepdims=True)
                dohat = don * gw
                do = rstd * (dohat - jnp.mean(dohat, axis=-1, keepdims=True)
                             - ohat * jnp.mean(dohat * ohat, axis=-1, keepdims=True))
                dob.append(do.astype(BF16))
            ds = [_dot(dob[h], vb[h], NT) for h in heads]
            dq_state = [_dot(dob[h], rpb[h], NT) for h in heads]
            dv = [_dot(sb[h], dob[h], TN) + dv_state[h] for h in heads]
            dr_new = [_dot(qx[h], dob[h], TN) for h in heads]
            dsb = [(ds[h] * dm_ref[h]).astype(BF16) for h in heads]
            dqh = [_dot(dsb[h], kb[h]) + dq_state[h] * xi_ref[h] for h in heads]
            dkh = [_dot(dsb[h], qb[h], TN) + dk_state[h] * zeta_ref[h] for h in heads]
            for h in heads:
                dr_s[h] = drh[h] * gc_ref[h] + dr_new[h]
                dqk_s[slot, rows, c0[h]] = _rope_t(dqh[h], cos, sin).astype(BF16)
                dqk_s[slot, rows, c1[h]] = _rope_t(dkh[h] * scale, cos, sin).astype(BF16)
                dvg_s[slot, rows, c0[h]] = dv[h].astype(BF16)
            return carry

        lax.fori_loop(0, nc, chunk, 0)
        for cp in out_copies(i, slot):
            cp.start()

        @pl.when(i == nt - 1)
        def _():
            if nt >= 2:
                for cp in out_copies(i - 1, 1 - slot):
                    cp.wait()
            for cp in out_copies(i, slot):
                cp.wait()

    rev = lambda i: nt - 1 - i
    return _pcall(
        body, name="ret_bwd", grid=(nt,), aliases={11: 0}, comm=comm,
        operands=[proj, proj, cosf, sins, dm, zeta, xi, gc, gnw, rprev, dy, dproj],
        in_specs=[pl.BlockSpec((tb, 2 * DR), lambda i: (rev(i), 1)), pl.BlockSpec((tb, 2 * DR), lambda i: (rev(i), 2)),
                  pl.BlockSpec((tb, HEAD_DIM), lambda i: (rev(i), 0)), pl.BlockSpec((tb, HEAD_DIM), lambda i: (rev(i), 0))]
        + _ret_const_specs(H, DR)
        + [pl.BlockSpec((nc, H, CHUNK, HEAD_DIM), lambda i: (rev(i), 0, 0, 0)),
           pl.BlockSpec((tb, DR), lambda i: (rev(i), 1)), HBM_SPEC],
        out_specs=[HBM_SPEC, pl.BlockSpec((1, DR), lambda i: (0, 0))],
        out_shape=[jax.ShapeDtypeStruct(dproj.shape, BF16), jax.ShapeDtypeStruct((1, DR), F32)],
        scratch_shapes=[pltpu.VMEM((H, CHUNK, HEAD_DIM), F32), pltpu.VMEM((2, tb, 2 * DR), BF16),
                        pltpu.VMEM((2, tb, 2 * DR), BF16), pltpu.SemaphoreType.DMA((2, 2))])


def _place():
    x, y, c = lax.axis_index("x"), lax.axis_index("y"), lax.axis_index("c")
    chips = [(1 - x, y), (x, 1 - y), (1 - x, 1 - y)]
    return x, y, c, chips


def _own_slab(name, shard, place):
    R, C = shard.shape
    tr = _row_tile(R, C)
    return _ew("cast_" + name, lambda a: (a,), [(shard, pl.BlockSpec((tr, C), lambda i, p: (i, 0)))],
               [(jax.ShapeDtypeStruct((4, R, C), BF16), pl.BlockSpec((None, tr, C), lambda i, p: (p[1], i, 0)))],
               (R // tr,), sp=place)[0]


class _remote:
    def __init__(self, src, dst, ssem, rsem, k, to):
        self.args = dict(src_ref=src, dst_ref=dst, send_sem=ssem.at[k], recv_sem=rsem.at[k], device_id=to,
                         device_id_type=MESH)

    def start(self):
        pltpu.make_async_remote_copy(**self.args).start()

    def wait_send(self):
        pltpu.make_async_remote_copy(**self.args).wait_send()

    def wait_recv(self):
        pltpu.make_async_remote_copy(**self.args).wait_recv()


def _task_fns(copies):
    def start(cins, couts, ssem, rsem, base):
        for cp in copies(cins, couts, ssem, rsem, base)[0]:
            cp.start()

    def finish(cins, couts, ssem, rsem, base):
        sends, recvs = copies(cins, couts, ssem, rsem, base)
        for cp in sends:
            cp.wait_send()
        for cp in recvs:
            cp.wait_recv()

    return start, finish


NEIGHBOURS, DIAGONAL = (0, 1), (2,)


def _gather_ici(st, which=NEIGHBOURS + DIAGONAL):
    r2 = st.shape[1] // 2

    def copies(cins, couts, ssem, rsem, base):
        x, y, c, chips = _place()
        out = couts[0]
        mine = out.at[2 * x + y, pl.ds(c * r2, r2), :]
        sends, recvs = [], []
        for k, j in enumerate(which):
            cx, cy = chips[j]
            got = out.at[2 * cx + cy, pl.ds(c * r2, r2), :]
            sends.append(_remote(mine, mine, ssem, rsem, base + k, (cx, cy, c)))
            recvs.append(_remote(got, got, ssem, rsem, base + k, (x, y, c)))
        return sends, recvs

    start, finish = _task_fns(copies)
    return _Comm([st], [jax.ShapeDtypeStruct(st.shape, st.dtype)], {0: 0}, len(which), start, finish)


def _gather_d2d(st):
    r2 = st.shape[1] // 2

    def copies(cins, couts, ssem, rsem, base):
        x, y, c, chips = _place()
        out = couts[0]
        sends, recvs = [], []
        for j, (cx, cy) in enumerate(chips):
            have = out.at[2 * cx + cy, pl.ds(c * r2, r2), :]
            want = out.at[2 * cx + cy, pl.ds((1 - c) * r2, r2), :]
            sends.append(_remote(have, have, ssem, rsem, base + j, (x, y, 1 - c)))
            recvs.append(_remote(want, want, ssem, rsem, base + j, (x, y, c)))
        return sends, recvs

    start, finish = _task_fns(copies)
    return _Comm([st], [jax.ShapeDtypeStruct(st.shape, st.dtype)], {0: 0}, 3, start, finish)


def _gather_conv(conv_w):
    def copies(cins, couts, ssem, rsem, base):
        x, y, c, chips = _place()
        src, out = cins[0], couts[0]
        sends = [_remote(src, out.at[2 * x + y], ssem, rsem, base + j, (*chip, c)) for j, chip in enumerate(chips)]
        recvs = [_remote(src, out.at[2 * cx + cy], ssem, rsem, base + j, (x, y, c)) for j, (cx, cy) in enumerate(chips)]
        return sends, recvs

    start, finish = _task_fns(copies)
    return _Comm([conv_w], [jax.ShapeDtypeStruct((4,) + conv_w.shape, conv_w.dtype)], {}, 3, start, finish)


def _pair_exchange(g):
    r2 = g.shape[1] // 2

    def copies(cins, couts, ssem, rsem, base):
        x, y, c, _ = _place()
        cp = _remote(cins[0].at[:, pl.ds((1 - c) * r2, r2), :], couts[0], ssem, rsem, base, (x, y, 1 - c))
        return [cp], [cp]

    start, finish = _task_fns(copies)
    return _Comm([g], [jax.ShapeDtypeStruct((g.shape[0], r2, g.shape[2]), g.dtype)], {}, 1, start, finish)


def _chip_exchange(part):
    def copies(cins, couts, ssem, rsem, base):
        x, y, c, chips = _place()
        cps = [_remote(cins[0].at[2 * cx + cy], couts[0].at[j], ssem, rsem, base + j, (cx, cy, c))
               for j, (cx, cy) in enumerate(chips)]
        return cps, cps

    start, finish = _task_fns(copies)
    return _Comm([part], [jax.ShapeDtypeStruct((3,) + part.shape[1:], part.dtype)], {}, 3, start, finish)


def _pair_share(slot):
    def copies(cins, couts, ssem, rsem, base):
        x, y, c, _ = _place()
        out = couts[0]
        return ([_remote(out.at[c], out.at[c], ssem, rsem, base, (x, y, 1 - c))],
                [_remote(out.at[1 - c], out.at[1 - c], ssem, rsem, base, (x, y, c))])

    start, finish = _task_fns(copies)
    return _Comm([slot], [jax.ShapeDtypeStruct(slot.shape, slot.dtype)], {0: 0}, 1, start, finish)


def _gather_small(sm):
    flips = [(fx, fy, fc) for fx in (0, 1) for fy in (0, 1) for fc in (0, 1)][1:]

    def copies(cins, couts, ssem, rsem, base):
        x, y, c, _ = _place()
        src, out = cins[0], couts[0]
        peers = [(1 - x if fx else x, 1 - y if fy else y, 1 - c if fc else c) for fx, fy, fc in flips]
        sends = [_remote(src, out.at[4 * x + 2 * y + c], ssem, rsem, base + k, peer) for k, peer in enumerate(peers)]
        recvs = [_remote(src, out.at[4 * px + 2 * py + pc], ssem, rsem, base + k, (x, y, c))
                 for k, (px, py, pc) in enumerate(peers)]
        return sends, recvs

    start, finish = _task_fns(copies)
    return _Comm([sm], [jax.ShapeDtypeStruct((8,) + sm.shape, sm.dtype)], {}, 7, start, finish)


def _comm_call(name, tasks):
    task = _merge(tasks)
    nci = len(task.ins)

    def body(*refs):
        cins, couts, (ssem, rsem) = refs[:nci], refs[nci:nci + len(task.outs)], refs[nci + len(task.outs):]
        task.start(cins, couts, ssem, rsem, 0)
        task.finish(cins, couts, ssem, rsem, 0)

    return pl.pallas_call(
        body, in_specs=[HBM_SPEC] * nci, out_specs=[HBM_SPEC] * len(task.outs), out_shape=list(task.outs),
        scratch_shapes=[pltpu.SemaphoreType.DMA((task.n_sem,)), pltpu.SemaphoreType.DMA((task.n_sem,))],
        input_output_aliases=task.aliases, name=name)(*task.ins)


def _adamw(w, g, m, v):
    m = ADAM_B1 * m + (1.0 - ADAM_B1) * g
    v = ADAM_B2 * v + (1.0 - ADAM_B2) * (g * g)
    m_hat = m / (1.0 - ADAM_B1 ** ADAM_STEP)
    v_hat = v / (1.0 - ADAM_B2 ** ADAM_STEP)
    delta = -ADAM_LR * (m_hat / (jnp.sqrt(v_hat) + ADAM_EPS) + ADAM_WD * w)
    return delta, m, v


def _adamw_call(name, w, g, m, v):
    R, C = w.shape
    tr = _row_tile(R, C, 1024 * 1024)
    row = pl.BlockSpec((tr, C), lambda i: (i, 0))
    o = jax.ShapeDtypeStruct((R, C), F32)
    return _ew(name, lambda w_, g_, m_, v_: (*_adamw(w_, g_, m_, v_), g_), [(w, row), (g, row), (m, row), (v, row)],
               [(o, row), (o, row), (o, row), (o, row)], (R // tr,))


def _pair_sum(name, g, ra, place):
    _, R, C = g.shape
    r2 = R // 2
    tr = _row_tile(r2, C)
    nb = r2 // tr
    own = pl.BlockSpec((None, tr, C), lambda j, i, p: (j, p[0] * nb + i, 0))
    blk = pl.BlockSpec((None, tr, C), lambda j, i, p: (j, i, 0))
    return _ew("rs_pair_sum_" + name, lambda a, b: (a + b,), [(g, own), (ra, blk)],
               [(jax.ShapeDtypeStruct((4, r2, C), BF16), blk)], (4, nb), sp=place)[0]


def _chip_sum(name, g, ra, rb, place):
    _, R, C = g.shape
    r2 = R // 2
    tr = _row_tile(r2, C)
    nb = r2 // tr
    own = pl.BlockSpec((None, tr, C), lambda i, p: (p[1], p[0] * nb + i, 0))
    mine = pl.BlockSpec((None, tr, C), lambda i, p: (p[1], i, 0))
    src = [pl.BlockSpec((None, tr, C), functools.partial(lambda i, p, j: (j, i, 0), j=j)) for j in range(3)]
    out = pl.BlockSpec((None, tr, C), lambda i, p: (p[0], i, 0))

    def total(a, b, r0, r1, r2_):
        return ((((a + b) + r0.astype(F32)) + r1.astype(F32)) + r2_.astype(F32),)

    return _ew("rs_chip_sum_" + name, total, [(g, own), (ra, mine), (rb, src[0]), (rb, src[1]), (rb, src[2])],
               [(jax.ShapeDtypeStruct((2, r2, C), F32), out)], (nb,), sp=place)[0]


def _pack(arrays):
    rows, offs, pos = [], [], 0
    for a in arrays:
        flat = a.reshape(-1)
        n = -(-flat.shape[0] // (8 * LANES)) * (8 * LANES)
        if n != flat.shape[0]:
            flat = jnp.pad(flat, (0, n - flat.shape[0]))
        rows.append(flat.reshape(-1, LANES))
        offs.append(pos)
        pos += n // LANES
    return jnp.concatenate(rows, axis=0), offs


def _unpack(packed, offs, shapes):
    out = []
    for off, shp in zip(offs, shapes):
        n = 1
        for s in shp:
            n *= s
        out.append(packed[off:off + -(-n // LANES)].reshape(-1)[:n].reshape(shp))
    return out


def _sum8(gathered):
    _, R, C = gathered.shape
    tr = _row_tile(R, C, 256 * 1024)
    specs = [pl.BlockSpec((None, tr, C), functools.partial(lambda i, d: (d, i, 0), d=d)) for d in range(8)]

    def fn(*parts):
        t = parts[0]
        for p in parts[1:]:
            t = t + p
        return (t,)

    return _ew("small_sum", fn, [(gathered, s) for s in specs],
               [(jax.ShapeDtypeStruct((R, C), F32), pl.BlockSpec((tr, C), lambda i: (i, 0)))], (R // tr,))[0]


BIG = ("w_in", "w_out", "w_ffn_gate", "w_ffn_up", "w_ffn_down")
SMALL = ("ln1_w", "conv_w", "conv_b", "gate_a_w", "gate_a_b", "gate_x_w", "gate_x_b", "lru_lambda", "ret_gn_w", "ln2_w",
         "final_norm_w")
WEIGHTS = ("ln1_w", "w_in", "conv_w", "conv_b", "gate_a_w", "gate_a_b", "gate_x_w", "gate_x_b", "lru_lambda", "ret_gn_w",
           "w_out", "ln2_w", "w_ffn_gate", "w_ffn_up", "w_ffn_down", "final_norm_w")


def kernel(x, ln1_w, w_in, conv_w, conv_b, gate_a_w, gate_a_b, gate_x_w, gate_x_b, lru_lambda, ret_gn_w, w_out, ln2_w, w_ffn_gate, w_ffn_up, w_ffn_down, final_norm_w, loss_target, m_ln1_w, m_w_in, m_conv_w, m_conv_b, m_gate_a_w, m_gate_a_b, m_gate_x_w, m_gate_x_b, m_lru_lambda, m_ret_gn_w, m_w_out, m_ln2_w, m_w_ffn_gate, m_w_ffn_up, m_w_ffn_down, m_final_norm_w, v_ln1_w, v_w_in, v_conv_w, v_conv_b, v_gate_a_w, v_gate_a_b, v_gate_x_w, v_gate_x_b, v_lru_lambda, v_ret_gn_w, v_w_out, v_ln2_w, v_w_ffn_gate, v_w_ffn_up, v_w_ffn_down, v_final_norm_w):
    w = dict(ln1_w=ln1_w, w_in=w_in, conv_w=conv_w, conv_b=conv_b, gate_a_w=gate_a_w, gate_a_b=gate_a_b, gate_x_w=gate_x_w,
             gate_x_b=gate_x_b, lru_lambda=lru_lambda, ret_gn_w=ret_gn_w, w_out=w_out, ln2_w=ln2_w, w_ffn_gate=w_ffn_gate,
             w_ffn_up=w_ffn_up, w_ffn_down=w_ffn_down, final_norm_w=final_norm_w)
    m = dict(ln1_w=m_ln1_w, w_in=m_w_in, conv_w=m_conv_w, conv_b=m_conv_b, gate_a_w=m_gate_a_w, gate_a_b=m_gate_a_b,
             gate_x_w=m_gate_x_w, gate_x_b=m_gate_x_b, lru_lambda=m_lru_lambda, ret_gn_w=m_ret_gn_w, w_out=m_w_out,
             ln2_w=m_ln2_w, w_ffn_gate=m_w_ffn_gate, w_ffn_up=m_w_ffn_up, w_ffn_down=m_w_ffn_down,
             final_norm_w=m_final_norm_w)
    v = dict(ln1_w=v_ln1_w, w_in=v_w_in, conv_w=v_conv_w, conv_b=v_conv_b, gate_a_w=v_gate_a_w, gate_a_b=v_gate_a_b,
             gate_x_w=v_gate_x_w, gate_x_b=v_gate_x_b, lru_lambda=v_lru_lambda, ret_gn_w=v_ret_gn_w, w_out=v_w_out,
             ln2_w=v_ln2_w, w_ffn_gate=v_w_ffn_gate, w_ffn_up=v_w_ffn_up, w_ffn_down=v_w_ffn_down,
             final_norm_w=v_final_norm_w)
    xs, tgt = x[0], loss_target[0]
    S, D = xs.shape
    DL, DR = conv_b.shape[1], ret_gn_w.shape[1]
    assert DL == DR and DL % HEAD_DIM == 0 and S % CHUNK == 0
    d_mix = DL + DR
    cx, cy, cc = lax.axis_index("x"), lax.axis_index("y"), lax.axis_index("c")
    chip = 2 * cx + cy
    place = jnp.stack([cc, chip]).astype(jnp.int32)
    grad, delta, new_m, new_v = {}, {}, {}, {}

    def finish_big(n, full):
        shp = w[n].shape
        g2 = full.reshape(shp[1], shp[2])
        w2, m2, v2 = (t[n].reshape(shp[1], shp[2]) for t in (w, m, v))
        d_, m_, v_, g_ = _adamw_call("adamw_" + n, w2, g2, m2, v2)
        grad[n], delta[n], new_m[n], new_v[n] = (t.reshape(shp) for t in (g_, d_, m_, v_))

    def all_sum(gathered, own):
        return _sum8(lax.dynamic_update_slice(gathered, own[None], (4 * cx + 2 * cy + cc, 0, 0)))

    st = {n: _own_slab(n, w[n][0], place) for n in BIG}
    TM, TK = 512, 2048
    (u1,), (w_in_st,) = _rms_fwd("rms1", xs, ln1_w, TM, comm=_gather_ici(st["w_in"]))
    w_in_st, conv_st = _comm_call("gather_w_in", [_gather_d2d(w_in_st), _gather_conv(conv_w[0])])
    conv_st = lax.dynamic_update_slice(conv_st, conv_w, (chip, 0, 0))
    cw_cols = conv_st.shape[2]
    conv_full = jnp.transpose(conv_st, (1, 0, 2)).reshape(CONV_W, 4 * cw_cols)
    n_in, n_ff = w_in_st.shape[2], st["w_ffn_gate"].shape[2]
    tables = _ret_tables(S, DR // HEAD_DIM)
    wab, wxb = gate_a_w[0].astype(BF16), gate_x_w[0].astype(BF16)
    lru_w = (conv_full, conv_b, wab, gate_a_b, wxb, gate_x_b, lru_lambda)

    proj, (w_out_st, wg_st) = _mm_nn_stacked("proj", u1, w_in_st, F32, 2 * TM,
                                             comm=_merge([_gather_ici(st["w_out"]), _gather_ici(st["w_ffn_gate"])]))
    (hs, y), (w_out_st, wg_st, wu_st) = _lru_fwd(
        proj, *lru_w, 128, d_mix, comm=_merge([_gather_d2d(w_out_st), _gather_d2d(wg_st), _gather_ici(st["w_ffn_up"])]))
    (y, rprev), (wu_st, wd_st) = _ret_fwd(proj, y, tables, ret_gn_w, 256,
                                          comm=_merge([_gather_d2d(wu_st), _gather_ici(st["w_ffn_down"], NEIGHBOURS)]))
    w_out_f = w_out_st.reshape(d_mix, D)
    (h1, u2), (wd_st,) = _out_proj_rms(y, w_out_f, xs, ln2_w, TM, comm=_gather_ici(wd_st, DIAGONAL))
    (dg_fac, du_fac, ff), (wd_st,) = _ffn_gate_up(u2, wg_st, wu_st, TM, comm=_gather_d2d(wd_st))
    wd_f = wd_st.reshape(4 * n_ff, D)
    dh2, dh2b, d_fw, loss = _ffn_down_loss(ff, wd_f, h1, tgt, final_norm_w.reshape(1, D), 256)

    g_wd = _mm_tn("g_w_down", ff, dh2b, n_ff, 1024, TK).reshape(4, n_ff, D)
    (dgt, dup), (ra_wd,) = _ffn_gate_up_bwd(dh2b, wd_f, dg_fac, du_fac, 2 * TM, n_ff, comm=_pair_exchange(g_wd))
    pb_wd = _pair_sum("w_ffn_down", g_wd, ra_wd, place)
    g_wg, (rb_wd,) = _mm_tn("g_w_gate", u2, dgt, 1024, None, TK, stacked_cols=n_ff, comm=_chip_exchange(pb_wd))
    slot_wd = _chip_sum("w_ffn_down", g_wd, ra_wd, rb_wd, place)
    g_wu, (full_wd, ra_wg) = _mm_tn("g_w_up", u2, dup, 1024, None, TK, stacked_cols=n_ff,
                                    comm=_merge([_pair_share(slot_wd), _pair_exchange(g_wg)]))
    finish_big("w_ffn_down", full_wd)
    pb_wg = _pair_sum("w_ffn_gate", g_wg, ra_wg, place)
    du2, (rb_wg,) = _mm_nt_stacked("d_u2_gate", dgt, wg_st, TM, comm=_chip_exchange(pb_wg))
    du2, (ra_wu,) = _mm_nt_stacked("d_u2_up", dup, wu_st, TM, res=du2, comm=_pair_exchange(g_wu))
    slot_wg = _chip_sum("w_ffn_gate", g_wg, ra_wg, rb_wg, place)
    pb_wu = _pair_sum("w_ffn_up", g_wu, ra_wu, place)
    (dh1, dh1b, dy, d_ln2), (full_wg,) = _rms_bwd_dy(h1, ln2_w, du2, dh2, w_out_f, 256, comm=_pair_share(slot_wg))
    finish_big("w_ffn_gate", full_wg)
    g_wout = _mm_tn("g_w_out", y, dh1b, 1024, 1024, TK).reshape(4, d_mix // 4, D)
    (dproj, d_cw, d_cb, d_wa, d_ba, d_wx, d_bx, d_lam), (rb_wu, ra_wout) = _lru_bwd(
        proj, hs, dy, *lru_w, 128, comm=_merge([_chip_exchange(pb_wu), _pair_exchange(g_wout)]))
    slot_wu = _chip_sum("w_ffn_up", g_wu, ra_wu, rb_wu, place)
    pb_wout = _pair_sum("w_out", g_wout, ra_wout, place)
    (dproj, d_gn), (full_wu, rb_wout) = _ret_bwd(proj, rprev, dy, dproj, tables, ret_gn_w, 256,
                                                 comm=_merge([_pair_share(slot_wu), _chip_exchange(pb_wout)]))
    finish_big("w_ffn_up", full_wu)
    slot_wout = _chip_sum("w_out", g_wout, ra_wout, rb_wout, place)
    small = dict(conv_w=d_cw, conv_b=d_cb, gate_a_w=d_wa, gate_a_b=d_ba, gate_x_w=d_wx, gate_x_b=d_bx, lru_lambda=d_lam,
                 ret_gn_w=d_gn, ln2_w=d_ln2, final_norm_w=d_fw)
    packed, offs = _pack([small[n] for n in SMALL[1:]] + [loss])
    g_win, (full_wout, got_small) = _mm_tn("g_w_in", u1, dproj, 1024, None, TK, stacked_cols=n_in,
                                           comm=_merge([_pair_share(slot_wout), _gather_small(packed)]))
    finish_big("w_out", full_wout)
    (ra_win,) = _comm_call("rs_pair_w_in", [_pair_exchange(g_win)])
    pb_win = _pair_sum("w_in", g_win, ra_win, place)
    du1, (rb_win,) = _mm_nt_stacked("d_u1", dproj, w_in_st, TM, comm=_chip_exchange(pb_win))
    slot_win = _chip_sum("w_in", g_win, ra_win, rb_win, place)
    gx, d_ln1 = _rms_bwd("rms1_bwd", xs, ln1_w, du1, dh1, TM)
    packed1, _ = _pack([d_ln1])
    full_win, got_ln1 = _comm_call("reduce_tail", [_pair_share(slot_win), _gather_small(packed1)])
    finish_big("w_in", full_win)

    red = _unpack(all_sum(got_small, packed), offs, [small[n].shape for n in SMALL[1:]] + [(1, LANES)])
    g = dict(zip(SMALL[1:], red[:-1]))
    g["ln1_w"] = all_sum(got_ln1, packed1)[:-(-D // LANES)].reshape(1, D)
    loss_out = red[-1][0, 0]
    g["conv_w"] = lax.dynamic_slice(g["conv_w"], (0, chip * cw_cols), (CONV_W, cw_cols))
    packs = [_pack([t[n] for n in SMALL])[0] for t in (w, m, v)]
    gp, offs2 = _pack([g[n] for n in SMALL])
    outs = _adamw_call("adamw_small", packs[0], gp, packs[1], packs[2])
    shapes = [w[n].shape for n in SMALL]
    for dst, arr in zip((delta, new_m, new_v), outs):
        dst.update(zip(SMALL, _unpack(arr, offs2, shapes)))
    for n in SMALL:
        grad[n] = g[n].reshape(w[n].shape)

    return (loss_out, gx.reshape(x.shape), *[grad[n] for n in WEIGHTS], *[delta[n] for n in WEIGHTS],
            *[new_m[n] for n in WEIGHTS], *[new_v[n] for n in WEIGHTS])
```

```python
import functools

import jax
import jax.numpy as jnp
from jax import lax
from jax.experimental import pallas as pl
from jax.experimental.pallas import tpu as pltpu

F32 = jnp.float32
BF16 = jnp.bfloat16
MESH = pl.DeviceIdType.MESH

EPS = 1e-6
LRU_C = 8.0
ROPE_BASE = 10000.0
CHUNK = 128
HEAD_DIM = 128
CONV_W = 4
ADAM_LR = 0.001
ADAM_B1 = 0.9
ADAM_B2 = 0.999
ADAM_EPS = 1e-08
ADAM_WD = 0.01
ADAM_STEP = 10

V7X_VMEM_BYTES = 64 * 1024 * 1024
VMEM_LIMIT = V7X_VMEM_BYTES - 8 * 1024 * 1024
LANES = 128
SUBLANES_16BIT = 16

NN = (((1,), (0,)), ((), ()))
NT = (((1,), (1,)), ((), ()))
TN = (((0,), (0,)), ((), ()))


def _dot(a, b, dims=NN):
    return lax.dot_general(a, b, dims, preferred_element_type=F32)


def _tile(n, pref, mult=SUBLANES_16BIT):
    best = None
    t = mult
    while t <= min(n, pref):
        if n % t == 0:
            best = t
        t += mult
    return best if best is not None else n


def _row_tile(rows, cols, budget_bytes=2 * 1024 * 1024):
    return _tile(rows, max(SUBLANES_16BIT, budget_bytes // (cols * 4)))


def _params(sem):
    return pltpu.CompilerParams(dimension_semantics=sem, vmem_limit_bytes=VMEM_LIMIT)


HBM_SPEC = pl.BlockSpec(memory_space=pl.ANY)


class _Comm:
    def __init__(self, ins, outs, aliases, n_sem, start, finish):
        self.ins, self.outs, self.aliases, self.n_sem, self.start, self.finish = ins, outs, aliases, n_sem, start, finish


def _merge(tasks):
    ins, outs, aliases, plans, n_sem = [], [], {}, [], 0
    for t in tasks:
        i0, o0 = len(ins), len(outs)
        plans.append((t, i0, o0, n_sem))
        ins += t.ins
        outs += t.outs
        aliases.update({i0 + a: o0 + b for a, b in t.aliases.items()})
        n_sem += t.n_sem

    def run(which):
        def go(cins, couts, ssem, rsem, base):
            for t, i0, o0, s0 in plans:
                getattr(t, which)(cins[i0:i0 + len(t.ins)], couts[o0:o0 + len(t.outs)], ssem, rsem, base + s0)
        return go

    return _Comm(ins, outs, aliases, n_sem, run("start"), run("finish"))


def _pcall(body, *, name, grid, in_specs, out_specs, out_shape, operands, scratch_shapes=(), aliases=None, comm=None):
    n_in, n_out, n_scr = len(operands), len(out_shape), len(scratch_shapes)
    aliases = dict(aliases or {})
    params = _params(("arbitrary",) * len(grid))
    if comm is None:
        return pl.pallas_call(body, grid=grid, in_specs=list(in_specs), out_specs=list(out_specs), out_shape=list(out_shape),
                              scratch_shapes=list(scratch_shapes), input_output_aliases=aliases, name=name,
                              compiler_params=params)(*operands)
    nci, nco = len(comm.ins), len(comm.outs)

    def wrapped(*refs):
        ins, cins = refs[:n_in], refs[n_in:n_in + nci]
        o0 = n_in + nci
        outs, couts = refs[o0:o0 + n_out], refs[o0 + n_out:o0 + n_out + nco]
        s0 = o0 + n_out + nco
        scr, (ssem, rsem) = refs[s0:s0 + n_scr], refs[s0 + n_scr:]
        ids = [pl.program_id(a) for a in range(len(grid))]
        first = functools.reduce(jnp.logical_and, [i == 0 for i in ids])
        last = functools.reduce(jnp.logical_and, [i == g - 1 for i, g in zip(ids, grid)])

        @pl.when(first)
        def _():
            comm.start(cins, couts, ssem, rsem, 0)

        body(*ins, *outs, *scr)

        @pl.when(last)
        def _():
            comm.finish(cins, couts, ssem, rsem, 0)

    aliases.update({n_in + a: n_out + b for a, b in comm.aliases.items()})
    res = pl.pallas_call(
        wrapped, grid=grid, in_specs=list(in_specs) + [HBM_SPEC] * nci, out_specs=list(out_specs) + [HBM_SPEC] * nco,
        out_shape=list(out_shape) + list(comm.outs),
        scratch_shapes=list(scratch_shapes) + [pltpu.SemaphoreType.DMA((comm.n_sem,)), pltpu.SemaphoreType.DMA((comm.n_sem,))],
        input_output_aliases=aliases, name=name, compiler_params=params)(*operands, *comm.ins)
    return res[:n_out], res[n_out:]


def _ew(name, fn, ins, outs, grid, sp=None):
    n_in = len(ins)

    def body(*refs):
        if sp is not None:
            refs = refs[1:]
        vals = [r[...] for r in refs[:n_in]]
        res = fn(*vals)
        for o_ref, v in zip(refs[n_in:], res):
            o_ref[...] = v.astype(o_ref.dtype)

    in_specs = [s for _, s in ins]
    out_specs = [s for _, s in outs]
    out_shape = [s for s, _ in outs]
    sem = ("arbitrary",) * len(grid)
    if sp is None:
        return pl.pallas_call(body, grid=grid, in_specs=in_specs, out_specs=out_specs, out_shape=out_shape,
                              name=name, compiler_params=_params(sem))(*[a for a, _ in ins])
    gs = pltpu.PrefetchScalarGridSpec(num_scalar_prefetch=1, grid=grid, in_specs=in_specs, out_specs=out_specs)
    return pl.pallas_call(body, grid_spec=gs, out_shape=out_shape, name=name,
                          compiler_params=_params(sem))(sp, *[a for a, _ in ins])


def _matmul(name, pairs, dims, grid, out_shape, out_spec, acc_shape, res=None, comm=None):
    n = len(pairs)
    nk = grid[2]

    def body(*refs):
        ab = refs[:2 * n]
        pos = 2 * n
        res_ref = None
        if res is not None:
            res_ref = refs[pos]
            pos += 1
        o_ref = refs[pos]
        acc_ref = refs[pos + 1] if nk > 1 else None

        def partial():
            t = None
            for p in range(n):
                d = _dot(ab[2 * p][...], ab[2 * p + 1][...], dims)
                t = d if t is None else t + d
            return t

        def finish(t):
            if res_ref is not None:
                t = t + res_ref[...]
            o_ref[...] = t.astype(o_ref.dtype)

        if nk == 1:
            finish(partial())
        else:
            k = pl.program_id(2)

            @pl.when(k == 0)
            def _():
                acc_ref[...] = partial()

            @pl.when(k > 0)
            def _():
                acc_ref[...] += partial()

            @pl.when(k == nk - 1)
            def _():
                finish(acc_ref[...])

    operands, in_specs = [], []
    for a, a_spec, b, b_spec in pairs:
        operands += [a, b]
        in_specs += [a_spec, b_spec]
    if res is not None:
        operands.append(res[0])
        in_specs.append(res[1])
    scratch = [pltpu.VMEM(acc_shape, F32)] if nk > 1 else []
    res = _pcall(body, name=name, grid=grid, in_specs=in_specs, out_specs=[out_spec], out_shape=[out_shape],
                 operands=operands, scratch_shapes=scratch, comm=comm)
    return res[0] if comm is None else (res[0][0], res[1])


def _mm_nn_stacked(name, a, b_st, out_dtype, tm, comm=None):
    M, K = a.shape
    J, _, Nj = b_st.shape
    tm = _tile(M, tm)
    return _matmul(
        name, [(a, pl.BlockSpec((tm, K), lambda j, i, k: (i, 0)), b_st, pl.BlockSpec((None, K, Nj), lambda j, i, k: (j, 0, 0)))],
        NN, (J, M // tm, 1), jax.ShapeDtypeStruct((M, J * Nj), out_dtype), pl.BlockSpec((tm, Nj), lambda j, i, k: (i, j)), None,
        comm=comm)


def _mm_nt_stacked(name, a, b_st, tm, res=None, comm=None):
    M = a.shape[0]
    J, N, Nj = b_st.shape
    tm = _tile(M, tm)

    def body(a_ref, b_ref, *rest):
        o_ref = rest[-1]
        t = None if res is None else rest[0][...]
        for s in range(J):
            d = _dot(a_ref[:, s * Nj:(s + 1) * Nj], b_ref[s], NT)
            t = d if t is None else t + d
        o_ref[...] = t

    row = pl.BlockSpec((tm, N), lambda i: (i, 0))
    out = _pcall(body, name=name, grid=(M // tm,),
                 in_specs=[pl.BlockSpec((tm, J * Nj), lambda i: (i, 0)),
                           pl.BlockSpec((J, N, Nj), lambda i: (0, 0, 0), pipeline_mode=pl.Buffered(1))] + [row] * (res is not None),
                 out_specs=[row], out_shape=[jax.ShapeDtypeStruct((M, N), F32)],
                 operands=[a, b_st] + [res] * (res is not None), comm=comm)
    return out[0] if comm is None else (out[0][0], out[1])


MXU_COLUMNS = 256


def _col_blocks(n):
    return [slice(s, min(s + MXU_COLUMNS, n)) for s in range(0, n, MXU_COLUMNS)]


def _ffn_gate_up(u2, wg_st, wu_st, tm, comm=None):
    S, D = u2.shape
    J, _, Nj = wg_st.shape
    tm = _tile(S, tm)

    def body(a_ref, wg_ref, wu_ref, dg_ref, du_ref, ff_ref):
        a = a_ref[...]
        blocks = _col_blocks(Nj)
        ahead = (_dot(a, wg_ref[:, blocks[0]]), _dot(a, wu_ref[:, blocks[0]]))
        for j, cols in enumerate(blocks):
            g, u = ahead
            if j + 1 < len(blocks):
                ahead = (_dot(a, wg_ref[:, blocks[j + 1]]), _dot(a, wu_ref[:, blocks[j + 1]]))
            sg = jax.nn.sigmoid(g)
            silu = g * sg
            dg_ref[:, cols] = (u * (sg * (1.0 + g * (1.0 - sg)))).astype(BF16)
            du_ref[:, cols] = silu.astype(BF16)
            ff_ref[:, cols] = (silu * u).astype(BF16)

    w_spec = pl.BlockSpec((None, D, Nj), lambda j, i: (j, 0, 0), pipeline_mode=pl.Buffered(1))
    o_spec = pl.BlockSpec((tm, Nj), lambda j, i: (i, j))
    o = jax.ShapeDtypeStruct((S, J * Nj), BF16)
    return _pcall(body, name="ffn_gate_up", grid=(J, S // tm),
                  in_specs=[pl.BlockSpec((tm, D), lambda j, i: (i, 0)), w_spec, w_spec],
                  out_specs=[o_spec, o_spec, o_spec], out_shape=[o, o, o], operands=[u2, wg_st, wu_st], comm=comm)


def _ffn_gate_up_bwd(dh2b, wd, dg_fac, du_fac, tm, tn, comm=None):
    S, D = dh2b.shape
    F = wd.shape[0]
    tm, tn = _tile(S, tm), _tile(F, tn, LANES)

    def body(a_ref, wd_ref, dg_ref, du_ref, dgt_ref, dup_ref):
        a = a_ref[...]
        for cols in _col_blocks(tn):
            d = _dot(a, wd_ref[cols, :], NT)
            dgt_ref[:, cols] = (d * dg_ref[:, cols].astype(F32)).astype(BF16)
            dup_ref[:, cols] = (d * du_ref[:, cols].astype(F32)).astype(BF16)

    blk = pl.BlockSpec((tm, tn), lambda j, i: (i, j))
    o = jax.ShapeDtypeStruct((S, F), BF16)
    return _pcall(body, name="ffn_gate_up_bwd", grid=(F // tn, S // tm),
                  in_specs=[pl.BlockSpec((tm, D), lambda j, i: (i, 0)), pl.BlockSpec((tn, D), lambda j, i: (j, 0)), blk, blk],
                  out_specs=[blk, blk], out_shape=[o, o], operands=[dh2b, wd, dg_fac, du_fac], comm=comm)


def _mm_tn(name, a, b, tmo, tn, tk, stacked_cols=None, comm=None):
    S, Mo = a.shape
    N = b.shape[1]
    tmo, tk = _tile(Mo, tmo, LANES), _tile(S, tk)
    if stacked_cols is None:
        tn = _tile(N, tn, LANES)
        out_shape = jax.ShapeDtypeStruct((Mo, N), F32)
        out_spec = pl.BlockSpec((tmo, tn), lambda i, j, k: (i, j))
    else:
        tn = stacked_cols
        out_shape = jax.ShapeDtypeStruct((N // tn, Mo, tn), F32)
        out_spec = pl.BlockSpec((None, tmo, tn), lambda i, j, k: (j, i, 0))
    return _matmul(
        name, [(a, pl.BlockSpec((tk, tmo), lambda i, j, k: (k, i)), b, pl.BlockSpec((tk, tn), lambda i, j, k: (k, j)))],
        TN, (Mo // tmo, N // tn, S // tk), out_shape, out_spec, (tmo, tn), comm=comm)


def _rms_fwd(name, x, w, tm, comm=None):
    S, D = x.shape
    tm = _tile(S, tm)

    def body(x_ref, w_ref, o_ref):
        xv = x_ref[...]
        r = lax.rsqrt(jnp.mean(xv * xv, axis=-1, keepdims=True) + EPS)
        o_ref[...] = ((xv * r) * w_ref[...]).astype(BF16)

    row = pl.BlockSpec((tm, D), lambda i: (i, 0))
    return _pcall(body, name=name, grid=(S // tm,), in_specs=[row, pl.BlockSpec((1, D), lambda i: (0, 0))], out_specs=[row],
                  out_shape=[jax.ShapeDtypeStruct((S, D), BF16)], operands=[x, w], comm=comm)


def _rms_bwd(name, x, w, dy, dres, tm, comm=None):
    S, D = x.shape
    tm = _tile(S, tm)

    def body(x_ref, w_ref, dy_ref, dres_ref, dx_ref, dw_ref):
        i = pl.program_id(0)

        @pl.when(i == 0)
        def _():
            dw_ref[...] = jnp.zeros_like(dw_ref)

        xv = x_ref[...]
        r = lax.rsqrt(jnp.mean(xv * xv, axis=-1, keepdims=True) + EPS)
        nv = xv * r
        dyv = dy_ref[...]
        dn = dyv * w_ref[...]
        dw_ref[...] += jnp.sum(dyv * nv, axis=0, keepdims=True)
        dx = dres_ref[...] + r * (dn - nv * jnp.mean(dn * nv, axis=-1, keepdims=True))
        dx_ref[...] = dx

    row = pl.BlockSpec((tm, D), lambda i: (i, 0))
    vec = pl.BlockSpec((1, D), lambda i: (0, 0))
    return _pcall(body, name=name, grid=(S // tm,), in_specs=[row, vec, row, row], out_specs=[row, vec],
                  out_shape=[jax.ShapeDtypeStruct((S, D), F32), jax.ShapeDtypeStruct((1, D), F32)],
                  operands=[x, w, dy, dres], comm=comm)


def _rms_bwd_dy(h1, w, du2, dh2, w_out, tm, comm=None):
    S, D = h1.shape
    d_mix = w_out.shape[0]
    tm = _tile(S, tm)

    def body(x_ref, w_ref, dy_ref, dres_ref, wo_ref, dx_ref, dxb_ref, out_ref, dw_ref):
        i = pl.program_id(0)

        @pl.when(i == 0)
        def _():
            dw_ref[...] = jnp.zeros_like(dw_ref)

        xv = x_ref[...]
        r = lax.rsqrt(jnp.mean(xv * xv, axis=-1, keepdims=True) + EPS)
        nv = xv * r
        dyv = dy_ref[...]
        dn = dyv * w_ref[...]
        dw_ref[...] += jnp.sum(dyv * nv, axis=0, keepdims=True)
        dx = dres_ref[...] + r * (dn - nv * jnp.mean(dn * nv, axis=-1, keepdims=True))
        dx_ref[...] = dx
        dxb = dx.astype(BF16)
        dxb_ref[...] = dxb
        out_ref[...] = _dot(dxb, wo_ref[...], NT)

    row = pl.BlockSpec((tm, D), lambda i: (i, 0))
    vec = pl.BlockSpec((1, D), lambda i: (0, 0))
    return _pcall(
        body, name="rms2_bwd_dy", grid=(S // tm,),
        in_specs=[row, vec, row, row, pl.BlockSpec((d_mix, D), lambda i: (0, 0), pipeline_mode=pl.Buffered(1))],
        out_specs=[row, row, pl.BlockSpec((tm, d_mix), lambda i: (i, 0)), vec],
        out_shape=[jax.ShapeDtypeStruct((S, D), F32), jax.ShapeDtypeStruct((S, D), BF16),
                   jax.ShapeDtypeStruct((S, d_mix), F32), jax.ShapeDtypeStruct((1, D), F32)],
        operands=[h1, w, du2, dh2, w_out], comm=comm)


def _out_proj_rms(y, w_out, x, ln_w, tm, comm=None):
    S, K = y.shape
    D = w_out.shape[1]
    tm = _tile(S, tm)

    def body(a_ref, w_ref, x_ref, lw_ref, h_ref, u_ref):
        hv = _dot(a_ref[...], w_ref[...]) + x_ref[...]
        h_ref[...] = hv
        r = lax.rsqrt(jnp.mean(hv * hv, axis=-1, keepdims=True) + EPS)
        u_ref[...] = ((hv * r) * lw_ref[...]).astype(BF16)

    row = pl.BlockSpec((tm, D), lambda i: (i, 0))
    return _pcall(
        body, name="out_proj", grid=(S // tm,),
        in_specs=[pl.BlockSpec((tm, K), lambda i: (i, 0)),
                  pl.BlockSpec((K, D), lambda i: (0, 0), pipeline_mode=pl.Buffered(1)), row,
                  pl.BlockSpec((1, D), lambda i: (0, 0))],
        out_specs=[row, row], out_shape=[jax.ShapeDtypeStruct((S, D), F32), jax.ShapeDtypeStruct((S, D), BF16)],
        operands=[y, w_out, x, ln_w], comm=comm)


def _ffn_down_loss(ff, wd, h1, tgt, fw, tm):
    S, K = ff.shape
    D = wd.shape[1]
    tm = _tile(S, tm)

    def body(a_ref, wd_ref, h1_ref, t_ref, w_ref, dh_ref, dhb_ref, dw_ref, loss_ref):
        i = pl.program_id(0)

        @pl.when(i == 0)
        def _():
            dw_ref[...] = jnp.zeros_like(dw_ref)
            loss_ref[...] = jnp.zeros_like(loss_ref)

        hv = _dot(a_ref[...], wd_ref[...]) + h1_ref[...]
        wv = w_ref[...]
        r = lax.rsqrt(jnp.mean(hv * hv, axis=-1, keepdims=True) + EPS)
        nv = hv * r
        err = nv * wv - t_ref[...]
        row_loss = jnp.mean(err * err, axis=-1, keepdims=True)
        loss_ref[...] += 0.5 * jnp.sum(row_loss, axis=0, keepdims=True)
        dyo = err * (1.0 / D)
        dn = dyo * wv
        dw_ref[...] += jnp.sum(dyo * nv, axis=0, keepdims=True)
        dh = r * (dn - nv * jnp.mean(dn * nv, axis=-1, keepdims=True))
        dh_ref[...] = dh
        dhb_ref[...] = dh.astype(BF16)

    row = pl.BlockSpec((tm, D), lambda i: (i, 0))
    vec = pl.BlockSpec((1, D), lambda i: (0, 0))
    return _pcall(
        body, name="ffn_down_loss", grid=(S // tm,),
        in_specs=[pl.BlockSpec((tm, K), lambda i: (i, 0)),
                  pl.BlockSpec((K, D), lambda i: (0, 0), pipeline_mode=pl.Buffered(1)), row, row, vec],
        out_specs=[row, row, vec, pl.BlockSpec((1, LANES), lambda i: (0, 0))],
        out_shape=[jax.ShapeDtypeStruct((S, D), F32), jax.ShapeDtypeStruct((S, D), BF16),
                   jax.ShapeDtypeStruct((1, D), F32), jax.ShapeDtypeStruct((1, LANES), F32)],
        operands=[ff, wd, h1, tgt, fw])


def _shift_down(x, d, head8):
    r = pltpu.roll(x, d, 0)
    rh = pltpu.roll(head8, d, 0)
    row8 = lax.broadcasted_iota(jnp.int32, head8.shape, 0)
    top = jnp.where(row8 < d, rh, r[0:8])
    return jnp.concatenate([top, r[8:]], axis=0)


def _shift_up(x, d, tail8):
    n = x.shape[0]
    r = pltpu.roll(x, n - d, 0)
    rt = pltpu.roll(tail8, 8 - d, 0)
    row8 = lax.broadcasted_iota(jnp.int32, tail8.shape, 0)
    bot = jnp.where(row8 + d >= 8, rt, r[n - 8:n])
    return jnp.concatenate([r[:n - 8], bot], axis=0)


def _roll_in_groups(x, d):
    n, c = x.shape
    return pltpu.roll(x.reshape(n // 8, 8, c), d, 1).reshape(n, c)


def _log_sigmoid(lam):
    z = jnp.exp(-jnp.abs(lam))
    u = 1.0 + z
    log1p = jnp.where(u == 1.0, z, jnp.log(u) * (z / jnp.where(u == 1.0, 1.0, u - 1.0)))
    return jnp.minimum(lam, 0.0) - log1p


def _neg_expm1(z, exp_z):
    series = -z * (1.0 + z * (0.5 + z * (1.0 / 6.0)))
    return jnp.where(z > -0.02, series, 1.0 - exp_z)


_GELU_C = 0.7978845608028654


def _gelu(x):
    t = jnp.tanh(_GELU_C * (x + 0.044715 * (x * x * x)))
    return x * (0.5 * (1.0 + t)), t


def _gelu_grad(x, t):
    return 0.5 * (1.0 + t) + 0.5 * x * (1.0 - t * t) * (_GELU_C * (1.0 + 3.0 * 0.044715 * (x * x)))


def _lru_gates(lx, head8, cw, cb, wa_ref, ba, wx_ref, bx, ls):
    nb = wa_ref.shape[0]
    sh = [lx] + [_shift_down(lx, d, head8) for d in (1, 2, 3)]
    cx = cb + sh[3] * cw[0:1]
    cx = cx + sh[2] * cw[1:2]
    cx = cx + sh[1] * cw[2:3]
    cx = cx + sh[0] * cw[3:4]
    cxb = cx.astype(BF16)
    ra = jnp.concatenate([_dot(cxb[:, n * HEAD_DIM:(n + 1) * HEAD_DIM], wa_ref[n]) for n in range(nb)], axis=1) + ba
    ia = jnp.concatenate([_dot(cxb[:, n * HEAD_DIM:(n + 1) * HEAD_DIM], wx_ref[n]) for n in range(nb)], axis=1) + bx
    r = jax.nn.sigmoid(ra)
    ig = jax.nn.sigmoid(ia)
    log_a = LRU_C * r * ls
    a = jnp.exp(log_a)
    m2 = _neg_expm1(2.0 * log_a, a * a)
    return sh, cx, cxb, r, ig, a, m2, jnp.sqrt(m2)


def _lru_specs(tl, DL):
    nb = DL // HEAD_DIM
    vec = pl.BlockSpec((1, DL), lambda i: (0, 0))
    return [pl.BlockSpec((CONV_W, DL), lambda i: (0, 0)), vec,
            pl.BlockSpec((nb, HEAD_DIM, HEAD_DIM), lambda i: (0, 0, 0)), vec,
            pl.BlockSpec((nb, HEAD_DIM, HEAD_DIM), lambda i: (0, 0, 0)), vec, vec]


def _lru_fwd(proj, cw, cb, wa, ba, wx, bx, lam, tl, d_mix, comm=None):
    S = proj.shape[0]
    DL = cb.shape[1]
    tl = _tile(S, tl)

    def body(lx_ref, lg_ref, cw_ref, cb_ref, wa_ref, ba_ref, wx_ref, bx_ref, lam_ref, h_ref, y_ref, prev8, hc, a_s, b_s):
        i = pl.program_id(0)

        @pl.when(i == 0)
        def _():
            prev8[...] = jnp.zeros_like(prev8)
            hc[...] = jnp.zeros_like(hc)

        lx = lx_ref[...]
        ls = _log_sigmoid(lam_ref[...])
        _, cx, _, _, ig, a, _, mult = _lru_gates(lx, prev8[...], cw_ref[...], cb_ref[...], wa_ref, ba_ref[...],
                                                 wx_ref, bx_ref[...], ls)
        b = mult * (ig * cx)
        row = lax.broadcasted_iota(jnp.int32, a.shape, 0) & 7
        for d in (1, 2, 4):
            a_sh = _roll_in_groups(a, d)
            b_sh = _roll_in_groups(b, d)
            m = row >= d
            b = jnp.where(m, a * b_sh + b, b)
            a = jnp.where(m, a * a_sh, a)
        a_s[...] = a
        b_s[...] = b

        def step(g, hprev):
            sl = pl.ds(pl.multiple_of(g * 8, 8), 8)
            hh = a_s[sl, :] * hprev + b_s[sl, :]
            h_ref[sl, :] = hh
            return hh[7:8, :]

        hc[0:1, :] = lax.fori_loop(0, tl // 8, step, hc[0:1, :])
        prev8[...] = lx[tl - 8:tl]
        g, _ = _gelu(lg_ref[...])
        y_ref[...] = (h_ref[...] * g).astype(BF16)

    return _pcall(
        body, name="lru_fwd", grid=(S // tl,),
        in_specs=[pl.BlockSpec((tl, DL), lambda i: (i, 0)), pl.BlockSpec((tl, DL), lambda i: (i, 1))] + _lru_specs(tl, DL),
        out_specs=[pl.BlockSpec((tl, DL), lambda i: (i, 0)), pl.BlockSpec((tl, DL), lambda i: (i, 0))],
        out_shape=[jax.ShapeDtypeStruct((S, DL), F32), jax.ShapeDtypeStruct((S, d_mix), BF16)],
        scratch_shapes=[pltpu.VMEM((8, DL), F32), pltpu.VMEM((8, DL), F32), pltpu.VMEM((tl, DL), F32), pltpu.VMEM((tl, DL), F32)],
        operands=[proj, proj, cw, cb, wa, ba, wx, bx, lam], comm=comm)


def _lru_bwd(proj, h, dy, cw, cb, wa, ba, wx, bx, lam, tl, comm=None):
    S = proj.shape[0]
    DL = cb.shape[1]
    nb = DL // HEAD_DIM
    tl = _tile(S, tl)
    nt = S // tl
    ng = tl // 8
    t8 = tl // 8

    def body(lx_ref, lxp_ref, lg_ref, h_ref, hp_ref, dy_ref, cw_ref, cb_ref, wa_ref, ba_ref, wx_ref, bx_ref, lam_ref,
             dlxg_ref, dcw_ref, dcb_ref, dwa_ref, dba_ref, dwx_ref, dbx_ref, dlam_ref,
             a_next, g_carry, dcx_next, an_s, dh_s, g_s):
        i = pl.program_id(0)

        @pl.when(i == 0)
        def _():
            for ref in (dcw_ref, dcb_ref, dwa_ref, dba_ref, dwx_ref, dbx_ref, dlam_ref, a_next, g_carry, dcx_next):
                ref[...] = jnp.zeros_like(ref)

        first = i == nt - 1
        lx = lx_ref[...]
        hv = h_ref[...]
        lg = lg_ref[...]
        dyv = dy_ref[...]
        head8 = jnp.where(first, 0.0, lxp_ref[...])
        hhead8 = jnp.where(first, 0.0, hp_ref[...])
        lamv = lam_ref[...]
        ls = _log_sigmoid(lamv)
        cwv = cw_ref[...]
        sh, cx, cxb, r, ig, a, m2, mult = _lru_gates(lx, head8, cwv, cb_ref[...], wa_ref, ba_ref[...], wx_ref, bx_ref[...],
                                                     ls)
        hprev = _shift_down(hv, 1, hhead8)
        g, t = _gelu(lg)
        dlg = dyv * hv * _gelu_grad(lg, t)
        dh = dyv * g
        an = _shift_up(a, 1, a_next[...])
        row = lax.broadcasted_iota(jnp.int32, a.shape, 0) & 7
        for d in (1, 2, 4):
            an_sh = _roll_in_groups(an, 8 - d)
            dh_sh = _roll_in_groups(dh, 8 - d)
            m = row + d < 8
            dh = jnp.where(m, an * dh_sh + dh, dh)
            an = jnp.where(m, an * an_sh, an)
        an_s[...] = an
        dh_s[...] = dh

        def step(k, gc):
            sl = pl.ds(pl.multiple_of((ng - 1 - k) * 8, 8), 8)
            gg = an_s[sl, :] * gc + dh_s[sl, :]
            g_s[sl, :] = gg
            return gg[0:1, :]

        g_carry[0:1, :] = lax.fori_loop(0, ng, step, g_carry[0:1, :])
        a_next[...] = a[0:8]
        G = g_s[...]
        da = G * hprev
        icx = ig * cx
        dmult = G * icx
        dicx = G * mult
        di = dicx * cx
        dcx = dicx * ig
        dlog = da * a - dmult * ((a * a) * lax.rsqrt(m2))
        dr = dlog * (LRU_C * ls)
        dlam_ref[...] += jnp.sum(dlog * (LRU_C * r), axis=0, keepdims=True)
        dra = dr * r * (1.0 - r)
        dia = di * ig * (1.0 - ig)
        dba_ref[...] += jnp.sum(dra, axis=0, keepdims=True)
        dbx_ref[...] += jnp.sum(dia, axis=0, keepdims=True)
        drab = dra.astype(BF16)
        diab = dia.astype(BF16)
        back = []
        for n in range(nb):
            cs = slice(n * HEAD_DIM, (n + 1) * HEAD_DIM)
            dwa_ref[n] += _dot(cxb[:, cs], drab[:, cs], TN)
            dwx_ref[n] += _dot(cxb[:, cs], diab[:, cs], TN)
            back.append(_dot(drab[:, cs], wa_ref[n], NT) + _dot(diab[:, cs], wx_ref[n], NT))
        dcx = dcx + jnp.concatenate(back, axis=1)
        dcb_ref[...] += jnp.sum(dcx, axis=0, keepdims=True)
        for tap in range(CONV_W):
            dcw_ref[tap:tap + 1, :] += jnp.sum(dcx * sh[CONV_W - 1 - tap], axis=0, keepdims=True)
        tail = dcx_next[...]
        dlx = dcx * cwv[3:4]
        for d in (1, 2, 3):
            dlx = dlx + _shift_up(dcx, d, tail) * cwv[3 - d:4 - d]
        dcx_next[...] = dcx[0:8]
        dlxg_ref[:, 0:DL] = dlx.astype(BF16)
        dlxg_ref[:, DL:2 * DL] = dlg.astype(BF16)

        @pl.when(i == nt - 1)
        def _():
            dlam_ref[...] = dlam_ref[...] * (1.0 - jax.nn.sigmoid(lamv))

    rev = lambda i: nt - 1 - i
    prev8_map = lambda i: (jnp.maximum((nt - 1 - i) * t8 - 1, 0), 0)
    vec = pl.BlockSpec((1, DL), lambda i: (0, 0))
    mat = pl.BlockSpec((nb, HEAD_DIM, HEAD_DIM), lambda i: (0, 0, 0))
    return _pcall(
        body, name="lru_bwd", grid=(nt,), operands=[proj, proj, proj, h, h, dy, cw, cb, wa, ba, wx, bx, lam], comm=comm,
        in_specs=[pl.BlockSpec((tl, DL), lambda i: (rev(i), 0)), pl.BlockSpec((8, DL), prev8_map),
                  pl.BlockSpec((tl, DL), lambda i: (rev(i), 1)),
                  pl.BlockSpec((tl, DL), lambda i: (rev(i), 0)), pl.BlockSpec((8, DL), prev8_map),
                  pl.BlockSpec((tl, DL), lambda i: (rev(i), 0))] + _lru_specs(tl, DL),
        out_specs=[pl.BlockSpec((tl, 2 * DL), lambda i: (rev(i), 0)), pl.BlockSpec((CONV_W, DL), lambda i: (0, 0)), vec,
                   mat, vec, mat, vec, vec],
        out_shape=[jax.ShapeDtypeStruct(proj.shape, BF16), jax.ShapeDtypeStruct((CONV_W, DL), F32),
                   jax.ShapeDtypeStruct((1, DL), F32), jax.ShapeDtypeStruct((nb, HEAD_DIM, HEAD_DIM), F32),
                   jax.ShapeDtypeStruct((1, DL), F32), jax.ShapeDtypeStruct((nb, HEAD_DIM, HEAD_DIM), F32),
                   jax.ShapeDtypeStruct((1, DL), F32), jax.ShapeDtypeStruct((1, DL), F32)],
        scratch_shapes=[pltpu.VMEM((8, DL), F32), pltpu.VMEM((8, DL), F32), pltpu.VMEM((8, DL), F32),
                        pltpu.VMEM((tl, DL), F32), pltpu.VMEM((tl, DL), F32), pltpu.VMEM((tl, DL), F32)])


def _ret_tables(S, H):
    pos = jnp.arange(S, dtype=F32)
    inv_freq = ROPE_BASE ** (-jnp.arange(0, HEAD_DIM, 2, dtype=F32) / HEAD_DIM)
    ang = pos[:, None] * inv_freq[None, :]
    cos, sin = jnp.cos(ang), jnp.sin(ang)
    cosf = jnp.concatenate([cos, cos], axis=1)
    sins = jnp.concatenate([-sin, sin], axis=1)
    log_gamma = jnp.log1p(-jnp.exp2(-5.0 - jnp.arange(H, dtype=F32)))
    idx = jnp.arange(CHUNK)
    diff = idx[:, None] - idx[None, :]
    causal = diff >= 0
    decay = jnp.where(causal[None], jnp.exp(log_gamma[:, None, None] * jnp.where(causal, diff, 0)[None].astype(F32)), 0.0)
    zeta = jnp.exp(log_gamma[:, None] * (CHUNK - 1 - idx).astype(F32)[None, :])
    xi = jnp.exp(log_gamma[:, None] * (idx + 1).astype(F32)[None, :])
    gc = jnp.exp(log_gamma * CHUNK)
    lanes = (H, CHUNK, HEAD_DIM)
    return (cosf, sins, decay, jnp.broadcast_to(zeta[:, :, None], lanes), jnp.broadcast_to(xi[:, :, None], lanes),
            jnp.broadcast_to(gc[:, None, None], lanes))


def _rope(t, cos, sin_signed):
    return t * cos + pltpu.roll(t, HEAD_DIM // 2, 1) * sin_signed


def _rope_t(d, cos, sin_signed):
    return d * cos + pltpu.roll(d * sin_signed, HEAD_DIM // 2, 1)


def _ret_const_specs(H, DR):
    full = pl.BlockSpec((H, CHUNK, HEAD_DIM), lambda *_: (0, 0, 0))
    return [full, full, full, full, pl.BlockSpec((1, DR), lambda *_: (0, 0))]


def _ret_fwd(proj, y, tables, gnw, tb, comm=None):
    S = proj.shape[0]
    DR = gnw.shape[1]
    H = DR // HEAD_DIM
    tb = _tile(S, tb, CHUNK)
    nc = tb // CHUNK
    cosf, sins, dm, zeta, xi, gc = tables
    scale = HEAD_DIM ** -0.5

    def body(qk_ref, vg_ref, cos_ref, sin_ref, dm_ref, zeta_ref, xi_ref, gc_ref, gnw_ref, y_in, y_ref, rprev_ref, r_s):
        del y_in
        i = pl.program_id(0)

        @pl.when(i == 0)
        def _():
            r_s[...] = jnp.zeros_like(r_s)

        def chunk(c, carry):
            rows = pl.ds(pl.multiple_of(c * CHUNK, CHUNK), CHUNK)
            cos = cos_ref[rows, :]
            sin = sin_ref[rows, :]
            heads = range(H)
            c0 = [slice(h * HEAD_DIM, (h + 1) * HEAD_DIM) for h in heads]
            c1 = [slice(DR + h * HEAD_DIM, DR + (h + 1) * HEAD_DIM) for h in heads]
            qh = [_rope(qk_ref[rows, c0[h]], cos, sin) for h in heads]
            kh = [_rope(qk_ref[rows, c1[h]], cos, sin) * scale for h in heads]
            vb = [vg_ref[rows, c0[h]].astype(BF16) for h in heads]
            rp = [r_s[h] for h in heads]
            rpb = [rp[h].astype(BF16) for h in heads]
            s = [_dot(qh[h].astype(BF16), kh[h].astype(BF16), NT) for h in heads]
            kv = [_dot((kh[h] * zeta_ref[h]).astype(BF16), vb[h], TN) for h in heads]
            cross = [_dot((qh[h] * xi_ref[h]).astype(BF16), rpb[h]) for h in heads]
            o = [_dot((s[h] * dm_ref[h]).astype(BF16), vb[h]) + cross[h] for h in heads]
            for h in heads:
                rprev_ref[c, h] = rpb[h]
                r_s[h] = rp[h] * gc_ref[h] + kv[h]
                mu = jnp.mean(o[h], axis=-1, keepdims=True)
                oc = o[h] - mu
                var = jnp.mean(oc * oc, axis=-1, keepdims=True)
                on = oc * lax.rsqrt(var + EPS) * gnw_ref[:, c0[h]]
                gate = vg_ref[rows, c1[h]]
                y_ref[rows, c0[h]] = (gate * jax.nn.sigmoid(gate) * on).astype(BF16)
            return carry

        lax.fori_loop(0, nc, chunk, 0)

    return _pcall(
        body, name="ret_fwd", grid=(S // tb,),
        in_specs=[pl.BlockSpec((tb, 2 * DR), lambda i: (i, 1)), pl.BlockSpec((tb, 2 * DR), lambda i: (i, 2)),
                  pl.BlockSpec((tb, HEAD_DIM), lambda i: (i, 0)), pl.BlockSpec((tb, HEAD_DIM), lambda i: (i, 0))]
        + _ret_const_specs(H, DR) + [HBM_SPEC],
        out_specs=[pl.BlockSpec((tb, DR), lambda i: (i, 1)),
                   pl.BlockSpec((nc, H, CHUNK, HEAD_DIM), lambda i: (i, 0, 0, 0))],
        out_shape=[jax.ShapeDtypeStruct(y.shape, BF16), jax.ShapeDtypeStruct((S // CHUNK, H, CHUNK, HEAD_DIM), BF16)],
        scratch_shapes=[pltpu.VMEM((H, CHUNK, HEAD_DIM), F32)], aliases={9: 0},
        operands=[proj, proj, cosf, sins, dm, zeta, xi, gc, gnw, y], comm=comm)


def _ret_bwd(proj, rprev, dy, dproj, tables, gnw, tb, comm=None):
    S = proj.shape[0]
    DR = gnw.shape[1]
    H = DR // HEAD_DIM
    tb = _tile(S, tb, CHUNK)
    nc = tb // CHUNK
    nt = S // tb
    cosf, sins, dm, zeta, xi, gc = tables
    scale = HEAD_DIM ** -0.5

    def body(qk_ref, vg_ref, cos_ref, sin_ref, dm_ref, zeta_ref, xi_ref, gc_ref, gnw_ref, rprev_ref, dy_ref, dp_in,
             dp_ref, dgn_ref, dr_s, dqk_s, dvg_s, out_sems):
        del dp_in
        i = pl.program_id(0)
        slot = i % 2

        def out_copies(step, sl):
            rows = pl.ds(pl.multiple_of((nt - 1 - step) * tb, tb), tb)
            return (pltpu.make_async_copy(dqk_s.at[sl], dp_ref.at[rows, pl.ds(2 * DR, 2 * DR)], out_sems.at[sl, 0]),
                    pltpu.make_async_copy(dvg_s.at[sl], dp_ref.at[rows, pl.ds(4 * DR, 2 * DR)], out_sems.at[sl, 1]))

        @pl.when(i == 0)
        def _():
            dr_s[...] = jnp.zeros_like(dr_s)
            dgn_ref[...] = jnp.zeros_like(dgn_ref)

        @pl.when(i >= 2)
        def _():
            for cp in out_copies(i - 2, slot):
                cp.wait()

        def chunk(cc, carry):
            c = nc - 1 - cc
            rows = pl.ds(pl.multiple_of(c * CHUNK, CHUNK), CHUNK)
            cos = cos_ref[rows, :]
            sin = sin_ref[rows, :]
            heads = range(H)
            c0 = [slice(h * HEAD_DIM, (h + 1) * HEAD_DIM) for h in heads]
            c1 = [slice(DR + h * HEAD_DIM, DR + (h + 1) * HEAD_DIM) for h in heads]
            qh = [_rope(qk_ref[rows, c0[h]], cos, sin) for h in heads]
            kh = [_rope(qk_ref[rows, c1[h]], cos, sin) * scale for h in heads]
            qb = [t.astype(BF16) for t in qh]
            kb = [t.astype(BF16) for t in kh]
            vb = [vg_ref[rows, c0[h]].astype(BF16) for h in heads]
            rpb = [rprev_ref[c, h] for h in heads]
            qx = [(qh[h] * xi_ref[h]).astype(BF16) for h in heads]
            kz = [(kh[h] * zeta_ref[h]).astype(BF16) for h in heads]
            drh = [dr_s[h] for h in heads]
            drb = [t.astype(BF16) for t in drh]
            s = [_dot(qb[h], kb[h], NT) for h in heads]
            cross = [_dot(qx[h], rpb[h]) for h in heads]
            dv_state = [_dot(kz[h], drb[h]) for h in heads]
            dk_state = [_dot(vb[h], drb[h], NT) for h in heads]
            sb = [(s[h] * dm_ref[h]).astype(BF16) for h in heads]
            o = [_dot(sb[h], vb[h]) + cross[h] for h in heads]
            dob = []
            for h in heads:
                mu = jnp.mean(o[h], axis=-1, keepdims=True)
                oc = o[h] - mu
                rstd = lax.rsqrt(jnp.mean(oc * oc, axis=-1, keepdims=True) + EPS)
                ohat = oc * rstd
                gw = gnw_ref[:, c0[h]]
                gate = vg_ref[rows, c1[h]]
                sg = jax.nn.sigmoid(gate)
                dyv = dy_ref[rows, c0[h]]
                dvg_s[slot, rows, c1[h]] = (dyv * (ohat * gw) * (sg * (1.0 + gate * (1.0 - sg)))).astype(BF16)
                don = dyv * (gate * sg)
                dgn_ref[:, c0[h]] += jnp.sum(don * ohat, axis=0, keepdims=True)
                dohat = don * gw
                do = rstd * (dohat - jnp.mean(dohat, axis=-1, keepdims=True)
                             - ohat * jnp.mean(dohat * ohat, axis=-1, keepdims=True))
                dob.append(do.astype(BF16))
            ds = [_dot(dob[h], vb[h], NT) for h in heads]
            dq_state = [_dot(dob[h], rpb[h], NT) for h in heads]
            dv = [_dot(sb[h], dob[h], TN) + dv_state[h] for h in heads]
            dr_new = [_dot(qx[h], dob[h], TN) for h in heads]
            dsb = [(ds[h] * dm_ref[h]).astype(BF16) for h in heads]
            dqh = [_dot(dsb[h], kb[h]) + dq_state[h] * xi_ref[h] for h in heads]
            dkh = [_dot(dsb[h], qb[h], TN) + dk_state[h] * zeta_ref[h] for h in heads]
            for h in heads:
                dr_s[h] = drh[h] * gc_ref[h] + dr_new[h]
                dqk_s[slot, rows, c0[h]] = _rope_t(dqh[h], cos, sin).astype(BF16)
                dqk_s[slot, rows, c1[h]] = _rope_t(dkh[h] * scale, cos, sin).astype(BF16)
                dvg_s[slot, rows, c0[h]] = dv[h].astype(BF16)
            return carry

        lax.fori_loop(0, nc, chunk, 0)
        for cp in out_copies(i, slot):
            cp.start()

        @pl.when(i == nt - 1)
        def _():
            if nt >= 2:
                for cp in out_copies(i - 1, 1 - slot):
                    cp.wait()
            for cp in out_copies(i, slot):
                cp.wait()

    rev = lambda i: nt - 1 - i
    return _pcall(
        body, name="ret_bwd", grid=(nt,), aliases={11: 0}, comm=comm,
        operands=[proj, proj, cosf, sins, dm, zeta, xi, gc, gnw, rprev, dy, dproj],
        in_specs=[pl.BlockSpec((tb, 2 * DR), lambda i: (rev(i), 1)), pl.BlockSpec((tb, 2 * DR), lambda i: (rev(i), 2)),
                  pl.BlockSpec((tb, HEAD_DIM), lambda i: (rev(i), 0)), pl.BlockSpec((tb, HEAD_DIM), lambda i: (rev(i), 0))]
        + _ret_const_specs(H, DR)
        + [pl.BlockSpec((nc, H, CHUNK, HEAD_DIM), lambda i: (rev(i), 0, 0, 0)),
           pl.BlockSpec((tb, DR), lambda i: (rev(i), 1)), HBM_SPEC],
        out_specs=[HBM_SPEC, pl.BlockSpec((1, DR), lambda i: (0, 0))],
        out_shape=[jax.ShapeDtypeStruct(dproj.shape, BF16), jax.ShapeDtypeStruct((1, DR), F32)],
        scratch_shapes=[pltpu.VMEM((H, CHUNK, HEAD_DIM), F32), pltpu.VMEM((2, tb, 2 * DR), BF16),
                        pltpu.VMEM((2, tb, 2 * DR), BF16), pltpu.SemaphoreType.DMA((2, 2))])


def _place():
    x, y, c = lax.axis_index("x"), lax.axis_index("y"), lax.axis_index("c")
    chips = [(1 - x, y), (x, 1 - y), (1 - x, 1 - y)]
    return x, y, c, chips


def _own_slab(name, shard, place):
    R, C = shard.shape
    tr = _row_tile(R, C)
    return _ew("cast_" + name, lambda a: (a,), [(shard, pl.BlockSpec((tr, C), lambda i, p: (i, 0)))],
               [(jax.ShapeDtypeStruct((4, R, C), BF16), pl.BlockSpec((None, tr, C), lambda i, p: (p[1], i, 0)))],
               (R // tr,), sp=place)[0]


class _remote:
    def __init__(self, src, dst, ssem, rsem, k, to):
        self.args = dict(src_ref=src, dst_ref=dst, send_sem=ssem.at[k], recv_sem=rsem.at[k], device_id=to,
                         device_id_type=MESH)

    def start(self):
        pltpu.make_async_remote_copy(**self.args).start()

    def wait_send(self):
        pltpu.make_async_remote_copy(**self.args).wait_send()

    def wait_recv(self):
        pltpu.make_async_remote_copy(**self.args).wait_recv()


def _task_fns(copies):
    def start(cins, couts, ssem, rsem, base):
        for cp in copies(cins, couts, ssem, rsem, base)[0]:
            cp.start()

    def finish(cins, couts, ssem, rsem, base):
        sends, recvs = copies(cins, couts, ssem, rsem, base)
        for cp in sends:
            cp.wait_send()
        for cp in recvs:
            cp.wait_recv()

    return start, finish


NEIGHBOURS, DIAGONAL = (0, 1), (2,)


def _gather_ici(st, which=NEIGHBOURS + DIAGONAL):
    r2 = st.shape[1] // 2

    def copies(cins, couts, ssem, rsem, base):
        x, y, c, chips = _place()
        out = couts[0]
        mine = out.at[2 * x + y, pl.ds(c * r2, r2), :]
        sends, recvs = [], []
        for k, j in enumerate(which):
            cx, cy = chips[j]
            got = out.at[2 * cx + cy, pl.ds(c * r2, r2), :]
            sends.append(_remote(mine, mine, ssem, rsem, base + k, (cx, cy, c)))
            recvs.append(_remote(got, got, ssem, rsem, base + k, (x, y, c)))
        return sends, recvs

    start, finish = _task_fns(copies)
    return _Comm([st], [jax.ShapeDtypeStruct(st.shape, st.dtype)], {0: 0}, len(which), start, finish)


def _gather_d2d(st):
    r2 = st.shape[1] // 2

    def copies(cins, couts, ssem, rsem, base):
        x, y, c, chips = _place()
        out = couts[0]
        sends, recvs = [], []
        for j, (cx, cy) in enumerate(chips):
            have = out.at[2 * cx + cy, pl.ds(c * r2, r2), :]
            want = out.at[2 * cx + cy, pl.ds((1 - c) * r2, r2), :]
            sends.append(_remote(have, have, ssem, rsem, base + j, (x, y, 1 - c)))
            recvs.append(_remote(want, want, ssem, rsem, base + j, (x, y, c)))
        return sends, recvs

    start, finish = _task_fns(copies)
    return _Comm([st], [jax.ShapeDtypeStruct(st.shape, st.dtype)], {0: 0}, 3, start, finish)


def _gather_conv(conv_w):
    def copies(cins, couts, ssem, rsem, base):
        x, y, c, chips = _place()
        src, out = cins[0], couts[0]
        sends = [_remote(src, out.at[2 * x + y], ssem, rsem, base + j, (*chip, c)) for j, chip in enumerate(chips)]
        recvs = [_remote(src, out.at[2 * cx + cy], ssem, rsem, base + j, (x, y, c)) for j, (cx, cy) in enumerate(chips)]
        return sends, recvs

    start, finish = _task_fns(copies)
    return _Comm([conv_w], [jax.ShapeDtypeStruct((4,) + conv_w.shape, conv_w.dtype)], {}, 3, start, finish)


def _pair_exchange(g):
    r2 = g.shape[1] // 2

    def copies(cins, couts, ssem, rsem, base):
        x, y, c, _ = _place()
        cp = _remote(cins[0].at[:, pl.ds((1 - c) * r2, r2), :], couts[0], ssem, rsem, base, (x, y, 1 - c))
        return [cp], [cp]

    start, finish = _task_fns(copies)
    return _Comm([g], [jax.ShapeDtypeStruct((g.shape[0], r2, g.shape[2]), g.dtype)], {}, 1, start, finish)


def _chip_exchange(part):
    def copies(cins, couts, ssem, rsem, base):
        x, y, c, chips = _place()
        cps = [_remote(cins[0].at[2 * cx + cy], couts[0].at[j], ssem, rsem, base + j, (cx, cy, c))
               for j, (cx, cy) in enumerate(chips)]
        return cps, cps

    start, finish = _task_fns(copies)
    return _Comm([part], [jax.ShapeDtypeStruct((3,) + part.shape[1:], part.dtype)], {}, 3, start, finish)


def _pair_share(slot):
    def copies(cins, couts, ssem, rsem, base):
        x, y, c, _ = _place()
        out = couts[0]
        return ([_remote(out.at[c], out.at[c], ssem, rsem, base, (x, y, 1 - c))],
                [_remote(out.at[1 - c], out.at[1 - c], ssem, rsem, base, (x, y, c))])

    start, finish = _task_fns(copies)
    return _Comm([slot], [jax.ShapeDtypeStruct(slot.shape, slot.dtype)], {0: 0}, 1, start, finish)


def _gather_small(sm):
    flips = [(fx, fy, fc) for fx in (0, 1) for fy in (0, 1) for fc in (0, 1)][1:]

    def copies(cins, couts, ssem, rsem, base):
        x, y, c, _ = _place()
        src, out = cins[0], couts[0]
        peers = [(1 - x if fx else x, 1 - y if fy else y, 1 - c if fc else c) for fx, fy, fc in flips]
        sends = [_remote(src, out.at[4 * x + 2 * y + c], ssem, rsem, base + k, peer) for k, peer in enumerate(peers)]
        recvs = [_remote(src, out.at[4 * px + 2 * py + pc], ssem, rsem, base + k, (x, y, c))
                 for k, (px, py, pc) in enumerate(peers)]
        return sends, recvs

    start, finish = _task_fns(copies)
    return _Comm([sm], [jax.ShapeDtypeStruct((8,) + sm.shape, sm.dtype)], {}, 7, start, finish)


def _comm_call(name, tasks):
    task = _merge(tasks)
    nci = len(task.ins)

    def body(*refs):
        cins, couts, (ssem, rsem) = refs[:nci], refs[nci:nci + len(task.outs)], refs[nci + len(task.outs):]
        task.start(cins, couts, ssem, rsem, 0)
        task.finish(cins, couts, ssem, rsem, 0)

    return pl.pallas_call(
        body, in_specs=[HBM_SPEC] * nci, out_specs=[HBM_SPEC] * len(task.outs), out_shape=list(task.outs),
        scratch_shapes=[pltpu.SemaphoreType.DMA((task.n_sem,)), pltpu.SemaphoreType.DMA((task.n_sem,))],
        input_output_aliases=task.aliases, name=name)(*task.ins)


def _adamw(w, g, m, v):
    m = ADAM_B1 * m + (1.0 - ADAM_B1) * g
    v = ADAM_B2 * v + (1.0 - ADAM_B2) * (g * g)
    m_hat = m / (1.0 - ADAM_B1 ** ADAM_STEP)
    v_hat = v / (1.0 - ADAM_B2 ** ADAM_STEP)
    delta = -ADAM_LR * (m_hat / (jnp.sqrt(v_hat) + ADAM_EPS) + ADAM_WD * w)
    return delta, m, v


def _adamw_call(name, w, g, m, v):
    R, C = w.shape
    tr = _row_tile(R, C, 1024 * 1024)
    row = pl.BlockSpec((tr, C), lambda i: (i, 0))
    o = jax.ShapeDtypeStruct((R, C), F32)
    return _ew(name, lambda w_, g_, m_, v_: (*_adamw(w_, g_, m_, v_), g_), [(w, row), (g, row), (m, row), (v, row)],
               [(o, row), (o, row), (o, row), (o, row)], (R // tr,))


def _pair_sum(name, g, ra, place):
    _, R, C = g.shape
    r2 = R // 2
    tr = _row_tile(r2, C)
    nb = r2 // tr
    own = pl.BlockSpec((None, tr, C), lambda j, i, p: (j, p[0] * nb + i, 0))
    blk = pl.BlockSpec((None, tr, C), lambda j, i, p: (j, i, 0))
    return _ew("rs_pair_sum_" + name, lambda a, b: (a + b,), [(g, own), (ra, blk)],
               [(jax.ShapeDtypeStruct((4, r2, C), BF16), blk)], (4, nb), sp=place)[0]


def _chip_sum(name, g, ra, rb, place):
    _, R, C = g.shape
    r2 = R // 2
    tr = _row_tile(r2, C)
    nb = r2 // tr
    own = pl.BlockSpec((None, tr, C), lambda i, p: (p[1], p[0] * nb + i, 0))
    mine = pl.BlockSpec((None, tr, C), lambda i, p: (p[1], i, 0))
    src = [pl.BlockSpec((None, tr, C), functools.partial(lambda i, p, j: (j, i, 0), j=j)) for j in range(3)]
    out = pl.BlockSpec((None, tr, C), lambda i, p: (p[0], i, 0))

    def total(a, b, r0, r1, r2_):
        return ((((a + b) + r0.astype(F32)) + r1.astype(F32)) + r2_.astype(F32),)

    return _ew("rs_chip_sum_" + name, total, [(g, own), (ra, mine), (rb, src[0]), (rb, src[1]), (rb, src[2])],
               [(jax.ShapeDtypeStruct((2, r2, C), F32), out)], (nb,), sp=place)[0]


def _pack(arrays):
    rows, offs, pos = [], [], 0
    for a in arrays:
        flat = a.reshape(-1)
        n = -(-flat.shape[0] // (8 * LANES)) * (8 * LANES)
        if n != flat.shape[0]:
            flat = jnp.pad(flat, (0, n - flat.shape[0]))
        rows.append(flat.reshape(-1, LANES))
        offs.append(pos)
        pos += n // LANES
    return jnp.concatenate(rows, axis=0), offs


def _unpack(packed, offs, shapes):
    out = []
    for off, shp in zip(offs, shapes):
        n = 1
        for s in shp:
            n *= s
        out.append(packed[off:off + -(-n // LANES)].reshape(-1)[:n].reshape(shp))
    return out


def _sum8(gathered):
    _, R, C = gathered.shape
    tr = _row_tile(R, C, 256 * 1024)
    specs = [pl.BlockSpec((None, tr, C), functools.partial(lambda i, d: (d, i, 0), d=d)) for d in range(8)]

    def fn(*parts):
        t = parts[0]
        for p in parts[1:]:
            t = t + p
        return (t,)

    return _ew("small_sum", fn, [(gathered, s) for s in specs],
               [(jax.ShapeDtypeStruct((R, C), F32), pl.BlockSpec((tr, C), lambda i: (i, 0)))], (R // tr,))[0]


BIG = ("w_in", "w_out", "w_ffn_gate", "w_ffn_up", "w_ffn_down")
SMALL = ("ln1_w", "conv_w", "conv_b", "gate_a_w", "gate_a_b", "gate_x_w", "gate_x_b", "lru_lambda", "ret_gn_w", "ln2_w",
         "final_norm_w")
WEIGHTS = ("ln1_w", "w_in", "conv_w", "conv_b", "gate_a_w", "gate_a_b", "gate_x_w", "gate_x_b", "lru_lambda", "ret_gn_w",
           "w_out", "ln2_w", "w_ffn_gate", "w_ffn_up", "w_ffn_down", "final_norm_w")


def kernel(x, ln1_w, w_in, conv_w, conv_b, gate_a_w, gate_a_b, gate_x_w, gate_x_b, lru_lambda, ret_gn_w, w_out, ln2_w, w_ffn_gate, w_ffn_up, w_ffn_down, final_norm_w, loss_target, m_ln1_w, m_w_in, m_conv_w, m_conv_b, m_gate_a_w, m_gate_a_b, m_gate_x_w, m_gate_x_b, m_lru_lambda, m_ret_gn_w, m_w_out, m_ln2_w, m_w_ffn_gate, m_w_ffn_up, m_w_ffn_down, m_final_norm_w, v_ln1_w, v_w_in, v_conv_w, v_conv_b, v_gate_a_w, v_gate_a_b, v_gate_x_w, v_gate_x_b, v_lru_lambda, v_ret_gn_w, v_w_out, v_ln2_w, v_w_ffn_gate, v_w_ffn_up, v_w_ffn_down, v_final_norm_w):
    w = dict(ln1_w=ln1_w, w_in=w_in, conv_w=conv_w, conv_b=conv_b, gate_a_w=gate_a_w, gate_a_b=gate_a_b, gate_x_w=gate_x_w,
             gate_x_b=gate_x_b, lru_lambda=lru_lambda, ret_gn_w=ret_gn_w, w_out=w_out, ln2_w=ln2_w, w_ffn_gate=w_ffn_gate,
             w_ffn_up=w_ffn_up, w_ffn_down=w_ffn_down, final_norm_w=final_norm_w)
    m = dict(ln1_w=m_ln1_w, w_in=m_w_in, conv_w=m_conv_w, conv_b=m_conv_b, gate_a_w=m_gate_a_w, gate_a_b=m_gate_a_b,
             gate_x_w=m_gate_x_w, gate_x_b=m_gate_x_b, lru_lambda=m_lru_lambda, ret_gn_w=m_ret_gn_w, w_out=m_w_out,
             ln2_w=m_ln2_w, w_ffn_gate=m_w_ffn_gate, w_ffn_up=m_w_ffn_up, w_ffn_down=m_w_ffn_down,
             final_norm_w=m_final_norm_w)
    v = dict(ln1_w=v_ln1_w, w_in=v_w_in, conv_w=v_conv_w, conv_b=v_conv_b, gate_a_w=v_gate_a_w, gate_a_b=v_gate_a_b,
             gate_x_w=v_gate_x_w, gate_x_b=v_gate_x_b, lru_lambda=v_lru_lambda, ret_gn_w=v_ret_gn_w, w_out=v_w_out,
             ln2_w=v_ln2_w, w_ffn_gate=v_w_ffn_gate, w_ffn_up=v_w_ffn_up, w_ffn_down=v_w_ffn_down,
             final_norm_w=v_final_norm_w)
    xs, tgt = x[0], loss_target[0]
    S, D = xs.shape
    DL, DR = conv_b.shape[1], ret_gn_w.shape[1]
    assert DL == DR and DL % HEAD_DIM == 0 and S % CHUNK == 0
    d_mix = DL + DR
    cx, cy, cc = lax.axis_index("x"), lax.axis_index("y"), lax.axis_index("c")
    chip = 2 * cx + cy
    place = jnp.stack([cc, chip]).astype(jnp.int32)
    grad, delta, new_m, new_v = {}, {}, {}, {}

    def finish_big(n, full):
        shp = w[n].shape
        g2 = full.reshape(shp[1], shp[2])
        w2, m2, v2 = (t[n].reshape(shp[1], shp[2]) for t in (w, m, v))
        d_, m_, v_, g_ = _adamw_call("adamw_" + n, w2, g2, m2, v2)
        grad[n], delta[n], new_m[n], new_v[n] = (t.reshape(shp) for t in (g_, d_, m_, v_))

    def all_sum(gathered, own):
        return _sum8(lax.dynamic_update_slice(gathered, own[None], (4 * cx + 2 * cy + cc, 0, 0)))

    st = {n: _own_slab(n, w[n][0], place) for n in BIG}
    TM, TK = 512, 2048
    (u1,), (w_in_st,) = _rms_fwd("rms1", xs, ln1_w, TM, comm=_gather_ici(st["w_in"]))
    w_in_st, conv_st = _comm_call("gather_w_in", [_gather_d2d(w_in_st), _gather_conv(conv_w[0])])
    conv_st = lax.dynamic_update_slice(conv_st, conv_w, (chip, 0, 0))
    cw_cols = conv_st.shape[2]
    conv_full = jnp.transpose(conv_st, (1, 0, 2)).reshape(CONV_W, 4 * cw_cols)
    n_in, n_ff = w_in_st.shape[2], st["w_ffn_gate"].shape[2]
    tables = _ret_tables(S, DR // HEAD_DIM)
    wab, wxb = gate_a_w[0].astype(BF16), gate_x_w[0].astype(BF16)
    lru_w = (conv_full, conv_b, wab, gate_a_b, wxb, gate_x_b, lru_lambda)

    proj, (w_out_st, wg_st, wu_st) = _mm_nn_stacked(
        "proj", u1, w_in_st, F32, 2 * TM,
        comm=_merge([_gather_ici(st["w_out"]), _gather_ici(st["w_ffn_gate"]), _gather_ici(st["w_ffn_up"], NEIGHBOURS)]))
    (hs, y), (w_out_st, wg_st, wu_st) = _lru_fwd(
        proj, *lru_w, 128, d_mix, comm=_merge([_gather_d2d(w_out_st), _gather_d2d(wg_st), _gather_ici(wu_st, DIAGONAL)]))
    (y, rprev), (wu_st, wd_st) = _ret_fwd(proj, y, tables, ret_gn_w, 256,
                                          comm=_merge([_gather_d2d(wu_st), _gather_ici(st["w_ffn_down"], NEIGHBOURS)]))
    w_out_f = w_out_st.reshape(d_mix, D)
    (h1, u2), (wd_st,) = _out_proj_rms(y, w_out_f, xs, ln2_w, TM, comm=_gather_ici(wd_st, DIAGONAL))
    (dg_fac, du_fac, ff), (wd_st,) = _ffn_gate_up(u2, wg_st, wu_st, 2 * TM, comm=_gather_d2d(wd_st))
    wd_f = wd_st.reshape(4 * n_ff, D)
    dh2, dh2b, d_fw, loss = _ffn_down_loss(ff, wd_f, h1, tgt, final_norm_w.reshape(1, D), 256)

    g_wd = _mm_tn("g_w_down", ff, dh2b, n_ff, 1024, TK).reshape(4, n_ff, D)
    (dgt, dup), (ra_wd,) = _ffn_gate_up_bwd(dh2b, wd_f, dg_fac, du_fac, 2 * TM, n_ff, comm=_pair_exchange(g_wd))
    pb_wd = _pair_sum("w_ffn_down", g_wd, ra_wd, place)
    g_wg, (rb_wd,) = _mm_tn("g_w_gate", u2, dgt, 1024, None, TK, stacked_cols=n_ff, comm=_chip_exchange(pb_wd))
    slot_wd = _chip_sum("w_ffn_down", g_wd, ra_wd, rb_wd, place)
    g_wu, (full_wd, ra_wg) = _mm_tn("g_w_up", u2, dup, 1024, None, TK, stacked_cols=n_ff,
                                    comm=_merge([_pair_share(slot_wd), _pair_exchange(g_wg)]))
    finish_big("w_ffn_down", full_wd)
    pb_wg = _pair_sum("w_ffn_gate", g_wg, ra_wg, place)
    du2, (rb_wg,) = _mm_nt_stacked("d_u2_gate", dgt, wg_st, TM, comm=_chip_exchange(pb_wg))
    du2, (ra_wu,) = _mm_nt_stacked("d_u2_up", dup, wu_st, TM, res=du2, comm=_pair_exchange(g_wu))
    slot_wg = _chip_sum("w_ffn_gate", g_wg, ra_wg, rb_wg, place)
    pb_wu = _pair_sum("w_ffn_up", g_wu, ra_wu, place)
    (dh1, dh1b, dy, d_ln2), (full_wg,) = _rms_bwd_dy(h1, ln2_w, du2, dh2, w_out_f, 256, comm=_pair_share(slot_wg))
    finish_big("w_ffn_gate", full_wg)
    g_wout = _mm_tn("g_w_out", y, dh1b, 1024, 1024, TK).reshape(4, d_mix // 4, D)
    (dproj, d_cw, d_cb, d_wa, d_ba, d_wx, d_bx, d_lam), (rb_wu, ra_wout) = _lru_bwd(
        proj, hs, dy, *lru_w, 128, comm=_merge([_chip_exchange(pb_wu), _pair_exchange(g_wout)]))
    slot_wu = _chip_sum("w_ffn_up", g_wu, ra_wu, rb_wu, place)
    pb_wout = _pair_sum("w_out", g_wout, ra_wout, place)
    (dproj, d_gn), (full_wu, rb_wout) = _ret_bwd(proj, rprev, dy, dproj, tables, ret_gn_w, 256,
                                                 comm=_merge([_pair_share(slot_wu), _chip_exchange(pb_wout)]))
    finish_big("w_ffn_up", full_wu)
    slot_wout = _chip_sum("w_out", g_wout, ra_wout, rb_wout, place)
    small = dict(conv_w=d_cw, conv_b=d_cb, gate_a_w=d_wa, gate_a_b=d_ba, gate_x_w=d_wx, gate_x_b=d_bx, lru_lambda=d_lam,
                 ret_gn_w=d_gn, ln2_w=d_ln2, final_norm_w=d_fw)
    packed, offs = _pack([small[n] for n in SMALL[1:]] + [loss])
    g_win, (full_wout, got_small) = _mm_tn("g_w_in", u1, dproj, 1024, None, TK, stacked_cols=n_in,
                                           comm=_merge([_pair_share(slot_wout), _gather_small(packed)]))
    finish_big("w_out", full_wout)
    (ra_win,) = _comm_call("rs_pair_w_in", [_pair_exchange(g_win)])
    pb_win = _pair_sum("w_in", g_win, ra_win, place)
    du1, (rb_win,) = _mm_nt_stacked("d_u1", dproj, w_in_st, TM, comm=_chip_exchange(pb_win))
    slot_win = _chip_sum("w_in", g_win, ra_win, rb_win, place)
    gx, d_ln1 = _rms_bwd("rms1_bwd", xs, ln1_w, du1, dh1, TM)
    packed1, _ = _pack([d_ln1])
    full_win, got_ln1 = _comm_call("reduce_tail", [_pair_share(slot_win), _gather_small(packed1)])
    finish_big("w_in", full_win)

    red = _unpack(all_sum(got_small, packed), offs, [small[n].shape for n in SMALL[1:]] + [(1, LANES)])
    g = dict(zip(SMALL[1:], red[:-1]))
    g["ln1_w"] = all_sum(got_ln1, packed1)[:-(-D // LANES)].reshape(1, D)
    loss_out = red[-1][0, 0]
    g["conv_w"] = lax.dynamic_slice(g["conv_w"], (0, chip * cw_cols), (CONV_W, cw_cols))
    packs = [_pack([t[n] for n in SMALL])[0] for t in (w, m, v)]
    gp, offs2 = _pack([g[n] for n in SMALL])
    outs = _adamw_call("adamw_small", packs[0], gp, packs[1], packs[2])
    shapes = [w[n].shape for n in SMALL]
    for dst, arr in zip((delta, new_m, new_v), outs):
        dst.update(zip(SMALL, _unpack(arr, offs2, shapes)))
    for n in SMALL:
        grad[n] = g[n].reshape(w[n].shape)

    return (loss_out, gx.reshape(x.shape), *[grad[n] for n in WEIGHTS], *[delta[n] for n in WEIGHTS],
            *[new_m[n] for n in WEIGHTS], *[new_v[n] for n in WEIGHTS])
```

```python
import functools

import jax
import jax.numpy as jnp
from jax import lax
from jax.experimental import pallas as pl
from jax.experimental.pallas import tpu as pltpu

F32 = jnp.float32
BF16 = jnp.bfloat16
MESH = pl.DeviceIdType.MESH

EPS = 1e-6
LRU_C = 8.0
ROPE_BASE = 10000.0
CHUNK = 128
HEAD_DIM = 128
CONV_W = 4
ADAM_LR = 0.001
ADAM_B1 = 0.9
ADAM_B2 = 0.999
ADAM_EPS = 1e-08
ADAM_WD = 0.01
ADAM_STEP = 10

V7X_VMEM_BYTES = 64 * 1024 * 1024
VMEM_LIMIT = V7X_VMEM_BYTES - 8 * 1024 * 1024
LANES = 128
SUBLANES_16BIT = 16

NN = (((1,), (0,)), ((), ()))
NT = (((1,), (1,)), ((), ()))
TN = (((0,), (0,)), ((), ()))


def _dot(a, b, dims=NN):
    return lax.dot_general(a, b, dims, preferred_element_type=F32)


def _tile(n, pref, mult=SUBLANES_16BIT):
    best = None
    t = mult
    while t <= min(n, pref):
        if n % t == 0:
            best = t
        t += mult
    return best if best is not None else n


def _row_tile(rows, cols, budget_bytes=2 * 1024 * 1024):
    return _tile(rows, max(SUBLANES_16BIT, budget_bytes // (cols * 4)))


def _params(sem):
    return pltpu.CompilerParams(dimension_semantics=sem, vmem_limit_bytes=VMEM_LIMIT)


HBM_SPEC = pl.BlockSpec(memory_space=pl.ANY)


class _Comm:
    def __init__(self, ins, outs, aliases, n_sem, start, finish):
        self.ins, self.outs, self.aliases, self.n_sem, self.start, self.finish = ins, outs, aliases, n_sem, start, finish


def _merge(tasks):
    ins, outs, aliases, plans, n_sem = [], [], {}, [], 0
    for t in tasks:
        i0, o0 = len(ins), len(outs)
        plans.append((t, i0, o0, n_sem))
        ins += t.ins
        outs += t.outs
        aliases.update({i0 + a: o0 + b for a, b in t.aliases.items()})
        n_sem += t.n_sem

    def run(which):
        def go(cins, couts, ssem, rsem, base):
            for t, i0, o0, s0 in plans:
                getattr(t, which)(cins[i0:i0 + len(t.ins)], couts[o0:o0 + len(t.outs)], ssem, rsem, base + s0)
        return go

    return _Comm(ins, outs, aliases, n_sem, run("start"), run("finish"))


def _pcall(body, *, name, grid, in_specs, out_specs, out_shape, operands, scratch_shapes=(), aliases=None, comm=None):
    n_in, n_out, n_scr = len(operands), len(out_shape), len(scratch_shapes)
    aliases = dict(aliases or {})
    params = _params(("arbitrary",) * len(grid))
    if comm is None:
        return pl.pallas_call(body, grid=grid, in_specs=list(in_specs), out_specs=list(out_specs), out_shape=list(out_shape),
                              scratch_shapes=list(scratch_shapes), input_output_aliases=aliases, name=name,
                              compiler_params=params)(*operands)
    nci, nco = len(comm.ins), len(comm.outs)

    def wrapped(*refs):
        ins, cins = refs[:n_in], refs[n_in:n_in + nci]
        o0 = n_in + nci
        outs, couts = refs[o0:o0 + n_out], refs[o0 + n_out:o0 + n_out + nco]
        s0 = o0 + n_out + nco
        scr, (ssem, rsem) = refs[s0:s0 + n_scr], refs[s0 + n_scr:]
        ids = [pl.program_id(a) for a in range(len(grid))]
        first = functools.reduce(jnp.logical_and, [i == 0 for i in ids])
        last = functools.reduce(jnp.logical_and, [i == g - 1 for i, g in zip(ids, grid)])

        @pl.when(first)
        def _():
            comm.start(cins, couts, ssem, rsem, 0)

        body(*ins, *outs, *scr)

        @pl.when(last)
        def _():
            comm.finish(cins, couts, ssem, rsem, 0)

    aliases.update({n_in + a: n_out + b for a, b in comm.aliases.items()})
    res = pl.pallas_call(
        wrapped, grid=grid, in_specs=list(in_specs) + [HBM_SPEC] * nci, out_specs=list(out_specs) + [HBM_SPEC] * nco,
        out_shape=list(out_shape) + list(comm.outs),
        scratch_shapes=list(scratch_shapes) + [pltpu.SemaphoreType.DMA((comm.n_sem,)), pltpu.SemaphoreType.DMA((comm.n_sem,))],
        input_output_aliases=aliases, name=name, compiler_params=params)(*operands, *comm.ins)
    return res[:n_out], res[n_out:]


def _ew(name, fn, ins, outs, grid, sp=None):
    n_in = len(ins)

    def body(*refs):
        if sp is not None:
            refs = refs[1:]
        vals = [r[...] for r in refs[:n_in]]
        res = fn(*vals)
        for o_ref, v in zip(refs[n_in:], res):
            o_ref[...] = v.astype(o_ref.dtype)

    in_specs = [s for _, s in ins]
    out_specs = [s for _, s in outs]
    out_shape = [s for s, _ in outs]
    sem = ("arbitrary",) * len(grid)
    if sp is None:
        return pl.pallas_call(body, grid=grid, in_specs=in_specs, out_specs=out_specs, out_shape=out_shape,
                              name=name, compiler_params=_params(sem))(*[a for a, _ in ins])
    gs = pltpu.PrefetchScalarGridSpec(num_scalar_prefetch=1, grid=grid, in_specs=in_specs, out_specs=out_specs)
    return pl.pallas_call(body, grid_spec=gs, out_shape=out_shape, name=name,
                          compiler_params=_params(sem))(sp, *[a for a, _ in ins])


def _matmul(name, pairs, dims, grid, out_shape, out_spec, acc_shape, res=None, comm=None):
    n = len(pairs)
    nk = grid[2]

    def body(*refs):
        ab = refs[:2 * n]
        pos = 2 * n
        res_ref = None
        if res is not None:
            res_ref = refs[pos]
            pos += 1
        o_ref = refs[pos]
        acc_ref = refs[pos + 1] if nk > 1 else None

        def partial():
            t = None
            for p in range(n):
                d = _dot(ab[2 * p][...], ab[2 * p + 1][...], dims)
                t = d if t is None else t + d
            return t

        def finish(t):
            if res_ref is not None:
                t = t + res_ref[...]
            o_ref[...] = t.astype(o_ref.dtype)

        if nk == 1:
            finish(partial())
        else:
            k = pl.program_id(2)

            @pl.when(k == 0)
            def _():
                acc_ref[...] = partial()

            @pl.when(k > 0)
            def _():
                acc_ref[...] += partial()

            @pl.when(k == nk - 1)
            def _():
                finish(acc_ref[...])

    operands, in_specs = [], []
    for a, a_spec, b, b_spec in pairs:
        operands += [a, b]
        in_specs += [a_spec, b_spec]
    if res is not None:
        operands.append(res[0])
        in_specs.append(res[1])
    scratch = [pltpu.VMEM(acc_shape, F32)] if nk > 1 else []
    res = _pcall(body, name=name, grid=grid, in_specs=in_specs, out_specs=[out_spec], out_shape=[out_shape],
                 operands=operands, scratch_shapes=scratch, comm=comm)
    return res[0] if comm is None else (res[0][0], res[1])


def _mm_nn_stacked(name, a, b_st, out_dtype, tm, comm=None):
    M, K = a.shape
    J, _, Nj = b_st.shape
    tm = _tile(M, tm)
    return _matmul(
        name, [(a, pl.BlockSpec((tm, K), lambda j, i, k: (i, 0)), b_st, pl.BlockSpec((None, K, Nj), lambda j, i, k: (j, 0, 0)))],
        NN, (J, M // tm, 1), jax.ShapeDtypeStruct((M, J * Nj), out_dtype), pl.BlockSpec((tm, Nj), lambda j, i, k: (i, j)), None,
        comm=comm)


def _mm_nt_stacked(name, a, b_st, tm, res=None, comm=None):
    M = a.shape[0]
    J, N, Nj = b_st.shape
    tm = _tile(M, tm)

    def body(a_ref, b_ref, *rest):
        o_ref = rest[-1]
        t = None if res is None else rest[0][...]
        for s in range(J):
            d = _dot(a_ref[:, s * Nj:(s + 1) * Nj], b_ref[s], NT)
            t = d if t is None else t + d
        o_ref[...] = t

    row = pl.BlockSpec((tm, N), lambda i: (i, 0))
    out = _pcall(body, name=name, grid=(M // tm,),
                 in_specs=[pl.BlockSpec((tm, J * Nj), lambda i: (i, 0)),
                           pl.BlockSpec((J, N, Nj), lambda i: (0, 0, 0), pipeline_mode=pl.Buffered(1))] + [row] * (res is not None),
                 out_specs=[row], out_shape=[jax.ShapeDtypeStruct((M, N), F32)],
                 operands=[a, b_st] + [res] * (res is not None), comm=comm)
    return out[0] if comm is None else (out[0][0], out[1])


MXU_COLUMNS = 256


def _col_blocks(n):
    return [slice(s, min(s + MXU_COLUMNS, n)) for s in range(0, n, MXU_COLUMNS)]


def _ffn_gate_up(u2, wg_st, wu_st, tm, comm=None):
    S, D = u2.shape
    J, _, Nj = wg_st.shape
    tm = _tile(S, tm)

    def body(a_ref, wg_ref, wu_ref, dg_ref, du_ref, ff_ref):
        a = a_ref[...]
        blocks = _col_blocks(Nj)
        ahead = (_dot(a, wg_ref[:, blocks[0]]), _dot(a, wu_ref[:, blocks[0]]))
        for j, cols in enumerate(blocks):
            g, u = ahead
            if j + 1 < len(blocks):
                ahead = (_dot(a, wg_ref[:, blocks[j + 1]]), _dot(a, wu_ref[:, blocks[j + 1]]))
            sg = jax.nn.sigmoid(g)
            silu = g * sg
            dg_ref[:, cols] = (u * (sg * (1.0 + g * (1.0 - sg)))).astype(BF16)
            du_ref[:, cols] = silu.astype(BF16)
            ff_ref[:, cols] = (silu * u).astype(BF16)

    w_spec = pl.BlockSpec((None, D, Nj), lambda j, i: (j, 0, 0))
    o_spec = pl.BlockSpec((tm, Nj), lambda j, i: (i, j))
    o = jax.ShapeDtypeStruct((S, J * Nj), BF16)
    return _pcall(body, name="ffn_gate_up", grid=(J, S // tm),
                  in_specs=[pl.BlockSpec((tm, D), lambda j, i: (i, 0)), w_spec, w_spec],
                  out_specs=[o_spec, o_spec, o_spec], out_shape=[o, o, o], operands=[u2, wg_st, wu_st], comm=comm)


def _ffn_gate_up_bwd(dh2b, wd, dg_fac, du_fac, tm, tn, comm=None):
    S, D = dh2b.shape
    F = wd.shape[0]
    tm, tn = _tile(S, tm), _tile(F, tn, LANES)

    def body(a_ref, wd_ref, dg_ref, du_ref, dgt_ref, dup_ref):
        a = a_ref[...]
        for cols in _col_blocks(tn):
            d = _dot(a, wd_ref[cols, :], NT)
            dgt_ref[:, cols] = (d * dg_ref[:, cols].astype(F32)).astype(BF16)
            dup_ref[:, cols] = (d * du_ref[:, cols].astype(F32)).astype(BF16)

    blk = pl.BlockSpec((tm, tn), lambda j, i: (i, j))
    o = jax.ShapeDtypeStruct((S, F), BF16)
    return _pcall(body, name="ffn_gate_up_bwd", grid=(F // tn, S // tm),
                  in_specs=[pl.BlockSpec((tm, D), lambda j, i: (i, 0)), pl.BlockSpec((tn, D), lambda j, i: (j, 0)), blk, blk],
                  out_specs=[blk, blk], out_shape=[o, o], operands=[dh2b, wd, dg_fac, du_fac], comm=comm)


def _mm_tn(name, a, b, tmo, tn, tk, stacked_cols=None, comm=None):
    S, Mo = a.shape
    N = b.shape[1]
    tmo, tk = _tile(Mo, tmo, LANES), _tile(S, tk)
    if stacked_cols is None:
        tn = _tile(N, tn, LANES)
        out_shape = jax.ShapeDtypeStruct((Mo, N), F32)
        out_spec = pl.BlockSpec((tmo, tn), lambda i, j, k: (i, j))
    else:
        tn = stacked_cols
        out_shape = jax.ShapeDtypeStruct((N // tn, Mo, tn), F32)
        out_spec = pl.BlockSpec((None, tmo, tn), lambda i, j, k: (j, i, 0))
    return _matmul(
        name, [(a, pl.BlockSpec((tk, tmo), lambda i, j, k: (k, i)), b, pl.BlockSpec((tk, tn), lambda i, j, k: (k, j)))],
        TN, (Mo // tmo, N // tn, S // tk), out_shape, out_spec, (tmo, tn), comm=comm)


def _rms_fwd(name, x, w, tm, comm=None):
    S, D = x.shape
    tm = _tile(S, tm)

    def body(x_ref, w_ref, o_ref):
        xv = x_ref[...]
        r = lax.rsqrt(jnp.mean(xv * xv, axis=-1, keepdims=True) + EPS)
        o_ref[...] = ((xv * r) * w_ref[...]).astype(BF16)

    row = pl.BlockSpec((tm, D), lambda i: (i, 0))
    return _pcall(body, name=name, grid=(S // tm,), in_specs=[row, pl.BlockSpec((1, D), lambda i: (0, 0))], out_specs=[row],
                  out_shape=[jax.ShapeDtypeStruct((S, D), BF16)], operands=[x, w], comm=comm)


def _rms_bwd(name, x, w, dy, dres, tm, comm=None):
    S, D = x.shape
    tm = _tile(S, tm)

    def body(x_ref, w_ref, dy_ref, dres_ref, dx_ref, dw_ref):
        i = pl.program_id(0)

        @pl.when(i == 0)
        def _():
            dw_ref[...] = jnp.zeros_like(dw_ref)

        xv = x_ref[...]
        r = lax.rsqrt(jnp.mean(xv * xv, axis=-1, keepdims=True) + EPS)
        nv = xv * r
        dyv = dy_ref[...]
        dn = dyv * w_ref[...]
        dw_ref[...] += jnp.sum(dyv * nv, axis=0, keepdims=True)
        dx = dres_ref[...] + r * (dn - nv * jnp.mean(dn * nv, axis=-1, keepdims=True))
        dx_ref[...] = dx

    row = pl.BlockSpec((tm, D), lambda i: (i, 0))
    vec = pl.BlockSpec((1, D), lambda i: (0, 0))
    return _pcall(body, name=name, grid=(S // tm,), in_specs=[row, vec, row, row], out_specs=[row, vec],
                  out_shape=[jax.ShapeDtypeStruct((S, D), F32), jax.ShapeDtypeStruct((1, D), F32)],
                  operands=[x, w, dy, dres], comm=comm)


def _rms_bwd_dy(h1, w, du2, dh2, w_out, tm, comm=None):
    S, D = h1.shape
    d_mix = w_out.shape[0]
    tm = _tile(S, tm)

    def body(x_ref, w_ref, dy_ref, dres_ref, wo_ref, dx_ref, dxb_ref, out_ref, dw_ref):
        i = pl.program_id(0)

        @pl.when(i == 0)
        def _():
            dw_ref[...] = jnp.zeros_like(dw_ref)

        xv = x_ref[...]
        r = lax.rsqrt(jnp.mean(xv * xv, axis=-1, keepdims=True) + EPS)
        nv = xv * r
        dyv = dy_ref[...]
        dn = dyv * w_ref[...]
        dw_ref[...] += jnp.sum(dyv * nv, axis=0, keepdims=True)
        dx = dres_ref[...] + r * (dn - nv * jnp.mean(dn * nv, axis=-1, keepdims=True))
        dx_ref[...] = dx
        dxb = dx.astype(BF16)
        dxb_ref[...] = dxb
        out_ref[...] = _dot(dxb, wo_ref[...], NT)

    row = pl.BlockSpec((tm, D), lambda i: (i, 0))
    vec = pl.BlockSpec((1, D), lambda i: (0, 0))
    return _pcall(
        body, name="rms2_bwd_dy", grid=(S // tm,),
        in_specs=[row, vec, row, row, pl.BlockSpec((d_mix, D), lambda i: (0, 0), pipeline_mode=pl.Buffered(1))],
        out_specs=[row, row, pl.BlockSpec((tm, d_mix), lambda i: (i, 0)), vec],
        out_shape=[jax.ShapeDtypeStruct((S, D), F32), jax.ShapeDtypeStruct((S, D), BF16),
                   jax.ShapeDtypeStruct((S, d_mix), F32), jax.ShapeDtypeStruct((1, D), F32)],
        operands=[h1, w, du2, dh2, w_out], comm=comm)


def _out_proj_rms(y, w_out, x, ln_w, tm, comm=None):
    S, K = y.shape
    D = w_out.shape[1]
    tm = _tile(S, tm)

    def body(a_ref, w_ref, x_ref, lw_ref, h_ref, u_ref):
        hv = _dot(a_ref[...], w_ref[...]) + x_ref[...]
        h_ref[...] = hv
        r = lax.rsqrt(jnp.mean(hv * hv, axis=-1, keepdims=True) + EPS)
        u_ref[...] = ((hv * r) * lw_ref[...]).astype(BF16)

    row = pl.BlockSpec((tm, D), lambda i: (i, 0))
    return _pcall(
        body, name="out_proj", grid=(S // tm,),
        in_specs=[pl.BlockSpec((tm, K), lambda i: (i, 0)),
                  pl.BlockSpec((K, D), lambda i: (0, 0), pipeline_mode=pl.Buffered(1)), row,
                  pl.BlockSpec((1, D), lambda i: (0, 0))],
        out_specs=[row, row], out_shape=[jax.ShapeDtypeStruct((S, D), F32), jax.ShapeDtypeStruct((S, D), BF16)],
        operands=[y, w_out, x, ln_w], comm=comm)


def _ffn_down_loss(ff, wd, h1, tgt, fw, tm):
    S, K = ff.shape
    D = wd.shape[1]
    tm = _tile(S, tm)

    def body(a_ref, wd_ref, h1_ref, t_ref, w_ref, dh_ref, dhb_ref, dw_ref, loss_ref):
        i = pl.program_id(0)

        @pl.when(i == 0)
        def _():
            dw_ref[...] = jnp.zeros_like(dw_ref)
            loss_ref[...] = jnp.zeros_like(loss_ref)

        hv = _dot(a_ref[...], wd_ref[...]) + h1_ref[...]
        wv = w_ref[...]
        r = lax.rsqrt(jnp.mean(hv * hv, axis=-1, keepdims=True) + EPS)
        nv = hv * r
        err = nv * wv - t_ref[...]
        row_loss = jnp.mean(err * err, axis=-1, keepdims=True)
        loss_ref[...] += 0.5 * jnp.sum(row_loss, axis=0, keepdims=True)
        dyo = err * (1.0 / D)
        dn = dyo * wv
        dw_ref[...] += jnp.sum(dyo * nv, axis=0, keepdims=True)
        dh = r * (dn - nv * jnp.mean(dn * nv, axis=-1, keepdims=True))
        dh_ref[...] = dh
        dhb_ref[...] = dh.astype(BF16)

    row = pl.BlockSpec((tm, D), lambda i: (i, 0))
    vec = pl.BlockSpec((1, D), lambda i: (0, 0))
    return _pcall(
        body, name="ffn_down_loss", grid=(S // tm,),
        in_specs=[pl.BlockSpec((tm, K), lambda i: (i, 0)),
                  pl.BlockSpec((K, D), lambda i: (0, 0), pipeline_mode=pl.Buffered(1)), row, row, vec],
        out_specs=[row, row, vec, pl.BlockSpec((1, LANES), lambda i: (0, 0))],
        out_shape=[jax.ShapeDtypeStruct((S, D), F32), jax.ShapeDtypeStruct((S, D), BF16),
                   jax.ShapeDtypeStruct((1, D), F32), jax.ShapeDtypeStruct((1, LANES), F32)],
        operands=[ff, wd, h1, tgt, fw])


def _shift_down(x, d, head8):
    r = pltpu.roll(x, d, 0)
    rh = pltpu.roll(head8, d, 0)
    row8 = lax.broadcasted_iota(jnp.int32, head8.shape, 0)
    top = jnp.where(row8 < d, rh, r[0:8])
    return jnp.concatenate([top, r[8:]], axis=0)


def _shift_up(x, d, tail8):
    n = x.shape[0]
    r = pltpu.roll(x, n - d, 0)
    rt = pltpu.roll(tail8, 8 - d, 0)
    row8 = lax.broadcasted_iota(jnp.int32, tail8.shape, 0)
    bot = jnp.where(row8 + d >= 8, rt, r[n - 8:n])
    return jnp.concatenate([r[:n - 8], bot], axis=0)


def _roll_in_groups(x, d):
    n, c = x.shape
    return pltpu.roll(x.reshape(n // 8, 8, c), d, 1).reshape(n, c)


def _log_sigmoid(lam):
    z = jnp.exp(-jnp.abs(lam))
    u = 1.0 + z
    log1p = jnp.where(u == 1.0, z, jnp.log(u) * (z / jnp.where(u == 1.0, 1.0, u - 1.0)))
    return jnp.minimum(lam, 0.0) - log1p


def _neg_expm1(z, exp_z):
    series = -z * (1.0 + z * (0.5 + z * (1.0 / 6.0)))
    return jnp.where(z > -0.02, series, 1.0 - exp_z)


_GELU_C = 0.7978845608028654


def _gelu(x):
    t = jnp.tanh(_GELU_C * (x + 0.044715 * (x * x * x)))
    return x * (0.5 * (1.0 + t)), t


def _gelu_grad(x, t):
    return 0.5 * (1.0 + t) + 0.5 * x * (1.0 - t * t) * (_GELU_C * (1.0 + 3.0 * 0.044715 * (x * x)))


def _lx_shifts(lx, head8):
    return [lx] + [_shift_down(lx, d, head8) for d in (1, 2, 3)]


def _lru_gates(lx, head8, cw, cb, wa_ref, ba, wx_ref, bx, ls):
    nb = wa_ref.shape[0]
    sh = _lx_shifts(lx, head8)
    cx = cb + sh[3] * cw[0:1]
    cx = cx + sh[2] * cw[1:2]
    cx = cx + sh[1] * cw[2:3]
    cx = cx + sh[0] * cw[3:4]
    cxb = cx.astype(BF16)
    ra = jnp.concatenate([_dot(cxb[:, n * HEAD_DIM:(n + 1) * HEAD_DIM], wa_ref[n]) for n in range(nb)], axis=1) + ba
    ia = jnp.concatenate([_dot(cxb[:, n * HEAD_DIM:(n + 1) * HEAD_DIM], wx_ref[n]) for n in range(nb)], axis=1) + bx
    r = jax.nn.sigmoid(ra)
    ig = jax.nn.sigmoid(ia)
    log_a = LRU_C * r * ls
    a = jnp.exp(log_a)
    return cx, r, ig, a, jnp.sqrt(_neg_expm1(2.0 * log_a, a * a))


def _lru_specs(tl, DL):
    nb = DL // HEAD_DIM
    vec = pl.BlockSpec((1, DL), lambda i: (0, 0))
    return [pl.BlockSpec((CONV_W, DL), lambda i: (0, 0)), vec,
            pl.BlockSpec((nb, HEAD_DIM, HEAD_DIM), lambda i: (0, 0, 0)), vec,
            pl.BlockSpec((nb, HEAD_DIM, HEAD_DIM), lambda i: (0, 0, 0)), vec, vec]


def _lru_fwd(proj, cw, cb, wa, ba, wx, bx, lam, tl, d_mix, comm=None):
    S = proj.shape[0]
    DL = cb.shape[1]
    tl = _tile(S, tl)

    def body(lx_ref, lg_ref, cw_ref, cb_ref, wa_ref, ba_ref, wx_ref, bx_ref, lam_ref, h_ref, kept_ref, y_ref,
             prev8, hc, a_s, b_s):
        i = pl.program_id(0)

        @pl.when(i == 0)
        def _():
            prev8[...] = jnp.zeros_like(prev8)
            hc[...] = jnp.zeros_like(hc)

        lx = lx_ref[...]
        ls = _log_sigmoid(lam_ref[...])
        kept = _lru_gates(lx, prev8[...], cw_ref[...], cb_ref[...], wa_ref, ba_ref[...], wx_ref, bx_ref[...], ls)
        for n, val in enumerate(kept):
            kept_ref[:, n * DL:(n + 1) * DL] = val
        cx, _, ig, a, mult = kept
        b = mult * (ig * cx)
        row = lax.broadcasted_iota(jnp.int32, a.shape, 0) & 7
        for d in (1, 2, 4):
            a_sh = _roll_in_groups(a, d)
            b_sh = _roll_in_groups(b, d)
            m = row >= d
            b = jnp.where(m, a * b_sh + b, b)
            a = jnp.where(m, a * a_sh, a)
        a_s[...] = a
        b_s[...] = b

        def step(g, hprev):
            sl = pl.ds(pl.multiple_of(g * 8, 8), 8)
            hh = a_s[sl, :] * hprev + b_s[sl, :]
            h_ref[sl, :] = hh
            return hh[7:8, :]

        hc[0:1, :] = lax.fori_loop(0, tl // 8, step, hc[0:1, :])
        prev8[...] = lx[tl - 8:tl]
        g, _ = _gelu(lg_ref[...])
        y_ref[...] = (h_ref[...] * g).astype(BF16)

    return _pcall(
        body, name="lru_fwd", grid=(S // tl,),
        in_specs=[pl.BlockSpec((tl, DL), lambda i: (i, 0)), pl.BlockSpec((tl, DL), lambda i: (i, 1))] + _lru_specs(tl, DL),
        out_specs=[pl.BlockSpec((tl, DL), lambda i: (i, 0)), pl.BlockSpec((tl, 5 * DL), lambda i: (i, 0)),
                   pl.BlockSpec((tl, DL), lambda i: (i, 0))],
        out_shape=[jax.ShapeDtypeStruct((S, DL), F32), jax.ShapeDtypeStruct((S, 5 * DL), F32),
                   jax.ShapeDtypeStruct((S, d_mix), BF16)],
        scratch_shapes=[pltpu.VMEM((8, DL), F32), pltpu.VMEM((8, DL), F32), pltpu.VMEM((tl, DL), F32), pltpu.VMEM((tl, DL), F32)],
        operands=[proj, proj, cw, cb, wa, ba, wx, bx, lam], comm=comm)


def _lru_bwd(proj, h, kept, dy, cw, wa, wx, lam, tl, comm=None):
    S = proj.shape[0]
    DL = lam.shape[1]
    nb = DL // HEAD_DIM
    tl = _tile(S, tl)
    nt = S // tl
    ng = tl // 8
    t8 = tl // 8

    def body(lx_ref, lxp_ref, lg_ref, h_ref, hp_ref, kept_ref, dy_ref, cw_ref, wa_ref, wx_ref, lam_ref,
             dlxg_ref, dcw_ref, dcb_ref, dwa_ref, dba_ref, dwx_ref, dbx_ref, dlam_ref,
             a_next, g_carry, dcx_next, an_s, dh_s, g_s):
        i = pl.program_id(0)

        @pl.when(i == 0)
        def _():
            for ref in (dcw_ref, dcb_ref, dwa_ref, dba_ref, dwx_ref, dbx_ref, dlam_ref, a_next, g_carry, dcx_next):
                ref[...] = jnp.zeros_like(ref)

        first = i == nt - 1
        hv = h_ref[...]
        lg = lg_ref[...]
        dyv = dy_ref[...]
        hhead8 = jnp.where(first, 0.0, hp_ref[...])
        lamv = lam_ref[...]
        ls = _log_sigmoid(lamv)
        cwv = cw_ref[...]
        sh = _lx_shifts(lx_ref[...], jnp.where(first, 0.0, lxp_ref[...]))
        cx, r, ig, a, mult = (kept_ref[:, n * DL:(n + 1) * DL] for n in range(5))
        cxb = cx.astype(BF16)
        hprev = _shift_down(hv, 1, hhead8)
        g, t = _gelu(lg)
        dlg = dyv * hv * _gelu_grad(lg, t)
        dh = dyv * g
        an = _shift_up(a, 1, a_next[...])
        row = lax.broadcasted_iota(jnp.int32, a.shape, 0) & 7
        for d in (1, 2, 4):
            an_sh = _roll_in_groups(an, 8 - d)
            dh_sh = _roll_in_groups(dh, 8 - d)
            m = row + d < 8
            dh = jnp.where(m, an * dh_sh + dh, dh)
            an = jnp.where(m, an * an_sh, an)
        an_s[...] = an
        dh_s[...] = dh

        def step(k, gc):
            sl = pl.ds(pl.multiple_of((ng - 1 - k) * 8, 8), 8)
            gg = an_s[sl, :] * gc + dh_s[sl, :]
            g_s[sl, :] = gg
            return gg[0:1, :]

        g_carry[0:1, :] = lax.fori_loop(0, ng, step, g_carry[0:1, :])
        a_next[...] = a[0:8]
        G = g_s[...]
        da = G * hprev
        icx = ig * cx
        dmult = G * icx
        dicx = G * mult
        di = dicx * cx
        dcx = dicx * ig
        dlog = da * a - dmult * ((a * a) * lax.rsqrt(mult * mult))
        dr = dlog * (LRU_C * ls)
        dlam_ref[...] += jnp.sum(dlog * (LRU_C * r), axis=0, keepdims=True)
        dra = dr * r * (1.0 - r)
        dia = di * ig * (1.0 - ig)
        dba_ref[...] += jnp.sum(dra, axis=0, keepdims=True)
        dbx_ref[...] += jnp.sum(dia, axis=0, keepdims=True)
        drab = dra.astype(BF16)
        diab = dia.astype(BF16)
        back = []
        for n in range(nb):
            cs = slice(n * HEAD_DIM, (n + 1) * HEAD_DIM)
            dwa_ref[n] += _dot(cxb[:, cs], drab[:, cs], TN)
            dwx_ref[n] += _dot(cxb[:, cs], diab[:, cs], TN)
            back.append(_dot(drab[:, cs], wa_ref[n], NT) + _dot(diab[:, cs], wx_ref[n], NT))
        dcx = dcx + jnp.concatenate(back, axis=1)
        dcb_ref[...] += jnp.sum(dcx, axis=0, keepdims=True)
        for tap in range(CONV_W):
            dcw_ref[tap:tap + 1, :] += jnp.sum(dcx * sh[CONV_W - 1 - tap], axis=0, keepdims=True)
        tail = dcx_next[...]
        dlx = dcx * cwv[3:4]
        for d in (1, 2, 3):
            dlx = dlx + _shift_up(dcx, d, tail) * cwv[3 - d:4 - d]
        dcx_next[...] = dcx[0:8]
        dlxg_ref[:, 0:DL] = dlx.astype(BF16)
        dlxg_ref[:, DL:2 * DL] = dlg.astype(BF16)

        @pl.when(i == nt - 1)
        def _():
            dlam_ref[...] = dlam_ref[...] * (1.0 - jax.nn.sigmoid(lamv))

    rev = lambda i: nt - 1 - i
    prev8_map = lambda i: (jnp.maximum((nt - 1 - i) * t8 - 1, 0), 0)
    vec = pl.BlockSpec((1, DL), lambda i: (0, 0))
    mat = pl.BlockSpec((nb, HEAD_DIM, HEAD_DIM), lambda i: (0, 0, 0))
    return _pcall(
        body, name="lru_bwd", grid=(nt,), operands=[proj, proj, proj, h, h, kept, dy, cw, wa, wx, lam], comm=comm,
        in_specs=[pl.BlockSpec((tl, DL), lambda i: (rev(i), 0)), pl.BlockSpec((8, DL), prev8_map),
                  pl.BlockSpec((tl, DL), lambda i: (rev(i), 1)),
                  pl.BlockSpec((tl, DL), lambda i: (rev(i), 0)), pl.BlockSpec((8, DL), prev8_map),
                  pl.BlockSpec((tl, 5 * DL), lambda i: (rev(i), 0)),
                  pl.BlockSpec((tl, DL), lambda i: (rev(i), 0)), pl.BlockSpec((CONV_W, DL), lambda i: (0, 0)), mat, mat, vec],
        out_specs=[pl.BlockSpec((tl, 2 * DL), lambda i: (rev(i), 0)), pl.BlockSpec((CONV_W, DL), lambda i: (0, 0)), vec,
                   mat, vec, mat, vec, vec],
        out_shape=[jax.ShapeDtypeStruct(proj.shape, BF16), jax.ShapeDtypeStruct((CONV_W, DL), F32),
                   jax.ShapeDtypeStruct((1, DL), F32), jax.ShapeDtypeStruct((nb, HEAD_DIM, HEAD_DIM), F32),
                   jax.ShapeDtypeStruct((1, DL), F32), jax.ShapeDtypeStruct((nb, HEAD_DIM, HEAD_DIM), F32),
                   jax.ShapeDtypeStruct((1, DL), F32), jax.ShapeDtypeStruct((1, DL), F32)],
        scratch_shapes=[pltpu.VMEM((8, DL), F32), pltpu.VMEM((8, DL), F32), pltpu.VMEM((8, DL), F32),
                        pltpu.VMEM((tl, DL), F32), pltpu.VMEM((tl, DL), F32), pltpu.VMEM((tl, DL), F32)])


def _ret_tables(S, H):
    pos = jnp.arange(S, dtype=F32)
    inv_freq = ROPE_BASE ** (-jnp.arange(0, HEAD_DIM, 2, dtype=F32) / HEAD_DIM)
    ang = pos[:, None] * inv_freq[None, :]
    cos, sin = jnp.cos(ang), jnp.sin(ang)
    cosf = jnp.concatenate([cos, cos], axis=1)
    sins = jnp.concatenate([-sin, sin], axis=1)
    log_gamma = jnp.log1p(-jnp.exp2(-5.0 - jnp.arange(H, dtype=F32)))
    idx = jnp.arange(CHUNK)
    diff = idx[:, None] - idx[None, :]
    causal = diff >= 0
    decay = jnp.where(causal[None], jnp.exp(log_gamma[:, None, None] * jnp.where(causal, diff, 0)[None].astype(F32)), 0.0)
    zeta = jnp.exp(log_gamma[:, None] * (CHUNK - 1 - idx).astype(F32)[None, :])
    xi = jnp.exp(log_gamma[:, None] * (idx + 1).astype(F32)[None, :])
    gc = jnp.exp(log_gamma * CHUNK)
    lanes = (H, CHUNK, HEAD_DIM)
    return (cosf, sins, decay, jnp.broadcast_to(zeta[:, :, None], lanes), jnp.broadcast_to(xi[:, :, None], lanes),
            jnp.broadcast_to(gc[:, None, None], lanes))


def _rope(t, cos, sin_signed):
    return t * cos + pltpu.roll(t, HEAD_DIM // 2, 1) * sin_signed


def _rope_t(d, cos, sin_signed):
    return d * cos + pltpu.roll(d * sin_signed, HEAD_DIM // 2, 1)


def _ret_const_specs(H, DR):
    full = pl.BlockSpec((H, CHUNK, HEAD_DIM), lambda *_: (0, 0, 0))
    return [full, full, full, full, pl.BlockSpec((1, DR), lambda *_: (0, 0))]


def _ret_fwd(proj, y, tables, gnw, tb, comm=None):
    S = proj.shape[0]
    DR = gnw.shape[1]
    H = DR // HEAD_DIM
    tb = _tile(S, tb, CHUNK)
    nc = tb // CHUNK
    cosf, sins, dm, zeta, xi, gc = tables
    scale = HEAD_DIM ** -0.5

    def body(qk_ref, vg_ref, cos_ref, sin_ref, dm_ref, zeta_ref, xi_ref, gc_ref, gnw_ref, y_in, y_ref, rprev_ref, qkr_ref,
             r_s):
        del y_in
        i = pl.program_id(0)

        @pl.when(i == 0)
        def _():
            r_s[...] = jnp.zeros_like(r_s)

        def chunk(c, carry):
            rows = pl.ds(pl.multiple_of(c * CHUNK, CHUNK), CHUNK)
            cos = cos_ref[rows, :]
            sin = sin_ref[rows, :]
            heads = range(H)
            c0 = [slice(h * HEAD_DIM, (h + 1) * HEAD_DIM) for h in heads]
            c1 = [slice(DR + h * HEAD_DIM, DR + (h + 1) * HEAD_DIM) for h in heads]
            qh = [_rope(qk_ref[rows, c0[h]], cos, sin) for h in heads]
            kh = [_rope(qk_ref[rows, c1[h]], cos, sin) * scale for h in heads]
            vb = [vg_ref[rows, c0[h]].astype(BF16) for h in heads]
            rp = [r_s[h] for h in heads]
            rpb = [rp[h].astype(BF16) for h in heads]
            s = [_dot(qh[h].astype(BF16), kh[h].astype(BF16), NT) for h in heads]
            kv = [_dot((kh[h] * zeta_ref[h]).astype(BF16), vb[h], TN) for h in heads]
            cross = [_dot((qh[h] * xi_ref[h]).astype(BF16), rpb[h]) for h in heads]
            o = [_dot((s[h] * dm_ref[h]).astype(BF16), vb[h]) + cross[h] for h in heads]
            for h in heads:
                rprev_ref[c, h] = rpb[h]
                qkr_ref[rows, c0[h]] = qh[h]
                qkr_ref[rows, c1[h]] = kh[h]
                r_s[h] = rp[h] * gc_ref[h] + kv[h]
                mu = jnp.mean(o[h], axis=-1, keepdims=True)
                oc = o[h] - mu
                var = jnp.mean(oc * oc, axis=-1, keepdims=True)
                on = oc * lax.rsqrt(var + EPS) * gnw_ref[:, c0[h]]
                gate = vg_ref[rows, c1[h]]
                y_ref[rows, c0[h]] = (gate * jax.nn.sigmoid(gate) * on).astype(BF16)
            return carry

        lax.fori_loop(0, nc, chunk, 0)

    return _pcall(
        body, name="ret_fwd", grid=(S // tb,),
        in_specs=[pl.BlockSpec((tb, 2 * DR), lambda i: (i, 1)), pl.BlockSpec((tb, 2 * DR), lambda i: (i, 2)),
                  pl.BlockSpec((tb, HEAD_DIM), lambda i: (i, 0)), pl.BlockSpec((tb, HEAD_DIM), lambda i: (i, 0))]
        + _ret_const_specs(H, DR) + [HBM_SPEC],
        out_specs=[pl.BlockSpec((tb, DR), lambda i: (i, 1)),
                   pl.BlockSpec((nc, H, CHUNK, HEAD_DIM), lambda i: (i, 0, 0, 0)),
                   pl.BlockSpec((tb, 2 * DR), lambda i: (i, 0))],
        out_shape=[jax.ShapeDtypeStruct(y.shape, BF16), jax.ShapeDtypeStruct((S // CHUNK, H, CHUNK, HEAD_DIM), BF16),
                   jax.ShapeDtypeStruct((S, 2 * DR), F32)],
        scratch_shapes=[pltpu.VMEM((H, CHUNK, HEAD_DIM), F32)], aliases={9: 0},
        operands=[proj, proj, cosf, sins, dm, zeta, xi, gc, gnw, y], comm=comm)


def _ret_bwd(proj, qk_rot, rprev, dy, dproj, tables, gnw, tb, comm=None):
    S = proj.shape[0]
    DR = gnw.shape[1]
    H = DR // HEAD_DIM
    tb = _tile(S, tb, CHUNK)
    nc = tb // CHUNK
    nt = S // tb
    cosf, sins, dm, zeta, xi, gc = tables
    scale = HEAD_DIM ** -0.5

    def body(qk_ref, vg_ref, cos_ref, sin_ref, dm_ref, zeta_ref, xi_ref, gc_ref, gnw_ref, rprev_ref, dy_ref, dp_in,
             dp_ref, dgn_ref, dr_s, dqk_s, dvg_s, out_sems):
        del dp_in
        i = pl.program_id(0)
        slot = i % 2

        def out_copies(step, sl):
            rows = pl.ds(pl.multiple_of((nt - 1 - step) * tb, tb), tb)
            return (pltpu.make_async_copy(dqk_s.at[sl], dp_ref.at[rows, pl.ds(2 * DR, 2 * DR)], out_sems.at[sl, 0]),
                    pltpu.make_async_copy(dvg_s.at[sl], dp_ref.at[rows, pl.ds(4 * DR, 2 * DR)], out_sems.at[sl, 1]))

        @pl.when(i == 0)
        def _():
            dr_s[...] = jnp.zeros_like(dr_s)
            dgn_ref[...] = jnp.zeros_like(dgn_ref)

        @pl.when(i >= 2)
        def _():
            for cp in out_copies(i - 2, slot):
                cp.wait()

        def chunk(cc, carry):
            c = nc - 1 - cc
            rows = pl.ds(pl.multiple_of(c * CHUNK, CHUNK), CHUNK)
            cos = cos_ref[rows, :]
            sin = sin_ref[rows, :]
            heads = range(H)
            c0 = [slice(h * HEAD_DIM, (h + 1) * HEAD_DIM) for h in heads]
            c1 = [slice(DR + h * HEAD_DIM, DR + (h + 1) * HEAD_DIM) for h in heads]
            qh = [qk_ref[rows, c0[h]] for h in heads]
            kh = [qk_ref[rows, c1[h]] for h in heads]
            qb = [t.astype(BF16) for t in qh]
            kb = [t.astype(BF16) for t in kh]
            vb = [vg_ref[rows, c0[h]].astype(BF16) for h in heads]
            rpb = [rprev_ref[c, h] for h in heads]
            qx = [(qh[h] * xi_ref[h]).astype(BF16) for h in heads]
            kz = [(kh[h] * zeta_ref[h]).astype(BF16) for h in heads]
            drh = [dr_s[h] for h in heads]
            drb = [t.astype(BF16) for t in drh]
            s = [_dot(qb[h], kb[h], NT) for h in heads]
            cross = [_dot(qx[h], rpb[h]) for h in heads]
            dv_state = [_dot(kz[h], drb[h]) for h in heads]
            dk_state = [_dot(vb[h], drb[h], NT) for h in heads]
            sb = [(s[h] * dm_ref[h]).astype(BF16) for h in heads]
            o = [_dot(sb[h], vb[h]) + cross[h] for h in heads]
            dob = []
            for h in heads:
                mu = jnp.mean(o[h], axis=-1, keepdims=True)
                oc = o[h] - mu
                rstd = lax.rsqrt(jnp.mean(oc * oc, axis=-1, keepdims=True) + EPS)
                ohat = oc * rstd
                gw = gnw_ref[:, c0[h]]
                gate = vg_ref[rows, c1[h]]
                sg = jax.nn.sigmoid(gate)
                dyv = dy_ref[rows, c0[h]]
                dvg_s[slot, rows, c1[h]] = (dyv * (ohat * gw) * (sg * (1.0 + gate * (1.0 - sg)))).astype(BF16)
                don = dyv * (gate * sg)
                dgn_ref[:, c0[h]] += jnp.sum(don * ohat, axis=0, keepdims=True)
                dohat = don * gw
                do = rstd * (dohat - jnp.mean(dohat, axis=-1, keepdims=True)
                             - ohat * jnp.mean(dohat * ohat, axis=-1, keepdims=True))
                dob.append(do.astype(BF16))
            ds = [_dot(dob[h], vb[h], NT) for h in heads]
            dq_state = [_dot(dob[h], rpb[h], NT) for h in heads]
            dv = [_dot(sb[h], dob[h], TN) + dv_state[h] for h in heads]
            dr_new = [_dot(qx[h], dob[h], TN) for h in heads]
            dsb = [(ds[h] * dm_ref[h]).astype(BF16) for h in heads]
            dqh = [_dot(dsb[h], kb[h]) + dq_state[h] * xi_ref[h] for h in heads]
            dkh = [_dot(dsb[h], qb[h], TN) + dk_state[h] * zeta_ref[h] for h in heads]
            for h in heads:
                dr_s[h] = drh[h] * gc_ref[h] + dr_new[h]
                dqk_s[slot, rows, c0[h]] = _rope_t(dqh[h], cos, sin).astype(BF16)
                dqk_s[slot, rows, c1[h]] = _rope_t(dkh[h] * scale, cos, sin).astype(BF16)
                dvg_s[slot, rows, c0[h]] = dv[h].astype(BF16)
            return carry

        lax.fori_loop(0, nc, chunk, 0)
        for cp in out_copies(i, slot):
            cp.start()

        @pl.when(i == nt - 1)
        def _():
            if nt >= 2:
                for cp in out_copies(i - 1, 1 - slot):
                    cp.wait()
            for cp in out_copies(i, slot):
                cp.wait()

    rev = lambda i: nt - 1 - i
    return _pcall(
        body, name="ret_bwd", grid=(nt,), aliases={11: 0}, comm=comm,
        operands=[qk_rot, proj, cosf, sins, dm, zeta, xi, gc, gnw, rprev, dy, dproj],
        in_specs=[pl.BlockSpec((tb, 2 * DR), lambda i: (rev(i), 0)), pl.BlockSpec((tb, 2 * DR), lambda i: (rev(i), 2)),
                  pl.BlockSpec((tb, HEAD_DIM), lambda i: (rev(i), 0)), pl.BlockSpec((tb, HEAD_DIM), lambda i: (rev(i), 0))]
        + _ret_const_specs(H, DR)
        + [pl.BlockSpec((nc, H, CHUNK, HEAD_DIM), lambda i: (rev(i), 0, 0, 0)),
           pl.BlockSpec((tb, DR), lambda i: (rev(i), 1)), HBM_SPEC],
        out_specs=[HBM_SPEC, pl.BlockSpec((1, DR), lambda i: (0, 0))],
        out_shape=[jax.ShapeDtypeStruct(dproj.shape, BF16), jax.ShapeDtypeStruct((1, DR), F32)],
        scratch_shapes=[pltpu.VMEM((H, CHUNK, HEAD_DIM), F32), pltpu.VMEM((2, tb, 2 * DR), BF16),
                        pltpu.VMEM((2, tb, 2 * DR), BF16), pltpu.SemaphoreType.DMA((2, 2))])


def _place():
    x, y, c = lax.axis_index("x"), lax.axis_index("y"), lax.axis_index("c")
    chips = [(1 - x, y), (x, 1 - y), (1 - x, 1 - y)]
    return x, y, c, chips


def _own_slab(name, shard, place):
    R, C = shard.shape
    tr = _row_tile(R, C)
    return _ew("cast_" + name, lambda a: (a,), [(shard, pl.BlockSpec((tr, C), lambda i, p: (i, 0)))],
               [(jax.ShapeDtypeStruct((4, R, C), BF16), pl.BlockSpec((None, tr, C), lambda i, p: (p[1], i, 0)))],
               (R // tr,), sp=place)[0]


class _remote:
    def __init__(self, src, dst, ssem, rsem, k, to):
        self.args = dict(src_ref=src, dst_ref=dst, send_sem=ssem.at[k], recv_sem=rsem.at[k], device_id=to,
                         device_id_type=MESH)

    def start(self):
        pltpu.make_async_remote_copy(**self.args).start()

    def wait_send(self):
        pltpu.make_async_remote_copy(**self.args).wait_send()

    def wait_recv(self):
        pltpu.make_async_remote_copy(**self.args).wait_recv()


def _task_fns(copies):
    def start(cins, couts, ssem, rsem, base):
        for cp in copies(cins, couts, ssem, rsem, base)[0]:
            cp.start()

    def finish(cins, couts, ssem, rsem, base):
        sends, recvs = copies(cins, couts, ssem, rsem, base)
        for cp in sends:
            cp.wait_send()
        for cp in recvs:
            cp.wait_recv()

    return start, finish


NEIGHBOURS, DIAGONAL = (0, 1), (2,)


def _gather_ici(st, which=NEIGHBOURS + DIAGONAL):
    r2 = st.shape[1] // 2

    def copies(cins, couts, ssem, rsem, base):
        x, y, c, chips = _place()
        out = couts[0]
        mine = out.at[2 * x + y, pl.ds(c * r2, r2), :]
        sends, recvs = [], []
        for k, j in enumerate(which):
            cx, cy = chips[j]
            got = out.at[2 * cx + cy, pl.ds(c * r2, r2), :]
            sends.append(_remote(mine, mine, ssem, rsem, base + k, (cx, cy, c)))
            recvs.append(_remote(got, got, ssem, rsem, base + k, (x, y, c)))
        return sends, recvs

    start, finish = _task_fns(copies)
    return _Comm([st], [jax.ShapeDtypeStruct(st.shape, st.dtype)], {0: 0}, len(which), start, finish)


def _gather_d2d(st):
    r2 = st.shape[1] // 2

    def copies(cins, couts, ssem, rsem, base):
        x, y, c, chips = _place()
        out = couts[0]
        sends, recvs = [], []
        for j, (cx, cy) in enumerate(chips):
            have = out.at[2 * cx + cy, pl.ds(c * r2, r2), :]
            want = out.at[2 * cx + cy, pl.ds((1 - c) * r2, r2), :]
            sends.append(_remote(have, have, ssem, rsem, base + j, (x, y, 1 - c)))
            recvs.append(_remote(want, want, ssem, rsem, base + j, (x, y, c)))
        return sends, recvs

    start, finish = _task_fns(copies)
    return _Comm([st], [jax.ShapeDtypeStruct(st.shape, st.dtype)], {0: 0}, 3, start, finish)


def _gather_conv(conv_w):
    def copies(cins, couts, ssem, rsem, base):
        x, y, c, chips = _place()
        src, out = cins[0], couts[0]
        sends = [_remote(src, out.at[2 * x + y], ssem, rsem, base + j, (*chip, c)) for j, chip in enumerate(chips)]
        recvs = [_remote(src, out.at[2 * cx + cy], ssem, rsem, base + j, (x, y, c)) for j, (cx, cy) in enumerate(chips)]
        return sends, recvs

    start, finish = _task_fns(copies)
    return _Comm([conv_w], [jax.ShapeDtypeStruct((4,) + conv_w.shape, conv_w.dtype)], {}, 3, start, finish)


def _pair_exchange(g):
    r2 = g.shape[1] // 2

    def copies(cins, couts, ssem, rsem, base):
        x, y, c, _ = _place()
        cp = _remote(cins[0].at[:, pl.ds((1 - c) * r2, r2), :], couts[0], ssem, rsem, base, (x, y, 1 - c))
        return [cp], [cp]

    start, finish = _task_fns(copies)
    return _Comm([g], [jax.ShapeDtypeStruct((g.shape[0], r2, g.shape[2]), g.dtype)], {}, 1, start, finish)


def _chip_exchange(part):
    def copies(cins, couts, ssem, rsem, base):
        x, y, c, chips = _place()
        cps = [_remote(cins[0].at[2 * cx + cy], couts[0].at[j], ssem, rsem, base + j, (cx, cy, c))
               for j, (cx, cy) in enumerate(chips)]
        return cps, cps

    start, finish = _task_fns(copies)
    return _Comm([part], [jax.ShapeDtypeStruct((3,) + part.shape[1:], part.dtype)], {}, 3, start, finish)


def _pair_share(slot):
    def copies(cins, couts, ssem, rsem, base):
        x, y, c, _ = _place()
        out = couts[0]
        return ([_remote(out.at[c], out.at[c], ssem, rsem, base, (x, y, 1 - c))],
                [_remote(out.at[1 - c], out.at[1 - c], ssem, rsem, base, (x, y, c))])

    start, finish = _task_fns(copies)
    return _Comm([slot], [jax.ShapeDtypeStruct(slot.shape, slot.dtype)], {0: 0}, 1, start, finish)


def _gather_small(sm):
    flips = [(fx, fy, fc) for fx in (0, 1) for fy in (0, 1) for fc in (0, 1)][1:]

    def copies(cins, couts, ssem, rsem, base):
        x, y, c, _ = _place()
        src, out = cins[0], couts[0]
        peers = [(1 - x if fx else x, 1 - y if fy else y, 1 - c if fc else c) for fx, fy, fc in flips]
        sends = [_remote(src, out.at[4 * x + 2 * y + c], ssem, rsem, base + k, peer) for k, peer in enumerate(peers)]
        recvs = [_remote(src, out.at[4 * px + 2 * py + pc], ssem, rsem, base + k, (x, y, c))
                 for k, (px, py, pc) in enumerate(peers)]
        return sends, recvs

    start, finish = _task_fns(copies)
    return _Comm([sm], [jax.ShapeDtypeStruct((8,) + sm.shape, sm.dtype)], {}, 7, start, finish)


def _comm_call(name, tasks):
    task = _merge(tasks)
    nci = len(task.ins)

    def body(*refs):
        cins, couts, (ssem, rsem) = refs[:nci], refs[nci:nci + len(task.outs)], refs[nci + len(task.outs):]
        task.start(cins, couts, ssem, rsem, 0)
        task.finish(cins, couts, ssem, rsem, 0)

    return pl.pallas_call(
        body, in_specs=[HBM_SPEC] * nci, out_specs=[HBM_SPEC] * len(task.outs), out_shape=list(task.outs),
        scratch_shapes=[pltpu.SemaphoreType.DMA((task.n_sem,)), pltpu.SemaphoreType.DMA((task.n_sem,))],
        input_output_aliases=task.aliases, name=name)(*task.ins)


def _adamw(w, g, m, v):
    m = ADAM_B1 * m + (1.0 - ADAM_B1) * g
    v = ADAM_B2 * v + (1.0 - ADAM_B2) * (g * g)
    m_hat = m / (1.0 - ADAM_B1 ** ADAM_STEP)
    v_hat = v / (1.0 - ADAM_B2 ** ADAM_STEP)
    delta = -ADAM_LR * (m_hat / (jnp.sqrt(v_hat) + ADAM_EPS) + ADAM_WD * w)
    return delta, m, v


def _adamw_call(name, w, g, m, v):
    R, C = w.shape
    tr = _row_tile(R, C, 1024 * 1024)
    row = pl.BlockSpec((tr, C), lambda i: (i, 0))
    o = jax.ShapeDtypeStruct((R, C), F32)
    return _ew(name, lambda w_, g_, m_, v_: (*_adamw(w_, g_, m_, v_), g_), [(w, row), (g, row), (m, row), (v, row)],
               [(o, row), (o, row), (o, row), (o, row)], (R // tr,))


def _pair_sum(name, g, ra, place):
    _, R, C = g.shape
    r2 = R // 2
    tr = _row_tile(r2, C)
    nb = r2 // tr
    own = pl.BlockSpec((None, tr, C), lambda j, i, p: (j, p[0] * nb + i, 0))
    blk = pl.BlockSpec((None, tr, C), lambda j, i, p: (j, i, 0))
    return _ew("rs_pair_sum_" + name, lambda a, b: (a + b,), [(g, own), (ra, blk)],
               [(jax.ShapeDtypeStruct((4, r2, C), BF16), blk)], (4, nb), sp=place)[0]


def _chip_sum(name, g, ra, rb, place):
    _, R, C = g.shape
    r2 = R // 2
    tr = _row_tile(r2, C)
    nb = r2 // tr
    own = pl.BlockSpec((None, tr, C), lambda i, p: (p[1], p[0] * nb + i, 0))
    mine = pl.BlockSpec((None, tr, C), lambda i, p: (p[1], i, 0))
    src = [pl.BlockSpec((None, tr, C), functools.partial(lambda i, p, j: (j, i, 0), j=j)) for j in range(3)]
    out = pl.BlockSpec((None, tr, C), lambda i, p: (p[0], i, 0))

    def total(a, b, r0, r1, r2_):
        return ((((a + b) + r0.astype(F32)) + r1.astype(F32)) + r2_.astype(F32),)

    return _ew("rs_chip_sum_" + name, total, [(g, own), (ra, mine), (rb, src[0]), (rb, src[1]), (rb, src[2])],
               [(jax.ShapeDtypeStruct((2, r2, C), F32), out)], (nb,), sp=place)[0]


def _pack(arrays):
    rows, offs, pos = [], [], 0
    for a in arrays:
        flat = a.reshape(-1)
        n = -(-flat.shape[0] // (8 * LANES)) * (8 * LANES)
        if n != flat.shape[0]:
            flat = jnp.pad(flat, (0, n - flat.shape[0]))
        rows.append(flat.reshape(-1, LANES))
        offs.append(pos)
        pos += n // LANES
    return jnp.concatenate(rows, axis=0), offs


def _unpack(packed, offs, shapes):
    out = []
    for off, shp in zip(offs, shapes):
        n = 1
        for s in shp:
            n *= s
        out.append(packed[off:off + -(-n // LANES)].reshape(-1)[:n].reshape(shp))
    return out


def _sum8(gathered):
    _, R, C = gathered.shape
    tr = _row_tile(R, C, 256 * 1024)
    specs = [pl.BlockSpec((None, tr, C), functools.partial(lambda i, d: (d, i, 0), d=d)) for d in range(8)]

    def fn(*parts):
        t = parts[0]
        for p in parts[1:]:
            t = t + p
        return (t,)

    return _ew("small_sum", fn, [(gathered, s) for s in specs],
               [(jax.ShapeDtypeStruct((R, C), F32), pl.BlockSpec((tr, C), lambda i: (i, 0)))], (R // tr,))[0]


BIG = ("w_in", "w_out", "w_ffn_gate", "w_ffn_up", "w_ffn_down")
SMALL = ("ln1_w", "conv_w", "conv_b", "gate_a_w", "gate_a_b", "gate_x_w", "gate_x_b", "lru_lambda", "ret_gn_w", "ln2_w",
         "final_norm_w")
WEIGHTS = ("ln1_w", "w_in", "conv_w", "conv_b", "gate_a_w", "gate_a_b", "gate_x_w", "gate_x_b", "lru_lambda", "ret_gn_w",
           "w_out", "ln2_w", "w_ffn_gate", "w_ffn_up", "w_ffn_down", "final_norm_w")


def kernel(x, ln1_w, w_in, conv_w, conv_b, gate_a_w, gate_a_b, gate_x_w, gate_x_b, lru_lambda, ret_gn_w, w_out, ln2_w, w_ffn_gate, w_ffn_up, w_ffn_down, final_norm_w, loss_target, m_ln1_w, m_w_in, m_conv_w, m_conv_b, m_gate_a_w, m_gate_a_b, m_gate_x_w, m_gate_x_b, m_lru_lambda, m_ret_gn_w, m_w_out, m_ln2_w, m_w_ffn_gate, m_w_ffn_up, m_w_ffn_down, m_final_norm_w, v_ln1_w, v_w_in, v_conv_w, v_conv_b, v_gate_a_w, v_gate_a_b, v_gate_x_w, v_gate_x_b, v_lru_lambda, v_ret_gn_w, v_w_out, v_ln2_w, v_w_ffn_gate, v_w_ffn_up, v_w_ffn_down, v_final_norm_w):
    w = dict(ln1_w=ln1_w, w_in=w_in, conv_w=conv_w, conv_b=conv_b, gate_a_w=gate_a_w, gate_a_b=gate_a_b, gate_x_w=gate_x_w,
             gate_x_b=gate_x_b, lru_lambda=lru_lambda, ret_gn_w=ret_gn_w, w_out=w_out, ln2_w=ln2_w, w_ffn_gate=w_ffn_gate,
             w_ffn_up=w_ffn_up, w_ffn_down=w_ffn_down, final_norm_w=final_norm_w)
    m = dict(ln1_w=m_ln1_w, w_in=m_w_in, conv_w=m_conv_w, conv_b=m_conv_b, gate_a_w=m_gate_a_w, gate_a_b=m_gate_a_b,
             gate_x_w=m_gate_x_w, gate_x_b=m_gate_x_b, lru_lambda=m_lru_lambda, ret_gn_w=m_ret_gn_w, w_out=m_w_out,
             ln2_w=m_ln2_w, w_ffn_gate=m_w_ffn_gate, w_ffn_up=m_w_ffn_up, w_ffn_down=m_w_ffn_down,
             final_norm_w=m_final_norm_w)
    v = dict(ln1_w=v_ln1_w, w_in=v_w_in, conv_w=v_conv_w, conv_b=v_conv_b, gate_a_w=v_gate_a_w, gate_a_b=v_gate_a_b,
             gate_x_w=v_gate_x_w, gate_x_b=v_gate_x_b, lru_lambda=v_lru_lambda, ret_gn_w=v_ret_gn_w, w_out=v_w_out,
             ln2_w=v_ln2_w, w_ffn_gate=v_w_ffn_gate, w_ffn_up=v_w_ffn_up, w_ffn_down=v_w_ffn_down,
             final_norm_w=v_final_norm_w)
    xs, tgt = x[0], loss_target[0]
    S, D = xs.shape
    DL, DR = conv_b.shape[1], ret_gn_w.shape[1]
    assert DL == DR and DL % HEAD_DIM == 0 and S % CHUNK == 0
    d_mix = DL + DR
    cx, cy, cc = lax.axis_index("x"), lax.axis_index("y"), lax.axis_index("c")
    chip = 2 * cx + cy
    place = jnp.stack([cc, chip]).astype(jnp.int32)
    grad, delta, new_m, new_v = {}, {}, {}, {}

    def finish_big(n, full):
        shp = w[n].shape
        g2 = full.reshape(shp[1], shp[2])
        w2, m2, v2 = (t[n].reshape(shp[1], shp[2]) for t in (w, m, v))
        d_, m_, v_, g_ = _adamw_call("adamw_" + n, w2, g2, m2, v2)
        grad[n], delta[n], new_m[n], new_v[n] = (t.reshape(shp) for t in (g_, d_, m_, v_))

    def all_sum(gathered, own):
        return _sum8(lax.dynamic_update_slice(gathered, own[None], (4 * cx + 2 * cy + cc, 0, 0)))

    st = {n: _own_slab(n, w[n][0], place) for n in BIG}
    TM, TK = 512, 2048
    (u1,), (w_in_st,) = _rms_fwd("rms1", xs, ln1_w, TM, comm=_gather_ici(st["w_in"]))
    w_in_st, conv_st = _comm_call("gather_w_in", [_gather_d2d(w_in_st), _gather_conv(conv_w[0])])
    conv_st = lax.dynamic_update_slice(conv_st, conv_w, (chip, 0, 0))
    cw_cols = conv_st.shape[2]
    conv_full = jnp.transpose(conv_st, (1, 0, 2)).reshape(CONV_W, 4 * cw_cols)
    n_in, n_ff = w_in_st.shape[2], st["w_ffn_gate"].shape[2]
    tables = _ret_tables(S, DR // HEAD_DIM)
    wab, wxb = gate_a_w[0].astype(BF16), gate_x_w[0].astype(BF16)
    lru_w = (conv_full, conv_b, wab, gate_a_b, wxb, gate_x_b, lru_lambda)

    proj, (w_out_st, wg_st) = _mm_nn_stacked("proj", u1, w_in_st, F32, 2 * TM,
                                             comm=_merge([_gather_ici(st["w_out"]), _gather_ici(st["w_ffn_gate"])]))
    (hs, kept, y), (w_out_st, wg_st, wu_st) = _lru_fwd(
        proj, *lru_w, 128, d_mix, comm=_merge([_gather_d2d(w_out_st), _gather_d2d(wg_st), _gather_ici(st["w_ffn_up"])]))
    (y, rprev, qk_rot), (wu_st, wd_st) = _ret_fwd(proj, y, tables, ret_gn_w, 256,
                                          comm=_merge([_gather_d2d(wu_st), _gather_ici(st["w_ffn_down"], NEIGHBOURS)]))
    w_out_f = w_out_st.reshape(d_mix, D)
    (h1, u2), (wd_st,) = _out_proj_rms(y, w_out_f, xs, ln2_w, TM, comm=_gather_ici(wd_st, DIAGONAL))
    (dg_fac, du_fac, ff), (wd_st,) = _ffn_gate_up(u2, wg_st, wu_st, TM, comm=_gather_d2d(wd_st))
    wd_f = wd_st.reshape(4 * n_ff, D)
    dh2, dh2b, d_fw, loss = _ffn_down_loss(ff, wd_f, h1, tgt, final_norm_w.reshape(1, D), 256)

    g_wd = _mm_tn("g_w_down", ff, dh2b, n_ff, 1024, TK).reshape(4, n_ff, D)
    (dgt, dup), (ra_wd,) = _ffn_gate_up_bwd(dh2b, wd_f, dg_fac, du_fac, 2 * TM, n_ff, comm=_pair_exchange(g_wd))
    pb_wd = _pair_sum("w_ffn_down", g_wd, ra_wd, place)
    g_wg, (rb_wd,) = _mm_tn("g_w_gate", u2, dgt, 1024, None, TK, stacked_cols=n_ff, comm=_chip_exchange(pb_wd))
    slot_wd = _chip_sum("w_ffn_down", g_wd, ra_wd, rb_wd, place)
    g_wu, (full_wd, ra_wg) = _mm_tn("g_w_up", u2, dup, 1024, None, TK, stacked_cols=n_ff,
                                    comm=_merge([_pair_share(slot_wd), _pair_exchange(g_wg)]))
    finish_big("w_ffn_down", full_wd)
    pb_wg = _pair_sum("w_ffn_gate", g_wg, ra_wg, place)
    du2, (rb_wg,) = _mm_nt_stacked("d_u2_gate", dgt, wg_st, TM, comm=_chip_exchange(pb_wg))
    du2, (ra_wu,) = _mm_nt_stacked("d_u2_up", dup, wu_st, TM, res=du2, comm=_pair_exchange(g_wu))
    slot_wg = _chip_sum("w_ffn_gate", g_wg, ra_wg, rb_wg, place)
    pb_wu = _pair_sum("w_ffn_up", g_wu, ra_wu, place)
    (dh1, dh1b, dy, d_ln2), (full_wg,) = _rms_bwd_dy(h1, ln2_w, du2, dh2, w_out_f, 256, comm=_pair_share(slot_wg))
    finish_big("w_ffn_gate", full_wg)
    g_wout = _mm_tn("g_w_out", y, dh1b, 1024, 1024, TK).reshape(4, d_mix // 4, D)
    (dproj, d_cw, d_cb, d_wa, d_ba, d_wx, d_bx, d_lam), (rb_wu, ra_wout) = _lru_bwd(
        proj, hs, kept, dy, conv_full, wab, wxb, lru_lambda, 128,
        comm=_merge([_chip_exchange(pb_wu), _pair_exchange(g_wout)]))
    slot_wu = _chip_sum("w_ffn_up", g_wu, ra_wu, rb_wu, place)
    pb_wout = _pair_sum("w_out", g_wout, ra_wout, place)
    (dproj, d_gn), (full_wu, rb_wout) = _ret_bwd(proj, qk_rot, rprev, dy, dproj, tables, ret_gn_w, 256,
                                                 comm=_merge([_pair_share(slot_wu), _chip_exchange(pb_wout)]))
    finish_big("w_ffn_up", full_wu)
    slot_wout = _chip_sum("w_out", g_wout, ra_wout, rb_wout, place)
    small = dict(conv_w=d_cw, conv_b=d_cb, gate_a_w=d_wa, gate_a_b=d_ba, gate_x_w=d_wx, gate_x_b=d_bx, lru_lambda=d_lam,
                 ret_gn_w=d_gn, ln2_w=d_ln2, final_norm_w=d_fw)
    packed, offs = _pack([small[n] for n in SMALL[1:]] + [loss])
    g_win, (full_wout, got_small) = _mm_tn("g_w_in", u1, dproj, 1024, None, TK, stacked_cols=n_in,
                                           comm=_merge([_pair_share(slot_wout), _gather_small(packed)]))
    finish_big("w_out", full_wout)
    (ra_win,) = _comm_call("rs_pair_w_in", [_pair_exchange(g_win)])
    pb_win = _pair_sum("w_in", g_win, ra_win, place)
    du1, (rb_win,) = _mm_nt_stacked("d_u1", dproj, w_in_st, TM, comm=_chip_exchange(pb_win))
    slot_win = _chip_sum("w_in", g_win, ra_win, rb_win, place)
    gx, d_ln1 = _rms_bwd("rms1_bwd", xs, ln1_w, du1, dh1, TM)
    packed1, _ = _pack([d_ln1])
    full_win, got_ln1 = _comm_call("reduce_tail", [_pair_share(slot_win), _gather_small(packed1)])
    finish_big("w_in", full_win)

    red = _unpack(all_sum(got_small, packed), offs, [small[n].shape for n in SMALL[1:]] + [(1, LANES)])
    g = dict(zip(SMALL[1:], red[:-1]))
    g["ln1_w"] = all_sum(got_ln1, packed1)[:-(-D // LANES)].reshape(1, D)
    loss_out = red[-1][0, 0]
    g["conv_w"] = lax.dynamic_slice(g["conv_w"], (0, chip * cw_cols), (CONV_W, cw_cols))
    packs = [_pack([t[n] for n in SMALL])[0] for t in (w, m, v)]
    gp, offs2 = _pack([g[n] for n in SMALL])
    outs = _adamw_call("adamw_small", packs[0], gp, packs[1], packs[2])
    shapes = [w[n].shape for n in SMALL]
    for dst, arr in zip((delta, new_m, new_v), outs):
        dst.update(zip(SMALL, _unpack(arr, offs2, shapes)))
    for n in SMALL:
        grad[n] = g[n].reshape(w[n].shape)

    return (loss_out, gx.reshape(x.shape), *[grad[n] for n in WEIGHTS], *[delta[n] for n in WEIGHTS],
            *[new_m[n] for n in WEIGHTS], *[new_v[n] for n in WEIGHTS])
```

```python
import functools

import jax
import jax.numpy as jnp
from jax import lax
from jax.experimental import pallas as pl
from jax.experimental.pallas import tpu as pltpu

F32 = jnp.float32
BF16 = jnp.bfloat16
MESH = pl.DeviceIdType.MESH

EPS = 1e-6
LRU_C = 8.0
ROPE_BASE = 10000.0
CHUNK = 128
HEAD_DIM = 128
CONV_W = 4
ADAM_LR = 0.001
ADAM_B1 = 0.9
ADAM_B2 = 0.999
ADAM_EPS = 1e-08
ADAM_WD = 0.01
ADAM_STEP = 10

V7X_VMEM_BYTES = 64 * 1024 * 1024
VMEM_LIMIT = V7X_VMEM_BYTES - 8 * 1024 * 1024
LANES = 128
SUBLANES_16BIT = 16

TM = 512
TM_WIDE = 1024
TM_RESIDENT = 256
TK_GRAD = 2048
TILE_GRAD = 1024
LRU_TILE = 128
RET_BLOCK = 256

NN = (((1,), (0,)), ((), ()))
NT = (((1,), (1,)), ((), ()))
TN = (((0,), (0,)), ((), ()))


def _dot(a, b, dims=NN):
    return lax.dot_general(a, b, dims, preferred_element_type=F32)


def _tile(n, pref, mult=SUBLANES_16BIT):
    best = None
    t = mult
    while t <= min(n, pref):
        if n % t == 0:
            best = t
        t += mult
    return best if best is not None else n


def _row_tile(rows, cols, budget_bytes=2 * 1024 * 1024):
    return _tile(rows, max(SUBLANES_16BIT, budget_bytes // (cols * 4)))


def _params(sem):
    return pltpu.CompilerParams(dimension_semantics=sem, vmem_limit_bytes=VMEM_LIMIT)


HBM_SPEC = pl.BlockSpec(memory_space=pl.ANY)


class _Comm:
    def __init__(self, ins, outs, aliases, n_sem, start, finish):
        self.ins, self.outs, self.aliases, self.n_sem, self.start, self.finish = ins, outs, aliases, n_sem, start, finish


def _merge(tasks):
    ins, outs, aliases, plans, n_sem = [], [], {}, [], 0
    for t in tasks:
        i0, o0 = len(ins), len(outs)
        plans.append((t, i0, o0, n_sem))
        ins += t.ins
        outs += t.outs
        aliases.update({i0 + a: o0 + b for a, b in t.aliases.items()})
        n_sem += t.n_sem

    def run(which):
        def go(cins, couts, ssem, rsem, base):
            for t, i0, o0, s0 in plans:
                getattr(t, which)(cins[i0:i0 + len(t.ins)], couts[o0:o0 + len(t.outs)], ssem, rsem, base + s0)
        return go

    return _Comm(ins, outs, aliases, n_sem, run("start"), run("finish"))


def _pcall(body, *, name, grid, in_specs, out_specs, out_shape, operands, scratch_shapes=(), aliases=None, comm=None):
    n_in, n_out, n_scr = len(operands), len(out_shape), len(scratch_shapes)
    aliases = dict(aliases or {})
    params = _params(("arbitrary",) * len(grid))
    if comm is None:
        return pl.pallas_call(body, grid=grid, in_specs=list(in_specs), out_specs=list(out_specs), out_shape=list(out_shape),
                              scratch_shapes=list(scratch_shapes), input_output_aliases=aliases, name=name,
                              compiler_params=params)(*operands)
    nci, nco = len(comm.ins), len(comm.outs)

    def wrapped(*refs):
        ins, cins = refs[:n_in], refs[n_in:n_in + nci]
        o0 = n_in + nci
        outs, couts = refs[o0:o0 + n_out], refs[o0 + n_out:o0 + n_out + nco]
        s0 = o0 + n_out + nco
        scr, (ssem, rsem) = refs[s0:s0 + n_scr], refs[s0 + n_scr:]
        ids = [pl.program_id(a) for a in range(len(grid))]
        first = functools.reduce(jnp.logical_and, [i == 0 for i in ids])
        last = functools.reduce(jnp.logical_and, [i == g - 1 for i, g in zip(ids, grid)])

        @pl.when(first)
        def _():
            comm.start(cins, couts, ssem, rsem, 0)

        body(*ins, *outs, *scr)

        @pl.when(last)
        def _():
            comm.finish(cins, couts, ssem, rsem, 0)

    aliases.update({n_in + a: n_out + b for a, b in comm.aliases.items()})
    res = pl.pallas_call(
        wrapped, grid=grid, in_specs=list(in_specs) + [HBM_SPEC] * nci, out_specs=list(out_specs) + [HBM_SPEC] * nco,
        out_shape=list(out_shape) + list(comm.outs),
        scratch_shapes=list(scratch_shapes) + [pltpu.SemaphoreType.DMA((comm.n_sem,)), pltpu.SemaphoreType.DMA((comm.n_sem,))],
        input_output_aliases=aliases, name=name, compiler_params=params)(*operands, *comm.ins)
    return res[:n_out], res[n_out:]


def _ew(name, fn, ins, outs, grid, sp=None):
    n_in = len(ins)

    def body(*refs):
        if sp is not None:
            refs = refs[1:]
        vals = [r[...] for r in refs[:n_in]]
        res = fn(*vals)
        for o_ref, v in zip(refs[n_in:], res):
            o_ref[...] = v.astype(o_ref.dtype)

    in_specs = [s for _, s in ins]
    out_specs = [s for _, s in outs]
    out_shape = [s for s, _ in outs]
    sem = ("arbitrary",) * len(grid)
    if sp is None:
        return pl.pallas_call(body, grid=grid, in_specs=in_specs, out_specs=out_specs, out_shape=out_shape,
                              name=name, compiler_params=_params(sem))(*[a for a, _ in ins])
    gs = pltpu.PrefetchScalarGridSpec(num_scalar_prefetch=1, grid=grid, in_specs=in_specs, out_specs=out_specs)
    return pl.pallas_call(body, grid_spec=gs, out_shape=out_shape, name=name,
                          compiler_params=_params(sem))(sp, *[a for a, _ in ins])


def _matmul(name, pairs, dims, grid, out_shape, out_spec, acc_shape, res=None, comm=None):
    n = len(pairs)
    nk = grid[2]

    def body(*refs):
        ab = refs[:2 * n]
        pos = 2 * n
        res_ref = None
        if res is not None:
            res_ref = refs[pos]
            pos += 1
        o_ref = refs[pos]
        acc_ref = refs[pos + 1] if nk > 1 else None

        def partial():
            t = None
            for p in range(n):
                d = _dot(ab[2 * p][...], ab[2 * p + 1][...], dims)
                t = d if t is None else t + d
            return t

        def finish(t):
            if res_ref is not None:
                t = t + res_ref[...]
            o_ref[...] = t.astype(o_ref.dtype)

        if nk == 1:
            finish(partial())
        else:
            k = pl.program_id(2)

            @pl.when(k == 0)
            def _():
                acc_ref[...] = partial()

            @pl.when(k > 0)
            def _():
                acc_ref[...] += partial()

            @pl.when(k == nk - 1)
            def _():
                finish(acc_ref[...])

    operands, in_specs = [], []
    for a, a_spec, b, b_spec in pairs:
        operands += [a, b]
        in_specs += [a_spec, b_spec]
    if res is not None:
        operands.append(res[0])
        in_specs.append(res[1])
    scratch = [pltpu.VMEM(acc_shape, F32)] if nk > 1 else []
    res = _pcall(body, name=name, grid=grid, in_specs=in_specs, out_specs=[out_spec], out_shape=[out_shape],
                 operands=operands, scratch_shapes=scratch, comm=comm)
    return res[0] if comm is None else (res[0][0], res[1])


def _mm_nn_stacked(name, a, b_st, out_dtype, tm, comm=None):
    M, K = a.shape
    J, _, Nj = b_st.shape
    tm = _tile(M, tm)
    return _matmul(
        name, [(a, pl.BlockSpec((tm, K), lambda j, i, k: (i, 0)), b_st, pl.BlockSpec((None, K, Nj), lambda j, i, k: (j, 0, 0)))],
        NN, (J, M // tm, 1), jax.ShapeDtypeStruct((M, J * Nj), out_dtype), pl.BlockSpec((tm, Nj), lambda j, i, k: (i, j)), None,
        comm=comm)


def _mm_nt_stacked(name, a, b_st, tm, res=None, comm=None):
    M = a.shape[0]
    J, N, Nj = b_st.shape
    tm = _tile(M, tm)

    def body(a_ref, b_ref, *rest):
        o_ref = rest[-1]
        t = None if res is None else rest[0][...]
        for s in range(J):
            d = _dot(a_ref[:, s * Nj:(s + 1) * Nj], b_ref[s], NT)
            t = d if t is None else t + d
        o_ref[...] = t

    row = pl.BlockSpec((tm, N), lambda i: (i, 0))
    out = _pcall(body, name=name, grid=(M // tm,),
                 in_specs=[pl.BlockSpec((tm, J * Nj), lambda i: (i, 0)),
                           pl.BlockSpec((J, N, Nj), lambda i: (0, 0, 0), pipeline_mode=pl.Buffered(1))] + [row] * (res is not None),
                 out_specs=[row], out_shape=[jax.ShapeDtypeStruct((M, N), F32)],
                 operands=[a, b_st] + [res] * (res is not None), comm=comm)
    return out[0] if comm is None else (out[0][0], out[1])


MXU_COLUMNS = 256


def _col_blocks(n):
    return [slice(s, min(s + MXU_COLUMNS, n)) for s in range(0, n, MXU_COLUMNS)]


def _ffn_gate_up(u2, wg_st, wu_st, tm, comm=None):
    S, D = u2.shape
    J, _, Nj = wg_st.shape
    tm = _tile(S, tm)

    def body(a_ref, wg_ref, wu_ref, dg_ref, du_ref, ff_ref):
        a = a_ref[...]
        blocks = _col_blocks(Nj)
        ahead = (_dot(a, wg_ref[:, blocks[0]]), _dot(a, wu_ref[:, blocks[0]]))
        for j, cols in enumerate(blocks):
            g, u = ahead
            if j + 1 < len(blocks):
                ahead = (_dot(a, wg_ref[:, blocks[j + 1]]), _dot(a, wu_ref[:, blocks[j + 1]]))
            sg = jax.nn.sigmoid(g)
            silu = g * sg
            dg_ref[:, cols] = (u * (sg * (1.0 + g * (1.0 - sg)))).astype(BF16)
            du_ref[:, cols] = silu.astype(BF16)
            ff_ref[:, cols] = (silu * u).astype(BF16)

    w_spec = pl.BlockSpec((None, D, Nj), lambda j, i: (j, 0, 0))
    o_spec = pl.BlockSpec((tm, Nj), lambda j, i: (i, j))
    o = jax.ShapeDtypeStruct((S, J * Nj), BF16)
    return _pcall(body, name="ffn_gate_up", grid=(J, S // tm),
                  in_specs=[pl.BlockSpec((tm, D), lambda j, i: (i, 0)), w_spec, w_spec],
                  out_specs=[o_spec, o_spec, o_spec], out_shape=[o, o, o], operands=[u2, wg_st, wu_st], comm=comm)


def _ffn_gate_up_bwd(dh2b, wd, dg_fac, du_fac, tm, tn, comm=None):
    S, D = dh2b.shape
    F = wd.shape[0]
    tm, tn = _tile(S, tm), _tile(F, tn, LANES)

    def body(a_ref, wd_ref, dg_ref, du_ref, dgt_ref, dup_ref):
        a = a_ref[...]
        for cols in _col_blocks(tn):
            d = _dot(a, wd_ref[cols, :], NT)
            dgt_ref[:, cols] = (d * dg_ref[:, cols].astype(F32)).astype(BF16)
            dup_ref[:, cols] = (d * du_ref[:, cols].astype(F32)).astype(BF16)

    blk = pl.BlockSpec((tm, tn), lambda j, i: (i, j))
    o = jax.ShapeDtypeStruct((S, F), BF16)
    return _pcall(body, name="ffn_gate_up_bwd", grid=(F // tn, S // tm),
                  in_specs=[pl.BlockSpec((tm, D), lambda j, i: (i, 0)), pl.BlockSpec((tn, D), lambda j, i: (j, 0)), blk, blk],
                  out_specs=[blk, blk], out_shape=[o, o], operands=[dh2b, wd, dg_fac, du_fac], comm=comm)


def _mm_tn(name, a, b, tmo, tn, tk, stacked_cols=None, comm=None):
    S, Mo = a.shape
    N = b.shape[1]
    tmo, tk = _tile(Mo, tmo, LANES), _tile(S, tk)
    if stacked_cols is None:
        tn = _tile(N, tn, LANES)
        out_shape = jax.ShapeDtypeStruct((Mo, N), F32)
        out_spec = pl.BlockSpec((tmo, tn), lambda i, j, k: (i, j))
    else:
        tn = stacked_cols
        out_shape = jax.ShapeDtypeStruct((N // tn, Mo, tn), F32)
        out_spec = pl.BlockSpec((None, tmo, tn), lambda i, j, k: (j, i, 0))
    return _matmul(
        name, [(a, pl.BlockSpec((tk, tmo), lambda i, j, k: (k, i)), b, pl.BlockSpec((tk, tn), lambda i, j, k: (k, j)))],
        TN, (Mo // tmo, N // tn, S // tk), out_shape, out_spec, (tmo, tn), comm=comm)


def _rms_fwd(name, x, w, tm, comm=None):
    S, D = x.shape
    tm = _tile(S, tm)

    def body(x_ref, w_ref, o_ref):
        xv = x_ref[...]
        r = lax.rsqrt(jnp.mean(xv * xv, axis=-1, keepdims=True) + EPS)
        o_ref[...] = ((xv * r) * w_ref[...]).astype(BF16)

    row = pl.BlockSpec((tm, D), lambda i: (i, 0))
    return _pcall(body, name=name, grid=(S // tm,), in_specs=[row, pl.BlockSpec((1, D), lambda i: (0, 0))], out_specs=[row],
                  out_shape=[jax.ShapeDtypeStruct((S, D), BF16)], operands=[x, w], comm=comm)


def _rms_bwd(name, x, w, dy, dres, tm, comm=None):
    S, D = x.shape
    tm = _tile(S, tm)

    def body(x_ref, w_ref, dy_ref, dres_ref, dx_ref, dw_ref):
        i = pl.program_id(0)

        @pl.when(i == 0)
        def _():
            dw_ref[...] = jnp.zeros_like(dw_ref)

        xv = x_ref[...]
        r = lax.rsqrt(jnp.mean(xv * xv, axis=-1, keepdims=True) + EPS)
        nv = xv * r
        dyv = dy_ref[...]
        dn = dyv * w_ref[...]
        dw_ref[...] += jnp.sum(dyv * nv, axis=0, keepdims=True)
        dx = dres_ref[...] + r * (dn - nv * jnp.mean(dn * nv, axis=-1, keepdims=True))
        dx_ref[...] = dx

    row = pl.BlockSpec((tm, D), lambda i: (i, 0))
    vec = pl.BlockSpec((1, D), lambda i: (0, 0))
    return _pcall(body, name=name, grid=(S // tm,), in_specs=[row, vec, row, row], out_specs=[row, vec],
                  out_shape=[jax.ShapeDtypeStruct((S, D), F32), jax.ShapeDtypeStruct((1, D), F32)],
                  operands=[x, w, dy, dres], comm=comm)


def _rms_bwd_dy(h1, w, du2, dh2, w_out, tm, comm=None):
    S, D = h1.shape
    d_mix = w_out.shape[0]
    tm = _tile(S, tm)

    def body(x_ref, w_ref, dy_ref, dres_ref, wo_ref, dx_ref, dxb_ref, out_ref, dw_ref):
        i = pl.program_id(0)

        @pl.when(i == 0)
        def _():
            dw_ref[...] = jnp.zeros_like(dw_ref)

        xv = x_ref[...]
        r = lax.rsqrt(jnp.mean(xv * xv, axis=-1, keepdims=True) + EPS)
        nv = xv * r
        dyv = dy_ref[...]
        dn = dyv * w_ref[...]
        dw_ref[...] += jnp.sum(dyv * nv, axis=0, keepdims=True)
        dx = dres_ref[...] + r * (dn - nv * jnp.mean(dn * nv, axis=-1, keepdims=True))
        dx_ref[...] = dx
        dxb = dx.astype(BF16)
        dxb_ref[...] = dxb
        out_ref[...] = _dot(dxb, wo_ref[...], NT)

    row = pl.BlockSpec((tm, D), lambda i: (i, 0))
    vec = pl.BlockSpec((1, D), lambda i: (0, 0))
    return _pcall(
        body, name="rms2_bwd_dy", grid=(S // tm,),
        in_specs=[row, vec, row, row, pl.BlockSpec((d_mix, D), lambda i: (0, 0), pipeline_mode=pl.Buffered(1))],
        out_specs=[row, row, pl.BlockSpec((tm, d_mix), lambda i: (i, 0)), vec],
        out_shape=[jax.ShapeDtypeStruct((S, D), F32), jax.ShapeDtypeStruct((S, D), BF16),
                   jax.ShapeDtypeStruct((S, d_mix), F32), jax.ShapeDtypeStruct((1, D), F32)],
        operands=[h1, w, du2, dh2, w_out], comm=comm)


def _out_proj_rms(y, w_out, x, ln_w, tm, comm=None):
    S, K = y.shape
    D = w_out.shape[1]
    tm = _tile(S, tm)

    def body(a_ref, w_ref, x_ref, lw_ref, h_ref, u_ref):
        hv = _dot(a_ref[...], w_ref[...]) + x_ref[...]
        h_ref[...] = hv
        r = lax.rsqrt(jnp.mean(hv * hv, axis=-1, keepdims=True) + EPS)
        u_ref[...] = ((hv * r) * lw_ref[...]).astype(BF16)

    row = pl.BlockSpec((tm, D), lambda i: (i, 0))
    return _pcall(
        body, name="out_proj", grid=(S // tm,),
        in_specs=[pl.BlockSpec((tm, K), lambda i: (i, 0)),
                  pl.BlockSpec((K, D), lambda i: (0, 0), pipeline_mode=pl.Buffered(1)), row,
                  pl.BlockSpec((1, D), lambda i: (0, 0))],
        out_specs=[row, row], out_shape=[jax.ShapeDtypeStruct((S, D), F32), jax.ShapeDtypeStruct((S, D), BF16)],
        operands=[y, w_out, x, ln_w], comm=comm)


def _ffn_down_loss(ff, wd, h1, tgt, fw, tm):
    S, K = ff.shape
    D = wd.shape[1]
    tm = _tile(S, tm)

    def body(a_ref, wd_ref, h1_ref, t_ref, w_ref, dh_ref, dhb_ref, dw_ref, loss_ref):
        i = pl.program_id(0)

        @pl.when(i == 0)
        def _():
            dw_ref[...] = jnp.zeros_like(dw_ref)
            loss_ref[...] = jnp.zeros_like(loss_ref)

        hv = _dot(a_ref[...], wd_ref[...]) + h1_ref[...]
        wv = w_ref[...]
        r = lax.rsqrt(jnp.mean(hv * hv, axis=-1, keepdims=True) + EPS)
        nv = hv * r
        err = nv * wv - t_ref[...]
        row_loss = jnp.mean(err * err, axis=-1, keepdims=True)
        loss_ref[...] += 0.5 * jnp.sum(row_loss, axis=0, keepdims=True)
        dyo = err * (1.0 / D)
        dn = dyo * wv
        dw_ref[...] += jnp.sum(dyo * nv, axis=0, keepdims=True)
        dh = r * (dn - nv * jnp.mean(dn * nv, axis=-1, keepdims=True))
        dh_ref[...] = dh
        dhb_ref[...] = dh.astype(BF16)

    row = pl.BlockSpec((tm, D), lambda i: (i, 0))
    vec = pl.BlockSpec((1, D), lambda i: (0, 0))
    return _pcall(
        body, name="ffn_down_loss", grid=(S // tm,),
        in_specs=[pl.BlockSpec((tm, K), lambda i: (i, 0)),
                  pl.BlockSpec((K, D), lambda i: (0, 0), pipeline_mode=pl.Buffered(1)), row, row, vec],
        out_specs=[row, row, vec, pl.BlockSpec((1, LANES), lambda i: (0, 0))],
        out_shape=[jax.ShapeDtypeStruct((S, D), F32), jax.ShapeDtypeStruct((S, D), BF16),
                   jax.ShapeDtypeStruct((1, D), F32), jax.ShapeDtypeStruct((1, LANES), F32)],
        operands=[ff, wd, h1, tgt, fw])


def _shift_down(x, d, head8):
    r = pltpu.roll(x, d, 0)
    rh = pltpu.roll(head8, d, 0)
    row8 = lax.broadcasted_iota(jnp.int32, head8.shape, 0)
    top = jnp.where(row8 < d, rh, r[0:8])
    return jnp.concatenate([top, r[8:]], axis=0)


def _shift_up(x, d, tail8):
    n = x.shape[0]
    r = pltpu.roll(x, n - d, 0)
    rt = pltpu.roll(tail8, 8 - d, 0)
    row8 = lax.broadcasted_iota(jnp.int32, tail8.shape, 0)
    bot = jnp.where(row8 + d >= 8, rt, r[n - 8:n])
    return jnp.concatenate([r[:n - 8], bot], axis=0)


def _roll_in_groups(x, d):
    n, c = x.shape
    return pltpu.roll(x.reshape(n // 8, 8, c), d, 1).reshape(n, c)


def _log_sigmoid(lam):
    z = jnp.exp(-jnp.abs(lam))
    u = 1.0 + z
    log1p = jnp.where(u == 1.0, z, jnp.log(u) * (z / jnp.where(u == 1.0, 1.0, u - 1.0)))
    return jnp.minimum(lam, 0.0) - log1p


def _neg_expm1(z, exp_z):
    series = -z * (1.0 + z * (0.5 + z * (1.0 / 6.0)))
    return jnp.where(z > -0.02, series, 1.0 - exp_z)


_GELU_C = 0.7978845608028654


def _gelu(x):
    t = jnp.tanh(_GELU_C * (x + 0.044715 * (x * x * x)))
    return x * (0.5 * (1.0 + t)), t


def _gelu_grad(x, t):
    return 0.5 * (1.0 + t) + 0.5 * x * (1.0 - t * t) * (_GELU_C * (1.0 + 3.0 * 0.044715 * (x * x)))


def _lx_shifts(lx, head8):
    return [lx] + [_shift_down(lx, d, head8) for d in (1, 2, 3)]


def _lru_gates(lx, head8, cw, cb, wa_ref, ba, wx_ref, bx, ls):
    nb = wa_ref.shape[0]
    sh = _lx_shifts(lx, head8)
    cx = cb + sh[3] * cw[0:1]
    cx = cx + sh[2] * cw[1:2]
    cx = cx + sh[1] * cw[2:3]
    cx = cx + sh[0] * cw[3:4]
    cxb = cx.astype(BF16)
    ra = jnp.concatenate([_dot(cxb[:, n * HEAD_DIM:(n + 1) * HEAD_DIM], wa_ref[n]) for n in range(nb)], axis=1) + ba
    ia = jnp.concatenate([_dot(cxb[:, n * HEAD_DIM:(n + 1) * HEAD_DIM], wx_ref[n]) for n in range(nb)], axis=1) + bx
    r = jax.nn.sigmoid(ra)
    ig = jax.nn.sigmoid(ia)
    log_a = LRU_C * r * ls
    a = jnp.exp(log_a)
    return cx, r, ig, a, jnp.sqrt(_neg_expm1(2.0 * log_a, a * a))


def _lru_specs(tl, DL):
    nb = DL // HEAD_DIM
    vec = pl.BlockSpec((1, DL), lambda i: (0, 0))
    return [pl.BlockSpec((CONV_W, DL), lambda i: (0, 0)), vec,
            pl.BlockSpec((nb, HEAD_DIM, HEAD_DIM), lambda i: (0, 0, 0)), vec,
            pl.BlockSpec((nb, HEAD_DIM, HEAD_DIM), lambda i: (0, 0, 0)), vec, vec]


def _lru_fwd(proj, cw, cb, wa, ba, wx, bx, lam, tl, d_mix, comm=None):
    S = proj.shape[0]
    DL = cb.shape[1]
    tl = _tile(S, tl)

    def body(lx_ref, lg_ref, cw_ref, cb_ref, wa_ref, ba_ref, wx_ref, bx_ref, lam_ref, h_ref, kept_ref, y_ref,
             prev8, hc, a_s, b_s):
        i = pl.program_id(0)

        @pl.when(i == 0)
        def _():
            prev8[...] = jnp.zeros_like(prev8)
            hc[...] = jnp.zeros_like(hc)

        lx = lx_ref[...]
        ls = _log_sigmoid(lam_ref[...])
        kept = _lru_gates(lx, prev8[...], cw_ref[...], cb_ref[...], wa_ref, ba_ref[...], wx_ref, bx_ref[...], ls)
        for n, val in enumerate(kept):
            kept_ref[:, n * DL:(n + 1) * DL] = val
        cx, _, ig, a, mult = kept
        b = mult * (ig * cx)
        row = lax.broadcasted_iota(jnp.int32, a.shape, 0) & 7
        for d in (1, 2, 4):
            a_sh = _roll_in_groups(a, d)
            b_sh = _roll_in_groups(b, d)
            m = row >= d
            b = jnp.where(m, a * b_sh + b, b)
            a = jnp.where(m, a * a_sh, a)
        a_s[...] = a
        b_s[...] = b

        def step(g, hprev):
            sl = pl.ds(pl.multiple_of(g * 8, 8), 8)
            hh = a_s[sl, :] * hprev + b_s[sl, :]
            h_ref[sl, :] = hh
            return hh[7:8, :]

        hc[0:1, :] = lax.fori_loop(0, tl // 8, step, hc[0:1, :])
        prev8[...] = lx[tl - 8:tl]
        g, _ = _gelu(lg_ref[...])
        y_ref[...] = (h_ref[...] * g).astype(BF16)

    return _pcall(
        body, name="lru_fwd", grid=(S // tl,),
        in_specs=[pl.BlockSpec((tl, DL), lambda i: (i, 0)), pl.BlockSpec((tl, DL), lambda i: (i, 1))] + _lru_specs(tl, DL),
        out_specs=[pl.BlockSpec((tl, DL), lambda i: (i, 0)), pl.BlockSpec((tl, 5 * DL), lambda i: (i, 0)),
                   pl.BlockSpec((tl, DL), lambda i: (i, 0))],
        out_shape=[jax.ShapeDtypeStruct((S, DL), F32), jax.ShapeDtypeStruct((S, 5 * DL), F32),
                   jax.ShapeDtypeStruct((S, d_mix), BF16)],
        scratch_shapes=[pltpu.VMEM((8, DL), F32), pltpu.VMEM((8, DL), F32), pltpu.VMEM((tl, DL), F32), pltpu.VMEM((tl, DL), F32)],
        operands=[proj, proj, cw, cb, wa, ba, wx, bx, lam], comm=comm)


def _lru_bwd(proj, h, kept, dy, cw, wa, wx, lam, tl, comm=None):
    S = proj.shape[0]
    DL = lam.shape[1]
    nb = DL // HEAD_DIM
    tl = _tile(S, tl)
    nt = S // tl
    ng = tl // 8
    t8 = tl // 8

    def body(lx_ref, lxp_ref, lg_ref, h_ref, hp_ref, kept_ref, dy_ref, cw_ref, wa_ref, wx_ref, lam_ref,
             dlxg_ref, dcw_ref, dcb_ref, dwa_ref, dba_ref, dwx_ref, dbx_ref, dlam_ref,
             a_next, g_carry, dcx_next, an_s, dh_s, g_s):
        i = pl.program_id(0)

        @pl.when(i == 0)
        def _():
            for ref in (dcw_ref, dcb_ref, dwa_ref, dba_ref, dwx_ref, dbx_ref, dlam_ref, a_next, g_carry, dcx_next):
                ref[...] = jnp.zeros_like(ref)

        first = i == nt - 1
        hv = h_ref[...]
        lg = lg_ref[...]
        dyv = dy_ref[...]
        hhead8 = jnp.where(first, 0.0, hp_ref[...])
        lamv = lam_ref[...]
        ls = _log_sigmoid(lamv)
        cwv = cw_ref[...]
        sh = _lx_shifts(lx_ref[...], jnp.where(first, 0.0, lxp_ref[...]))
        cx, r, ig, a, mult = (kept_ref[:, n * DL:(n + 1) * DL] for n in range(5))
        cxb = cx.astype(BF16)
        hprev = _shift_down(hv, 1, hhead8)
        g, t = _gelu(lg)
        dlg = dyv * hv * _gelu_grad(lg, t)
        dh = dyv * g
        an = _shift_up(a, 1, a_next[...])
        row = lax.broadcasted_iota(jnp.int32, a.shape, 0) & 7
        for d in (1, 2, 4):
            an_sh = _roll_in_groups(an, 8 - d)
            dh_sh = _roll_in_groups(dh, 8 - d)
            m = row + d < 8
            dh = jnp.where(m, an * dh_sh + dh, dh)
            an = jnp.where(m, an * an_sh, an)
        an_s[...] = an
        dh_s[...] = dh

        def step(k, gc):
            sl = pl.ds(pl.multiple_of((ng - 1 - k) * 8, 8), 8)
            gg = an_s[sl, :] * gc + dh_s[sl, :]
            g_s[sl, :] = gg
            return gg[0:1, :]

        g_carry[0:1, :] = lax.fori_loop(0, ng, step, g_carry[0:1, :])
        a_next[...] = a[0:8]
        G = g_s[...]
        da = G * hprev
        icx = ig * cx
        dmult = G * icx
        dicx = G * mult
        di = dicx * cx
        dcx = dicx * ig
        dlog = da * a - dmult * ((a * a) * lax.rsqrt(mult * mult))
        dr = dlog * (LRU_C * ls)
        dlam_ref[...] += jnp.sum(dlog * (LRU_C * r), axis=0, keepdims=True)
        dra = dr * r * (1.0 - r)
        dia = di * ig * (1.0 - ig)
        dba_ref[...] += jnp.sum(dra, axis=0, keepdims=True)
        dbx_ref[...] += jnp.sum(dia, axis=0, keepdims=True)
        drab = dra.astype(BF16)
        diab = dia.astype(BF16)
        back = []
        for n in range(nb):
            cs = slice(n * HEAD_DIM, (n + 1) * HEAD_DIM)
            dwa_ref[n] += _dot(cxb[:, cs], drab[:, cs], TN)
            dwx_ref[n] += _dot(cxb[:, cs], diab[:, cs], TN)
            back.append(_dot(drab[:, cs], wa_ref[n], NT) + _dot(diab[:, cs], wx_ref[n], NT))
        dcx = dcx + jnp.concatenate(back, axis=1)
        dcb_ref[...] += jnp.sum(dcx, axis=0, keepdims=True)
        for tap in range(CONV_W):
            dcw_ref[tap:tap + 1, :] += jnp.sum(dcx * sh[CONV_W - 1 - tap], axis=0, keepdims=True)
        tail = dcx_next[...]
        dlx = dcx * cwv[3:4]
        for d in (1, 2, 3):
            dlx = dlx + _shift_up(dcx, d, tail) * cwv[3 - d:4 - d]
        dcx_next[...] = dcx[0:8]
        dlxg_ref[:, 0:DL] = dlx.astype(BF16)
        dlxg_ref[:, DL:2 * DL] = dlg.astype(BF16)

        @pl.when(i == nt - 1)
        def _():
            dlam_ref[...] = dlam_ref[...] * (1.0 - jax.nn.sigmoid(lamv))

    rev = lambda i: nt - 1 - i
    prev8_map = lambda i: (jnp.maximum((nt - 1 - i) * t8 - 1, 0), 0)
    vec = pl.BlockSpec((1, DL), lambda i: (0, 0))
    mat = pl.BlockSpec((nb, HEAD_DIM, HEAD_DIM), lambda i: (0, 0, 0))
    return _pcall(
        body, name="lru_bwd", grid=(nt,), operands=[proj, proj, proj, h, h, kept, dy, cw, wa, wx, lam], comm=comm,
        in_specs=[pl.BlockSpec((tl, DL), lambda i: (rev(i), 0)), pl.BlockSpec((8, DL), prev8_map),
                  pl.BlockSpec((tl, DL), lambda i: (rev(i), 1)),
                  pl.BlockSpec((tl, DL), lambda i: (rev(i), 0)), pl.BlockSpec((8, DL), prev8_map),
                  pl.BlockSpec((tl, 5 * DL), lambda i: (rev(i), 0)),
                  pl.BlockSpec((tl, DL), lambda i: (rev(i), 0)), pl.BlockSpec((CONV_W, DL), lambda i: (0, 0)), mat, mat, vec],
        out_specs=[pl.BlockSpec((tl, 2 * DL), lambda i: (rev(i), 0)), pl.BlockSpec((CONV_W, DL), lambda i: (0, 0)), vec,
                   mat, vec, mat, vec, vec],
        out_shape=[jax.ShapeDtypeStruct(proj.shape, BF16), jax.ShapeDtypeStruct((CONV_W, DL), F32),
                   jax.ShapeDtypeStruct((1, DL), F32), jax.ShapeDtypeStruct((nb, HEAD_DIM, HEAD_DIM), F32),
                   jax.ShapeDtypeStruct((1, DL), F32), jax.ShapeDtypeStruct((nb, HEAD_DIM, HEAD_DIM), F32),
                   jax.ShapeDtypeStruct((1, DL), F32), jax.ShapeDtypeStruct((1, DL), F32)],
        scratch_shapes=[pltpu.VMEM((8, DL), F32), pltpu.VMEM((8, DL), F32), pltpu.VMEM((8, DL), F32),
                        pltpu.VMEM((tl, DL), F32), pltpu.VMEM((tl, DL), F32), pltpu.VMEM((tl, DL), F32)])


def _ret_tables(S, H):
    pos = jnp.arange(S, dtype=F32)
    inv_freq = ROPE_BASE ** (-jnp.arange(0, HEAD_DIM, 2, dtype=F32) / HEAD_DIM)
    ang = pos[:, None] * inv_freq[None, :]
    cos, sin = jnp.cos(ang), jnp.sin(ang)
    cosf = jnp.concatenate([cos, cos], axis=1)
    sins = jnp.concatenate([-sin, sin], axis=1)
    log_gamma = jnp.log1p(-jnp.exp2(-5.0 - jnp.arange(H, dtype=F32)))
    idx = jnp.arange(CHUNK)
    diff = idx[:, None] - idx[None, :]
    causal = diff >= 0
    decay = jnp.where(causal[None], jnp.exp(log_gamma[:, None, None] * jnp.where(causal, diff, 0)[None].astype(F32)), 0.0)
    zeta = jnp.exp(log_gamma[:, None] * (CHUNK - 1 - idx).astype(F32)[None, :])
    xi = jnp.exp(log_gamma[:, None] * (idx + 1).astype(F32)[None, :])
    gc = jnp.exp(log_gamma * CHUNK)
    lanes = (H, CHUNK, HEAD_DIM)
    return (cosf, sins, decay, jnp.broadcast_to(zeta[:, :, None], lanes), jnp.broadcast_to(xi[:, :, None], lanes),
            jnp.broadcast_to(gc[:, None, None], lanes))


def _rope(t, cos, sin_signed):
    return t * cos + pltpu.roll(t, HEAD_DIM // 2, 1) * sin_signed


def _rope_t(d, cos, sin_signed):
    return d * cos + pltpu.roll(d * sin_signed, HEAD_DIM // 2, 1)


def _ret_const_specs(H, DR):
    full = pl.BlockSpec((H, CHUNK, HEAD_DIM), lambda *_: (0, 0, 0))
    return [full, full, full, full, pl.BlockSpec((1, DR), lambda *_: (0, 0))]


def _ret_fwd(proj, y, tables, gnw, tb, comm=None):
    S = proj.shape[0]
    DR = gnw.shape[1]
    H = DR // HEAD_DIM
    tb = _tile(S, tb, CHUNK)
    nc = tb // CHUNK
    cosf, sins, dm, zeta, xi, gc = tables
    scale = HEAD_DIM ** -0.5

    def body(qk_ref, vg_ref, cos_ref, sin_ref, dm_ref, zeta_ref, xi_ref, gc_ref, gnw_ref, y_in, y_ref, rprev_ref, r_s):
        del y_in
        i = pl.program_id(0)

        @pl.when(i == 0)
        def _():
            r_s[...] = jnp.zeros_like(r_s)

        def chunk(c, carry):
            rows = pl.ds(pl.multiple_of(c * CHUNK, CHUNK), CHUNK)
            cos = cos_ref[rows, :]
            sin = sin_ref[rows, :]
            heads = range(H)
            c0 = [slice(h * HEAD_DIM, (h + 1) * HEAD_DIM) for h in heads]
            c1 = [slice(DR + h * HEAD_DIM, DR + (h + 1) * HEAD_DIM) for h in heads]
            qh = [_rope(qk_ref[rows, c0[h]], cos, sin) for h in heads]
            kh = [_rope(qk_ref[rows, c1[h]], cos, sin) * scale for h in heads]
            vb = [vg_ref[rows, c0[h]].astype(BF16) for h in heads]
            rp = [r_s[h] for h in heads]
            rpb = [rp[h].astype(BF16) for h in heads]
            s = [_dot(qh[h].astype(BF16), kh[h].astype(BF16), NT) for h in heads]
            kv = [_dot((kh[h] * zeta_ref[h]).astype(BF16), vb[h], TN) for h in heads]
            cross = [_dot((qh[h] * xi_ref[h]).astype(BF16), rpb[h]) for h in heads]
            o = [_dot((s[h] * dm_ref[h]).astype(BF16), vb[h]) + cross[h] for h in heads]
            for h in heads:
                rprev_ref[c, h] = rpb[h]
                r_s[h] = rp[h] * gc_ref[h] + kv[h]
                mu = jnp.mean(o[h], axis=-1, keepdims=True)
                oc = o[h] - mu
                var = jnp.mean(oc * oc, axis=-1, keepdims=True)
                on = oc * lax.rsqrt(var + EPS) * gnw_ref[:, c0[h]]
                gate = vg_ref[rows, c1[h]]
                y_ref[rows, c0[h]] = (gate * jax.nn.sigmoid(gate) * on).astype(BF16)
            return carry

        lax.fori_loop(0, nc, chunk, 0)

    return _pcall(
        body, name="ret_fwd", grid=(S // tb,),
        in_specs=[pl.BlockSpec((tb, 2 * DR), lambda i: (i, 1)), pl.BlockSpec((tb, 2 * DR), lambda i: (i, 2)),
                  pl.BlockSpec((tb, HEAD_DIM), lambda i: (i, 0)), pl.BlockSpec((tb, HEAD_DIM), lambda i: (i, 0))]
        + _ret_const_specs(H, DR) + [HBM_SPEC],
        out_specs=[pl.BlockSpec((tb, DR), lambda i: (i, 1)),
                   pl.BlockSpec((nc, H, CHUNK, HEAD_DIM), lambda i: (i, 0, 0, 0))],
        out_shape=[jax.ShapeDtypeStruct(y.shape, BF16), jax.ShapeDtypeStruct((S // CHUNK, H, CHUNK, HEAD_DIM), BF16)],
        scratch_shapes=[pltpu.VMEM((H, CHUNK, HEAD_DIM), F32)], aliases={9: 0},
        operands=[proj, proj, cosf, sins, dm, zeta, xi, gc, gnw, y], comm=comm)


def _ret_bwd(proj, rprev, dy, dproj, tables, gnw, tb, comm=None):
    S = proj.shape[0]
    DR = gnw.shape[1]
    H = DR // HEAD_DIM
    tb = _tile(S, tb, CHUNK)
    nc = tb // CHUNK
    nt = S // tb
    cosf, sins, dm, zeta, xi, gc = tables
    scale = HEAD_DIM ** -0.5

    def body(qk_ref, vg_ref, cos_ref, sin_ref, dm_ref, zeta_ref, xi_ref, gc_ref, gnw_ref, rprev_ref, dy_ref, dp_in,
             dp_ref, dgn_ref, dr_s, dqk_s, dvg_s, out_sems):
        del dp_in
        i = pl.program_id(0)
        slot = i % 2

        def out_copies(step, sl):
            rows = pl.ds(pl.multiple_of((nt - 1 - step) * tb, tb), tb)
            return (pltpu.make_async_copy(dqk_s.at[sl], dp_ref.at[rows, pl.ds(2 * DR, 2 * DR)], out_sems.at[sl, 0]),
                    pltpu.make_async_copy(dvg_s.at[sl], dp_ref.at[rows, pl.ds(4 * DR, 2 * DR)], out_sems.at[sl, 1]))

        @pl.when(i == 0)
        def _():
            dr_s[...] = jnp.zeros_like(dr_s)
            dgn_ref[...] = jnp.zeros_like(dgn_ref)

        @pl.when(i >= 2)
        def _():
            for cp in out_copies(i - 2, slot):
                cp.wait()

        def chunk(cc, carry):
            c = nc - 1 - cc
            rows = pl.ds(pl.multiple_of(c * CHUNK, CHUNK), CHUNK)
            cos = cos_ref[rows, :]
            sin = sin_ref[rows, :]
            heads = range(H)
            c0 = [slice(h * HEAD_DIM, (h + 1) * HEAD_DIM) for h in heads]
            c1 = [slice(DR + h * HEAD_DIM, DR + (h + 1) * HEAD_DIM) for h in heads]
            qh = [_rope(qk_ref[rows, c0[h]], cos, sin) for h in heads]
            kh = [_rope(qk_ref[rows, c1[h]], cos, sin) * scale for h in heads]
            qb = [t.astype(BF16) for t in qh]
            kb = [t.astype(BF16) for t in kh]
            vb = [vg_ref[rows, c0[h]].astype(BF16) for h in heads]
            rpb = [rprev_ref[c, h] for h in heads]
            qx = [(qh[h] * xi_ref[h]).astype(BF16) for h in heads]
            kz = [(kh[h] * zeta_ref[h]).astype(BF16) for h in heads]
            drh = [dr_s[h] for h in heads]
            drb = [t.astype(BF16) for t in drh]
            s = [_dot(qb[h], kb[h], NT) for h in heads]
            cross = [_dot(qx[h], rpb[h]) for h in heads]
            dv_state = [_dot(kz[h], drb[h]) for h in heads]
            dk_state = [_dot(vb[h], drb[h], NT) for h in heads]
            sb = [(s[h] * dm_ref[h]).astype(BF16) for h in heads]
            o = [_dot(sb[h], vb[h]) + cross[h] for h in heads]
            dob = []
            for h in heads:
                mu = jnp.mean(o[h], axis=-1, keepdims=True)
                oc = o[h] - mu
                rstd = lax.rsqrt(jnp.mean(oc * oc, axis=-1, keepdims=True) + EPS)
                ohat = oc * rstd
                gw = gnw_ref[:, c0[h]]
                gate = vg_ref[rows, c1[h]]
                sg = jax.nn.sigmoid(gate)
                dyv = dy_ref[rows, c0[h]]
                dvg_s[slot, rows, c1[h]] = (dyv * (ohat * gw) * (sg * (1.0 + gate * (1.0 - sg)))).astype(BF16)
                don = dyv * (gate * sg)
                dgn_ref[:, c0[h]] += jnp.sum(don * ohat, axis=0, keepdims=True)
                dohat = don * gw
                do = rstd * (dohat - jnp.mean(dohat, axis=-1, keepdims=True)
                             - ohat * jnp.mean(dohat * ohat, axis=-1, keepdims=True))
                dob.append(do.astype(BF16))
            ds = [_dot(dob[h], vb[h], NT) for h in heads]
            dq_state = [_dot(dob[h], rpb[h], NT) for h in heads]
            dv = [_dot(sb[h], dob[h], TN) + dv_state[h] for h in heads]
            dr_new = [_dot(qx[h], dob[h], TN) for h in heads]
            dsb = [(ds[h] * dm_ref[h]).astype(BF16) for h in heads]
            dqh = [_dot(dsb[h], kb[h]) + dq_state[h] * xi_ref[h] for h in heads]
            dkh = [_dot(dsb[h], qb[h], TN) + dk_state[h] * zeta_ref[h] for h in heads]
            for h in heads:
                dr_s[h] = drh[h] * gc_ref[h] + dr_new[h]
                dqk_s[slot, rows, c0[h]] = _rope_t(dqh[h], cos, sin).astype(BF16)
                dqk_s[slot, rows, c1[h]] = _rope_t(dkh[h] * scale, cos, sin).astype(BF16)
                dvg_s[slot, rows, c0[h]] = dv[h].astype(BF16)
            return carry

        lax.fori_loop(0, nc, chunk, 0)
        for cp in out_copies(i, slot):
            cp.start()

        @pl.when(i == nt - 1)
        def _():
            if nt >= 2:
                for cp in out_copies(i - 1, 1 - slot):
                    cp.wait()
            for cp in out_copies(i, slot):
                cp.wait()

    rev = lambda i: nt - 1 - i
    return _pcall(
        body, name="ret_bwd", grid=(nt,), aliases={11: 0}, comm=comm,
        operands=[proj, proj, cosf, sins, dm, zeta, xi, gc, gnw, rprev, dy, dproj],
        in_specs=[pl.BlockSpec((tb, 2 * DR), lambda i: (rev(i), 1)), pl.BlockSpec((tb, 2 * DR), lambda i: (rev(i), 2)),
                  pl.BlockSpec((tb, HEAD_DIM), lambda i: (rev(i), 0)), pl.BlockSpec((tb, HEAD_DIM), lambda i: (rev(i), 0))]
        + _ret_const_specs(H, DR)
        + [pl.BlockSpec((nc, H, CHUNK, HEAD_DIM), lambda i: (rev(i), 0, 0, 0)),
           pl.BlockSpec((tb, DR), lambda i: (rev(i), 1)), HBM_SPEC],
        out_specs=[HBM_SPEC, pl.BlockSpec((1, DR), lambda i: (0, 0))],
        out_shape=[jax.ShapeDtypeStruct(dproj.shape, BF16), jax.ShapeDtypeStruct((1, DR), F32)],
        scratch_shapes=[pltpu.VMEM((H, CHUNK, HEAD_DIM), F32), pltpu.VMEM((2, tb, 2 * DR), BF16),
                        pltpu.VMEM((2, tb, 2 * DR), BF16), pltpu.SemaphoreType.DMA((2, 2))])


def _place():
    x, y, c = lax.axis_index("x"), lax.axis_index("y"), lax.axis_index("c")
    chips = [(1 - x, y), (x, 1 - y), (1 - x, 1 - y)]
    return x, y, c, chips


def _own_slab(name, shard, place):
    R, C = shard.shape
    tr = _row_tile(R, C)
    return _ew("cast_" + name, lambda a: (a,), [(shard, pl.BlockSpec((tr, C), lambda i, p: (i, 0)))],
               [(jax.ShapeDtypeStruct((4, R, C), BF16), pl.BlockSpec((None, tr, C), lambda i, p: (p[1], i, 0)))],
               (R // tr,), sp=place)[0]


class _remote:
    def __init__(self, src, dst, ssem, rsem, k, to):
        self.args = dict(src_ref=src, dst_ref=dst, send_sem=ssem.at[k], recv_sem=rsem.at[k], device_id=to,
                         device_id_type=MESH)

    def start(self):
        pltpu.make_async_remote_copy(**self.args).start()

    def wait_send(self):
        pltpu.make_async_remote_copy(**self.args).wait_send()

    def wait_recv(self):
        pltpu.make_async_remote_copy(**self.args).wait_recv()


def _task_fns(copies):
    def start(cins, couts, ssem, rsem, base):
        for cp in copies(cins, couts, ssem, rsem, base)[0]:
            cp.start()

    def finish(cins, couts, ssem, rsem, base):
        sends, recvs = copies(cins, couts, ssem, rsem, base)
        for cp in sends:
            cp.wait_send()
        for cp in recvs:
            cp.wait_recv()

    return start, finish


NEIGHBOURS, DIAGONAL = (0, 1), (2,)


def _gather_ici(st, which=NEIGHBOURS + DIAGONAL):
    r2 = st.shape[1] // 2

    def copies(cins, couts, ssem, rsem, base):
        x, y, c, chips = _place()
        out = couts[0]
        mine = out.at[2 * x + y, pl.ds(c * r2, r2), :]
        sends, recvs = [], []
        for k, j in enumerate(which):
            cx, cy = chips[j]
            got = out.at[2 * cx + cy, pl.ds(c * r2, r2), :]
            sends.append(_remote(mine, mine, ssem, rsem, base + k, (cx, cy, c)))
            recvs.append(_remote(got, got, ssem, rsem, base + k, (x, y, c)))
        return sends, recvs

    start, finish = _task_fns(copies)
    return _Comm([st], [jax.ShapeDtypeStruct(st.shape, st.dtype)], {0: 0}, len(which), start, finish)


def _gather_d2d(st):
    r2 = st.shape[1] // 2

    def copies(cins, couts, ssem, rsem, base):
        x, y, c, chips = _place()
        out = couts[0]
        sends, recvs = [], []
        for j, (cx, cy) in enumerate(chips):
            have = out.at[2 * cx + cy, pl.ds(c * r2, r2), :]
            want = out.at[2 * cx + cy, pl.ds((1 - c) * r2, r2), :]
            sends.append(_remote(have, have, ssem, rsem, base + j, (x, y, 1 - c)))
            recvs.append(_remote(want, want, ssem, rsem, base + j, (x, y, c)))
        return sends, recvs

    start, finish = _task_fns(copies)
    return _Comm([st], [jax.ShapeDtypeStruct(st.shape, st.dtype)], {0: 0}, 3, start, finish)


def _gather_conv(conv_w):
    def copies(cins, couts, ssem, rsem, base):
        x, y, c, chips = _place()
        src, out = cins[0], couts[0]
        sends = [_remote(src, out.at[2 * x + y], ssem, rsem, base + j, (*chip, c)) for j, chip in enumerate(chips)]
        recvs = [_remote(src, out.at[2 * cx + cy], ssem, rsem, base + j, (x, y, c)) for j, (cx, cy) in enumerate(chips)]
        return sends, recvs

    start, finish = _task_fns(copies)
    return _Comm([conv_w], [jax.ShapeDtypeStruct((4,) + conv_w.shape, conv_w.dtype)], {}, 3, start, finish)


def _pair_exchange(g):
    r2 = g.shape[1] // 2

    def copies(cins, couts, ssem, rsem, base):
        x, y, c, _ = _place()
        cp = _remote(cins[0].at[:, pl.ds((1 - c) * r2, r2), :], couts[0], ssem, rsem, base, (x, y, 1 - c))
        return [cp], [cp]

    start, finish = _task_fns(copies)
    return _Comm([g], [jax.ShapeDtypeStruct((g.shape[0], r2, g.shape[2]), g.dtype)], {}, 1, start, finish)


def _chip_exchange(part):
    def copies(cins, couts, ssem, rsem, base):
        x, y, c, chips = _place()
        cps = [_remote(cins[0].at[2 * cx + cy], couts[0].at[j], ssem, rsem, base + j, (cx, cy, c))
               for j, (cx, cy) in enumerate(chips)]
        return cps, cps

    start, finish = _task_fns(copies)
    return _Comm([part], [jax.ShapeDtypeStruct((3,) + part.shape[1:], part.dtype)], {}, 3, start, finish)


def _pair_share(slot):
    def copies(cins, couts, ssem, rsem, base):
        x, y, c, _ = _place()
        out = couts[0]
        return ([_remote(out.at[c], out.at[c], ssem, rsem, base, (x, y, 1 - c))],
                [_remote(out.at[1 - c], out.at[1 - c], ssem, rsem, base, (x, y, c))])

    start, finish = _task_fns(copies)
    return _Comm([slot], [jax.ShapeDtypeStruct(slot.shape, slot.dtype)], {0: 0}, 1, start, finish)


def _gather_small(sm):
    flips = [(fx, fy, fc) for fx in (0, 1) for fy in (0, 1) for fc in (0, 1)][1:]

    def copies(cins, couts, ssem, rsem, base):
        x, y, c, _ = _place()
        src, out = cins[0], couts[0]
        peers = [(1 - x if fx else x, 1 - y if fy else y, 1 - c if fc else c) for fx, fy, fc in flips]
        sends = [_remote(src, out.at[4 * x + 2 * y + c], ssem, rsem, base + k, peer) for k, peer in enumerate(peers)]
        recvs = [_remote(src, out.at[4 * px + 2 * py + pc], ssem, rsem, base + k, (x, y, c))
                 for k, (px, py, pc) in enumerate(peers)]
        return sends, recvs

    start, finish = _task_fns(copies)
    return _Comm([sm], [jax.ShapeDtypeStruct((8,) + sm.shape, sm.dtype)], {}, 7, start, finish)


def _comm_call(name, tasks):
    task = _merge(tasks)
    nci = len(task.ins)

    def body(*refs):
        cins, couts, (ssem, rsem) = refs[:nci], refs[nci:nci + len(task.outs)], refs[nci + len(task.outs):]
        task.start(cins, couts, ssem, rsem, 0)
        task.finish(cins, couts, ssem, rsem, 0)

    return pl.pallas_call(
        body, in_specs=[HBM_SPEC] * nci, out_specs=[HBM_SPEC] * len(task.outs), out_shape=list(task.outs),
        scratch_shapes=[pltpu.SemaphoreType.DMA((task.n_sem,)), pltpu.SemaphoreType.DMA((task.n_sem,))],
        input_output_aliases=task.aliases, name=name)(*task.ins)


def _adamw(w, g, m, v):
    m = ADAM_B1 * m + (1.0 - ADAM_B1) * g
    v = ADAM_B2 * v + (1.0 - ADAM_B2) * (g * g)
    m_hat = m / (1.0 - ADAM_B1 ** ADAM_STEP)
    v_hat = v / (1.0 - ADAM_B2 ** ADAM_STEP)
    delta = -ADAM_LR * (m_hat / (jnp.sqrt(v_hat) + ADAM_EPS) + ADAM_WD * w)
    return delta, m, v


def _adamw_call(name, w, g, m, v):
    R, C = w.shape
    tr = _row_tile(R, C, 1024 * 1024)
    row = pl.BlockSpec((tr, C), lambda i: (i, 0))
    o = jax.ShapeDtypeStruct((R, C), F32)
    return _ew(name, lambda w_, g_, m_, v_: (*_adamw(w_, g_, m_, v_), g_), [(w, row), (g, row), (m, row), (v, row)],
               [(o, row), (o, row), (o, row), (o, row)], (R // tr,))


def _pair_sum(name, g, ra, place):
    _, R, C = g.shape
    r2 = R // 2
    tr = _row_tile(r2, C)
    nb = r2 // tr
    own = pl.BlockSpec((None, tr, C), lambda j, i, p: (j, p[0] * nb + i, 0))
    blk = pl.BlockSpec((None, tr, C), lambda j, i, p: (j, i, 0))
    return _ew("rs_pair_sum_" + name, lambda a, b: (a + b,), [(g, own), (ra, blk)],
               [(jax.ShapeDtypeStruct((4, r2, C), BF16), blk)], (4, nb), sp=place)[0]


def _chip_sum(name, g, ra, rb, place):
    _, R, C = g.shape
    r2 = R // 2
    tr = _row_tile(r2, C)
    nb = r2 // tr
    own = pl.BlockSpec((None, tr, C), lambda i, p: (p[1], p[0] * nb + i, 0))
    mine = pl.BlockSpec((None, tr, C), lambda i, p: (p[1], i, 0))
    src = [pl.BlockSpec((None, tr, C), functools.partial(lambda i, p, j: (j, i, 0), j=j)) for j in range(3)]
    out = pl.BlockSpec((None, tr, C), lambda i, p: (p[0], i, 0))

    def total(a, b, r0, r1, r2_):
        return ((((a + b) + r0.astype(F32)) + r1.astype(F32)) + r2_.astype(F32),)

    return _ew("rs_chip_sum_" + name, total, [(g, own), (ra, mine), (rb, src[0]), (rb, src[1]), (rb, src[2])],
               [(jax.ShapeDtypeStruct((2, r2, C), F32), out)], (nb,), sp=place)[0]


def _pack(arrays):
    rows, offs, pos = [], [], 0
    for a in arrays:
        flat = a.reshape(-1)
        n = -(-flat.shape[0] // (8 * LANES)) * (8 * LANES)
        if n != flat.shape[0]:
            flat = jnp.pad(flat, (0, n - flat.shape[0]))
        rows.append(flat.reshape(-1, LANES))
        offs.append(pos)
        pos += n // LANES
    return jnp.concatenate(rows, axis=0), offs


def _unpack(packed, offs, shapes):
    out = []
    for off, shp in zip(offs, shapes):
        n = 1
        for s in shp:
            n *= s
        out.append(packed[off:off + -(-n // LANES)].reshape(-1)[:n].reshape(shp))
    return out


def _sum8(gathered):
    _, R, C = gathered.shape
    tr = _row_tile(R, C, 256 * 1024)
    specs = [pl.BlockSpec((None, tr, C), functools.partial(lambda i, d: (d, i, 0), d=d)) for d in range(8)]

    def fn(*parts):
        t = parts[0]
        for p in parts[1:]:
            t = t + p
        return (t,)

    return _ew("small_sum", fn, [(gathered, s) for s in specs],
               [(jax.ShapeDtypeStruct((R, C), F32), pl.BlockSpec((tr, C), lambda i: (i, 0)))], (R // tr,))[0]


BIG = ("w_in", "w_out", "w_ffn_gate", "w_ffn_up", "w_ffn_down")
SMALL = ("ln1_w", "conv_w", "conv_b", "gate_a_w", "gate_a_b", "gate_x_w", "gate_x_b", "lru_lambda", "ret_gn_w", "ln2_w",
         "final_norm_w")
WEIGHTS = ("ln1_w", "w_in", "conv_w", "conv_b", "gate_a_w", "gate_a_b", "gate_x_w", "gate_x_b", "lru_lambda", "ret_gn_w",
           "w_out", "ln2_w", "w_ffn_gate", "w_ffn_up", "w_ffn_down", "final_norm_w")


def kernel(x, ln1_w, w_in, conv_w, conv_b, gate_a_w, gate_a_b, gate_x_w, gate_x_b, lru_lambda, ret_gn_w, w_out, ln2_w, w_ffn_gate, w_ffn_up, w_ffn_down, final_norm_w, loss_target, m_ln1_w, m_w_in, m_conv_w, m_conv_b, m_gate_a_w, m_gate_a_b, m_gate_x_w, m_gate_x_b, m_lru_lambda, m_ret_gn_w, m_w_out, m_ln2_w, m_w_ffn_gate, m_w_ffn_up, m_w_ffn_down, m_final_norm_w, v_ln1_w, v_w_in, v_conv_w, v_conv_b, v_gate_a_w, v_gate_a_b, v_gate_x_w, v_gate_x_b, v_lru_lambda, v_ret_gn_w, v_w_out, v_ln2_w, v_w_ffn_gate, v_w_ffn_up, v_w_ffn_down, v_final_norm_w):
    w = dict(ln1_w=ln1_w, w_in=w_in, conv_w=conv_w, conv_b=conv_b, gate_a_w=gate_a_w, gate_a_b=gate_a_b, gate_x_w=gate_x_w,
             gate_x_b=gate_x_b, lru_lambda=lru_lambda, ret_gn_w=ret_gn_w, w_out=w_out, ln2_w=ln2_w, w_ffn_gate=w_ffn_gate,
             w_ffn_up=w_ffn_up, w_ffn_down=w_ffn_down, final_norm_w=final_norm_w)
    m = dict(ln1_w=m_ln1_w, w_in=m_w_in, conv_w=m_conv_w, conv_b=m_conv_b, gate_a_w=m_gate_a_w, gate_a_b=m_gate_a_b,
             gate_x_w=m_gate_x_w, gate_x_b=m_gate_x_b, lru_lambda=m_lru_lambda, ret_gn_w=m_ret_gn_w, w_out=m_w_out,
             ln2_w=m_ln2_w, w_ffn_gate=m_w_ffn_gate, w_ffn_up=m_w_ffn_up, w_ffn_down=m_w_ffn_down,
             final_norm_w=m_final_norm_w)
    v = dict(ln1_w=v_ln1_w, w_in=v_w_in, conv_w=v_conv_w, conv_b=v_conv_b, gate_a_w=v_gate_a_w, gate_a_b=v_gate_a_b,
             gate_x_w=v_gate_x_w, gate_x_b=v_gate_x_b, lru_lambda=v_lru_lambda, ret_gn_w=v_ret_gn_w, w_out=v_w_out,
             ln2_w=v_ln2_w, w_ffn_gate=v_w_ffn_gate, w_ffn_up=v_w_ffn_up, w_ffn_down=v_w_ffn_down,
             final_norm_w=v_final_norm_w)
    xs, tgt = x[0], loss_target[0]
    S, D = xs.shape
    DL, DR = conv_b.shape[1], ret_gn_w.shape[1]
    assert DL == DR and DL % HEAD_DIM == 0 and S % CHUNK == 0
    d_mix = DL + DR
    cx, cy, cc = lax.axis_index("x"), lax.axis_index("y"), lax.axis_index("c")
    chip = 2 * cx + cy
    place = jnp.stack([cc, chip]).astype(jnp.int32)
    grad, delta, new_m, new_v = {}, {}, {}, {}

    def finish_big(n, full):
        shp = w[n].shape
        g2 = full.reshape(shp[1], shp[2])
        w2, m2, v2 = (t[n].reshape(shp[1], shp[2]) for t in (w, m, v))
        d_, m_, v_, g_ = _adamw_call("adamw_" + n, w2, g2, m2, v2)
        grad[n], delta[n], new_m[n], new_v[n] = (t.reshape(shp) for t in (g_, d_, m_, v_))

    def all_sum(gathered, own):
        return _sum8(lax.dynamic_update_slice(gathered, own[None], (4 * cx + 2 * cy + cc, 0, 0)))

    st = {n: _own_slab(n, w[n][0], place) for n in BIG}
    (u1,), (w_in_st,) = _rms_fwd("rms1", xs, ln1_w, TM, comm=_gather_ici(st["w_in"]))
    w_in_st, conv_st = _comm_call("gather_w_in", [_gather_d2d(w_in_st), _gather_conv(conv_w[0])])
    conv_st = lax.dynamic_update_slice(conv_st, conv_w, (chip, 0, 0))
    cw_cols = conv_st.shape[2]
    conv_full = jnp.transpose(conv_st, (1, 0, 2)).reshape(CONV_W, 4 * cw_cols)
    n_in, n_ff = w_in_st.shape[2], st["w_ffn_gate"].shape[2]
    tables = _ret_tables(S, DR // HEAD_DIM)
    wab, wxb = gate_a_w[0].astype(BF16), gate_x_w[0].astype(BF16)
    lru_w = (conv_full, conv_b, wab, gate_a_b, wxb, gate_x_b, lru_lambda)

    proj, (w_out_st, wg_st) = _mm_nn_stacked("proj", u1, w_in_st, F32, TM_WIDE,
                                             comm=_merge([_gather_ici(st["w_out"]), _gather_ici(st["w_ffn_gate"])]))
    (hs, kept, y), (w_out_st, wg_st, wu_st) = _lru_fwd(
        proj, *lru_w, LRU_TILE, d_mix,
        comm=_merge([_gather_d2d(w_out_st), _gather_d2d(wg_st), _gather_ici(st["w_ffn_up"])]))
    (y, rprev), (wu_st, wd_st) = _ret_fwd(proj, y, tables, ret_gn_w, RET_BLOCK,
                                          comm=_merge([_gather_d2d(wu_st), _gather_ici(st["w_ffn_down"], NEIGHBOURS)]))
    w_out_f = w_out_st.reshape(d_mix, D)
    (h1, u2), (wd_st,) = _out_proj_rms(y, w_out_f, xs, ln2_w, TM, comm=_gather_ici(wd_st, DIAGONAL))
    (dg_fac, du_fac, ff), (wd_st,) = _ffn_gate_up(u2, wg_st, wu_st, TM, comm=_gather_d2d(wd_st))
    wd_f = wd_st.reshape(4 * n_ff, D)
    dh2, dh2b, d_fw, loss = _ffn_down_loss(ff, wd_f, h1, tgt, final_norm_w.reshape(1, D), TM_RESIDENT)

    g_wd = _mm_tn("g_w_down", ff, dh2b, n_ff, TILE_GRAD, TK_GRAD).reshape(4, n_ff, D)
    (dgt, dup), (ra_wd,) = _ffn_gate_up_bwd(dh2b, wd_f, dg_fac, du_fac, TM_WIDE, n_ff, comm=_pair_exchange(g_wd))
    pb_wd = _pair_sum("w_ffn_down", g_wd, ra_wd, place)
    g_wg, (rb_wd,) = _mm_tn("g_w_gate", u2, dgt, TILE_GRAD, None, TK_GRAD, stacked_cols=n_ff,
                            comm=_chip_exchange(pb_wd))
    slot_wd = _chip_sum("w_ffn_down", g_wd, ra_wd, rb_wd, place)
    g_wu, (full_wd, ra_wg) = _mm_tn("g_w_up", u2, dup, TILE_GRAD, None, TK_GRAD, stacked_cols=n_ff,
                                    comm=_merge([_pair_share(slot_wd), _pair_exchange(g_wg)]))
    finish_big("w_ffn_down", full_wd)
    pb_wg = _pair_sum("w_ffn_gate", g_wg, ra_wg, place)
    du2, (rb_wg,) = _mm_nt_stacked("d_u2_gate", dgt, wg_st, TM, comm=_chip_exchange(pb_wg))
    du2, (ra_wu,) = _mm_nt_stacked("d_u2_up", dup, wu_st, TM, res=du2, comm=_pair_exchange(g_wu))
    slot_wg = _chip_sum("w_ffn_gate", g_wg, ra_wg, rb_wg, place)
    pb_wu = _pair_sum("w_ffn_up", g_wu, ra_wu, place)
    (dh1, dh1b, dy, d_ln2), (full_wg,) = _rms_bwd_dy(h1, ln2_w, du2, dh2, w_out_f, TM_RESIDENT,
                                                     comm=_pair_share(slot_wg))
    finish_big("w_ffn_gate", full_wg)
    g_wout = _mm_tn("g_w_out", y, dh1b, TILE_GRAD, TILE_GRAD, TK_GRAD).reshape(4, d_mix // 4, D)
    (dproj, d_cw, d_cb, d_wa, d_ba, d_wx, d_bx, d_lam), (rb_wu, ra_wout) = _lru_bwd(
        proj, hs, kept, dy, conv_full, wab, wxb, lru_lambda, LRU_TILE,
        comm=_merge([_chip_exchange(pb_wu), _pair_exchange(g_wout)]))
    slot_wu = _chip_sum("w_ffn_up", g_wu, ra_wu, rb_wu, place)
    pb_wout = _pair_sum("w_out", g_wout, ra_wout, place)
    (dproj, d_gn), (full_wu, rb_wout) = _ret_bwd(proj, rprev, dy, dproj, tables, ret_gn_w, RET_BLOCK,
                                                 comm=_merge([_pair_share(slot_wu), _chip_exchange(pb_wout)]))
    finish_big("w_ffn_up", full_wu)
    slot_wout = _chip_sum("w_out", g_wout, ra_wout, rb_wout, place)
    small = dict(conv_w=d_cw, conv_b=d_cb, gate_a_w=d_wa, gate_a_b=d_ba, gate_x_w=d_wx, gate_x_b=d_bx, lru_lambda=d_lam,
                 ret_gn_w=d_gn, ln2_w=d_ln2, final_norm_w=d_fw)
    packed, offs = _pack([small[n] for n in SMALL[1:]] + [loss])
    g_win, (full_wout, got_small) = _mm_tn("g_w_in", u1, dproj, TILE_GRAD, None, TK_GRAD, stacked_cols=n_in,
                                           comm=_merge([_pair_share(slot_wout), _gather_small(packed)]))
    finish_big("w_out", full_wout)
    (ra_win,) = _comm_call("rs_pair_w_in", [_pair_exchange(g_win)])
    pb_win = _pair_sum("w_in", g_win, ra_win, place)
    du1, (rb_win,) = _mm_nt_stacked("d_u1", dproj, w_in_st, TM, comm=_chip_exchange(pb_win))
    slot_win = _chip_sum("w_in", g_win, ra_win, rb_win, place)
    gx, d_ln1 = _rms_bwd("rms1_bwd", xs, ln1_w, du1, dh1, TM)
    packed1, _ = _pack([d_ln1])
    full_win, got_ln1 = _comm_call("reduce_tail", [_pair_share(slot_win), _gather_small(packed1)])
    finish_big("w_in", full_win)

    red = _unpack(all_sum(got_small, packed), offs, [small[n].shape for n in SMALL[1:]] + [(1, LANES)])
    g = dict(zip(SMALL[1:], red[:-1]))
    g["ln1_w"] = all_sum(got_ln1, packed1)[:-(-D // LANES)].reshape(1, D)
    loss_out = red[-1][0, 0]
    g["conv_w"] = lax.dynamic_slice(g["conv_w"], (0, chip * cw_cols), (CONV_W, cw_cols))
    packs = [_pack([t[n] for n in SMALL])[0] for t in (w, m, v)]
    gp, offs2 = _pack([g[n] for n in SMALL])
    outs = _adamw_call("adamw_small", packs[0], gp, packs[1], packs[2])
    shapes = [w[n].shape for n in SMALL]
    for dst, arr in zip((delta, new_m, new_v), outs):
        dst.update(zip(SMALL, _unpack(arr, offs2, shapes)))
    for n in SMALL:
        grad[n] = g[n].reshape(w[n].shape)

    return (loss_out, gx.reshape(x.shape), *[grad[n] for n in WEIGHTS], *[delta[n] for n in WEIGHTS],
            *[new_m[n] for n in WEIGHTS], *[new_v[n] for n in WEIGHTS])
```

```python
import functools

import jax
import jax.numpy as jnp
from jax import lax
from jax.experimental import pallas as pl
from jax.experimental.pallas import tpu as pltpu

F32 = jnp.float32
BF16 = jnp.bfloat16
MESH = pl.DeviceIdType.MESH

EPS = 1e-6
LRU_C = 8.0
ROPE_BASE = 10000.0
CHUNK = 128
HEAD_DIM = 128
CONV_W = 4
ADAM_LR = 0.001
ADAM_B1 = 0.9
ADAM_B2 = 0.999
ADAM_EPS = 1e-08
ADAM_WD = 0.01
ADAM_STEP = 10

V7X_VMEM_BYTES = 64 * 1024 * 1024
VMEM_LIMIT = V7X_VMEM_BYTES - 8 * 1024 * 1024
LANES = 128
SUBLANES_16BIT = 16

TM = 512
TM_WIDE = 1024
TM_RESIDENT = 256
TK_GRAD = 2048
TILE_GRAD = 1024
LRU_TILE = 128
RET_BLOCK = 256

NN = (((1,), (0,)), ((), ()))
NT = (((1,), (1,)), ((), ()))
TN = (((0,), (0,)), ((), ()))


def _dot(a, b, dims=NN):
    return lax.dot_general(a, b, dims, preferred_element_type=F32)


def _tile(n, pref, mult=SUBLANES_16BIT):
    best = None
    t = mult
    while t <= min(n, pref):
        if n % t == 0:
            best = t
        t += mult
    return best if best is not None else n


def _row_tile(rows, cols, budget_bytes=2 * 1024 * 1024):
    return _tile(rows, max(SUBLANES_16BIT, budget_bytes // (cols * 4)))


def _params(sem):
    return pltpu.CompilerParams(dimension_semantics=sem, vmem_limit_bytes=VMEM_LIMIT)


HBM_SPEC = pl.BlockSpec(memory_space=pl.ANY)


class _Comm:
    def __init__(self, ins, outs, aliases, n_sem, start, finish):
        self.ins, self.outs, self.aliases, self.n_sem, self.start, self.finish = ins, outs, aliases, n_sem, start, finish


def _merge(tasks):
    ins, outs, aliases, plans, n_sem = [], [], {}, [], 0
    for t in tasks:
        i0, o0 = len(ins), len(outs)
        plans.append((t, i0, o0, n_sem))
        ins += t.ins
        outs += t.outs
        aliases.update({i0 + a: o0 + b for a, b in t.aliases.items()})
        n_sem += t.n_sem

    def run(which):
        def go(cins, couts, ssem, rsem, base):
            for t, i0, o0, s0 in plans:
                getattr(t, which)(cins[i0:i0 + len(t.ins)], couts[o0:o0 + len(t.outs)], ssem, rsem, base + s0)
        return go

    return _Comm(ins, outs, aliases, n_sem, run("start"), run("finish"))


def _pcall(body, *, name, grid, in_specs, out_specs, out_shape, operands, scratch_shapes=(), aliases=None, comm=None):
    n_in, n_out, n_scr = len(operands), len(out_shape), len(scratch_shapes)
    aliases = dict(aliases or {})
    params = _params(("arbitrary",) * len(grid))
    if comm is None:
        return pl.pallas_call(body, grid=grid, in_specs=list(in_specs), out_specs=list(out_specs), out_shape=list(out_shape),
                              scratch_shapes=list(scratch_shapes), input_output_aliases=aliases, name=name,
                              compiler_params=params)(*operands)
    nci, nco = len(comm.ins), len(comm.outs)

    def wrapped(*refs):
        ins, cins = refs[:n_in], refs[n_in:n_in + nci]
        o0 = n_in + nci
        outs, couts = refs[o0:o0 + n_out], refs[o0 + n_out:o0 + n_out + nco]
        s0 = o0 + n_out + nco
        scr, (ssem, rsem) = refs[s0:s0 + n_scr], refs[s0 + n_scr:]
        ids = [pl.program_id(a) for a in range(len(grid))]
        first = functools.reduce(jnp.logical_and, [i == 0 for i in ids])
        last = functools.reduce(jnp.logical_and, [i == g - 1 for i, g in zip(ids, grid)])

        @pl.when(first)
        def _():
            comm.start(cins, couts, ssem, rsem, 0)

        body(*ins, *outs, *scr)

        @pl.when(last)
        def _():
            comm.finish(cins, couts, ssem, rsem, 0)

    aliases.update({n_in + a: n_out + b for a, b in comm.aliases.items()})
    res = pl.pallas_call(
        wrapped, grid=grid, in_specs=list(in_specs) + [HBM_SPEC] * nci, out_specs=list(out_specs) + [HBM_SPEC] * nco,
        out_shape=list(out_shape) + list(comm.outs),
        scratch_shapes=list(scratch_shapes) + [pltpu.SemaphoreType.DMA((comm.n_sem,)), pltpu.SemaphoreType.DMA((comm.n_sem,))],
        input_output_aliases=aliases, name=name, compiler_params=params)(*operands, *comm.ins)
    return res[:n_out], res[n_out:]


def _ew(name, fn, ins, outs, grid, sp=None):
    n_in = len(ins)

    def body(*refs):
        if sp is not None:
            refs = refs[1:]
        vals = [r[...] for r in refs[:n_in]]
        res = fn(*vals)
        for o_ref, v in zip(refs[n_in:], res):
            o_ref[...] = v.astype(o_ref.dtype)

    in_specs = [s for _, s in ins]
    out_specs = [s for _, s in outs]
    out_shape = [s for s, _ in outs]
    sem = ("arbitrary",) * len(grid)
    if sp is None:
        return pl.pallas_call(body, grid=grid, in_specs=in_specs, out_specs=out_specs, out_shape=out_shape,
                              name=name, compiler_params=_params(sem))(*[a for a, _ in ins])
    gs = pltpu.PrefetchScalarGridSpec(num_scalar_prefetch=1, grid=grid, in_specs=in_specs, out_specs=out_specs)
    return pl.pallas_call(body, grid_spec=gs, out_shape=out_shape, name=name,
                          compiler_params=_params(sem))(sp, *[a for a, _ in ins])


def _matmul(name, pairs, dims, grid, out_shape, out_spec, acc_shape, res=None, comm=None):
    n = len(pairs)
    nk = grid[2]

    def body(*refs):
        ab = refs[:2 * n]
        pos = 2 * n
        res_ref = None
        if res is not None:
            res_ref = refs[pos]
            pos += 1
        o_ref = refs[pos]
        acc_ref = refs[pos + 1] if nk > 1 else None

        def partial():
            t = None
            for p in range(n):
                d = _dot(ab[2 * p][...], ab[2 * p + 1][...], dims)
                t = d if t is None else t + d
            return t

        def finish(t):
            if res_ref is not None:
                t = t + res_ref[...]
            if len(o_ref.shape) == 3:
                nj = o_ref.shape[2]
                for s in range(o_ref.shape[0]):
                    o_ref[s] = t[:, s * nj:(s + 1) * nj].astype(o_ref.dtype)
            else:
                o_ref[...] = t.astype(o_ref.dtype)

        if nk == 1:
            finish(partial())
        else:
            k = pl.program_id(2)

            @pl.when(k == 0)
            def _():
                acc_ref[...] = partial()

            @pl.when(k > 0)
            def _():
                acc_ref[...] += partial()

            @pl.when(k == nk - 1)
            def _():
                finish(acc_ref[...])

    operands, in_specs = [], []
    for a, a_spec, b, b_spec in pairs:
        operands += [a, b]
        in_specs += [a_spec, b_spec]
    if res is not None:
        operands.append(res[0])
        in_specs.append(res[1])
    scratch = [pltpu.VMEM(acc_shape, F32)] if nk > 1 else []
    res = _pcall(body, name=name, grid=grid, in_specs=in_specs, out_specs=[out_spec], out_shape=[out_shape],
                 operands=operands, scratch_shapes=scratch, comm=comm)
    return res[0] if comm is None else (res[0][0], res[1])


def _mm_nn_stacked(name, a, b_st, out_dtype, tm, comm=None):
    M, K = a.shape
    J, _, Nj = b_st.shape
    tm = _tile(M, tm)
    return _matmul(
        name, [(a, pl.BlockSpec((tm, K), lambda j, i, k: (i, 0)), b_st, pl.BlockSpec((None, K, Nj), lambda j, i, k: (j, 0, 0)))],
        NN, (J, M // tm, 1), jax.ShapeDtypeStruct((M, J * Nj), out_dtype), pl.BlockSpec((tm, Nj), lambda j, i, k: (i, j)), None,
        comm=comm)


def _mm_nt_stacked(name, a, b_st, tm, res=None, comm=None):
    M = a.shape[0]
    J, N, Nj = b_st.shape
    tm = _tile(M, tm)

    def body(a_ref, b_ref, *rest):
        o_ref = rest[-1]
        t = None if res is None else rest[0][...]
        for s in range(J):
            d = _dot(a_ref[:, s * Nj:(s + 1) * Nj], b_ref[s], NT)
            t = d if t is None else t + d
        o_ref[...] = t

    row = pl.BlockSpec((tm, N), lambda i: (i, 0))
    out = _pcall(body, name=name, grid=(M // tm,),
                 in_specs=[pl.BlockSpec((tm, J * Nj), lambda i: (i, 0)),
                           pl.BlockSpec((J, N, Nj), lambda i: (0, 0, 0), pipeline_mode=pl.Buffered(1))] + [row] * (res is not None),
                 out_specs=[row], out_shape=[jax.ShapeDtypeStruct((M, N), F32)],
                 operands=[a, b_st] + [res] * (res is not None), comm=comm)
    return out[0] if comm is None else (out[0][0], out[1])


MXU_COLUMNS = 256


def _col_blocks(n):
    return [slice(s, min(s + MXU_COLUMNS, n)) for s in range(0, n, MXU_COLUMNS)]


def _ffn_gate_up(u2, wg_st, wu_st, tm, comm=None):
    S, D = u2.shape
    J, _, Nj = wg_st.shape
    tm = _tile(S, tm)

    def body(a_ref, wg_ref, wu_ref, dg_ref, du_ref, ff_ref):
        a = a_ref[...]
        blocks = _col_blocks(Nj)
        ahead = (_dot(a, wg_ref[:, blocks[0]]), _dot(a, wu_ref[:, blocks[0]]))
        for j, cols in enumerate(blocks):
            g, u = ahead
            if j + 1 < len(blocks):
                ahead = (_dot(a, wg_ref[:, blocks[j + 1]]), _dot(a, wu_ref[:, blocks[j + 1]]))
            sg = jax.nn.sigmoid(g)
            silu = g * sg
            dg_ref[:, cols] = (u * (sg * (1.0 + g * (1.0 - sg)))).astype(BF16)
            du_ref[:, cols] = silu.astype(BF16)
            ff_ref[:, cols] = (silu * u).astype(BF16)

    w_spec = pl.BlockSpec((None, D, Nj), lambda j, i: (j, 0, 0))
    o_spec = pl.BlockSpec((tm, Nj), lambda j, i: (i, j))
    o = jax.ShapeDtypeStruct((S, J * Nj), BF16)
    return _pcall(body, name="ffn_gate_up", grid=(J, S // tm),
                  in_specs=[pl.BlockSpec((tm, D), lambda j, i: (i, 0)), w_spec, w_spec],
                  out_specs=[o_spec, o_spec, o_spec], out_shape=[o, o, o], operands=[u2, wg_st, wu_st], comm=comm)


def _ffn_gate_up_bwd(dh2b, wd, dg_fac, du_fac, tm, tn, comm=None):
    S, D = dh2b.shape
    F = wd.shape[0]
    tm, tn = _tile(S, tm), _tile(F, tn, LANES)

    def body(a_ref, wd_ref, dg_ref, du_ref, dgt_ref, dup_ref):
        a = a_ref[...]
        for cols in _col_blocks(tn):
            d = _dot(a, wd_ref[cols, :], NT)
            dgt_ref[:, cols] = (d * dg_ref[:, cols].astype(F32)).astype(BF16)
            dup_ref[:, cols] = (d * du_ref[:, cols].astype(F32)).astype(BF16)

    blk = pl.BlockSpec((tm, tn), lambda j, i: (i, j))
    o = jax.ShapeDtypeStruct((S, F), BF16)
    return _pcall(body, name="ffn_gate_up_bwd", grid=(F // tn, S // tm),
                  in_specs=[pl.BlockSpec((tm, D), lambda j, i: (i, 0)), pl.BlockSpec((tn, D), lambda j, i: (j, 0)), blk, blk],
                  out_specs=[blk, blk], out_shape=[o, o], operands=[dh2b, wd, dg_fac, du_fac], comm=comm)


def _mm_tn(name, a, b, tmo, tn, tk, stacked_cols=None, comm=None):
    S, Mo = a.shape
    N = b.shape[1]
    tmo, tk = _tile(Mo, tmo, LANES), _tile(S, tk)
    if stacked_cols is None:
        tn = _tile(N, tn, LANES)
        out_shape = jax.ShapeDtypeStruct((Mo, N), F32)
        out_spec = pl.BlockSpec((tmo, tn), lambda i, j, k: (i, j))
    elif stacked_cols % MXU_COLUMNS and (N // stacked_cols) % 2 == 0:
        tn = 2 * stacked_cols
        out_shape = jax.ShapeDtypeStruct((N // stacked_cols, Mo, stacked_cols), F32)
        out_spec = pl.BlockSpec((2, tmo, stacked_cols), lambda i, j, k: (j, i, 0))
    else:
        tn = stacked_cols
        out_shape = jax.ShapeDtypeStruct((N // tn, Mo, tn), F32)
        out_spec = pl.BlockSpec((None, tmo, tn), lambda i, j, k: (j, i, 0))
    return _matmul(
        name, [(a, pl.BlockSpec((tk, tmo), lambda i, j, k: (k, i)), b, pl.BlockSpec((tk, tn), lambda i, j, k: (k, j)))],
        TN, (Mo // tmo, N // tn, S // tk), out_shape, out_spec, (tmo, tn), comm=comm)


def _rms_fwd(name, x, w, tm, comm=None):
    S, D = x.shape
    tm = _tile(S, tm)

    def body(x_ref, w_ref, o_ref):
        xv = x_ref[...]
        r = lax.rsqrt(jnp.mean(xv * xv, axis=-1, keepdims=True) + EPS)
        o_ref[...] = ((xv * r) * w_ref[...]).astype(BF16)

    row = pl.BlockSpec((tm, D), lambda i: (i, 0))
    return _pcall(body, name=name, grid=(S // tm,), in_specs=[row, pl.BlockSpec((1, D), lambda i: (0, 0))], out_specs=[row],
                  out_shape=[jax.ShapeDtypeStruct((S, D), BF16)], operands=[x, w], comm=comm)


def _rms_bwd(name, x, w, dy, dres, tm, comm=None):
    S, D = x.shape
    tm = _tile(S, tm)

    def body(x_ref, w_ref, dy_ref, dres_ref, dx_ref, dw_ref):
        i = pl.program_id(0)

        @pl.when(i == 0)
        def _():
            dw_ref[...] = jnp.zeros_like(dw_ref)

        xv = x_ref[...]
        r = lax.rsqrt(jnp.mean(xv * xv, axis=-1, keepdims=True) + EPS)
        nv = xv * r
        dyv = dy_ref[...]
        dn = dyv * w_ref[...]
        dw_ref[...] += jnp.sum(dyv * nv, axis=0, keepdims=True)
        dx = dres_ref[...] + r * (dn - nv * jnp.mean(dn * nv, axis=-1, keepdims=True))
        dx_ref[...] = dx

    row = pl.BlockSpec((tm, D), lambda i: (i, 0))
    vec = pl.BlockSpec((1, D), lambda i: (0, 0))
    return _pcall(body, name=name, grid=(S // tm,), in_specs=[row, vec, row, row], out_specs=[row, vec],
                  out_shape=[jax.ShapeDtypeStruct((S, D), F32), jax.ShapeDtypeStruct((1, D), F32)],
                  operands=[x, w, dy, dres], comm=comm)


def _rms_bwd_dy(h1, w, du2, dh2, w_out, tm, comm=None):
    S, D = h1.shape
    d_mix = w_out.shape[0]
    tm = _tile(S, tm)

    def body(x_ref, w_ref, dy_ref, dres_ref, wo_ref, dx_ref, dxb_ref, out_ref, dw_ref):
        i = pl.program_id(0)

        @pl.when(i == 0)
        def _():
            dw_ref[...] = jnp.zeros_like(dw_ref)

        xv = x_ref[...]
        r = lax.rsqrt(jnp.mean(xv * xv, axis=-1, keepdims=True) + EPS)
        nv = xv * r
        dyv = dy_ref[...]
        dn = dyv * w_ref[...]
        dw_ref[...] += jnp.sum(dyv * nv, axis=0, keepdims=True)
        dx = dres_ref[...] + r * (dn - nv * jnp.mean(dn * nv, axis=-1, keepdims=True))
        dx_ref[...] = dx
        dxb = dx.astype(BF16)
        dxb_ref[...] = dxb
        out_ref[...] = _dot(dxb, wo_ref[...], NT)

    row = pl.BlockSpec((tm, D), lambda i: (i, 0))
    vec = pl.BlockSpec((1, D), lambda i: (0, 0))
    return _pcall(
        body, name="rms2_bwd_dy", grid=(S // tm,),
        in_specs=[row, vec, row, row, pl.BlockSpec((d_mix, D), lambda i: (0, 0), pipeline_mode=pl.Buffered(1))],
        out_specs=[row, row, pl.BlockSpec((tm, d_mix), lambda i: (i, 0)), vec],
        out_shape=[jax.ShapeDtypeStruct((S, D), F32), jax.ShapeDtypeStruct((S, D), BF16),
                   jax.ShapeDtypeStruct((S, d_mix), F32), jax.ShapeDtypeStruct((1, D), F32)],
        operands=[h1, w, du2, dh2, w_out], comm=comm)


def _out_proj_rms(y, w_out, x, ln_w, tm, comm=None):
    S, K = y.shape
    D = w_out.shape[1]
    tm = _tile(S, tm)

    def body(a_ref, w_ref, x_ref, lw_ref, h_ref, u_ref):
        hv = _dot(a_ref[...], w_ref[...]) + x_ref[...]
        h_ref[...] = hv
        r = lax.rsqrt(jnp.mean(hv * hv, axis=-1, keepdims=True) + EPS)
        u_ref[...] = ((hv * r) * lw_ref[...]).astype(BF16)

    row = pl.BlockSpec((tm, D), lambda i: (i, 0))
    return _pcall(
        body, name="out_proj", grid=(S // tm,),
        in_specs=[pl.BlockSpec((tm, K), lambda i: (i, 0)),
                  pl.BlockSpec((K, D), lambda i: (0, 0), pipeline_mode=pl.Buffered(1)), row,
                  pl.BlockSpec((1, D), lambda i: (0, 0))],
        out_specs=[row, row], out_shape=[jax.ShapeDtypeStruct((S, D), F32), jax.ShapeDtypeStruct((S, D), BF16)],
        operands=[y, w_out, x, ln_w], comm=comm)


def _ffn_down_loss(ff, wd, h1, tgt, fw, tm):
    S, K = ff.shape
    D = wd.shape[1]
    tm = _tile(S, tm)

    def body(a_ref, wd_ref, h1_ref, t_ref, w_ref, dh_ref, dhb_ref, dw_ref, loss_ref):
        i = pl.program_id(0)

        @pl.when(i == 0)
        def _():
            dw_ref[...] = jnp.zeros_like(dw_ref)
            loss_ref[...] = jnp.zeros_like(loss_ref)

        hv = _dot(a_ref[...], wd_ref[...]) + h1_ref[...]
        wv = w_ref[...]
        r = lax.rsqrt(jnp.mean(hv * hv, axis=-1, keepdims=True) + EPS)
        nv = hv * r
        err = nv * wv - t_ref[...]
        row_loss = jnp.mean(err * err, axis=-1, keepdims=True)
        loss_ref[...] += 0.5 * jnp.sum(row_loss, axis=0, keepdims=True)
        dyo = err * (1.0 / D)
        dn = dyo * wv
        dw_ref[...] += jnp.sum(dyo * nv, axis=0, keepdims=True)
        dh = r * (dn - nv * jnp.mean(dn * nv, axis=-1, keepdims=True))
        dh_ref[...] = dh
        dhb_ref[...] = dh.astype(BF16)

    row = pl.BlockSpec((tm, D), lambda i: (i, 0))
    vec = pl.BlockSpec((1, D), lambda i: (0, 0))
    return _pcall(
        body, name="ffn_down_loss", grid=(S // tm,),
        in_specs=[pl.BlockSpec((tm, K), lambda i: (i, 0)),
                  pl.BlockSpec((K, D), lambda i: (0, 0), pipeline_mode=pl.Buffered(1)), row, row, vec],
        out_specs=[row, row, vec, pl.BlockSpec((1, LANES), lambda i: (0, 0))],
        out_shape=[jax.ShapeDtypeStruct((S, D), F32), jax.ShapeDtypeStruct((S, D), BF16),
                   jax.ShapeDtypeStruct((1, D), F32), jax.ShapeDtypeStruct((1, LANES), F32)],
        operands=[ff, wd, h1, tgt, fw])


def _shift_down(x, d, head8):
    r = pltpu.roll(x, d, 0)
    rh = pltpu.roll(head8, d, 0)
    row8 = lax.broadcasted_iota(jnp.int32, head8.shape, 0)
    top = jnp.where(row8 < d, rh, r[0:8])
    return jnp.concatenate([top, r[8:]], axis=0)


def _shift_up(x, d, tail8):
    n = x.shape[0]
    r = pltpu.roll(x, n - d, 0)
    rt = pltpu.roll(tail8, 8 - d, 0)
    row8 = lax.broadcasted_iota(jnp.int32, tail8.shape, 0)
    bot = jnp.where(row8 + d >= 8, rt, r[n - 8:n])
    return jnp.concatenate([r[:n - 8], bot], axis=0)


def _roll_in_groups(x, d):
    n, c = x.shape
    return pltpu.roll(x.reshape(n // 8, 8, c), d, 1).reshape(n, c)


def _log_sigmoid(lam):
    z = jnp.exp(-jnp.abs(lam))
    u = 1.0 + z
    log1p = jnp.where(u == 1.0, z, jnp.log(u) * (z / jnp.where(u == 1.0, 1.0, u - 1.0)))
    return jnp.minimum(lam, 0.0) - log1p


def _neg_expm1(z, exp_z):
    series = -z * (1.0 + z * (0.5 + z * (1.0 / 6.0)))
    return jnp.where(z > -0.02, series, 1.0 - exp_z)


_GELU_C = 0.7978845608028654


def _gelu(x):
    t = jnp.tanh(_GELU_C * (x + 0.044715 * (x * x * x)))
    return x * (0.5 * (1.0 + t)), t


def _gelu_grad(x, t):
    return 0.5 * (1.0 + t) + 0.5 * x * (1.0 - t * t) * (_GELU_C * (1.0 + 3.0 * 0.044715 * (x * x)))


def _lx_shifts(lx, head8):
    return [lx] + [_shift_down(lx, d, head8) for d in (1, 2, 3)]


def _lru_gates(lx, head8, cw, cb, wa_ref, ba, wx_ref, bx, ls):
    nb = wa_ref.shape[0]
    sh = _lx_shifts(lx, head8)
    cx = cb + sh[3] * cw[0:1]
    cx = cx + sh[2] * cw[1:2]
    cx = cx + sh[1] * cw[2:3]
    cx = cx + sh[0] * cw[3:4]
    cxb = cx.astype(BF16)
    ra = jnp.concatenate([_dot(cxb[:, n * HEAD_DIM:(n + 1) * HEAD_DIM], wa_ref[n]) for n in range(nb)], axis=1) + ba
    ia = jnp.concatenate([_dot(cxb[:, n * HEAD_DIM:(n + 1) * HEAD_DIM], wx_ref[n]) for n in range(nb)], axis=1) + bx
    r = jax.nn.sigmoid(ra)
    ig = jax.nn.sigmoid(ia)
    log_a = LRU_C * r * ls
    a = jnp.exp(log_a)
    return cx, r, ig, a, jnp.sqrt(_neg_expm1(2.0 * log_a, a * a))


def _lru_specs(tl, DL):
    nb = DL // HEAD_DIM
    vec = pl.BlockSpec((1, DL), lambda i: (0, 0))
    return [pl.BlockSpec((CONV_W, DL), lambda i: (0, 0)), vec,
            pl.BlockSpec((nb, HEAD_DIM, HEAD_DIM), lambda i: (0, 0, 0)), vec,
            pl.BlockSpec((nb, HEAD_DIM, HEAD_DIM), lambda i: (0, 0, 0)), vec, vec]


def _lru_fwd(proj, cw, cb, wa, ba, wx, bx, lam, tl, d_mix, comm=None):
    S = proj.shape[0]
    DL = cb.shape[1]
    tl = _tile(S, tl)

    def body(lx_ref, lg_ref, cw_ref, cb_ref, wa_ref, ba_ref, wx_ref, bx_ref, lam_ref, h_ref, kept_ref, y_ref,
             prev8, hc, a_s, b_s):
        i = pl.program_id(0)

        @pl.when(i == 0)
        def _():
            prev8[...] = jnp.zeros_like(prev8)
            hc[...] = jnp.zeros_like(hc)

        lx = lx_ref[...]
        ls = _log_sigmoid(lam_ref[...])
        kept = _lru_gates(lx, prev8[...], cw_ref[...], cb_ref[...], wa_ref, ba_ref[...], wx_ref, bx_ref[...], ls)
        for n, val in enumerate(kept):
            kept_ref[:, n * DL:(n + 1) * DL] = val
        cx, _, ig, a, mult = kept
        b = mult * (ig * cx)
        row = lax.broadcasted_iota(jnp.int32, a.shape, 0) & 7
        for d in (1, 2, 4):
            a_sh = _roll_in_groups(a, d)
            b_sh = _roll_in_groups(b, d)
            m = row >= d
            b = jnp.where(m, a * b_sh + b, b)
            a = jnp.where(m, a * a_sh, a)
        a_s[...] = a
        b_s[...] = b

        def step(g, hprev):
            sl = pl.ds(pl.multiple_of(g * 8, 8), 8)
            hh = a_s[sl, :] * hprev + b_s[sl, :]
            h_ref[sl, :] = hh
            return hh[7:8, :]

        hc[0:1, :] = lax.fori_loop(0, tl // 8, step, hc[0:1, :])
        prev8[...] = lx[tl - 8:tl]
        g, _ = _gelu(lg_ref[...])
        y_ref[...] = (h_ref[...] * g).astype(BF16)

    return _pcall(
        body, name="lru_fwd", grid=(S // tl,),
        in_specs=[pl.BlockSpec((tl, DL), lambda i: (i, 0)), pl.BlockSpec((tl, DL), lambda i: (i, 1))] + _lru_specs(tl, DL),
        out_specs=[pl.BlockSpec((tl, DL), lambda i: (i, 0)), pl.BlockSpec((tl, 5 * DL), lambda i: (i, 0)),
                   pl.BlockSpec((tl, DL), lambda i: (i, 0))],
        out_shape=[jax.ShapeDtypeStruct((S, DL), F32), jax.ShapeDtypeStruct((S, 5 * DL), F32),
                   jax.ShapeDtypeStruct((S, d_mix), BF16)],
        scratch_shapes=[pltpu.VMEM((8, DL), F32), pltpu.VMEM((8, DL), F32), pltpu.VMEM((tl, DL), F32), pltpu.VMEM((tl, DL), F32)],
        operands=[proj, proj, cw, cb, wa, ba, wx, bx, lam], comm=comm)


def _lru_bwd(proj, h, kept, dy, cw, wa, wx, lam, tl, comm=None):
    S = proj.shape[0]
    DL = lam.shape[1]
    nb = DL // HEAD_DIM
    tl = _tile(S, tl)
    nt = S // tl
    ng = tl // 8
    t8 = tl // 8

    def body(lx_ref, lxp_ref, lg_ref, h_ref, hp_ref, kept_ref, dy_ref, cw_ref, wa_ref, wx_ref, lam_ref,
             dlxg_ref, dcw_ref, dcb_ref, dwa_ref, dba_ref, dwx_ref, dbx_ref, dlam_ref,
             a_next, g_carry, dcx_next, an_s, dh_s, g_s):
        i = pl.program_id(0)

        @pl.when(i == 0)
        def _():
            for ref in (dcw_ref, dcb_ref, dwa_ref, dba_ref, dwx_ref, dbx_ref, dlam_ref, a_next, g_carry, dcx_next):
                ref[...] = jnp.zeros_like(ref)

        first = i == nt - 1
        hv = h_ref[...]
        lg = lg_ref[...]
        dyv = dy_ref[...]
        hhead8 = jnp.where(first, 0.0, hp_ref[...])
        lamv = lam_ref[...]
        ls = _log_sigmoid(lamv)
        cwv = cw_ref[...]
        sh = _lx_shifts(lx_ref[...], jnp.where(first, 0.0, lxp_ref[...]))
        cx, r, ig, a, mult = (kept_ref[:, n * DL:(n + 1) * DL] for n in range(5))
        cxb = cx.astype(BF16)
        hprev = _shift_down(hv, 1, hhead8)
        g, t = _gelu(lg)
        dlg = dyv * hv * _gelu_grad(lg, t)
        dh = dyv * g
        an = _shift_up(a, 1, a_next[...])
        row = lax.broadcasted_iota(jnp.int32, a.shape, 0) & 7
        for d in (1, 2, 4):
            an_sh = _roll_in_groups(an, 8 - d)
            dh_sh = _roll_in_groups(dh, 8 - d)
            m = row + d < 8
            dh = jnp.where(m, an * dh_sh + dh, dh)
            an = jnp.where(m, an * an_sh, an)
        an_s[...] = an
        dh_s[...] = dh

        def step(k, gc):
            sl = pl.ds(pl.multiple_of((ng - 1 - k) * 8, 8), 8)
            gg = an_s[sl, :] * gc + dh_s[sl, :]
            g_s[sl, :] = gg
            return gg[0:1, :]

        g_carry[0:1, :] = lax.fori_loop(0, ng, step, g_carry[0:1, :])
        a_next[...] = a[0:8]
        G = g_s[...]
        da = G * hprev
        icx = ig * cx
        dmult = G * icx
        dicx = G * mult
        di = dicx * cx
        dcx = dicx * ig
        dlog = da * a - dmult * ((a * a) * lax.rsqrt(mult * mult))
        dr = dlog * (LRU_C * ls)
        dlam_ref[...] += jnp.sum(dlog * (LRU_C * r), axis=0, keepdims=True)
        dra = dr * r * (1.0 - r)
        dia = di * ig * (1.0 - ig)
        dba_ref[...] += jnp.sum(dra, axis=0, keepdims=True)
        dbx_ref[...] += jnp.sum(dia, axis=0, keepdims=True)
        drab = dra.astype(BF16)
        diab = dia.astype(BF16)
        back = []
        for n in range(nb):
            cs = slice(n * HEAD_DIM, (n + 1) * HEAD_DIM)
            dwa_ref[n] += _dot(cxb[:, cs], drab[:, cs], TN)
            dwx_ref[n] += _dot(cxb[:, cs], diab[:, cs], TN)
            back.append(_dot(drab[:, cs], wa_ref[n], NT) + _dot(diab[:, cs], wx_ref[n], NT))
        dcx = dcx + jnp.concatenate(back, axis=1)
        dcb_ref[...] += jnp.sum(dcx, axis=0, keepdims=True)
        for tap in range(CONV_W):
            dcw_ref[tap:tap + 1, :] += jnp.sum(dcx * sh[CONV_W - 1 - tap], axis=0, keepdims=True)
        tail = dcx_next[...]
        dlx = dcx * cwv[3:4]
        for d in (1, 2, 3):
            dlx = dlx + _shift_up(dcx, d, tail) * cwv[3 - d:4 - d]
        dcx_next[...] = dcx[0:8]
        dlxg_ref[:, 0:DL] = dlx.astype(BF16)
        dlxg_ref[:, DL:2 * DL] = dlg.astype(BF16)

        @pl.when(i == nt - 1)
        def _():
            dlam_ref[...] = dlam_ref[...] * (1.0 - jax.nn.sigmoid(lamv))

    rev = lambda i: nt - 1 - i
    prev8_map = lambda i: (jnp.maximum((nt - 1 - i) * t8 - 1, 0), 0)
    vec = pl.BlockSpec((1, DL), lambda i: (0, 0))
    mat = pl.BlockSpec((nb, HEAD_DIM, HEAD_DIM), lambda i: (0, 0, 0))
    return _pcall(
        body, name="lru_bwd", grid=(nt,), operands=[proj, proj, proj, h, h, kept, dy, cw, wa, wx, lam], comm=comm,
        in_specs=[pl.BlockSpec((tl, DL), lambda i: (rev(i), 0)), pl.BlockSpec((8, DL), prev8_map),
                  pl.BlockSpec((tl, DL), lambda i: (rev(i), 1)),
                  pl.BlockSpec((tl, DL), lambda i: (rev(i), 0)), pl.BlockSpec((8, DL), prev8_map),
                  pl.BlockSpec((tl, 5 * DL), lambda i: (rev(i), 0)),
                  pl.BlockSpec((tl, DL), lambda i: (rev(i), 0)), pl.BlockSpec((CONV_W, DL), lambda i: (0, 0)), mat, mat, vec],
        out_specs=[pl.BlockSpec((tl, 2 * DL), lambda i: (rev(i), 0)), pl.BlockSpec((CONV_W, DL), lambda i: (0, 0)), vec,
                   mat, vec, mat, vec, vec],
        out_shape=[jax.ShapeDtypeStruct(proj.shape, BF16), jax.ShapeDtypeStruct((CONV_W, DL), F32),
                   jax.ShapeDtypeStruct((1, DL), F32), jax.ShapeDtypeStruct((nb, HEAD_DIM, HEAD_DIM), F32),
                   jax.ShapeDtypeStruct((1, DL), F32), jax.ShapeDtypeStruct((nb, HEAD_DIM, HEAD_DIM), F32),
                   jax.ShapeDtypeStruct((1, DL), F32), jax.ShapeDtypeStruct((1, DL), F32)],
        scratch_shapes=[pltpu.VMEM((8, DL), F32), pltpu.VMEM((8, DL), F32), pltpu.VMEM((8, DL), F32),
                        pltpu.VMEM((tl, DL), F32), pltpu.VMEM((tl, DL), F32), pltpu.VMEM((tl, DL), F32)])


def _ret_tables(S, H):
    pos = jnp.arange(S, dtype=F32)
    inv_freq = ROPE_BASE ** (-jnp.arange(0, HEAD_DIM, 2, dtype=F32) / HEAD_DIM)
    ang = pos[:, None] * inv_freq[None, :]
    cos, sin = jnp.cos(ang), jnp.sin(ang)
    cosf = jnp.concatenate([cos, cos], axis=1)
    sins = jnp.concatenate([-sin, sin], axis=1)
    log_gamma = jnp.log1p(-jnp.exp2(-5.0 - jnp.arange(H, dtype=F32)))
    idx = jnp.arange(CHUNK)
    diff = idx[:, None] - idx[None, :]
    causal = diff >= 0
    decay = jnp.where(causal[None], jnp.exp(log_gamma[:, None, None] * jnp.where(causal, diff, 0)[None].astype(F32)), 0.0)
    zeta = jnp.exp(log_gamma[:, None] * (CHUNK - 1 - idx).astype(F32)[None, :])
    xi = jnp.exp(log_gamma[:, None] * (idx + 1).astype(F32)[None, :])
    gc = jnp.exp(log_gamma * CHUNK)
    lanes = (H, CHUNK, HEAD_DIM)
    return (cosf, sins, decay, jnp.broadcast_to(zeta[:, :, None], lanes), jnp.broadcast_to(xi[:, :, None], lanes),
            jnp.broadcast_to(gc[:, None, None], lanes))


def _rope(t, cos, sin_signed):
    return t * cos + pltpu.roll(t, HEAD_DIM // 2, 1) * sin_signed


def _rope_t(d, cos, sin_signed):
    return d * cos + pltpu.roll(d * sin_signed, HEAD_DIM // 2, 1)


def _ret_const_specs(H, DR):
    full = pl.BlockSpec((H, CHUNK, HEAD_DIM), lambda *_: (0, 0, 0))
    return [full, full, full, full, pl.BlockSpec((1, DR), lambda *_: (0, 0))]


def _ret_fwd(proj, y, tables, gnw, tb, comm=None):
    S = proj.shape[0]
    DR = gnw.shape[1]
    H = DR // HEAD_DIM
    tb = _tile(S, tb, CHUNK)
    nc = tb // CHUNK
    cosf, sins, dm, zeta, xi, gc = tables
    scale = HEAD_DIM ** -0.5

    def body(qk_ref, vg_ref, cos_ref, sin_ref, dm_ref, zeta_ref, xi_ref, gc_ref, gnw_ref, y_in, y_ref, rprev_ref, r_s):
        del y_in
        i = pl.program_id(0)

        @pl.when(i == 0)
        def _():
            r_s[...] = jnp.zeros_like(r_s)

        def chunk(c, carry):
            rows = pl.ds(pl.multiple_of(c * CHUNK, CHUNK), CHUNK)
            cos = cos_ref[rows, :]
            sin = sin_ref[rows, :]
            heads = range(H)
            c0 = [slice(h * HEAD_DIM, (h + 1) * HEAD_DIM) for h in heads]
            c1 = [slice(DR + h * HEAD_DIM, DR + (h + 1) * HEAD_DIM) for h in heads]
            qh = [_rope(qk_ref[rows, c0[h]], cos, sin) for h in heads]
            kh = [_rope(qk_ref[rows, c1[h]], cos, sin) * scale for h in heads]
            vb = [vg_ref[rows, c0[h]].astype(BF16) for h in heads]
            rp = [r_s[h] for h in heads]
            rpb = [rp[h].astype(BF16) for h in heads]
            s = [_dot(qh[h].astype(BF16), kh[h].astype(BF16), NT) for h in heads]
            kv = [_dot((kh[h] * zeta_ref[h]).astype(BF16), vb[h], TN) for h in heads]
            cross = [_dot((qh[h] * xi_ref[h]).astype(BF16), rpb[h]) for h in heads]
            o = [_dot((s[h] * dm_ref[h]).astype(BF16), vb[h]) + cross[h] for h in heads]
            for h in heads:
                rprev_ref[c, h] = rpb[h]
                r_s[h] = rp[h] * gc_ref[h] + kv[h]
                mu = jnp.mean(o[h], axis=-1, keepdims=True)
                oc = o[h] - mu
                var = jnp.mean(oc * oc, axis=-1, keepdims=True)
                on = oc * lax.rsqrt(var + EPS) * gnw_ref[:, c0[h]]
                gate = vg_ref[rows, c1[h]]
                y_ref[rows, c0[h]] = (gate * jax.nn.sigmoid(gate) * on).astype(BF16)
            return carry

        lax.fori_loop(0, nc, chunk, 0)

    return _pcall(
        body, name="ret_fwd", grid=(S // tb,),
        in_specs=[pl.BlockSpec((tb, 2 * DR), lambda i: (i, 1)), pl.BlockSpec((tb, 2 * DR), lambda i: (i, 2)),
                  pl.BlockSpec((tb, HEAD_DIM), lambda i: (i, 0)), pl.BlockSpec((tb, HEAD_DIM), lambda i: (i, 0))]
        + _ret_const_specs(H, DR) + [HBM_SPEC],
        out_specs=[pl.BlockSpec((tb, DR), lambda i: (i, 1)),
                   pl.BlockSpec((nc, H, CHUNK, HEAD_DIM), lambda i: (i, 0, 0, 0))],
        out_shape=[jax.ShapeDtypeStruct(y.shape, BF16), jax.ShapeDtypeStruct((S // CHUNK, H, CHUNK, HEAD_DIM), BF16)],
        scratch_shapes=[pltpu.VMEM((H, CHUNK, HEAD_DIM), F32)], aliases={9: 0},
        operands=[proj, proj, cosf, sins, dm, zeta, xi, gc, gnw, y], comm=comm)


def _ret_bwd(proj, rprev, dy, dproj, tables, gnw, tb, comm=None):
    S = proj.shape[0]
    DR = gnw.shape[1]
    H = DR // HEAD_DIM
    tb = _tile(S, tb, CHUNK)
    nc = tb // CHUNK
    nt = S // tb
    cosf, sins, dm, zeta, xi, gc = tables
    scale = HEAD_DIM ** -0.5

    def body(qk_ref, vg_ref, cos_ref, sin_ref, dm_ref, zeta_ref, xi_ref, gc_ref, gnw_ref, rprev_ref, dy_ref, dp_in,
             dp_ref, dgn_ref, dr_s, dqk_s, dvg_s, out_sems):
        del dp_in
        i = pl.program_id(0)
        slot = i % 2

        def out_copies(step, sl):
            rows = pl.ds(pl.multiple_of((nt - 1 - step) * tb, tb), tb)
            return (pltpu.make_async_copy(dqk_s.at[sl], dp_ref.at[rows, pl.ds(2 * DR, 2 * DR)], out_sems.at[sl, 0]),
                    pltpu.make_async_copy(dvg_s.at[sl], dp_ref.at[rows, pl.ds(4 * DR, 2 * DR)], out_sems.at[sl, 1]))

        @pl.when(i == 0)
        def _():
            dr_s[...] = jnp.zeros_like(dr_s)
            dgn_ref[...] = jnp.zeros_like(dgn_ref)

        @pl.when(i >= 2)
        def _():
            for cp in out_copies(i - 2, slot):
                cp.wait()

        def chunk(cc, carry):
            c = nc - 1 - cc
            rows = pl.ds(pl.multiple_of(c * CHUNK, CHUNK), CHUNK)
            cos = cos_ref[rows, :]
            sin = sin_ref[rows, :]
            heads = range(H)
            c0 = [slice(h * HEAD_DIM, (h + 1) * HEAD_DIM) for h in heads]
            c1 = [slice(DR + h * HEAD_DIM, DR + (h + 1) * HEAD_DIM) for h in heads]
            qh = [_rope(qk_ref[rows, c0[h]], cos, sin) for h in heads]
            kh = [_rope(qk_ref[rows, c1[h]], cos, sin) * scale for h in heads]
            qb = [t.astype(BF16) for t in qh]
            kb = [t.astype(BF16) for t in kh]
            vb = [vg_ref[rows, c0[h]].astype(BF16) for h in heads]
            rpb = [rprev_ref[c, h] for h in heads]
            qx = [(qh[h] * xi_ref[h]).astype(BF16) for h in heads]
            kz = [(kh[h] * zeta_ref[h]).astype(BF16) for h in heads]
            drh = [dr_s[h] for h in heads]
            drb = [t.astype(BF16) for t in drh]
            s = [_dot(qb[h], kb[h], NT) for h in heads]
            cross = [_dot(qx[h], rpb[h]) for h in heads]
            dv_state = [_dot(kz[h], drb[h]) for h in heads]
            dk_state = [_dot(vb[h], drb[h], NT) for h in heads]
            sb = [(s[h] * dm_ref[h]).astype(BF16) for h in heads]
            o = [_dot(sb[h], vb[h]) + cross[h] for h in heads]
            dob = []
            for h in heads:
                mu = jnp.mean(o[h], axis=-1, keepdims=True)
                oc = o[h] - mu
                rstd = lax.rsqrt(jnp.mean(oc * oc, axis=-1, keepdims=True) + EPS)
                ohat = oc * rstd
                gw = gnw_ref[:, c0[h]]
                gate = vg_ref[rows, c1[h]]
                sg = jax.nn.sigmoid(gate)
                dyv = dy_ref[rows, c0[h]]
                dvg_s[slot, rows, c1[h]] = (dyv * (ohat * gw) * (sg * (1.0 + gate * (1.0 - sg)))).astype(BF16)
                don = dyv * (gate * sg)
                dgn_ref[:, c0[h]] += jnp.sum(don * ohat, axis=0, keepdims=True)
                dohat = don * gw
                do = rstd * (dohat - jnp.mean(dohat, axis=-1, keepdims=True)
                             - ohat * jnp.mean(dohat * ohat, axis=-1, keepdims=True))
                dob.append(do.astype(BF16))
            ds = [_dot(dob[h], vb[h], NT) for h in heads]
            dq_state = [_dot(dob[h], rpb[h], NT) for h in heads]
            dv = [_dot(sb[h], dob[h], TN) + dv_state[h] for h in heads]
            dr_new = [_dot(qx[h], dob[h], TN) for h in heads]
            dsb = [(ds[h] * dm_ref[h]).astype(BF16) for h in heads]
            dqh = [_dot(dsb[h], kb[h]) + dq_state[h] * xi_ref[h] for h in heads]
            dkh = [_dot(dsb[h], qb[h], TN) + dk_state[h] * zeta_ref[h] for h in heads]
            for h in heads:
                dr_s[h] = drh[h] * gc_ref[h] + dr_new[h]
                dqk_s[slot, rows, c0[h]] = _rope_t(dqh[h], cos, sin).astype(BF16)
                dqk_s[slot, rows, c1[h]] = _rope_t(dkh[h] * scale, cos, sin).astype(BF16)
                dvg_s[slot, rows, c0[h]] = dv[h].astype(BF16)
            return carry

        lax.fori_loop(0, nc, chunk, 0)
        for cp in out_copies(i, slot):
            cp.start()

        @pl.when(i == nt - 1)
        def _():
            if nt >= 2:
                for cp in out_copies(i - 1, 1 - slot):
                    cp.wait()
            for cp in out_copies(i, slot):
                cp.wait()

    rev = lambda i: nt - 1 - i
    return _pcall(
        body, name="ret_bwd", grid=(nt,), aliases={11: 0}, comm=comm,
        operands=[proj, proj, cosf, sins, dm, zeta, xi, gc, gnw, rprev, dy, dproj],
        in_specs=[pl.BlockSpec((tb, 2 * DR), lambda i: (rev(i), 1)), pl.BlockSpec((tb, 2 * DR), lambda i: (rev(i), 2)),
                  pl.BlockSpec((tb, HEAD_DIM), lambda i: (rev(i), 0)), pl.BlockSpec((tb, HEAD_DIM), lambda i: (rev(i), 0))]
        + _ret_const_specs(H, DR)
        + [pl.BlockSpec((nc, H, CHUNK, HEAD_DIM), lambda i: (rev(i), 0, 0, 0)),
           pl.BlockSpec((tb, DR), lambda i: (rev(i), 1)), HBM_SPEC],
        out_specs=[HBM_SPEC, pl.BlockSpec((1, DR), lambda i: (0, 0))],
        out_shape=[jax.ShapeDtypeStruct(dproj.shape, BF16), jax.ShapeDtypeStruct((1, DR), F32)],
        scratch_shapes=[pltpu.VMEM((H, CHUNK, HEAD_DIM), F32), pltpu.VMEM((2, tb, 2 * DR), BF16),
                        pltpu.VMEM((2, tb, 2 * DR), BF16), pltpu.SemaphoreType.DMA((2, 2))])


def _place():
    x, y, c = lax.axis_index("x"), lax.axis_index("y"), lax.axis_index("c")
    chips = [(1 - x, y), (x, 1 - y), (1 - x, 1 - y)]
    return x, y, c, chips


def _own_slab(name, shard, place):
    R, C = shard.shape
    tr = _row_tile(R, C)
    return _ew("cast_" + name, lambda a: (a,), [(shard, pl.BlockSpec((tr, C), lambda i, p: (i, 0)))],
               [(jax.ShapeDtypeStruct((4, R, C), BF16), pl.BlockSpec((None, tr, C), lambda i, p: (p[1], i, 0)))],
               (R // tr,), sp=place)[0]


class _remote:
    def __init__(self, src, dst, ssem, rsem, k, to):
        self.args = dict(src_ref=src, dst_ref=dst, send_sem=ssem.at[k], recv_sem=rsem.at[k], device_id=to,
                         device_id_type=MESH)

    def start(self):
        pltpu.make_async_remote_copy(**self.args).start()

    def wait_send(self):
        pltpu.make_async_remote_copy(**self.args).wait_send()

    def wait_recv(self):
        pltpu.make_async_remote_copy(**self.args).wait_recv()


def _task_fns(copies):
    def start(cins, couts, ssem, rsem, base):
        for cp in copies(cins, couts, ssem, rsem, base)[0]:
            cp.start()

    def finish(cins, couts, ssem, rsem, base):
        sends, recvs = copies(cins, couts, ssem, rsem, base)
        for cp in sends:
            cp.wait_send()
        for cp in recvs:
            cp.wait_recv()

    return start, finish


NEIGHBOURS, DIAGONAL = (0, 1), (2,)


def _gather_ici(st, which=NEIGHBOURS + DIAGONAL):
    r2 = st.shape[1] // 2

    def copies(cins, couts, ssem, rsem, base):
        x, y, c, chips = _place()
        out = couts[0]
        mine = out.at[2 * x + y, pl.ds(c * r2, r2), :]
        sends, recvs = [], []
        for k, j in enumerate(which):
            cx, cy = chips[j]
            got = out.at[2 * cx + cy, pl.ds(c * r2, r2), :]
            sends.append(_remote(mine, mine, ssem, rsem, base + k, (cx, cy, c)))
            recvs.append(_remote(got, got, ssem, rsem, base + k, (x, y, c)))
        return sends, recvs

    start, finish = _task_fns(copies)
    return _Comm([st], [jax.ShapeDtypeStruct(st.shape, st.dtype)], {0: 0}, len(which), start, finish)


def _gather_d2d(st):
    r2 = st.shape[1] // 2

    def copies(cins, couts, ssem, rsem, base):
        x, y, c, chips = _place()
        out = couts[0]
        sends, recvs = [], []
        for j, (cx, cy) in enumerate(chips):
            have = out.at[2 * cx + cy, pl.ds(c * r2, r2), :]
            want = out.at[2 * cx + cy, pl.ds((1 - c) * r2, r2), :]
            sends.append(_remote(have, have, ssem, rsem, base + j, (x, y, 1 - c)))
            recvs.append(_remote(want, want, ssem, rsem, base + j, (x, y, c)))
        return sends, recvs

    start, finish = _task_fns(copies)
    return _Comm([st], [jax.ShapeDtypeStruct(st.shape, st.dtype)], {0: 0}, 3, start, finish)


def _gather_conv(conv_w):
    def copies(cins, couts, ssem, rsem, base):
        x, y, c, chips = _place()
        src, out = cins[0], couts[0]
        sends = [_remote(src, out.at[2 * x + y], ssem, rsem, base + j, (*chip, c)) for j, chip in enumerate(chips)]
        recvs = [_remote(src, out.at[2 * cx + cy], ssem, rsem, base + j, (x, y, c)) for j, (cx, cy) in enumerate(chips)]
        return sends, recvs

    start, finish = _task_fns(copies)
    return _Comm([conv_w], [jax.ShapeDtypeStruct((4,) + conv_w.shape, conv_w.dtype)], {}, 3, start, finish)


def _pair_exchange(g):
    r2 = g.shape[1] // 2

    def copies(cins, couts, ssem, rsem, base):
        x, y, c, _ = _place()
        cp = _remote(cins[0].at[:, pl.ds((1 - c) * r2, r2), :], couts[0], ssem, rsem, base, (x, y, 1 - c))
        return [cp], [cp]

    start, finish = _task_fns(copies)
    return _Comm([g], [jax.ShapeDtypeStruct((g.shape[0], r2, g.shape[2]), g.dtype)], {}, 1, start, finish)


def _chip_exchange(part):
    def copies(cins, couts, ssem, rsem, base):
        x, y, c, chips = _place()
        cps = [_remote(cins[0].at[2 * cx + cy], couts[0].at[j], ssem, rsem, base + j, (cx, cy, c))
               for j, (cx, cy) in enumerate(chips)]
        return cps, cps

    start, finish = _task_fns(copies)
    return _Comm([part], [jax.ShapeDtypeStruct((3,) + part.shape[1:], part.dtype)], {}, 3, start, finish)


def _pair_share(slot):
    def copies(cins, couts, ssem, rsem, base):
        x, y, c, _ = _place()
        out = couts[0]
        return ([_remote(out.at[c], out.at[c], ssem, rsem, base, (x, y, 1 - c))],
                [_remote(out.at[1 - c], out.at[1 - c], ssem, rsem, base, (x, y, c))])

    start, finish = _task_fns(copies)
    return _Comm([slot], [jax.ShapeDtypeStruct(slot.shape, slot.dtype)], {0: 0}, 1, start, finish)


def _gather_small(sm):
    flips = [(fx, fy, fc) for fx in (0, 1) for fy in (0, 1) for fc in (0, 1)][1:]

    def copies(cins, couts, ssem, rsem, base):
        x, y, c, _ = _place()
        src, out = cins[0], couts[0]
        peers = [(1 - x if fx else x, 1 - y if fy else y, 1 - c if fc else c) for fx, fy, fc in flips]
        sends = [_remote(src, out.at[4 * x + 2 * y + c], ssem, rsem, base + k, peer) for k, peer in enumerate(peers)]
        recvs = [_remote(src, out.at[4 * px + 2 * py + pc], ssem, rsem, base + k, (x, y, c))
                 for k, (px, py, pc) in enumerate(peers)]
        return sends, recvs

    start, finish = _task_fns(copies)
    return _Comm([sm], [jax.ShapeDtypeStruct((8,) + sm.shape, sm.dtype)], {}, 7, start, finish)


def _comm_call(name, tasks):
    task = _merge(tasks)
    nci = len(task.ins)

    def body(*refs):
        cins, couts, (ssem, rsem) = refs[:nci], refs[nci:nci + len(task.outs)], refs[nci + len(task.outs):]
        task.start(cins, couts, ssem, rsem, 0)
        task.finish(cins, couts, ssem, rsem, 0)

    return pl.pallas_call(
        body, in_specs=[HBM_SPEC] * nci, out_specs=[HBM_SPEC] * len(task.outs), out_shape=list(task.outs),
        scratch_shapes=[pltpu.SemaphoreType.DMA((task.n_sem,)), pltpu.SemaphoreType.DMA((task.n_sem,))],
        input_output_aliases=task.aliases, name=name)(*task.ins)


def _adamw(w, g, m, v):
    m = ADAM_B1 * m + (1.0 - ADAM_B1) * g
    v = ADAM_B2 * v + (1.0 - ADAM_B2) * (g * g)
    m_hat = m / (1.0 - ADAM_B1 ** ADAM_STEP)
    v_hat = v / (1.0 - ADAM_B2 ** ADAM_STEP)
    delta = -ADAM_LR * (m_hat / (jnp.sqrt(v_hat) + ADAM_EPS) + ADAM_WD * w)
    return delta, m, v


def _adamw_call(name, w, g, m, v):
    R, C = w.shape
    tr = _row_tile(R, C, 1024 * 1024)
    row = pl.BlockSpec((tr, C), lambda i: (i, 0))
    o = jax.ShapeDtypeStruct((R, C), F32)
    return _ew(name, lambda w_, g_, m_, v_: (*_adamw(w_, g_, m_, v_), g_), [(w, row), (g, row), (m, row), (v, row)],
               [(o, row), (o, row), (o, row), (o, row)], (R // tr,))


def _pair_sum(name, g, ra, place):
    _, R, C = g.shape
    r2 = R // 2
    tr = _row_tile(r2, C)
    nb = r2 // tr
    own = pl.BlockSpec((None, tr, C), lambda j, i, p: (j, p[0] * nb + i, 0))
    blk = pl.BlockSpec((None, tr, C), lambda j, i, p: (j, i, 0))
    return _ew("rs_pair_sum_" + name, lambda a, b: (a + b,), [(g, own), (ra, blk)],
               [(jax.ShapeDtypeStruct((4, r2, C), BF16), blk)], (4, nb), sp=place)[0]


def _chip_sum(name, g, ra, rb, place):
    _, R, C = g.shape
    r2 = R // 2
    tr = _row_tile(r2, C)
    nb = r2 // tr
    own = pl.BlockSpec((None, tr, C), lambda i, p: (p[1], p[0] * nb + i, 0))
    mine = pl.BlockSpec((None, tr, C), lambda i, p: (p[1], i, 0))
    src = [pl.BlockSpec((None, tr, C), functools.partial(lambda i, p, j: (j, i, 0), j=j)) for j in range(3)]
    out = pl.BlockSpec((None, tr, C), lambda i, p: (p[0], i, 0))

    def total(a, b, r0, r1, r2_):
        return ((((a + b) + r0.astype(F32)) + r1.astype(F32)) + r2_.astype(F32),)

    return _ew("rs_chip_sum_" + name, total, [(g, own), (ra, mine), (rb, src[0]), (rb, src[1]), (rb, src[2])],
               [(jax.ShapeDtypeStruct((2, r2, C), F32), out)], (nb,), sp=place)[0]


def _pack(arrays):
    rows, offs, pos = [], [], 0
    for a in arrays:
        flat = a.reshape(-1)
        n = -(-flat.shape[0] // (8 * LANES)) * (8 * LANES)
        if n != flat.shape[0]:
            flat = jnp.pad(flat, (0, n - flat.shape[0]))
        rows.append(flat.reshape(-1, LANES))
        offs.append(pos)
        pos += n // LANES
    return jnp.concatenate(rows, axis=0), offs


def _unpack(packed, offs, shapes):
    out = []
    for off, shp in zip(offs, shapes):
        n = 1
        for s in shp:
            n *= s
        out.append(packed[off:off + -(-n // LANES)].reshape(-1)[:n].reshape(shp))
    return out


def _sum8(gathered):
    _, R, C = gathered.shape
    tr = _row_tile(R, C, 256 * 1024)
    specs = [pl.BlockSpec((None, tr, C), functools.partial(lambda i, d: (d, i, 0), d=d)) for d in range(8)]

    def fn(*parts):
        t = parts[0]
        for p in parts[1:]:
            t = t + p
        return (t,)

    return _ew("small_sum", fn, [(gathered, s) for s in specs],
               [(jax.ShapeDtypeStruct((R, C), F32), pl.BlockSpec((tr, C), lambda i: (i, 0)))], (R // tr,))[0]


BIG = ("w_in", "w_out", "w_ffn_gate", "w_ffn_up", "w_ffn_down")
SMALL = ("ln1_w", "conv_w", "conv_b", "gate_a_w", "gate_a_b", "gate_x_w", "gate_x_b", "lru_lambda", "ret_gn_w", "ln2_w",
         "final_norm_w")
WEIGHTS = ("ln1_w", "w_in", "conv_w", "conv_b", "gate_a_w", "gate_a_b", "gate_x_w", "gate_x_b", "lru_lambda", "ret_gn_w",
           "w_out", "ln2_w", "w_ffn_gate", "w_ffn_up", "w_ffn_down", "final_norm_w")


def kernel(x, ln1_w, w_in, conv_w, conv_b, gate_a_w, gate_a_b, gate_x_w, gate_x_b, lru_lambda, ret_gn_w, w_out, ln2_w, w_ffn_gate, w_ffn_up, w_ffn_down, final_norm_w, loss_target, m_ln1_w, m_w_in, m_conv_w, m_conv_b, m_gate_a_w, m_gate_a_b, m_gate_x_w, m_gate_x_b, m_lru_lambda, m_ret_gn_w, m_w_out, m_ln2_w, m_w_ffn_gate, m_w_ffn_up, m_w_ffn_down, m_final_norm_w, v_ln1_w, v_w_in, v_conv_w, v_conv_b, v_gate_a_w, v_gate_a_b, v_gate_x_w, v_gate_x_b, v_lru_lambda, v_ret_gn_w, v_w_out, v_ln2_w, v_w_ffn_gate, v_w_ffn_up, v_w_ffn_down, v_final_norm_w):
    w = dict(ln1_w=ln1_w, w_in=w_in, conv_w=conv_w, conv_b=conv_b, gate_a_w=gate_a_w, gate_a_b=gate_a_b, gate_x_w=gate_x_w,
             gate_x_b=gate_x_b, lru_lambda=lru_lambda, ret_gn_w=ret_gn_w, w_out=w_out, ln2_w=ln2_w, w_ffn_gate=w_ffn_gate,
             w_ffn_up=w_ffn_up, w_ffn_down=w_ffn_down, final_norm_w=final_norm_w)
    m = dict(ln1_w=m_ln1_w, w_in=m_w_in, conv_w=m_conv_w, conv_b=m_conv_b, gate_a_w=m_gate_a_w, gate_a_b=m_gate_a_b,
             gate_x_w=m_gate_x_w, gate_x_b=m_gate_x_b, lru_lambda=m_lru_lambda, ret_gn_w=m_ret_gn_w, w_out=m_w_out,
             ln2_w=m_ln2_w, w_ffn_gate=m_w_ffn_gate, w_ffn_up=m_w_ffn_up, w_ffn_down=m_w_ffn_down,
             final_norm_w=m_final_norm_w)
    v = dict(ln1_w=v_ln1_w, w_in=v_w_in, conv_w=v_conv_w, conv_b=v_conv_b, gate_a_w=v_gate_a_w, gate_a_b=v_gate_a_b,
             gate_x_w=v_gate_x_w, gate_x_b=v_gate_x_b, lru_lambda=v_lru_lambda, ret_gn_w=v_ret_gn_w, w_out=v_w_out,
             ln2_w=v_ln2_w, w_ffn_gate=v_w_ffn_gate, w_ffn_up=v_w_ffn_up, w_ffn_down=v_w_ffn_down,
             final_norm_w=v_final_norm_w)
    xs, tgt = x[0], loss_target[0]
    S, D = xs.shape
    DL, DR = conv_b.shape[1], ret_gn_w.shape[1]
    assert DL == DR and DL % HEAD_DIM == 0 and S % CHUNK == 0
    d_mix = DL + DR
    cx, cy, cc = lax.axis_index("x"), lax.axis_index("y"), lax.axis_index("c")
    chip = 2 * cx + cy
    place = jnp.stack([cc, chip]).astype(jnp.int32)
    grad, delta, new_m, new_v = {}, {}, {}, {}

    def finish_big(n, full):
        shp = w[n].shape
        g2 = full.reshape(shp[1], shp[2])
        w2, m2, v2 = (t[n].reshape(shp[1], shp[2]) for t in (w, m, v))
        d_, m_, v_, g_ = _adamw_call("adamw_" + n, w2, g2, m2, v2)
        grad[n], delta[n], new_m[n], new_v[n] = (t.reshape(shp) for t in (g_, d_, m_, v_))

    def all_sum(gathered, own):
        return _sum8(lax.dynamic_update_slice(gathered, own[None], (4 * cx + 2 * cy + cc, 0, 0)))

    st = {n: _own_slab(n, w[n][0], place) for n in BIG}
    (u1,), (w_in_st,) = _rms_fwd("rms1", xs, ln1_w, TM, comm=_gather_ici(st["w_in"]))
    w_in_st, conv_st = _comm_call("gather_w_in", [_gather_d2d(w_in_st), _gather_conv(conv_w[0])])
    conv_st = lax.dynamic_update_slice(conv_st, conv_w, (chip, 0, 0))
    cw_cols = conv_st.shape[2]
    conv_full = jnp.transpose(conv_st, (1, 0, 2)).reshape(CONV_W, 4 * cw_cols)
    n_in, n_ff = w_in_st.shape[2], st["w_ffn_gate"].shape[2]
    tables = _ret_tables(S, DR // HEAD_DIM)
    wab, wxb = gate_a_w[0].astype(BF16), gate_x_w[0].astype(BF16)
    lru_w = (conv_full, conv_b, wab, gate_a_b, wxb, gate_x_b, lru_lambda)

    proj, (w_out_st, wg_st) = _mm_nn_stacked("proj", u1, w_in_st, F32, TM_WIDE,
                                             comm=_merge([_gather_ici(st["w_out"]), _gather_ici(st["w_ffn_gate"])]))
    (hs, kept, y), (w_out_st, wg_st, wu_st) = _lru_fwd(
        proj, *lru_w, LRU_TILE, d_mix,
        comm=_merge([_gather_d2d(w_out_st), _gather_d2d(wg_st), _gather_ici(st["w_ffn_up"])]))
    (y, rprev), (wu_st, wd_st) = _ret_fwd(proj, y, tables, ret_gn_w, RET_BLOCK,
                                          comm=_merge([_gather_d2d(wu_st), _gather_ici(st["w_ffn_down"], NEIGHBOURS)]))
    w_out_f = w_out_st.reshape(d_mix, D)
    (h1, u2), (wd_st,) = _out_proj_rms(y, w_out_f, xs, ln2_w, TM, comm=_gather_ici(wd_st, DIAGONAL))
    (dg_fac, du_fac, ff), (wd_st,) = _ffn_gate_up(u2, wg_st, wu_st, TM, comm=_gather_d2d(wd_st))
    wd_f = wd_st.reshape(4 * n_ff, D)
    dh2, dh2b, d_fw, loss = _ffn_down_loss(ff, wd_f, h1, tgt, final_norm_w.reshape(1, D), TM_RESIDENT)

    g_wd = _mm_tn("g_w_down", ff, dh2b, n_ff, TILE_GRAD, TK_GRAD).reshape(4, n_ff, D)
    (dgt, dup), (ra_wd,) = _ffn_gate_up_bwd(dh2b, wd_f, dg_fac, du_fac, TM_WIDE, 2 * MXU_COLUMNS,
                                            comm=_pair_exchange(g_wd))
    pb_wd = _pair_sum("w_ffn_down", g_wd, ra_wd, place)
    g_wg, (rb_wd,) = _mm_tn("g_w_gate", u2, dgt, TILE_GRAD // 2, None, TK_GRAD, stacked_cols=n_ff,
                            comm=_chip_exchange(pb_wd))
    slot_wd = _chip_sum("w_ffn_down", g_wd, ra_wd, rb_wd, place)
    g_wu, (full_wd, ra_wg) = _mm_tn("g_w_up", u2, dup, TILE_GRAD // 2, None, TK_GRAD, stacked_cols=n_ff,
                                    comm=_merge([_pair_share(slot_wd), _pair_exchange(g_wg)]))
    finish_big("w_ffn_down", full_wd)
    pb_wg = _pair_sum("w_ffn_gate", g_wg, ra_wg, place)
    du2, (rb_wg,) = _mm_nt_stacked("d_u2_gate", dgt, wg_st, TM, comm=_chip_exchange(pb_wg))
    du2, (ra_wu,) = _mm_nt_stacked("d_u2_up", dup, wu_st, TM, res=du2, comm=_pair_exchange(g_wu))
    slot_wg = _chip_sum("w_ffn_gate", g_wg, ra_wg, rb_wg, place)
    pb_wu = _pair_sum("w_ffn_up", g_wu, ra_wu, place)
    (dh1, dh1b, dy, d_ln2), (full_wg,) = _rms_bwd_dy(h1, ln2_w, du2, dh2, w_out_f, TM_RESIDENT,
                                                     comm=_pair_share(slot_wg))
    finish_big("w_ffn_gate", full_wg)
    g_wout = _mm_tn("g_w_out", y, dh1b, TILE_GRAD, TILE_GRAD, TK_GRAD).reshape(4, d_mix // 4, D)
    (dproj, d_cw, d_cb, d_wa, d_ba, d_wx, d_bx, d_lam), (rb_wu, ra_wout) = _lru_bwd(
        proj, hs, kept, dy, conv_full, wab, wxb, lru_lambda, LRU_TILE,
        comm=_merge([_chip_exchange(pb_wu), _pair_exchange(g_wout)]))
    slot_wu = _chip_sum("w_ffn_up", g_wu, ra_wu, rb_wu, place)
    pb_wout = _pair_sum("w_out", g_wout, ra_wout, place)
    (dproj, d_gn), (full_wu, rb_wout) = _ret_bwd(proj, rprev, dy, dproj, tables, ret_gn_w, RET_BLOCK,
                                                 comm=_merge([_pair_share(slot_wu), _chip_exchange(pb_wout)]))
    finish_big("w_ffn_up", full_wu)
    slot_wout = _chip_sum("w_out", g_wout, ra_wout, rb_wout, place)
    small = dict(conv_w=d_cw, conv_b=d_cb, gate_a_w=d_wa, gate_a_b=d_ba, gate_x_w=d_wx, gate_x_b=d_bx, lru_lambda=d_lam,
                 ret_gn_w=d_gn, ln2_w=d_ln2, final_norm_w=d_fw)
    packed, offs = _pack([small[n] for n in SMALL[1:]] + [loss])
    g_win, (full_wout, got_small) = _mm_tn("g_w_in", u1, dproj, TILE_GRAD, None, TK_GRAD, stacked_cols=n_in,
                                           comm=_merge([_pair_share(slot_wout), _gather_small(packed)]))
    finish_big("w_out", full_wout)
    (ra_win,) = _comm_call("rs_pair_w_in", [_pair_exchange(g_win)])
    pb_win = _pair_sum("w_in", g_win, ra_win, place)
    du1, (rb_win,) = _mm_nt_stacked("d_u1", dproj, w_in_st, TM, comm=_chip_exchange(pb_win))
    slot_win = _chip_sum("w_in", g_win, ra_win, rb_win, place)
    gx, d_ln1 = _rms_bwd("rms1_bwd", xs, ln1_w, du1, dh1, TM)
    packed1, _ = _pack([d_ln1])
    full_win, got_ln1 = _comm_call("reduce_tail", [_pair_share(slot_win), _gather_small(packed1)])
    finish_big("w_in", full_win)

    red = _unpack(all_sum(got_small, packed), offs, [small[n].shape for n in SMALL[1:]] + [(1, LANES)])
    g = dict(zip(SMALL[1:], red[:-1]))
    g["ln1_w"] = all_sum(got_ln1, packed1)[:-(-D // LANES)].reshape(1, D)
    loss_out = red[-1][0, 0]
    g["conv_w"] = lax.dynamic_slice(g["conv_w"], (0, chip * cw_cols), (CONV_W, cw_cols))
    packs = [_pack([t[n] for n in SMALL])[0] for t in (w, m, v)]
    gp, offs2 = _pack([g[n] for n in SMALL])
    outs = _adamw_call("adamw_small", packs[0], gp, packs[1], packs[2])
    shapes = [w[n].shape for n in SMALL]
    for dst, arr in zip((delta, new_m, new_v), outs):
        dst.update(zip(SMALL, _unpack(arr, offs2, shapes)))
    for n in SMALL:
        grad[n] = g[n].reshape(w[n].shape)

    return (loss_out, gx.reshape(x.shape), *[grad[n] for n in WEIGHTS], *[delta[n] for n in WEIGHTS],
            *[new_m[n] for n in WEIGHTS], *[new_v[n] for n in WEIGHTS])
```

```python
import functools

import jax
import jax.numpy as jnp
from jax import lax
from jax.experimental import pallas as pl
from jax.experimental.pallas import tpu as pltpu

F32 = jnp.float32
BF16 = jnp.bfloat16
MESH = pl.DeviceIdType.MESH

EPS = 1e-6
LRU_C = 8.0
ROPE_BASE = 10000.0
CHUNK = 128
HEAD_DIM = 128
CONV_W = 4
ADAM_LR = 0.001
ADAM_B1 = 0.9
ADAM_B2 = 0.999
ADAM_EPS = 1e-08
ADAM_WD = 0.01
ADAM_STEP = 10

V7X_VMEM_BYTES = 64 * 1024 * 1024
VMEM_LIMIT = V7X_VMEM_BYTES - 8 * 1024 * 1024
LANES = 128
SUBLANES_16BIT = 16

TM = 512
TM_WIDE = 1024
TM_RESIDENT = 256
TK_GRAD = 2048
TILE_GRAD = 1024
LRU_TILE = 256
RET_BLOCK = 512

NN = (((1,), (0,)), ((), ()))
NT = (((1,), (1,)), ((), ()))
TN = (((0,), (0,)), ((), ()))


def _dot(a, b, dims=NN):
    return lax.dot_general(a, b, dims, preferred_element_type=F32)


def _tile(n, pref, mult=SUBLANES_16BIT):
    best = None
    t = mult
    while t <= min(n, pref):
        if n % t == 0:
            best = t
        t += mult
    return best if best is not None else n


def _row_tile(rows, cols, budget_bytes=2 * 1024 * 1024):
    return _tile(rows, max(SUBLANES_16BIT, budget_bytes // (cols * 4)))


def _params(sem):
    return pltpu.CompilerParams(dimension_semantics=sem, vmem_limit_bytes=VMEM_LIMIT)


HBM_SPEC = pl.BlockSpec(memory_space=pl.ANY)


class _Comm:
    def __init__(self, ins, outs, aliases, n_sem, start, finish):
        self.ins, self.outs, self.aliases, self.n_sem, self.start, self.finish = ins, outs, aliases, n_sem, start, finish


def _merge(tasks):
    ins, outs, aliases, plans, n_sem = [], [], {}, [], 0
    for t in tasks:
        i0, o0 = len(ins), len(outs)
        plans.append((t, i0, o0, n_sem))
        ins += t.ins
        outs += t.outs
        aliases.update({i0 + a: o0 + b for a, b in t.aliases.items()})
        n_sem += t.n_sem

    def run(which):
        def go(cins, couts, ssem, rsem, base):
            for t, i0, o0, s0 in plans:
                getattr(t, which)(cins[i0:i0 + len(t.ins)], couts[o0:o0 + len(t.outs)], ssem, rsem, base + s0)
        return go

    return _Comm(ins, outs, aliases, n_sem, run("start"), run("finish"))


def _pcall(body, *, name, grid, in_specs, out_specs, out_shape, operands, scratch_shapes=(), aliases=None, comm=None):
    n_in, n_out, n_scr = len(operands), len(out_shape), len(scratch_shapes)
    aliases = dict(aliases or {})
    params = _params(("arbitrary",) * len(grid))
    if comm is None:
        return pl.pallas_call(body, grid=grid, in_specs=list(in_specs), out_specs=list(out_specs), out_shape=list(out_shape),
                              scratch_shapes=list(scratch_shapes), input_output_aliases=aliases, name=name,
                              compiler_params=params)(*operands)
    nci, nco = len(comm.ins), len(comm.outs)

    def wrapped(*refs):
        ins, cins = refs[:n_in], refs[n_in:n_in + nci]
        o0 = n_in + nci
        outs, couts = refs[o0:o0 + n_out], refs[o0 + n_out:o0 + n_out + nco]
        s0 = o0 + n_out + nco
        scr, (ssem, rsem) = refs[s0:s0 + n_scr], refs[s0 + n_scr:]
        ids = [pl.program_id(a) for a in range(len(grid))]
        first = functools.reduce(jnp.logical_and, [i == 0 for i in ids])
        last = functools.reduce(jnp.logical_and, [i == g - 1 for i, g in zip(ids, grid)])

        @pl.when(first)
        def _():
            comm.start(cins, couts, ssem, rsem, 0)

        body(*ins, *outs, *scr)

        @pl.when(last)
        def _():
            comm.finish(cins, couts, ssem, rsem, 0)

    aliases.update({n_in + a: n_out + b for a, b in comm.aliases.items()})
    res = pl.pallas_call(
        wrapped, grid=grid, in_specs=list(in_specs) + [HBM_SPEC] * nci, out_specs=list(out_specs) + [HBM_SPEC] * nco,
        out_shape=list(out_shape) + list(comm.outs),
        scratch_shapes=list(scratch_shapes) + [pltpu.SemaphoreType.DMA((comm.n_sem,)), pltpu.SemaphoreType.DMA((comm.n_sem,))],
        input_output_aliases=aliases, name=name, compiler_params=params)(*operands, *comm.ins)
    return res[:n_out], res[n_out:]


def _ew(name, fn, ins, outs, grid, sp=None):
    n_in = len(ins)

    def body(*refs):
        if sp is not None:
            refs = refs[1:]
        vals = [r[...] for r in refs[:n_in]]
        res = fn(*vals)
        for o_ref, v in zip(refs[n_in:], res):
            o_ref[...] = v.astype(o_ref.dtype)

    in_specs = [s for _, s in ins]
    out_specs = [s for _, s in outs]
    out_shape = [s for s, _ in outs]
    sem = ("arbitrary",) * len(grid)
    if sp is None:
        return pl.pallas_call(body, grid=grid, in_specs=in_specs, out_specs=out_specs, out_shape=out_shape,
                              name=name, compiler_params=_params(sem))(*[a for a, _ in ins])
    gs = pltpu.PrefetchScalarGridSpec(num_scalar_prefetch=1, grid=grid, in_specs=in_specs, out_specs=out_specs)
    return pl.pallas_call(body, grid_spec=gs, out_shape=out_shape, name=name,
                          compiler_params=_params(sem))(sp, *[a for a, _ in ins])


def _matmul(name, pairs, dims, grid, out_shape, out_spec, acc_shape, res=None, comm=None):
    n = len(pairs)
    nk = grid[2]

    def body(*refs):
        ab = refs[:2 * n]
        pos = 2 * n
        res_ref = None
        if res is not None:
            res_ref = refs[pos]
            pos += 1
        o_ref = refs[pos]
        acc_ref = refs[pos + 1] if nk > 1 else None

        def partial():
            t = None
            for p in range(n):
                d = _dot(ab[2 * p][...], ab[2 * p + 1][...], dims)
                t = d if t is None else t + d
            return t

        def finish(t):
            if res_ref is not None:
                t = t + res_ref[...]
            o_ref[...] = t.astype(o_ref.dtype)

        if nk == 1:
            finish(partial())
        else:
            k = pl.program_id(2)

            @pl.when(k == 0)
            def _():
                acc_ref[...] = partial()

            @pl.when(k > 0)
            def _():
                acc_ref[...] += partial()

            @pl.when(k == nk - 1)
            def _():
                finish(acc_ref[...])

    operands, in_specs = [], []
    for a, a_spec, b, b_spec in pairs:
        operands += [a, b]
        in_specs += [a_spec, b_spec]
    if res is not None:
        operands.append(res[0])
        in_specs.append(res[1])
    scratch = [pltpu.VMEM(acc_shape, F32)] if nk > 1 else []
    res = _pcall(body, name=name, grid=grid, in_specs=in_specs, out_specs=[out_spec], out_shape=[out_shape],
                 operands=operands, scratch_shapes=scratch, comm=comm)
    return res[0] if comm is None else (res[0][0], res[1])


def _mm_nn_stacked(name, a, b_st, out_dtype, tm, comm=None):
    M, K = a.shape
    J, _, Nj = b_st.shape
    tm = _tile(M, tm)
    return _matmul(
        name, [(a, pl.BlockSpec((tm, K), lambda j, i, k: (i, 0)), b_st, pl.BlockSpec((None, K, Nj), lambda j, i, k: (j, 0, 0)))],
        NN, (J, M // tm, 1), jax.ShapeDtypeStruct((M, J * Nj), out_dtype), pl.BlockSpec((tm, Nj), lambda j, i, k: (i, j)), None,
        comm=comm)


def _mm_nt_stacked(name, a, b_st, tm, res=None, comm=None):
    M = a.shape[0]
    J, N, Nj = b_st.shape
    tm = _tile(M, tm)

    def body(a_ref, b_ref, *rest):
        o_ref = rest[-1]
        t = None if res is None else rest[0][...]
        for s in range(J):
            d = _dot(a_ref[:, s * Nj:(s + 1) * Nj], b_ref[s], NT)
            t = d if t is None else t + d
        o_ref[...] = t

    row = pl.BlockSpec((tm, N), lambda i: (i, 0))
    out = _pcall(body, name=name, grid=(M // tm,),
                 in_specs=[pl.BlockSpec((tm, J * Nj), lambda i: (i, 0)),
                           pl.BlockSpec((J, N, Nj), lambda i: (0, 0, 0), pipeline_mode=pl.Buffered(1))] + [row] * (res is not None),
                 out_specs=[row], out_shape=[jax.ShapeDtypeStruct((M, N), F32)],
                 operands=[a, b_st] + [res] * (res is not None), comm=comm)
    return out[0] if comm is None else (out[0][0], out[1])


MXU_COLUMNS = 256


def _col_blocks(n):
    return [slice(s, min(s + MXU_COLUMNS, n)) for s in range(0, n, MXU_COLUMNS)]


def _ffn_gate_up(u2, wg_st, wu_st, tm, comm=None):
    S, D = u2.shape
    J, _, Nj = wg_st.shape
    tm = _tile(S, tm)

    def body(a_ref, wg_ref, wu_ref, dg_ref, du_ref, ff_ref):
        a = a_ref[...]
        blocks = _col_blocks(Nj)
        ahead = (_dot(a, wg_ref[:, blocks[0]]), _dot(a, wu_ref[:, blocks[0]]))
        for j, cols in enumerate(blocks):
            g, u = ahead
            if j + 1 < len(blocks):
                ahead = (_dot(a, wg_ref[:, blocks[j + 1]]), _dot(a, wu_ref[:, blocks[j + 1]]))
            sg = jax.nn.sigmoid(g)
            silu = g * sg
            dg_ref[:, cols] = (u * (sg * (1.0 + g * (1.0 - sg)))).astype(BF16)
            du_ref[:, cols] = silu.astype(BF16)
            ff_ref[:, cols] = (silu * u).astype(BF16)

    w_spec = pl.BlockSpec((None, D, Nj), lambda j, i: (j, 0, 0))
    o_spec = pl.BlockSpec((tm, Nj), lambda j, i: (i, j))
    o = jax.ShapeDtypeStruct((S, J * Nj), BF16)
    return _pcall(body, name="ffn_gate_up", grid=(J, S // tm),
                  in_specs=[pl.BlockSpec((tm, D), lambda j, i: (i, 0)), w_spec, w_spec],
                  out_specs=[o_spec, o_spec, o_spec], out_shape=[o, o, o], operands=[u2, wg_st, wu_st], comm=comm)


def _ffn_gate_up_bwd(dh2b, wd, dg_fac, du_fac, tm, tn, comm=None):
    S, D = dh2b.shape
    F = wd.shape[0]
    tm, tn = _tile(S, tm), _tile(F, tn, LANES)

    def body(a_ref, wd_ref, dg_ref, du_ref, dgt_ref, dup_ref):
        a = a_ref[...]
        for cols in _col_blocks(tn):
            d = _dot(a, wd_ref[cols, :], NT)
            dgt_ref[:, cols] = (d * dg_ref[:, cols].astype(F32)).astype(BF16)
            dup_ref[:, cols] = (d * du_ref[:, cols].astype(F32)).astype(BF16)

    blk = pl.BlockSpec((tm, tn), lambda j, i: (i, j))
    o = jax.ShapeDtypeStruct((S, F), BF16)
    return _pcall(body, name="ffn_gate_up_bwd", grid=(F // tn, S // tm),
                  in_specs=[pl.BlockSpec((tm, D), lambda j, i: (i, 0)), pl.BlockSpec((tn, D), lambda j, i: (j, 0)), blk, blk],
                  out_specs=[blk, blk], out_shape=[o, o], operands=[dh2b, wd, dg_fac, du_fac], comm=comm)


def _mm_tn(name, a, b, tmo, tn, tk, stacked_cols=None, comm=None):
    S, Mo = a.shape
    N = b.shape[1]
    tmo, tk = _tile(Mo, tmo, LANES), _tile(S, tk)
    if stacked_cols is None:
        tn = _tile(N, tn, LANES)
        out_shape = jax.ShapeDtypeStruct((Mo, N), F32)
        out_spec = pl.BlockSpec((tmo, tn), lambda i, j, k: (i, j))
    else:
        tn = stacked_cols
        out_shape = jax.ShapeDtypeStruct((N // tn, Mo, tn), F32)
        out_spec = pl.BlockSpec((None, tmo, tn), lambda i, j, k: (j, i, 0))
    return _matmul(
        name, [(a, pl.BlockSpec((tk, tmo), lambda i, j, k: (k, i)), b, pl.BlockSpec((tk, tn), lambda i, j, k: (k, j)))],
        TN, (Mo // tmo, N // tn, S // tk), out_shape, out_spec, (tmo, tn), comm=comm)


def _rms_fwd(name, x, w, tm, comm=None):
    S, D = x.shape
    tm = _tile(S, tm)

    def body(x_ref, w_ref, o_ref):
        xv = x_ref[...]
        r = lax.rsqrt(jnp.mean(xv * xv, axis=-1, keepdims=True) + EPS)
        o_ref[...] = ((xv * r) * w_ref[...]).astype(BF16)

    row = pl.BlockSpec((tm, D), lambda i: (i, 0))
    return _pcall(body, name=name, grid=(S // tm,), in_specs=[row, pl.BlockSpec((1, D), lambda i: (0, 0))], out_specs=[row],
                  out_shape=[jax.ShapeDtypeStruct((S, D), BF16)], operands=[x, w], comm=comm)


def _rms_bwd(name, x, w, dy, dres, tm, comm=None):
    S, D = x.shape
    tm = _tile(S, tm)

    def body(x_ref, w_ref, dy_ref, dres_ref, dx_ref, dw_ref):
        i = pl.program_id(0)

        @pl.when(i == 0)
        def _():
            dw_ref[...] = jnp.zeros_like(dw_ref)

        xv = x_ref[...]
        r = lax.rsqrt(jnp.mean(xv * xv, axis=-1, keepdims=True) + EPS)
        nv = xv * r
        dyv = dy_ref[...]
        dn = dyv * w_ref[...]
        dw_ref[...] += jnp.sum(dyv * nv, axis=0, keepdims=True)
        dx = dres_ref[...] + r * (dn - nv * jnp.mean(dn * nv, axis=-1, keepdims=True))
        dx_ref[...] = dx

    row = pl.BlockSpec((tm, D), lambda i: (i, 0))
    vec = pl.BlockSpec((1, D), lambda i: (0, 0))
    return _pcall(body, name=name, grid=(S // tm,), in_specs=[row, vec, row, row], out_specs=[row, vec],
                  out_shape=[jax.ShapeDtypeStruct((S, D), F32), jax.ShapeDtypeStruct((1, D), F32)],
                  operands=[x, w, dy, dres], comm=comm)


def _rms_bwd_dy(h1, w, du2, dh2, w_out, tm, comm=None):
    S, D = h1.shape
    d_mix = w_out.shape[0]
    tm = _tile(S, tm)

    def body(x_ref, w_ref, dy_ref, dres_ref, wo_ref, dx_ref, dxb_ref, out_ref, dw_ref):
        i = pl.program_id(0)

        @pl.when(i == 0)
        def _():
            dw_ref[...] = jnp.zeros_like(dw_ref)

        xv = x_ref[...]
        r = lax.rsqrt(jnp.mean(xv * xv, axis=-1, keepdims=True) + EPS)
        nv = xv * r
        dyv = dy_ref[...]
        dn = dyv * w_ref[...]
        dw_ref[...] += jnp.sum(dyv * nv, axis=0, keepdims=True)
        dx = dres_ref[...] + r * (dn - nv * jnp.mean(dn * nv, axis=-1, keepdims=True))
        dx_ref[...] = dx
        dxb = dx.astype(BF16)
        dxb_ref[...] = dxb
        out_ref[...] = _dot(dxb, wo_ref[...], NT)

    row = pl.BlockSpec((tm, D), lambda i: (i, 0))
    vec = pl.BlockSpec((1, D), lambda i: (0, 0))
    return _pcall(
        body, name="rms2_bwd_dy", grid=(S // tm,),
        in_specs=[row, vec, row, row, pl.BlockSpec((d_mix, D), lambda i: (0, 0), pipeline_mode=pl.Buffered(1))],
        out_specs=[row, row, pl.BlockSpec((tm, d_mix), lambda i: (i, 0)), vec],
        out_shape=[jax.ShapeDtypeStruct((S, D), F32), jax.ShapeDtypeStruct((S, D), BF16),
                   jax.ShapeDtypeStruct((S, d_mix), F32), jax.ShapeDtypeStruct((1, D), F32)],
        operands=[h1, w, du2, dh2, w_out], comm=comm)


def _out_proj_rms(y, w_out, x, ln_w, tm, comm=None):
    S, K = y.shape
    D = w_out.shape[1]
    tm = _tile(S, tm)

    def body(a_ref, w_ref, x_ref, lw_ref, h_ref, u_ref):
        hv = _dot(a_ref[...], w_ref[...]) + x_ref[...]
        h_ref[...] = hv
        r = lax.rsqrt(jnp.mean(hv * hv, axis=-1, keepdims=True) + EPS)
        u_ref[...] = ((hv * r) * lw_ref[...]).astype(BF16)

    row = pl.BlockSpec((tm, D), lambda i: (i, 0))
    return _pcall(
        body, name="out_proj", grid=(S // tm,),
        in_specs=[pl.BlockSpec((tm, K), lambda i: (i, 0)),
                  pl.BlockSpec((K, D), lambda i: (0, 0), pipeline_mode=pl.Buffered(1)), row,
                  pl.BlockSpec((1, D), lambda i: (0, 0))],
        out_specs=[row, row], out_shape=[jax.ShapeDtypeStruct((S, D), F32), jax.ShapeDtypeStruct((S, D), BF16)],
        operands=[y, w_out, x, ln_w], comm=comm)


def _ffn_down_loss(ff, wd, h1, tgt, fw, tm):
    S, K = ff.shape
    D = wd.shape[1]
    tm = _tile(S, tm)

    def body(a_ref, wd_ref, h1_ref, t_ref, w_ref, dh_ref, dhb_ref, dw_ref, loss_ref):
        i = pl.program_id(0)

        @pl.when(i == 0)
        def _():
            dw_ref[...] = jnp.zeros_like(dw_ref)
            loss_ref[...] = jnp.zeros_like(loss_ref)

        hv = _dot(a_ref[...], wd_ref[...]) + h1_ref[...]
        wv = w_ref[...]
        r = lax.rsqrt(jnp.mean(hv * hv, axis=-1, keepdims=True) + EPS)
        nv = hv * r
        err = nv * wv - t_ref[...]
        row_loss = jnp.mean(err * err, axis=-1, keepdims=True)
        loss_ref[...] += 0.5 * jnp.sum(row_loss, axis=0, keepdims=True)
        dyo = err * (1.0 / D)
        dn = dyo * wv
        dw_ref[...] += jnp.sum(dyo * nv, axis=0, keepdims=True)
        dh = r * (dn - nv * jnp.mean(dn * nv, axis=-1, keepdims=True))
        dh_ref[...] = dh
        dhb_ref[...] = dh.astype(BF16)

    row = pl.BlockSpec((tm, D), lambda i: (i, 0))
    vec = pl.BlockSpec((1, D), lambda i: (0, 0))
    return _pcall(
        body, name="ffn_down_loss", grid=(S // tm,),
        in_specs=[pl.BlockSpec((tm, K), lambda i: (i, 0)),
                  pl.BlockSpec((K, D), lambda i: (0, 0), pipeline_mode=pl.Buffered(1)), row, row, vec],
        out_specs=[row, row, vec, pl.BlockSpec((1, LANES), lambda i: (0, 0))],
        out_shape=[jax.ShapeDtypeStruct((S, D), F32), jax.ShapeDtypeStruct((S, D), BF16),
                   jax.ShapeDtypeStruct((1, D), F32), jax.ShapeDtypeStruct((1, LANES), F32)],
        operands=[ff, wd, h1, tgt, fw])


def _shift_down(x, d, head8):
    r = pltpu.roll(x, d, 0)
    rh = pltpu.roll(head8, d, 0)
    row8 = lax.broadcasted_iota(jnp.int32, head8.shape, 0)
    top = jnp.where(row8 < d, rh, r[0:8])
    return jnp.concatenate([top, r[8:]], axis=0)


def _shift_up(x, d, tail8):
    n = x.shape[0]
    r = pltpu.roll(x, n - d, 0)
    rt = pltpu.roll(tail8, 8 - d, 0)
    row8 = lax.broadcasted_iota(jnp.int32, tail8.shape, 0)
    bot = jnp.where(row8 + d >= 8, rt, r[n - 8:n])
    return jnp.concatenate([r[:n - 8], bot], axis=0)


def _roll_in_groups(x, d):
    n, c = x.shape
    return pltpu.roll(x.reshape(n // 8, 8, c), d, 1).reshape(n, c)


def _log_sigmoid(lam):
    z = jnp.exp(-jnp.abs(lam))
    u = 1.0 + z
    log1p = jnp.where(u == 1.0, z, jnp.log(u) * (z / jnp.where(u == 1.0, 1.0, u - 1.0)))
    return jnp.minimum(lam, 0.0) - log1p


def _neg_expm1(z, exp_z):
    series = -z * (1.0 + z * (0.5 + z * (1.0 / 6.0)))
    return jnp.where(z > -0.02, series, 1.0 - exp_z)


_GELU_C = 0.7978845608028654


def _gelu(x):
    t = jnp.tanh(_GELU_C * (x + 0.044715 * (x * x * x)))
    return x * (0.5 * (1.0 + t)), t


def _gelu_grad(x, t):
    return 0.5 * (1.0 + t) + 0.5 * x * (1.0 - t * t) * (_GELU_C * (1.0 + 3.0 * 0.044715 * (x * x)))


def _lx_shifts(lx, head8):
    return [lx] + [_shift_down(lx, d, head8) for d in (1, 2, 3)]


def _lru_gates(lx, head8, cw, cb, wa_ref, ba, wx_ref, bx, ls):
    nb = wa_ref.shape[0]
    sh = _lx_shifts(lx, head8)
    cx = cb + sh[3] * cw[0:1]
    cx = cx + sh[2] * cw[1:2]
    cx = cx + sh[1] * cw[2:3]
    cx = cx + sh[0] * cw[3:4]
    cxb = cx.astype(BF16)
    ra = jnp.concatenate([_dot(cxb[:, n * HEAD_DIM:(n + 1) * HEAD_DIM], wa_ref[n]) for n in range(nb)], axis=1) + ba
    ia = jnp.concatenate([_dot(cxb[:, n * HEAD_DIM:(n + 1) * HEAD_DIM], wx_ref[n]) for n in range(nb)], axis=1) + bx
    r = jax.nn.sigmoid(ra)
    ig = jax.nn.sigmoid(ia)
    log_a = LRU_C * r * ls
    a = jnp.exp(log_a)
    return cx, r, ig, a, jnp.sqrt(_neg_expm1(2.0 * log_a, a * a))


def _lru_specs(tl, DL):
    nb = DL // HEAD_DIM
    vec = pl.BlockSpec((1, DL), lambda i: (0, 0))
    return [pl.BlockSpec((CONV_W, DL), lambda i: (0, 0)), vec,
            pl.BlockSpec((nb, HEAD_DIM, HEAD_DIM), lambda i: (0, 0, 0)), vec,
            pl.BlockSpec((nb, HEAD_DIM, HEAD_DIM), lambda i: (0, 0, 0)), vec, vec]


def _lru_fwd(proj, cw, cb, wa, ba, wx, bx, lam, tl, d_mix, comm=None):
    S = proj.shape[0]
    DL = cb.shape[1]
    tl = _tile(S, tl)

    def body(lx_ref, lg_ref, cw_ref, cb_ref, wa_ref, ba_ref, wx_ref, bx_ref, lam_ref, h_ref, kept_ref, y_ref,
             prev8, hc, a_s, b_s):
        i = pl.program_id(0)

        @pl.when(i == 0)
        def _():
            prev8[...] = jnp.zeros_like(prev8)
            hc[...] = jnp.zeros_like(hc)

        lx = lx_ref[...]
        ls = _log_sigmoid(lam_ref[...])
        kept = _lru_gates(lx, prev8[...], cw_ref[...], cb_ref[...], wa_ref, ba_ref[...], wx_ref, bx_ref[...], ls)
        for n, val in enumerate(kept):
            kept_ref[:, n * DL:(n + 1) * DL] = val
        cx, _, ig, a, mult = kept
        b = mult * (ig * cx)
        row = lax.broadcasted_iota(jnp.int32, a.shape, 0) & 7
        for d in (1, 2, 4):
            a_sh = _roll_in_groups(a, d)
            b_sh = _roll_in_groups(b, d)
            m = row >= d
            b = jnp.where(m, a * b_sh + b, b)
            a = jnp.where(m, a * a_sh, a)
        a_s[...] = a
        b_s[...] = b

        def step(g, hprev):
            sl = pl.ds(pl.multiple_of(g * 8, 8), 8)
            hh = a_s[sl, :] * hprev + b_s[sl, :]
            h_ref[sl, :] = hh
            return hh[7:8, :]

        hc[0:1, :] = lax.fori_loop(0, tl // 8, step, hc[0:1, :])
        prev8[...] = lx[tl - 8:tl]
        g, _ = _gelu(lg_ref[...])
        y_ref[...] = (h_ref[...] * g).astype(BF16)

    return _pcall(
        body, name="lru_fwd", grid=(S // tl,),
        in_specs=[pl.BlockSpec((tl, DL), lambda i: (i, 0)), pl.BlockSpec((tl, DL), lambda i: (i, 1))] + _lru_specs(tl, DL),
        out_specs=[pl.BlockSpec((tl, DL), lambda i: (i, 0)), pl.BlockSpec((tl, 5 * DL), lambda i: (i, 0)),
                   pl.BlockSpec((tl, DL), lambda i: (i, 0))],
        out_shape=[jax.ShapeDtypeStruct((S, DL), F32), jax.ShapeDtypeStruct((S, 5 * DL), F32),
                   jax.ShapeDtypeStruct((S, d_mix), BF16)],
        scratch_shapes=[pltpu.VMEM((8, DL), F32), pltpu.VMEM((8, DL), F32), pltpu.VMEM((tl, DL), F32), pltpu.VMEM((tl, DL), F32)],
        operands=[proj, proj, cw, cb, wa, ba, wx, bx, lam], comm=comm)


def _lru_bwd(proj, h, kept, dy, cw, wa, wx, lam, tl, comm=None):
    S = proj.shape[0]
    DL = lam.shape[1]
    nb = DL // HEAD_DIM
    tl = _tile(S, tl)
    nt = S // tl
    ng = tl // 8
    t8 = tl // 8

    def body(lx_ref, lxp_ref, lg_ref, h_ref, hp_ref, kept_ref, dy_ref, cw_ref, wa_ref, wx_ref, lam_ref,
             dlxg_ref, dcw_ref, dcb_ref, dwa_ref, dba_ref, dwx_ref, dbx_ref, dlam_ref,
             a_next, g_carry, dcx_next, an_s, dh_s, g_s):
        i = pl.program_id(0)

        @pl.when(i == 0)
        def _():
            for ref in (dcw_ref, dcb_ref, dwa_ref, dba_ref, dwx_ref, dbx_ref, dlam_ref, a_next, g_carry, dcx_next):
                ref[...] = jnp.zeros_like(ref)

        first = i == nt - 1
        hv = h_ref[...]
        lg = lg_ref[...]
        dyv = dy_ref[...]
        hhead8 = jnp.where(first, 0.0, hp_ref[...])
        lamv = lam_ref[...]
        ls = _log_sigmoid(lamv)
        cwv = cw_ref[...]
        sh = _lx_shifts(lx_ref[...], jnp.where(first, 0.0, lxp_ref[...]))
        cx, r, ig, a, mult = (kept_ref[:, n * DL:(n + 1) * DL] for n in range(5))
        cxb = cx.astype(BF16)
        hprev = _shift_down(hv, 1, hhead8)
        g, t = _gelu(lg)
        dlg = dyv * hv * _gelu_grad(lg, t)
        dh = dyv * g
        an = _shift_up(a, 1, a_next[...])
        row = lax.broadcasted_iota(jnp.int32, a.shape, 0) & 7
        for d in (1, 2, 4):
            an_sh = _roll_in_groups(an, 8 - d)
            dh_sh = _roll_in_groups(dh, 8 - d)
            m = row + d < 8
            dh = jnp.where(m, an * dh_sh + dh, dh)
            an = jnp.where(m, an * an_sh, an)
        an_s[...] = an
        dh_s[...] = dh

        def step(k, gc):
            sl = pl.ds(pl.multiple_of((ng - 1 - k) * 8, 8), 8)
            gg = an_s[sl, :] * gc + dh_s[sl, :]
            g_s[sl, :] = gg
            return gg[0:1, :]

        g_carry[0:1, :] = lax.fori_loop(0, ng, step, g_carry[0:1, :])
        a_next[...] = a[0:8]
        G = g_s[...]
        da = G * hprev
        icx = ig * cx
        dmult = G * icx
        dicx = G * mult
        di = dicx * cx
        dcx = dicx * ig
        dlog = da * a - dmult * ((a * a) * lax.rsqrt(mult * mult))
        dr = dlog * (LRU_C * ls)
        dlam_ref[...] += jnp.sum(dlog * (LRU_C * r), axis=0, keepdims=True)
        dra = dr * r * (1.0 - r)
        dia = di * ig * (1.0 - ig)
        dba_ref[...] += jnp.sum(dra, axis=0, keepdims=True)
        dbx_ref[...] += jnp.sum(dia, axis=0, keepdims=True)
        drab = dra.astype(BF16)
        diab = dia.astype(BF16)
        back = []
        for n in range(nb):
            cs = slice(n * HEAD_DIM, (n + 1) * HEAD_DIM)
            dwa_ref[n] += _dot(cxb[:, cs], drab[:, cs], TN)
            dwx_ref[n] += _dot(cxb[:, cs], diab[:, cs], TN)
            back.append(_dot(drab[:, cs], wa_ref[n], NT) + _dot(diab[:, cs], wx_ref[n], NT))
        dcx = dcx + jnp.concatenate(back, axis=1)
        dcb_ref[...] += jnp.sum(dcx, axis=0, keepdims=True)
        for tap in range(CONV_W):
            dcw_ref[tap:tap + 1, :] += jnp.sum(dcx * sh[CONV_W - 1 - tap], axis=0, keepdims=True)
        tail = dcx_next[...]
        dlx = dcx * cwv[3:4]
        for d in (1, 2, 3):
            dlx = dlx + _shift_up(dcx, d, tail) * cwv[3 - d:4 - d]
        dcx_next[...] = dcx[0:8]
        dlxg_ref[:, 0:DL] = dlx.astype(BF16)
        dlxg_ref[:, DL:2 * DL] = dlg.astype(BF16)

        @pl.when(i == nt - 1)
        def _():
            dlam_ref[...] = dlam_ref[...] * (1.0 - jax.nn.sigmoid(lamv))

    rev = lambda i: nt - 1 - i
    prev8_map = lambda i: (jnp.maximum((nt - 1 - i) * t8 - 1, 0), 0)
    vec = pl.BlockSpec((1, DL), lambda i: (0, 0))
    mat = pl.BlockSpec((nb, HEAD_DIM, HEAD_DIM), lambda i: (0, 0, 0))
    return _pcall(
        body, name="lru_bwd", grid=(nt,), operands=[proj, proj, proj, h, h, kept, dy, cw, wa, wx, lam], comm=comm,
        in_specs=[pl.BlockSpec((tl, DL), lambda i: (rev(i), 0)), pl.BlockSpec((8, DL), prev8_map),
                  pl.BlockSpec((tl, DL), lambda i: (rev(i), 1)),
                  pl.BlockSpec((tl, DL), lambda i: (rev(i), 0)), pl.BlockSpec((8, DL), prev8_map),
                  pl.BlockSpec((tl, 5 * DL), lambda i: (rev(i), 0)),
                  pl.BlockSpec((tl, DL), lambda i: (rev(i), 0)), pl.BlockSpec((CONV_W, DL), lambda i: (0, 0)), mat, mat, vec],
        out_specs=[pl.BlockSpec((tl, 2 * DL), lambda i: (rev(i), 0)), pl.BlockSpec((CONV_W, DL), lambda i: (0, 0)), vec,
                   mat, vec, mat, vec, vec],
        out_shape=[jax.ShapeDtypeStruct(proj.shape, BF16), jax.ShapeDtypeStruct((CONV_W, DL), F32),
                   jax.ShapeDtypeStruct((1, DL), F32), jax.ShapeDtypeStruct((nb, HEAD_DIM, HEAD_DIM), F32),
                   jax.ShapeDtypeStruct((1, DL), F32), jax.ShapeDtypeStruct((nb, HEAD_DIM, HEAD_DIM), F32),
                   jax.ShapeDtypeStruct((1, DL), F32), jax.ShapeDtypeStruct((1, DL), F32)],
        scratch_shapes=[pltpu.VMEM((8, DL), F32), pltpu.VMEM((8, DL), F32), pltpu.VMEM((8, DL), F32),
                        pltpu.VMEM((tl, DL), F32), pltpu.VMEM((tl, DL), F32), pltpu.VMEM((tl, DL), F32)])


def _ret_tables(S, H):
    pos = jnp.arange(S, dtype=F32)
    inv_freq = ROPE_BASE ** (-jnp.arange(0, HEAD_DIM, 2, dtype=F32) / HEAD_DIM)
    ang = pos[:, None] * inv_freq[None, :]
    cos, sin = jnp.cos(ang), jnp.sin(ang)
    cosf = jnp.concatenate([cos, cos], axis=1)
    sins = jnp.concatenate([-sin, sin], axis=1)
    log_gamma = jnp.log1p(-jnp.exp2(-5.0 - jnp.arange(H, dtype=F32)))
    idx = jnp.arange(CHUNK)
    diff = idx[:, None] - idx[None, :]
    causal = diff >= 0
    decay = jnp.where(causal[None], jnp.exp(log_gamma[:, None, None] * jnp.where(causal, diff, 0)[None].astype(F32)), 0.0)
    zeta = jnp.exp(log_gamma[:, None] * (CHUNK - 1 - idx).astype(F32)[None, :])
    xi = jnp.exp(log_gamma[:, None] * (idx + 1).astype(F32)[None, :])
    gc = jnp.exp(log_gamma * CHUNK)
    lanes = (H, CHUNK, HEAD_DIM)
    return (cosf, sins, decay, jnp.broadcast_to(zeta[:, :, None], lanes), jnp.broadcast_to(xi[:, :, None], lanes),
            jnp.broadcast_to(gc[:, None, None], lanes))


def _rope(t, cos, sin_signed):
    return t * cos + pltpu.roll(t, HEAD_DIM // 2, 1) * sin_signed


def _rope_t(d, cos, sin_signed):
    return d * cos + pltpu.roll(d * sin_signed, HEAD_DIM // 2, 1)


def _ret_const_specs(H, DR):
    full = pl.BlockSpec((H, CHUNK, HEAD_DIM), lambda *_: (0, 0, 0))
    return [full, full, full, full, pl.BlockSpec((1, DR), lambda *_: (0, 0))]


def _ret_fwd(proj, y, tables, gnw, tb, comm=None):
    S = proj.shape[0]
    DR = gnw.shape[1]
    H = DR // HEAD_DIM
    tb = _tile(S, tb, CHUNK)
    nc = tb // CHUNK
    cosf, sins, dm, zeta, xi, gc = tables
    scale = HEAD_DIM ** -0.5

    def body(qk_ref, vg_ref, cos_ref, sin_ref, dm_ref, zeta_ref, xi_ref, gc_ref, gnw_ref, y_in, y_ref, rprev_ref, r_s):
        del y_in
        i = pl.program_id(0)

        @pl.when(i == 0)
        def _():
            r_s[...] = jnp.zeros_like(r_s)

        def chunk(c, carry):
            rows = pl.ds(pl.multiple_of(c * CHUNK, CHUNK), CHUNK)
            cos = cos_ref[rows, :]
            sin = sin_ref[rows, :]
            heads = range(H)
            c0 = [slice(h * HEAD_DIM, (h + 1) * HEAD_DIM) for h in heads]
            c1 = [slice(DR + h * HEAD_DIM, DR + (h + 1) * HEAD_DIM) for h in heads]
            qh = [_rope(qk_ref[rows, c0[h]], cos, sin) for h in heads]
            kh = [_rope(qk_ref[rows, c1[h]], cos, sin) * scale for h in heads]
            vb = [vg_ref[rows, c0[h]].astype(BF16) for h in heads]
            rp = [r_s[h] for h in heads]
            rpb = [rp[h].astype(BF16) for h in heads]
            s = [_dot(qh[h].astype(BF16), kh[h].astype(BF16), NT) for h in heads]
            kv = [_dot((kh[h] * zeta_ref[h]).astype(BF16), vb[h], TN) for h in heads]
            cross = [_dot((qh[h] * xi_ref[h]).astype(BF16), rpb[h]) for h in heads]
            o = [_dot((s[h] * dm_ref[h]).astype(BF16), vb[h]) + cross[h] for h in heads]
            for h in heads:
                rprev_ref[c, h] = rpb[h]
                r_s[h] = rp[h] * gc_ref[h] + kv[h]
                mu = jnp.mean(o[h], axis=-1, keepdims=True)
                oc = o[h] - mu
                var = jnp.mean(oc * oc, axis=-1, keepdims=True)
                on = oc * lax.rsqrt(var + EPS) * gnw_ref[:, c0[h]]
                gate = vg_ref[rows, c1[h]]
                y_ref[rows, c0[h]] = (gate * jax.nn.sigmoid(gate) * on).astype(BF16)
            return carry

        lax.fori_loop(0, nc, chunk, 0)

    return _pcall(
        body, name="ret_fwd", grid=(S // tb,),
        in_specs=[pl.BlockSpec((tb, 2 * DR), lambda i: (i, 1)), pl.BlockSpec((tb, 2 * DR), lambda i: (i, 2)),
                  pl.BlockSpec((tb, HEAD_DIM), lambda i: (i, 0)), pl.BlockSpec((tb, HEAD_DIM), lambda i: (i, 0))]
        + _ret_const_specs(H, DR) + [HBM_SPEC],
        out_specs=[pl.BlockSpec((tb, DR), lambda i: (i, 1)),
                   pl.BlockSpec((nc, H, CHUNK, HEAD_DIM), lambda i: (i, 0, 0, 0))],
        out_shape=[jax.ShapeDtypeStruct(y.shape, BF16), jax.ShapeDtypeStruct((S // CHUNK, H, CHUNK, HEAD_DIM), BF16)],
        scratch_shapes=[pltpu.VMEM((H, CHUNK, HEAD_DIM), F32)], aliases={9: 0},
        operands=[proj, proj, cosf, sins, dm, zeta, xi, gc, gnw, y], comm=comm)


def _ret_bwd(proj, rprev, dy, dproj, tables, gnw, tb, comm=None):
    S = proj.shape[0]
    DR = gnw.shape[1]
    H = DR // HEAD_DIM
    tb = _tile(S, tb, CHUNK)
    nc = tb // CHUNK
    nt = S // tb
    cosf, sins, dm, zeta, xi, gc = tables
    scale = HEAD_DIM ** -0.5

    def body(qk_ref, vg_ref, cos_ref, sin_ref, dm_ref, zeta_ref, xi_ref, gc_ref, gnw_ref, rprev_ref, dy_ref, dp_in,
             dp_ref, dgn_ref, dr_s, dqk_s, dvg_s, out_sems):
        del dp_in
        i = pl.program_id(0)
        slot = i % 2

        def out_copies(step, sl):
            rows = pl.ds(pl.multiple_of((nt - 1 - step) * tb, tb), tb)
            return (pltpu.make_async_copy(dqk_s.at[sl], dp_ref.at[rows, pl.ds(2 * DR, 2 * DR)], out_sems.at[sl, 0]),
                    pltpu.make_async_copy(dvg_s.at[sl], dp_ref.at[rows, pl.ds(4 * DR, 2 * DR)], out_sems.at[sl, 1]))

        @pl.when(i == 0)
        def _():
            dr_s[...] = jnp.zeros_like(dr_s)
            dgn_ref[...] = jnp.zeros_like(dgn_ref)

        @pl.when(i >= 2)
        def _():
            for cp in out_copies(i - 2, slot):
                cp.wait()

        def chunk(cc, carry):
            c = nc - 1 - cc
            rows = pl.ds(pl.multiple_of(c * CHUNK, CHUNK), CHUNK)
            cos = cos_ref[rows, :]
            sin = sin_ref[rows, :]
            heads = range(H)
            c0 = [slice(h * HEAD_DIM, (h + 1) * HEAD_DIM) for h in heads]
            c1 = [slice(DR + h * HEAD_DIM, DR + (h + 1) * HEAD_DIM) for h in heads]
            qh = [_rope(qk_ref[rows, c0[h]], cos, sin) for h in heads]
            kh = [_rope(qk_ref[rows, c1[h]], cos, sin) * scale for h in heads]
            qb = [t.astype(BF16) for t in qh]
            kb = [t.astype(BF16) for t in kh]
            vb = [vg_ref[rows, c0[h]].astype(BF16) for h in heads]
            rpb = [rprev_ref[c, h] for h in heads]
            qx = [(qh[h] * xi_ref[h]).astype(BF16) for h in heads]
            kz = [(kh[h] * zeta_ref[h]).astype(BF16) for h in heads]
            drh = [dr_s[h] for h in heads]
            drb = [t.astype(BF16) for t in drh]
            s = [_dot(qb[h], kb[h], NT) for h in heads]
            cross = [_dot(qx[h], rpb[h]) for h in heads]
            dv_state = [_dot(kz[h], drb[h]) for h in heads]
            dk_state = [_dot(vb[h], drb[h], NT) for h in heads]
            sb = [(s[h] * dm_ref[h]).astype(BF16) for h in heads]
            o = [_dot(sb[h], vb[h]) + cross[h] for h in heads]
            dob = []
            for h in heads:
                mu = jnp.mean(o[h], axis=-1, keepdims=True)
                oc = o[h] - mu
                rstd = lax.rsqrt(jnp.mean(oc * oc, axis=-1, keepdims=True) + EPS)
                ohat = oc * rstd
                gw = gnw_ref[:, c0[h]]
                gate = vg_ref[rows, c1[h]]
                sg = jax.nn.sigmoid(gate)
                dyv = dy_ref[rows, c0[h]]
                dvg_s[slot, rows, c1[h]] = (dyv * (ohat * gw) * (sg * (1.0 + gate * (1.0 - sg)))).astype(BF16)
                don = dyv * (gate * sg)
                dgn_ref[:, c0[h]] += jnp.sum(don * ohat, axis=0, keepdims=True)
                dohat = don * gw
                do = rstd * (dohat - jnp.mean(dohat, axis=-1, keepdims=True)
                             - ohat * jnp.mean(dohat * ohat, axis=-1, keepdims=True))
                dob.append(do.astype(BF16))
            ds = [_dot(dob[h], vb[h], NT) for h in heads]
            dq_state = [_dot(dob[h], rpb[h], NT) for h in heads]
            dv = [_dot(sb[h], dob[h], TN) + dv_state[h] for h in heads]
            dr_new = [_dot(qx[h], dob[h], TN) for h in heads]
            dsb = [(ds[h] * dm_ref[h]).astype(BF16) for h in heads]
            dqh = [_dot(dsb[h], kb[h]) + dq_state[h] * xi_ref[h] for h in heads]
            dkh = [_dot(dsb[h], qb[h], TN) + dk_state[h] * zeta_ref[h] for h in heads]
            for h in heads:
                dr_s[h] = drh[h] * gc_ref[h] + dr_new[h]
                dqk_s[slot, rows, c0[h]] = _rope_t(dqh[h], cos, sin).astype(BF16)
                dqk_s[slot, rows, c1[h]] = _rope_t(dkh[h] * scale, cos, sin).astype(BF16)
                dvg_s[slot, rows, c0[h]] = dv[h].astype(BF16)
            return carry

        lax.fori_loop(0, nc, chunk, 0)
        for cp in out_copies(i, slot):
            cp.start()

        @pl.when(i == nt - 1)
        def _():
            if nt >= 2:
                for cp in out_copies(i - 1, 1 - slot):
                    cp.wait()
            for cp in out_copies(i, slot):
                cp.wait()

    rev = lambda i: nt - 1 - i
    return _pcall(
        body, name="ret_bwd", grid=(nt,), aliases={11: 0}, comm=comm,
        operands=[proj, proj, cosf, sins, dm, zeta, xi, gc, gnw, rprev, dy, dproj],
        in_specs=[pl.BlockSpec((tb, 2 * DR), lambda i: (rev(i), 1)), pl.BlockSpec((tb, 2 * DR), lambda i: (rev(i), 2)),
                  pl.BlockSpec((tb, HEAD_DIM), lambda i: (rev(i), 0)), pl.BlockSpec((tb, HEAD_DIM), lambda i: (rev(i), 0))]
        + _ret_const_specs(H, DR)
        + [pl.BlockSpec((nc, H, CHUNK, HEAD_DIM), lambda i: (rev(i), 0, 0, 0)),
           pl.BlockSpec((tb, DR), lambda i: (rev(i), 1)), HBM_SPEC],
        out_specs=[HBM_SPEC, pl.BlockSpec((1, DR), lambda i: (0, 0))],
        out_shape=[jax.ShapeDtypeStruct(dproj.shape, BF16), jax.ShapeDtypeStruct((1, DR), F32)],
        scratch_shapes=[pltpu.VMEM((H, CHUNK, HEAD_DIM), F32), pltpu.VMEM((2, tb, 2 * DR), BF16),
                        pltpu.VMEM((2, tb, 2 * DR), BF16), pltpu.SemaphoreType.DMA((2, 2))])


def _place():
    x, y, c = lax.axis_index("x"), lax.axis_index("y"), lax.axis_index("c")
    chips = [(1 - x, y), (x, 1 - y), (1 - x, 1 - y)]
    return x, y, c, chips


def _own_slab(name, shard, place):
    R, C = shard.shape
    tr = _row_tile(R, C)
    return _ew("cast_" + name, lambda a: (a,), [(shard, pl.BlockSpec((tr, C), lambda i, p: (i, 0)))],
               [(jax.ShapeDtypeStruct((4, R, C), BF16), pl.BlockSpec((None, tr, C), lambda i, p: (p[1], i, 0)))],
               (R // tr,), sp=place)[0]


class _remote:
    def __init__(self, src, dst, ssem, rsem, k, to):
        self.args = dict(src_ref=src, dst_ref=dst, send_sem=ssem.at[k], recv_sem=rsem.at[k], device_id=to,
                         device_id_type=MESH)

    def start(self):
        pltpu.make_async_remote_copy(**self.args).start()

    def wait_send(self):
        pltpu.make_async_remote_copy(**self.args).wait_send()

    def wait_recv(self):
        pltpu.make_async_remote_copy(**self.args).wait_recv()


def _task_fns(copies):
    def start(cins, couts, ssem, rsem, base):
        for cp in copies(cins, couts, ssem, rsem, base)[0]:
            cp.start()

    def finish(cins, couts, ssem, rsem, base):
        sends, recvs = copies(cins, couts, ssem, rsem, base)
        for cp in sends:
            cp.wait_send()
        for cp in recvs:
            cp.wait_recv()

    return start, finish


NEIGHBOURS, DIAGONAL = (0, 1), (2,)


def _gather_ici(st, which=NEIGHBOURS + DIAGONAL):
    r2 = st.shape[1] // 2

    def copies(cins, couts, ssem, rsem, base):
        x, y, c, chips = _place()
        out = couts[0]
        mine = out.at[2 * x + y, pl.ds(c * r2, r2), :]
        sends, recvs = [], []
        for k, j in enumerate(which):
            cx, cy = chips[j]
            got = out.at[2 * cx + cy, pl.ds(c * r2, r2), :]
            sends.append(_remote(mine, mine, ssem, rsem, base + k, (cx, cy, c)))
            recvs.append(_remote(got, got, ssem, rsem, base + k, (x, y, c)))
        return sends, recvs

    start, finish = _task_fns(copies)
    return _Comm([st], [jax.ShapeDtypeStruct(st.shape, st.dtype)], {0: 0}, len(which), start, finish)


def _gather_d2d(st):
    r2 = st.shape[1] // 2

    def copies(cins, couts, ssem, rsem, base):
        x, y, c, chips = _place()
        out = couts[0]
        sends, recvs = [], []
        for j, (cx, cy) in enumerate(chips):
            have = out.at[2 * cx + cy, pl.ds(c * r2, r2), :]
            want = out.at[2 * cx + cy, pl.ds((1 - c) * r2, r2), :]
            sends.append(_remote(have, have, ssem, rsem, base + j, (x, y, 1 - c)))
            recvs.append(_remote(want, want, ssem, rsem, base + j, (x, y, c)))
        return sends, recvs

    start, finish = _task_fns(copies)
    return _Comm([st], [jax.ShapeDtypeStruct(st.shape, st.dtype)], {0: 0}, 3, start, finish)


def _gather_conv(conv_w):
    def copies(cins, couts, ssem, rsem, base):
        x, y, c, chips = _place()
        src, out = cins[0], couts[0]
        sends = [_remote(src, out.at[2 * x + y], ssem, rsem, base + j, (*chip, c)) for j, chip in enumerate(chips)]
        recvs = [_remote(src, out.at[2 * cx + cy], ssem, rsem, base + j, (x, y, c)) for j, (cx, cy) in enumerate(chips)]
        return sends, recvs

    start, finish = _task_fns(copies)
    return _Comm([conv_w], [jax.ShapeDtypeStruct((4,) + conv_w.shape, conv_w.dtype)], {}, 3, start, finish)


def _pair_exchange(g):
    r2 = g.shape[1] // 2

    def copies(cins, couts, ssem, rsem, base):
        x, y, c, _ = _place()
        cp = _remote(cins[0].at[:, pl.ds((1 - c) * r2, r2), :], couts[0], ssem, rsem, base, (x, y, 1 - c))
        return [cp], [cp]

    start, finish = _task_fns(copies)
    return _Comm([g], [jax.ShapeDtypeStruct((g.shape[0], r2, g.shape[2]), g.dtype)], {}, 1, start, finish)


def _chip_exchange(part):
    def copies(cins, couts, ssem, rsem, base):
        x, y, c, chips = _place()
        cps = [_remote(cins[0].at[2 * cx + cy], couts[0].at[j], ssem, rsem, base + j, (cx, cy, c))
               for j, (cx, cy) in enumerate(chips)]
        return cps, cps

    start, finish = _task_fns(copies)
    return _Comm([part], [jax.ShapeDtypeStruct((3,) + part.shape[1:], part.dtype)], {}, 3, start, finish)


def _pair_share(slot):
    def copies(cins, couts, ssem, rsem, base):
        x, y, c, _ = _place()
        out = couts[0]
        return ([_remote(out.at[c], out.at[c], ssem, rsem, base, (x, y, 1 - c))],
                [_remote(out.at[1 - c], out.at[1 - c], ssem, rsem, base, (x, y, c))])

    start, finish = _task_fns(copies)
    return _Comm([slot], [jax.ShapeDtypeStruct(slot.shape, slot.dtype)], {0: 0}, 1, start, finish)


def _gather_small(sm):
    flips = [(fx, fy, fc) for fx in (0, 1) for fy in (0, 1) for fc in (0, 1)][1:]

    def copies(cins, couts, ssem, rsem, base):
        x, y, c, _ = _place()
        src, out = cins[0], couts[0]
        peers = [(1 - x if fx else x, 1 - y if fy else y, 1 - c if fc else c) for fx, fy, fc in flips]
        sends = [_remote(src, out.at[4 * x + 2 * y + c], ssem, rsem, base + k, peer) for k, peer in enumerate(peers)]
        recvs = [_remote(src, out.at[4 * px + 2 * py + pc], ssem, rsem, base + k, (x, y, c))
                 for k, (px, py, pc) in enumerate(peers)]
        return sends, recvs

    start, finish = _task_fns(copies)
    return _Comm([sm], [jax.ShapeDtypeStruct((8,) + sm.shape, sm.dtype)], {}, 7, start, finish)


def _comm_call(name, tasks):
    task = _merge(tasks)
    nci = len(task.ins)

    def body(*refs):
        cins, couts, (ssem, rsem) = refs[:nci], refs[nci:nci + len(task.outs)], refs[nci + len(task.outs):]
        task.start(cins, couts, ssem, rsem, 0)
        task.finish(cins, couts, ssem, rsem, 0)

    return pl.pallas_call(
        body, in_specs=[HBM_SPEC] * nci, out_specs=[HBM_SPEC] * len(task.outs), out_shape=list(task.outs),
        scratch_shapes=[pltpu.SemaphoreType.DMA((task.n_sem,)), pltpu.SemaphoreType.DMA((task.n_sem,))],
        input_output_aliases=task.aliases, name=name)(*task.ins)


def _adamw(w, g, m, v):
    m = ADAM_B1 * m + (1.0 - ADAM_B1) * g
    v = ADAM_B2 * v + (1.0 - ADAM_B2) * (g * g)
    m_hat = m / (1.0 - ADAM_B1 ** ADAM_STEP)
    v_hat = v / (1.0 - ADAM_B2 ** ADAM_STEP)
    delta = -ADAM_LR * (m_hat / (jnp.sqrt(v_hat) + ADAM_EPS) + ADAM_WD * w)
    return delta, m, v


def _adamw_call(name, w, g, m, v):
    R, C = w.shape
    tr = _row_tile(R, C, 1024 * 1024)
    row = pl.BlockSpec((tr, C), lambda i: (i, 0))
    o = jax.ShapeDtypeStruct((R, C), F32)
    return _ew(name, lambda w_, g_, m_, v_: (*_adamw(w_, g_, m_, v_), g_), [(w, row), (g, row), (m, row), (v, row)],
               [(o, row), (o, row), (o, row), (o, row)], (R // tr,))


def _pair_sum(name, g, ra, place):
    _, R, C = g.shape
    r2 = R // 2
    tr = _row_tile(r2, C)
    nb = r2 // tr
    own = pl.BlockSpec((None, tr, C), lambda j, i, p: (j, p[0] * nb + i, 0))
    blk = pl.BlockSpec((None, tr, C), lambda j, i, p: (j, i, 0))
    return _ew("rs_pair_sum_" + name, lambda a, b: (a + b,), [(g, own), (ra, blk)],
               [(jax.ShapeDtypeStruct((4, r2, C), BF16), blk)], (4, nb), sp=place)[0]


def _chip_sum(name, g, ra, rb, place):
    _, R, C = g.shape
    r2 = R // 2
    tr = _row_tile(r2, C)
    nb = r2 // tr
    own = pl.BlockSpec((None, tr, C), lambda i, p: (p[1], p[0] * nb + i, 0))
    mine = pl.BlockSpec((None, tr, C), lambda i, p: (p[1], i, 0))
    src = [pl.BlockSpec((None, tr, C), functools.partial(lambda i, p, j: (j, i, 0), j=j)) for j in range(3)]
    out = pl.BlockSpec((None, tr, C), lambda i, p: (p[0], i, 0))

    def total(a, b, r0, r1, r2_):
        return ((((a + b) + r0.astype(F32)) + r1.astype(F32)) + r2_.astype(F32),)

    return _ew("rs_chip_sum_" + name, total, [(g, own), (ra, mine), (rb, src[0]), (rb, src[1]), (rb, src[2])],
               [(jax.ShapeDtypeStruct((2, r2, C), F32), out)], (nb,), sp=place)[0]


def _pack(arrays):
    rows, offs, pos = [], [], 0
    for a in arrays:
        flat = a.reshape(-1)
        n = -(-flat.shape[0] // (8 * LANES)) * (8 * LANES)
        if n != flat.shape[0]:
            flat = jnp.pad(flat, (0, n - flat.shape[0]))
        rows.append(flat.reshape(-1, LANES))
        offs.append(pos)
        pos += n // LANES
    return jnp.concatenate(rows, axis=0), offs


def _unpack(packed, offs, shapes):
    out = []
    for off, shp in zip(offs, shapes):
        n = 1
        for s in shp:
            n *= s
        out.append(packed[off:off + -(-n // LANES)].reshape(-1)[:n].reshape(shp))
    return out


def _sum8(gathered):
    _, R, C = gathered.shape
    tr = _row_tile(R, C, 256 * 1024)
    specs = [pl.BlockSpec((None, tr, C), functools.partial(lambda i, d: (d, i, 0), d=d)) for d in range(8)]

    def fn(*parts):
        t = parts[0]
        for p in parts[1:]:
            t = t + p
        return (t,)

    return _ew("small_sum", fn, [(gathered, s) for s in specs],
               [(jax.ShapeDtypeStruct((R, C), F32), pl.BlockSpec((tr, C), lambda i: (i, 0)))], (R // tr,))[0]


BIG = ("w_in", "w_out", "w_ffn_gate", "w_ffn_up", "w_ffn_down")
SMALL = ("ln1_w", "conv_w", "conv_b", "gate_a_w", "gate_a_b", "gate_x_w", "gate_x_b", "lru_lambda", "ret_gn_w", "ln2_w",
         "final_norm_w")
WEIGHTS = ("ln1_w", "w_in", "conv_w", "conv_b", "gate_a_w", "gate_a_b", "gate_x_w", "gate_x_b", "lru_lambda", "ret_gn_w",
           "w_out", "ln2_w", "w_ffn_gate", "w_ffn_up", "w_ffn_down", "final_norm_w")


def kernel(x, ln1_w, w_in, conv_w, conv_b, gate_a_w, gate_a_b, gate_x_w, gate_x_b, lru_lambda, ret_gn_w, w_out, ln2_w, w_ffn_gate, w_ffn_up, w_ffn_down, final_norm_w, loss_target, m_ln1_w, m_w_in, m_conv_w, m_conv_b, m_gate_a_w, m_gate_a_b, m_gate_x_w, m_gate_x_b, m_lru_lambda, m_ret_gn_w, m_w_out, m_ln2_w, m_w_ffn_gate, m_w_ffn_up, m_w_ffn_down, m_final_norm_w, v_ln1_w, v_w_in, v_conv_w, v_conv_b, v_gate_a_w, v_gate_a_b, v_gate_x_w, v_gate_x_b, v_lru_lambda, v_ret_gn_w, v_w_out, v_ln2_w, v_w_ffn_gate, v_w_ffn_up, v_w_ffn_down, v_final_norm_w):
    w = dict(ln1_w=ln1_w, w_in=w_in, conv_w=conv_w, conv_b=conv_b, gate_a_w=gate_a_w, gate_a_b=gate_a_b, gate_x_w=gate_x_w,
             gate_x_b=gate_x_b, lru_lambda=lru_lambda, ret_gn_w=ret_gn_w, w_out=w_out, ln2_w=ln2_w, w_ffn_gate=w_ffn_gate,
             w_ffn_up=w_ffn_up, w_ffn_down=w_ffn_down, final_norm_w=final_norm_w)
    m = dict(ln1_w=m_ln1_w, w_in=m_w_in, conv_w=m_conv_w, conv_b=m_conv_b, gate_a_w=m_gate_a_w, gate_a_b=m_gate_a_b,
             gate_x_w=m_gate_x_w, gate_x_b=m_gate_x_b, lru_lambda=m_lru_lambda, ret_gn_w=m_ret_gn_w, w_out=m_w_out,
             ln2_w=m_ln2_w, w_ffn_gate=m_w_ffn_gate, w_ffn_up=m_w_ffn_up, w_ffn_down=m_w_ffn_down,
             final_norm_w=m_final_norm_w)
    v = dict(ln1_w=v_ln1_w, w_in=v_w_in, conv_w=v_conv_w, conv_b=v_conv_b, gate_a_w=v_gate_a_w, gate_a_b=v_gate_a_b,
             gate_x_w=v_gate_x_w, gate_x_b=v_gate_x_b, lru_lambda=v_lru_lambda, ret_gn_w=v_ret_gn_w, w_out=v_w_out,
             ln2_w=v_ln2_w, w_ffn_gate=v_w_ffn_gate, w_ffn_up=v_w_ffn_up, w_ffn_down=v_w_ffn_down,
             final_norm_w=v_final_norm_w)
    xs, tgt = x[0], loss_target[0]
    S, D = xs.shape
    DL, DR = conv_b.shape[1], ret_gn_w.shape[1]
    assert DL == DR and DL % HEAD_DIM == 0 and S % CHUNK == 0
    d_mix = DL + DR
    cx, cy, cc = lax.axis_index("x"), lax.axis_index("y"), lax.axis_index("c")
    chip = 2 * cx + cy
    place = jnp.stack([cc, chip]).astype(jnp.int32)
    grad, delta, new_m, new_v = {}, {}, {}, {}

    def finish_big(n, full):
        shp = w[n].shape
        g2 = full.reshape(shp[1], shp[2])
        w2, m2, v2 = (t[n].reshape(shp[1], shp[2]) for t in (w, m, v))
        d_, m_, v_, g_ = _adamw_call("adamw_" + n, w2, g2, m2, v2)
        grad[n], delta[n], new_m[n], new_v[n] = (t.reshape(shp) for t in (g_, d_, m_, v_))

    def all_sum(gathered, own):
        return _sum8(lax.dynamic_update_slice(gathered, own[None], (4 * cx + 2 * cy + cc, 0, 0)))

    st = {n: _own_slab(n, w[n][0], place) for n in BIG}
    (u1,), (w_in_st,) = _rms_fwd("rms1", xs, ln1_w, TM, comm=_gather_ici(st["w_in"]))
    w_in_st, conv_st = _comm_call("gather_w_in", [_gather_d2d(w_in_st), _gather_conv(conv_w[0])])
    conv_st = lax.dynamic_update_slice(conv_st, conv_w, (chip, 0, 0))
    cw_cols = conv_st.shape[2]
    conv_full = jnp.transpose(conv_st, (1, 0, 2)).reshape(CONV_W, 4 * cw_cols)
    n_in, n_ff = w_in_st.shape[2], st["w_ffn_gate"].shape[2]
    tables = _ret_tables(S, DR // HEAD_DIM)
    wab, wxb = gate_a_w[0].astype(BF16), gate_x_w[0].astype(BF16)
    lru_w = (conv_full, conv_b, wab, gate_a_b, wxb, gate_x_b, lru_lambda)

    proj, (w_out_st, wg_st) = _mm_nn_stacked("proj", u1, w_in_st, F32, TM_WIDE,
                                             comm=_merge([_gather_ici(st["w_out"]), _gather_ici(st["w_ffn_gate"])]))
    (hs, kept, y), (w_out_st, wg_st, wu_st) = _lru_fwd(
        proj, *lru_w, LRU_TILE, d_mix,
        comm=_merge([_gather_d2d(w_out_st), _gather_d2d(wg_st), _gather_ici(st["w_ffn_up"])]))
    (y, rprev), (wu_st, wd_st) = _ret_fwd(proj, y, tables, ret_gn_w, RET_BLOCK,
                                          comm=_merge([_gather_d2d(wu_st), _gather_ici(st["w_ffn_down"], NEIGHBOURS)]))
    w_out_f = w_out_st.reshape(d_mix, D)
    (h1, u2), (wd_st,) = _out_proj_rms(y, w_out_f, xs, ln2_w, TM, comm=_gather_ici(wd_st, DIAGONAL))
    (dg_fac, du_fac, ff), (wd_st,) = _ffn_gate_up(u2, wg_st, wu_st, TM, comm=_gather_d2d(wd_st))
    wd_f = wd_st.reshape(4 * n_ff, D)
    dh2, dh2b, d_fw, loss = _ffn_down_loss(ff, wd_f, h1, tgt, final_norm_w.reshape(1, D), TM_RESIDENT)

    g_wd = _mm_tn("g_w_down", ff, dh2b, n_ff, TILE_GRAD, TK_GRAD).reshape(4, n_ff, D)
    (dgt, dup), (ra_wd,) = _ffn_gate_up_bwd(dh2b, wd_f, dg_fac, du_fac, TM_WIDE, n_ff, comm=_pair_exchange(g_wd))
    pb_wd = _pair_sum("w_ffn_down", g_wd, ra_wd, place)
    g_wg, (rb_wd,) = _mm_tn("g_w_gate", u2, dgt, TILE_GRAD, None, TK_GRAD, stacked_cols=n_ff,
                            comm=_chip_exchange(pb_wd))
    slot_wd = _chip_sum("w_ffn_down", g_wd, ra_wd, rb_wd, place)
    g_wu, (full_wd, ra_wg) = _mm_tn("g_w_up", u2, dup, TILE_GRAD, None, TK_GRAD, stacked_cols=n_ff,
                                    comm=_merge([_pair_share(slot_wd), _pair_exchange(g_wg)]))
    finish_big("w_ffn_down", full_wd)
    pb_wg = _pair_sum("w_ffn_gate", g_wg, ra_wg, place)
    du2, (rb_wg,) = _mm_nt_stacked("d_u2_gate", dgt, wg_st, TM, comm=_chip_exchange(pb_wg))
    du2, (ra_wu,) = _mm_nt_stacked("d_u2_up", dup, wu_st, TM, res=du2, comm=_pair_exchange(g_wu))
    slot_wg = _chip_sum("w_ffn_gate", g_wg, ra_wg, rb_wg, place)
    pb_wu = _pair_sum("w_ffn_up", g_wu, ra_wu, place)
    (dh1, dh1b, dy, d_ln2), (full_wg,) = _rms_bwd_dy(h1, ln2_w, du2, dh2, w_out_f, TM_RESIDENT,
                                                     comm=_pair_share(slot_wg))
    finish_big("w_ffn_gate", full_wg)
    g_wout = _mm_tn("g_w_out", y, dh1b, TILE_GRAD, TILE_GRAD, TK_GRAD).reshape(4, d_mix // 4, D)
    (dproj, d_cw, d_cb, d_wa, d_ba, d_wx, d_bx, d_lam), (rb_wu, ra_wout) = _lru_bwd(
        proj, hs, kept, dy, conv_full, wab, wxb, lru_lambda, LRU_TILE,
        comm=_merge([_chip_exchange(pb_wu), _pair_exchange(g_wout)]))
    slot_wu = _chip_sum("w_ffn_up", g_wu, ra_wu, rb_wu, place)
    pb_wout = _pair_sum("w_out", g_wout, ra_wout, place)
    (dproj, d_gn), (full_wu, rb_wout) = _ret_bwd(proj, rprev, dy, dproj, tables, ret_gn_w, RET_BLOCK,
                                                 comm=_merge([_pair_share(slot_wu), _chip_exchange(pb_wout)]))
    finish_big("w_ffn_up", full_wu)
    slot_wout = _chip_sum("w_out", g_wout, ra_wout, rb_wout, place)
    small = dict(conv_w=d_cw, conv_b=d_cb, gate_a_w=d_wa, gate_a_b=d_ba, gate_x_w=d_wx, gate_x_b=d_bx, lru_lambda=d_lam,
                 ret_gn_w=d_gn, ln2_w=d_ln2, final_norm_w=d_fw)
    packed, offs = _pack([small[n] for n in SMALL[1:]] + [loss])
    g_win, (full_wout, got_small) = _mm_tn("g_w_in", u1, dproj, TILE_GRAD, None, TK_GRAD, stacked_cols=n_in,
                                           comm=_merge([_pair_share(slot_wout), _gather_small(packed)]))
    finish_big("w_out", full_wout)
    (ra_win,) = _comm_call("rs_pair_w_in", [_pair_exchange(g_win)])
    pb_win = _pair_sum("w_in", g_win, ra_win, place)
    du1, (rb_win,) = _mm_nt_stacked("d_u1", dproj, w_in_st, TM, comm=_chip_exchange(pb_win))
    slot_win = _chip_sum("w_in", g_win, ra_win, rb_win, place)
    gx, d_ln1 = _rms_bwd("rms1_bwd", xs, ln1_w, du1, dh1, TM)
    packed1, _ = _pack([d_ln1])
    full_win, got_ln1 = _comm_call("reduce_tail", [_pair_share(slot_win), _gather_small(packed1)])
    finish_big("w_in", full_win)

    red = _unpack(all_sum(got_small, packed), offs, [small[n].shape for n in SMALL[1:]] + [(1, LANES)])
    g = dict(zip(SMALL[1:], red[:-1]))
    g["ln1_w"] = all_sum(got_ln1, packed1)[:-(-D // LANES)].reshape(1, D)
    loss_out = red[-1][0, 0]
    g["conv_w"] = lax.dynamic_slice(g["conv_w"], (0, chip * cw_cols), (CONV_W, cw_cols))
    packs = [_pack([t[n] for n in SMALL])[0] for t in (w, m, v)]
    gp, offs2 = _pack([g[n] for n in SMALL])
    outs = _adamw_call("adamw_small", packs[0], gp, packs[1], packs[2])
    shapes = [w[n].shape for n in SMALL]
    for dst, arr in zip((delta, new_m, new_v), outs):
        dst.update(zip(SMALL, _unpack(arr, offs2, shapes)))
    for n in SMALL:
        grad[n] = g[n].reshape(w[n].shape)

    return (loss_out, gx.reshape(x.shape), *[grad[n] for n in WEIGHTS], *[delta[n] for n in WEIGHTS],
            *[new_m[n] for n in WEIGHTS], *[new_v[n] for n in WEIGHTS])
```

```python
import functools

import jax
import jax.numpy as jnp
from jax import lax
from jax.experimental import pallas as pl
from jax.experimental.pallas import tpu as pltpu

F32 = jnp.float32
BF16 = jnp.bfloat16
MESH = pl.DeviceIdType.MESH

EPS = 1e-6
LRU_C = 8.0
ROPE_BASE = 10000.0
CHUNK = 128
HEAD_DIM = 128
CONV_W = 4
ADAM_LR = 0.001
ADAM_B1 = 0.9
ADAM_B2 = 0.999
ADAM_EPS = 1e-08
ADAM_WD = 0.01
ADAM_STEP = 10

V7X_VMEM_BYTES = 64 * 1024 * 1024
VMEM_LIMIT = V7X_VMEM_BYTES - 8 * 1024 * 1024
LANES = 128
SUBLANES_16BIT = 16

TM = 512
TM_WIDE = 1024
TM_RESIDENT = 256
TK_GRAD = 2048
TILE_GRAD = 1024
LRU_TILE = 256
RET_BLOCK = 512

NN = (((1,), (0,)), ((), ()))
NT = (((1,), (1,)), ((), ()))
TN = (((0,), (0,)), ((), ()))


def _dot(a, b, dims=NN):
    return lax.dot_general(a, b, dims, preferred_element_type=F32)


def _tile(n, pref, mult=SUBLANES_16BIT):
    best = None
    t = mult
    while t <= min(n, pref):
        if n % t == 0:
            best = t
        t += mult
    return best if best is not None else n


def _row_tile(rows, cols, budget_bytes=2 * 1024 * 1024):
    return _tile(rows, max(SUBLANES_16BIT, budget_bytes // (cols * 4)))


def _params(sem):
    return pltpu.CompilerParams(dimension_semantics=sem, vmem_limit_bytes=VMEM_LIMIT)


HBM_SPEC = pl.BlockSpec(memory_space=pl.ANY)


class _Comm:
    def __init__(self, ins, outs, aliases, n_sem, start, finish):
        self.ins, self.outs, self.aliases, self.n_sem, self.start, self.finish = ins, outs, aliases, n_sem, start, finish


def _merge(tasks):
    ins, outs, aliases, plans, n_sem = [], [], {}, [], 0
    for t in tasks:
        i0, o0 = len(ins), len(outs)
        plans.append((t, i0, o0, n_sem))
        ins += t.ins
        outs += t.outs
        aliases.update({i0 + a: o0 + b for a, b in t.aliases.items()})
        n_sem += t.n_sem

    def run(which):
        def go(cins, couts, ssem, rsem, base):
            for t, i0, o0, s0 in plans:
                getattr(t, which)(cins[i0:i0 + len(t.ins)], couts[o0:o0 + len(t.outs)], ssem, rsem, base + s0)
        return go

    return _Comm(ins, outs, aliases, n_sem, run("start"), run("finish"))


def _pcall(body, *, name, grid, in_specs, out_specs, out_shape, operands, scratch_shapes=(), aliases=None, comm=None):
    n_in, n_out, n_scr = len(operands), len(out_shape), len(scratch_shapes)
    aliases = dict(aliases or {})
    params = _params(("arbitrary",) * len(grid))
    if comm is None:
        return pl.pallas_call(body, grid=grid, in_specs=list(in_specs), out_specs=list(out_specs), out_shape=list(out_shape),
                              scratch_shapes=list(scratch_shapes), input_output_aliases=aliases, name=name,
                              compiler_params=params)(*operands)
    nci, nco = len(comm.ins), len(comm.outs)

    def wrapped(*refs):
        ins, cins = refs[:n_in], refs[n_in:n_in + nci]
        o0 = n_in + nci
        outs, couts = refs[o0:o0 + n_out], refs[o0 + n_out:o0 + n_out + nco]
        s0 = o0 + n_out + nco
        scr, (ssem, rsem) = refs[s0:s0 + n_scr], refs[s0 + n_scr:]
        ids = [pl.program_id(a) for a in range(len(grid))]
        first = functools.reduce(jnp.logical_and, [i == 0 for i in ids])
        last = functools.reduce(jnp.logical_and, [i == g - 1 for i, g in zip(ids, grid)])

        @pl.when(first)
        def _():
            comm.start(cins, couts, ssem, rsem, 0)

        body(*ins, *outs, *scr)

        @pl.when(last)
        def _():
            comm.finish(cins, couts, ssem, rsem, 0)

    aliases.update({n_in + a: n_out + b for a, b in comm.aliases.items()})
    res = pl.pallas_call(
        wrapped, grid=grid, in_specs=list(in_specs) + [HBM_SPEC] * nci, out_specs=list(out_specs) + [HBM_SPEC] * nco,
        out_shape=list(out_shape) + list(comm.outs),
        scratch_shapes=list(scratch_shapes) + [pltpu.SemaphoreType.DMA((comm.n_sem,)), pltpu.SemaphoreType.DMA((comm.n_sem,))],
        input_output_aliases=aliases, name=name, compiler_params=params)(*operands, *comm.ins)
    return res[:n_out], res[n_out:]


def _ew(name, fn, ins, outs, grid, sp=None):
    n_in = len(ins)

    def body(*refs):
        if sp is not None:
            refs = refs[1:]
        vals = [r[...] for r in refs[:n_in]]
        res = fn(*vals)
        for o_ref, v in zip(refs[n_in:], res):
            o_ref[...] = v.astype(o_ref.dtype)

    in_specs = [s for _, s in ins]
    out_specs = [s for _, s in outs]
    out_shape = [s for s, _ in outs]
    sem = ("arbitrary",) * len(grid)
    if sp is None:
        return pl.pallas_call(body, grid=grid, in_specs=in_specs, out_specs=out_specs, out_shape=out_shape,
                              name=name, compiler_params=_params(sem))(*[a for a, _ in ins])
    gs = pltpu.PrefetchScalarGridSpec(num_scalar_prefetch=1, grid=grid, in_specs=in_specs, out_specs=out_specs)
    return pl.pallas_call(body, grid_spec=gs, out_shape=out_shape, name=name,
                          compiler_params=_params(sem))(sp, *[a for a, _ in ins])


def _matmul(name, pairs, dims, grid, out_shape, out_spec, acc_shape, res=None, comm=None):
    n = len(pairs)
    nk = grid[2]

    def body(*refs):
        ab = refs[:2 * n]
        pos = 2 * n
        res_ref = None
        if res is not None:
            res_ref = refs[pos]
            pos += 1
        o_ref = refs[pos]
        acc_ref = refs[pos + 1] if nk > 1 else None

        def partial():
            t = None
            for p in range(n):
                d = _dot(ab[2 * p][...], ab[2 * p + 1][...], dims)
                t = d if t is None else t + d
            return t

        def finish(t):
            if res_ref is not None:
                t = t + res_ref[...]
            o_ref[...] = t.astype(o_ref.dtype)

        if nk == 1:
            finish(partial())
        else:
            k = pl.program_id(2)

            @pl.when(k == 0)
            def _():
                acc_ref[...] = partial()

            @pl.when(k > 0)
            def _():
                acc_ref[...] += partial()

            @pl.when(k == nk - 1)
            def _():
                finish(acc_ref[...])

    operands, in_specs = [], []
    for a, a_spec, b, b_spec in pairs:
        operands += [a, b]
        in_specs += [a_spec, b_spec]
    if res is not None:
        operands.append(res[0])
        in_specs.append(res[1])
    scratch = [pltpu.VMEM(acc_shape, F32)] if nk > 1 else []
    res = _pcall(body, name=name, grid=grid, in_specs=in_specs, out_specs=[out_spec], out_shape=[out_shape],
                 operands=operands, scratch_shapes=scratch, comm=comm)
    return res[0] if comm is None else (res[0][0], res[1])


def _mm_nn_stacked(name, a, b_st, out_dtype, tm, comm=None):
    M, K = a.shape
    J, _, Nj = b_st.shape
    tm = _tile(M, tm)
    return _matmul(
        name, [(a, pl.BlockSpec((tm, K), lambda j, i, k: (i, 0)), b_st, pl.BlockSpec((None, K, Nj), lambda j, i, k: (j, 0, 0)))],
        NN, (J, M // tm, 1), jax.ShapeDtypeStruct((M, J * Nj), out_dtype), pl.BlockSpec((tm, Nj), lambda j, i, k: (i, j)), None,
        comm=comm)


def _mm_nt_stacked(name, a, b_st, tm, res=None, comm=None):
    M, K = a.shape
    N = b_st.shape[-2]
    tm = _tile(M, tm)

    def body(a_ref, b_ref, *rest):
        o_ref = rest[-1]
        t = None if res is None else rest[0][...]
        if len(b_ref.shape) == 2:
            d = _dot(a_ref[...], b_ref[...], NT)
            t = d if t is None else t + d
        else:
            nj = b_ref.shape[2]
            for s in range(b_ref.shape[0]):
                d = _dot(a_ref[:, s * nj:(s + 1) * nj], b_ref[s], NT)
                t = d if t is None else t + d
        o_ref[...] = t

    row = pl.BlockSpec((tm, N), lambda i: (i, 0))
    out = _pcall(body, name=name, grid=(M // tm,),
                 in_specs=[pl.BlockSpec((tm, K), lambda i: (i, 0)),
                           pl.BlockSpec(b_st.shape, lambda i: (0,) * b_st.ndim, pipeline_mode=pl.Buffered(1))]
                 + [row] * (res is not None),
                 out_specs=[row], out_shape=[jax.ShapeDtypeStruct((M, N), F32)],
                 operands=[a, b_st] + [res] * (res is not None), comm=comm)
    return out[0] if comm is None else (out[0][0], out[1])


MXU_COLUMNS = 256


def _col_blocks(n):
    return [slice(s, min(s + MXU_COLUMNS, n)) for s in range(0, n, MXU_COLUMNS)]


def _ffn_gate_up(u2, wg, wu, n_slab, tm, comm=None):
    S, D = u2.shape
    F = wg.shape[1]
    tm = _tile(S, tm)
    tn = 2 * n_slab if (n_slab % MXU_COLUMNS and (F // n_slab) % 2 == 0) else n_slab

    def body(a_ref, wg_ref, wu_ref, dg_ref, du_ref, ff_ref):
        a = a_ref[...]
        blocks = _col_blocks(tn)
        ahead = (_dot(a, wg_ref[:, blocks[0]]), _dot(a, wu_ref[:, blocks[0]]))
        for j, cols in enumerate(blocks):
            g, u = ahead
            if j + 1 < len(blocks):
                ahead = (_dot(a, wg_ref[:, blocks[j + 1]]), _dot(a, wu_ref[:, blocks[j + 1]]))
            sg = jax.nn.sigmoid(g)
            silu = g * sg
            dg_ref[:, cols] = (u * (sg * (1.0 + g * (1.0 - sg)))).astype(BF16)
            du_ref[:, cols] = silu.astype(BF16)
            ff_ref[:, cols] = (silu * u).astype(BF16)

    w_spec = pl.BlockSpec((D, tn), lambda j, i: (0, j), pipeline_mode=pl.Buffered(1))
    o_spec = pl.BlockSpec((tm, tn), lambda j, i: (i, j))
    o = jax.ShapeDtypeStruct((S, F), BF16)
    return _pcall(body, name="ffn_gate_up", grid=(F // tn, S // tm),
                  in_specs=[pl.BlockSpec((tm, D), lambda j, i: (i, 0)), w_spec, w_spec],
                  out_specs=[o_spec, o_spec, o_spec], out_shape=[o, o, o], operands=[u2, wg, wu], comm=comm)


def _ffn_gate_up_bwd(dh2b, wd, dg_fac, du_fac, tm, tn, comm=None):
    S, D = dh2b.shape
    F = wd.shape[0]
    tm, tn = _tile(S, tm), _tile(F, tn, LANES)

    def body(a_ref, wd_ref, dg_ref, du_ref, dgt_ref, dup_ref):
        a = a_ref[...]
        for cols in _col_blocks(tn):
            d = _dot(a, wd_ref[cols, :], NT)
            dgt_ref[:, cols] = (d * dg_ref[:, cols].astype(F32)).astype(BF16)
            dup_ref[:, cols] = (d * du_ref[:, cols].astype(F32)).astype(BF16)

    blk = pl.BlockSpec((tm, tn), lambda j, i: (i, j))
    o = jax.ShapeDtypeStruct((S, F), BF16)
    return _pcall(body, name="ffn_gate_up_bwd", grid=(F // tn, S // tm),
                  in_specs=[pl.BlockSpec((tm, D), lambda j, i: (i, 0)), pl.BlockSpec((tn, D), lambda j, i: (j, 0)), blk, blk],
                  out_specs=[blk, blk], out_shape=[o, o], operands=[dh2b, wd, dg_fac, du_fac], comm=comm)


def _mm_tn(name, a, b, tmo, tn, tk, stacked_cols=None, comm=None):
    S, Mo = a.shape
    N = b.shape[1]
    tmo, tk = _tile(Mo, tmo, LANES), _tile(S, tk)
    if stacked_cols is None:
        tn = _tile(N, tn, LANES)
        out_shape = jax.ShapeDtypeStruct((Mo, N), F32)
        out_spec = pl.BlockSpec((tmo, tn), lambda i, j, k: (i, j))
    else:
        tn = stacked_cols
        out_shape = jax.ShapeDtypeStruct((N // tn, Mo, tn), F32)
        out_spec = pl.BlockSpec((None, tmo, tn), lambda i, j, k: (j, i, 0))
    return _matmul(
        name, [(a, pl.BlockSpec((tk, tmo), lambda i, j, k: (k, i)), b, pl.BlockSpec((tk, tn), lambda i, j, k: (k, j)))],
        TN, (Mo // tmo, N // tn, S // tk), out_shape, out_spec, (tmo, tn), comm=comm)


def _rms_fwd(name, x, w, tm, comm=None):
    S, D = x.shape
    tm = _tile(S, tm)

    def body(x_ref, w_ref, o_ref):
        xv = x_ref[...]
        r = lax.rsqrt(jnp.mean(xv * xv, axis=-1, keepdims=True) + EPS)
        o_ref[...] = ((xv * r) * w_ref[...]).astype(BF16)

    row = pl.BlockSpec((tm, D), lambda i: (i, 0))
    return _pcall(body, name=name, grid=(S // tm,), in_specs=[row, pl.BlockSpec((1, D), lambda i: (0, 0))], out_specs=[row],
                  out_shape=[jax.ShapeDtypeStruct((S, D), BF16)], operands=[x, w], comm=comm)


def _rms_bwd(name, x, w, dy, dres, tm, comm=None):
    S, D = x.shape
    tm = _tile(S, tm)

    def body(x_ref, w_ref, dy_ref, dres_ref, dx_ref, dw_ref):
        i = pl.program_id(0)

        @pl.when(i == 0)
        def _():
            dw_ref[...] = jnp.zeros_like(dw_ref)

        xv = x_ref[...]
        r = lax.rsqrt(jnp.mean(xv * xv, axis=-1, keepdims=True) + EPS)
        nv = xv * r
        dyv = dy_ref[...]
        dn = dyv * w_ref[...]
        dw_ref[...] += jnp.sum(dyv * nv, axis=0, keepdims=True)
        dx = dres_ref[...] + r * (dn - nv * jnp.mean(dn * nv, axis=-1, keepdims=True))
        dx_ref[...] = dx

    row = pl.BlockSpec((tm, D), lambda i: (i, 0))
    vec = pl.BlockSpec((1, D), lambda i: (0, 0))
    return _pcall(body, name=name, grid=(S // tm,), in_specs=[row, vec, row, row], out_specs=[row, vec],
                  out_shape=[jax.ShapeDtypeStruct((S, D), F32), jax.ShapeDtypeStruct((1, D), F32)],
                  operands=[x, w, dy, dres], comm=comm)


def _rms_bwd_dy(h1, w, du2, dh2, w_out, tm, comm=None):
    S, D = h1.shape
    d_mix = w_out.shape[0]
    tm = _tile(S, tm)

    def body(x_ref, w_ref, dy_ref, dres_ref, wo_ref, dx_ref, dxb_ref, out_ref, dw_ref):
        i = pl.program_id(0)

        @pl.when(i == 0)
        def _():
            dw_ref[...] = jnp.zeros_like(dw_ref)

        xv = x_ref[...]
        r = lax.rsqrt(jnp.mean(xv * xv, axis=-1, keepdims=True) + EPS)
        nv = xv * r
        dyv = dy_ref[...]
        dn = dyv * w_ref[...]
        dw_ref[...] += jnp.sum(dyv * nv, axis=0, keepdims=True)
        dx = dres_ref[...] + r * (dn - nv * jnp.mean(dn * nv, axis=-1, keepdims=True))
        dx_ref[...] = dx
        dxb = dx.astype(BF16)
        dxb_ref[...] = dxb
        out_ref[...] = _dot(dxb, wo_ref[...], NT)

    row = pl.BlockSpec((tm, D), lambda i: (i, 0))
    vec = pl.BlockSpec((1, D), lambda i: (0, 0))
    return _pcall(
        body, name="rms2_bwd_dy", grid=(S // tm,),
        in_specs=[row, vec, row, row, pl.BlockSpec((d_mix, D), lambda i: (0, 0), pipeline_mode=pl.Buffered(1))],
        out_specs=[row, row, pl.BlockSpec((tm, d_mix), lambda i: (i, 0)), vec],
        out_shape=[jax.ShapeDtypeStruct((S, D), F32), jax.ShapeDtypeStruct((S, D), BF16),
                   jax.ShapeDtypeStruct((S, d_mix), F32), jax.ShapeDtypeStruct((1, D), F32)],
        operands=[h1, w, du2, dh2, w_out], comm=comm)


def _out_proj_rms(y, w_out, x, ln_w, tm, comm=None):
    S, K = y.shape
    D = w_out.shape[1]
    tm = _tile(S, tm)

    def body(a_ref, w_ref, x_ref, lw_ref, h_ref, u_ref):
        hv = _dot(a_ref[...], w_ref[...]) + x_ref[...]
        h_ref[...] = hv
        r = lax.rsqrt(jnp.mean(hv * hv, axis=-1, keepdims=True) + EPS)
        u_ref[...] = ((hv * r) * lw_ref[...]).astype(BF16)

    row = pl.BlockSpec((tm, D), lambda i: (i, 0))
    return _pcall(
        body, name="out_proj", grid=(S // tm,),
        in_specs=[pl.BlockSpec((tm, K), lambda i: (i, 0)),
                  pl.BlockSpec((K, D), lambda i: (0, 0), pipeline_mode=pl.Buffered(1)), row,
                  pl.BlockSpec((1, D), lambda i: (0, 0))],
        out_specs=[row, row], out_shape=[jax.ShapeDtypeStruct((S, D), F32), jax.ShapeDtypeStruct((S, D), BF16)],
        operands=[y, w_out, x, ln_w], comm=comm)


def _ffn_down_loss(ff, wd, h1, tgt, fw, tm):
    S, K = ff.shape
    D = wd.shape[1]
    tm = _tile(S, tm)

    def body(a_ref, wd_ref, h1_ref, t_ref, w_ref, dh_ref, dhb_ref, dw_ref, loss_ref):
        i = pl.program_id(0)

        @pl.when(i == 0)
        def _():
            dw_ref[...] = jnp.zeros_like(dw_ref)
            loss_ref[...] = jnp.zeros_like(loss_ref)

        hv = _dot(a_ref[...], wd_ref[...]) + h1_ref[...]
        wv = w_ref[...]
        r = lax.rsqrt(jnp.mean(hv * hv, axis=-1, keepdims=True) + EPS)
        nv = hv * r
        err = nv * wv - t_ref[...]
        row_loss = jnp.mean(err * err, axis=-1, keepdims=True)
        loss_ref[...] += 0.5 * jnp.sum(row_loss, axis=0, keepdims=True)
        dyo = err * (1.0 / D)
        dn = dyo * wv
        dw_ref[...] += jnp.sum(dyo * nv, axis=0, keepdims=True)
        dh = r * (dn - nv * jnp.mean(dn * nv, axis=-1, keepdims=True))
        dh_ref[...] = dh
        dhb_ref[...] = dh.astype(BF16)

    row = pl.BlockSpec((tm, D), lambda i: (i, 0))
    vec = pl.BlockSpec((1, D), lambda i: (0, 0))
    return _pcall(
        body, name="ffn_down_loss", grid=(S // tm,),
        in_specs=[pl.BlockSpec((tm, K), lambda i: (i, 0)),
                  pl.BlockSpec((K, D), lambda i: (0, 0), pipeline_mode=pl.Buffered(1)), row, row, vec],
        out_specs=[row, row, vec, pl.BlockSpec((1, LANES), lambda i: (0, 0))],
        out_shape=[jax.ShapeDtypeStruct((S, D), F32), jax.ShapeDtypeStruct((S, D), BF16),
                   jax.ShapeDtypeStruct((1, D), F32), jax.ShapeDtypeStruct((1, LANES), F32)],
        operands=[ff, wd, h1, tgt, fw])


def _shift_down(x, d, head8):
    r = pltpu.roll(x, d, 0)
    rh = pltpu.roll(head8, d, 0)
    row8 = lax.broadcasted_iota(jnp.int32, head8.shape, 0)
    top = jnp.where(row8 < d, rh, r[0:8])
    return jnp.concatenate([top, r[8:]], axis=0)


def _shift_up(x, d, tail8):
    n = x.shape[0]
    r = pltpu.roll(x, n - d, 0)
    rt = pltpu.roll(tail8, 8 - d, 0)
    row8 = lax.broadcasted_iota(jnp.int32, tail8.shape, 0)
    bot = jnp.where(row8 + d >= 8, rt, r[n - 8:n])
    return jnp.concatenate([r[:n - 8], bot], axis=0)


def _roll_in_groups(x, d):
    n, c = x.shape
    return pltpu.roll(x.reshape(n // 8, 8, c), d, 1).reshape(n, c)


def _log_sigmoid(lam):
    z = jnp.exp(-jnp.abs(lam))
    u = 1.0 + z
    log1p = jnp.where(u == 1.0, z, jnp.log(u) * (z / jnp.where(u == 1.0, 1.0, u - 1.0)))
    return jnp.minimum(lam, 0.0) - log1p


def _neg_expm1(z, exp_z):
    series = -z * (1.0 + z * (0.5 + z * (1.0 / 6.0)))
    return jnp.where(z > -0.02, series, 1.0 - exp_z)


_GELU_C = 0.7978845608028654


def _gelu(x):
    t = jnp.tanh(_GELU_C * (x + 0.044715 * (x * x * x)))
    return x * (0.5 * (1.0 + t)), t


def _gelu_grad(x, t):
    return 0.5 * (1.0 + t) + 0.5 * x * (1.0 - t * t) * (_GELU_C * (1.0 + 3.0 * 0.044715 * (x * x)))


def _lx_shifts(lx, head8):
    return [lx] + [_shift_down(lx, d, head8) for d in (1, 2, 3)]


def _lru_gates(lx, head8, cw, cb, wa_ref, ba, wx_ref, bx, ls):
    nb = wa_ref.shape[0]
    sh = _lx_shifts(lx, head8)
    cx = cb + sh[3] * cw[0:1]
    cx = cx + sh[2] * cw[1:2]
    cx = cx + sh[1] * cw[2:3]
    cx = cx + sh[0] * cw[3:4]
    cxb = cx.astype(BF16)
    ra = jnp.concatenate([_dot(cxb[:, n * HEAD_DIM:(n + 1) * HEAD_DIM], wa_ref[n]) for n in range(nb)], axis=1) + ba
    ia = jnp.concatenate([_dot(cxb[:, n * HEAD_DIM:(n + 1) * HEAD_DIM], wx_ref[n]) for n in range(nb)], axis=1) + bx
    r = jax.nn.sigmoid(ra)
    ig = jax.nn.sigmoid(ia)
    log_a = LRU_C * r * ls
    a = jnp.exp(log_a)
    return cx, r, ig, a, jnp.sqrt(_neg_expm1(2.0 * log_a, a * a))


def _lru_specs(tl, DL):
    nb = DL // HEAD_DIM
    vec = pl.BlockSpec((1, DL), lambda i: (0, 0))
    return [pl.BlockSpec((CONV_W, DL), lambda i: (0, 0)), vec,
            pl.BlockSpec((nb, HEAD_DIM, HEAD_DIM), lambda i: (0, 0, 0)), vec,
            pl.BlockSpec((nb, HEAD_DIM, HEAD_DIM), lambda i: (0, 0, 0)), vec, vec]


def _lru_fwd(proj, cw, cb, wa, ba, wx, bx, lam, tl, d_mix, comm=None):
    S = proj.shape[0]
    DL = cb.shape[1]
    tl = _tile(S, tl)

    def body(lx_ref, lg_ref, cw_ref, cb_ref, wa_ref, ba_ref, wx_ref, bx_ref, lam_ref, h_ref, kept_ref, y_ref,
             prev8, hc, a_s, b_s):
        i = pl.program_id(0)

        @pl.when(i == 0)
        def _():
            prev8[...] = jnp.zeros_like(prev8)
            hc[...] = jnp.zeros_like(hc)

        lx = lx_ref[...]
        ls = _log_sigmoid(lam_ref[...])
        kept = _lru_gates(lx, prev8[...], cw_ref[...], cb_ref[...], wa_ref, ba_ref[...], wx_ref, bx_ref[...], ls)
        for n, val in enumerate(kept):
            kept_ref[:, n * DL:(n + 1) * DL] = val
        cx, _, ig, a, mult = kept
        b = mult * (ig * cx)
        row = lax.broadcasted_iota(jnp.int32, a.shape, 0) & 7
        for d in (1, 2, 4):
            a_sh = _roll_in_groups(a, d)
            b_sh = _roll_in_groups(b, d)
            m = row >= d
            b = jnp.where(m, a * b_sh + b, b)
            a = jnp.where(m, a * a_sh, a)
        a_s[...] = a
        b_s[...] = b

        def step(g, hprev):
            sl = pl.ds(pl.multiple_of(g * 8, 8), 8)
            hh = a_s[sl, :] * hprev + b_s[sl, :]
            h_ref[sl, :] = hh
            return hh[7:8, :]

        hc[0:1, :] = lax.fori_loop(0, tl // 8, step, hc[0:1, :])
        prev8[...] = lx[tl - 8:tl]
        g, _ = _gelu(lg_ref[...])
        y_ref[...] = (h_ref[...] * g).astype(BF16)

    return _pcall(
        body, name="lru_fwd", grid=(S // tl,),
        in_specs=[pl.BlockSpec((tl, DL), lambda i: (i, 0)), pl.BlockSpec((tl, DL), lambda i: (i, 1))] + _lru_specs(tl, DL),
        out_specs=[pl.BlockSpec((tl, DL), lambda i: (i, 0)), pl.BlockSpec((tl, 5 * DL), lambda i: (i, 0)),
                   pl.BlockSpec((tl, DL), lambda i: (i, 0))],
        out_shape=[jax.ShapeDtypeStruct((S, DL), F32), jax.ShapeDtypeStruct((S, 5 * DL), F32),
                   jax.ShapeDtypeStruct((S, d_mix), BF16)],
        scratch_shapes=[pltpu.VMEM((8, DL), F32), pltpu.VMEM((8, DL), F32), pltpu.VMEM((tl, DL), F32), pltpu.VMEM((tl, DL), F32)],
        operands=[proj, proj, cw, cb, wa, ba, wx, bx, lam], comm=comm)


def _lru_bwd(proj, h, kept, dy, cw, wa, wx, lam, tl, comm=None):
    S = proj.shape[0]
    DL = lam.shape[1]
    nb = DL // HEAD_DIM
    tl = _tile(S, tl)
    nt = S // tl
    ng = tl // 8
    t8 = tl // 8

    def body(lx_ref, lxp_ref, lg_ref, h_ref, hp_ref, kept_ref, dy_ref, cw_ref, wa_ref, wx_ref, lam_ref,
             dlxg_ref, dcw_ref, dcb_ref, dwa_ref, dba_ref, dwx_ref, dbx_ref, dlam_ref,
             a_next, g_carry, dcx_next, an_s, dh_s, g_s):
        i = pl.program_id(0)

        @pl.when(i == 0)
        def _():
            for ref in (dcw_ref, dcb_ref, dwa_ref, dba_ref, dwx_ref, dbx_ref, dlam_ref, a_next, g_carry, dcx_next):
                ref[...] = jnp.zeros_like(ref)

        first = i == nt - 1
        hv = h_ref[...]
        lg = lg_ref[...]
        dyv = dy_ref[...]
        hhead8 = jnp.where(first, 0.0, hp_ref[...])
        lamv = lam_ref[...]
        ls = _log_sigmoid(lamv)
        cwv = cw_ref[...]
        sh = _lx_shifts(lx_ref[...], jnp.where(first, 0.0, lxp_ref[...]))
        cx, r, ig, a, mult = (kept_ref[:, n * DL:(n + 1) * DL] for n in range(5))
        cxb = cx.astype(BF16)
        hprev = _shift_down(hv, 1, hhead8)
        g, t = _gelu(lg)
        dlg = dyv * hv * _gelu_grad(lg, t)
        dh = dyv * g
        an = _shift_up(a, 1, a_next[...])
        row = lax.broadcasted_iota(jnp.int32, a.shape, 0) & 7
        for d in (1, 2, 4):
            an_sh = _roll_in_groups(an, 8 - d)
            dh_sh = _roll_in_groups(dh, 8 - d)
            m = row + d < 8
            dh = jnp.where(m, an * dh_sh + dh, dh)
            an = jnp.where(m, an * an_sh, an)
        an_s[...] = an
        dh_s[...] = dh

        def step(k, gc):
            sl = pl.ds(pl.multiple_of((ng - 1 - k) * 8, 8), 8)
            gg = an_s[sl, :] * gc + dh_s[sl, :]
            g_s[sl, :] = gg
            return gg[0:1, :]

        g_carry[0:1, :] = lax.fori_loop(0, ng, step, g_carry[0:1, :])
        a_next[...] = a[0:8]
        G = g_s[...]
        da = G * hprev
        icx = ig * cx
        dmult = G * icx
        dicx = G * mult
        di = dicx * cx
        dcx = dicx * ig
        dlog = da * a - dmult * ((a * a) * lax.rsqrt(mult * mult))
        dr = dlog * (LRU_C * ls)
        dlam_ref[...] += jnp.sum(dlog * (LRU_C * r), axis=0, keepdims=True)
        dra = dr * r * (1.0 - r)
        dia = di * ig * (1.0 - ig)
        dba_ref[...] += jnp.sum(dra, axis=0, keepdims=True)
        dbx_ref[...] += jnp.sum(dia, axis=0, keepdims=True)
        drab = dra.astype(BF16)
        diab = dia.astype(BF16)
        back = []
        for n in range(nb):
            cs = slice(n * HEAD_DIM, (n + 1) * HEAD_DIM)
            dwa_ref[n] += _dot(cxb[:, cs], drab[:, cs], TN)
            dwx_ref[n] += _dot(cxb[:, cs], diab[:, cs], TN)
            back.append(_dot(drab[:, cs], wa_ref[n], NT) + _dot(diab[:, cs], wx_ref[n], NT))
        dcx = dcx + jnp.concatenate(back, axis=1)
        dcb_ref[...] += jnp.sum(dcx, axis=0, keepdims=True)
        for tap in range(CONV_W):
            dcw_ref[tap:tap + 1, :] += jnp.sum(dcx * sh[CONV_W - 1 - tap], axis=0, keepdims=True)
        tail = dcx_next[...]
        dlx = dcx * cwv[3:4]
        for d in (1, 2, 3):
            dlx = dlx + _shift_up(dcx, d, tail) * cwv[3 - d:4 - d]
        dcx_next[...] = dcx[0:8]
        dlxg_ref[:, 0:DL] = dlx.astype(BF16)
        dlxg_ref[:, DL:2 * DL] = dlg.astype(BF16)

        @pl.when(i == nt - 1)
        def _():
            dlam_ref[...] = dlam_ref[...] * (1.0 - jax.nn.sigmoid(lamv))

    rev = lambda i: nt - 1 - i
    prev8_map = lambda i: (jnp.maximum((nt - 1 - i) * t8 - 1, 0), 0)
    vec = pl.BlockSpec((1, DL), lambda i: (0, 0))
    mat = pl.BlockSpec((nb, HEAD_DIM, HEAD_DIM), lambda i: (0, 0, 0))
    return _pcall(
        body, name="lru_bwd", grid=(nt,), operands=[proj, proj, proj, h, h, kept, dy, cw, wa, wx, lam], comm=comm,
        in_specs=[pl.BlockSpec((tl, DL), lambda i: (rev(i), 0)), pl.BlockSpec((8, DL), prev8_map),
                  pl.BlockSpec((tl, DL), lambda i: (rev(i), 1)),
                  pl.BlockSpec((tl, DL), lambda i: (rev(i), 0)), pl.BlockSpec((8, DL), prev8_map),
                  pl.BlockSpec((tl, 5 * DL), lambda i: (rev(i), 0)),
                  pl.BlockSpec((tl, DL), lambda i: (rev(i), 0)), pl.BlockSpec((CONV_W, DL), lambda i: (0, 0)), mat, mat, vec],
        out_specs=[pl.BlockSpec((tl, 2 * DL), lambda i: (rev(i), 0)), pl.BlockSpec((CONV_W, DL), lambda i: (0, 0)), vec,
                   mat, vec, mat, vec, vec],
        out_shape=[jax.ShapeDtypeStruct(proj.shape, BF16), jax.ShapeDtypeStruct((CONV_W, DL), F32),
                   jax.ShapeDtypeStruct((1, DL), F32), jax.ShapeDtypeStruct((nb, HEAD_DIM, HEAD_DIM), F32),
                   jax.ShapeDtypeStruct((1, DL), F32), jax.ShapeDtypeStruct((nb, HEAD_DIM, HEAD_DIM), F32),
                   jax.ShapeDtypeStruct((1, DL), F32), jax.ShapeDtypeStruct((1, DL), F32)],
        scratch_shapes=[pltpu.VMEM((8, DL), F32), pltpu.VMEM((8, DL), F32), pltpu.VMEM((8, DL), F32),
                        pltpu.VMEM((tl, DL), F32), pltpu.VMEM((tl, DL), F32), pltpu.VMEM((tl, DL), F32)])


def _ret_tables(S, H):
    pos = jnp.arange(S, dtype=F32)
    inv_freq = ROPE_BASE ** (-jnp.arange(0, HEAD_DIM, 2, dtype=F32) / HEAD_DIM)
    ang = pos[:, None] * inv_freq[None, :]
    cos, sin = jnp.cos(ang), jnp.sin(ang)
    cosf = jnp.concatenate([cos, cos], axis=1)
    sins = jnp.concatenate([-sin, sin], axis=1)
    log_gamma = jnp.log1p(-jnp.exp2(-5.0 - jnp.arange(H, dtype=F32)))
    idx = jnp.arange(CHUNK)
    diff = idx[:, None] - idx[None, :]
    causal = diff >= 0
    decay = jnp.where(causal[None], jnp.exp(log_gamma[:, None, None] * jnp.where(causal, diff, 0)[None].astype(F32)), 0.0)
    zeta = jnp.exp(log_gamma[:, None] * (CHUNK - 1 - idx).astype(F32)[None, :])
    xi = jnp.exp(log_gamma[:, None] * (idx + 1).astype(F32)[None, :])
    gc = jnp.exp(log_gamma * CHUNK)
    lanes = (H, CHUNK, HEAD_DIM)
    return (cosf, sins, decay, jnp.broadcast_to(zeta[:, :, None], lanes), jnp.broadcast_to(xi[:, :, None], lanes),
            jnp.broadcast_to(gc[:, None, None], lanes))


def _rope(t, cos, sin_signed):
    return t * cos + pltpu.roll(t, HEAD_DIM // 2, 1) * sin_signed


def _rope_t(d, cos, sin_signed):
    return d * cos + pltpu.roll(d * sin_signed, HEAD_DIM // 2, 1)


def _ret_const_specs(H, DR):
    full = pl.BlockSpec((H, CHUNK, HEAD_DIM), lambda *_: (0, 0, 0))
    return [full, full, full, full, pl.BlockSpec((1, DR), lambda *_: (0, 0))]


def _ret_fwd(proj, y, tables, gnw, tb, comm=None):
    S = proj.shape[0]
    DR = gnw.shape[1]
    H = DR // HEAD_DIM
    tb = _tile(S, tb, CHUNK)
    nc = tb // CHUNK
    cosf, sins, dm, zeta, xi, gc = tables
    scale = HEAD_DIM ** -0.5

    def body(qk_ref, vg_ref, cos_ref, sin_ref, dm_ref, zeta_ref, xi_ref, gc_ref, gnw_ref, y_in, y_ref, rprev_ref, r_s):
        del y_in
        i = pl.program_id(0)

        @pl.when(i == 0)
        def _():
            r_s[...] = jnp.zeros_like(r_s)

        def chunk(c, carry):
            rows = pl.ds(pl.multiple_of(c * CHUNK, CHUNK), CHUNK)
            cos = cos_ref[rows, :]
            sin = sin_ref[rows, :]
            heads = range(H)
            c0 = [slice(h * HEAD_DIM, (h + 1) * HEAD_DIM) for h in heads]
            c1 = [slice(DR + h * HEAD_DIM, DR + (h + 1) * HEAD_DIM) for h in heads]
            qh = [_rope(qk_ref[rows, c0[h]], cos, sin) for h in heads]
            kh = [_rope(qk_ref[rows, c1[h]], cos, sin) * scale for h in heads]
            vb = [vg_ref[rows, c0[h]].astype(BF16) for h in heads]
            rp = [r_s[h] for h in heads]
            rpb = [rp[h].astype(BF16) for h in heads]
            s = [_dot(qh[h].astype(BF16), kh[h].astype(BF16), NT) for h in heads]
            kv = [_dot((kh[h] * zeta_ref[h]).astype(BF16), vb[h], TN) for h in heads]
            cross = [_dot((qh[h] * xi_ref[h]).astype(BF16), rpb[h]) for h in heads]
            o = [_dot((s[h] * dm_ref[h]).astype(BF16), vb[h]) + cross[h] for h in heads]
            for h in heads:
                rprev_ref[c, h] = rpb[h]
                r_s[h] = rp[h] * gc_ref[h] + kv[h]
                mu = jnp.mean(o[h], axis=-1, keepdims=True)
                oc = o[h] - mu
                var = jnp.mean(oc * oc, axis=-1, keepdims=True)
                on = oc * lax.rsqrt(var + EPS) * gnw_ref[:, c0[h]]
                gate = vg_ref[rows, c1[h]]
                y_ref[rows, c0[h]] = (gate * jax.nn.sigmoid(gate) * on).astype(BF16)
            return carry

        lax.fori_loop(0, nc, chunk, 0)

    return _pcall(
        body, name="ret_fwd", grid=(S // tb,),
        in_specs=[pl.BlockSpec((tb, 2 * DR), lambda i: (i, 1)), pl.BlockSpec((tb, 2 * DR), lambda i: (i, 2)),
                  pl.BlockSpec((tb, HEAD_DIM), lambda i: (i, 0)), pl.BlockSpec((tb, HEAD_DIM), lambda i: (i, 0))]
        + _ret_const_specs(H, DR) + [HBM_SPEC],
        out_specs=[pl.BlockSpec((tb, DR), lambda i: (i, 1)),
                   pl.BlockSpec((nc, H, CHUNK, HEAD_DIM), lambda i: (i, 0, 0, 0))],
        out_shape=[jax.ShapeDtypeStruct(y.shape, BF16), jax.ShapeDtypeStruct((S // CHUNK, H, CHUNK, HEAD_DIM), BF16)],
        scratch_shapes=[pltpu.VMEM((H, CHUNK, HEAD_DIM), F32)], aliases={9: 0},
        operands=[proj, proj, cosf, sins, dm, zeta, xi, gc, gnw, y], comm=comm)


def _ret_bwd(proj, rprev, dy, dproj, tables, gnw, tb, comm=None):
    S = proj.shape[0]
    DR = gnw.shape[1]
    H = DR // HEAD_DIM
    tb = _tile(S, tb, CHUNK)
    nc = tb // CHUNK
    nt = S // tb
    cosf, sins, dm, zeta, xi, gc = tables
    scale = HEAD_DIM ** -0.5

    def body(qk_ref, vg_ref, cos_ref, sin_ref, dm_ref, zeta_ref, xi_ref, gc_ref, gnw_ref, rprev_ref, dy_ref, dp_in,
             dp_ref, dgn_ref, dr_s, dqk_s, dvg_s, out_sems):
        del dp_in
        i = pl.program_id(0)
        slot = i % 2

        def out_copies(step, sl):
            rows = pl.ds(pl.multiple_of((nt - 1 - step) * tb, tb), tb)
            return (pltpu.make_async_copy(dqk_s.at[sl], dp_ref.at[rows, pl.ds(2 * DR, 2 * DR)], out_sems.at[sl, 0]),
                    pltpu.make_async_copy(dvg_s.at[sl], dp_ref.at[rows, pl.ds(4 * DR, 2 * DR)], out_sems.at[sl, 1]))

        @pl.when(i == 0)
        def _():
            dr_s[...] = jnp.zeros_like(dr_s)
            dgn_ref[...] = jnp.zeros_like(dgn_ref)

        @pl.when(i >= 2)
        def _():
            for cp in out_copies(i - 2, slot):
                cp.wait()

        def chunk(cc, carry):
            c = nc - 1 - cc
            rows = pl.ds(pl.multiple_of(c * CHUNK, CHUNK), CHUNK)
            cos = cos_ref[rows, :]
            sin = sin_ref[rows, :]
            heads = range(H)
            c0 = [slice(h * HEAD_DIM, (h + 1) * HEAD_DIM) for h in heads]
            c1 = [slice(DR + h * HEAD_DIM, DR + (h + 1) * HEAD_DIM) for h in heads]
            qh = [_rope(qk_ref[rows, c0[h]], cos, sin) for h in heads]
            kh = [_rope(qk_ref[rows, c1[h]], cos, sin) * scale for h in heads]
            qb = [t.astype(BF16) for t in qh]
            kb = [t.astype(BF16) for t in kh]
            vb = [vg_ref[rows, c0[h]].astype(BF16) for h in heads]
            rpb = [rprev_ref[c, h] for h in heads]
            qx = [(qh[h] * xi_ref[h]).astype(BF16) for h in heads]
            kz = [(kh[h] * zeta_ref[h]).astype(BF16) for h in heads]
            drh = [dr_s[h] for h in heads]
            drb = [t.astype(BF16) for t in drh]
            s = [_dot(qb[h], kb[h], NT) for h in heads]
            cross = [_dot(qx[h], rpb[h]) for h in heads]
            dv_state = [_dot(kz[h], drb[h]) for h in heads]
            dk_state = [_dot(vb[h], drb[h], NT) for h in heads]
            sb = [(s[h] * dm_ref[h]).astype(BF16) for h in heads]
            o = [_dot(sb[h], vb[h]) + cross[h] for h in heads]
            dob = []
            for h in heads:
                mu = jnp.mean(o[h], axis=-1, keepdims=True)
                oc = o[h] - mu
                rstd = lax.rsqrt(jnp.mean(oc * oc, axis=-1, keepdims=True) + EPS)
                ohat = oc * rstd
                gw = gnw_ref[:, c0[h]]
                gate = vg_ref[rows, c1[h]]
                sg = jax.nn.sigmoid(gate)
                dyv = dy_ref[rows, c0[h]]
                dvg_s[slot, rows, c1[h]] = (dyv * (ohat * gw) * (sg * (1.0 + gate * (1.0 - sg)))).astype(BF16)
                don = dyv * (gate * sg)
                dgn_ref[:, c0[h]] += jnp.sum(don * ohat, axis=0, keepdims=True)
                dohat = don * gw
                do = rstd * (dohat - jnp.mean(dohat, axis=-1, keepdims=True)
                             - ohat * jnp.mean(dohat * ohat, axis=-1, keepdims=True))
                dob.append(do.astype(BF16))
            ds = [_dot(dob[h], vb[h], NT) for h in heads]
            dq_state = [_dot(dob[h], rpb[h], NT) for h in heads]
            dv = [_dot(sb[h], dob[h], TN) + dv_state[h] for h in heads]
            dr_new = [_dot(qx[h], dob[h], TN) for h in heads]
            dsb = [(ds[h] * dm_ref[h]).astype(BF16) for h in heads]
            dqh = [_dot(dsb[h], kb[h]) + dq_state[h] * xi_ref[h] for h in heads]
            dkh = [_dot(dsb[h], qb[h], TN) + dk_state[h] * zeta_ref[h] for h in heads]
            for h in heads:
                dr_s[h] = drh[h] * gc_ref[h] + dr_new[h]
                dqk_s[slot, rows, c0[h]] = _rope_t(dqh[h], cos, sin).astype(BF16)
                dqk_s[slot, rows, c1[h]] = _rope_t(dkh[h] * scale, cos, sin).astype(BF16)
                dvg_s[slot, rows, c0[h]] = dv[h].astype(BF16)
            return carry

        lax.fori_loop(0, nc, chunk, 0)
        for cp in out_copies(i, slot):
            cp.start()

        @pl.when(i == nt - 1)
        def _():
            if nt >= 2:
                for cp in out_copies(i - 1, 1 - slot):
                    cp.wait()
            for cp in out_copies(i, slot):
                cp.wait()

    rev = lambda i: nt - 1 - i
    return _pcall(
        body, name="ret_bwd", grid=(nt,), aliases={11: 0}, comm=comm,
        operands=[proj, proj, cosf, sins, dm, zeta, xi, gc, gnw, rprev, dy, dproj],
        in_specs=[pl.BlockSpec((tb, 2 * DR), lambda i: (rev(i), 1)), pl.BlockSpec((tb, 2 * DR), lambda i: (rev(i), 2)),
                  pl.BlockSpec((tb, HEAD_DIM), lambda i: (rev(i), 0)), pl.BlockSpec((tb, HEAD_DIM), lambda i: (rev(i), 0))]
        + _ret_const_specs(H, DR)
        + [pl.BlockSpec((nc, H, CHUNK, HEAD_DIM), lambda i: (rev(i), 0, 0, 0)),
           pl.BlockSpec((tb, DR), lambda i: (rev(i), 1)), HBM_SPEC],
        out_specs=[HBM_SPEC, pl.BlockSpec((1, DR), lambda i: (0, 0))],
        out_shape=[jax.ShapeDtypeStruct(dproj.shape, BF16), jax.ShapeDtypeStruct((1, DR), F32)],
        scratch_shapes=[pltpu.VMEM((H, CHUNK, HEAD_DIM), F32), pltpu.VMEM((2, tb, 2 * DR), BF16),
                        pltpu.VMEM((2, tb, 2 * DR), BF16), pltpu.SemaphoreType.DMA((2, 2))])


def _place():
    x, y, c = lax.axis_index("x"), lax.axis_index("y"), lax.axis_index("c")
    chips = [(1 - x, y), (x, 1 - y), (1 - x, 1 - y)]
    return x, y, c, chips


def _own_slab(name, shard, place, side_by_side=False):
    R, C = shard.shape
    tr = _row_tile(R, C)
    if side_by_side:
        out = (jax.ShapeDtypeStruct((R, 4 * C), BF16), pl.BlockSpec((tr, C), lambda i, p: (i, p[1])))
    else:
        out = (jax.ShapeDtypeStruct((4, R, C), BF16), pl.BlockSpec((None, tr, C), lambda i, p: (p[1], i, 0)))
    return _ew("cast_" + name, lambda a: (a,), [(shard, pl.BlockSpec((tr, C), lambda i, p: (i, 0)))], [out],
               (R // tr,), sp=place)[0]


def _slab_half(ref, chip, half, cols):
    if cols is None:
        r2 = ref.shape[1] // 2
        return ref.at[chip, pl.ds(half * r2, r2), :]
    r2 = ref.shape[0] // 2
    return ref.at[pl.ds(half * r2, r2), pl.ds(pl.multiple_of(chip * cols, LANES), cols)]


class _remote:
    def __init__(self, src, dst, ssem, rsem, k, to):
        self.args = dict(src_ref=src, dst_ref=dst, send_sem=ssem.at[k], recv_sem=rsem.at[k], device_id=to,
                         device_id_type=MESH)

    def start(self):
        pltpu.make_async_remote_copy(**self.args).start()

    def wait_send(self):
        pltpu.make_async_remote_copy(**self.args).wait_send()

    def wait_recv(self):
        pltpu.make_async_remote_copy(**self.args).wait_recv()


def _task_fns(copies):
    def start(cins, couts, ssem, rsem, base):
        for cp in copies(cins, couts, ssem, rsem, base)[0]:
            cp.start()

    def finish(cins, couts, ssem, rsem, base):
        sends, recvs = copies(cins, couts, ssem, rsem, base)
        for cp in sends:
            cp.wait_send()
        for cp in recvs:
            cp.wait_recv()

    return start, finish


NEIGHBOURS, DIAGONAL = (0, 1), (2,)


def _gather_ici(st, which=NEIGHBOURS + DIAGONAL, cols=None):
    def copies(cins, couts, ssem, rsem, base):
        x, y, c, chips = _place()
        out = couts[0]
        mine = _slab_half(out, 2 * x + y, c, cols)
        sends, recvs = [], []
        for k, j in enumerate(which):
            cx, cy = chips[j]
            got = _slab_half(out, 2 * cx + cy, c, cols)
            sends.append(_remote(mine, mine, ssem, rsem, base + k, (cx, cy, c)))
            recvs.append(_remote(got, got, ssem, rsem, base + k, (x, y, c)))
        return sends, recvs

    start, finish = _task_fns(copies)
    return _Comm([st], [jax.ShapeDtypeStruct(st.shape, st.dtype)], {0: 0}, len(which), start, finish)


def _gather_d2d(st, cols=None):
    def copies(cins, couts, ssem, rsem, base):
        x, y, c, chips = _place()
        out = couts[0]
        sends, recvs = [], []
        for j, (cx, cy) in enumerate(chips):
            have = _slab_half(out, 2 * cx + cy, c, cols)
            want = _slab_half(out, 2 * cx + cy, 1 - c, cols)
            sends.append(_remote(have, have, ssem, rsem, base + j, (x, y, 1 - c)))
            recvs.append(_remote(want, want, ssem, rsem, base + j, (x, y, c)))
        return sends, recvs

    start, finish = _task_fns(copies)
    return _Comm([st], [jax.ShapeDtypeStruct(st.shape, st.dtype)], {0: 0}, 3, start, finish)


def _gather_conv(conv_w):
    def copies(cins, couts, ssem, rsem, base):
        x, y, c, chips = _place()
        src, out = cins[0], couts[0]
        sends = [_remote(src, out.at[2 * x + y], ssem, rsem, base + j, (*chip, c)) for j, chip in enumerate(chips)]
        recvs = [_remote(src, out.at[2 * cx + cy], ssem, rsem, base + j, (x, y, c)) for j, (cx, cy) in enumerate(chips)]
        return sends, recvs

    start, finish = _task_fns(copies)
    return _Comm([conv_w], [jax.ShapeDtypeStruct((4,) + conv_w.shape, conv_w.dtype)], {}, 3, start, finish)


def _pair_exchange(g):
    r2 = g.shape[1] // 2

    def copies(cins, couts, ssem, rsem, base):
        x, y, c, _ = _place()
        cp = _remote(cins[0].at[:, pl.ds((1 - c) * r2, r2), :], couts[0], ssem, rsem, base, (x, y, 1 - c))
        return [cp], [cp]

    start, finish = _task_fns(copies)
    return _Comm([g], [jax.ShapeDtypeStruct((g.shape[0], r2, g.shape[2]), g.dtype)], {}, 1, start, finish)


def _chip_exchange(part):
    def copies(cins, couts, ssem, rsem, base):
        x, y, c, chips = _place()
        cps = [_remote(cins[0].at[2 * cx + cy], couts[0].at[j], ssem, rsem, base + j, (cx, cy, c))
               for j, (cx, cy) in enumerate(chips)]
        return cps, cps

    start, finish = _task_fns(copies)
    return _Comm([part], [jax.ShapeDtypeStruct((3,) + part.shape[1:], part.dtype)], {}, 3, start, finish)


def _pair_share(slot):
    def copies(cins, couts, ssem, rsem, base):
        x, y, c, _ = _place()
        out = couts[0]
        return ([_remote(out.at[c], out.at[c], ssem, rsem, base, (x, y, 1 - c))],
                [_remote(out.at[1 - c], out.at[1 - c], ssem, rsem, base, (x, y, c))])

    start, finish = _task_fns(copies)
    return _Comm([slot], [jax.ShapeDtypeStruct(slot.shape, slot.dtype)], {0: 0}, 1, start, finish)


def _gather_small(sm):
    flips = [(fx, fy, fc) for fx in (0, 1) for fy in (0, 1) for fc in (0, 1)][1:]

    def copies(cins, couts, ssem, rsem, base):
        x, y, c, _ = _place()
        src, out = cins[0], couts[0]
        peers = [(1 - x if fx else x, 1 - y if fy else y, 1 - c if fc else c) for fx, fy, fc in flips]
        sends = [_remote(src, out.at[4 * x + 2 * y + c], ssem, rsem, base + k, peer) for k, peer in enumerate(peers)]
        recvs = [_remote(src, out.at[4 * px + 2 * py + pc], ssem, rsem, base + k, (x, y, c))
                 for k, (px, py, pc) in enumerate(peers)]
        return sends, recvs

    start, finish = _task_fns(copies)
    return _Comm([sm], [jax.ShapeDtypeStruct((8,) + sm.shape, sm.dtype)], {}, 7, start, finish)


def _comm_call(name, tasks):
    task = _merge(tasks)
    nci = len(task.ins)

    def body(*refs):
        cins, couts, (ssem, rsem) = refs[:nci], refs[nci:nci + len(task.outs)], refs[nci + len(task.outs):]
        task.start(cins, couts, ssem, rsem, 0)
        task.finish(cins, couts, ssem, rsem, 0)

    return pl.pallas_call(
        body, in_specs=[HBM_SPEC] * nci, out_specs=[HBM_SPEC] * len(task.outs), out_shape=list(task.outs),
        scratch_shapes=[pltpu.SemaphoreType.DMA((task.n_sem,)), pltpu.SemaphoreType.DMA((task.n_sem,))],
        input_output_aliases=task.aliases, name=name)(*task.ins)


def _adamw(w, g, m, v):
    m = ADAM_B1 * m + (1.0 - ADAM_B1) * g
    v = ADAM_B2 * v + (1.0 - ADAM_B2) * (g * g)
    m_hat = m / (1.0 - ADAM_B1 ** ADAM_STEP)
    v_hat = v / (1.0 - ADAM_B2 ** ADAM_STEP)
    delta = -ADAM_LR * (m_hat / (jnp.sqrt(v_hat) + ADAM_EPS) + ADAM_WD * w)
    return delta, m, v


def _adamw_call(name, w, g, m, v):
    R, C = w.shape
    tr = _row_tile(R, C, 1024 * 1024)
    row = pl.BlockSpec((tr, C), lambda i: (i, 0))
    o = jax.ShapeDtypeStruct((R, C), F32)
    return _ew(name, lambda w_, g_, m_, v_: (*_adamw(w_, g_, m_, v_), g_), [(w, row), (g, row), (m, row), (v, row)],
               [(o, row), (o, row), (o, row), (o, row)], (R // tr,))


def _pair_sum(name, g, ra, place):
    _, R, C = g.shape
    r2 = R // 2
    tr = _row_tile(r2, C)
    nb = r2 // tr
    own = pl.BlockSpec((None, tr, C), lambda j, i, p: (j, p[0] * nb + i, 0))
    blk = pl.BlockSpec((None, tr, C), lambda j, i, p: (j, i, 0))
    return _ew("rs_pair_sum_" + name, lambda a, b: (a + b,), [(g, own), (ra, blk)],
               [(jax.ShapeDtypeStruct((4, r2, C), BF16), blk)], (4, nb), sp=place)[0]


def _chip_sum(name, g, ra, rb, place):
    _, R, C = g.shape
    r2 = R // 2
    tr = _row_tile(r2, C)
    nb = r2 // tr
    own = pl.BlockSpec((None, tr, C), lambda i, p: (p[1], p[0] * nb + i, 0))
    mine = pl.BlockSpec((None, tr, C), lambda i, p: (p[1], i, 0))
    src = [pl.BlockSpec((None, tr, C), functools.partial(lambda i, p, j: (j, i, 0), j=j)) for j in range(3)]
    out = pl.BlockSpec((None, tr, C), lambda i, p: (p[0], i, 0))

    def total(a, b, r0, r1, r2_):
        return ((((a + b) + r0.astype(F32)) + r1.astype(F32)) + r2_.astype(F32),)

    return _ew("rs_chip_sum_" + name, total, [(g, own), (ra, mine), (rb, src[0]), (rb, src[1]), (rb, src[2])],
               [(jax.ShapeDtypeStruct((2, r2, C), F32), out)], (nb,), sp=place)[0]


def _pack(arrays):
    rows, offs, pos = [], [], 0
    for a in arrays:
        flat = a.reshape(-1)
        n = -(-flat.shape[0] // (8 * LANES)) * (8 * LANES)
        if n != flat.shape[0]:
            flat = jnp.pad(flat, (0, n - flat.shape[0]))
        rows.append(flat.reshape(-1, LANES))
        offs.append(pos)
        pos += n // LANES
    return jnp.concatenate(rows, axis=0), offs


def _unpack(packed, offs, shapes):
    out = []
    for off, shp in zip(offs, shapes):
        n = 1
        for s in shp:
            n *= s
        out.append(packed[off:off + -(-n // LANES)].reshape(-1)[:n].reshape(shp))
    return out


def _sum8(gathered):
    _, R, C = gathered.shape
    tr = _row_tile(R, C, 256 * 1024)
    specs = [pl.BlockSpec((None, tr, C), functools.partial(lambda i, d: (d, i, 0), d=d)) for d in range(8)]

    def fn(*parts):
        t = parts[0]
        for p in parts[1:]:
            t = t + p
        return (t,)

    return _ew("small_sum", fn, [(gathered, s) for s in specs],
               [(jax.ShapeDtypeStruct((R, C), F32), pl.BlockSpec((tr, C), lambda i: (i, 0)))], (R // tr,))[0]


BIG = ("w_in", "w_out", "w_ffn_gate", "w_ffn_up", "w_ffn_down")
SMALL = ("ln1_w", "conv_w", "conv_b", "gate_a_w", "gate_a_b", "gate_x_w", "gate_x_b", "lru_lambda", "ret_gn_w", "ln2_w",
         "final_norm_w")
WEIGHTS = ("ln1_w", "w_in", "conv_w", "conv_b", "gate_a_w", "gate_a_b", "gate_x_w", "gate_x_b", "lru_lambda", "ret_gn_w",
           "w_out", "ln2_w", "w_ffn_gate", "w_ffn_up", "w_ffn_down", "final_norm_w")


def kernel(x, ln1_w, w_in, conv_w, conv_b, gate_a_w, gate_a_b, gate_x_w, gate_x_b, lru_lambda, ret_gn_w, w_out, ln2_w, w_ffn_gate, w_ffn_up, w_ffn_down, final_norm_w, loss_target, m_ln1_w, m_w_in, m_conv_w, m_conv_b, m_gate_a_w, m_gate_a_b, m_gate_x_w, m_gate_x_b, m_lru_lambda, m_ret_gn_w, m_w_out, m_ln2_w, m_w_ffn_gate, m_w_ffn_up, m_w_ffn_down, m_final_norm_w, v_ln1_w, v_w_in, v_conv_w, v_conv_b, v_gate_a_w, v_gate_a_b, v_gate_x_w, v_gate_x_b, v_lru_lambda, v_ret_gn_w, v_w_out, v_ln2_w, v_w_ffn_gate, v_w_ffn_up, v_w_ffn_down, v_final_norm_w):
    w = dict(ln1_w=ln1_w, w_in=w_in, conv_w=conv_w, conv_b=conv_b, gate_a_w=gate_a_w, gate_a_b=gate_a_b, gate_x_w=gate_x_w,
             gate_x_b=gate_x_b, lru_lambda=lru_lambda, ret_gn_w=ret_gn_w, w_out=w_out, ln2_w=ln2_w, w_ffn_gate=w_ffn_gate,
             w_ffn_up=w_ffn_up, w_ffn_down=w_ffn_down, final_norm_w=final_norm_w)
    m = dict(ln1_w=m_ln1_w, w_in=m_w_in, conv_w=m_conv_w, conv_b=m_conv_b, gate_a_w=m_gate_a_w, gate_a_b=m_gate_a_b,
             gate_x_w=m_gate_x_w, gate_x_b=m_gate_x_b, lru_lambda=m_lru_lambda, ret_gn_w=m_ret_gn_w, w_out=m_w_out,
             ln2_w=m_ln2_w, w_ffn_gate=m_w_ffn_gate, w_ffn_up=m_w_ffn_up, w_ffn_down=m_w_ffn_down,
             final_norm_w=m_final_norm_w)
    v = dict(ln1_w=v_ln1_w, w_in=v_w_in, conv_w=v_conv_w, conv_b=v_conv_b, gate_a_w=v_gate_a_w, gate_a_b=v_gate_a_b,
             gate_x_w=v_gate_x_w, gate_x_b=v_gate_x_b, lru_lambda=v_lru_lambda, ret_gn_w=v_ret_gn_w, w_out=v_w_out,
             ln2_w=v_ln2_w, w_ffn_gate=v_w_ffn_gate, w_ffn_up=v_w_ffn_up, w_ffn_down=v_w_ffn_down,
             final_norm_w=v_final_norm_w)
    xs, tgt = x[0], loss_target[0]
    S, D = xs.shape
    DL, DR = conv_b.shape[1], ret_gn_w.shape[1]
    assert DL == DR and DL % HEAD_DIM == 0 and S % CHUNK == 0
    d_mix = DL + DR
    cx, cy, cc = lax.axis_index("x"), lax.axis_index("y"), lax.axis_index("c")
    chip = 2 * cx + cy
    place = jnp.stack([cc, chip]).astype(jnp.int32)
    grad, delta, new_m, new_v = {}, {}, {}, {}

    def finish_big(n, full):
        shp = w[n].shape
        g2 = full.reshape(shp[1], shp[2])
        w2, m2, v2 = (t[n].reshape(shp[1], shp[2]) for t in (w, m, v))
        d_, m_, v_, g_ = _adamw_call("adamw_" + n, w2, g2, m2, v2)
        grad[n], delta[n], new_m[n], new_v[n] = (t.reshape(shp) for t in (g_, d_, m_, v_))

    def all_sum(gathered, own):
        return _sum8(lax.dynamic_update_slice(gathered, own[None], (4 * cx + 2 * cy + cc, 0, 0)))

    wide = ("w_ffn_gate", "w_ffn_up")
    st = {n: _own_slab(n, w[n][0], place, side_by_side=n in wide) for n in BIG}
    (u1,), (w_in_st,) = _rms_fwd("rms1", xs, ln1_w, TM, comm=_gather_ici(st["w_in"]))
    w_in_st, conv_st = _comm_call("gather_w_in", [_gather_d2d(w_in_st), _gather_conv(conv_w[0])])
    conv_st = lax.dynamic_update_slice(conv_st, conv_w, (chip, 0, 0))
    cw_cols = conv_st.shape[2]
    conv_full = jnp.transpose(conv_st, (1, 0, 2)).reshape(CONV_W, 4 * cw_cols)
    n_in, n_ff = w_in_st.shape[2], w_ffn_gate.shape[2]
    tables = _ret_tables(S, DR // HEAD_DIM)
    wab, wxb = gate_a_w[0].astype(BF16), gate_x_w[0].astype(BF16)
    lru_w = (conv_full, conv_b, wab, gate_a_b, wxb, gate_x_b, lru_lambda)

    proj, (w_out_st, wg_f) = _mm_nn_stacked(
        "proj", u1, w_in_st, F32, TM_WIDE,
        comm=_merge([_gather_ici(st["w_out"]), _gather_ici(st["w_ffn_gate"], cols=n_ff)]))
    (hs, kept, y), (w_out_st, wg_f, wu_f) = _lru_fwd(
        proj, *lru_w, LRU_TILE, d_mix,
        comm=_merge([_gather_d2d(w_out_st), _gather_d2d(wg_f, cols=n_ff), _gather_ici(st["w_ffn_up"], cols=n_ff)]))
    (y, rprev), (wu_f, wd_st) = _ret_fwd(
        proj, y, tables, ret_gn_w, RET_BLOCK,
        comm=_merge([_gather_d2d(wu_f, cols=n_ff), _gather_ici(st["w_ffn_down"], NEIGHBOURS)]))
    w_out_f = w_out_st.reshape(d_mix, D)
    (h1, u2), (wd_st,) = _out_proj_rms(y, w_out_f, xs, ln2_w, TM, comm=_gather_ici(wd_st, DIAGONAL))
    (dg_fac, du_fac, ff), (wd_st,) = _ffn_gate_up(u2, wg_f, wu_f, n_ff, TM, comm=_gather_d2d(wd_st))
    wd_f = wd_st.reshape(4 * n_ff, D)
    dh2, dh2b, d_fw, loss = _ffn_down_loss(ff, wd_f, h1, tgt, final_norm_w.reshape(1, D), TM_RESIDENT)

    g_wd = _mm_tn("g_w_down", ff, dh2b, n_ff, TILE_GRAD, TK_GRAD).reshape(4, n_ff, D)
    (dgt, dup), (ra_wd,) = _ffn_gate_up_bwd(dh2b, wd_f, dg_fac, du_fac, TM_WIDE, n_ff, comm=_pair_exchange(g_wd))
    pb_wd = _pair_sum("w_ffn_down", g_wd, ra_wd, place)
    g_wg, (rb_wd,) = _mm_tn("g_w_gate", u2, dgt, TILE_GRAD, None, TK_GRAD, stacked_cols=n_ff,
                            comm=_chip_exchange(pb_wd))
    slot_wd = _chip_sum("w_ffn_down", g_wd, ra_wd, rb_wd, place)
    g_wu, (full_wd, ra_wg) = _mm_tn("g_w_up", u2, dup, TILE_GRAD, None, TK_GRAD, stacked_cols=n_ff,
                                    comm=_merge([_pair_share(slot_wd), _pair_exchange(g_wg)]))
    finish_big("w_ffn_down", full_wd)
    pb_wg = _pair_sum("w_ffn_gate", g_wg, ra_wg, place)
    du2, (rb_wg,) = _mm_nt_stacked("d_u2_gate", dgt, wg_f, TM, comm=_chip_exchange(pb_wg))
    du2, (ra_wu,) = _mm_nt_stacked("d_u2_up", dup, wu_f, TM, res=du2, comm=_pair_exchange(g_wu))
    slot_wg = _chip_sum("w_ffn_gate", g_wg, ra_wg, rb_wg, place)
    pb_wu = _pair_sum("w_ffn_up", g_wu, ra_wu, place)
    (dh1, dh1b, dy, d_ln2), (full_wg,) = _rms_bwd_dy(h1, ln2_w, du2, dh2, w_out_f, TM_RESIDENT,
                                                     comm=_pair_share(slot_wg))
    finish_big("w_ffn_gate", full_wg)
    g_wout = _mm_tn("g_w_out", y, dh1b, TILE_GRAD, TILE_GRAD, TK_GRAD).reshape(4, d_mix // 4, D)
    (dproj, d_cw, d_cb, d_wa, d_ba, d_wx, d_bx, d_lam), (rb_wu, ra_wout) = _lru_bwd(
        proj, hs, kept, dy, conv_full, wab, wxb, lru_lambda, LRU_TILE,
        comm=_merge([_chip_exchange(pb_wu), _pair_exchange(g_wout)]))
    slot_wu = _chip_sum("w_ffn_up", g_wu, ra_wu, rb_wu, place)
    pb_wout = _pair_sum("w_out", g_wout, ra_wout, place)
    (dproj, d_gn), (full_wu, rb_wout) = _ret_bwd(proj, rprev, dy, dproj, tables, ret_gn_w, RET_BLOCK,
                                                 comm=_merge([_pair_share(slot_wu), _chip_exchange(pb_wout)]))
    finish_big("w_ffn_up", full_wu)
    slot_wout = _chip_sum("w_out", g_wout, ra_wout, rb_wout, place)
    small = dict(conv_w=d_cw, conv_b=d_cb, gate_a_w=d_wa, gate_a_b=d_ba, gate_x_w=d_wx, gate_x_b=d_bx, lru_lambda=d_lam,
                 ret_gn_w=d_gn, ln2_w=d_ln2, final_norm_w=d_fw)
    packed, offs = _pack([small[n] for n in SMALL[1:]] + [loss])
    g_win, (full_wout, got_small) = _mm_tn("g_w_in", u1, dproj, TILE_GRAD, None, TK_GRAD, stacked_cols=n_in,
                                           comm=_merge([_pair_share(slot_wout), _gather_small(packed)]))
    finish_big("w_out", full_wout)
    (ra_win,) = _comm_call("rs_pair_w_in", [_pair_exchange(g_win)])
    pb_win = _pair_sum("w_in", g_win, ra_win, place)
    du1, (rb_win,) = _mm_nt_stacked("d_u1", dproj, w_in_st, TM, comm=_chip_exchange(pb_win))
    slot_win = _chip_sum("w_in", g_win, ra_win, rb_win, place)
    gx, d_ln1 = _rms_bwd("rms1_bwd", xs, ln1_w, du1, dh1, TM)
    packed1, _ = _pack([d_ln1])
    full_win, got_ln1 = _comm_call("reduce_tail", [_pair_share(slot_win), _gather_small(packed1)])
    finish_big("w_in", full_win)

    red = _unpack(all_sum(got_small, packed), offs, [small[n].shape for n in SMALL[1:]] + [(1, LANES)])
    g = dict(zip(SMALL[1:], red[:-1]))
    g["ln1_w"] = all_sum(got_ln1, packed1)[:-(-D // LANES)].reshape(1, D)
    loss_out = red[-1][0, 0]
    g["conv_w"] = lax.dynamic_slice(g["conv_w"], (0, chip * cw_cols), (CONV_W, cw_cols))
    packs = [_pack([t[n] for n in SMALL])[0] for t in (w, m, v)]
    gp, offs2 = _pack([g[n] for n in SMALL])
    outs = _adamw_call("adamw_small", packs[0], gp, packs[1], packs[2])
    shapes = [w[n].shape for n in SMALL]
    for dst, arr in zip((delta, new_m, new_v), outs):
        dst.update(zip(SMALL, _unpack(arr, offs2, shapes)))
    for n in SMALL:
        grad[n] = g[n].reshape(w[n].shape)

    return (loss_out, gx.reshape(x.shape), *[grad[n] for n in WEIGHTS], *[delta[n] for n in WEIGHTS],
            *[new_m[n] for n in WEIGHTS], *[new_v[n] for n in WEIGHTS])
```

```python
import functools

import jax
import jax.numpy as jnp
from jax import lax
from jax.experimental import pallas as pl
from jax.experimental.pallas import tpu as pltpu

F32 = jnp.float32
BF16 = jnp.bfloat16
MESH = pl.DeviceIdType.MESH

EPS = 1e-6
LRU_C = 8.0
ROPE_BASE = 10000.0
CHUNK = 128
HEAD_DIM = 128
CONV_W = 4
ADAM_LR = 0.001
ADAM_B1 = 0.9
ADAM_B2 = 0.999
ADAM_EPS = 1e-08
ADAM_WD = 0.01
ADAM_STEP = 10

V7X_VMEM_BYTES = 64 * 1024 * 1024
VMEM_LIMIT = V7X_VMEM_BYTES - 8 * 1024 * 1024
LANES = 128
SUBLANES_16BIT = 16

TM = 512
TM_WIDE = 1024
TM_RESIDENT = 256
TK_GRAD = 2048
TILE_GRAD = 1024
LRU_TILE = 256
RET_BLOCK = 512

NN = (((1,), (0,)), ((), ()))
NT = (((1,), (1,)), ((), ()))
TN = (((0,), (0,)), ((), ()))


def _dot(a, b, dims=NN):
    return lax.dot_general(a, b, dims, preferred_element_type=F32)


def _tile(n, pref, mult=SUBLANES_16BIT):
    best = None
    t = mult
    while t <= min(n, pref):
        if n % t == 0:
            best = t
        t += mult
    return best if best is not None else n


def _row_tile(rows, cols, budget_bytes=2 * 1024 * 1024):
    return _tile(rows, max(SUBLANES_16BIT, budget_bytes // (cols * 4)))


def _params(sem):
    return pltpu.CompilerParams(dimension_semantics=sem, vmem_limit_bytes=VMEM_LIMIT)


HBM_SPEC = pl.BlockSpec(memory_space=pl.ANY)


class _Comm:
    def __init__(self, ins, outs, aliases, n_sem, start, finish):
        self.ins, self.outs, self.aliases, self.n_sem, self.start, self.finish = ins, outs, aliases, n_sem, start, finish


def _merge(tasks):
    ins, outs, aliases, plans, n_sem = [], [], {}, [], 0
    for t in tasks:
        i0, o0 = len(ins), len(outs)
        plans.append((t, i0, o0, n_sem))
        ins += t.ins
        outs += t.outs
        aliases.update({i0 + a: o0 + b for a, b in t.aliases.items()})
        n_sem += t.n_sem

    def run(which):
        def go(cins, couts, ssem, rsem, base):
            for t, i0, o0, s0 in plans:
                getattr(t, which)(cins[i0:i0 + len(t.ins)], couts[o0:o0 + len(t.outs)], ssem, rsem, base + s0)
        return go

    return _Comm(ins, outs, aliases, n_sem, run("start"), run("finish"))


def _pcall(body, *, name, grid, in_specs, out_specs, out_shape, operands, scratch_shapes=(), aliases=None, comm=None):
    n_in, n_out, n_scr = len(operands), len(out_shape), len(scratch_shapes)
    aliases = dict(aliases or {})
    params = _params(("arbitrary",) * len(grid))
    if comm is None:
        return pl.pallas_call(body, grid=grid, in_specs=list(in_specs), out_specs=list(out_specs), out_shape=list(out_shape),
                              scratch_shapes=list(scratch_shapes), input_output_aliases=aliases, name=name,
                              compiler_params=params)(*operands)
    nci, nco = len(comm.ins), len(comm.outs)

    def wrapped(*refs):
        ins, cins = refs[:n_in], refs[n_in:n_in + nci]
        o0 = n_in + nci
        outs, couts = refs[o0:o0 + n_out], refs[o0 + n_out:o0 + n_out + nco]
        s0 = o0 + n_out + nco
        scr, (ssem, rsem) = refs[s0:s0 + n_scr], refs[s0 + n_scr:]
        ids = [pl.program_id(a) for a in range(len(grid))]
        first = functools.reduce(jnp.logical_and, [i == 0 for i in ids])
        last = functools.reduce(jnp.logical_and, [i == g - 1 for i, g in zip(ids, grid)])

        @pl.when(first)
        def _():
            comm.start(cins, couts, ssem, rsem, 0)

        body(*ins, *outs, *scr)

        @pl.when(last)
        def _():
            comm.finish(cins, couts, ssem, rsem, 0)

    aliases.update({n_in + a: n_out + b for a, b in comm.aliases.items()})
    res = pl.pallas_call(
        wrapped, grid=grid, in_specs=list(in_specs) + [HBM_SPEC] * nci, out_specs=list(out_specs) + [HBM_SPEC] * nco,
        out_shape=list(out_shape) + list(comm.outs),
        scratch_shapes=list(scratch_shapes) + [pltpu.SemaphoreType.DMA((comm.n_sem,)), pltpu.SemaphoreType.DMA((comm.n_sem,))],
        input_output_aliases=aliases, name=name, compiler_params=params)(*operands, *comm.ins)
    return res[:n_out], res[n_out:]


def _ew(name, fn, ins, outs, grid, sp=None):
    n_in = len(ins)

    def body(*refs):
        if sp is not None:
            refs = refs[1:]
        vals = [r[...] for r in refs[:n_in]]
        res = fn(*vals)
        for o_ref, v in zip(refs[n_in:], res):
            o_ref[...] = v.astype(o_ref.dtype)

    in_specs = [s for _, s in ins]
    out_specs = [s for _, s in outs]
    out_shape = [s for s, _ in outs]
    sem = ("arbitrary",) * len(grid)
    if sp is None:
        return pl.pallas_call(body, grid=grid, in_specs=in_specs, out_specs=out_specs, out_shape=out_shape,
                              name=name, compiler_params=_params(sem))(*[a for a, _ in ins])
    gs = pltpu.PrefetchScalarGridSpec(num_scalar_prefetch=1, grid=grid, in_specs=in_specs, out_specs=out_specs)
    return pl.pallas_call(body, grid_spec=gs, out_shape=out_shape, name=name,
                          compiler_params=_params(sem))(sp, *[a for a, _ in ins])


def _matmul(name, pairs, dims, grid, out_shape, out_spec, acc_shape, res=None, comm=None, transpose_out=False):
    n = len(pairs)
    nk = grid[2]

    def body(*refs):
        ab = refs[:2 * n]
        pos = 2 * n
        res_ref = None
        if res is not None:
            res_ref = refs[pos]
            pos += 1
        o_ref = refs[pos]
        acc_ref = refs[pos + 1] if nk > 1 else None

        def partial():
            t = None
            for p in range(n):
                d = _dot(ab[2 * p][...], ab[2 * p + 1][...], dims)
                t = d if t is None else t + d
            return t

        def finish(t):
            if res_ref is not None:
                t = t + res_ref[...]
            o_ref[...] = (t.T if transpose_out else t).astype(o_ref.dtype)

        if nk == 1:
            finish(partial())
        else:
            k = pl.program_id(2)

            @pl.when(k == 0)
            def _():
                acc_ref[...] = partial()

            @pl.when(k > 0)
            def _():
                acc_ref[...] += partial()

            @pl.when(k == nk - 1)
            def _():
                finish(acc_ref[...])

    operands, in_specs = [], []
    for a, a_spec, b, b_spec in pairs:
        operands += [a, b]
        in_specs += [a_spec, b_spec]
    if res is not None:
        operands.append(res[0])
        in_specs.append(res[1])
    scratch = [pltpu.VMEM(acc_shape, F32)] if nk > 1 else []
    res = _pcall(body, name=name, grid=grid, in_specs=in_specs, out_specs=[out_spec], out_shape=[out_shape],
                 operands=operands, scratch_shapes=scratch, comm=comm)
    return res[0] if comm is None else (res[0][0], res[1])


def _mm_nn_stacked(name, a, b_st, out_dtype, tm, comm=None):
    M, K = a.shape
    J, _, Nj = b_st.shape
    tm = _tile(M, tm)
    return _matmul(
        name, [(a, pl.BlockSpec((tm, K), lambda j, i, k: (i, 0)), b_st, pl.BlockSpec((None, K, Nj), lambda j, i, k: (j, 0, 0)))],
        NN, (J, M // tm, 1), jax.ShapeDtypeStruct((M, J * Nj), out_dtype), pl.BlockSpec((tm, Nj), lambda j, i, k: (i, j)), None,
        comm=comm)


def _mm_nt_stacked(name, a, b_st, tm, res=None, comm=None):
    M, K = a.shape
    N = b_st.shape[-2]
    tm = _tile(M, tm)

    def body(a_ref, b_ref, *rest):
        o_ref = rest[-1]
        t = None if res is None else rest[0][...]
        if len(b_ref.shape) == 2:
            d = _dot(a_ref[...], b_ref[...], NT)
            t = d if t is None else t + d
        else:
            nj = b_ref.shape[2]
            for s in range(b_ref.shape[0]):
                d = _dot(a_ref[:, s * nj:(s + 1) * nj], b_ref[s], NT)
                t = d if t is None else t + d
        o_ref[...] = t

    row = pl.BlockSpec((tm, N), lambda i: (i, 0))
    out = _pcall(body, name=name, grid=(M // tm,),
                 in_specs=[pl.BlockSpec((tm, K), lambda i: (i, 0)),
                           pl.BlockSpec(b_st.shape, lambda i: (0,) * b_st.ndim, pipeline_mode=pl.Buffered(1))]
                 + [row] * (res is not None),
                 out_specs=[row], out_shape=[jax.ShapeDtypeStruct((M, N), F32)],
                 operands=[a, b_st] + [res] * (res is not None), comm=comm)
    return out[0] if comm is None else (out[0][0], out[1])


MXU_COLUMNS = 256


def _col_blocks(n):
    return [slice(s, min(s + MXU_COLUMNS, n)) for s in range(0, n, MXU_COLUMNS)]


def _ffn_gate_up(u2, wg, wu, n_slab, tm, comm=None):
    S, D = u2.shape
    F = wg.shape[1]
    tm = _tile(S, tm)
    tn = 2 * n_slab if (n_slab % MXU_COLUMNS and (F // n_slab) % 2 == 0) else n_slab

    def body(a_ref, wg_ref, wu_ref, dg_ref, du_ref, ff_ref):
        a = a_ref[...]
        blocks = _col_blocks(tn)
        ahead = (_dot(a, wg_ref[:, blocks[0]]), _dot(a, wu_ref[:, blocks[0]]))
        for j, cols in enumerate(blocks):
            g, u = ahead
            if j + 1 < len(blocks):
                ahead = (_dot(a, wg_ref[:, blocks[j + 1]]), _dot(a, wu_ref[:, blocks[j + 1]]))
            sg = jax.nn.sigmoid(g)
            silu = g * sg
            dg_ref[:, cols] = (u * (sg * (1.0 + g * (1.0 - sg)))).astype(BF16)
            du_ref[:, cols] = silu.astype(BF16)
            ff_ref[:, cols] = (silu * u).astype(BF16)

    w_spec = pl.BlockSpec((D, tn), lambda j, i: (0, j), pipeline_mode=pl.Buffered(1))
    o_spec = pl.BlockSpec((tm, tn), lambda j, i: (i, j))
    o = jax.ShapeDtypeStruct((S, F), BF16)
    return _pcall(body, name="ffn_gate_up", grid=(F // tn, S // tm),
                  in_specs=[pl.BlockSpec((tm, D), lambda j, i: (i, 0)), w_spec, w_spec],
                  out_specs=[o_spec, o_spec, o_spec], out_shape=[o, o, o], operands=[u2, wg, wu], comm=comm)


def _ffn_gate_up_bwd(dh2b, wd, dg_fac, du_fac, tm, tn, comm=None):
    S, D = dh2b.shape
    F = wd.shape[0]
    tm, tn = _tile(S, tm), _tile(F, tn, LANES)

    def body(a_ref, wd_ref, dg_ref, du_ref, dgt_ref, dup_ref):
        a = a_ref[...]
        for cols in _col_blocks(tn):
            d = _dot(a, wd_ref[cols, :], NT)
            dgt_ref[:, cols] = (d * dg_ref[:, cols].astype(F32)).astype(BF16)
            dup_ref[:, cols] = (d * du_ref[:, cols].astype(F32)).astype(BF16)

    blk = pl.BlockSpec((tm, tn), lambda j, i: (i, j))
    o = jax.ShapeDtypeStruct((S, F), BF16)
    return _pcall(body, name="ffn_gate_up_bwd", grid=(F // tn, S // tm),
                  in_specs=[pl.BlockSpec((tm, D), lambda j, i: (i, 0)), pl.BlockSpec((tn, D), lambda j, i: (j, 0)), blk, blk],
                  out_specs=[blk, blk], out_shape=[o, o], operands=[dh2b, wd, dg_fac, du_fac], comm=comm)


def _mm_tn(name, a, b, tmo, tn, tk, stacked_cols=None, comm=None):
    S, Mo = a.shape
    N = b.shape[1]
    tmo, tk = _tile(Mo, tmo, LANES), _tile(S, tk)
    if stacked_cols is None:
        tn = _tile(N, tn, LANES)
        out_shape = jax.ShapeDtypeStruct((Mo, N), F32)
        out_spec = pl.BlockSpec((tmo, tn), lambda i, j, k: (i, j))
    else:
        tn = stacked_cols
        out_shape = jax.ShapeDtypeStruct((N // tn, Mo, tn), F32)
        out_spec = pl.BlockSpec((None, tmo, tn), lambda i, j, k: (j, i, 0))
    return _matmul(
        name, [(a, pl.BlockSpec((tk, tmo), lambda i, j, k: (k, i)), b, pl.BlockSpec((tk, tn), lambda i, j, k: (k, j)))],
        TN, (Mo // tmo, N // tn, S // tk), out_shape, out_spec, (tmo, tn), comm=comm)


def _mm_tn_slabs(name, a, b, n_slab, tmo, tk, comm=None):
    S, Mo = a.shape
    J = b.shape[1] // n_slab
    tmo, tk = _tile(Mo, tmo, LANES), _tile(S, tk)
    return _matmul(
        name, [(b, pl.BlockSpec((tk, n_slab), lambda j, i, k: (k, j)), a, pl.BlockSpec((tk, tmo), lambda j, i, k: (k, i)))],
        TN, (J, Mo // tmo, S // tk), jax.ShapeDtypeStruct((J, Mo, n_slab), F32),
        pl.BlockSpec((None, tmo, n_slab), lambda j, i, k: (j, i, 0)), (n_slab, tmo), comm=comm, transpose_out=True)


def _rms_fwd(name, x, w, tm, comm=None):
    S, D = x.shape
    tm = _tile(S, tm)

    def body(x_ref, w_ref, o_ref):
        xv = x_ref[...]
        r = lax.rsqrt(jnp.mean(xv * xv, axis=-1, keepdims=True) + EPS)
        o_ref[...] = ((xv * r) * w_ref[...]).astype(BF16)

    row = pl.BlockSpec((tm, D), lambda i: (i, 0))
    return _pcall(body, name=name, grid=(S // tm,), in_specs=[row, pl.BlockSpec((1, D), lambda i: (0, 0))], out_specs=[row],
                  out_shape=[jax.ShapeDtypeStruct((S, D), BF16)], operands=[x, w], comm=comm)


def _rms_bwd(name, x, w, dy, dres, tm, comm=None):
    S, D = x.shape
    tm = _tile(S, tm)

    def body(x_ref, w_ref, dy_ref, dres_ref, dx_ref, dw_ref):
        i = pl.program_id(0)

        @pl.when(i == 0)
        def _():
            dw_ref[...] = jnp.zeros_like(dw_ref)

        xv = x_ref[...]
        r = lax.rsqrt(jnp.mean(xv * xv, axis=-1, keepdims=True) + EPS)
        nv = xv * r
        dyv = dy_ref[...]
        dn = dyv * w_ref[...]
        dw_ref[...] += jnp.sum(dyv * nv, axis=0, keepdims=True)
        dx = dres_ref[...] + r * (dn - nv * jnp.mean(dn * nv, axis=-1, keepdims=True))
        dx_ref[...] = dx

    row = pl.BlockSpec((tm, D), lambda i: (i, 0))
    vec = pl.BlockSpec((1, D), lambda i: (0, 0))
    return _pcall(body, name=name, grid=(S // tm,), in_specs=[row, vec, row, row], out_specs=[row, vec],
                  out_shape=[jax.ShapeDtypeStruct((S, D), F32), jax.ShapeDtypeStruct((1, D), F32)],
                  operands=[x, w, dy, dres], comm=comm)


def _rms_bwd_dy(h1, w, du2, dh2, w_out, tm, comm=None):
    S, D = h1.shape
    d_mix = w_out.shape[0]
    tm = _tile(S, tm)

    def body(x_ref, w_ref, dy_ref, dres_ref, wo_ref, dx_ref, dxb_ref, out_ref, dw_ref):
        i = pl.program_id(0)

        @pl.when(i == 0)
        def _():
            dw_ref[...] = jnp.zeros_like(dw_ref)

        xv = x_ref[...]
        r = lax.rsqrt(jnp.mean(xv * xv, axis=-1, keepdims=True) + EPS)
        nv = xv * r
        dyv = dy_ref[...]
        dn = dyv * w_ref[...]
        dw_ref[...] += jnp.sum(dyv * nv, axis=0, keepdims=True)
        dx = dres_ref[...] + r * (dn - nv * jnp.mean(dn * nv, axis=-1, keepdims=True))
        dx_ref[...] = dx
        dxb = dx.astype(BF16)
        dxb_ref[...] = dxb
        out_ref[...] = _dot(dxb, wo_ref[...], NT)

    row = pl.BlockSpec((tm, D), lambda i: (i, 0))
    vec = pl.BlockSpec((1, D), lambda i: (0, 0))
    return _pcall(
        body, name="rms2_bwd_dy", grid=(S // tm,),
        in_specs=[row, vec, row, row, pl.BlockSpec((d_mix, D), lambda i: (0, 0), pipeline_mode=pl.Buffered(1))],
        out_specs=[row, row, pl.BlockSpec((tm, d_mix), lambda i: (i, 0)), vec],
        out_shape=[jax.ShapeDtypeStruct((S, D), F32), jax.ShapeDtypeStruct((S, D), BF16),
                   jax.ShapeDtypeStruct((S, d_mix), F32), jax.ShapeDtypeStruct((1, D), F32)],
        operands=[h1, w, du2, dh2, w_out], comm=comm)


def _out_proj_rms(y, w_out, x, ln_w, tm, comm=None):
    S, K = y.shape
    D = w_out.shape[1]
    tm = _tile(S, tm)

    def body(a_ref, w_ref, x_ref, lw_ref, h_ref, u_ref):
        hv = _dot(a_ref[...], w_ref[...]) + x_ref[...]
        h_ref[...] = hv
        r = lax.rsqrt(jnp.mean(hv * hv, axis=-1, keepdims=True) + EPS)
        u_ref[...] = ((hv * r) * lw_ref[...]).astype(BF16)

    row = pl.BlockSpec((tm, D), lambda i: (i, 0))
    return _pcall(
        body, name="out_proj", grid=(S // tm,),
        in_specs=[pl.BlockSpec((tm, K), lambda i: (i, 0)),
                  pl.BlockSpec((K, D), lambda i: (0, 0), pipeline_mode=pl.Buffered(1)), row,
                  pl.BlockSpec((1, D), lambda i: (0, 0))],
        out_specs=[row, row], out_shape=[jax.ShapeDtypeStruct((S, D), F32), jax.ShapeDtypeStruct((S, D), BF16)],
        operands=[y, w_out, x, ln_w], comm=comm)


def _ffn_down_loss(ff, wd, h1, tgt, fw, tm):
    S, K = ff.shape
    D = wd.shape[1]
    tm = _tile(S, tm)

    def body(a_ref, wd_ref, h1_ref, t_ref, w_ref, dh_ref, dhb_ref, dw_ref, loss_ref):
        i = pl.program_id(0)

        @pl.when(i == 0)
        def _():
            dw_ref[...] = jnp.zeros_like(dw_ref)
            loss_ref[...] = jnp.zeros_like(loss_ref)

        hv = _dot(a_ref[...], wd_ref[...]) + h1_ref[...]
        wv = w_ref[...]
        r = lax.rsqrt(jnp.mean(hv * hv, axis=-1, keepdims=True) + EPS)
        nv = hv * r
        err = nv * wv - t_ref[...]
        row_loss = jnp.mean(err * err, axis=-1, keepdims=True)
        loss_ref[...] += 0.5 * jnp.sum(row_loss, axis=0, keepdims=True)
        dyo = err * (1.0 / D)
        dn = dyo * wv
        dw_ref[...] += jnp.sum(dyo * nv, axis=0, keepdims=True)
        dh = r * (dn - nv * jnp.mean(dn * nv, axis=-1, keepdims=True))
        dh_ref[...] = dh
        dhb_ref[...] = dh.astype(BF16)

    row = pl.BlockSpec((tm, D), lambda i: (i, 0))
    vec = pl.BlockSpec((1, D), lambda i: (0, 0))
    return _pcall(
        body, name="ffn_down_loss", grid=(S // tm,),
        in_specs=[pl.BlockSpec((tm, K), lambda i: (i, 0)),
                  pl.BlockSpec((K, D), lambda i: (0, 0), pipeline_mode=pl.Buffered(1)), row, row, vec],
        out_specs=[row, row, vec, pl.BlockSpec((1, LANES), lambda i: (0, 0))],
        out_shape=[jax.ShapeDtypeStruct((S, D), F32), jax.ShapeDtypeStruct((S, D), BF16),
                   jax.ShapeDtypeStruct((1, D), F32), jax.ShapeDtypeStruct((1, LANES), F32)],
        operands=[ff, wd, h1, tgt, fw])


def _shift_down(x, d, head8):
    r = pltpu.roll(x, d, 0)
    rh = pltpu.roll(head8, d, 0)
    row8 = lax.broadcasted_iota(jnp.int32, head8.shape, 0)
    top = jnp.where(row8 < d, rh, r[0:8])
    return jnp.concatenate([top, r[8:]], axis=0)


def _shift_up(x, d, tail8):
    n = x.shape[0]
    r = pltpu.roll(x, n - d, 0)
    rt = pltpu.roll(tail8, 8 - d, 0)
    row8 = lax.broadcasted_iota(jnp.int32, tail8.shape, 0)
    bot = jnp.where(row8 + d >= 8, rt, r[n - 8:n])
    return jnp.concatenate([r[:n - 8], bot], axis=0)


def _roll_in_groups(x, d):
    n, c = x.shape
    return pltpu.roll(x.reshape(n // 8, 8, c), d, 1).reshape(n, c)


def _log_sigmoid(lam):
    z = jnp.exp(-jnp.abs(lam))
    u = 1.0 + z
    log1p = jnp.where(u == 1.0, z, jnp.log(u) * (z / jnp.where(u == 1.0, 1.0, u - 1.0)))
    return jnp.minimum(lam, 0.0) - log1p


def _neg_expm1(z, exp_z):
    series = -z * (1.0 + z * (0.5 + z * (1.0 / 6.0)))
    return jnp.where(z > -0.02, series, 1.0 - exp_z)


_GELU_C = 0.7978845608028654


def _gelu(x):
    t = jnp.tanh(_GELU_C * (x + 0.044715 * (x * x * x)))
    return x * (0.5 * (1.0 + t)), t


def _gelu_grad(x, t):
    return 0.5 * (1.0 + t) + 0.5 * x * (1.0 - t * t) * (_GELU_C * (1.0 + 3.0 * 0.044715 * (x * x)))


def _lx_shifts(lx, head8):
    return [lx] + [_shift_down(lx, d, head8) for d in (1, 2, 3)]


def _lru_gates(lx, head8, cw, cb, wa_ref, ba, wx_ref, bx, ls):
    nb = wa_ref.shape[0]
    sh = _lx_shifts(lx, head8)
    cx = cb + sh[3] * cw[0:1]
    cx = cx + sh[2] * cw[1:2]
    cx = cx + sh[1] * cw[2:3]
    cx = cx + sh[0] * cw[3:4]
    cxb = cx.astype(BF16)
    ra = jnp.concatenate([_dot(cxb[:, n * HEAD_DIM:(n + 1) * HEAD_DIM], wa_ref[n]) for n in range(nb)], axis=1) + ba
    ia = jnp.concatenate([_dot(cxb[:, n * HEAD_DIM:(n + 1) * HEAD_DIM], wx_ref[n]) for n in range(nb)], axis=1) + bx
    r = jax.nn.sigmoid(ra)
    ig = jax.nn.sigmoid(ia)
    log_a = LRU_C * r * ls
    a = jnp.exp(log_a)
    return cx, r, ig, a, jnp.sqrt(_neg_expm1(2.0 * log_a, a * a))


def _lru_specs(tl, DL):
    nb = DL // HEAD_DIM
    vec = pl.BlockSpec((1, DL), lambda i: (0, 0))
    return [pl.BlockSpec((CONV_W, DL), lambda i: (0, 0)), vec,
            pl.BlockSpec((nb, HEAD_DIM, HEAD_DIM), lambda i: (0, 0, 0)), vec,
            pl.BlockSpec((nb, HEAD_DIM, HEAD_DIM), lambda i: (0, 0, 0)), vec, vec]


def _lru_fwd(proj, cw, cb, wa, ba, wx, bx, lam, tl, d_mix, comm=None):
    S = proj.shape[0]
    DL = cb.shape[1]
    tl = _tile(S, tl)

    def body(lx_ref, lg_ref, cw_ref, cb_ref, wa_ref, ba_ref, wx_ref, bx_ref, lam_ref, h_ref, kept_ref, y_ref,
             prev8, hc, a_s, b_s):
        i = pl.program_id(0)

        @pl.when(i == 0)
        def _():
            prev8[...] = jnp.zeros_like(prev8)
            hc[...] = jnp.zeros_like(hc)

        lx = lx_ref[...]
        ls = _log_sigmoid(lam_ref[...])
        kept = _lru_gates(lx, prev8[...], cw_ref[...], cb_ref[...], wa_ref, ba_ref[...], wx_ref, bx_ref[...], ls)
        for n, val in enumerate(kept):
            kept_ref[:, n * DL:(n + 1) * DL] = val
        cx, _, ig, a, mult = kept
        b = mult * (ig * cx)
        row = lax.broadcasted_iota(jnp.int32, a.shape, 0) & 7
        for d in (1, 2, 4):
            a_sh = _roll_in_groups(a, d)
            b_sh = _roll_in_groups(b, d)
            m = row >= d
            b = jnp.where(m, a * b_sh + b, b)
            a = jnp.where(m, a * a_sh, a)
        a_s[...] = a
        b_s[...] = b

        def step(g, hprev):
            sl = pl.ds(pl.multiple_of(g * 8, 8), 8)
            hh = a_s[sl, :] * hprev + b_s[sl, :]
            h_ref[sl, :] = hh
            return hh[7:8, :]

        hc[0:1, :] = lax.fori_loop(0, tl // 8, step, hc[0:1, :])
        prev8[...] = lx[tl - 8:tl]
        g, _ = _gelu(lg_ref[...])
        y_ref[...] = (h_ref[...] * g).astype(BF16)

    return _pcall(
        body, name="lru_fwd", grid=(S // tl,),
        in_specs=[pl.BlockSpec((tl, DL), lambda i: (i, 0)), pl.BlockSpec((tl, DL), lambda i: (i, 1))] + _lru_specs(tl, DL),
        out_specs=[pl.BlockSpec((tl, DL), lambda i: (i, 0)), pl.BlockSpec((tl, 5 * DL), lambda i: (i, 0)),
                   pl.BlockSpec((tl, DL), lambda i: (i, 0))],
        out_shape=[jax.ShapeDtypeStruct((S, DL), F32), jax.ShapeDtypeStruct((S, 5 * DL), F32),
                   jax.ShapeDtypeStruct((S, d_mix), BF16)],
        scratch_shapes=[pltpu.VMEM((8, DL), F32), pltpu.VMEM((8, DL), F32), pltpu.VMEM((tl, DL), F32), pltpu.VMEM((tl, DL), F32)],
        operands=[proj, proj, cw, cb, wa, ba, wx, bx, lam], comm=comm)


def _lru_bwd(proj, h, kept, dy, cw, wa, wx, lam, tl, comm=None):
    S = proj.shape[0]
    DL = lam.shape[1]
    nb = DL // HEAD_DIM
    tl = _tile(S, tl)
    nt = S // tl
    ng = tl // 8
    t8 = tl // 8

    def body(lx_ref, lxp_ref, lg_ref, h_ref, hp_ref, kept_ref, dy_ref, cw_ref, wa_ref, wx_ref, lam_ref,
             dlxg_ref, dcw_ref, dcb_ref, dwa_ref, dba_ref, dwx_ref, dbx_ref, dlam_ref,
             a_next, g_carry, dcx_next, an_s, dh_s, g_s):
        i = pl.program_id(0)

        @pl.when(i == 0)
        def _():
            for ref in (dcw_ref, dcb_ref, dwa_ref, dba_ref, dwx_ref, dbx_ref, dlam_ref, a_next, g_carry, dcx_next):
                ref[...] = jnp.zeros_like(ref)

        first = i == nt - 1
        hv = h_ref[...]
        lg = lg_ref[...]
        dyv = dy_ref[...]
        hhead8 = jnp.where(first, 0.0, hp_ref[...])
        lamv = lam_ref[...]
        ls = _log_sigmoid(lamv)
        cwv = cw_ref[...]
        sh = _lx_shifts(lx_ref[...], jnp.where(first, 0.0, lxp_ref[...]))
        cx, r, ig, a, mult = (kept_ref[:, n * DL:(n + 1) * DL] for n in range(5))
        cxb = cx.astype(BF16)
        hprev = _shift_down(hv, 1, hhead8)
        g, t = _gelu(lg)
        dlg = dyv * hv * _gelu_grad(lg, t)
        dh = dyv * g
        an = _shift_up(a, 1, a_next[...])
        row = lax.broadcasted_iota(jnp.int32, a.shape, 0) & 7
        for d in (1, 2, 4):
            an_sh = _roll_in_groups(an, 8 - d)
            dh_sh = _roll_in_groups(dh, 8 - d)
            m = row + d < 8
            dh = jnp.where(m, an * dh_sh + dh, dh)
            an = jnp.where(m, an * an_sh, an)
        an_s[...] = an
        dh_s[...] = dh

        def step(k, gc):
            sl = pl.ds(pl.multiple_of((ng - 1 - k) * 8, 8), 8)
            gg = an_s[sl, :] * gc + dh_s[sl, :]
            g_s[sl, :] = gg
            return gg[0:1, :]

        g_carry[0:1, :] = lax.fori_loop(0, ng, step, g_carry[0:1, :])
        a_next[...] = a[0:8]
        G = g_s[...]
        da = G * hprev
        icx = ig * cx
        dmult = G * icx
        dicx = G * mult
        di = dicx * cx
        dcx = dicx * ig
        dlog = da * a - dmult * ((a * a) * lax.rsqrt(mult * mult))
        dr = dlog * (LRU_C * ls)
        dlam_ref[...] += jnp.sum(dlog * (LRU_C * r), axis=0, keepdims=True)
        dra = dr * r * (1.0 - r)
        dia = di * ig * (1.0 - ig)
        dba_ref[...] += jnp.sum(dra, axis=0, keepdims=True)
        dbx_ref[...] += jnp.sum(dia, axis=0, keepdims=True)
        drab = dra.astype(BF16)
        diab = dia.astype(BF16)
        back = []
        for n in range(nb):
            cs = slice(n * HEAD_DIM, (n + 1) * HEAD_DIM)
            dwa_ref[n] += _dot(cxb[:, cs], drab[:, cs], TN)
            dwx_ref[n] += _dot(cxb[:, cs], diab[:, cs], TN)
            back.append(_dot(drab[:, cs], wa_ref[n], NT) + _dot(diab[:, cs], wx_ref[n], NT))
        dcx = dcx + jnp.concatenate(back, axis=1)
        dcb_ref[...] += jnp.sum(dcx, axis=0, keepdims=True)
        for tap in range(CONV_W):
            dcw_ref[tap:tap + 1, :] += jnp.sum(dcx * sh[CONV_W - 1 - tap], axis=0, keepdims=True)
        tail = dcx_next[...]
        dlx = dcx * cwv[3:4]
        for d in (1, 2, 3):
            dlx = dlx + _shift_up(dcx, d, tail) * cwv[3 - d:4 - d]
        dcx_next[...] = dcx[0:8]
        dlxg_ref[:, 0:DL] = dlx.astype(BF16)
        dlxg_ref[:, DL:2 * DL] = dlg.astype(BF16)

        @pl.when(i == nt - 1)
        def _():
            dlam_ref[...] = dlam_ref[...] * (1.0 - jax.nn.sigmoid(lamv))

    rev = lambda i: nt - 1 - i
    prev8_map = lambda i: (jnp.maximum((nt - 1 - i) * t8 - 1, 0), 0)
    vec = pl.BlockSpec((1, DL), lambda i: (0, 0))
    mat = pl.BlockSpec((nb, HEAD_DIM, HEAD_DIM), lambda i: (0, 0, 0))
    return _pcall(
        body, name="lru_bwd", grid=(nt,), operands=[proj, proj, proj, h, h, kept, dy, cw, wa, wx, lam], comm=comm,
        in_specs=[pl.BlockSpec((tl, DL), lambda i: (rev(i), 0)), pl.BlockSpec((8, DL), prev8_map),
                  pl.BlockSpec((tl, DL), lambda i: (rev(i), 1)),
                  pl.BlockSpec((tl, DL), lambda i: (rev(i), 0)), pl.BlockSpec((8, DL), prev8_map),
                  pl.BlockSpec((tl, 5 * DL), lambda i: (rev(i), 0)),
                  pl.BlockSpec((tl, DL), lambda i: (rev(i), 0)), pl.BlockSpec((CONV_W, DL), lambda i: (0, 0)), mat, mat, vec],
        out_specs=[pl.BlockSpec((tl, 2 * DL), lambda i: (rev(i), 0)), pl.BlockSpec((CONV_W, DL), lambda i: (0, 0)), vec,
                   mat, vec, mat, vec, vec],
        out_shape=[jax.ShapeDtypeStruct(proj.shape, BF16), jax.ShapeDtypeStruct((CONV_W, DL), F32),
                   jax.ShapeDtypeStruct((1, DL), F32), jax.ShapeDtypeStruct((nb, HEAD_DIM, HEAD_DIM), F32),
                   jax.ShapeDtypeStruct((1, DL), F32), jax.ShapeDtypeStruct((nb, HEAD_DIM, HEAD_DIM), F32),
                   jax.ShapeDtypeStruct((1, DL), F32), jax.ShapeDtypeStruct((1, DL), F32)],
        scratch_shapes=[pltpu.VMEM((8, DL), F32), pltpu.VMEM((8, DL), F32), pltpu.VMEM((8, DL), F32),
                        pltpu.VMEM((tl, DL), F32), pltpu.VMEM((tl, DL), F32), pltpu.VMEM((tl, DL), F32)])


def _ret_tables(S, H):
    pos = jnp.arange(S, dtype=F32)
    inv_freq = ROPE_BASE ** (-jnp.arange(0, HEAD_DIM, 2, dtype=F32) / HEAD_DIM)
    ang = pos[:, None] * inv_freq[None, :]
    cos, sin = jnp.cos(ang), jnp.sin(ang)
    cosf = jnp.concatenate([cos, cos], axis=1)
    sins = jnp.concatenate([-sin, sin], axis=1)
    log_gamma = jnp.log1p(-jnp.exp2(-5.0 - jnp.arange(H, dtype=F32)))
    idx = jnp.arange(CHUNK)
    diff = idx[:, None] - idx[None, :]
    causal = diff >= 0
    decay = jnp.where(causal[None], jnp.exp(log_gamma[:, None, None] * jnp.where(causal, diff, 0)[None].astype(F32)), 0.0)
    zeta = jnp.exp(log_gamma[:, None] * (CHUNK - 1 - idx).astype(F32)[None, :])
    xi = jnp.exp(log_gamma[:, None] * (idx + 1).astype(F32)[None, :])
    gc = jnp.exp(log_gamma * CHUNK)
    lanes = (H, CHUNK, HEAD_DIM)
    return (cosf, sins, decay, jnp.broadcast_to(zeta[:, :, None], lanes), jnp.broadcast_to(xi[:, :, None], lanes),
            jnp.broadcast_to(gc[:, None, None], lanes))


def _rope(t, cos, sin_signed):
    return t * cos + pltpu.roll(t, HEAD_DIM // 2, 1) * sin_signed


def _rope_t(d, cos, sin_signed):
    return d * cos + pltpu.roll(d * sin_signed, HEAD_DIM // 2, 1)


def _ret_const_specs(H, DR):
    full = pl.BlockSpec((H, CHUNK, HEAD_DIM), lambda *_: (0, 0, 0))
    return [full, full, full, full, pl.BlockSpec((1, DR), lambda *_: (0, 0))]


def _ret_fwd(proj, y, tables, gnw, tb, comm=None):
    S = proj.shape[0]
    DR = gnw.shape[1]
    H = DR // HEAD_DIM
    tb = _tile(S, tb, CHUNK)
    nc = tb // CHUNK
    cosf, sins, dm, zeta, xi, gc = tables
    scale = HEAD_DIM ** -0.5

    def body(qk_ref, vg_ref, cos_ref, sin_ref, dm_ref, zeta_ref, xi_ref, gc_ref, gnw_ref, y_in, y_ref, rprev_ref, r_s):
        del y_in
        i = pl.program_id(0)

        @pl.when(i == 0)
        def _():
            r_s[...] = jnp.zeros_like(r_s)

        def chunk(c, carry):
            rows = pl.ds(pl.multiple_of(c * CHUNK, CHUNK), CHUNK)
            cos = cos_ref[rows, :]
            sin = sin_ref[rows, :]
            heads = range(H)
            c0 = [slice(h * HEAD_DIM, (h + 1) * HEAD_DIM) for h in heads]
            c1 = [slice(DR + h * HEAD_DIM, DR + (h + 1) * HEAD_DIM) for h in heads]
            qh = [_rope(qk_ref[rows, c0[h]], cos, sin) for h in heads]
            kh = [_rope(qk_ref[rows, c1[h]], cos, sin) * scale for h in heads]
            vb = [vg_ref[rows, c0[h]].astype(BF16) for h in heads]
            rp = [r_s[h] for h in heads]
            rpb = [rp[h].astype(BF16) for h in heads]
            s = [_dot(qh[h].astype(BF16), kh[h].astype(BF16), NT) for h in heads]
            kv = [_dot((kh[h] * zeta_ref[h]).astype(BF16), vb[h], TN) for h in heads]
            cross = [_dot((qh[h] * xi_ref[h]).astype(BF16), rpb[h]) for h in heads]
            o = [_dot((s[h] * dm_ref[h]).astype(BF16), vb[h]) + cross[h] for h in heads]
            for h in heads:
                rprev_ref[c, h] = rpb[h]
                r_s[h] = rp[h] * gc_ref[h] + kv[h]
                mu = jnp.mean(o[h], axis=-1, keepdims=True)
                oc = o[h] - mu
                var = jnp.mean(oc * oc, axis=-1, keepdims=True)
                on = oc * lax.rsqrt(var + EPS) * gnw_ref[:, c0[h]]
                gate = vg_ref[rows, c1[h]]
                y_ref[rows, c0[h]] = (gate * jax.nn.sigmoid(gate) * on).astype(BF16)
            return carry

        lax.fori_loop(0, nc, chunk, 0)

    return _pcall(
        body, name="ret_fwd", grid=(S // tb,),
        in_specs=[pl.BlockSpec((tb, 2 * DR), lambda i: (i, 1)), pl.BlockSpec((tb, 2 * DR), lambda i: (i, 2)),
                  pl.BlockSpec((tb, HEAD_DIM), lambda i: (i, 0)), pl.BlockSpec((tb, HEAD_DIM), lambda i: (i, 0))]
        + _ret_const_specs(H, DR) + [HBM_SPEC],
        out_specs=[pl.BlockSpec((tb, DR), lambda i: (i, 1)),
                   pl.BlockSpec((nc, H, CHUNK, HEAD_DIM), lambda i: (i, 0, 0, 0))],
        out_shape=[jax.ShapeDtypeStruct(y.shape, BF16), jax.ShapeDtypeStruct((S // CHUNK, H, CHUNK, HEAD_DIM), BF16)],
        scratch_shapes=[pltpu.VMEM((H, CHUNK, HEAD_DIM), F32)], aliases={9: 0},
        operands=[proj, proj, cosf, sins, dm, zeta, xi, gc, gnw, y], comm=comm)


def _ret_bwd(proj, rprev, dy, dproj, tables, gnw, tb, comm=None):
    S = proj.shape[0]
    DR = gnw.shape[1]
    H = DR // HEAD_DIM
    tb = _tile(S, tb, CHUNK)
    nc = tb // CHUNK
    nt = S // tb
    cosf, sins, dm, zeta, xi, gc = tables
    scale = HEAD_DIM ** -0.5

    def body(qk_ref, vg_ref, cos_ref, sin_ref, dm_ref, zeta_ref, xi_ref, gc_ref, gnw_ref, rprev_ref, dy_ref, dp_in,
             dp_ref, dgn_ref, dr_s, dqk_s, dvg_s, out_sems):
        del dp_in
        i = pl.program_id(0)
        slot = i % 2

        def out_copies(step, sl):
            rows = pl.ds(pl.multiple_of((nt - 1 - step) * tb, tb), tb)
            return (pltpu.make_async_copy(dqk_s.at[sl], dp_ref.at[rows, pl.ds(2 * DR, 2 * DR)], out_sems.at[sl, 0]),
                    pltpu.make_async_copy(dvg_s.at[sl], dp_ref.at[rows, pl.ds(4 * DR, 2 * DR)], out_sems.at[sl, 1]))

        @pl.when(i == 0)
        def _():
            dr_s[...] = jnp.zeros_like(dr_s)
            dgn_ref[...] = jnp.zeros_like(dgn_ref)

        @pl.when(i >= 2)
        def _():
            for cp in out_copies(i - 2, slot):
                cp.wait()

        def chunk(cc, carry):
            c = nc - 1 - cc
            rows = pl.ds(pl.multiple_of(c * CHUNK, CHUNK), CHUNK)
            cos = cos_ref[rows, :]
            sin = sin_ref[rows, :]
            heads = range(H)
            c0 = [slice(h * HEAD_DIM, (h + 1) * HEAD_DIM) for h in heads]
            c1 = [slice(DR + h * HEAD_DIM, DR + (h + 1) * HEAD_DIM) for h in heads]
            qh = [_rope(qk_ref[rows, c0[h]], cos, sin) for h in heads]
            kh = [_rope(qk_ref[rows, c1[h]], cos, sin) * scale for h in heads]
            qb = [t.astype(BF16) for t in qh]
            kb = [t.astype(BF16) for t in kh]
            vb = [vg_ref[rows, c0[h]].astype(BF16) for h in heads]
            rpb = [rprev_ref[c, h] for h in heads]
            qx = [(qh[h] * xi_ref[h]).astype(BF16) for h in heads]
            kz = [(kh[h] * zeta_ref[h]).astype(BF16) for h in heads]
            drh = [dr_s[h] for h in heads]
            drb = [t.astype(BF16) for t in drh]
            s = [_dot(qb[h], kb[h], NT) for h in heads]
            cross = [_dot(qx[h], rpb[h]) for h in heads]
            dv_state = [_dot(kz[h], drb[h]) for h in heads]
            dk_state = [_dot(vb[h], drb[h], NT) for h in heads]
            sb = [(s[h] * dm_ref[h]).astype(BF16) for h in heads]
            o = [_dot(sb[h], vb[h]) + cross[h] for h in heads]
            dob = []
            for h in heads:
                mu = jnp.mean(o[h], axis=-1, keepdims=True)
                oc = o[h] - mu
                rstd = lax.rsqrt(jnp.mean(oc * oc, axis=-1, keepdims=True) + EPS)
                ohat = oc * rstd
                gw = gnw_ref[:, c0[h]]
                gate = vg_ref[rows, c1[h]]
                sg = jax.nn.sigmoid(gate)
                dyv = dy_ref[rows, c0[h]]
                dvg_s[slot, rows, c1[h]] = (dyv * (ohat * gw) * (sg * (1.0 + gate * (1.0 - sg)))).astype(BF16)
                don = dyv * (gate * sg)
                dgn_ref[:, c0[h]] += jnp.sum(don * ohat, axis=0, keepdims=True)
                dohat = don * gw
                do = rstd * (dohat - jnp.mean(dohat, axis=-1, keepdims=True)
                             - ohat * jnp.mean(dohat * ohat, axis=-1, keepdims=True))
                dob.append(do.astype(BF16))
            ds = [_dot(dob[h], vb[h], NT) for h in heads]
            dq_state = [_dot(dob[h], rpb[h], NT) for h in heads]
            dv = [_dot(sb[h], dob[h], TN) + dv_state[h] for h in heads]
            dr_new = [_dot(qx[h], dob[h], TN) for h in heads]
            dsb = [(ds[h] * dm_ref[h]).astype(BF16) for h in heads]
            dqh = [_dot(dsb[h], kb[h]) + dq_state[h] * xi_ref[h] for h in heads]
            dkh = [_dot(dsb[h], qb[h], TN) + dk_state[h] * zeta_ref[h] for h in heads]
            for h in heads:
                dr_s[h] = drh[h] * gc_ref[h] + dr_new[h]
                dqk_s[slot, rows, c0[h]] = _rope_t(dqh[h], cos, sin).astype(BF16)
                dqk_s[slot, rows, c1[h]] = _rope_t(dkh[h] * scale, cos, sin).astype(BF16)
                dvg_s[slot, rows, c0[h]] = dv[h].astype(BF16)
            return carry

        lax.fori_loop(0, nc, chunk, 0)
        for cp in out_copies(i, slot):
            cp.start()

        @pl.when(i == nt - 1)
        def _():
            if nt >= 2:
                for cp in out_copies(i - 1, 1 - slot):
                    cp.wait()
            for cp in out_copies(i, slot):
                cp.wait()

    rev = lambda i: nt - 1 - i
    return _pcall(
        body, name="ret_bwd", grid=(nt,), aliases={11: 0}, comm=comm,
        operands=[proj, proj, cosf, sins, dm, zeta, xi, gc, gnw, rprev, dy, dproj],
        in_specs=[pl.BlockSpec((tb, 2 * DR), lambda i: (rev(i), 1)), pl.BlockSpec((tb, 2 * DR), lambda i: (rev(i), 2)),
                  pl.BlockSpec((tb, HEAD_DIM), lambda i: (rev(i), 0)), pl.BlockSpec((tb, HEAD_DIM), lambda i: (rev(i), 0))]
        + _ret_const_specs(H, DR)
        + [pl.BlockSpec((nc, H, CHUNK, HEAD_DIM), lambda i: (rev(i), 0, 0, 0)),
           pl.BlockSpec((tb, DR), lambda i: (rev(i), 1)), HBM_SPEC],
        out_specs=[HBM_SPEC, pl.BlockSpec((1, DR), lambda i: (0, 0))],
        out_shape=[jax.ShapeDtypeStruct(dproj.shape, BF16), jax.ShapeDtypeStruct((1, DR), F32)],
        scratch_shapes=[pltpu.VMEM((H, CHUNK, HEAD_DIM), F32), pltpu.VMEM((2, tb, 2 * DR), BF16),
                        pltpu.VMEM((2, tb, 2 * DR), BF16), pltpu.SemaphoreType.DMA((2, 2))])


def _place():
    x, y, c = lax.axis_index("x"), lax.axis_index("y"), lax.axis_index("c")
    chips = [(1 - x, y), (x, 1 - y), (1 - x, 1 - y)]
    return x, y, c, chips


def _own_slab(name, shard, place, side_by_side=False):
    R, C = shard.shape
    tr = _row_tile(R, C)
    if side_by_side:
        out = (jax.ShapeDtypeStruct((R, 4 * C), BF16), pl.BlockSpec((tr, C), lambda i, p: (i, p[1])))
    else:
        out = (jax.ShapeDtypeStruct((4, R, C), BF16), pl.BlockSpec((None, tr, C), lambda i, p: (p[1], i, 0)))
    return _ew("cast_" + name, lambda a: (a,), [(shard, pl.BlockSpec((tr, C), lambda i, p: (i, 0)))], [out],
               (R // tr,), sp=place)[0]


def _slab_half(ref, chip, half, cols):
    if cols is None:
        r2 = ref.shape[1] // 2
        return ref.at[chip, pl.ds(half * r2, r2), :]
    r2 = ref.shape[0] // 2
    return ref.at[pl.ds(half * r2, r2), pl.ds(pl.multiple_of(chip * cols, LANES), cols)]


class _remote:
    def __init__(self, src, dst, ssem, rsem, k, to):
        self.args = dict(src_ref=src, dst_ref=dst, send_sem=ssem.at[k], recv_sem=rsem.at[k], device_id=to,
                         device_id_type=MESH)

    def start(self):
        pltpu.make_async_remote_copy(**self.args).start()

    def wait_send(self):
        pltpu.make_async_remote_copy(**self.args).wait_send()

    def wait_recv(self):
        pltpu.make_async_remote_copy(**self.args).wait_recv()


def _task_fns(copies):
    def start(cins, couts, ssem, rsem, base):
        for cp in copies(cins, couts, ssem, rsem, base)[0]:
            cp.start()

    def finish(cins, couts, ssem, rsem, base):
        sends, recvs = copies(cins, couts, ssem, rsem, base)
        for cp in sends:
            cp.wait_send()
        for cp in recvs:
            cp.wait_recv()

    return start, finish


NEIGHBOURS, DIAGONAL = (0, 1), (2,)


def _gather_ici(st, which=NEIGHBOURS + DIAGONAL, cols=None):
    def copies(cins, couts, ssem, rsem, base):
        x, y, c, chips = _place()
        out = couts[0]
        mine = _slab_half(out, 2 * x + y, c, cols)
        sends, recvs = [], []
        for k, j in enumerate(which):
            cx, cy = chips[j]
            got = _slab_half(out, 2 * cx + cy, c, cols)
            sends.append(_remote(mine, mine, ssem, rsem, base + k, (cx, cy, c)))
            recvs.append(_remote(got, got, ssem, rsem, base + k, (x, y, c)))
        return sends, recvs

    start, finish = _task_fns(copies)
    return _Comm([st], [jax.ShapeDtypeStruct(st.shape, st.dtype)], {0: 0}, len(which), start, finish)


def _gather_d2d(st, cols=None):
    def copies(cins, couts, ssem, rsem, base):
        x, y, c, chips = _place()
        out = couts[0]
        sends, recvs = [], []
        for j, (cx, cy) in enumerate(chips):
            have = _slab_half(out, 2 * cx + cy, c, cols)
            want = _slab_half(out, 2 * cx + cy, 1 - c, cols)
            sends.append(_remote(have, have, ssem, rsem, base + j, (x, y, 1 - c)))
            recvs.append(_remote(want, want, ssem, rsem, base + j, (x, y, c)))
        return sends, recvs

    start, finish = _task_fns(copies)
    return _Comm([st], [jax.ShapeDtypeStruct(st.shape, st.dtype)], {0: 0}, 3, start, finish)


def _gather_conv(conv_w):
    def copies(cins, couts, ssem, rsem, base):
        x, y, c, chips = _place()
        src, out = cins[0], couts[0]
        sends = [_remote(src, out.at[2 * x + y], ssem, rsem, base + j, (*chip, c)) for j, chip in enumerate(chips)]
        recvs = [_remote(src, out.at[2 * cx + cy], ssem, rsem, base + j, (x, y, c)) for j, (cx, cy) in enumerate(chips)]
        return sends, recvs

    start, finish = _task_fns(copies)
    return _Comm([conv_w], [jax.ShapeDtypeStruct((4,) + conv_w.shape, conv_w.dtype)], {}, 3, start, finish)


def _pair_exchange(g):
    r2 = g.shape[1] // 2

    def copies(cins, couts, ssem, rsem, base):
        x, y, c, _ = _place()
        cp = _remote(cins[0].at[:, pl.ds((1 - c) * r2, r2), :], couts[0], ssem, rsem, base, (x, y, 1 - c))
        return [cp], [cp]

    start, finish = _task_fns(copies)
    return _Comm([g], [jax.ShapeDtypeStruct((g.shape[0], r2, g.shape[2]), g.dtype)], {}, 1, start, finish)


def _chip_exchange(part):
    def copies(cins, couts, ssem, rsem, base):
        x, y, c, chips = _place()
        cps = [_remote(cins[0].at[2 * cx + cy], couts[0].at[j], ssem, rsem, base + j, (cx, cy, c))
               for j, (cx, cy) in enumerate(chips)]
        return cps, cps

    start, finish = _task_fns(copies)
    return _Comm([part], [jax.ShapeDtypeStruct((3,) + part.shape[1:], part.dtype)], {}, 3, start, finish)


def _pair_share(slot):
    def copies(cins, couts, ssem, rsem, base):
        x, y, c, _ = _place()
        out = couts[0]
        return ([_remote(out.at[c], out.at[c], ssem, rsem, base, (x, y, 1 - c))],
                [_remote(out.at[1 - c], out.at[1 - c], ssem, rsem, base, (x, y, c))])

    start, finish = _task_fns(copies)
    return _Comm([slot], [jax.ShapeDtypeStruct(slot.shape, slot.dtype)], {0: 0}, 1, start, finish)


def _gather_small(sm):
    flips = [(fx, fy, fc) for fx in (0, 1) for fy in (0, 1) for fc in (0, 1)][1:]

    def copies(cins, couts, ssem, rsem, base):
        x, y, c, _ = _place()
        src, out = cins[0], couts[0]
        peers = [(1 - x if fx else x, 1 - y if fy else y, 1 - c if fc else c) for fx, fy, fc in flips]
        sends = [_remote(src, out.at[4 * x + 2 * y + c], ssem, rsem, base + k, peer) for k, peer in enumerate(peers)]
        recvs = [_remote(src, out.at[4 * px + 2 * py + pc], ssem, rsem, base + k, (x, y, c))
                 for k, (px, py, pc) in enumerate(peers)]
        return sends, recvs

    start, finish = _task_fns(copies)
    return _Comm([sm], [jax.ShapeDtypeStruct((8,) + sm.shape, sm.dtype)], {}, 7, start, finish)


def _comm_call(name, tasks):
    task = _merge(tasks)
    nci = len(task.ins)

    def body(*refs):
        cins, couts, (ssem, rsem) = refs[:nci], refs[nci:nci + len(task.outs)], refs[nci + len(task.outs):]
        task.start(cins, couts, ssem, rsem, 0)
        task.finish(cins, couts, ssem, rsem, 0)

    return pl.pallas_call(
        body, in_specs=[HBM_SPEC] * nci, out_specs=[HBM_SPEC] * len(task.outs), out_shape=list(task.outs),
        scratch_shapes=[pltpu.SemaphoreType.DMA((task.n_sem,)), pltpu.SemaphoreType.DMA((task.n_sem,))],
        input_output_aliases=task.aliases, name=name)(*task.ins)


def _adamw(w, g, m, v):
    m = ADAM_B1 * m + (1.0 - ADAM_B1) * g
    v = ADAM_B2 * v + (1.0 - ADAM_B2) * (g * g)
    m_hat = m / (1.0 - ADAM_B1 ** ADAM_STEP)
    v_hat = v / (1.0 - ADAM_B2 ** ADAM_STEP)
    delta = -ADAM_LR * (m_hat / (jnp.sqrt(v_hat) + ADAM_EPS) + ADAM_WD * w)
    return delta, m, v


def _adamw_call(name, w, g, m, v):
    R, C = w.shape
    tr = _row_tile(R, C, 1024 * 1024)
    row = pl.BlockSpec((tr, C), lambda i: (i, 0))
    o = jax.ShapeDtypeStruct((R, C), F32)
    return _ew(name, lambda w_, g_, m_, v_: (*_adamw(w_, g_, m_, v_), g_), [(w, row), (g, row), (m, row), (v, row)],
               [(o, row), (o, row), (o, row), (o, row)], (R // tr,))


def _pair_sum(name, g, ra, place):
    _, R, C = g.shape
    r2 = R // 2
    tr = _row_tile(r2, C)
    nb = r2 // tr
    own = pl.BlockSpec((None, tr, C), lambda j, i, p: (j, p[0] * nb + i, 0))
    blk = pl.BlockSpec((None, tr, C), lambda j, i, p: (j, i, 0))
    return _ew("rs_pair_sum_" + name, lambda a, b: (a + b,), [(g, own), (ra, blk)],
               [(jax.ShapeDtypeStruct((4, r2, C), BF16), blk)], (4, nb), sp=place)[0]


def _chip_sum(name, g, ra, rb, place):
    _, R, C = g.shape
    r2 = R // 2
    tr = _row_tile(r2, C)
    nb = r2 // tr
    own = pl.BlockSpec((None, tr, C), lambda i, p: (p[1], p[0] * nb + i, 0))
    mine = pl.BlockSpec((None, tr, C), lambda i, p: (p[1], i, 0))
    src = [pl.BlockSpec((None, tr, C), functools.partial(lambda i, p, j: (j, i, 0), j=j)) for j in range(3)]
    out = pl.BlockSpec((None, tr, C), lambda i, p: (p[0], i, 0))

    def total(a, b, r0, r1, r2_):
        return ((((a + b) + r0.astype(F32)) + r1.astype(F32)) + r2_.astype(F32),)

    return _ew("rs_chip_sum_" + name, total, [(g, own), (ra, mine), (rb, src[0]), (rb, src[1]), (rb, src[2])],
               [(jax.ShapeDtypeStruct((2, r2, C), F32), out)], (nb,), sp=place)[0]


def _pack(arrays):
    rows, offs, pos = [], [], 0
    for a in arrays:
        flat = a.reshape(-1)
        n = -(-flat.shape[0] // (8 * LANES)) * (8 * LANES)
        if n != flat.shape[0]:
            flat = jnp.pad(flat, (0, n - flat.shape[0]))
        rows.append(flat.reshape(-1, LANES))
        offs.append(pos)
        pos += n // LANES
    return jnp.concatenate(rows, axis=0), offs


def _unpack(packed, offs, shapes):
    out = []
    for off, shp in zip(offs, shapes):
        n = 1
        for s in shp:
            n *= s
        out.append(packed[off:off + -(-n // LANES)].reshape(-1)[:n].reshape(shp))
    return out


def _sum8(gathered):
    _, R, C = gathered.shape
    tr = _row_tile(R, C, 256 * 1024)
    specs = [pl.BlockSpec((None, tr, C), functools.partial(lambda i, d: (d, i, 0), d=d)) for d in range(8)]

    def fn(*parts):
        t = parts[0]
        for p in parts[1:]:
            t = t + p
        return (t,)

    return _ew("small_sum", fn, [(gathered, s) for s in specs],
               [(jax.ShapeDtypeStruct((R, C), F32), pl.BlockSpec((tr, C), lambda i: (i, 0)))], (R // tr,))[0]


BIG = ("w_in", "w_out", "w_ffn_gate", "w_ffn_up", "w_ffn_down")
SMALL = ("ln1_w", "conv_w", "conv_b", "gate_a_w", "gate_a_b", "gate_x_w", "gate_x_b", "lru_lambda", "ret_gn_w", "ln2_w",
         "final_norm_w")
WEIGHTS = ("ln1_w", "w_in", "conv_w", "conv_b", "gate_a_w", "gate_a_b", "gate_x_w", "gate_x_b", "lru_lambda", "ret_gn_w",
           "w_out", "ln2_w", "w_ffn_gate", "w_ffn_up", "w_ffn_down", "final_norm_w")


def kernel(x, ln1_w, w_in, conv_w, conv_b, gate_a_w, gate_a_b, gate_x_w, gate_x_b, lru_lambda, ret_gn_w, w_out, ln2_w, w_ffn_gate, w_ffn_up, w_ffn_down, final_norm_w, loss_target, m_ln1_w, m_w_in, m_conv_w, m_conv_b, m_gate_a_w, m_gate_a_b, m_gate_x_w, m_gate_x_b, m_lru_lambda, m_ret_gn_w, m_w_out, m_ln2_w, m_w_ffn_gate, m_w_ffn_up, m_w_ffn_down, m_final_norm_w, v_ln1_w, v_w_in, v_conv_w, v_conv_b, v_gate_a_w, v_gate_a_b, v_gate_x_w, v_gate_x_b, v_lru_lambda, v_ret_gn_w, v_w_out, v_ln2_w, v_w_ffn_gate, v_w_ffn_up, v_w_ffn_down, v_final_norm_w):
    w = dict(ln1_w=ln1_w, w_in=w_in, conv_w=conv_w, conv_b=conv_b, gate_a_w=gate_a_w, gate_a_b=gate_a_b, gate_x_w=gate_x_w,
             gate_x_b=gate_x_b, lru_lambda=lru_lambda, ret_gn_w=ret_gn_w, w_out=w_out, ln2_w=ln2_w, w_ffn_gate=w_ffn_gate,
             w_ffn_up=w_ffn_up, w_ffn_down=w_ffn_down, final_norm_w=final_norm_w)
    m = dict(ln1_w=m_ln1_w, w_in=m_w_in, conv_w=m_conv_w, conv_b=m_conv_b, gate_a_w=m_gate_a_w, gate_a_b=m_gate_a_b,
             gate_x_w=m_gate_x_w, gate_x_b=m_gate_x_b, lru_lambda=m_lru_lambda, ret_gn_w=m_ret_gn_w, w_out=m_w_out,
             ln2_w=m_ln2_w, w_ffn_gate=m_w_ffn_gate, w_ffn_up=m_w_ffn_up, w_ffn_down=m_w_ffn_down,
             final_norm_w=m_final_norm_w)
    v = dict(ln1_w=v_ln1_w, w_in=v_w_in, conv_w=v_conv_w, conv_b=v_conv_b, gate_a_w=v_gate_a_w, gate_a_b=v_gate_a_b,
             gate_x_w=v_gate_x_w, gate_x_b=v_gate_x_b, lru_lambda=v_lru_lambda, ret_gn_w=v_ret_gn_w, w_out=v_w_out,
             ln2_w=v_ln2_w, w_ffn_gate=v_w_ffn_gate, w_ffn_up=v_w_ffn_up, w_ffn_down=v_w_ffn_down,
             final_norm_w=v_final_norm_w)
    xs, tgt = x[0], loss_target[0]
    S, D = xs.shape
    DL, DR = conv_b.shape[1], ret_gn_w.shape[1]
    assert DL == DR and DL % HEAD_DIM == 0 and S % CHUNK == 0
    d_mix = DL + DR
    cx, cy, cc = lax.axis_index("x"), lax.axis_index("y"), lax.axis_index("c")
    chip = 2 * cx + cy
    place = jnp.stack([cc, chip]).astype(jnp.int32)
    grad, delta, new_m, new_v = {}, {}, {}, {}

    def finish_big(n, full):
        shp = w[n].shape
        g2 = full.reshape(shp[1], shp[2])
        w2, m2, v2 = (t[n].reshape(shp[1], shp[2]) for t in (w, m, v))
        d_, m_, v_, g_ = _adamw_call("adamw_" + n, w2, g2, m2, v2)
        grad[n], delta[n], new_m[n], new_v[n] = (t.reshape(shp) for t in (g_, d_, m_, v_))

    def all_sum(gathered, own):
        return _sum8(lax.dynamic_update_slice(gathered, own[None], (4 * cx + 2 * cy + cc, 0, 0)))

    wide = ("w_ffn_gate", "w_ffn_up")
    st = {n: _own_slab(n, w[n][0], place, side_by_side=n in wide) for n in BIG}
    (u1,), (w_in_st,) = _rms_fwd("rms1", xs, ln1_w, TM, comm=_gather_ici(st["w_in"]))
    w_in_st, conv_st = _comm_call("gather_w_in", [_gather_d2d(w_in_st), _gather_conv(conv_w[0])])
    conv_st = lax.dynamic_update_slice(conv_st, conv_w, (chip, 0, 0))
    cw_cols = conv_st.shape[2]
    conv_full = jnp.transpose(conv_st, (1, 0, 2)).reshape(CONV_W, 4 * cw_cols)
    n_in, n_ff = w_in_st.shape[2], w_ffn_gate.shape[2]
    tables = _ret_tables(S, DR // HEAD_DIM)
    wab, wxb = gate_a_w[0].astype(BF16), gate_x_w[0].astype(BF16)
    lru_w = (conv_full, conv_b, wab, gate_a_b, wxb, gate_x_b, lru_lambda)

    proj, (w_out_st, wg_f) = _mm_nn_stacked(
        "proj", u1, w_in_st, F32, TM_WIDE,
        comm=_merge([_gather_ici(st["w_out"]), _gather_ici(st["w_ffn_gate"], cols=n_ff)]))
    (hs, kept, y), (w_out_st, wg_f, wu_f) = _lru_fwd(
        proj, *lru_w, LRU_TILE, d_mix,
        comm=_merge([_gather_d2d(w_out_st), _gather_d2d(wg_f, cols=n_ff), _gather_ici(st["w_ffn_up"], cols=n_ff)]))
    (y, rprev), (wu_f, wd_st) = _ret_fwd(
        proj, y, tables, ret_gn_w, RET_BLOCK,
        comm=_merge([_gather_d2d(wu_f, cols=n_ff), _gather_ici(st["w_ffn_down"], NEIGHBOURS)]))
    w_out_f = w_out_st.reshape(d_mix, D)
    (h1, u2), (wd_st,) = _out_proj_rms(y, w_out_f, xs, ln2_w, TM, comm=_gather_ici(wd_st, DIAGONAL))
    (dg_fac, du_fac, ff), (wd_st,) = _ffn_gate_up(u2, wg_f, wu_f, n_ff, TM, comm=_gather_d2d(wd_st))
    wd_f = wd_st.reshape(4 * n_ff, D)
    dh2, dh2b, d_fw, loss = _ffn_down_loss(ff, wd_f, h1, tgt, final_norm_w.reshape(1, D), TM_RESIDENT)

    g_wd = _mm_tn("g_w_down", ff, dh2b, n_ff, TILE_GRAD, TK_GRAD).reshape(4, n_ff, D)
    (dgt, dup), (ra_wd,) = _ffn_gate_up_bwd(dh2b, wd_f, dg_fac, du_fac, TM_WIDE, n_ff, comm=_pair_exchange(g_wd))
    pb_wd = _pair_sum("w_ffn_down", g_wd, ra_wd, place)
    g_wg, (rb_wd,) = _mm_tn_slabs("g_w_gate", u2, dgt, n_ff, TILE_GRAD, TK_GRAD, comm=_chip_exchange(pb_wd))
    slot_wd = _chip_sum("w_ffn_down", g_wd, ra_wd, rb_wd, place)
    g_wu, (full_wd, ra_wg) = _mm_tn_slabs("g_w_up", u2, dup, n_ff, TILE_GRAD, TK_GRAD,
                                          comm=_merge([_pair_share(slot_wd), _pair_exchange(g_wg)]))
    finish_big("w_ffn_down", full_wd)
    pb_wg = _pair_sum("w_ffn_gate", g_wg, ra_wg, place)
    du2, (rb_wg,) = _mm_nt_stacked("d_u2_gate", dgt, wg_f, TM, comm=_chip_exchange(pb_wg))
    du2, (ra_wu,) = _mm_nt_stacked("d_u2_up", dup, wu_f, TM, res=du2, comm=_pair_exchange(g_wu))
    slot_wg = _chip_sum("w_ffn_gate", g_wg, ra_wg, rb_wg, place)
    pb_wu = _pair_sum("w_ffn_up", g_wu, ra_wu, place)
    (dh1, dh1b, dy, d_ln2), (full_wg,) = _rms_bwd_dy(h1, ln2_w, du2, dh2, w_out_f, TM_RESIDENT,
                                                     comm=_pair_share(slot_wg))
    finish_big("w_ffn_gate", full_wg)
    g_wout = _mm_tn("g_w_out", y, dh1b, TILE_GRAD, TILE_GRAD, TK_GRAD).reshape(4, d_mix // 4, D)
    (dproj, d_cw, d_cb, d_wa, d_ba, d_wx, d_bx, d_lam), (rb_wu, ra_wout) = _lru_bwd(
        proj, hs, kept, dy, conv_full, wab, wxb, lru_lambda, LRU_TILE,
        comm=_merge([_chip_exchange(pb_wu), _pair_exchange(g_wout)]))
    slot_wu = _chip_sum("w_ffn_up", g_wu, ra_wu, rb_wu, place)
    pb_wout = _pair_sum("w_out", g_wout, ra_wout, place)
    (dproj, d_gn), (full_wu, rb_wout) = _ret_bwd(proj, rprev, dy, dproj, tables, ret_gn_w, RET_BLOCK,
                                                 comm=_merge([_pair_share(slot_wu), _chip_exchange(pb_wout)]))
    finish_big("w_ffn_up", full_wu)
    slot_wout = _chip_sum("w_out", g_wout, ra_wout, rb_wout, place)
    small = dict(conv_w=d_cw, conv_b=d_cb, gate_a_w=d_wa, gate_a_b=d_ba, gate_x_w=d_wx, gate_x_b=d_bx, lru_lambda=d_lam,
                 ret_gn_w=d_gn, ln2_w=d_ln2, final_norm_w=d_fw)
    packed, offs = _pack([small[n] for n in SMALL[1:]] + [loss])
    g_win, (full_wout, got_small) = _mm_tn("g_w_in", u1, dproj, TILE_GRAD, None, TK_GRAD, stacked_cols=n_in,
                                           comm=_merge([_pair_share(slot_wout), _gather_small(packed)]))
    finish_big("w_out", full_wout)
    (ra_win,) = _comm_call("rs_pair_w_in", [_pair_exchange(g_win)])
    pb_win = _pair_sum("w_in", g_win, ra_win, place)
    du1, (rb_win,) = _mm_nt_stacked("d_u1", dproj, w_in_st, TM, comm=_chip_exchange(pb_win))
    slot_win = _chip_sum("w_in", g_win, ra_win, rb_win, place)
    gx, d_ln1 = _rms_bwd("rms1_bwd", xs, ln1_w, du1, dh1, TM)
    packed1, _ = _pack([d_ln1])
    full_win, got_ln1 = _comm_call("reduce_tail", [_pair_share(slot_win), _gather_small(packed1)])
    finish_big("w_in", full_win)

    red = _unpack(all_sum(got_small, packed), offs, [small[n].shape for n in SMALL[1:]] + [(1, LANES)])
    g = dict(zip(SMALL[1:], red[:-1]))
    g["ln1_w"] = all_sum(got_ln1, packed1)[:-(-D // LANES)].reshape(1, D)
    loss_out = red[-1][0, 0]
    g["conv_w"] = lax.dynamic_slice(g["conv_w"], (0, chip * cw_cols), (CONV_W, cw_cols))
    packs = [_pack([t[n] for n in SMALL])[0] for t in (w, m, v)]
    gp, offs2 = _pack([g[n] for n in SMALL])
    outs = _adamw_call("adamw_small", packs[0], gp, packs[1], packs[2])
    shapes = [w[n].shape for n in SMALL]
    for dst, arr in zip((delta, new_m, new_v), outs):
        dst.update(zip(SMALL, _unpack(arr, offs2, shapes)))
    for n in SMALL:
        grad[n] = g[n].reshape(w[n].shape)

    return (loss_out, gx.reshape(x.shape), *[grad[n] for n in WEIGHTS], *[delta[n] for n in WEIGHTS],
            *[new_m[n] for n in WEIGHTS], *[new_v[n] for n in WEIGHTS])
```

```python
import functools

import jax
import jax.numpy as jnp
from jax import lax
from jax.experimental import pallas as pl
from jax.experimental.pallas import tpu as pltpu

F32 = jnp.float32
BF16 = jnp.bfloat16
MESH = pl.DeviceIdType.MESH

EPS = 1e-6
LRU_C = 8.0
ROPE_BASE = 10000.0
CHUNK = 128
HEAD_DIM = 128
CONV_W = 4
ADAM_LR = 0.001
ADAM_B1 = 0.9
ADAM_B2 = 0.999
ADAM_EPS = 1e-08
ADAM_WD = 0.01
ADAM_STEP = 10

V7X_VMEM_BYTES = 64 * 1024 * 1024
VMEM_LIMIT = V7X_VMEM_BYTES - 8 * 1024 * 1024
LANES = 128
SUBLANES_16BIT = 16

TM = 512
TM_WIDE = 1024
TM_RESIDENT = 256
TK_GRAD = 2048
TILE_GRAD = 1024
LRU_TILE = 256
RET_BLOCK = 512

NN = (((1,), (0,)), ((), ()))
NT = (((1,), (1,)), ((), ()))
TN = (((0,), (0,)), ((), ()))


def _dot(a, b, dims=NN):
    return lax.dot_general(a, b, dims, preferred_element_type=F32)


def _tile(n, pref, mult=SUBLANES_16BIT):
    best = None
    t = mult
    while t <= min(n, pref):
        if n % t == 0:
            best = t
        t += mult
    return best if best is not None else n


def _row_tile(rows, cols, budget_bytes=4 * 1024 * 1024):
    return _tile(rows, max(SUBLANES_16BIT, budget_bytes // (cols * 4)))


def _params(sem):
    return pltpu.CompilerParams(dimension_semantics=sem, vmem_limit_bytes=VMEM_LIMIT)


HBM_SPEC = pl.BlockSpec(memory_space=pl.ANY)


class _Comm:
    def __init__(self, ins, outs, aliases, n_sem, start, finish):
        self.ins, self.outs, self.aliases, self.n_sem, self.start, self.finish = ins, outs, aliases, n_sem, start, finish


def _merge(tasks):
    ins, outs, aliases, plans, n_sem = [], [], {}, [], 0
    for t in tasks:
        i0, o0 = len(ins), len(outs)
        plans.append((t, i0, o0, n_sem))
        ins += t.ins
        outs += t.outs
        aliases.update({i0 + a: o0 + b for a, b in t.aliases.items()})
        n_sem += t.n_sem

    def run(which):
        def go(cins, couts, ssem, rsem, base):
            for t, i0, o0, s0 in plans:
                getattr(t, which)(cins[i0:i0 + len(t.ins)], couts[o0:o0 + len(t.outs)], ssem, rsem, base + s0)
        return go

    return _Comm(ins, outs, aliases, n_sem, run("start"), run("finish"))


def _pcall(body, *, name, grid, in_specs, out_specs, out_shape, operands, scratch_shapes=(), aliases=None, comm=None):
    n_in, n_out, n_scr = len(operands), len(out_shape), len(scratch_shapes)
    aliases = dict(aliases or {})
    params = _params(("arbitrary",) * len(grid))
    if comm is None:
        return pl.pallas_call(body, grid=grid, in_specs=list(in_specs), out_specs=list(out_specs), out_shape=list(out_shape),
                              scratch_shapes=list(scratch_shapes), input_output_aliases=aliases, name=name,
                              compiler_params=params)(*operands)
    nci, nco = len(comm.ins), len(comm.outs)

    def wrapped(*refs):
        ins, cins = refs[:n_in], refs[n_in:n_in + nci]
        o0 = n_in + nci
        outs, couts = refs[o0:o0 + n_out], refs[o0 + n_out:o0 + n_out + nco]
        s0 = o0 + n_out + nco
        scr, (ssem, rsem) = refs[s0:s0 + n_scr], refs[s0 + n_scr:]
        ids = [pl.program_id(a) for a in range(len(grid))]
        first = functools.reduce(jnp.logical_and, [i == 0 for i in ids])
        last = functools.reduce(jnp.logical_and, [i == g - 1 for i, g in zip(ids, grid)])

        @pl.when(first)
        def _():
            comm.start(cins, couts, ssem, rsem, 0)

        body(*ins, *outs, *scr)

        @pl.when(last)
        def _():
            comm.finish(cins, couts, ssem, rsem, 0)

    aliases.update({n_in + a: n_out + b for a, b in comm.aliases.items()})
    res = pl.pallas_call(
        wrapped, grid=grid, in_specs=list(in_specs) + [HBM_SPEC] * nci, out_specs=list(out_specs) + [HBM_SPEC] * nco,
        out_shape=list(out_shape) + list(comm.outs),
        scratch_shapes=list(scratch_shapes) + [pltpu.SemaphoreType.DMA((comm.n_sem,)), pltpu.SemaphoreType.DMA((comm.n_sem,))],
        input_output_aliases=aliases, name=name, compiler_params=params)(*operands, *comm.ins)
    return res[:n_out], res[n_out:]


def _ew(name, fn, ins, outs, grid, sp=None):
    n_in = len(ins)

    def body(*refs):
        if sp is not None:
            refs = refs[1:]
        vals = [r[...] for r in refs[:n_in]]
        res = fn(*vals)
        for o_ref, v in zip(refs[n_in:], res):
            o_ref[...] = v.astype(o_ref.dtype)

    in_specs = [s for _, s in ins]
    out_specs = [s for _, s in outs]
    out_shape = [s for s, _ in outs]
    sem = ("arbitrary",) * len(grid)
    if sp is None:
        return pl.pallas_call(body, grid=grid, in_specs=in_specs, out_specs=out_specs, out_shape=out_shape,
                              name=name, compiler_params=_params(sem))(*[a for a, _ in ins])
    gs = pltpu.PrefetchScalarGridSpec(num_scalar_prefetch=1, grid=grid, in_specs=in_specs, out_specs=out_specs)
    return pl.pallas_call(body, grid_spec=gs, out_shape=out_shape, name=name,
                          compiler_params=_params(sem))(sp, *[a for a, _ in ins])


def _matmul(name, pairs, dims, grid, out_shape, out_spec, acc_shape, comm=None, transpose_out=False):
    n = len(pairs)
    nk = grid[2]

    def body(*refs):
        ab = refs[:2 * n]
        o_ref = refs[2 * n]
        acc_ref = refs[2 * n + 1] if nk > 1 else None

        def partial():
            t = None
            for p in range(n):
                d = _dot(ab[2 * p][...], ab[2 * p + 1][...], dims)
                t = d if t is None else t + d
            return t

        def finish(t):
            o_ref[...] = (t.T if transpose_out else t).astype(o_ref.dtype)

        if nk == 1:
            finish(partial())
        else:
            k = pl.program_id(2)

            @pl.when(k == 0)
            def _():
                acc_ref[...] = partial()

            @pl.when(k > 0)
            def _():
                acc_ref[...] += partial()

            @pl.when(k == nk - 1)
            def _():
                finish(acc_ref[...])

    operands, in_specs = [], []
    for a, a_spec, b, b_spec in pairs:
        operands += [a, b]
        in_specs += [a_spec, b_spec]
    scratch = [pltpu.VMEM(acc_shape, F32)] if nk > 1 else []
    res = _pcall(body, name=name, grid=grid, in_specs=in_specs, out_specs=[out_spec], out_shape=[out_shape],
                 operands=operands, scratch_shapes=scratch, comm=comm)
    return res[0] if comm is None else (res[0][0], res[1])


def _mm_nn_stacked(name, a, b_st, out_dtype, tm, comm=None):
    M, K = a.shape
    J, _, Nj = b_st.shape
    tm = _tile(M, tm)
    return _matmul(
        name, [(a, pl.BlockSpec((tm, K), lambda j, i, k: (i, 0)), b_st, pl.BlockSpec((None, K, Nj), lambda j, i, k: (j, 0, 0)))],
        NN, (J, M // tm, 1), jax.ShapeDtypeStruct((M, J * Nj), out_dtype), pl.BlockSpec((tm, Nj), lambda j, i, k: (i, j)), None,
        comm=comm)


def _mm_nt_stacked(name, a, b_st, tm, res=None, comm=None):
    M, K = a.shape
    N = b_st.shape[-2]
    tm = _tile(M, tm)

    def body(a_ref, b_ref, *rest):
        o_ref = rest[-1]
        t = None if res is None else rest[0][...]
        if len(b_ref.shape) == 2:
            d = _dot(a_ref[...], b_ref[...], NT)
            t = d if t is None else t + d
        else:
            nj = b_ref.shape[2]
            for s in range(b_ref.shape[0]):
                d = _dot(a_ref[:, s * nj:(s + 1) * nj], b_ref[s], NT)
                t = d if t is None else t + d
        o_ref[...] = t

    row = pl.BlockSpec((tm, N), lambda i: (i, 0))
    out = _pcall(body, name=name, grid=(M // tm,),
                 in_specs=[pl.BlockSpec((tm, K), lambda i: (i, 0)),
                           pl.BlockSpec(b_st.shape, lambda i: (0,) * b_st.ndim, pipeline_mode=pl.Buffered(1))]
                 + [row] * (res is not None),
                 out_specs=[row], out_shape=[jax.ShapeDtypeStruct((M, N), F32)],
                 operands=[a, b_st] + [res] * (res is not None), comm=comm)
    return out[0] if comm is None else (out[0][0], out[1])


MXU_COLUMNS = 256


def _col_blocks(n):
    return [slice(s, min(s + MXU_COLUMNS, n)) for s in range(0, n, MXU_COLUMNS)]


def _ffn_gate_up(u2, wg, wu, n_slab, tm, comm=None):
    S, D = u2.shape
    F = wg.shape[1]
    tm = _tile(S, tm)
    tn = 2 * n_slab if (n_slab % MXU_COLUMNS and (F // n_slab) % 2 == 0) else n_slab

    def body(a_ref, wg_ref, wu_ref, dg_ref, du_ref, ff_ref):
        a = a_ref[...]
        blocks = _col_blocks(tn)
        ahead = (_dot(a, wg_ref[:, blocks[0]]), _dot(a, wu_ref[:, blocks[0]]))
        for j, cols in enumerate(blocks):
            g, u = ahead
            if j + 1 < len(blocks):
                ahead = (_dot(a, wg_ref[:, blocks[j + 1]]), _dot(a, wu_ref[:, blocks[j + 1]]))
            sg = jax.nn.sigmoid(g)
            silu = g * sg
            dg_ref[:, cols] = (u * (sg * (1.0 + g * (1.0 - sg)))).astype(BF16)
            du_ref[:, cols] = silu.astype(BF16)
            ff_ref[:, cols] = (silu * u).astype(BF16)

    w_spec = pl.BlockSpec((D, tn), lambda j, i: (0, j), pipeline_mode=pl.Buffered(1))
    o_spec = pl.BlockSpec((tm, tn), lambda j, i: (i, j))
    o = jax.ShapeDtypeStruct((S, F), BF16)
    return _pcall(body, name="ffn_gate_up", grid=(F // tn, S // tm),
                  in_specs=[pl.BlockSpec((tm, D), lambda j, i: (i, 0)), w_spec, w_spec],
                  out_specs=[o_spec, o_spec, o_spec], out_shape=[o, o, o], operands=[u2, wg, wu], comm=comm)


def _ffn_gate_up_bwd(dh2b, wd, dg_fac, du_fac, tm, tn, comm=None):
    S, D = dh2b.shape
    F = wd.shape[0]
    tm, tn = _tile(S, tm), _tile(F, tn, LANES)

    def body(a_ref, wd_ref, dg_ref, du_ref, dgt_ref, dup_ref):
        a = a_ref[...]
        for cols in _col_blocks(tn):
            d = _dot(a, wd_ref[cols, :], NT)
            dgt_ref[:, cols] = (d * dg_ref[:, cols].astype(F32)).astype(BF16)
            dup_ref[:, cols] = (d * du_ref[:, cols].astype(F32)).astype(BF16)

    blk = pl.BlockSpec((tm, tn), lambda j, i: (i, j))
    o = jax.ShapeDtypeStruct((S, F), BF16)
    return _pcall(body, name="ffn_gate_up_bwd", grid=(F // tn, S // tm),
                  in_specs=[pl.BlockSpec((tm, D), lambda j, i: (i, 0)), pl.BlockSpec((tn, D), lambda j, i: (j, 0)), blk, blk],
                  out_specs=[blk, blk], out_shape=[o, o], operands=[dh2b, wd, dg_fac, du_fac], comm=comm)


def _mm_tn(name, a, b, tmo, tn, tk, stacked_cols=None, comm=None):
    S, Mo = a.shape
    N = b.shape[1]
    tmo, tk = _tile(Mo, tmo, LANES), _tile(S, tk)
    if stacked_cols is None:
        tn = _tile(N, tn, LANES)
        out_shape = jax.ShapeDtypeStruct((Mo, N), F32)
        out_spec = pl.BlockSpec((tmo, tn), lambda i, j, k: (i, j))
    else:
        tn = stacked_cols
        out_shape = jax.ShapeDtypeStruct((N // tn, Mo, tn), F32)
        out_spec = pl.BlockSpec((None, tmo, tn), lambda i, j, k: (j, i, 0))
    return _matmul(
        name, [(a, pl.BlockSpec((tk, tmo), lambda i, j, k: (k, i)), b, pl.BlockSpec((tk, tn), lambda i, j, k: (k, j)))],
        TN, (Mo // tmo, N // tn, S // tk), out_shape, out_spec, (tmo, tn), comm=comm)


def _mm_tn_slabs(name, a, b, n_slab, tmo, tk, comm=None):
    S, Mo = a.shape
    J = b.shape[1] // n_slab
    tmo, tk = _tile(Mo, tmo, LANES), _tile(S, tk)
    return _matmul(
        name, [(b, pl.BlockSpec((tk, n_slab), lambda j, i, k: (k, j)), a, pl.BlockSpec((tk, tmo), lambda j, i, k: (k, i)))],
        TN, (J, Mo // tmo, S // tk), jax.ShapeDtypeStruct((J, Mo, n_slab), F32),
        pl.BlockSpec((None, tmo, n_slab), lambda j, i, k: (j, i, 0)), (n_slab, tmo), comm=comm, transpose_out=True)


def _rms_fwd(name, x, w, tm, comm=None):
    S, D = x.shape
    tm = _tile(S, tm)

    def body(x_ref, w_ref, o_ref):
        xv = x_ref[...]
        r = lax.rsqrt(jnp.mean(xv * xv, axis=-1, keepdims=True) + EPS)
        o_ref[...] = ((xv * r) * w_ref[...]).astype(BF16)

    row = pl.BlockSpec((tm, D), lambda i: (i, 0))
    return _pcall(body, name=name, grid=(S // tm,), in_specs=[row, pl.BlockSpec((1, D), lambda i: (0, 0))], out_specs=[row],
                  out_shape=[jax.ShapeDtypeStruct((S, D), BF16)], operands=[x, w], comm=comm)


def _rms_bwd(name, x, w, dy, dres, tm, comm=None):
    S, D = x.shape
    tm = _tile(S, tm)

    def body(x_ref, w_ref, dy_ref, dres_ref, dx_ref, dw_ref):
        i = pl.program_id(0)

        @pl.when(i == 0)
        def _():
            dw_ref[...] = jnp.zeros_like(dw_ref)

        xv = x_ref[...]
        r = lax.rsqrt(jnp.mean(xv * xv, axis=-1, keepdims=True) + EPS)
        nv = xv * r
        dyv = dy_ref[...]
        dn = dyv * w_ref[...]
        dw_ref[...] += jnp.sum(dyv * nv, axis=0, keepdims=True)
        dx = dres_ref[...] + r * (dn - nv * jnp.mean(dn * nv, axis=-1, keepdims=True))
        dx_ref[...] = dx

    row = pl.BlockSpec((tm, D), lambda i: (i, 0))
    vec = pl.BlockSpec((1, D), lambda i: (0, 0))
    return _pcall(body, name=name, grid=(S // tm,), in_specs=[row, vec, row, row], out_specs=[row, vec],
                  out_shape=[jax.ShapeDtypeStruct((S, D), F32), jax.ShapeDtypeStruct((1, D), F32)],
                  operands=[x, w, dy, dres], comm=comm)


def _rms_bwd_dy(h1, w, du2, dh2, w_out, tm, comm=None):
    S, D = h1.shape
    d_mix = w_out.shape[0]
    tm = _tile(S, tm)

    def body(x_ref, w_ref, dy_ref, dres_ref, wo_ref, dx_ref, dxb_ref, out_ref, dw_ref):
        i = pl.program_id(0)

        @pl.when(i == 0)
        def _():
            dw_ref[...] = jnp.zeros_like(dw_ref)

        xv = x_ref[...]
        r = lax.rsqrt(jnp.mean(xv * xv, axis=-1, keepdims=True) + EPS)
        nv = xv * r
        dyv = dy_ref[...]
        dn = dyv * w_ref[...]
        dw_ref[...] += jnp.sum(dyv * nv, axis=0, keepdims=True)
        dx = dres_ref[...] + r * (dn - nv * jnp.mean(dn * nv, axis=-1, keepdims=True))
        dx_ref[...] = dx
        dxb = dx.astype(BF16)
        dxb_ref[...] = dxb
        out_ref[...] = _dot(dxb, wo_ref[...], NT)

    row = pl.BlockSpec((tm, D), lambda i: (i, 0))
    vec = pl.BlockSpec((1, D), lambda i: (0, 0))
    return _pcall(
        body, name="rms2_bwd_dy", grid=(S // tm,),
        in_specs=[row, vec, row, row, pl.BlockSpec((d_mix, D), lambda i: (0, 0), pipeline_mode=pl.Buffered(1))],
        out_specs=[row, row, pl.BlockSpec((tm, d_mix), lambda i: (i, 0)), vec],
        out_shape=[jax.ShapeDtypeStruct((S, D), F32), jax.ShapeDtypeStruct((S, D), BF16),
                   jax.ShapeDtypeStruct((S, d_mix), F32), jax.ShapeDtypeStruct((1, D), F32)],
        operands=[h1, w, du2, dh2, w_out], comm=comm)


def _out_proj_rms(y, w_out, x, ln_w, tm, comm=None):
    S, K = y.shape
    D = w_out.shape[1]
    tm = _tile(S, tm)

    def body(a_ref, w_ref, x_ref, lw_ref, h_ref, u_ref):
        hv = _dot(a_ref[...], w_ref[...]) + x_ref[...]
        h_ref[...] = hv
        r = lax.rsqrt(jnp.mean(hv * hv, axis=-1, keepdims=True) + EPS)
        u_ref[...] = ((hv * r) * lw_ref[...]).astype(BF16)

    row = pl.BlockSpec((tm, D), lambda i: (i, 0))
    return _pcall(
        body, name="out_proj", grid=(S // tm,),
        in_specs=[pl.BlockSpec((tm, K), lambda i: (i, 0)),
                  pl.BlockSpec((K, D), lambda i: (0, 0), pipeline_mode=pl.Buffered(1)), row,
                  pl.BlockSpec((1, D), lambda i: (0, 0))],
        out_specs=[row, row], out_shape=[jax.ShapeDtypeStruct((S, D), F32), jax.ShapeDtypeStruct((S, D), BF16)],
        operands=[y, w_out, x, ln_w], comm=comm)


def _ffn_down_loss(ff, wd, h1, tgt, fw, tm):
    S, K = ff.shape
    D = wd.shape[1]
    tm = _tile(S, tm)

    def body(a_ref, wd_ref, h1_ref, t_ref, w_ref, dh_ref, dhb_ref, dw_ref, loss_ref):
        i = pl.program_id(0)

        @pl.when(i == 0)
        def _():
            dw_ref[...] = jnp.zeros_like(dw_ref)
            loss_ref[...] = jnp.zeros_like(loss_ref)

        hv = _dot(a_ref[...], wd_ref[...]) + h1_ref[...]
        wv = w_ref[...]
        r = lax.rsqrt(jnp.mean(hv * hv, axis=-1, keepdims=True) + EPS)
        nv = hv * r
        err = nv * wv - t_ref[...]
        row_loss = jnp.mean(err * err, axis=-1, keepdims=True)
        loss_ref[...] += 0.5 * jnp.sum(row_loss, axis=0, keepdims=True)
        dyo = err * (1.0 / D)
        dn = dyo * wv
        dw_ref[...] += jnp.sum(dyo * nv, axis=0, keepdims=True)
        dh = r * (dn - nv * jnp.mean(dn * nv, axis=-1, keepdims=True))
        dh_ref[...] = dh
        dhb_ref[...] = dh.astype(BF16)

    row = pl.BlockSpec((tm, D), lambda i: (i, 0))
    vec = pl.BlockSpec((1, D), lambda i: (0, 0))
    return _pcall(
        body, name="ffn_down_loss", grid=(S // tm,),
        in_specs=[pl.BlockSpec((tm, K), lambda i: (i, 0)),
                  pl.BlockSpec((K, D), lambda i: (0, 0), pipeline_mode=pl.Buffered(1)), row, row, vec],
        out_specs=[row, row, vec, pl.BlockSpec((1, LANES), lambda i: (0, 0))],
        out_shape=[jax.ShapeDtypeStruct((S, D), F32), jax.ShapeDtypeStruct((S, D), BF16),
                   jax.ShapeDtypeStruct((1, D), F32), jax.ShapeDtypeStruct((1, LANES), F32)],
        operands=[ff, wd, h1, tgt, fw])


def _shift_down(x, d, head8):
    r = pltpu.roll(x, d, 0)
    rh = pltpu.roll(head8, d, 0)
    row8 = lax.broadcasted_iota(jnp.int32, head8.shape, 0)
    top = jnp.where(row8 < d, rh, r[0:8])
    return jnp.concatenate([top, r[8:]], axis=0)


def _shift_up(x, d, tail8):
    n = x.shape[0]
    r = pltpu.roll(x, n - d, 0)
    rt = pltpu.roll(tail8, 8 - d, 0)
    row8 = lax.broadcasted_iota(jnp.int32, tail8.shape, 0)
    bot = jnp.where(row8 + d >= 8, rt, r[n - 8:n])
    return jnp.concatenate([r[:n - 8], bot], axis=0)


def _roll_in_groups(x, d):
    n, c = x.shape
    return pltpu.roll(x.reshape(n // 8, 8, c), d, 1).reshape(n, c)


def _log_sigmoid(lam):
    z = jnp.exp(-jnp.abs(lam))
    u = 1.0 + z
    log1p = jnp.where(u == 1.0, z, jnp.log(u) * (z / jnp.where(u == 1.0, 1.0, u - 1.0)))
    return jnp.minimum(lam, 0.0) - log1p


def _neg_expm1(z, exp_z):
    series = -z * (1.0 + z * (0.5 + z * (1.0 / 6.0)))
    return jnp.where(z > -0.02, series, 1.0 - exp_z)


_GELU_C = 0.7978845608028654


def _gelu(x):
    t = jnp.tanh(_GELU_C * (x + 0.044715 * (x * x * x)))
    return x * (0.5 * (1.0 + t)), t


def _gelu_grad(x, t):
    return 0.5 * (1.0 + t) + 0.5 * x * (1.0 - t * t) * (_GELU_C * (1.0 + 3.0 * 0.044715 * (x * x)))


def _lx_shifts(lx, head8):
    return [lx] + [_shift_down(lx, d, head8) for d in (1, 2, 3)]


def _lru_gates(lx, head8, cw, cb, wa_ref, ba, wx_ref, bx, ls):
    nb = wa_ref.shape[0]
    sh = _lx_shifts(lx, head8)
    cx = cb + sh[3] * cw[0:1]
    cx = cx + sh[2] * cw[1:2]
    cx = cx + sh[1] * cw[2:3]
    cx = cx + sh[0] * cw[3:4]
    cxb = cx.astype(BF16)
    ra = jnp.concatenate([_dot(cxb[:, n * HEAD_DIM:(n + 1) * HEAD_DIM], wa_ref[n]) for n in range(nb)], axis=1) + ba
    ia = jnp.concatenate([_dot(cxb[:, n * HEAD_DIM:(n + 1) * HEAD_DIM], wx_ref[n]) for n in range(nb)], axis=1) + bx
    r = jax.nn.sigmoid(ra)
    ig = jax.nn.sigmoid(ia)
    log_a = LRU_C * r * ls
    a = jnp.exp(log_a)
    return cx, r, ig, a, jnp.sqrt(_neg_expm1(2.0 * log_a, a * a))


def _lru_specs(tl, DL):
    nb = DL // HEAD_DIM
    vec = pl.BlockSpec((1, DL), lambda i: (0, 0))
    return [pl.BlockSpec((CONV_W, DL), lambda i: (0, 0)), vec,
            pl.BlockSpec((nb, HEAD_DIM, HEAD_DIM), lambda i: (0, 0, 0)), vec,
            pl.BlockSpec((nb, HEAD_DIM, HEAD_DIM), lambda i: (0, 0, 0)), vec, vec]


def _lru_fwd(proj, cw, cb, wa, ba, wx, bx, lam, tl, d_mix, comm=None):
    S = proj.shape[0]
    DL = cb.shape[1]
    tl = _tile(S, tl)

    def body(lx_ref, lg_ref, cw_ref, cb_ref, wa_ref, ba_ref, wx_ref, bx_ref, lam_ref, h_ref, kept_ref, y_ref,
             prev8, hc, a_s, b_s):
        i = pl.program_id(0)

        @pl.when(i == 0)
        def _():
            prev8[...] = jnp.zeros_like(prev8)
            hc[...] = jnp.zeros_like(hc)

        lx = lx_ref[...]
        ls = _log_sigmoid(lam_ref[...])
        kept = _lru_gates(lx, prev8[...], cw_ref[...], cb_ref[...], wa_ref, ba_ref[...], wx_ref, bx_ref[...], ls)
        for n, val in enumerate(kept):
            kept_ref[:, n * DL:(n + 1) * DL] = val
        cx, _, ig, a, mult = kept
        b = mult * (ig * cx)
        row = lax.broadcasted_iota(jnp.int32, a.shape, 0) & 7
        for d in (1, 2, 4):
            a_sh = _roll_in_groups(a, d)
            b_sh = _roll_in_groups(b, d)
            m = row >= d
            b = jnp.where(m, a * b_sh + b, b)
            a = jnp.where(m, a * a_sh, a)
        a_s[...] = a
        b_s[...] = b

        def step(g, hprev):
            sl = pl.ds(pl.multiple_of(g * 8, 8), 8)
            hh = a_s[sl, :] * hprev + b_s[sl, :]
            h_ref[sl, :] = hh
            return hh[7:8, :]

        hc[0:1, :] = lax.fori_loop(0, tl // 8, step, hc[0:1, :])
        prev8[...] = lx[tl - 8:tl]
        g, _ = _gelu(lg_ref[...])
        y_ref[...] = (h_ref[...] * g).astype(BF16)

    return _pcall(
        body, name="lru_fwd", grid=(S // tl,),
        in_specs=[pl.BlockSpec((tl, DL), lambda i: (i, 0)), pl.BlockSpec((tl, DL), lambda i: (i, 1))] + _lru_specs(tl, DL),
        out_specs=[pl.BlockSpec((tl, DL), lambda i: (i, 0)), pl.BlockSpec((tl, 5 * DL), lambda i: (i, 0)),
                   pl.BlockSpec((tl, DL), lambda i: (i, 0))],
        out_shape=[jax.ShapeDtypeStruct((S, DL), F32), jax.ShapeDtypeStruct((S, 5 * DL), F32),
                   jax.ShapeDtypeStruct((S, d_mix), BF16)],
        scratch_shapes=[pltpu.VMEM((8, DL), F32), pltpu.VMEM((8, DL), F32), pltpu.VMEM((tl, DL), F32), pltpu.VMEM((tl, DL), F32)],
        operands=[proj, proj, cw, cb, wa, ba, wx, bx, lam], comm=comm)


def _lru_bwd(proj, h, kept, dy, cw, wa, wx, lam, tl, comm=None):
    S = proj.shape[0]
    DL = lam.shape[1]
    nb = DL // HEAD_DIM
    tl = _tile(S, tl)
    nt = S // tl
    ng = tl // 8
    t8 = tl // 8

    def body(lx_ref, lxp_ref, lg_ref, h_ref, hp_ref, kept_ref, dy_ref, cw_ref, wa_ref, wx_ref, lam_ref,
             dlxg_ref, dcw_ref, dcb_ref, dwa_ref, dba_ref, dwx_ref, dbx_ref, dlam_ref,
             a_next, g_carry, dcx_next, an_s, dh_s, g_s):
        i = pl.program_id(0)

        @pl.when(i == 0)
        def _():
            for ref in (dcw_ref, dcb_ref, dwa_ref, dba_ref, dwx_ref, dbx_ref, dlam_ref, a_next, g_carry, dcx_next):
                ref[...] = jnp.zeros_like(ref)

        first = i == nt - 1
        hv = h_ref[...]
        lg = lg_ref[...]
        dyv = dy_ref[...]
        hhead8 = jnp.where(first, 0.0, hp_ref[...])
        lamv = lam_ref[...]
        ls = _log_sigmoid(lamv)
        cwv = cw_ref[...]
        sh = _lx_shifts(lx_ref[...], jnp.where(first, 0.0, lxp_ref[...]))
        cx, r, ig, a, mult = (kept_ref[:, n * DL:(n + 1) * DL] for n in range(5))
        cxb = cx.astype(BF16)
        hprev = _shift_down(hv, 1, hhead8)
        g, t = _gelu(lg)
        dlg = dyv * hv * _gelu_grad(lg, t)
        dh = dyv * g
        an = _shift_up(a, 1, a_next[...])
        row = lax.broadcasted_iota(jnp.int32, a.shape, 0) & 7
        for d in (1, 2, 4):
            an_sh = _roll_in_groups(an, 8 - d)
            dh_sh = _roll_in_groups(dh, 8 - d)
            m = row + d < 8
            dh = jnp.where(m, an * dh_sh + dh, dh)
            an = jnp.where(m, an * an_sh, an)
        an_s[...] = an
        dh_s[...] = dh

        def step(k, gc):
            sl = pl.ds(pl.multiple_of((ng - 1 - k) * 8, 8), 8)
            gg = an_s[sl, :] * gc + dh_s[sl, :]
            g_s[sl, :] = gg
            return gg[0:1, :]

        g_carry[0:1, :] = lax.fori_loop(0, ng, step, g_carry[0:1, :])
        a_next[...] = a[0:8]
        G = g_s[...]
        da = G * hprev
        icx = ig * cx
        dmult = G * icx
        dicx = G * mult
        di = dicx * cx
        dcx = dicx * ig
        dlog = da * a - dmult * ((a * a) * lax.rsqrt(mult * mult))
        dr = dlog * (LRU_C * ls)
        dlam_ref[...] += jnp.sum(dlog * (LRU_C * r), axis=0, keepdims=True)
        dra = dr * r * (1.0 - r)
        dia = di * ig * (1.0 - ig)
        dba_ref[...] += jnp.sum(dra, axis=0, keepdims=True)
        dbx_ref[...] += jnp.sum(dia, axis=0, keepdims=True)
        drab = dra.astype(BF16)
        diab = dia.astype(BF16)
        back = []
        for n in range(nb):
            cs = slice(n * HEAD_DIM, (n + 1) * HEAD_DIM)
            dwa_ref[n] += _dot(cxb[:, cs], drab[:, cs], TN)
            dwx_ref[n] += _dot(cxb[:, cs], diab[:, cs], TN)
            back.append(_dot(drab[:, cs], wa_ref[n], NT) + _dot(diab[:, cs], wx_ref[n], NT))
        dcx = dcx + jnp.concatenate(back, axis=1)
        dcb_ref[...] += jnp.sum(dcx, axis=0, keepdims=True)
        for tap in range(CONV_W):
            dcw_ref[tap:tap + 1, :] += jnp.sum(dcx * sh[CONV_W - 1 - tap], axis=0, keepdims=True)
        tail = dcx_next[...]
        dlx = dcx * cwv[3:4]
        for d in (1, 2, 3):
            dlx = dlx + _shift_up(dcx, d, tail) * cwv[3 - d:4 - d]
        dcx_next[...] = dcx[0:8]
        dlxg_ref[:, 0:DL] = dlx.astype(BF16)
        dlxg_ref[:, DL:2 * DL] = dlg.astype(BF16)

        @pl.when(i == nt - 1)
        def _():
            dlam_ref[...] = dlam_ref[...] * (1.0 - jax.nn.sigmoid(lamv))

    rev = lambda i: nt - 1 - i
    prev8_map = lambda i: (jnp.maximum((nt - 1 - i) * t8 - 1, 0), 0)
    vec = pl.BlockSpec((1, DL), lambda i: (0, 0))
    mat = pl.BlockSpec((nb, HEAD_DIM, HEAD_DIM), lambda i: (0, 0, 0))
    return _pcall(
        body, name="lru_bwd", grid=(nt,), operands=[proj, proj, proj, h, h, kept, dy, cw, wa, wx, lam], comm=comm,
        in_specs=[pl.BlockSpec((tl, DL), lambda i: (rev(i), 0)), pl.BlockSpec((8, DL), prev8_map),
                  pl.BlockSpec((tl, DL), lambda i: (rev(i), 1)),
                  pl.BlockSpec((tl, DL), lambda i: (rev(i), 0)), pl.BlockSpec((8, DL), prev8_map),
                  pl.BlockSpec((tl, 5 * DL), lambda i: (rev(i), 0)),
                  pl.BlockSpec((tl, DL), lambda i: (rev(i), 0)), pl.BlockSpec((CONV_W, DL), lambda i: (0, 0)), mat, mat, vec],
        out_specs=[pl.BlockSpec((tl, 2 * DL), lambda i: (rev(i), 0)), pl.BlockSpec((CONV_W, DL), lambda i: (0, 0)), vec,
                   mat, vec, mat, vec, vec],
        out_shape=[jax.ShapeDtypeStruct(proj.shape, BF16), jax.ShapeDtypeStruct((CONV_W, DL), F32),
                   jax.ShapeDtypeStruct((1, DL), F32), jax.ShapeDtypeStruct((nb, HEAD_DIM, HEAD_DIM), F32),
                   jax.ShapeDtypeStruct((1, DL), F32), jax.ShapeDtypeStruct((nb, HEAD_DIM, HEAD_DIM), F32),
                   jax.ShapeDtypeStruct((1, DL), F32), jax.ShapeDtypeStruct((1, DL), F32)],
        scratch_shapes=[pltpu.VMEM((8, DL), F32), pltpu.VMEM((8, DL), F32), pltpu.VMEM((8, DL), F32),
                        pltpu.VMEM((tl, DL), F32), pltpu.VMEM((tl, DL), F32), pltpu.VMEM((tl, DL), F32)])


def _ret_tables(S, H):
    pos = jnp.arange(S, dtype=F32)
    inv_freq = ROPE_BASE ** (-jnp.arange(0, HEAD_DIM, 2, dtype=F32) / HEAD_DIM)
    ang = pos[:, None] * inv_freq[None, :]
    cos, sin = jnp.cos(ang), jnp.sin(ang)
    cosf = jnp.concatenate([cos, cos], axis=1)
    sins = jnp.concatenate([-sin, sin], axis=1)
    log_gamma = jnp.log1p(-jnp.exp2(-5.0 - jnp.arange(H, dtype=F32)))
    idx = jnp.arange(CHUNK)
    diff = idx[:, None] - idx[None, :]
    causal = diff >= 0
    decay = jnp.where(causal[None], jnp.exp(log_gamma[:, None, None] * jnp.where(causal, diff, 0)[None].astype(F32)), 0.0)
    zeta = jnp.exp(log_gamma[:, None] * (CHUNK - 1 - idx).astype(F32)[None, :])
    xi = jnp.exp(log_gamma[:, None] * (idx + 1).astype(F32)[None, :])
    gc = jnp.exp(log_gamma * CHUNK)
    lanes = (H, CHUNK, HEAD_DIM)
    return (cosf, sins, decay, jnp.broadcast_to(zeta[:, :, None], lanes), jnp.broadcast_to(xi[:, :, None], lanes),
            jnp.broadcast_to(gc[:, None, None], lanes))


def _rope(t, cos, sin_signed):
    return t * cos + pltpu.roll(t, HEAD_DIM // 2, 1) * sin_signed


def _rope_t(d, cos, sin_signed):
    return d * cos + pltpu.roll(d * sin_signed, HEAD_DIM // 2, 1)


def _ret_const_specs(H, DR):
    full = pl.BlockSpec((H, CHUNK, HEAD_DIM), lambda *_: (0, 0, 0))
    return [full, full, full, full, pl.BlockSpec((1, DR), lambda *_: (0, 0))]


def _ret_fwd(proj, y, tables, gnw, tb, comm=None):
    S = proj.shape[0]
    DR = gnw.shape[1]
    H = DR // HEAD_DIM
    tb = _tile(S, tb, CHUNK)
    nc = tb // CHUNK
    cosf, sins, dm, zeta, xi, gc = tables
    scale = HEAD_DIM ** -0.5

    def body(qk_ref, vg_ref, cos_ref, sin_ref, dm_ref, zeta_ref, xi_ref, gc_ref, gnw_ref, y_in, y_ref, rprev_ref, r_s):
        del y_in
        i = pl.program_id(0)

        @pl.when(i == 0)
        def _():
            r_s[...] = jnp.zeros_like(r_s)

        def chunk(c, carry):
            rows = pl.ds(pl.multiple_of(c * CHUNK, CHUNK), CHUNK)
            cos = cos_ref[rows, :]
            sin = sin_ref[rows, :]
            heads = range(H)
            c0 = [slice(h * HEAD_DIM, (h + 1) * HEAD_DIM) for h in heads]
            c1 = [slice(DR + h * HEAD_DIM, DR + (h + 1) * HEAD_DIM) for h in heads]
            qh = [_rope(qk_ref[rows, c0[h]], cos, sin) for h in heads]
            kh = [_rope(qk_ref[rows, c1[h]], cos, sin) * scale for h in heads]
            vb = [vg_ref[rows, c0[h]].astype(BF16) for h in heads]
            rp = [r_s[h] for h in heads]
            rpb = [rp[h].astype(BF16) for h in heads]
            s = [_dot(qh[h].astype(BF16), kh[h].astype(BF16), NT) for h in heads]
            kv = [_dot((kh[h] * zeta_ref[h]).astype(BF16), vb[h], TN) for h in heads]
            cross = [_dot((qh[h] * xi_ref[h]).astype(BF16), rpb[h]) for h in heads]
            o = [_dot((s[h] * dm_ref[h]).astype(BF16), vb[h]) + cross[h] for h in heads]
            for h in heads:
                rprev_ref[c, h] = rpb[h]
                r_s[h] = rp[h] * gc_ref[h] + kv[h]
                mu = jnp.mean(o[h], axis=-1, keepdims=True)
                oc = o[h] - mu
                var = jnp.mean(oc * oc, axis=-1, keepdims=True)
                on = oc * lax.rsqrt(var + EPS) * gnw_ref[:, c0[h]]
                gate = vg_ref[rows, c1[h]]
                y_ref[rows, c0[h]] = (gate * jax.nn.sigmoid(gate) * on).astype(BF16)
            return carry

        lax.fori_loop(0, nc, chunk, 0)

    return _pcall(
        body, name="ret_fwd", grid=(S // tb,),
        in_specs=[pl.BlockSpec((tb, 2 * DR), lambda i: (i, 1)), pl.BlockSpec((tb, 2 * DR), lambda i: (i, 2)),
                  pl.BlockSpec((tb, HEAD_DIM), lambda i: (i, 0)), pl.BlockSpec((tb, HEAD_DIM), lambda i: (i, 0))]
        + _ret_const_specs(H, DR) + [HBM_SPEC],
        out_specs=[pl.BlockSpec((tb, DR), lambda i: (i, 1)),
                   pl.BlockSpec((nc, H, CHUNK, HEAD_DIM), lambda i: (i, 0, 0, 0))],
        out_shape=[jax.ShapeDtypeStruct(y.shape, BF16), jax.ShapeDtypeStruct((S // CHUNK, H, CHUNK, HEAD_DIM), BF16)],
        scratch_shapes=[pltpu.VMEM((H, CHUNK, HEAD_DIM), F32)], aliases={9: 0},
        operands=[proj, proj, cosf, sins, dm, zeta, xi, gc, gnw, y], comm=comm)


def _ret_bwd(proj, rprev, dy, dproj, tables, gnw, tb, comm=None):
    S = proj.shape[0]
    DR = gnw.shape[1]
    H = DR // HEAD_DIM
    tb = _tile(S, tb, CHUNK)
    nc = tb // CHUNK
    nt = S // tb
    cosf, sins, dm, zeta, xi, gc = tables
    scale = HEAD_DIM ** -0.5

    def body(qk_ref, vg_ref, cos_ref, sin_ref, dm_ref, zeta_ref, xi_ref, gc_ref, gnw_ref, rprev_ref, dy_ref, dp_in,
             dp_ref, dgn_ref, dr_s, dqk_s, dvg_s, out_sems):
        del dp_in
        i = pl.program_id(0)
        slot = i % 2

        def out_copies(step, sl):
            rows = pl.ds(pl.multiple_of((nt - 1 - step) * tb, tb), tb)
            return (pltpu.make_async_copy(dqk_s.at[sl], dp_ref.at[rows, pl.ds(2 * DR, 2 * DR)], out_sems.at[sl, 0]),
                    pltpu.make_async_copy(dvg_s.at[sl], dp_ref.at[rows, pl.ds(4 * DR, 2 * DR)], out_sems.at[sl, 1]))

        @pl.when(i == 0)
        def _():
            dr_s[...] = jnp.zeros_like(dr_s)
            dgn_ref[...] = jnp.zeros_like(dgn_ref)

        @pl.when(i >= 2)
        def _():
            for cp in out_copies(i - 2, slot):
                cp.wait()

        def chunk(cc, carry):
            c = nc - 1 - cc
            rows = pl.ds(pl.multiple_of(c * CHUNK, CHUNK), CHUNK)
            cos = cos_ref[rows, :]
            sin = sin_ref[rows, :]
            heads = range(H)
            c0 = [slice(h * HEAD_DIM, (h + 1) * HEAD_DIM) for h in heads]
            c1 = [slice(DR + h * HEAD_DIM, DR + (h + 1) * HEAD_DIM) for h in heads]
            qh = [_rope(qk_ref[rows, c0[h]], cos, sin) for h in heads]
            kh = [_rope(qk_ref[rows, c1[h]], cos, sin) * scale for h in heads]
            qb = [t.astype(BF16) for t in qh]
            kb = [t.astype(BF16) for t in kh]
            vb = [vg_ref[rows, c0[h]].astype(BF16) for h in heads]
            rpb = [rprev_ref[c, h] for h in heads]
            qx = [(qh[h] * xi_ref[h]).astype(BF16) for h in heads]
            kz = [(kh[h] * zeta_ref[h]).astype(BF16) for h in heads]
            drh = [dr_s[h] for h in heads]
            drb = [t.astype(BF16) for t in drh]
            s = [_dot(qb[h], kb[h], NT) for h in heads]
            cross = [_dot(qx[h], rpb[h]) for h in heads]
            dv_state = [_dot(kz[h], drb[h]) for h in heads]
            dk_state = [_dot(vb[h], drb[h], NT) for h in heads]
            sb = [(s[h] * dm_ref[h]).astype(BF16) for h in heads]
            o = [_dot(sb[h], vb[h]) + cross[h] for h in heads]
            dob = []
            for h in heads:
                mu = jnp.mean(o[h], axis=-1, keepdims=True)
                oc = o[h] - mu
                rstd = lax.rsqrt(jnp.mean(oc * oc, axis=-1, keepdims=True) + EPS)
                ohat = oc * rstd
                gw = gnw_ref[:, c0[h]]
                gate = vg_ref[rows, c1[h]]
                sg = jax.nn.sigmoid(gate)
                dyv = dy_ref[rows, c0[h]]
                dvg_s[slot, rows, c1[h]] = (dyv * (ohat * gw) * (sg * (1.0 + gate * (1.0 - sg)))).astype(BF16)
                don = dyv * (gate * sg)
                dgn_ref[:, c0[h]] += jnp.sum(don * ohat, axis=0, keepdims=True)
                dohat = don * gw
                do = rstd * (dohat - jnp.mean(dohat, axis=-1, keepdims=True)
                             - ohat * jnp.mean(dohat * ohat, axis=-1, keepdims=True))
                dob.append(do.astype(BF16))
            ds = [_dot(dob[h], vb[h], NT) for h in heads]
            dq_state = [_dot(dob[h], rpb[h], NT) for h in heads]
            dv = [_dot(sb[h], dob[h], TN) + dv_state[h] for h in heads]
            dr_new = [_dot(qx[h], dob[h], TN) for h in heads]
            dsb = [(ds[h] * dm_ref[h]).astype(BF16) for h in heads]
            dqh = [_dot(dsb[h], kb[h]) + dq_state[h] * xi_ref[h] for h in heads]
            dkh = [_dot(dsb[h], qb[h], TN) + dk_state[h] * zeta_ref[h] for h in heads]
            for h in heads:
                dr_s[h] = drh[h] * gc_ref[h] + dr_new[h]
                dqk_s[slot, rows, c0[h]] = _rope_t(dqh[h], cos, sin).astype(BF16)
                dqk_s[slot, rows, c1[h]] = _rope_t(dkh[h] * scale, cos, sin).astype(BF16)
                dvg_s[slot, rows, c0[h]] = dv[h].astype(BF16)
            return carry

        lax.fori_loop(0, nc, chunk, 0)
        for cp in out_copies(i, slot):
            cp.start()

        @pl.when(i == nt - 1)
        def _():
            if nt >= 2:
                for cp in out_copies(i - 1, 1 - slot):
                    cp.wait()
            for cp in out_copies(i, slot):
                cp.wait()

    rev = lambda i: nt - 1 - i
    return _pcall(
        body, name="ret_bwd", grid=(nt,), aliases={11: 0}, comm=comm,
        operands=[proj, proj, cosf, sins, dm, zeta, xi, gc, gnw, rprev, dy, dproj],
        in_specs=[pl.BlockSpec((tb, 2 * DR), lambda i: (rev(i), 1)), pl.BlockSpec((tb, 2 * DR), lambda i: (rev(i), 2)),
                  pl.BlockSpec((tb, HEAD_DIM), lambda i: (rev(i), 0)), pl.BlockSpec((tb, HEAD_DIM), lambda i: (rev(i), 0))]
        + _ret_const_specs(H, DR)
        + [pl.BlockSpec((nc, H, CHUNK, HEAD_DIM), lambda i: (rev(i), 0, 0, 0)),
           pl.BlockSpec((tb, DR), lambda i: (rev(i), 1)), HBM_SPEC],
        out_specs=[HBM_SPEC, pl.BlockSpec((1, DR), lambda i: (0, 0))],
        out_shape=[jax.ShapeDtypeStruct(dproj.shape, BF16), jax.ShapeDtypeStruct((1, DR), F32)],
        scratch_shapes=[pltpu.VMEM((H, CHUNK, HEAD_DIM), F32), pltpu.VMEM((2, tb, 2 * DR), BF16),
                        pltpu.VMEM((2, tb, 2 * DR), BF16), pltpu.SemaphoreType.DMA((2, 2))])


def _place():
    x, y, c = lax.axis_index("x"), lax.axis_index("y"), lax.axis_index("c")
    chips = [(1 - x, y), (x, 1 - y), (1 - x, 1 - y)]
    return x, y, c, chips


def _own_slab(name, shard, place, side_by_side=False):
    R, C = shard.shape
    tr = _row_tile(R, C)
    if side_by_side:
        out = (jax.ShapeDtypeStruct((R, 4 * C), BF16), pl.BlockSpec((tr, C), lambda i, p: (i, p[1])))
    else:
        out = (jax.ShapeDtypeStruct((4, R, C), BF16), pl.BlockSpec((None, tr, C), lambda i, p: (p[1], i, 0)))
    return _ew("cast_" + name, lambda a: (a,), [(shard, pl.BlockSpec((tr, C), lambda i, p: (i, 0)))], [out],
               (R // tr,), sp=place)[0]


def _slab_half(ref, chip, half, cols):
    if cols is None:
        r2 = ref.shape[1] // 2
        return ref.at[chip, pl.ds(half * r2, r2), :]
    r2 = ref.shape[0] // 2
    return ref.at[pl.ds(half * r2, r2), pl.ds(pl.multiple_of(chip * cols, LANES), cols)]


class _remote:
    def __init__(self, src, dst, ssem, rsem, k, to):
        self.args = dict(src_ref=src, dst_ref=dst, send_sem=ssem.at[k], recv_sem=rsem.at[k], device_id=to,
                         device_id_type=MESH)

    def start(self):
        pltpu.make_async_remote_copy(**self.args).start()

    def wait_send(self):
        pltpu.make_async_remote_copy(**self.args).wait_send()

    def wait_recv(self):
        pltpu.make_async_remote_copy(**self.args).wait_recv()


def _task_fns(copies):
    def start(cins, couts, ssem, rsem, base):
        for cp in copies(cins, couts, ssem, rsem, base)[0]:
            cp.start()

    def finish(cins, couts, ssem, rsem, base):
        sends, recvs = copies(cins, couts, ssem, rsem, base)
        for cp in sends:
            cp.wait_send()
        for cp in recvs:
            cp.wait_recv()

    return start, finish


NEIGHBOURS, DIAGONAL = (0, 1), (2,)


def _gather_ici(st, which=NEIGHBOURS + DIAGONAL, cols=None):
    def copies(cins, couts, ssem, rsem, base):
        x, y, c, chips = _place()
        out = couts[0]
        mine = _slab_half(out, 2 * x + y, c, cols)
        sends, recvs = [], []
        for k, j in enumerate(which):
            cx, cy = chips[j]
            got = _slab_half(out, 2 * cx + cy, c, cols)
            sends.append(_remote(mine, mine, ssem, rsem, base + k, (cx, cy, c)))
            recvs.append(_remote(got, got, ssem, rsem, base + k, (x, y, c)))
        return sends, recvs

    start, finish = _task_fns(copies)
    return _Comm([st], [jax.ShapeDtypeStruct(st.shape, st.dtype)], {0: 0}, len(which), start, finish)


def _gather_d2d(st, cols=None):
    def copies(cins, couts, ssem, rsem, base):
        x, y, c, chips = _place()
        out = couts[0]
        sends, recvs = [], []
        for j, (cx, cy) in enumerate(chips):
            have = _slab_half(out, 2 * cx + cy, c, cols)
            want = _slab_half(out, 2 * cx + cy, 1 - c, cols)
            sends.append(_remote(have, have, ssem, rsem, base + j, (x, y, 1 - c)))
            recvs.append(_remote(want, want, ssem, rsem, base + j, (x, y, c)))
        return sends, recvs

    start, finish = _task_fns(copies)
    return _Comm([st], [jax.ShapeDtypeStruct(st.shape, st.dtype)], {0: 0}, 3, start, finish)


def _gather_conv(conv_w):
    def copies(cins, couts, ssem, rsem, base):
        x, y, c, chips = _place()
        src, out = cins[0], couts[0]
        sends = [_remote(src, out.at[2 * x + y], ssem, rsem, base + j, (*chip, c)) for j, chip in enumerate(chips)]
        recvs = [_remote(src, out.at[2 * cx + cy], ssem, rsem, base + j, (x, y, c)) for j, (cx, cy) in enumerate(chips)]
        return sends, recvs

    start, finish = _task_fns(copies)
    return _Comm([conv_w], [jax.ShapeDtypeStruct((4,) + conv_w.shape, conv_w.dtype)], {}, 3, start, finish)


def _pair_exchange(g):
    r2 = g.shape[1] // 2

    def copies(cins, couts, ssem, rsem, base):
        x, y, c, _ = _place()
        cp = _remote(cins[0].at[:, pl.ds((1 - c) * r2, r2), :], couts[0], ssem, rsem, base, (x, y, 1 - c))
        return [cp], [cp]

    start, finish = _task_fns(copies)
    return _Comm([g], [jax.ShapeDtypeStruct((g.shape[0], r2, g.shape[2]), g.dtype)], {}, 1, start, finish)


def _chip_exchange(part):
    def copies(cins, couts, ssem, rsem, base):
        x, y, c, chips = _place()
        cps = [_remote(cins[0].at[2 * cx + cy], couts[0].at[j], ssem, rsem, base + j, (cx, cy, c))
               for j, (cx, cy) in enumerate(chips)]
        return cps, cps

    start, finish = _task_fns(copies)
    return _Comm([part], [jax.ShapeDtypeStruct((3,) + part.shape[1:], part.dtype)], {}, 3, start, finish)


def _pair_share(slot):
    def copies(cins, couts, ssem, rsem, base):
        x, y, c, _ = _place()
        out = couts[0]
        return ([_remote(out.at[c], out.at[c], ssem, rsem, base, (x, y, 1 - c))],
                [_remote(out.at[1 - c], out.at[1 - c], ssem, rsem, base, (x, y, c))])

    start, finish = _task_fns(copies)
    return _Comm([slot], [jax.ShapeDtypeStruct(slot.shape, slot.dtype)], {0: 0}, 1, start, finish)


def _gather_small(sm):
    flips = [(fx, fy, fc) for fx in (0, 1) for fy in (0, 1) for fc in (0, 1)][1:]

    def copies(cins, couts, ssem, rsem, base):
        x, y, c, _ = _place()
        src, out = cins[0], couts[0]
        peers = [(1 - x if fx else x, 1 - y if fy else y, 1 - c if fc else c) for fx, fy, fc in flips]
        sends = [_remote(src, out.at[4 * x + 2 * y + c], ssem, rsem, base + k, peer) for k, peer in enumerate(peers)]
        recvs = [_remote(src, out.at[4 * px + 2 * py + pc], ssem, rsem, base + k, (x, y, c))
                 for k, (px, py, pc) in enumerate(peers)]
        return sends, recvs

    start, finish = _task_fns(copies)
    return _Comm([sm], [jax.ShapeDtypeStruct((8,) + sm.shape, sm.dtype)], {}, 7, start, finish)


def _comm_call(name, tasks):
    task = _merge(tasks)
    nci = len(task.ins)

    def body(*refs):
        cins, couts, (ssem, rsem) = refs[:nci], refs[nci:nci + len(task.outs)], refs[nci + len(task.outs):]
        task.start(cins, couts, ssem, rsem, 0)
        task.finish(cins, couts, ssem, rsem, 0)

    return pl.pallas_call(
        body, in_specs=[HBM_SPEC] * nci, out_specs=[HBM_SPEC] * len(task.outs), out_shape=list(task.outs),
        scratch_shapes=[pltpu.SemaphoreType.DMA((task.n_sem,)), pltpu.SemaphoreType.DMA((task.n_sem,))],
        input_output_aliases=task.aliases, name=name)(*task.ins)


def _adamw(w, g, m, v):
    m = ADAM_B1 * m + (1.0 - ADAM_B1) * g
    v = ADAM_B2 * v + (1.0 - ADAM_B2) * (g * g)
    m_hat = m / (1.0 - ADAM_B1 ** ADAM_STEP)
    v_hat = v / (1.0 - ADAM_B2 ** ADAM_STEP)
    delta = -ADAM_LR * (m_hat / (jnp.sqrt(v_hat) + ADAM_EPS) + ADAM_WD * w)
    return delta, m, v


def _adamw_call(name, w, g, m, v):
    R, C = w.shape
    tr = _row_tile(R, C, 2 * 1024 * 1024)
    row = pl.BlockSpec((tr, C), lambda i: (i, 0))
    o = jax.ShapeDtypeStruct((R, C), F32)
    return _ew(name, lambda w_, g_, m_, v_: (*_adamw(w_, g_, m_, v_), g_), [(w, row), (g, row), (m, row), (v, row)],
               [(o, row), (o, row), (o, row), (o, row)], (R // tr,))


def _pair_sum(name, g, ra, place):
    _, R, C = g.shape
    r2 = R // 2
    tr = _row_tile(r2, C)
    nb = r2 // tr
    own = pl.BlockSpec((None, tr, C), lambda j, i, p: (j, p[0] * nb + i, 0))
    blk = pl.BlockSpec((None, tr, C), lambda j, i, p: (j, i, 0))
    return _ew("rs_pair_sum_" + name, lambda a, b: (a + b,), [(g, own), (ra, blk)],
               [(jax.ShapeDtypeStruct((4, r2, C), BF16), blk)], (4, nb), sp=place)[0]


def _chip_sum(name, g, ra, rb, place):
    _, R, C = g.shape
    r2 = R // 2
    tr = _row_tile(r2, C)
    nb = r2 // tr
    own = pl.BlockSpec((None, tr, C), lambda i, p: (p[1], p[0] * nb + i, 0))
    mine = pl.BlockSpec((None, tr, C), lambda i, p: (p[1], i, 0))
    src = [pl.BlockSpec((None, tr, C), functools.partial(lambda i, p, j: (j, i, 0), j=j)) for j in range(3)]
    out = pl.BlockSpec((None, tr, C), lambda i, p: (p[0], i, 0))

    def total(a, b, r0, r1, r2_):
        return ((((a + b) + r0.astype(F32)) + r1.astype(F32)) + r2_.astype(F32),)

    return _ew("rs_chip_sum_" + name, total, [(g, own), (ra, mine), (rb, src[0]), (rb, src[1]), (rb, src[2])],
               [(jax.ShapeDtypeStruct((2, r2, C), F32), out)], (nb,), sp=place)[0]


def _pack(arrays):
    rows, offs, pos = [], [], 0
    for a in arrays:
        flat = a.reshape(-1)
        n = -(-flat.shape[0] // (8 * LANES)) * (8 * LANES)
        if n != flat.shape[0]:
            flat = jnp.pad(flat, (0, n - flat.shape[0]))
        rows.append(flat.reshape(-1, LANES))
        offs.append(pos)
        pos += n // LANES
    return jnp.concatenate(rows, axis=0), offs


def _unpack(packed, offs, shapes):
    out = []
    for off, shp in zip(offs, shapes):
        n = 1
        for s in shp:
            n *= s
        out.append(packed[off:off + -(-n // LANES)].reshape(-1)[:n].reshape(shp))
    return out


def _sum8(gathered):
    _, R, C = gathered.shape
    tr = _row_tile(R, C, 256 * 1024)
    specs = [pl.BlockSpec((None, tr, C), functools.partial(lambda i, d: (d, i, 0), d=d)) for d in range(8)]

    def fn(*parts):
        t = parts[0]
        for p in parts[1:]:
            t = t + p
        return (t,)

    return _ew("small_sum", fn, [(gathered, s) for s in specs],
               [(jax.ShapeDtypeStruct((R, C), F32), pl.BlockSpec((tr, C), lambda i: (i, 0)))], (R // tr,))[0]


BIG = ("w_in", "w_out", "w_ffn_gate", "w_ffn_up", "w_ffn_down")
SMALL = ("ln1_w", "conv_w", "conv_b", "gate_a_w", "gate_a_b", "gate_x_w", "gate_x_b", "lru_lambda", "ret_gn_w", "ln2_w",
         "final_norm_w")
WEIGHTS = ("ln1_w", "w_in", "conv_w", "conv_b", "gate_a_w", "gate_a_b", "gate_x_w", "gate_x_b", "lru_lambda", "ret_gn_w",
           "w_out", "ln2_w", "w_ffn_gate", "w_ffn_up", "w_ffn_down", "final_norm_w")


def kernel(x, ln1_w, w_in, conv_w, conv_b, gate_a_w, gate_a_b, gate_x_w, gate_x_b, lru_lambda, ret_gn_w, w_out, ln2_w, w_ffn_gate, w_ffn_up, w_ffn_down, final_norm_w, loss_target, m_ln1_w, m_w_in, m_conv_w, m_conv_b, m_gate_a_w, m_gate_a_b, m_gate_x_w, m_gate_x_b, m_lru_lambda, m_ret_gn_w, m_w_out, m_ln2_w, m_w_ffn_gate, m_w_ffn_up, m_w_ffn_down, m_final_norm_w, v_ln1_w, v_w_in, v_conv_w, v_conv_b, v_gate_a_w, v_gate_a_b, v_gate_x_w, v_gate_x_b, v_lru_lambda, v_ret_gn_w, v_w_out, v_ln2_w, v_w_ffn_gate, v_w_ffn_up, v_w_ffn_down, v_final_norm_w):
    w = dict(ln1_w=ln1_w, w_in=w_in, conv_w=conv_w, conv_b=conv_b, gate_a_w=gate_a_w, gate_a_b=gate_a_b, gate_x_w=gate_x_w,
             gate_x_b=gate_x_b, lru_lambda=lru_lambda, ret_gn_w=ret_gn_w, w_out=w_out, ln2_w=ln2_w, w_ffn_gate=w_ffn_gate,
             w_ffn_up=w_ffn_up, w_ffn_down=w_ffn_down, final_norm_w=final_norm_w)
    m = dict(ln1_w=m_ln1_w, w_in=m_w_in, conv_w=m_conv_w, conv_b=m_conv_b, gate_a_w=m_gate_a_w, gate_a_b=m_gate_a_b,
             gate_x_w=m_gate_x_w, gate_x_b=m_gate_x_b, lru_lambda=m_lru_lambda, ret_gn_w=m_ret_gn_w, w_out=m_w_out,
             ln2_w=m_ln2_w, w_ffn_gate=m_w_ffn_gate, w_ffn_up=m_w_ffn_up, w_ffn_down=m_w_ffn_down,
             final_norm_w=m_final_norm_w)
    v = dict(ln1_w=v_ln1_w, w_in=v_w_in, conv_w=v_conv_w, conv_b=v_conv_b, gate_a_w=v_gate_a_w, gate_a_b=v_gate_a_b,
             gate_x_w=v_gate_x_w, gate_x_b=v_gate_x_b, lru_lambda=v_lru_lambda, ret_gn_w=v_ret_gn_w, w_out=v_w_out,
             ln2_w=v_ln2_w, w_ffn_gate=v_w_ffn_gate, w_ffn_up=v_w_ffn_up, w_ffn_down=v_w_ffn_down,
             final_norm_w=v_final_norm_w)
    xs, tgt = x[0], loss_target[0]
    S, D = xs.shape
    DL, DR = conv_b.shape[1], ret_gn_w.shape[1]
    assert DL == DR and DL % HEAD_DIM == 0 and S % CHUNK == 0
    d_mix = DL + DR
    cx, cy, cc = lax.axis_index("x"), lax.axis_index("y"), lax.axis_index("c")
    chip = 2 * cx + cy
    place = jnp.stack([cc, chip]).astype(jnp.int32)
    grad, delta, new_m, new_v = {}, {}, {}, {}

    def finish_big(n, full):
        shp = w[n].shape
        g2 = full.reshape(shp[1], shp[2])
        w2, m2, v2 = (t[n].reshape(shp[1], shp[2]) for t in (w, m, v))
        d_, m_, v_, g_ = _adamw_call("adamw_" + n, w2, g2, m2, v2)
        grad[n], delta[n], new_m[n], new_v[n] = (t.reshape(shp) for t in (g_, d_, m_, v_))

    def all_sum(gathered, own):
        return _sum8(lax.dynamic_update_slice(gathered, own[None], (4 * cx + 2 * cy + cc, 0, 0)))

    wide = ("w_ffn_gate", "w_ffn_up")
    st = {n: _own_slab(n, w[n][0], place, side_by_side=n in wide) for n in BIG}
    (u1,), (w_in_st,) = _rms_fwd("rms1", xs, ln1_w, TM, comm=_gather_ici(st["w_in"]))
    w_in_st, conv_st = _comm_call("gather_w_in", [_gather_d2d(w_in_st), _gather_conv(conv_w[0])])
    conv_st = lax.dynamic_update_slice(conv_st, conv_w, (chip, 0, 0))
    cw_cols = conv_st.shape[2]
    conv_full = jnp.transpose(conv_st, (1, 0, 2)).reshape(CONV_W, 4 * cw_cols)
    n_in, n_ff = w_in_st.shape[2], w_ffn_gate.shape[2]
    tables = _ret_tables(S, DR // HEAD_DIM)
    wab, wxb = gate_a_w[0].astype(BF16), gate_x_w[0].astype(BF16)
    lru_w = (conv_full, conv_b, wab, gate_a_b, wxb, gate_x_b, lru_lambda)

    proj, (w_out_st, wg_f) = _mm_nn_stacked(
        "proj", u1, w_in_st, F32, TM_WIDE,
        comm=_merge([_gather_ici(st["w_out"]), _gather_ici(st["w_ffn_gate"], cols=n_ff)]))
    (hs, kept, y), (w_out_st, wg_f, wu_f) = _lru_fwd(
        proj, *lru_w, LRU_TILE, d_mix,
        comm=_merge([_gather_d2d(w_out_st), _gather_d2d(wg_f, cols=n_ff), _gather_ici(st["w_ffn_up"], cols=n_ff)]))
    (y, rprev), (wu_f, wd_st) = _ret_fwd(
        proj, y, tables, ret_gn_w, RET_BLOCK,
        comm=_merge([_gather_d2d(wu_f, cols=n_ff), _gather_ici(st["w_ffn_down"], NEIGHBOURS)]))
    w_out_f = w_out_st.reshape(d_mix, D)
    (h1, u2), (wd_st,) = _out_proj_rms(y, w_out_f, xs, ln2_w, TM, comm=_gather_ici(wd_st, DIAGONAL))
    (dg_fac, du_fac, ff), (wd_st,) = _ffn_gate_up(u2, wg_f, wu_f, n_ff, TM, comm=_gather_d2d(wd_st))
    wd_f = wd_st.reshape(4 * n_ff, D)
    dh2, dh2b, d_fw, loss = _ffn_down_loss(ff, wd_f, h1, tgt, final_norm_w.reshape(1, D), TM_RESIDENT)

    g_wd = _mm_tn("g_w_down", ff, dh2b, n_ff, TILE_GRAD, TK_GRAD).reshape(4, n_ff, D)
    (dgt, dup), (ra_wd,) = _ffn_gate_up_bwd(dh2b, wd_f, dg_fac, du_fac, TM_WIDE, n_ff, comm=_pair_exchange(g_wd))
    pb_wd = _pair_sum("w_ffn_down", g_wd, ra_wd, place)
    g_wg, (rb_wd,) = _mm_tn_slabs("g_w_gate", u2, dgt, n_ff, TILE_GRAD, TK_GRAD, comm=_chip_exchange(pb_wd))
    slot_wd = _chip_sum("w_ffn_down", g_wd, ra_wd, rb_wd, place)
    g_wu, (full_wd, ra_wg) = _mm_tn_slabs("g_w_up", u2, dup, n_ff, TILE_GRAD, TK_GRAD,
                                          comm=_merge([_pair_share(slot_wd), _pair_exchange(g_wg)]))
    finish_big("w_ffn_down", full_wd)
    pb_wg = _pair_sum("w_ffn_gate", g_wg, ra_wg, place)
    du2, (rb_wg,) = _mm_nt_stacked("d_u2_gate", dgt, wg_f, TM, comm=_chip_exchange(pb_wg))
    du2, (ra_wu,) = _mm_nt_stacked("d_u2_up", dup, wu_f, TM, res=du2, comm=_pair_exchange(g_wu))
    slot_wg = _chip_sum("w_ffn_gate", g_wg, ra_wg, rb_wg, place)
    pb_wu = _pair_sum("w_ffn_up", g_wu, ra_wu, place)
    (dh1, dh1b, dy, d_ln2), (full_wg,) = _rms_bwd_dy(h1, ln2_w, du2, dh2, w_out_f, TM_RESIDENT,
                                                     comm=_pair_share(slot_wg))
    finish_big("w_ffn_gate", full_wg)
    g_wout = _mm_tn("g_w_out", y, dh1b, TILE_GRAD, TILE_GRAD, TK_GRAD).reshape(4, d_mix // 4, D)
    (dproj, d_cw, d_cb, d_wa, d_ba, d_wx, d_bx, d_lam), (rb_wu, ra_wout) = _lru_bwd(
        proj, hs, kept, dy, conv_full, wab, wxb, lru_lambda, LRU_TILE,
        comm=_merge([_chip_exchange(pb_wu), _pair_exchange(g_wout)]))
    slot_wu = _chip_sum("w_ffn_up", g_wu, ra_wu, rb_wu, place)
    pb_wout = _pair_sum("w_out", g_wout, ra_wout, place)
    (dproj, d_gn), (full_wu, rb_wout) = _ret_bwd(proj, rprev, dy, dproj, tables, ret_gn_w, RET_BLOCK,
                                                 comm=_merge([_pair_share(slot_wu), _chip_exchange(pb_wout)]))
    finish_big("w_ffn_up", full_wu)
    slot_wout = _chip_sum("w_out", g_wout, ra_wout, rb_wout, place)
    small = dict(conv_w=d_cw, conv_b=d_cb, gate_a_w=d_wa, gate_a_b=d_ba, gate_x_w=d_wx, gate_x_b=d_bx, lru_lambda=d_lam,
                 ret_gn_w=d_gn, ln2_w=d_ln2, final_norm_w=d_fw)
    packed, offs = _pack([small[n] for n in SMALL[1:]] + [loss])
    g_win, (full_wout, got_small) = _mm_tn("g_w_in", u1, dproj, TILE_GRAD, None, TK_GRAD, stacked_cols=n_in,
                                           comm=_merge([_pair_share(slot_wout), _gather_small(packed)]))
    finish_big("w_out", full_wout)
    (ra_win,) = _comm_call("rs_pair_w_in", [_pair_exchange(g_win)])
    pb_win = _pair_sum("w_in", g_win, ra_win, place)
    du1, (rb_win,) = _mm_nt_stacked("d_u1", dproj, w_in_st, TM, comm=_chip_exchange(pb_win))
    slot_win = _chip_sum("w_in", g_win, ra_win, rb_win, place)
    gx, d_ln1 = _rms_bwd("rms1_bwd", xs, ln1_w, du1, dh1, TM)
    packed1, _ = _pack([d_ln1])
    full_win, got_ln1 = _comm_call("reduce_tail", [_pair_share(slot_win), _gather_small(packed1)])
    finish_big("w_in", full_win)

    red = _unpack(all_sum(got_small, packed), offs, [small[n].shape for n in SMALL[1:]] + [(1, LANES)])
    g = dict(zip(SMALL[1:], red[:-1]))
    g["ln1_w"] = all_sum(got_ln1, packed1)[:-(-D // LANES)].reshape(1, D)
    loss_out = red[-1][0, 0]
    g["conv_w"] = lax.dynamic_slice(g["conv_w"], (0, chip * cw_cols), (CONV_W, cw_cols))
    packs = [_pack([t[n] for n in SMALL])[0] for t in (w, m, v)]
    gp, offs2 = _pack([g[n] for n in SMALL])
    outs = _adamw_call("adamw_small", packs[0], gp, packs[1], packs[2])
    shapes = [w[n].shape for n in SMALL]
    for dst, arr in zip((delta, new_m, new_v), outs):
        dst.update(zip(SMALL, _unpack(arr, offs2, shapes)))
    for n in SMALL:
        grad[n] = g[n].reshape(w[n].shape)

    return (loss_out, gx.reshape(x.shape), *[grad[n] for n in WEIGHTS], *[delta[n] for n in WEIGHTS],
            *[new_m[n] for n in WEIGHTS], *[new_v[n] for n in WEIGHTS])
```

```python
import functools

import jax
import jax.numpy as jnp
from jax import lax
from jax.experimental import pallas as pl
from jax.experimental.pallas import tpu as pltpu

F32 = jnp.float32
BF16 = jnp.bfloat16
MESH = pl.DeviceIdType.MESH

EPS = 1e-6
LRU_C = 8.0
ROPE_BASE = 10000.0
CHUNK = 128
HEAD_DIM = 128
CONV_W = 4
ADAM_LR = 0.001
ADAM_B1 = 0.9
ADAM_B2 = 0.999
ADAM_EPS = 1e-08
ADAM_WD = 0.01
ADAM_STEP = 10

V7X_VMEM_BYTES = 64 * 1024 * 1024
VMEM_LIMIT = V7X_VMEM_BYTES - 8 * 1024 * 1024
LANES = 128
SUBLANES_16BIT = 16

TM = 512
TM_WIDE = 1024
TM_RESIDENT = 256
TK_GRAD = 2048
TILE_GRAD = 1024
LRU_TILE = 256
RET_BLOCK = 512

NN = (((1,), (0,)), ((), ()))
NT = (((1,), (1,)), ((), ()))
TN = (((0,), (0,)), ((), ()))


def _dot(a, b, dims=NN):
    return lax.dot_general(a, b, dims, preferred_element_type=F32)


def _tile(n, pref, mult=SUBLANES_16BIT):
    best = None
    t = mult
    while t <= min(n, pref):
        if n % t == 0:
            best = t
        t += mult
    return best if best is not None else n


ELEMENTWISE_BLOCK = 4 * 1024 * 1024


def _row_tile(rows, cols, budget_bytes=ELEMENTWISE_BLOCK):
    return _tile(rows, max(SUBLANES_16BIT, budget_bytes // (cols * 4)))


def _params(sem):
    return pltpu.CompilerParams(dimension_semantics=sem, vmem_limit_bytes=VMEM_LIMIT)


HBM_SPEC = pl.BlockSpec(memory_space=pl.ANY)


class _Comm:
    def __init__(self, ins, outs, aliases, n_sem, start, finish):
        self.ins, self.outs, self.aliases, self.n_sem, self.start, self.finish = ins, outs, aliases, n_sem, start, finish


def _merge(tasks):
    ins, outs, aliases, plans, n_sem = [], [], {}, [], 0
    for t in tasks:
        i0, o0 = len(ins), len(outs)
        plans.append((t, i0, o0, n_sem))
        ins += t.ins
        outs += t.outs
        aliases.update({i0 + a: o0 + b for a, b in t.aliases.items()})
        n_sem += t.n_sem

    def run(which):
        def go(cins, couts, ssem, rsem, base):
            for t, i0, o0, s0 in plans:
                getattr(t, which)(cins[i0:i0 + len(t.ins)], couts[o0:o0 + len(t.outs)], ssem, rsem, base + s0)
        return go

    return _Comm(ins, outs, aliases, n_sem, run("start"), run("finish"))


def _pcall(body, *, name, grid, in_specs, out_specs, out_shape, operands, scratch_shapes=(), aliases=None, comm=None):
    n_in, n_out, n_scr = len(operands), len(out_shape), len(scratch_shapes)
    aliases = dict(aliases or {})
    params = _params(("arbitrary",) * len(grid))
    if comm is None:
        return pl.pallas_call(body, grid=grid, in_specs=list(in_specs), out_specs=list(out_specs), out_shape=list(out_shape),
                              scratch_shapes=list(scratch_shapes), input_output_aliases=aliases, name=name,
                              compiler_params=params)(*operands)
    nci, nco = len(comm.ins), len(comm.outs)

    def wrapped(*refs):
        ins, cins = refs[:n_in], refs[n_in:n_in + nci]
        o0 = n_in + nci
        outs, couts = refs[o0:o0 + n_out], refs[o0 + n_out:o0 + n_out + nco]
        s0 = o0 + n_out + nco
        scr, (ssem, rsem) = refs[s0:s0 + n_scr], refs[s0 + n_scr:]
        ids = [pl.program_id(a) for a in range(len(grid))]
        first = functools.reduce(jnp.logical_and, [i == 0 for i in ids])
        last = functools.reduce(jnp.logical_and, [i == g - 1 for i, g in zip(ids, grid)])

        @pl.when(first)
        def _():
            comm.start(cins, couts, ssem, rsem, 0)

        body(*ins, *outs, *scr)

        @pl.when(last)
        def _():
            comm.finish(cins, couts, ssem, rsem, 0)

    aliases.update({n_in + a: n_out + b for a, b in comm.aliases.items()})
    res = pl.pallas_call(
        wrapped, grid=grid, in_specs=list(in_specs) + [HBM_SPEC] * nci, out_specs=list(out_specs) + [HBM_SPEC] * nco,
        out_shape=list(out_shape) + list(comm.outs),
        scratch_shapes=list(scratch_shapes) + [pltpu.SemaphoreType.DMA((comm.n_sem,)), pltpu.SemaphoreType.DMA((comm.n_sem,))],
        input_output_aliases=aliases, name=name, compiler_params=params)(*operands, *comm.ins)
    return res[:n_out], res[n_out:]


def _ew(name, fn, ins, outs, grid, sp=None):
    n_in = len(ins)

    def body(*refs):
        if sp is not None:
            refs = refs[1:]
        vals = [r[...] for r in refs[:n_in]]
        res = fn(*vals)
        for o_ref, v in zip(refs[n_in:], res):
            o_ref[...] = v.astype(o_ref.dtype)

    in_specs = [s for _, s in ins]
    out_specs = [s for _, s in outs]
    out_shape = [s for s, _ in outs]
    sem = ("arbitrary",) * len(grid)
    if sp is None:
        return pl.pallas_call(body, grid=grid, in_specs=in_specs, out_specs=out_specs, out_shape=out_shape,
                              name=name, compiler_params=_params(sem))(*[a for a, _ in ins])
    gs = pltpu.PrefetchScalarGridSpec(num_scalar_prefetch=1, grid=grid, in_specs=in_specs, out_specs=out_specs)
    return pl.pallas_call(body, grid_spec=gs, out_shape=out_shape, name=name,
                          compiler_params=_params(sem))(sp, *[a for a, _ in ins])


def _matmul(name, pairs, dims, grid, out_shape, out_spec, acc_shape, comm=None, transpose_out=False):
    n = len(pairs)
    nk = grid[2]

    def body(*refs):
        ab = refs[:2 * n]
        o_ref = refs[2 * n]
        acc_ref = refs[2 * n + 1] if nk > 1 else None

        def partial():
            t = None
            for p in range(n):
                d = _dot(ab[2 * p][...], ab[2 * p + 1][...], dims)
                t = d if t is None else t + d
            return t

        def finish(t):
            o_ref[...] = (t.T if transpose_out else t).astype(o_ref.dtype)

        if nk == 1:
            finish(partial())
        else:
            k = pl.program_id(2)

            @pl.when(k == 0)
            def _():
                acc_ref[...] = partial()

            @pl.when(k > 0)
            def _():
                acc_ref[...] += partial()

            @pl.when(k == nk - 1)
            def _():
                finish(acc_ref[...])

    operands, in_specs = [], []
    for a, a_spec, b, b_spec in pairs:
        operands += [a, b]
        in_specs += [a_spec, b_spec]
    scratch = [pltpu.VMEM(acc_shape, F32)] if nk > 1 else []
    res = _pcall(body, name=name, grid=grid, in_specs=in_specs, out_specs=[out_spec], out_shape=[out_shape],
                 operands=operands, scratch_shapes=scratch, comm=comm)
    return res[0] if comm is None else (res[0][0], res[1])


def _mm_nn_stacked(name, a, b_st, out_dtype, tm, comm=None):
    M, K = a.shape
    J, _, Nj = b_st.shape
    tm = _tile(M, tm)
    return _matmul(
        name, [(a, pl.BlockSpec((tm, K), lambda j, i, k: (i, 0)), b_st, pl.BlockSpec((None, K, Nj), lambda j, i, k: (j, 0, 0)))],
        NN, (J, M // tm, 1), jax.ShapeDtypeStruct((M, J * Nj), out_dtype), pl.BlockSpec((tm, Nj), lambda j, i, k: (i, j)), None,
        comm=comm)


def _mm_nt_stacked(name, a, b_st, tm, res=None, comm=None):
    M, K = a.shape
    N = b_st.shape[-2]
    tm = _tile(M, tm)

    def body(a_ref, b_ref, *rest):
        o_ref = rest[-1]
        t = None if res is None else rest[0][...]
        if len(b_ref.shape) == 2:
            d = _dot(a_ref[...], b_ref[...], NT)
            t = d if t is None else t + d
        else:
            nj = b_ref.shape[2]
            for s in range(b_ref.shape[0]):
                d = _dot(a_ref[:, s * nj:(s + 1) * nj], b_ref[s], NT)
                t = d if t is None else t + d
        o_ref[...] = t

    row = pl.BlockSpec((tm, N), lambda i: (i, 0))
    out = _pcall(body, name=name, grid=(M // tm,),
                 in_specs=[pl.BlockSpec((tm, K), lambda i: (i, 0)),
                           pl.BlockSpec(b_st.shape, lambda i: (0,) * b_st.ndim, pipeline_mode=pl.Buffered(1))]
                 + [row] * (res is not None),
                 out_specs=[row], out_shape=[jax.ShapeDtypeStruct((M, N), F32)],
                 operands=[a, b_st] + [res] * (res is not None), comm=comm)
    return out[0] if comm is None else (out[0][0], out[1])


MXU_COLUMNS = 256


def _col_blocks(n):
    return [slice(s, min(s + MXU_COLUMNS, n)) for s in range(0, n, MXU_COLUMNS)]


def _ffn_gate_up(u2, wg, wu, n_slab, tm, comm=None):
    S, D = u2.shape
    F = wg.shape[1]
    tm = _tile(S, tm)
    tn = 2 * n_slab if (n_slab % MXU_COLUMNS and (F // n_slab) % 2 == 0) else n_slab

    def body(a_ref, wg_ref, wu_ref, dg_ref, du_ref, ff_ref):
        a = a_ref[...]
        blocks = _col_blocks(tn)
        ahead = (_dot(a, wg_ref[:, blocks[0]]), _dot(a, wu_ref[:, blocks[0]]))
        for j, cols in enumerate(blocks):
            g, u = ahead
            if j + 1 < len(blocks):
                ahead = (_dot(a, wg_ref[:, blocks[j + 1]]), _dot(a, wu_ref[:, blocks[j + 1]]))
            sg = jax.nn.sigmoid(g)
            silu = g * sg
            dg_ref[:, cols] = (u * (sg * (1.0 + g * (1.0 - sg)))).astype(BF16)
            du_ref[:, cols] = silu.astype(BF16)
            ff_ref[:, cols] = (silu * u).astype(BF16)

    w_spec = pl.BlockSpec((D, tn), lambda j, i: (0, j), pipeline_mode=pl.Buffered(1))
    o_spec = pl.BlockSpec((tm, tn), lambda j, i: (i, j))
    o = jax.ShapeDtypeStruct((S, F), BF16)
    return _pcall(body, name="ffn_gate_up", grid=(F // tn, S // tm),
                  in_specs=[pl.BlockSpec((tm, D), lambda j, i: (i, 0)), w_spec, w_spec],
                  out_specs=[o_spec, o_spec, o_spec], out_shape=[o, o, o], operands=[u2, wg, wu], comm=comm)


def _ffn_gate_up_bwd(dh2b, wd, dg_fac, du_fac, tm, tn, comm=None):
    S, D = dh2b.shape
    F = wd.shape[0]
    tm, tn = _tile(S, tm), _tile(F, tn, LANES)

    def body(a_ref, wd_ref, dg_ref, du_ref, dgt_ref, dup_ref):
        a = a_ref[...]
        for cols in _col_blocks(tn):
            d = _dot(a, wd_ref[cols, :], NT)
            dgt_ref[:, cols] = (d * dg_ref[:, cols].astype(F32)).astype(BF16)
            dup_ref[:, cols] = (d * du_ref[:, cols].astype(F32)).astype(BF16)

    blk = pl.BlockSpec((tm, tn), lambda j, i: (i, j))
    o = jax.ShapeDtypeStruct((S, F), BF16)
    return _pcall(body, name="ffn_gate_up_bwd", grid=(F // tn, S // tm),
                  in_specs=[pl.BlockSpec((tm, D), lambda j, i: (i, 0)), pl.BlockSpec((tn, D), lambda j, i: (j, 0)), blk, blk],
                  out_specs=[blk, blk], out_shape=[o, o], operands=[dh2b, wd, dg_fac, du_fac], comm=comm)


def _mm_tn(name, a, b, tmo, tn, tk, stacked_cols=None, comm=None):
    S, Mo = a.shape
    N = b.shape[1]
    tmo, tk = _tile(Mo, tmo, LANES), _tile(S, tk)
    if stacked_cols is None:
        tn = _tile(N, tn, LANES)
        out_shape = jax.ShapeDtypeStruct((Mo, N), F32)
        out_spec = pl.BlockSpec((tmo, tn), lambda i, j, k: (i, j))
    else:
        tn = stacked_cols
        out_shape = jax.ShapeDtypeStruct((N // tn, Mo, tn), F32)
        out_spec = pl.BlockSpec((None, tmo, tn), lambda i, j, k: (j, i, 0))
    return _matmul(
        name, [(a, pl.BlockSpec((tk, tmo), lambda i, j, k: (k, i)), b, pl.BlockSpec((tk, tn), lambda i, j, k: (k, j)))],
        TN, (Mo // tmo, N // tn, S // tk), out_shape, out_spec, (tmo, tn), comm=comm)


def _mm_tn_slabs(name, a, b, n_slab, tmo, tk, comm=None):
    S, Mo = a.shape
    J = b.shape[1] // n_slab
    tmo, tk = _tile(Mo, tmo, LANES), _tile(S, tk)
    return _matmul(
        name, [(b, pl.BlockSpec((tk, n_slab), lambda j, i, k: (k, j)), a, pl.BlockSpec((tk, tmo), lambda j, i, k: (k, i)))],
        TN, (J, Mo // tmo, S // tk), jax.ShapeDtypeStruct((J, Mo, n_slab), F32),
        pl.BlockSpec((None, tmo, n_slab), lambda j, i, k: (j, i, 0)), (n_slab, tmo), comm=comm, transpose_out=True)


def _rms_fwd(name, x, w, tm, comm=None):
    S, D = x.shape
    tm = _tile(S, tm)

    def body(x_ref, w_ref, o_ref):
        xv = x_ref[...]
        r = lax.rsqrt(jnp.mean(xv * xv, axis=-1, keepdims=True) + EPS)
        o_ref[...] = ((xv * r) * w_ref[...]).astype(BF16)

    row = pl.BlockSpec((tm, D), lambda i: (i, 0))
    return _pcall(body, name=name, grid=(S // tm,), in_specs=[row, pl.BlockSpec((1, D), lambda i: (0, 0))], out_specs=[row],
                  out_shape=[jax.ShapeDtypeStruct((S, D), BF16)], operands=[x, w], comm=comm)


def _rms_bwd(name, x, w, dy, dres, tm, comm=None):
    S, D = x.shape
    tm = _tile(S, tm)

    def body(x_ref, w_ref, dy_ref, dres_ref, dx_ref, dw_ref):
        i = pl.program_id(0)

        @pl.when(i == 0)
        def _():
            dw_ref[...] = jnp.zeros_like(dw_ref)

        xv = x_ref[...]
        r = lax.rsqrt(jnp.mean(xv * xv, axis=-1, keepdims=True) + EPS)
        nv = xv * r
        dyv = dy_ref[...]
        dn = dyv * w_ref[...]
        dw_ref[...] += jnp.sum(dyv * nv, axis=0, keepdims=True)
        dx = dres_ref[...] + r * (dn - nv * jnp.mean(dn * nv, axis=-1, keepdims=True))
        dx_ref[...] = dx

    row = pl.BlockSpec((tm, D), lambda i: (i, 0))
    vec = pl.BlockSpec((1, D), lambda i: (0, 0))
    return _pcall(body, name=name, grid=(S // tm,), in_specs=[row, vec, row, row], out_specs=[row, vec],
                  out_shape=[jax.ShapeDtypeStruct((S, D), F32), jax.ShapeDtypeStruct((1, D), F32)],
                  operands=[x, w, dy, dres], comm=comm)


def _rms_bwd_dy(h1, w, du2, dh2, w_out, tm, comm=None):
    S, D = h1.shape
    d_mix = w_out.shape[0]
    tm = _tile(S, tm)

    def body(x_ref, w_ref, dy_ref, dres_ref, wo_ref, dx_ref, dxb_ref, out_ref, dw_ref):
        i = pl.program_id(0)

        @pl.when(i == 0)
        def _():
            dw_ref[...] = jnp.zeros_like(dw_ref)

        xv = x_ref[...]
        r = lax.rsqrt(jnp.mean(xv * xv, axis=-1, keepdims=True) + EPS)
        nv = xv * r
        dyv = dy_ref[...]
        dn = dyv * w_ref[...]
        dw_ref[...] += jnp.sum(dyv * nv, axis=0, keepdims=True)
        dx = dres_ref[...] + r * (dn - nv * jnp.mean(dn * nv, axis=-1, keepdims=True))
        dx_ref[...] = dx
        dxb = dx.astype(BF16)
        dxb_ref[...] = dxb
        out_ref[...] = _dot(dxb, wo_ref[...], NT)

    row = pl.BlockSpec((tm, D), lambda i: (i, 0))
    vec = pl.BlockSpec((1, D), lambda i: (0, 0))
    return _pcall(
        body, name="rms2_bwd_dy", grid=(S // tm,),
        in_specs=[row, vec, row, row, pl.BlockSpec((d_mix, D), lambda i: (0, 0), pipeline_mode=pl.Buffered(1))],
        out_specs=[row, row, pl.BlockSpec((tm, d_mix), lambda i: (i, 0)), vec],
        out_shape=[jax.ShapeDtypeStruct((S, D), F32), jax.ShapeDtypeStruct((S, D), BF16),
                   jax.ShapeDtypeStruct((S, d_mix), F32), jax.ShapeDtypeStruct((1, D), F32)],
        operands=[h1, w, du2, dh2, w_out], comm=comm)


def _out_proj_rms(y, w_out, x, ln_w, tm, comm=None):
    S, K = y.shape
    D = w_out.shape[1]
    tm = _tile(S, tm)

    def body(a_ref, w_ref, x_ref, lw_ref, h_ref, u_ref):
        hv = _dot(a_ref[...], w_ref[...]) + x_ref[...]
        h_ref[...] = hv
        r = lax.rsqrt(jnp.mean(hv * hv, axis=-1, keepdims=True) + EPS)
        u_ref[...] = ((hv * r) * lw_ref[...]).astype(BF16)

    row = pl.BlockSpec((tm, D), lambda i: (i, 0))
    return _pcall(
        body, name="out_proj", grid=(S // tm,),
        in_specs=[pl.BlockSpec((tm, K), lambda i: (i, 0)),
                  pl.BlockSpec((K, D), lambda i: (0, 0), pipeline_mode=pl.Buffered(1)), row,
                  pl.BlockSpec((1, D), lambda i: (0, 0))],
        out_specs=[row, row], out_shape=[jax.ShapeDtypeStruct((S, D), F32), jax.ShapeDtypeStruct((S, D), BF16)],
        operands=[y, w_out, x, ln_w], comm=comm)


def _ffn_down_loss(ff, wd, h1, tgt, fw, tm):
    S, K = ff.shape
    D = wd.shape[1]
    tm = _tile(S, tm)

    def body(a_ref, wd_ref, h1_ref, t_ref, w_ref, dh_ref, dhb_ref, dw_ref, loss_ref):
        i = pl.program_id(0)

        @pl.when(i == 0)
        def _():
            dw_ref[...] = jnp.zeros_like(dw_ref)
            loss_ref[...] = jnp.zeros_like(loss_ref)

        hv = _dot(a_ref[...], wd_ref[...]) + h1_ref[...]
        wv = w_ref[...]
        r = lax.rsqrt(jnp.mean(hv * hv, axis=-1, keepdims=True) + EPS)
        nv = hv * r
        err = nv * wv - t_ref[...]
        row_loss = jnp.mean(err * err, axis=-1, keepdims=True)
        loss_ref[...] += 0.5 * jnp.sum(row_loss, axis=0, keepdims=True)
        dyo = err * (1.0 / D)
        dn = dyo * wv
        dw_ref[...] += jnp.sum(dyo * nv, axis=0, keepdims=True)
        dh = r * (dn - nv * jnp.mean(dn * nv, axis=-1, keepdims=True))
        dh_ref[...] = dh
        dhb_ref[...] = dh.astype(BF16)

    row = pl.BlockSpec((tm, D), lambda i: (i, 0))
    vec = pl.BlockSpec((1, D), lambda i: (0, 0))
    return _pcall(
        body, name="ffn_down_loss", grid=(S // tm,),
        in_specs=[pl.BlockSpec((tm, K), lambda i: (i, 0)),
                  pl.BlockSpec((K, D), lambda i: (0, 0), pipeline_mode=pl.Buffered(1)), row, row, vec],
        out_specs=[row, row, vec, pl.BlockSpec((1, LANES), lambda i: (0, 0))],
        out_shape=[jax.ShapeDtypeStruct((S, D), F32), jax.ShapeDtypeStruct((S, D), BF16),
                   jax.ShapeDtypeStruct((1, D), F32), jax.ShapeDtypeStruct((1, LANES), F32)],
        operands=[ff, wd, h1, tgt, fw])


def _shift_down(x, d, head8):
    r = pltpu.roll(x, d, 0)
    rh = pltpu.roll(head8, d, 0)
    row8 = lax.broadcasted_iota(jnp.int32, head8.shape, 0)
    top = jnp.where(row8 < d, rh, r[0:8])
    return jnp.concatenate([top, r[8:]], axis=0)


def _shift_up(x, d, tail8):
    n = x.shape[0]
    r = pltpu.roll(x, n - d, 0)
    rt = pltpu.roll(tail8, 8 - d, 0)
    row8 = lax.broadcasted_iota(jnp.int32, tail8.shape, 0)
    bot = jnp.where(row8 + d >= 8, rt, r[n - 8:n])
    return jnp.concatenate([r[:n - 8], bot], axis=0)


def _roll_in_groups(x, d):
    n, c = x.shape
    return pltpu.roll(x.reshape(n // 8, 8, c), d, 1).reshape(n, c)


def _log_sigmoid(lam):
    z = jnp.exp(-jnp.abs(lam))
    u = 1.0 + z
    log1p = jnp.where(u == 1.0, z, jnp.log(u) * (z / jnp.where(u == 1.0, 1.0, u - 1.0)))
    return jnp.minimum(lam, 0.0) - log1p


def _neg_expm1(z, exp_z):
    series = -z * (1.0 + z * (0.5 + z * (1.0 / 6.0)))
    return jnp.where(z > -0.02, series, 1.0 - exp_z)


_GELU_C = 0.7978845608028654


def _gelu(x):
    t = jnp.tanh(_GELU_C * (x + 0.044715 * (x * x * x)))
    return x * (0.5 * (1.0 + t)), t


def _gelu_grad(x, t):
    return 0.5 * (1.0 + t) + 0.5 * x * (1.0 - t * t) * (_GELU_C * (1.0 + 3.0 * 0.044715 * (x * x)))


def _lx_shifts(lx, head8):
    return [lx] + [_shift_down(lx, d, head8) for d in (1, 2, 3)]


def _lru_gates(lx, head8, cw, cb, wa_ref, ba, wx_ref, bx, ls):
    nb = wa_ref.shape[0]
    sh = _lx_shifts(lx, head8)
    cx = cb + sh[3] * cw[0:1]
    cx = cx + sh[2] * cw[1:2]
    cx = cx + sh[1] * cw[2:3]
    cx = cx + sh[0] * cw[3:4]
    cxb = cx.astype(BF16)
    ra = jnp.concatenate([_dot(cxb[:, n * HEAD_DIM:(n + 1) * HEAD_DIM], wa_ref[n]) for n in range(nb)], axis=1) + ba
    ia = jnp.concatenate([_dot(cxb[:, n * HEAD_DIM:(n + 1) * HEAD_DIM], wx_ref[n]) for n in range(nb)], axis=1) + bx
    r = jax.nn.sigmoid(ra)
    ig = jax.nn.sigmoid(ia)
    log_a = LRU_C * r * ls
    a = jnp.exp(log_a)
    return cx, r, ig, a, jnp.sqrt(_neg_expm1(2.0 * log_a, a * a))


def _lru_specs(tl, DL):
    nb = DL // HEAD_DIM
    vec = pl.BlockSpec((1, DL), lambda i: (0, 0))
    return [pl.BlockSpec((CONV_W, DL), lambda i: (0, 0)), vec,
            pl.BlockSpec((nb, HEAD_DIM, HEAD_DIM), lambda i: (0, 0, 0)), vec,
            pl.BlockSpec((nb, HEAD_DIM, HEAD_DIM), lambda i: (0, 0, 0)), vec, vec]


def _lru_fwd(proj, cw, cb, wa, ba, wx, bx, lam, tl, d_mix, comm=None):
    S = proj.shape[0]
    DL = cb.shape[1]
    tl = _tile(S, tl)

    def body(lx_ref, lg_ref, cw_ref, cb_ref, wa_ref, ba_ref, wx_ref, bx_ref, lam_ref, h_ref, kept_ref, y_ref,
             prev8, hc, a_s, b_s):
        i = pl.program_id(0)

        @pl.when(i == 0)
        def _():
            prev8[...] = jnp.zeros_like(prev8)
            hc[...] = jnp.zeros_like(hc)

        lx = lx_ref[...]
        ls = _log_sigmoid(lam_ref[...])
        kept = _lru_gates(lx, prev8[...], cw_ref[...], cb_ref[...], wa_ref, ba_ref[...], wx_ref, bx_ref[...], ls)
        for n, val in enumerate(kept):
            kept_ref[:, n * DL:(n + 1) * DL] = val
        cx, _, ig, a, mult = kept
        b = mult * (ig * cx)
        row = lax.broadcasted_iota(jnp.int32, a.shape, 0) & 7
        for d in (1, 2, 4):
            a_sh = _roll_in_groups(a, d)
            b_sh = _roll_in_groups(b, d)
            m = row >= d
            b = jnp.where(m, a * b_sh + b, b)
            a = jnp.where(m, a * a_sh, a)
        a_s[...] = a
        b_s[...] = b

        def step(g, hprev):
            sl = pl.ds(pl.multiple_of(g * 8, 8), 8)
            hh = a_s[sl, :] * hprev + b_s[sl, :]
            h_ref[sl, :] = hh
            return hh[7:8, :]

        hc[0:1, :] = lax.fori_loop(0, tl // 8, step, hc[0:1, :])
        prev8[...] = lx[tl - 8:tl]
        g, _ = _gelu(lg_ref[...])
        y_ref[...] = (h_ref[...] * g).astype(BF16)

    return _pcall(
        body, name="lru_fwd", grid=(S // tl,),
        in_specs=[pl.BlockSpec((tl, DL), lambda i: (i, 0)), pl.BlockSpec((tl, DL), lambda i: (i, 1))] + _lru_specs(tl, DL),
        out_specs=[pl.BlockSpec((tl, DL), lambda i: (i, 0)), pl.BlockSpec((tl, 5 * DL), lambda i: (i, 0)),
                   pl.BlockSpec((tl, DL), lambda i: (i, 0))],
        out_shape=[jax.ShapeDtypeStruct((S, DL), F32), jax.ShapeDtypeStruct((S, 5 * DL), F32),
                   jax.ShapeDtypeStruct((S, d_mix), BF16)],
        scratch_shapes=[pltpu.VMEM((8, DL), F32), pltpu.VMEM((8, DL), F32), pltpu.VMEM((tl, DL), F32), pltpu.VMEM((tl, DL), F32)],
        operands=[proj, proj, cw, cb, wa, ba, wx, bx, lam], comm=comm)


def _lru_bwd(proj, h, kept, dy, cw, wa, wx, lam, tl, comm=None):
    S = proj.shape[0]
    DL = lam.shape[1]
    nb = DL // HEAD_DIM
    tl = _tile(S, tl)
    nt = S // tl
    ng = tl // 8
    t8 = tl // 8

    def body(lx_ref, lxp_ref, lg_ref, h_ref, hp_ref, kept_ref, dy_ref, cw_ref, wa_ref, wx_ref, lam_ref,
             dlxg_ref, dcw_ref, dcb_ref, dwa_ref, dba_ref, dwx_ref, dbx_ref, dlam_ref,
             a_next, g_carry, dcx_next, an_s, dh_s, g_s):
        i = pl.program_id(0)

        @pl.when(i == 0)
        def _():
            for ref in (dcw_ref, dcb_ref, dwa_ref, dba_ref, dwx_ref, dbx_ref, dlam_ref, a_next, g_carry, dcx_next):
                ref[...] = jnp.zeros_like(ref)

        first = i == nt - 1
        hv = h_ref[...]
        lg = lg_ref[...]
        dyv = dy_ref[...]
        hhead8 = jnp.where(first, 0.0, hp_ref[...])
        lamv = lam_ref[...]
        ls = _log_sigmoid(lamv)
        cwv = cw_ref[...]
        sh = _lx_shifts(lx_ref[...], jnp.where(first, 0.0, lxp_ref[...]))
        cx, r, ig, a, mult = (kept_ref[:, n * DL:(n + 1) * DL] for n in range(5))
        cxb = cx.astype(BF16)
        hprev = _shift_down(hv, 1, hhead8)
        g, t = _gelu(lg)
        dlg = dyv * hv * _gelu_grad(lg, t)
        dh = dyv * g
        an = _shift_up(a, 1, a_next[...])
        row = lax.broadcasted_iota(jnp.int32, a.shape, 0) & 7
        for d in (1, 2, 4):
            an_sh = _roll_in_groups(an, 8 - d)
            dh_sh = _roll_in_groups(dh, 8 - d)
            m = row + d < 8
            dh = jnp.where(m, an * dh_sh + dh, dh)
            an = jnp.where(m, an * an_sh, an)
        an_s[...] = an
        dh_s[...] = dh

        def step(k, gc):
            sl = pl.ds(pl.multiple_of((ng - 1 - k) * 8, 8), 8)
            gg = an_s[sl, :] * gc + dh_s[sl, :]
            g_s[sl, :] = gg
            return gg[0:1, :]

        g_carry[0:1, :] = lax.fori_loop(0, ng, step, g_carry[0:1, :])
        a_next[...] = a[0:8]
        G = g_s[...]
        da = G * hprev
        icx = ig * cx
        dmult = G * icx
        dicx = G * mult
        di = dicx * cx
        dcx = dicx * ig
        dlog = da * a - dmult * ((a * a) * lax.rsqrt(mult * mult))
        dr = dlog * (LRU_C * ls)
        dlam_ref[...] += jnp.sum(dlog * (LRU_C * r), axis=0, keepdims=True)
        dra = dr * r * (1.0 - r)
        dia = di * ig * (1.0 - ig)
        dba_ref[...] += jnp.sum(dra, axis=0, keepdims=True)
        dbx_ref[...] += jnp.sum(dia, axis=0, keepdims=True)
        drab = dra.astype(BF16)
        diab = dia.astype(BF16)
        back = []
        for n in range(nb):
            cs = slice(n * HEAD_DIM, (n + 1) * HEAD_DIM)
            dwa_ref[n] += _dot(cxb[:, cs], drab[:, cs], TN)
            dwx_ref[n] += _dot(cxb[:, cs], diab[:, cs], TN)
            back.append(_dot(drab[:, cs], wa_ref[n], NT) + _dot(diab[:, cs], wx_ref[n], NT))
        dcx = dcx + jnp.concatenate(back, axis=1)
        dcb_ref[...] += jnp.sum(dcx, axis=0, keepdims=True)
        for tap in range(CONV_W):
            dcw_ref[tap:tap + 1, :] += jnp.sum(dcx * sh[CONV_W - 1 - tap], axis=0, keepdims=True)
        tail = dcx_next[...]
        dlx = dcx * cwv[3:4]
        for d in (1, 2, 3):
            dlx = dlx + _shift_up(dcx, d, tail) * cwv[3 - d:4 - d]
        dcx_next[...] = dcx[0:8]
        dlxg_ref[:, 0:DL] = dlx.astype(BF16)
        dlxg_ref[:, DL:2 * DL] = dlg.astype(BF16)

        @pl.when(i == nt - 1)
        def _():
            dlam_ref[...] = dlam_ref[...] * (1.0 - jax.nn.sigmoid(lamv))

    rev = lambda i: nt - 1 - i
    prev8_map = lambda i: (jnp.maximum((nt - 1 - i) * t8 - 1, 0), 0)
    vec = pl.BlockSpec((1, DL), lambda i: (0, 0))
    mat = pl.BlockSpec((nb, HEAD_DIM, HEAD_DIM), lambda i: (0, 0, 0))
    return _pcall(
        body, name="lru_bwd", grid=(nt,), operands=[proj, proj, proj, h, h, kept, dy, cw, wa, wx, lam], comm=comm,
        in_specs=[pl.BlockSpec((tl, DL), lambda i: (rev(i), 0)), pl.BlockSpec((8, DL), prev8_map),
                  pl.BlockSpec((tl, DL), lambda i: (rev(i), 1)),
                  pl.BlockSpec((tl, DL), lambda i: (rev(i), 0)), pl.BlockSpec((8, DL), prev8_map),
                  pl.BlockSpec((tl, 5 * DL), lambda i: (rev(i), 0)),
                  pl.BlockSpec((tl, DL), lambda i: (rev(i), 0)), pl.BlockSpec((CONV_W, DL), lambda i: (0, 0)), mat, mat, vec],
        out_specs=[pl.BlockSpec((tl, 2 * DL), lambda i: (rev(i), 0)), pl.BlockSpec((CONV_W, DL), lambda i: (0, 0)), vec,
                   mat, vec, mat, vec, vec],
        out_shape=[jax.ShapeDtypeStruct(proj.shape, BF16), jax.ShapeDtypeStruct((CONV_W, DL), F32),
                   jax.ShapeDtypeStruct((1, DL), F32), jax.ShapeDtypeStruct((nb, HEAD_DIM, HEAD_DIM), F32),
                   jax.ShapeDtypeStruct((1, DL), F32), jax.ShapeDtypeStruct((nb, HEAD_DIM, HEAD_DIM), F32),
                   jax.ShapeDtypeStruct((1, DL), F32), jax.ShapeDtypeStruct((1, DL), F32)],
        scratch_shapes=[pltpu.VMEM((8, DL), F32), pltpu.VMEM((8, DL), F32), pltpu.VMEM((8, DL), F32),
                        pltpu.VMEM((tl, DL), F32), pltpu.VMEM((tl, DL), F32), pltpu.VMEM((tl, DL), F32)])


def _ret_tables(S, H):
    pos = jnp.arange(S, dtype=F32)
    inv_freq = ROPE_BASE ** (-jnp.arange(0, HEAD_DIM, 2, dtype=F32) / HEAD_DIM)
    ang = pos[:, None] * inv_freq[None, :]
    cos, sin = jnp.cos(ang), jnp.sin(ang)
    cosf = jnp.concatenate([cos, cos], axis=1)
    sins = jnp.concatenate([-sin, sin], axis=1)
    log_gamma = jnp.log1p(-jnp.exp2(-5.0 - jnp.arange(H, dtype=F32)))
    idx = jnp.arange(CHUNK)
    diff = idx[:, None] - idx[None, :]
    causal = diff >= 0
    decay = jnp.where(causal[None], jnp.exp(log_gamma[:, None, None] * jnp.where(causal, diff, 0)[None].astype(F32)), 0.0)
    zeta = jnp.exp(log_gamma[:, None] * (CHUNK - 1 - idx).astype(F32)[None, :])
    xi = jnp.exp(log_gamma[:, None] * (idx + 1).astype(F32)[None, :])
    gc = jnp.exp(log_gamma * CHUNK)
    lanes = (H, CHUNK, HEAD_DIM)
    return (cosf, sins, decay, jnp.broadcast_to(zeta[:, :, None], lanes), jnp.broadcast_to(xi[:, :, None], lanes),
            jnp.broadcast_to(gc[:, None, None], lanes))


def _rope(t, cos, sin_signed):
    return t * cos + pltpu.roll(t, HEAD_DIM // 2, 1) * sin_signed


def _rope_t(d, cos, sin_signed):
    return d * cos + pltpu.roll(d * sin_signed, HEAD_DIM // 2, 1)


def _ret_const_specs(H, DR):
    full = pl.BlockSpec((H, CHUNK, HEAD_DIM), lambda *_: (0, 0, 0))
    return [full, full, full, full, pl.BlockSpec((1, DR), lambda *_: (0, 0))]


def _ret_fwd(proj, y, tables, gnw, tb, comm=None):
    S = proj.shape[0]
    DR = gnw.shape[1]
    H = DR // HEAD_DIM
    tb = _tile(S, tb, CHUNK)
    nc = tb // CHUNK
    cosf, sins, dm, zeta, xi, gc = tables
    scale = HEAD_DIM ** -0.5

    def body(qk_ref, vg_ref, cos_ref, sin_ref, dm_ref, zeta_ref, xi_ref, gc_ref, gnw_ref, y_in, y_ref, rprev_ref, r_s):
        del y_in
        i = pl.program_id(0)

        @pl.when(i == 0)
        def _():
            r_s[...] = jnp.zeros_like(r_s)

        def chunk(c, carry):
            rows = pl.ds(pl.multiple_of(c * CHUNK, CHUNK), CHUNK)
            cos = cos_ref[rows, :]
            sin = sin_ref[rows, :]
            heads = range(H)
            c0 = [slice(h * HEAD_DIM, (h + 1) * HEAD_DIM) for h in heads]
            c1 = [slice(DR + h * HEAD_DIM, DR + (h + 1) * HEAD_DIM) for h in heads]
            qh = [_rope(qk_ref[rows, c0[h]], cos, sin) for h in heads]
            kh = [_rope(qk_ref[rows, c1[h]], cos, sin) * scale for h in heads]
            vb = [vg_ref[rows, c0[h]].astype(BF16) for h in heads]
            rp = [r_s[h] for h in heads]
            rpb = [rp[h].astype(BF16) for h in heads]
            s = [_dot(qh[h].astype(BF16), kh[h].astype(BF16), NT) for h in heads]
            kv = [_dot((kh[h] * zeta_ref[h]).astype(BF16), vb[h], TN) for h in heads]
            cross = [_dot((qh[h] * xi_ref[h]).astype(BF16), rpb[h]) for h in heads]
            o = [_dot((s[h] * dm_ref[h]).astype(BF16), vb[h]) + cross[h] for h in heads]
            for h in heads:
                rprev_ref[c, h] = rpb[h]
                r_s[h] = rp[h] * gc_ref[h] + kv[h]
                mu = jnp.mean(o[h], axis=-1, keepdims=True)
                oc = o[h] - mu
                var = jnp.mean(oc * oc, axis=-1, keepdims=True)
                on = oc * lax.rsqrt(var + EPS) * gnw_ref[:, c0[h]]
                gate = vg_ref[rows, c1[h]]
                y_ref[rows, c0[h]] = (gate * jax.nn.sigmoid(gate) * on).astype(BF16)
            return carry

        lax.fori_loop(0, nc, chunk, 0)

    return _pcall(
        body, name="ret_fwd", grid=(S // tb,),
        in_specs=[pl.BlockSpec((tb, 2 * DR), lambda i: (i, 1)), pl.BlockSpec((tb, 2 * DR), lambda i: (i, 2)),
                  pl.BlockSpec((tb, HEAD_DIM), lambda i: (i, 0)), pl.BlockSpec((tb, HEAD_DIM), lambda i: (i, 0))]
        + _ret_const_specs(H, DR) + [HBM_SPEC],
        out_specs=[pl.BlockSpec((tb, DR), lambda i: (i, 1)),
                   pl.BlockSpec((nc, H, CHUNK, HEAD_DIM), lambda i: (i, 0, 0, 0))],
        out_shape=[jax.ShapeDtypeStruct(y.shape, BF16), jax.ShapeDtypeStruct((S // CHUNK, H, CHUNK, HEAD_DIM), BF16)],
        scratch_shapes=[pltpu.VMEM((H, CHUNK, HEAD_DIM), F32)], aliases={9: 0},
        operands=[proj, proj, cosf, sins, dm, zeta, xi, gc, gnw, y], comm=comm)


def _ret_bwd(proj, rprev, dy, dproj, tables, gnw, tb, comm=None):
    S = proj.shape[0]
    DR = gnw.shape[1]
    H = DR // HEAD_DIM
    tb = _tile(S, tb, CHUNK)
    nc = tb // CHUNK
    nt = S // tb
    cosf, sins, dm, zeta, xi, gc = tables
    scale = HEAD_DIM ** -0.5

    def body(qk_ref, vg_ref, cos_ref, sin_ref, dm_ref, zeta_ref, xi_ref, gc_ref, gnw_ref, rprev_ref, dy_ref, dp_in,
             dp_ref, dgn_ref, dr_s, dqk_s, dvg_s, out_sems):
        del dp_in
        i = pl.program_id(0)
        slot = i % 2

        def out_copies(step, sl):
            rows = pl.ds(pl.multiple_of((nt - 1 - step) * tb, tb), tb)
            return (pltpu.make_async_copy(dqk_s.at[sl], dp_ref.at[rows, pl.ds(2 * DR, 2 * DR)], out_sems.at[sl, 0]),
                    pltpu.make_async_copy(dvg_s.at[sl], dp_ref.at[rows, pl.ds(4 * DR, 2 * DR)], out_sems.at[sl, 1]))

        @pl.when(i == 0)
        def _():
            dr_s[...] = jnp.zeros_like(dr_s)
            dgn_ref[...] = jnp.zeros_like(dgn_ref)

        @pl.when(i >= 2)
        def _():
            for cp in out_copies(i - 2, slot):
                cp.wait()

        def chunk(cc, carry):
            c = nc - 1 - cc
            rows = pl.ds(pl.multiple_of(c * CHUNK, CHUNK), CHUNK)
            cos = cos_ref[rows, :]
            sin = sin_ref[rows, :]
            heads = range(H)
            c0 = [slice(h * HEAD_DIM, (h + 1) * HEAD_DIM) for h in heads]
            c1 = [slice(DR + h * HEAD_DIM, DR + (h + 1) * HEAD_DIM) for h in heads]
            qh = [_rope(qk_ref[rows, c0[h]], cos, sin) for h in heads]
            kh = [_rope(qk_ref[rows, c1[h]], cos, sin) * scale for h in heads]
            qb = [t.astype(BF16) for t in qh]
            kb = [t.astype(BF16) for t in kh]
            vb = [vg_ref[rows, c0[h]].astype(BF16) for h in heads]
            rpb = [rprev_ref[c, h] for h in heads]
            qx = [(qh[h] * xi_ref[h]).astype(BF16) for h in heads]
            kz = [(kh[h] * zeta_ref[h]).astype(BF16) for h in heads]
            drh = [dr_s[h] for h in heads]
            drb = [t.astype(BF16) for t in drh]
            s = [_dot(qb[h], kb[h], NT) for h in heads]
            cross = [_dot(qx[h], rpb[h]) for h in heads]
            dv_state = [_dot(kz[h], drb[h]) for h in heads]
            dk_state = [_dot(vb[h], drb[h], NT) for h in heads]
            sb = [(s[h] * dm_ref[h]).astype(BF16) for h in heads]
            o = [_dot(sb[h], vb[h]) + cross[h] for h in heads]
            dob = []
            for h in heads:
                mu = jnp.mean(o[h], axis=-1, keepdims=True)
                oc = o[h] - mu
                rstd = lax.rsqrt(jnp.mean(oc * oc, axis=-1, keepdims=True) + EPS)
                ohat = oc * rstd
                gw = gnw_ref[:, c0[h]]
                gate = vg_ref[rows, c1[h]]
                sg = jax.nn.sigmoid(gate)
                dyv = dy_ref[rows, c0[h]]
                dvg_s[slot, rows, c1[h]] = (dyv * (ohat * gw) * (sg * (1.0 + gate * (1.0 - sg)))).astype(BF16)
                don = dyv * (gate * sg)
                dgn_ref[:, c0[h]] += jnp.sum(don * ohat, axis=0, keepdims=True)
                dohat = don * gw
                do = rstd * (dohat - jnp.mean(dohat, axis=-1, keepdims=True)
                             - ohat * jnp.mean(dohat * ohat, axis=-1, keepdims=True))
                dob.append(do.astype(BF16))
            ds = [_dot(dob[h], vb[h], NT) for h in heads]
            dq_state = [_dot(dob[h], rpb[h], NT) for h in heads]
            dv = [_dot(sb[h], dob[h], TN) + dv_state[h] for h in heads]
            dr_new = [_dot(qx[h], dob[h], TN) for h in heads]
            dsb = [(ds[h] * dm_ref[h]).astype(BF16) for h in heads]
            dqh = [_dot(dsb[h], kb[h]) + dq_state[h] * xi_ref[h] for h in heads]
            dkh = [_dot(dsb[h], qb[h], TN) + dk_state[h] * zeta_ref[h] for h in heads]
            for h in heads:
                dr_s[h] = drh[h] * gc_ref[h] + dr_new[h]
                dqk_s[slot, rows, c0[h]] = _rope_t(dqh[h], cos, sin).astype(BF16)
                dqk_s[slot, rows, c1[h]] = _rope_t(dkh[h] * scale, cos, sin).astype(BF16)
                dvg_s[slot, rows, c0[h]] = dv[h].astype(BF16)
            return carry

        lax.fori_loop(0, nc, chunk, 0)
        for cp in out_copies(i, slot):
            cp.start()

        @pl.when(i == nt - 1)
        def _():
            if nt >= 2:
                for cp in out_copies(i - 1, 1 - slot):
                    cp.wait()
            for cp in out_copies(i, slot):
                cp.wait()

    rev = lambda i: nt - 1 - i
    return _pcall(
        body, name="ret_bwd", grid=(nt,), aliases={11: 0}, comm=comm,
        operands=[proj, proj, cosf, sins, dm, zeta, xi, gc, gnw, rprev, dy, dproj],
        in_specs=[pl.BlockSpec((tb, 2 * DR), lambda i: (rev(i), 1)), pl.BlockSpec((tb, 2 * DR), lambda i: (rev(i), 2)),
                  pl.BlockSpec((tb, HEAD_DIM), lambda i: (rev(i), 0)), pl.BlockSpec((tb, HEAD_DIM), lambda i: (rev(i), 0))]
        + _ret_const_specs(H, DR)
        + [pl.BlockSpec((nc, H, CHUNK, HEAD_DIM), lambda i: (rev(i), 0, 0, 0)),
           pl.BlockSpec((tb, DR), lambda i: (rev(i), 1)), HBM_SPEC],
        out_specs=[HBM_SPEC, pl.BlockSpec((1, DR), lambda i: (0, 0))],
        out_shape=[jax.ShapeDtypeStruct(dproj.shape, BF16), jax.ShapeDtypeStruct((1, DR), F32)],
        scratch_shapes=[pltpu.VMEM((H, CHUNK, HEAD_DIM), F32), pltpu.VMEM((2, tb, 2 * DR), BF16),
                        pltpu.VMEM((2, tb, 2 * DR), BF16), pltpu.SemaphoreType.DMA((2, 2))])


def _place():
    x, y, c = lax.axis_index("x"), lax.axis_index("y"), lax.axis_index("c")
    chips = [(1 - x, y), (x, 1 - y), (1 - x, 1 - y)]
    return x, y, c, chips


def _own_slab(name, shard, place, side_by_side=False):
    R, C = shard.shape
    tr = _row_tile(R, C, 2 * ELEMENTWISE_BLOCK)
    if side_by_side:
        out = (jax.ShapeDtypeStruct((R, 4 * C), BF16), pl.BlockSpec((tr, C), lambda i, p: (i, p[1])))
    else:
        out = (jax.ShapeDtypeStruct((4, R, C), BF16), pl.BlockSpec((None, tr, C), lambda i, p: (p[1], i, 0)))
    return _ew("cast_" + name, lambda a: (a,), [(shard, pl.BlockSpec((tr, C), lambda i, p: (i, 0)))], [out],
               (R // tr,), sp=place)[0]


def _slab_half(ref, chip, half, cols):
    if cols is None:
        r2 = ref.shape[1] // 2
        return ref.at[chip, pl.ds(half * r2, r2), :]
    r2 = ref.shape[0] // 2
    return ref.at[pl.ds(half * r2, r2), pl.ds(pl.multiple_of(chip * cols, LANES), cols)]


class _remote:
    def __init__(self, src, dst, ssem, rsem, k, to):
        self.args = dict(src_ref=src, dst_ref=dst, send_sem=ssem.at[k], recv_sem=rsem.at[k], device_id=to,
                         device_id_type=MESH)

    def start(self):
        pltpu.make_async_remote_copy(**self.args).start()

    def wait_send(self):
        pltpu.make_async_remote_copy(**self.args).wait_send()

    def wait_recv(self):
        pltpu.make_async_remote_copy(**self.args).wait_recv()


def _task_fns(copies):
    def start(cins, couts, ssem, rsem, base):
        for cp in copies(cins, couts, ssem, rsem, base)[0]:
            cp.start()

    def finish(cins, couts, ssem, rsem, base):
        sends, recvs = copies(cins, couts, ssem, rsem, base)
        for cp in sends:
            cp.wait_send()
        for cp in recvs:
            cp.wait_recv()

    return start, finish


NEIGHBOURS, DIAGONAL = (0, 1), (2,)


def _gather_ici(st, which=NEIGHBOURS + DIAGONAL, cols=None):
    def copies(cins, couts, ssem, rsem, base):
        x, y, c, chips = _place()
        out = couts[0]
        mine = _slab_half(out, 2 * x + y, c, cols)
        sends, recvs = [], []
        for k, j in enumerate(which):
            cx, cy = chips[j]
            got = _slab_half(out, 2 * cx + cy, c, cols)
            sends.append(_remote(mine, mine, ssem, rsem, base + k, (cx, cy, c)))
            recvs.append(_remote(got, got, ssem, rsem, base + k, (x, y, c)))
        return sends, recvs

    start, finish = _task_fns(copies)
    return _Comm([st], [jax.ShapeDtypeStruct(st.shape, st.dtype)], {0: 0}, len(which), start, finish)


def _gather_d2d(st, cols=None):
    def copies(cins, couts, ssem, rsem, base):
        x, y, c, chips = _place()
        out = couts[0]
        sends, recvs = [], []
        for j, (cx, cy) in enumerate(chips):
            have = _slab_half(out, 2 * cx + cy, c, cols)
            want = _slab_half(out, 2 * cx + cy, 1 - c, cols)
            sends.append(_remote(have, have, ssem, rsem, base + j, (x, y, 1 - c)))
            recvs.append(_remote(want, want, ssem, rsem, base + j, (x, y, c)))
        return sends, recvs

    start, finish = _task_fns(copies)
    return _Comm([st], [jax.ShapeDtypeStruct(st.shape, st.dtype)], {0: 0}, 3, start, finish)


def _gather_conv(conv_w):
    def copies(cins, couts, ssem, rsem, base):
        x, y, c, chips = _place()
        src, out = cins[0], couts[0]
        sends = [_remote(src, out.at[2 * x + y], ssem, rsem, base + j, (*chip, c)) for j, chip in enumerate(chips)]
        recvs = [_remote(src, out.at[2 * cx + cy], ssem, rsem, base + j, (x, y, c)) for j, (cx, cy) in enumerate(chips)]
        return sends, recvs

    start, finish = _task_fns(copies)
    return _Comm([conv_w], [jax.ShapeDtypeStruct((4,) + conv_w.shape, conv_w.dtype)], {}, 3, start, finish)


def _pair_exchange(g):
    r2 = g.shape[1] // 2

    def copies(cins, couts, ssem, rsem, base):
        x, y, c, _ = _place()
        cp = _remote(cins[0].at[:, pl.ds((1 - c) * r2, r2), :], couts[0], ssem, rsem, base, (x, y, 1 - c))
        return [cp], [cp]

    start, finish = _task_fns(copies)
    return _Comm([g], [jax.ShapeDtypeStruct((g.shape[0], r2, g.shape[2]), g.dtype)], {}, 1, start, finish)


def _chip_exchange(part):
    def copies(cins, couts, ssem, rsem, base):
        x, y, c, chips = _place()
        cps = [_remote(cins[0].at[2 * cx + cy], couts[0].at[j], ssem, rsem, base + j, (cx, cy, c))
               for j, (cx, cy) in enumerate(chips)]
        return cps, cps

    start, finish = _task_fns(copies)
    return _Comm([part], [jax.ShapeDtypeStruct((3,) + part.shape[1:], part.dtype)], {}, 3, start, finish)


def _pair_share(slot):
    def copies(cins, couts, ssem, rsem, base):
        x, y, c, _ = _place()
        out = couts[0]
        return ([_remote(out.at[c], out.at[c], ssem, rsem, base, (x, y, 1 - c))],
                [_remote(out.at[1 - c], out.at[1 - c], ssem, rsem, base, (x, y, c))])

    start, finish = _task_fns(copies)
    return _Comm([slot], [jax.ShapeDtypeStruct(slot.shape, slot.dtype)], {0: 0}, 1, start, finish)


def _gather_small(sm):
    flips = [(fx, fy, fc) for fx in (0, 1) for fy in (0, 1) for fc in (0, 1)][1:]

    def copies(cins, couts, ssem, rsem, base):
        x, y, c, _ = _place()
        src, out = cins[0], couts[0]
        peers = [(1 - x if fx else x, 1 - y if fy else y, 1 - c if fc else c) for fx, fy, fc in flips]
        sends = [_remote(src, out.at[4 * x + 2 * y + c], ssem, rsem, base + k, peer) for k, peer in enumerate(peers)]
        recvs = [_remote(src, out.at[4 * px + 2 * py + pc], ssem, rsem, base + k, (x, y, c))
                 for k, (px, py, pc) in enumerate(peers)]
        return sends, recvs

    start, finish = _task_fns(copies)
    return _Comm([sm], [jax.ShapeDtypeStruct((8,) + sm.shape, sm.dtype)], {}, 7, start, finish)


def _comm_call(name, tasks):
    task = _merge(tasks)
    nci = len(task.ins)

    def body(*refs):
        cins, couts, (ssem, rsem) = refs[:nci], refs[nci:nci + len(task.outs)], refs[nci + len(task.outs):]
        task.start(cins, couts, ssem, rsem, 0)
        task.finish(cins, couts, ssem, rsem, 0)

    return pl.pallas_call(
        body, in_specs=[HBM_SPEC] * nci, out_specs=[HBM_SPEC] * len(task.outs), out_shape=list(task.outs),
        scratch_shapes=[pltpu.SemaphoreType.DMA((task.n_sem,)), pltpu.SemaphoreType.DMA((task.n_sem,))],
        input_output_aliases=task.aliases, name=name)(*task.ins)


def _adamw(w, g, m, v):
    m = ADAM_B1 * m + (1.0 - ADAM_B1) * g
    v = ADAM_B2 * v + (1.0 - ADAM_B2) * (g * g)
    m_hat = m / (1.0 - ADAM_B1 ** ADAM_STEP)
    v_hat = v / (1.0 - ADAM_B2 ** ADAM_STEP)
    delta = -ADAM_LR * (m_hat / (jnp.sqrt(v_hat) + ADAM_EPS) + ADAM_WD * w)
    return delta, m, v


def _adamw_call(name, w, g, m, v):
    R, C = w.shape
    tr = _row_tile(R, C, ELEMENTWISE_BLOCK // 2)
    row = pl.BlockSpec((tr, C), lambda i: (i, 0))
    o = jax.ShapeDtypeStruct((R, C), F32)
    return _ew(name, lambda w_, g_, m_, v_: (*_adamw(w_, g_, m_, v_), g_), [(w, row), (g, row), (m, row), (v, row)],
               [(o, row), (o, row), (o, row), (o, row)], (R // tr,))


def _pair_sum(name, g, ra, place):
    _, R, C = g.shape
    r2 = R // 2
    tr = _row_tile(r2, C, 2 * ELEMENTWISE_BLOCK)
    nb = r2 // tr
    own = pl.BlockSpec((None, tr, C), lambda j, i, p: (j, p[0] * nb + i, 0))
    blk = pl.BlockSpec((None, tr, C), lambda j, i, p: (j, i, 0))
    return _ew("rs_pair_sum_" + name, lambda a, b: (a + b,), [(g, own), (ra, blk)],
               [(jax.ShapeDtypeStruct((4, r2, C), BF16), blk)], (4, nb), sp=place)[0]


def _chip_sum(name, g, ra, rb, place):
    _, R, C = g.shape
    r2 = R // 2
    tr = _row_tile(r2, C)
    nb = r2 // tr
    own = pl.BlockSpec((None, tr, C), lambda i, p: (p[1], p[0] * nb + i, 0))
    mine = pl.BlockSpec((None, tr, C), lambda i, p: (p[1], i, 0))
    src = [pl.BlockSpec((None, tr, C), functools.partial(lambda i, p, j: (j, i, 0), j=j)) for j in range(3)]
    out = pl.BlockSpec((None, tr, C), lambda i, p: (p[0], i, 0))

    def total(a, b, r0, r1, r2_):
        return ((((a + b) + r0.astype(F32)) + r1.astype(F32)) + r2_.astype(F32),)

    return _ew("rs_chip_sum_" + name, total, [(g, own), (ra, mine), (rb, src[0]), (rb, src[1]), (rb, src[2])],
               [(jax.ShapeDtypeStruct((2, r2, C), F32), out)], (nb,), sp=place)[0]


def _pack(arrays):
    rows, offs, pos = [], [], 0
    for a in arrays:
        flat = a.reshape(-1)
        n = -(-flat.shape[0] // (8 * LANES)) * (8 * LANES)
        if n != flat.shape[0]:
            flat = jnp.pad(flat, (0, n - flat.shape[0]))
        rows.append(flat.reshape(-1, LANES))
        offs.append(pos)
        pos += n // LANES
    return jnp.concatenate(rows, axis=0), offs


def _unpack(packed, offs, shapes):
    out = []
    for off, shp in zip(offs, shapes):
        n = 1
        for s in shp:
            n *= s
        out.append(packed[off:off + -(-n // LANES)].reshape(-1)[:n].reshape(shp))
    return out


def _sum8(gathered):
    _, R, C = gathered.shape
    tr = _row_tile(R, C, 256 * 1024)
    specs = [pl.BlockSpec((None, tr, C), functools.partial(lambda i, d: (d, i, 0), d=d)) for d in range(8)]

    def fn(*parts):
        t = parts[0]
        for p in parts[1:]:
            t = t + p
        return (t,)

    return _ew("small_sum", fn, [(gathered, s) for s in specs],
               [(jax.ShapeDtypeStruct((R, C), F32), pl.BlockSpec((tr, C), lambda i: (i, 0)))], (R // tr,))[0]


BIG = ("w_in", "w_out", "w_ffn_gate", "w_ffn_up", "w_ffn_down")
SMALL = ("ln1_w", "conv_w", "conv_b", "gate_a_w", "gate_a_b", "gate_x_w", "gate_x_b", "lru_lambda", "ret_gn_w", "ln2_w",
         "final_norm_w")
WEIGHTS = ("ln1_w", "w_in", "conv_w", "conv_b", "gate_a_w", "gate_a_b", "gate_x_w", "gate_x_b", "lru_lambda", "ret_gn_w",
           "w_out", "ln2_w", "w_ffn_gate", "w_ffn_up", "w_ffn_down", "final_norm_w")


def kernel(x, ln1_w, w_in, conv_w, conv_b, gate_a_w, gate_a_b, gate_x_w, gate_x_b, lru_lambda, ret_gn_w, w_out, ln2_w, w_ffn_gate, w_ffn_up, w_ffn_down, final_norm_w, loss_target, m_ln1_w, m_w_in, m_conv_w, m_conv_b, m_gate_a_w, m_gate_a_b, m_gate_x_w, m_gate_x_b, m_lru_lambda, m_ret_gn_w, m_w_out, m_ln2_w, m_w_ffn_gate, m_w_ffn_up, m_w_ffn_down, m_final_norm_w, v_ln1_w, v_w_in, v_conv_w, v_conv_b, v_gate_a_w, v_gate_a_b, v_gate_x_w, v_gate_x_b, v_lru_lambda, v_ret_gn_w, v_w_out, v_ln2_w, v_w_ffn_gate, v_w_ffn_up, v_w_ffn_down, v_final_norm_w):
    w = dict(ln1_w=ln1_w, w_in=w_in, conv_w=conv_w, conv_b=conv_b, gate_a_w=gate_a_w, gate_a_b=gate_a_b, gate_x_w=gate_x_w,
             gate_x_b=gate_x_b, lru_lambda=lru_lambda, ret_gn_w=ret_gn_w, w_out=w_out, ln2_w=ln2_w, w_ffn_gate=w_ffn_gate,
             w_ffn_up=w_ffn_up, w_ffn_down=w_ffn_down, final_norm_w=final_norm_w)
    m = dict(ln1_w=m_ln1_w, w_in=m_w_in, conv_w=m_conv_w, conv_b=m_conv_b, gate_a_w=m_gate_a_w, gate_a_b=m_gate_a_b,
             gate_x_w=m_gate_x_w, gate_x_b=m_gate_x_b, lru_lambda=m_lru_lambda, ret_gn_w=m_ret_gn_w, w_out=m_w_out,
             ln2_w=m_ln2_w, w_ffn_gate=m_w_ffn_gate, w_ffn_up=m_w_ffn_up, w_ffn_down=m_w_ffn_down,
             final_norm_w=m_final_norm_w)
    v = dict(ln1_w=v_ln1_w, w_in=v_w_in, conv_w=v_conv_w, conv_b=v_conv_b, gate_a_w=v_gate_a_w, gate_a_b=v_gate_a_b,
             gate_x_w=v_gate_x_w, gate_x_b=v_gate_x_b, lru_lambda=v_lru_lambda, ret_gn_w=v_ret_gn_w, w_out=v_w_out,
             ln2_w=v_ln2_w, w_ffn_gate=v_w_ffn_gate, w_ffn_up=v_w_ffn_up, w_ffn_down=v_w_ffn_down,
             final_norm_w=v_final_norm_w)
    xs, tgt = x[0], loss_target[0]
    S, D = xs.shape
    DL, DR = conv_b.shape[1], ret_gn_w.shape[1]
    assert DL == DR and DL % HEAD_DIM == 0 and S % CHUNK == 0
    d_mix = DL + DR
    cx, cy, cc = lax.axis_index("x"), lax.axis_index("y"), lax.axis_index("c")
    chip = 2 * cx + cy
    place = jnp.stack([cc, chip]).astype(jnp.int32)
    grad, delta, new_m, new_v = {}, {}, {}, {}

    def finish_big(n, full):
        shp = w[n].shape
        g2 = full.reshape(shp[1], shp[2])
        w2, m2, v2 = (t[n].reshape(shp[1], shp[2]) for t in (w, m, v))
        d_, m_, v_, g_ = _adamw_call("adamw_" + n, w2, g2, m2, v2)
        grad[n], delta[n], new_m[n], new_v[n] = (t.reshape(shp) for t in (g_, d_, m_, v_))

    def all_sum(gathered, own):
        return _sum8(lax.dynamic_update_slice(gathered, own[None], (4 * cx + 2 * cy + cc, 0, 0)))

    wide = ("w_ffn_gate", "w_ffn_up")
    st = {n: _own_slab(n, w[n][0], place, side_by_side=n in wide) for n in BIG}
    (u1,), (w_in_st,) = _rms_fwd("rms1", xs, ln1_w, TM, comm=_gather_ici(st["w_in"]))
    w_in_st, conv_st = _comm_call("gather_w_in", [_gather_d2d(w_in_st), _gather_conv(conv_w[0])])
    conv_st = lax.dynamic_update_slice(conv_st, conv_w, (chip, 0, 0))
    cw_cols = conv_st.shape[2]
    conv_full = jnp.transpose(conv_st, (1, 0, 2)).reshape(CONV_W, 4 * cw_cols)
    n_in, n_ff = w_in_st.shape[2], w_ffn_gate.shape[2]
    tables = _ret_tables(S, DR // HEAD_DIM)
    wab, wxb = gate_a_w[0].astype(BF16), gate_x_w[0].astype(BF16)
    lru_w = (conv_full, conv_b, wab, gate_a_b, wxb, gate_x_b, lru_lambda)

    proj, (w_out_st, wg_f) = _mm_nn_stacked(
        "proj", u1, w_in_st, F32, TM_WIDE,
        comm=_merge([_gather_ici(st["w_out"]), _gather_ici(st["w_ffn_gate"], cols=n_ff)]))
    (hs, kept, y), (w_out_st, wg_f, wu_f) = _lru_fwd(
        proj, *lru_w, LRU_TILE, d_mix,
        comm=_merge([_gather_d2d(w_out_st), _gather_d2d(wg_f, cols=n_ff), _gather_ici(st["w_ffn_up"], cols=n_ff)]))
    (y, rprev), (wu_f, wd_st) = _ret_fwd(
        proj, y, tables, ret_gn_w, RET_BLOCK,
        comm=_merge([_gather_d2d(wu_f, cols=n_ff), _gather_ici(st["w_ffn_down"], NEIGHBOURS)]))
    w_out_f = w_out_st.reshape(d_mix, D)
    (h1, u2), (wd_st,) = _out_proj_rms(y, w_out_f, xs, ln2_w, TM, comm=_gather_ici(wd_st, DIAGONAL))
    (dg_fac, du_fac, ff), (wd_st,) = _ffn_gate_up(u2, wg_f, wu_f, n_ff, TM, comm=_gather_d2d(wd_st))
    wd_f = wd_st.reshape(4 * n_ff, D)
    dh2, dh2b, d_fw, loss = _ffn_down_loss(ff, wd_f, h1, tgt, final_norm_w.reshape(1, D), TM_RESIDENT)

    g_wd = _mm_tn("g_w_down", ff, dh2b, n_ff, TILE_GRAD, TK_GRAD).reshape(4, n_ff, D)
    (dgt, dup), (ra_wd,) = _ffn_gate_up_bwd(dh2b, wd_f, dg_fac, du_fac, TM_WIDE, n_ff, comm=_pair_exchange(g_wd))
    pb_wd = _pair_sum("w_ffn_down", g_wd, ra_wd, place)
    g_wg, (rb_wd,) = _mm_tn_slabs("g_w_gate", u2, dgt, n_ff, TILE_GRAD, TK_GRAD, comm=_chip_exchange(pb_wd))
    slot_wd = _chip_sum("w_ffn_down", g_wd, ra_wd, rb_wd, place)
    g_wu, (full_wd, ra_wg) = _mm_tn_slabs("g_w_up", u2, dup, n_ff, TILE_GRAD, TK_GRAD,
                                          comm=_merge([_pair_share(slot_wd), _pair_exchange(g_wg)]))
    finish_big("w_ffn_down", full_wd)
    pb_wg = _pair_sum("w_ffn_gate", g_wg, ra_wg, place)
    du2, (rb_wg,) = _mm_nt_stacked("d_u2_gate", dgt, wg_f, TM, comm=_chip_exchange(pb_wg))
    du2, (ra_wu,) = _mm_nt_stacked("d_u2_up", dup, wu_f, TM, res=du2, comm=_pair_exchange(g_wu))
    slot_wg = _chip_sum("w_ffn_gate", g_wg, ra_wg, rb_wg, place)
    pb_wu = _pair_sum("w_ffn_up", g_wu, ra_wu, place)
    (dh1, dh1b, dy, d_ln2), (full_wg,) = _rms_bwd_dy(h1, ln2_w, du2, dh2, w_out_f, TM_RESIDENT,
                                                     comm=_pair_share(slot_wg))
    finish_big("w_ffn_gate", full_wg)
    g_wout = _mm_tn("g_w_out", y, dh1b, TILE_GRAD, TILE_GRAD, TK_GRAD).reshape(4, d_mix // 4, D)
    (dproj, d_cw, d_cb, d_wa, d_ba, d_wx, d_bx, d_lam), (rb_wu, ra_wout) = _lru_bwd(
        proj, hs, kept, dy, conv_full, wab, wxb, lru_lambda, LRU_TILE,
        comm=_merge([_chip_exchange(pb_wu), _pair_exchange(g_wout)]))
    slot_wu = _chip_sum("w_ffn_up", g_wu, ra_wu, rb_wu, place)
    pb_wout = _pair_sum("w_out", g_wout, ra_wout, place)
    (dproj, d_gn), (full_wu, rb_wout) = _ret_bwd(proj, rprev, dy, dproj, tables, ret_gn_w, RET_BLOCK,
                                                 comm=_merge([_pair_share(slot_wu), _chip_exchange(pb_wout)]))
    finish_big("w_ffn_up", full_wu)
    slot_wout = _chip_sum("w_out", g_wout, ra_wout, rb_wout, place)
    small = dict(conv_w=d_cw, conv_b=d_cb, gate_a_w=d_wa, gate_a_b=d_ba, gate_x_w=d_wx, gate_x_b=d_bx, lru_lambda=d_lam,
                 ret_gn_w=d_gn, ln2_w=d_ln2, final_norm_w=d_fw)
    packed, offs = _pack([small[n] for n in SMALL[1:]] + [loss])
    g_win, (full_wout, got_small) = _mm_tn("g_w_in", u1, dproj, TILE_GRAD, None, TK_GRAD, stacked_cols=n_in,
                                           comm=_merge([_pair_share(slot_wout), _gather_small(packed)]))
    finish_big("w_out", full_wout)
    (ra_win,) = _comm_call("rs_pair_w_in", [_pair_exchange(g_win)])
    pb_win = _pair_sum("w_in", g_win, ra_win, place)
    du1, (rb_win,) = _mm_nt_stacked("d_u1", dproj, w_in_st, TM, comm=_chip_exchange(pb_win))
    slot_win = _chip_sum("w_in", g_win, ra_win, rb_win, place)
    gx, d_ln1 = _rms_bwd("rms1_bwd", xs, ln1_w, du1, dh1, TM)
    packed1, _ = _pack([d_ln1])
    full_win, got_ln1 = _comm_call("reduce_tail", [_pair_share(slot_win), _gather_small(packed1)])
    finish_big("w_in", full_win)

    red = _unpack(all_sum(got_small, packed), offs, [small[n].shape for n in SMALL[1:]] + [(1, LANES)])
    g = dict(zip(SMALL[1:], red[:-1]))
    g["ln1_w"] = all_sum(got_ln1, packed1)[:-(-D // LANES)].reshape(1, D)
    loss_out = red[-1][0, 0]
    g["conv_w"] = lax.dynamic_slice(g["conv_w"], (0, chip * cw_cols), (CONV_W, cw_cols))
    packs = [_pack([t[n] for n in SMALL])[0] for t in (w, m, v)]
    gp, offs2 = _pack([g[n] for n in SMALL])
    outs = _adamw_call("adamw_small", packs[0], gp, packs[1], packs[2])
    shapes = [w[n].shape for n in SMALL]
    for dst, arr in zip((delta, new_m, new_v), outs):
        dst.update(zip(SMALL, _unpack(arr, offs2, shapes)))
    for n in SMALL:
        grad[n] = g[n].reshape(w[n].shape)

    return (loss_out, gx.reshape(x.shape), *[grad[n] for n in WEIGHTS], *[delta[n] for n in WEIGHTS],
            *[new_m[n] for n in WEIGHTS], *[new_v[n] for n in WEIGHTS])
```

```python
import functools

import jax
import jax.numpy as jnp
from jax import lax
from jax.experimental import pallas as pl
from jax.experimental.pallas import tpu as pltpu

F32 = jnp.float32
BF16 = jnp.bfloat16
MESH = pl.DeviceIdType.MESH

EPS = 1e-6
LRU_C = 8.0
ROPE_BASE = 10000.0
CHUNK = 128
HEAD_DIM = 128
CONV_W = 4
ADAM_LR = 0.001
ADAM_B1 = 0.9
ADAM_B2 = 0.999
ADAM_EPS = 1e-08
ADAM_WD = 0.01
ADAM_STEP = 10

V7X_VMEM_BYTES = 64 * 1024 * 1024
VMEM_LIMIT = V7X_VMEM_BYTES - 8 * 1024 * 1024
LANES = 128
SUBLANES_16BIT = 16

TM = 512
TM_WIDE = 1024
TM_RESIDENT = 256
TK_GRAD = 2048
TILE_GRAD = 1024
LRU_TILE = 256
RET_BLOCK = 512

NN = (((1,), (0,)), ((), ()))
NT = (((1,), (1,)), ((), ()))
TN = (((0,), (0,)), ((), ()))


def _dot(a, b, dims=NN):
    return lax.dot_general(a, b, dims, preferred_element_type=F32)


def _tile(n, pref, mult=SUBLANES_16BIT):
    best = None
    t = mult
    while t <= min(n, pref):
        if n % t == 0:
            best = t
        t += mult
    return best if best is not None else n


ELEMENTWISE_BLOCK = 4 * 1024 * 1024


def _row_tile(rows, cols, budget_bytes=ELEMENTWISE_BLOCK):
    return _tile(rows, max(SUBLANES_16BIT, budget_bytes // (cols * 4)))


def _params(sem):
    return pltpu.CompilerParams(dimension_semantics=sem, vmem_limit_bytes=VMEM_LIMIT)


HBM_SPEC = pl.BlockSpec(memory_space=pl.ANY)


class _Comm:
    def __init__(self, ins, outs, aliases, n_sem, start, finish):
        self.ins, self.outs, self.aliases, self.n_sem, self.start, self.finish = ins, outs, aliases, n_sem, start, finish


def _merge(tasks):
    ins, outs, aliases, plans, n_sem = [], [], {}, [], 0
    for t in tasks:
        i0, o0 = len(ins), len(outs)
        plans.append((t, i0, o0, n_sem))
        ins += t.ins
        outs += t.outs
        aliases.update({i0 + a: o0 + b for a, b in t.aliases.items()})
        n_sem += t.n_sem

    def run(which):
        def go(cins, couts, ssem, rsem, base):
            for t, i0, o0, s0 in plans:
                getattr(t, which)(cins[i0:i0 + len(t.ins)], couts[o0:o0 + len(t.outs)], ssem, rsem, base + s0)
        return go

    return _Comm(ins, outs, aliases, n_sem, run("start"), run("finish"))


def _pcall(body, *, name, grid, in_specs, out_specs, out_shape, operands, scratch_shapes=(), aliases=None, comm=None):
    n_in, n_out, n_scr = len(operands), len(out_shape), len(scratch_shapes)
    aliases = dict(aliases or {})
    params = _params(("arbitrary",) * len(grid))
    if comm is None:
        return pl.pallas_call(body, grid=grid, in_specs=list(in_specs), out_specs=list(out_specs), out_shape=list(out_shape),
                              scratch_shapes=list(scratch_shapes), input_output_aliases=aliases, name=name,
                              compiler_params=params)(*operands)
    nci, nco = len(comm.ins), len(comm.outs)

    def wrapped(*refs):
        ins, cins = refs[:n_in], refs[n_in:n_in + nci]
        o0 = n_in + nci
        outs, couts = refs[o0:o0 + n_out], refs[o0 + n_out:o0 + n_out + nco]
        s0 = o0 + n_out + nco
        scr, (ssem, rsem) = refs[s0:s0 + n_scr], refs[s0 + n_scr:]
        ids = [pl.program_id(a) for a in range(len(grid))]
        first = functools.reduce(jnp.logical_and, [i == 0 for i in ids])
        last = functools.reduce(jnp.logical_and, [i == g - 1 for i, g in zip(ids, grid)])

        @pl.when(first)
        def _():
            comm.start(cins, couts, ssem, rsem, 0)

        body(*ins, *outs, *scr)

        @pl.when(last)
        def _():
            comm.finish(cins, couts, ssem, rsem, 0)

    aliases.update({n_in + a: n_out + b for a, b in comm.aliases.items()})
    res = pl.pallas_call(
        wrapped, grid=grid, in_specs=list(in_specs) + [HBM_SPEC] * nci, out_specs=list(out_specs) + [HBM_SPEC] * nco,
        out_shape=list(out_shape) + list(comm.outs),
        scratch_shapes=list(scratch_shapes) + [pltpu.SemaphoreType.DMA((comm.n_sem,)), pltpu.SemaphoreType.DMA((comm.n_sem,))],
        input_output_aliases=aliases, name=name, compiler_params=params)(*operands, *comm.ins)
    return res[:n_out], res[n_out:]


def _ew(name, fn, ins, outs, grid, sp=None):
    n_in = len(ins)

    def body(*refs):
        if sp is not None:
            refs = refs[1:]
        vals = [r[...] for r in refs[:n_in]]
        res = fn(*vals)
        for o_ref, v in zip(refs[n_in:], res):
            o_ref[...] = v.astype(o_ref.dtype)

    in_specs = [s for _, s in ins]
    out_specs = [s for _, s in outs]
    out_shape = [s for s, _ in outs]
    sem = ("arbitrary",) * len(grid)
    if sp is None:
        return pl.pallas_call(body, grid=grid, in_specs=in_specs, out_specs=out_specs, out_shape=out_shape,
                              name=name, compiler_params=_params(sem))(*[a for a, _ in ins])
    gs = pltpu.PrefetchScalarGridSpec(num_scalar_prefetch=1, grid=grid, in_specs=in_specs, out_specs=out_specs)
    return pl.pallas_call(body, grid_spec=gs, out_shape=out_shape, name=name,
                          compiler_params=_params(sem))(sp, *[a for a, _ in ins])


def _matmul(name, pairs, dims, grid, out_shape, out_spec, acc_shape, comm=None, transpose_out=False):
    n = len(pairs)
    nk = grid[2]

    def body(*refs):
        ab = refs[:2 * n]
        o_ref = refs[2 * n]
        acc_ref = refs[2 * n + 1] if nk > 1 else None

        def partial():
            t = None
            for p in range(n):
                d = _dot(ab[2 * p][...], ab[2 * p + 1][...], dims)
                t = d if t is None else t + d
            return t

        def finish(t):
            o_ref[...] = (t.T if transpose_out else t).astype(o_ref.dtype)

        if nk == 1:
            finish(partial())
        else:
            k = pl.program_id(2)

            @pl.when(k == 0)
            def _():
                acc_ref[...] = partial()

            @pl.when(k > 0)
            def _():
                acc_ref[...] += partial()

            @pl.when(k == nk - 1)
            def _():
                finish(acc_ref[...])

    operands, in_specs = [], []
    for a, a_spec, b, b_spec in pairs:
        operands += [a, b]
        in_specs += [a_spec, b_spec]
    scratch = [pltpu.VMEM(acc_shape, F32)] if nk > 1 else []
    res = _pcall(body, name=name, grid=grid, in_specs=in_specs, out_specs=[out_spec], out_shape=[out_shape],
                 operands=operands, scratch_shapes=scratch, comm=comm)
    return res[0] if comm is None else (res[0][0], res[1])


def _mm_nn_stacked(name, a, b_st, out_dtype, tm, comm=None):
    M, K = a.shape
    J, _, Nj = b_st.shape
    tm = _tile(M, tm)
    return _matmul(
        name, [(a, pl.BlockSpec((tm, K), lambda j, i, k: (i, 0)), b_st, pl.BlockSpec((None, K, Nj), lambda j, i, k: (j, 0, 0)))],
        NN, (J, M // tm, 1), jax.ShapeDtypeStruct((M, J * Nj), out_dtype), pl.BlockSpec((tm, Nj), lambda j, i, k: (i, j)), None,
        comm=comm)


def _mm_nt_stacked(name, a, b_st, tm, res=None, comm=None):
    M, K = a.shape
    N = b_st.shape[-2]
    tm = _tile(M, tm)

    def body(a_ref, b_ref, *rest):
        o_ref = rest[-1]
        t = None if res is None else rest[0][...]
        if len(b_ref.shape) == 2:
            d = _dot(a_ref[...], b_ref[...], NT)
            t = d if t is None else t + d
        else:
            nj = b_ref.shape[2]
            for s in range(b_ref.shape[0]):
                d = _dot(a_ref[:, s * nj:(s + 1) * nj], b_ref[s], NT)
                t = d if t is None else t + d
        o_ref[...] = t

    row = pl.BlockSpec((tm, N), lambda i: (i, 0))
    out = _pcall(body, name=name, grid=(M // tm,),
                 in_specs=[pl.BlockSpec((tm, K), lambda i: (i, 0)),
                           pl.BlockSpec(b_st.shape, lambda i: (0,) * b_st.ndim, pipeline_mode=pl.Buffered(1))]
                 + [row] * (res is not None),
                 out_specs=[row], out_shape=[jax.ShapeDtypeStruct((M, N), F32)],
                 operands=[a, b_st] + [res] * (res is not None), comm=comm)
    return out[0] if comm is None else (out[0][0], out[1])


MXU_COLUMNS = 256


def _col_blocks(n):
    return [slice(s, min(s + MXU_COLUMNS, n)) for s in range(0, n, MXU_COLUMNS)]


def _ffn_gate_up(u2, wg, wu, n_slab, tm, comm=None):
    S, D = u2.shape
    F = wg.shape[1]
    tm = _tile(S, tm)
    tn = 2 * n_slab if (n_slab % MXU_COLUMNS and (F // n_slab) % 2 == 0) else n_slab

    def body(a_ref, wg_ref, wu_ref, dg_ref, du_ref, ff_ref):
        a = a_ref[...]
        blocks = _col_blocks(tn)
        ahead = (_dot(a, wg_ref[:, blocks[0]]), _dot(a, wu_ref[:, blocks[0]]))
        for j, cols in enumerate(blocks):
            g, u = ahead
            if j + 1 < len(blocks):
                ahead = (_dot(a, wg_ref[:, blocks[j + 1]]), _dot(a, wu_ref[:, blocks[j + 1]]))
            sg = jax.nn.sigmoid(g)
            silu = g * sg
            dg_ref[:, cols] = (u * (sg * (1.0 + g * (1.0 - sg)))).astype(BF16)
            du_ref[:, cols] = silu.astype(BF16)
            ff_ref[:, cols] = (silu * u).astype(BF16)

    w_spec = pl.BlockSpec((D, tn), lambda j, i: (0, j), pipeline_mode=pl.Buffered(1))
    o_spec = pl.BlockSpec((tm, tn), lambda j, i: (i, j))
    o = jax.ShapeDtypeStruct((S, F), BF16)
    return _pcall(body, name="ffn_gate_up", grid=(F // tn, S // tm),
                  in_specs=[pl.BlockSpec((tm, D), lambda j, i: (i, 0)), w_spec, w_spec],
                  out_specs=[o_spec, o_spec, o_spec], out_shape=[o, o, o], operands=[u2, wg, wu], comm=comm)


def _ffn_gate_up_bwd(dh2b, wd, dg_fac, du_fac, tm, tn, comm=None):
    S, D = dh2b.shape
    F = wd.shape[0]
    tm, tn = _tile(S, tm), _tile(F, tn, LANES)

    def body(a_ref, wd_ref, dg_ref, du_ref, dgt_ref, dup_ref):
        a = a_ref[...]
        for cols in _col_blocks(tn):
            d = _dot(a, wd_ref[cols, :], NT)
            dgt_ref[:, cols] = (d * dg_ref[:, cols].astype(F32)).astype(BF16)
            dup_ref[:, cols] = (d * du_ref[:, cols].astype(F32)).astype(BF16)

    blk = pl.BlockSpec((tm, tn), lambda j, i: (i, j))
    o = jax.ShapeDtypeStruct((S, F), BF16)
    return _pcall(body, name="ffn_gate_up_bwd", grid=(F // tn, S // tm),
                  in_specs=[pl.BlockSpec((tm, D), lambda j, i: (i, 0)), pl.BlockSpec((tn, D), lambda j, i: (j, 0)), blk, blk],
                  out_specs=[blk, blk], out_shape=[o, o], operands=[dh2b, wd, dg_fac, du_fac], comm=comm)


def _mm_tn(name, a, b, tmo, tn, tk, stacked_cols=None, comm=None):
    S, Mo = a.shape
    N = b.shape[1]
    tmo, tk = _tile(Mo, tmo, LANES), _tile(S, tk)
    if stacked_cols is None:
        tn = _tile(N, tn, LANES)
        out_shape = jax.ShapeDtypeStruct((Mo, N), F32)
        out_spec = pl.BlockSpec((tmo, tn), lambda i, j, k: (i, j))
    else:
        tn = stacked_cols
        out_shape = jax.ShapeDtypeStruct((N // tn, Mo, tn), F32)
        out_spec = pl.BlockSpec((None, tmo, tn), lambda i, j, k: (j, i, 0))
    return _matmul(
        name, [(a, pl.BlockSpec((tk, tmo), lambda i, j, k: (k, i)), b, pl.BlockSpec((tk, tn), lambda i, j, k: (k, j)))],
        TN, (Mo // tmo, N // tn, S // tk), out_shape, out_spec, (tmo, tn), comm=comm)


def _mm_tn_slabs(name, a, b, n_slab, tmo, tk, comm=None):
    S, Mo = a.shape
    J = b.shape[1] // n_slab
    tmo, tk = _tile(Mo, tmo, LANES), _tile(S, tk)
    return _matmul(
        name, [(b, pl.BlockSpec((tk, n_slab), lambda j, i, k: (k, j)), a, pl.BlockSpec((tk, tmo), lambda j, i, k: (k, i)))],
        TN, (J, Mo // tmo, S // tk), jax.ShapeDtypeStruct((J, Mo, n_slab), F32),
        pl.BlockSpec((None, tmo, n_slab), lambda j, i, k: (j, i, 0)), (n_slab, tmo), comm=comm, transpose_out=True)


def _rms_fwd(name, x, w, tm, comm=None):
    S, D = x.shape
    tm = _tile(S, tm)

    def body(x_ref, w_ref, o_ref):
        xv = x_ref[...]
        r = lax.rsqrt(jnp.mean(xv * xv, axis=-1, keepdims=True) + EPS)
        o_ref[...] = ((xv * r) * w_ref[...]).astype(BF16)

    row = pl.BlockSpec((tm, D), lambda i: (i, 0))
    return _pcall(body, name=name, grid=(S // tm,), in_specs=[row, pl.BlockSpec((1, D), lambda i: (0, 0))], out_specs=[row],
                  out_shape=[jax.ShapeDtypeStruct((S, D), BF16)], operands=[x, w], comm=comm)


def _rms_bwd(name, x, w, dy, dres, tm, comm=None):
    S, D = x.shape
    tm = _tile(S, tm)

    def body(x_ref, w_ref, dy_ref, dres_ref, dx_ref, dw_ref):
        i = pl.program_id(0)

        @pl.when(i == 0)
        def _():
            dw_ref[...] = jnp.zeros_like(dw_ref)

        xv = x_ref[...]
        r = lax.rsqrt(jnp.mean(xv * xv, axis=-1, keepdims=True) + EPS)
        nv = xv * r
        dyv = dy_ref[...]
        dn = dyv * w_ref[...]
        dw_ref[...] += jnp.sum(dyv * nv, axis=0, keepdims=True)
        dx = dres_ref[...] + r * (dn - nv * jnp.mean(dn * nv, axis=-1, keepdims=True))
        dx_ref[...] = dx

    row = pl.BlockSpec((tm, D), lambda i: (i, 0))
    vec = pl.BlockSpec((1, D), lambda i: (0, 0))
    return _pcall(body, name=name, grid=(S // tm,), in_specs=[row, vec, row, row], out_specs=[row, vec],
                  out_shape=[jax.ShapeDtypeStruct((S, D), F32), jax.ShapeDtypeStruct((1, D), F32)],
                  operands=[x, w, dy, dres], comm=comm)


def _rms_bwd_dy(h1, w, du2, dh2, w_out, tm, comm=None):
    S, D = h1.shape
    d_mix = w_out.shape[0]
    tm = _tile(S, tm)

    def body(x_ref, w_ref, dy_ref, dres_ref, wo_ref, dx_ref, dxb_ref, out_ref, dw_ref):
        i = pl.program_id(0)

        @pl.when(i == 0)
        def _():
            dw_ref[...] = jnp.zeros_like(dw_ref)

        xv = x_ref[...]
        r = lax.rsqrt(jnp.mean(xv * xv, axis=-1, keepdims=True) + EPS)
        nv = xv * r
        dyv = dy_ref[...]
        dn = dyv * w_ref[...]
        dw_ref[...] += jnp.sum(dyv * nv, axis=0, keepdims=True)
        dx = dres_ref[...] + r * (dn - nv * jnp.mean(dn * nv, axis=-1, keepdims=True))
        dx_ref[...] = dx
        dxb = dx.astype(BF16)
        dxb_ref[...] = dxb
        out_ref[...] = _dot(dxb, wo_ref[...], NT)

    row = pl.BlockSpec((tm, D), lambda i: (i, 0))
    vec = pl.BlockSpec((1, D), lambda i: (0, 0))
    return _pcall(
        body, name="rms2_bwd_dy", grid=(S // tm,),
        in_specs=[row, vec, row, row, pl.BlockSpec((d_mix, D), lambda i: (0, 0), pipeline_mode=pl.Buffered(1))],
        out_specs=[row, row, pl.BlockSpec((tm, d_mix), lambda i: (i, 0)), vec],
        out_shape=[jax.ShapeDtypeStruct((S, D), F32), jax.ShapeDtypeStruct((S, D), BF16),
                   jax.ShapeDtypeStruct((S, d_mix), F32), jax.ShapeDtypeStruct((1, D), F32)],
        operands=[h1, w, du2, dh2, w_out], comm=comm)


def _out_proj_rms(y, w_out, x, ln_w, tm, comm=None):
    S, K = y.shape
    D = w_out.shape[1]
    tm = _tile(S, tm)

    def body(a_ref, w_ref, x_ref, lw_ref, h_ref, u_ref):
        hv = _dot(a_ref[...], w_ref[...]) + x_ref[...]
        h_ref[...] = hv
        r = lax.rsqrt(jnp.mean(hv * hv, axis=-1, keepdims=True) + EPS)
        u_ref[...] = ((hv * r) * lw_ref[...]).astype(BF16)

    row = pl.BlockSpec((tm, D), lambda i: (i, 0))
    return _pcall(
        body, name="out_proj", grid=(S // tm,),
        in_specs=[pl.BlockSpec((tm, K), lambda i: (i, 0)),
                  pl.BlockSpec((K, D), lambda i: (0, 0), pipeline_mode=pl.Buffered(1)), row,
                  pl.BlockSpec((1, D), lambda i: (0, 0))],
        out_specs=[row, row], out_shape=[jax.ShapeDtypeStruct((S, D), F32), jax.ShapeDtypeStruct((S, D), BF16)],
        operands=[y, w_out, x, ln_w], comm=comm)


def _ffn_down_loss(ff, wd, h1, tgt, fw, tm):
    S, K = ff.shape
    D = wd.shape[1]
    tm = _tile(S, tm)

    def body(a_ref, wd_ref, h1_ref, t_ref, w_ref, dh_ref, dhb_ref, dw_ref, loss_ref):
        i = pl.program_id(0)

        @pl.when(i == 0)
        def _():
            dw_ref[...] = jnp.zeros_like(dw_ref)
            loss_ref[...] = jnp.zeros_like(loss_ref)

        hv = _dot(a_ref[...], wd_ref[...]) + h1_ref[...]
        wv = w_ref[...]
        r = lax.rsqrt(jnp.mean(hv * hv, axis=-1, keepdims=True) + EPS)
        nv = hv * r
        err = nv * wv - t_ref[...]
        row_loss = jnp.mean(err * err, axis=-1, keepdims=True)
        loss_ref[...] += 0.5 * jnp.sum(row_loss, axis=0, keepdims=True)
        dyo = err * (1.0 / D)
        dn = dyo * wv
        dw_ref[...] += jnp.sum(dyo * nv, axis=0, keepdims=True)
        dh = r * (dn - nv * jnp.mean(dn * nv, axis=-1, keepdims=True))
        dh_ref[...] = dh
        dhb_ref[...] = dh.astype(BF16)

    row = pl.BlockSpec((tm, D), lambda i: (i, 0))
    vec = pl.BlockSpec((1, D), lambda i: (0, 0))
    return _pcall(
        body, name="ffn_down_loss", grid=(S // tm,),
        in_specs=[pl.BlockSpec((tm, K), lambda i: (i, 0)),
                  pl.BlockSpec((K, D), lambda i: (0, 0), pipeline_mode=pl.Buffered(1)), row, row, vec],
        out_specs=[row, row, vec, pl.BlockSpec((1, LANES), lambda i: (0, 0))],
        out_shape=[jax.ShapeDtypeStruct((S, D), F32), jax.ShapeDtypeStruct((S, D), BF16),
                   jax.ShapeDtypeStruct((1, D), F32), jax.ShapeDtypeStruct((1, LANES), F32)],
        operands=[ff, wd, h1, tgt, fw])


def _shift_down(x, d, head8):
    r = pltpu.roll(x, d, 0)
    rh = pltpu.roll(head8, d, 0)
    row8 = lax.broadcasted_iota(jnp.int32, head8.shape, 0)
    top = jnp.where(row8 < d, rh, r[0:8])
    return jnp.concatenate([top, r[8:]], axis=0)


def _shift_up(x, d, tail8):
    n = x.shape[0]
    r = pltpu.roll(x, n - d, 0)
    rt = pltpu.roll(tail8, 8 - d, 0)
    row8 = lax.broadcasted_iota(jnp.int32, tail8.shape, 0)
    bot = jnp.where(row8 + d >= 8, rt, r[n - 8:n])
    return jnp.concatenate([r[:n - 8], bot], axis=0)


def _roll_in_groups(x, d):
    n, c = x.shape
    return pltpu.roll(x.reshape(n // 8, 8, c), d, 1).reshape(n, c)


def _log_sigmoid(lam):
    z = jnp.exp(-jnp.abs(lam))
    u = 1.0 + z
    log1p = jnp.where(u == 1.0, z, jnp.log(u) * (z / jnp.where(u == 1.0, 1.0, u - 1.0)))
    return jnp.minimum(lam, 0.0) - log1p


def _neg_expm1(z, exp_z):
    series = -z * (1.0 + z * (0.5 + z * (1.0 / 6.0)))
    return jnp.where(z > -0.02, series, 1.0 - exp_z)


_GELU_C = 0.7978845608028654


def _gelu(x):
    t = jnp.tanh(_GELU_C * (x + 0.044715 * (x * x * x)))
    return x * (0.5 * (1.0 + t)), t


def _gelu_grad(x, t):
    return 0.5 * (1.0 + t) + 0.5 * x * (1.0 - t * t) * (_GELU_C * (1.0 + 3.0 * 0.044715 * (x * x)))


def _lx_shifts(lx, head8):
    return [lx] + [_shift_down(lx, d, head8) for d in (1, 2, 3)]


def _lru_gates(lx, head8, cw, cb, wa_ref, ba, wx_ref, bx, ls):
    nb = wa_ref.shape[0]
    sh = _lx_shifts(lx, head8)
    cx = cb + sh[3] * cw[0:1]
    cx = cx + sh[2] * cw[1:2]
    cx = cx + sh[1] * cw[2:3]
    cx = cx + sh[0] * cw[3:4]
    cxb = cx.astype(BF16)
    ra = jnp.concatenate([_dot(cxb[:, n * HEAD_DIM:(n + 1) * HEAD_DIM], wa_ref[n]) for n in range(nb)], axis=1) + ba
    ia = jnp.concatenate([_dot(cxb[:, n * HEAD_DIM:(n + 1) * HEAD_DIM], wx_ref[n]) for n in range(nb)], axis=1) + bx
    r = jax.nn.sigmoid(ra)
    ig = jax.nn.sigmoid(ia)
    log_a = LRU_C * r * ls
    a = jnp.exp(log_a)
    return cx, r, ig, a, jnp.sqrt(_neg_expm1(2.0 * log_a, a * a))


def _lru_specs(tl, DL):
    nb = DL // HEAD_DIM
    vec = pl.BlockSpec((1, DL), lambda i: (0, 0))
    return [pl.BlockSpec((CONV_W, DL), lambda i: (0, 0)), vec,
            pl.BlockSpec((nb, HEAD_DIM, HEAD_DIM), lambda i: (0, 0, 0)), vec,
            pl.BlockSpec((nb, HEAD_DIM, HEAD_DIM), lambda i: (0, 0, 0)), vec, vec]


def _lru_fwd(proj, cw, cb, wa, ba, wx, bx, lam, tl, d_mix, comm=None):
    S = proj.shape[0]
    DL = cb.shape[1]
    tl = _tile(S, tl)

    def body(lx_ref, lg_ref, cw_ref, cb_ref, wa_ref, ba_ref, wx_ref, bx_ref, lam_ref, h_ref, kept_ref, y_ref,
             prev8, hc, a_s, b_s):
        i = pl.program_id(0)

        @pl.when(i == 0)
        def _():
            prev8[...] = jnp.zeros_like(prev8)
            hc[...] = jnp.zeros_like(hc)

        lx = lx_ref[...]
        ls = _log_sigmoid(lam_ref[...])
        kept = _lru_gates(lx, prev8[...], cw_ref[...], cb_ref[...], wa_ref, ba_ref[...], wx_ref, bx_ref[...], ls)
        for n, val in enumerate(kept):
            kept_ref[:, n * DL:(n + 1) * DL] = val
        cx, _, ig, a, mult = kept
        b = mult * (ig * cx)
        row = lax.broadcasted_iota(jnp.int32, a.shape, 0) & 7
        for d in (1, 2, 4):
            a_sh = _roll_in_groups(a, d)
            b_sh = _roll_in_groups(b, d)
            m = row >= d
            b = jnp.where(m, a * b_sh + b, b)
            a = jnp.where(m, a * a_sh, a)
        a_s[...] = a
        b_s[...] = b

        def step(g, hprev):
            sl = pl.ds(pl.multiple_of(g * 8, 8), 8)
            hh = a_s[sl, :] * hprev + b_s[sl, :]
            h_ref[sl, :] = hh
            return hh[7:8, :]

        hc[0:1, :] = lax.fori_loop(0, tl // 8, step, hc[0:1, :])
        prev8[...] = lx[tl - 8:tl]
        g, _ = _gelu(lg_ref[...])
        y_ref[...] = (h_ref[...] * g).astype(BF16)

    return _pcall(
        body, name="lru_fwd", grid=(S // tl,),
        in_specs=[pl.BlockSpec((tl, DL), lambda i: (i, 0)), pl.BlockSpec((tl, DL), lambda i: (i, 1))] + _lru_specs(tl, DL),
        out_specs=[pl.BlockSpec((tl, DL), lambda i: (i, 0)), pl.BlockSpec((tl, 5 * DL), lambda i: (i, 0)),
                   pl.BlockSpec((tl, DL), lambda i: (i, 0))],
        out_shape=[jax.ShapeDtypeStruct((S, DL), F32), jax.ShapeDtypeStruct((S, 5 * DL), F32),
                   jax.ShapeDtypeStruct((S, d_mix), BF16)],
        scratch_shapes=[pltpu.VMEM((8, DL), F32), pltpu.VMEM((8, DL), F32), pltpu.VMEM((tl, DL), F32), pltpu.VMEM((tl, DL), F32)],
        operands=[proj, proj, cw, cb, wa, ba, wx, bx, lam], comm=comm)


def _lru_bwd(proj, h, kept, dy, cw, wa, wx, lam, tl, comm=None):
    S = proj.shape[0]
    DL = lam.shape[1]
    nb = DL // HEAD_DIM
    tl = _tile(S, tl)
    nt = S // tl
    ng = tl // 8
    t8 = tl // 8

    def body(lx_ref, lxp_ref, lg_ref, h_ref, hp_ref, kept_ref, dy_ref, cw_ref, wa_ref, wx_ref, lam_ref,
             dlxg_ref, dcw_ref, dcb_ref, dwa_ref, dba_ref, dwx_ref, dbx_ref, dlam_ref,
             a_next, g_carry, dcx_next, an_s, dh_s, g_s):
        i = pl.program_id(0)

        @pl.when(i == 0)
        def _():
            for ref in (dcw_ref, dcb_ref, dwa_ref, dba_ref, dwx_ref, dbx_ref, dlam_ref, a_next, g_carry, dcx_next):
                ref[...] = jnp.zeros_like(ref)

        first = i == nt - 1
        hv = h_ref[...]
        lg = lg_ref[...]
        dyv = dy_ref[...]
        hhead8 = jnp.where(first, 0.0, hp_ref[...])
        lamv = lam_ref[...]
        ls = _log_sigmoid(lamv)
        cwv = cw_ref[...]
        sh = _lx_shifts(lx_ref[...], jnp.where(first, 0.0, lxp_ref[...]))
        cx, r, ig, a, mult = (kept_ref[:, n * DL:(n + 1) * DL] for n in range(5))
        cxb = cx.astype(BF16)
        hprev = _shift_down(hv, 1, hhead8)
        g, t = _gelu(lg)
        dlg = dyv * hv * _gelu_grad(lg, t)
        dh = dyv * g
        an = _shift_up(a, 1, a_next[...])
        row = lax.broadcasted_iota(jnp.int32, a.shape, 0) & 7
        for d in (1, 2, 4):
            an_sh = _roll_in_groups(an, 8 - d)
            dh_sh = _roll_in_groups(dh, 8 - d)
            m = row + d < 8
            dh = jnp.where(m, an * dh_sh + dh, dh)
            an = jnp.where(m, an * an_sh, an)
        an_s[...] = an
        dh_s[...] = dh

        def step(k, gc):
            sl = pl.ds(pl.multiple_of((ng - 1 - k) * 8, 8), 8)
            gg = an_s[sl, :] * gc + dh_s[sl, :]
            g_s[sl, :] = gg
            return gg[0:1, :]

        g_carry[0:1, :] = lax.fori_loop(0, ng, step, g_carry[0:1, :])
        a_next[...] = a[0:8]
        G = g_s[...]
        da = G * hprev
        icx = ig * cx
        dmult = G * icx
        dicx = G * mult
        di = dicx * cx
        dcx = dicx * ig
        dlog = da * a - dmult * ((a * a) * lax.rsqrt(mult * mult))
        dr = dlog * (LRU_C * ls)
        dlam_ref[...] += jnp.sum(dlog * (LRU_C * r), axis=0, keepdims=True)
        dra = dr * r * (1.0 - r)
        dia = di * ig * (1.0 - ig)
        dba_ref[...] += jnp.sum(dra, axis=0, keepdims=True)
        dbx_ref[...] += jnp.sum(dia, axis=0, keepdims=True)
        drab = dra.astype(BF16)
        diab = dia.astype(BF16)
        back = []
        for n in range(nb):
            cs = slice(n * HEAD_DIM, (n + 1) * HEAD_DIM)
            dwa_ref[n] += _dot(cxb[:, cs], drab[:, cs], TN)
            dwx_ref[n] += _dot(cxb[:, cs], diab[:, cs], TN)
            back.append(_dot(drab[:, cs], wa_ref[n], NT) + _dot(diab[:, cs], wx_ref[n], NT))
        dcx = dcx + jnp.concatenate(back, axis=1)
        dcb_ref[...] += jnp.sum(dcx, axis=0, keepdims=True)
        for tap in range(CONV_W):
            dcw_ref[tap:tap + 1, :] += jnp.sum(dcx * sh[CONV_W - 1 - tap], axis=0, keepdims=True)
        tail = dcx_next[...]
        dlx = dcx * cwv[3:4]
        for d in (1, 2, 3):
            dlx = dlx + _shift_up(dcx, d, tail) * cwv[3 - d:4 - d]
        dcx_next[...] = dcx[0:8]
        dlxg_ref[:, 0:DL] = dlx.astype(BF16)
        dlxg_ref[:, DL:2 * DL] = dlg.astype(BF16)

        @pl.when(i == nt - 1)
        def _():
            dlam_ref[...] = dlam_ref[...] * (1.0 - jax.nn.sigmoid(lamv))

    rev = lambda i: nt - 1 - i
    prev8_map = lambda i: (jnp.maximum((nt - 1 - i) * t8 - 1, 0), 0)
    vec = pl.BlockSpec((1, DL), lambda i: (0, 0))
    mat = pl.BlockSpec((nb, HEAD_DIM, HEAD_DIM), lambda i: (0, 0, 0))
    return _pcall(
        body, name="lru_bwd", grid=(nt,), operands=[proj, proj, proj, h, h, kept, dy, cw, wa, wx, lam], comm=comm,
        in_specs=[pl.BlockSpec((tl, DL), lambda i: (rev(i), 0)), pl.BlockSpec((8, DL), prev8_map),
                  pl.BlockSpec((tl, DL), lambda i: (rev(i), 1)),
                  pl.BlockSpec((tl, DL), lambda i: (rev(i), 0)), pl.BlockSpec((8, DL), prev8_map),
                  pl.BlockSpec((tl, 5 * DL), lambda i: (rev(i), 0)),
                  pl.BlockSpec((tl, DL), lambda i: (rev(i), 0)), pl.BlockSpec((CONV_W, DL), lambda i: (0, 0)), mat, mat, vec],
        out_specs=[pl.BlockSpec((tl, 2 * DL), lambda i: (rev(i), 0)), pl.BlockSpec((CONV_W, DL), lambda i: (0, 0)), vec,
                   mat, vec, mat, vec, vec],
        out_shape=[jax.ShapeDtypeStruct(proj.shape, BF16), jax.ShapeDtypeStruct((CONV_W, DL), F32),
                   jax.ShapeDtypeStruct((1, DL), F32), jax.ShapeDtypeStruct((nb, HEAD_DIM, HEAD_DIM), F32),
                   jax.ShapeDtypeStruct((1, DL), F32), jax.ShapeDtypeStruct((nb, HEAD_DIM, HEAD_DIM), F32),
                   jax.ShapeDtypeStruct((1, DL), F32), jax.ShapeDtypeStruct((1, DL), F32)],
        scratch_shapes=[pltpu.VMEM((8, DL), F32), pltpu.VMEM((8, DL), F32), pltpu.VMEM((8, DL), F32),
                        pltpu.VMEM((tl, DL), F32), pltpu.VMEM((tl, DL), F32), pltpu.VMEM((tl, DL), F32)])


def _ret_tables(S, H):
    pos = jnp.arange(S, dtype=F32)
    inv_freq = ROPE_BASE ** (-jnp.arange(0, HEAD_DIM, 2, dtype=F32) / HEAD_DIM)
    ang = pos[:, None] * inv_freq[None, :]
    cos, sin = jnp.cos(ang), jnp.sin(ang)
    cosf = jnp.concatenate([cos, cos], axis=1)
    sins = jnp.concatenate([-sin, sin], axis=1)
    log_gamma = jnp.log1p(-jnp.exp2(-5.0 - jnp.arange(H, dtype=F32)))
    idx = jnp.arange(CHUNK)
    diff = idx[:, None] - idx[None, :]
    causal = diff >= 0
    decay = jnp.where(causal[None], jnp.exp(log_gamma[:, None, None] * jnp.where(causal, diff, 0)[None].astype(F32)), 0.0)
    zeta = jnp.exp(log_gamma[:, None] * (CHUNK - 1 - idx).astype(F32)[None, :])
    xi = jnp.exp(log_gamma[:, None] * (idx + 1).astype(F32)[None, :])
    gc = jnp.exp(log_gamma * CHUNK)
    lanes = (H, CHUNK, HEAD_DIM)
    return (cosf, sins, decay, jnp.broadcast_to(zeta[:, :, None], lanes), jnp.broadcast_to(xi[:, :, None], lanes),
            jnp.broadcast_to(gc[:, None, None], lanes))


def _rope(t, cos, sin_signed):
    return t * cos + pltpu.roll(t, HEAD_DIM // 2, 1) * sin_signed


def _rope_t(d, cos, sin_signed):
    return d * cos + pltpu.roll(d * sin_signed, HEAD_DIM // 2, 1)


def _ret_const_specs(H, DR):
    full = pl.BlockSpec((H, CHUNK, HEAD_DIM), lambda *_: (0, 0, 0))
    return [full, full, full, full, pl.BlockSpec((1, DR), lambda *_: (0, 0))]


def _ret_fwd(proj, y, tables, gnw, tb, comm=None):
    S = proj.shape[0]
    DR = gnw.shape[1]
    H = DR // HEAD_DIM
    tb = _tile(S, tb, CHUNK)
    nc = tb // CHUNK
    cosf, sins, dm, zeta, xi, gc = tables
    scale = HEAD_DIM ** -0.5

    def body(qk_ref, vg_ref, cos_ref, sin_ref, dm_ref, zeta_ref, xi_ref, gc_ref, gnw_ref, y_in, y_ref, rprev_ref, r_s):
        del y_in
        i = pl.program_id(0)

        @pl.when(i == 0)
        def _():
            r_s[...] = jnp.zeros_like(r_s)

        def chunk(c, carry):
            rows = pl.ds(pl.multiple_of(c * CHUNK, CHUNK), CHUNK)
            cos = cos_ref[rows, :]
            sin = sin_ref[rows, :]
            heads = range(H)
            c0 = [slice(h * HEAD_DIM, (h + 1) * HEAD_DIM) for h in heads]
            c1 = [slice(DR + h * HEAD_DIM, DR + (h + 1) * HEAD_DIM) for h in heads]
            qh = [_rope(qk_ref[rows, c0[h]], cos, sin) for h in heads]
            kh = [_rope(qk_ref[rows, c1[h]], cos, sin) * scale for h in heads]
            vb = [vg_ref[rows, c0[h]].astype(BF16) for h in heads]
            rp = [r_s[h] for h in heads]
            rpb = [rp[h].astype(BF16) for h in heads]
            s = [_dot(qh[h].astype(BF16), kh[h].astype(BF16), NT) for h in heads]
            kv = [_dot((kh[h] * zeta_ref[h]).astype(BF16), vb[h], TN) for h in heads]
            cross = [_dot((qh[h] * xi_ref[h]).astype(BF16), rpb[h]) for h in heads]
            o = [_dot((s[h] * dm_ref[h]).astype(BF16), vb[h]) + cross[h] for h in heads]
            for h in heads:
                rprev_ref[c, h] = rpb[h]
                r_s[h] = rp[h] * gc_ref[h] + kv[h]
                mu = jnp.mean(o[h], axis=-1, keepdims=True)
                oc = o[h] - mu
                var = jnp.mean(oc * oc, axis=-1, keepdims=True)
                on = oc * lax.rsqrt(var + EPS) * gnw_ref[:, c0[h]]
                gate = vg_ref[rows, c1[h]]
                y_ref[rows, c0[h]] = (gate * jax.nn.sigmoid(gate) * on).astype(BF16)
            return carry

        lax.fori_loop(0, nc, chunk, 0)

    return _pcall(
        body, name="ret_fwd", grid=(S // tb,),
        in_specs=[pl.BlockSpec((tb, 2 * DR), lambda i: (i, 1)), pl.BlockSpec((tb, 2 * DR), lambda i: (i, 2)),
                  pl.BlockSpec((tb, HEAD_DIM), lambda i: (i, 0)), pl.BlockSpec((tb, HEAD_DIM), lambda i: (i, 0))]
        + _ret_const_specs(H, DR) + [HBM_SPEC],
        out_specs=[pl.BlockSpec((tb, DR), lambda i: (i, 1)),
                   pl.BlockSpec((nc, H, CHUNK, HEAD_DIM), lambda i: (i, 0, 0, 0))],
        out_shape=[jax.ShapeDtypeStruct(y.shape, BF16), jax.ShapeDtypeStruct((S // CHUNK, H, CHUNK, HEAD_DIM), BF16)],
        scratch_shapes=[pltpu.VMEM((H, CHUNK, HEAD_DIM), F32)], aliases={9: 0},
        operands=[proj, proj, cosf, sins, dm, zeta, xi, gc, gnw, y], comm=comm)


def _ret_bwd(proj, rprev, dy, dproj, tables, gnw, tb, comm=None):
    S = proj.shape[0]
    DR = gnw.shape[1]
    H = DR // HEAD_DIM
    tb = _tile(S, tb, CHUNK)
    nc = tb // CHUNK
    nt = S // tb
    cosf, sins, dm, zeta, xi, gc = tables
    scale = HEAD_DIM ** -0.5

    def body(qk_ref, vg_ref, cos_ref, sin_ref, dm_ref, zeta_ref, xi_ref, gc_ref, gnw_ref, rprev_ref, dy_ref, dp_in,
             dp_ref, dgn_ref, dr_s, dqk_s, dvg_s, out_sems):
        del dp_in
        i = pl.program_id(0)
        slot = i % 2

        def out_copies(step, sl):
            rows = pl.ds(pl.multiple_of((nt - 1 - step) * tb, tb), tb)
            return (pltpu.make_async_copy(dqk_s.at[sl], dp_ref.at[rows, pl.ds(2 * DR, 2 * DR)], out_sems.at[sl, 0]),
                    pltpu.make_async_copy(dvg_s.at[sl], dp_ref.at[rows, pl.ds(4 * DR, 2 * DR)], out_sems.at[sl, 1]))

        @pl.when(i == 0)
        def _():
            dr_s[...] = jnp.zeros_like(dr_s)
            dgn_ref[...] = jnp.zeros_like(dgn_ref)

        @pl.when(i >= 2)
        def _():
            for cp in out_copies(i - 2, slot):
                cp.wait()

        def chunk(cc, carry):
            c = nc - 1 - cc
            rows = pl.ds(pl.multiple_of(c * CHUNK, CHUNK), CHUNK)
            cos = cos_ref[rows, :]
            sin = sin_ref[rows, :]
            heads = range(H)
            c0 = [slice(h * HEAD_DIM, (h + 1) * HEAD_DIM) for h in heads]
            c1 = [slice(DR + h * HEAD_DIM, DR + (h + 1) * HEAD_DIM) for h in heads]
            qh = [_rope(qk_ref[rows, c0[h]], cos, sin) for h in heads]
            kh = [_rope(qk_ref[rows, c1[h]], cos, sin) * scale for h in heads]
            qb = [t.astype(BF16) for t in qh]
            kb = [t.astype(BF16) for t in kh]
            vb = [vg_ref[rows, c0[h]].astype(BF16) for h in heads]
            rpb = [rprev_ref[c, h] for h in heads]
            qx = [(qh[h] * xi_ref[h]).astype(BF16) for h in heads]
            kz = [(kh[h] * zeta_ref[h]).astype(BF16) for h in heads]
            drh = [dr_s[h] for h in heads]
            drb = [t.astype(BF16) for t in drh]
            s = [_dot(qb[h], kb[h], NT) for h in heads]
            cross = [_dot(qx[h], rpb[h]) for h in heads]
            dv_state = [_dot(kz[h], drb[h]) for h in heads]
            dk_state = [_dot(vb[h], drb[h], NT) for h in heads]
            sb = [(s[h] * dm_ref[h]).astype(BF16) for h in heads]
            o = [_dot(sb[h], vb[h]) + cross[h] for h in heads]
            dob = []
            for h in heads:
                mu = jnp.mean(o[h], axis=-1, keepdims=True)
                oc = o[h] - mu
                rstd = lax.rsqrt(jnp.mean(oc * oc, axis=-1, keepdims=True) + EPS)
                ohat = oc * rstd
                gw = gnw_ref[:, c0[h]]
                gate = vg_ref[rows, c1[h]]
                sg = jax.nn.sigmoid(gate)
                dyv = dy_ref[rows, c0[h]]
                dvg_s[slot, rows, c1[h]] = (dyv * (ohat * gw) * (sg * (1.0 + gate * (1.0 - sg)))).astype(BF16)
                don = dyv * (gate * sg)
                dgn_ref[:, c0[h]] += jnp.sum(don * ohat, axis=0, keepdims=True)
                dohat = don * gw
                do = rstd * (dohat - jnp.mean(dohat, axis=-1, keepdims=True)
                             - ohat * jnp.mean(dohat * ohat, axis=-1, keepdims=True))
                dob.append(do.astype(BF16))
            ds = [_dot(dob[h], vb[h], NT) for h in heads]
            dq_state = [_dot(dob[h], rpb[h], NT) for h in heads]
            dv = [_dot(sb[h], dob[h], TN) + dv_state[h] for h in heads]
            dr_new = [_dot(qx[h], dob[h], TN) for h in heads]
            dsb = [(ds[h] * dm_ref[h]).astype(BF16) for h in heads]
            dqh = [_dot(dsb[h], kb[h]) + dq_state[h] * xi_ref[h] for h in heads]
            dkh = [_dot(dsb[h], qb[h], TN) + dk_state[h] * zeta_ref[h] for h in heads]
            for h in heads:
                dr_s[h] = drh[h] * gc_ref[h] + dr_new[h]
                dqk_s[slot, rows, c0[h]] = _rope_t(dqh[h], cos, sin).astype(BF16)
                dqk_s[slot, rows, c1[h]] = _rope_t(dkh[h] * scale, cos, sin).astype(BF16)
                dvg_s[slot, rows, c0[h]] = dv[h].astype(BF16)
            return carry

        lax.fori_loop(0, nc, chunk, 0)
        for cp in out_copies(i, slot):
            cp.start()

        @pl.when(i == nt - 1)
        def _():
            if nt >= 2:
                for cp in out_copies(i - 1, 1 - slot):
                    cp.wait()
            for cp in out_copies(i, slot):
                cp.wait()

    rev = lambda i: nt - 1 - i
    return _pcall(
        body, name="ret_bwd", grid=(nt,), aliases={11: 0}, comm=comm,
        operands=[proj, proj, cosf, sins, dm, zeta, xi, gc, gnw, rprev, dy, dproj],
        in_specs=[pl.BlockSpec((tb, 2 * DR), lambda i: (rev(i), 1)), pl.BlockSpec((tb, 2 * DR), lambda i: (rev(i), 2)),
                  pl.BlockSpec((tb, HEAD_DIM), lambda i: (rev(i), 0)), pl.BlockSpec((tb, HEAD_DIM), lambda i: (rev(i), 0))]
        + _ret_const_specs(H, DR)
        + [pl.BlockSpec((nc, H, CHUNK, HEAD_DIM), lambda i: (rev(i), 0, 0, 0)),
           pl.BlockSpec((tb, DR), lambda i: (rev(i), 1)), HBM_SPEC],
        out_specs=[HBM_SPEC, pl.BlockSpec((1, DR), lambda i: (0, 0))],
        out_shape=[jax.ShapeDtypeStruct(dproj.shape, BF16), jax.ShapeDtypeStruct((1, DR), F32)],
        scratch_shapes=[pltpu.VMEM((H, CHUNK, HEAD_DIM), F32), pltpu.VMEM((2, tb, 2 * DR), BF16),
                        pltpu.VMEM((2, tb, 2 * DR), BF16), pltpu.SemaphoreType.DMA((2, 2))])


def _place():
    x, y, c = lax.axis_index("x"), lax.axis_index("y"), lax.axis_index("c")
    chips = [(1 - x, y), (x, 1 - y), (1 - x, 1 - y)]
    return x, y, c, chips


def _own_slab(name, shard, place, side_by_side=False):
    R, C = shard.shape
    tr = _row_tile(R, C, 2 * ELEMENTWISE_BLOCK)
    if side_by_side:
        out = (jax.ShapeDtypeStruct((R, 4 * C), BF16), pl.BlockSpec((tr, C), lambda i, p: (i, p[1])))
    else:
        out = (jax.ShapeDtypeStruct((4, R, C), BF16), pl.BlockSpec((None, tr, C), lambda i, p: (p[1], i, 0)))
    return _ew("cast_" + name, lambda a: (a,), [(shard, pl.BlockSpec((tr, C), lambda i, p: (i, 0)))], [out],
               (R // tr,), sp=place)[0]


def _slab_half(ref, chip, half, cols):
    if cols is None:
        r2 = ref.shape[1] // 2
        return ref.at[chip, pl.ds(half * r2, r2), :]
    r2 = ref.shape[0] // 2
    return ref.at[pl.ds(half * r2, r2), pl.ds(pl.multiple_of(chip * cols, LANES), cols)]


class _remote:
    def __init__(self, src, dst, ssem, rsem, k, to):
        self.args = dict(src_ref=src, dst_ref=dst, send_sem=ssem.at[k], recv_sem=rsem.at[k], device_id=to,
                         device_id_type=MESH)

    def start(self):
        pltpu.make_async_remote_copy(**self.args).start()

    def wait_send(self):
        pltpu.make_async_remote_copy(**self.args).wait_send()

    def wait_recv(self):
        pltpu.make_async_remote_copy(**self.args).wait_recv()


def _task_fns(copies):
    def start(cins, couts, ssem, rsem, base):
        for cp in copies(cins, couts, ssem, rsem, base)[0]:
            cp.start()

    def finish(cins, couts, ssem, rsem, base):
        sends, recvs = copies(cins, couts, ssem, rsem, base)
        for cp in sends:
            cp.wait_send()
        for cp in recvs:
            cp.wait_recv()

    return start, finish


NEIGHBOURS, DIAGONAL = (0, 1), (2,)


def _gather_ici(st, which=NEIGHBOURS + DIAGONAL, cols=None):
    def copies(cins, couts, ssem, rsem, base):
        x, y, c, chips = _place()
        out = couts[0]
        mine = _slab_half(out, 2 * x + y, c, cols)
        sends, recvs = [], []
        for k, j in enumerate(which):
            cx, cy = chips[j]
            got = _slab_half(out, 2 * cx + cy, c, cols)
            sends.append(_remote(mine, mine, ssem, rsem, base + k, (cx, cy, c)))
            recvs.append(_remote(got, got, ssem, rsem, base + k, (x, y, c)))
        return sends, recvs

    start, finish = _task_fns(copies)
    return _Comm([st], [jax.ShapeDtypeStruct(st.shape, st.dtype)], {0: 0}, len(which), start, finish)


def _gather_d2d(st, cols=None):
    def copies(cins, couts, ssem, rsem, base):
        x, y, c, chips = _place()
        out = couts[0]
        sends, recvs = [], []
        for j, (cx, cy) in enumerate(chips):
            have = _slab_half(out, 2 * cx + cy, c, cols)
            want = _slab_half(out, 2 * cx + cy, 1 - c, cols)
            sends.append(_remote(have, have, ssem, rsem, base + j, (x, y, 1 - c)))
            recvs.append(_remote(want, want, ssem, rsem, base + j, (x, y, c)))
        return sends, recvs

    start, finish = _task_fns(copies)
    return _Comm([st], [jax.ShapeDtypeStruct(st.shape, st.dtype)], {0: 0}, 3, start, finish)


def _gather_both(st):
    ici, d2d = _gather_ici(st), _gather_d2d(st)

    def finish(cins, couts, ssem, rsem, base):
        ici.finish(cins, couts, ssem, rsem, base)
        d2d.start(cins, couts, ssem, rsem, base + ici.n_sem)
        d2d.finish(cins, couts, ssem, rsem, base + ici.n_sem)

    return _Comm(ici.ins, ici.outs, ici.aliases, ici.n_sem + d2d.n_sem, ici.start, finish)


def _gather_conv(conv_w):
    def copies(cins, couts, ssem, rsem, base):
        x, y, c, chips = _place()
        src, out = cins[0], couts[0]
        sends = [_remote(src, out.at[2 * x + y], ssem, rsem, base + j, (*chip, c)) for j, chip in enumerate(chips)]
        recvs = [_remote(src, out.at[2 * cx + cy], ssem, rsem, base + j, (x, y, c)) for j, (cx, cy) in enumerate(chips)]
        return sends, recvs

    start, finish = _task_fns(copies)
    return _Comm([conv_w], [jax.ShapeDtypeStruct((4,) + conv_w.shape, conv_w.dtype)], {}, 3, start, finish)


def _pair_exchange(g):
    r2 = g.shape[1] // 2

    def copies(cins, couts, ssem, rsem, base):
        x, y, c, _ = _place()
        cp = _remote(cins[0].at[:, pl.ds((1 - c) * r2, r2), :], couts[0], ssem, rsem, base, (x, y, 1 - c))
        return [cp], [cp]

    start, finish = _task_fns(copies)
    return _Comm([g], [jax.ShapeDtypeStruct((g.shape[0], r2, g.shape[2]), g.dtype)], {}, 1, start, finish)


def _chip_exchange(part):
    def copies(cins, couts, ssem, rsem, base):
        x, y, c, chips = _place()
        cps = [_remote(cins[0].at[2 * cx + cy], couts[0].at[j], ssem, rsem, base + j, (cx, cy, c))
               for j, (cx, cy) in enumerate(chips)]
        return cps, cps

    start, finish = _task_fns(copies)
    return _Comm([part], [jax.ShapeDtypeStruct((3,) + part.shape[1:], part.dtype)], {}, 3, start, finish)


def _pair_share(slot):
    def copies(cins, couts, ssem, rsem, base):
        x, y, c, _ = _place()
        out = couts[0]
        return ([_remote(out.at[c], out.at[c], ssem, rsem, base, (x, y, 1 - c))],
                [_remote(out.at[1 - c], out.at[1 - c], ssem, rsem, base, (x, y, c))])

    start, finish = _task_fns(copies)
    return _Comm([slot], [jax.ShapeDtypeStruct(slot.shape, slot.dtype)], {0: 0}, 1, start, finish)


def _gather_small(sm):
    flips = [(fx, fy, fc) for fx in (0, 1) for fy in (0, 1) for fc in (0, 1)][1:]

    def copies(cins, couts, ssem, rsem, base):
        x, y, c, _ = _place()
        src, out = cins[0], couts[0]
        peers = [(1 - x if fx else x, 1 - y if fy else y, 1 - c if fc else c) for fx, fy, fc in flips]
        sends = [_remote(src, out.at[4 * x + 2 * y + c], ssem, rsem, base + k, peer) for k, peer in enumerate(peers)]
        recvs = [_remote(src, out.at[4 * px + 2 * py + pc], ssem, rsem, base + k, (x, y, c))
                 for k, (px, py, pc) in enumerate(peers)]
        return sends, recvs

    start, finish = _task_fns(copies)
    return _Comm([sm], [jax.ShapeDtypeStruct((8,) + sm.shape, sm.dtype)], {}, 7, start, finish)


def _comm_call(name, tasks):
    task = _merge(tasks)
    nci = len(task.ins)

    def body(*refs):
        cins, couts, (ssem, rsem) = refs[:nci], refs[nci:nci + len(task.outs)], refs[nci + len(task.outs):]
        task.start(cins, couts, ssem, rsem, 0)
        task.finish(cins, couts, ssem, rsem, 0)

    return pl.pallas_call(
        body, in_specs=[HBM_SPEC] * nci, out_specs=[HBM_SPEC] * len(task.outs), out_shape=list(task.outs),
        scratch_shapes=[pltpu.SemaphoreType.DMA((task.n_sem,)), pltpu.SemaphoreType.DMA((task.n_sem,))],
        input_output_aliases=task.aliases, name=name)(*task.ins)


def _adamw(w, g, m, v):
    m = ADAM_B1 * m + (1.0 - ADAM_B1) * g
    v = ADAM_B2 * v + (1.0 - ADAM_B2) * (g * g)
    m_hat = m / (1.0 - ADAM_B1 ** ADAM_STEP)
    v_hat = v / (1.0 - ADAM_B2 ** ADAM_STEP)
    delta = -ADAM_LR * (m_hat / (jnp.sqrt(v_hat) + ADAM_EPS) + ADAM_WD * w)
    return delta, m, v


def _adamw_call(name, w, g, m, v):
    R, C = w.shape
    tr = _row_tile(R, C, ELEMENTWISE_BLOCK // 2)
    row = pl.BlockSpec((tr, C), lambda i: (i, 0))
    o = jax.ShapeDtypeStruct((R, C), F32)
    return _ew(name, lambda w_, g_, m_, v_: (*_adamw(w_, g_, m_, v_), g_), [(w, row), (g, row), (m, row), (v, row)],
               [(o, row), (o, row), (o, row), (o, row)], (R // tr,))


def _pair_sum(name, g, ra, place):
    _, R, C = g.shape
    r2 = R // 2
    tr = _row_tile(r2, C, 2 * ELEMENTWISE_BLOCK)
    nb = r2 // tr
    own = pl.BlockSpec((None, tr, C), lambda j, i, p: (j, p[0] * nb + i, 0))
    blk = pl.BlockSpec((None, tr, C), lambda j, i, p: (j, i, 0))
    return _ew("rs_pair_sum_" + name, lambda a, b: (a + b,), [(g, own), (ra, blk)],
               [(jax.ShapeDtypeStruct((4, r2, C), BF16), blk)], (4, nb), sp=place)[0]


def _chip_sum(name, g, ra, rb, place):
    _, R, C = g.shape
    r2 = R // 2
    tr = _row_tile(r2, C)
    nb = r2 // tr
    own = pl.BlockSpec((None, tr, C), lambda i, p: (p[1], p[0] * nb + i, 0))
    mine = pl.BlockSpec((None, tr, C), lambda i, p: (p[1], i, 0))
    src = [pl.BlockSpec((None, tr, C), functools.partial(lambda i, p, j: (j, i, 0), j=j)) for j in range(3)]
    out = pl.BlockSpec((None, tr, C), lambda i, p: (p[0], i, 0))

    def total(a, b, r0, r1, r2_):
        return ((((a + b) + r0.astype(F32)) + r1.astype(F32)) + r2_.astype(F32),)

    return _ew("rs_chip_sum_" + name, total, [(g, own), (ra, mine), (rb, src[0]), (rb, src[1]), (rb, src[2])],
               [(jax.ShapeDtypeStruct((2, r2, C), F32), out)], (nb,), sp=place)[0]


def _pack(arrays):
    rows, offs, pos = [], [], 0
    for a in arrays:
        flat = a.reshape(-1)
        n = -(-flat.shape[0] // (8 * LANES)) * (8 * LANES)
        if n != flat.shape[0]:
            flat = jnp.pad(flat, (0, n - flat.shape[0]))
        rows.append(flat.reshape(-1, LANES))
        offs.append(pos)
        pos += n // LANES
    return jnp.concatenate(rows, axis=0), offs


def _unpack(packed, offs, shapes):
    out = []
    for off, shp in zip(offs, shapes):
        n = 1
        for s in shp:
            n *= s
        out.append(packed[off:off + -(-n // LANES)].reshape(-1)[:n].reshape(shp))
    return out


def _sum8(gathered):
    _, R, C = gathered.shape
    tr = _row_tile(R, C, 256 * 1024)
    specs = [pl.BlockSpec((None, tr, C), functools.partial(lambda i, d: (d, i, 0), d=d)) for d in range(8)]

    def fn(*parts):
        t = parts[0]
        for p in parts[1:]:
            t = t + p
        return (t,)

    return _ew("small_sum", fn, [(gathered, s) for s in specs],
               [(jax.ShapeDtypeStruct((R, C), F32), pl.BlockSpec((tr, C), lambda i: (i, 0)))], (R // tr,))[0]


BIG = ("w_in", "w_out", "w_ffn_gate", "w_ffn_up", "w_ffn_down")
SMALL = ("ln1_w", "conv_w", "conv_b", "gate_a_w", "gate_a_b", "gate_x_w", "gate_x_b", "lru_lambda", "ret_gn_w", "ln2_w",
         "final_norm_w")
WEIGHTS = ("ln1_w", "w_in", "conv_w", "conv_b", "gate_a_w", "gate_a_b", "gate_x_w", "gate_x_b", "lru_lambda", "ret_gn_w",
           "w_out", "ln2_w", "w_ffn_gate", "w_ffn_up", "w_ffn_down", "final_norm_w")


def kernel(x, ln1_w, w_in, conv_w, conv_b, gate_a_w, gate_a_b, gate_x_w, gate_x_b, lru_lambda, ret_gn_w, w_out, ln2_w, w_ffn_gate, w_ffn_up, w_ffn_down, final_norm_w, loss_target, m_ln1_w, m_w_in, m_conv_w, m_conv_b, m_gate_a_w, m_gate_a_b, m_gate_x_w, m_gate_x_b, m_lru_lambda, m_ret_gn_w, m_w_out, m_ln2_w, m_w_ffn_gate, m_w_ffn_up, m_w_ffn_down, m_final_norm_w, v_ln1_w, v_w_in, v_conv_w, v_conv_b, v_gate_a_w, v_gate_a_b, v_gate_x_w, v_gate_x_b, v_lru_lambda, v_ret_gn_w, v_w_out, v_ln2_w, v_w_ffn_gate, v_w_ffn_up, v_w_ffn_down, v_final_norm_w):
    w = dict(ln1_w=ln1_w, w_in=w_in, conv_w=conv_w, conv_b=conv_b, gate_a_w=gate_a_w, gate_a_b=gate_a_b, gate_x_w=gate_x_w,
             gate_x_b=gate_x_b, lru_lambda=lru_lambda, ret_gn_w=ret_gn_w, w_out=w_out, ln2_w=ln2_w, w_ffn_gate=w_ffn_gate,
             w_ffn_up=w_ffn_up, w_ffn_down=w_ffn_down, final_norm_w=final_norm_w)
    m = dict(ln1_w=m_ln1_w, w_in=m_w_in, conv_w=m_conv_w, conv_b=m_conv_b, gate_a_w=m_gate_a_w, gate_a_b=m_gate_a_b,
             gate_x_w=m_gate_x_w, gate_x_b=m_gate_x_b, lru_lambda=m_lru_lambda, ret_gn_w=m_ret_gn_w, w_out=m_w_out,
             ln2_w=m_ln2_w, w_ffn_gate=m_w_ffn_gate, w_ffn_up=m_w_ffn_up, w_ffn_down=m_w_ffn_down,
             final_norm_w=m_final_norm_w)
    v = dict(ln1_w=v_ln1_w, w_in=v_w_in, conv_w=v_conv_w, conv_b=v_conv_b, gate_a_w=v_gate_a_w, gate_a_b=v_gate_a_b,
             gate_x_w=v_gate_x_w, gate_x_b=v_gate_x_b, lru_lambda=v_lru_lambda, ret_gn_w=v_ret_gn_w, w_out=v_w_out,
             ln2_w=v_ln2_w, w_ffn_gate=v_w_ffn_gate, w_ffn_up=v_w_ffn_up, w_ffn_down=v_w_ffn_down,
             final_norm_w=v_final_norm_w)
    xs, tgt = x[0], loss_target[0]
    S, D = xs.shape
    DL, DR = conv_b.shape[1], ret_gn_w.shape[1]
    assert DL == DR and DL % HEAD_DIM == 0 and S % CHUNK == 0
    d_mix = DL + DR
    cx, cy, cc = lax.axis_index("x"), lax.axis_index("y"), lax.axis_index("c")
    chip = 2 * cx + cy
    place = jnp.stack([cc, chip]).astype(jnp.int32)
    grad, delta, new_m, new_v = {}, {}, {}, {}

    def finish_big(n, full):
        shp = w[n].shape
        g2 = full.reshape(shp[1], shp[2])
        w2, m2, v2 = (t[n].reshape(shp[1], shp[2]) for t in (w, m, v))
        d_, m_, v_, g_ = _adamw_call("adamw_" + n, w2, g2, m2, v2)
        grad[n], delta[n], new_m[n], new_v[n] = (t.reshape(shp) for t in (g_, d_, m_, v_))

    def all_sum(gathered, own):
        return _sum8(lax.dynamic_update_slice(gathered, own[None], (4 * cx + 2 * cy + cc, 0, 0)))

    wide = ("w_ffn_gate", "w_ffn_up")
    st = {n: _own_slab(n, w[n][0], place, side_by_side=n in wide) for n in BIG}
    (u1,), (w_in_st, conv_st) = _rms_fwd("rms1", xs, ln1_w, TM,
                                         comm=_merge([_gather_both(st["w_in"]), _gather_conv(conv_w[0])]))
    conv_st = lax.dynamic_update_slice(conv_st, conv_w, (chip, 0, 0))
    cw_cols = conv_st.shape[2]
    conv_full = jnp.transpose(conv_st, (1, 0, 2)).reshape(CONV_W, 4 * cw_cols)
    n_in, n_ff = w_in_st.shape[2], w_ffn_gate.shape[2]
    tables = _ret_tables(S, DR // HEAD_DIM)
    wab, wxb = gate_a_w[0].astype(BF16), gate_x_w[0].astype(BF16)
    lru_w = (conv_full, conv_b, wab, gate_a_b, wxb, gate_x_b, lru_lambda)

    proj, (w_out_st, wg_f) = _mm_nn_stacked(
        "proj", u1, w_in_st, F32, TM_WIDE,
        comm=_merge([_gather_ici(st["w_out"]), _gather_ici(st["w_ffn_gate"], cols=n_ff)]))
    (hs, kept, y), (w_out_st, wg_f, wu_f) = _lru_fwd(
        proj, *lru_w, LRU_TILE, d_mix,
        comm=_merge([_gather_d2d(w_out_st), _gather_d2d(wg_f, cols=n_ff), _gather_ici(st["w_ffn_up"], cols=n_ff)]))
    (y, rprev), (wu_f, wd_st) = _ret_fwd(
        proj, y, tables, ret_gn_w, RET_BLOCK,
        comm=_merge([_gather_d2d(wu_f, cols=n_ff), _gather_ici(st["w_ffn_down"], NEIGHBOURS)]))
    w_out_f = w_out_st.reshape(d_mix, D)
    (h1, u2), (wd_st,) = _out_proj_rms(y, w_out_f, xs, ln2_w, TM, comm=_gather_ici(wd_st, DIAGONAL))
    (dg_fac, du_fac, ff), (wd_st,) = _ffn_gate_up(u2, wg_f, wu_f, n_ff, TM, comm=_gather_d2d(wd_st))
    wd_f = wd_st.reshape(4 * n_ff, D)
    dh2, dh2b, d_fw, loss = _ffn_down_loss(ff, wd_f, h1, tgt, final_norm_w.reshape(1, D), TM_RESIDENT)

    g_wd = _mm_tn("g_w_down", ff, dh2b, n_ff, TILE_GRAD, TK_GRAD).reshape(4, n_ff, D)
    (dgt, dup), (ra_wd,) = _ffn_gate_up_bwd(dh2b, wd_f, dg_fac, du_fac, TM_WIDE, n_ff, comm=_pair_exchange(g_wd))
    pb_wd = _pair_sum("w_ffn_down", g_wd, ra_wd, place)
    g_wg, (rb_wd,) = _mm_tn_slabs("g_w_gate", u2, dgt, n_ff, TILE_GRAD, TK_GRAD, comm=_chip_exchange(pb_wd))
    slot_wd = _chip_sum("w_ffn_down", g_wd, ra_wd, rb_wd, place)
    g_wu, (full_wd, ra_wg) = _mm_tn_slabs("g_w_up", u2, dup, n_ff, TILE_GRAD, TK_GRAD,
                                          comm=_merge([_pair_share(slot_wd), _pair_exchange(g_wg)]))
    finish_big("w_ffn_down", full_wd)
    pb_wg = _pair_sum("w_ffn_gate", g_wg, ra_wg, place)
    du2, (rb_wg,) = _mm_nt_stacked("d_u2_gate", dgt, wg_f, TM, comm=_chip_exchange(pb_wg))
    du2, (ra_wu,) = _mm_nt_stacked("d_u2_up", dup, wu_f, TM, res=du2, comm=_pair_exchange(g_wu))
    slot_wg = _chip_sum("w_ffn_gate", g_wg, ra_wg, rb_wg, place)
    pb_wu = _pair_sum("w_ffn_up", g_wu, ra_wu, place)
    (dh1, dh1b, dy, d_ln2), (full_wg,) = _rms_bwd_dy(h1, ln2_w, du2, dh2, w_out_f, TM_RESIDENT,
                                                     comm=_pair_share(slot_wg))
    finish_big("w_ffn_gate", full_wg)
    g_wout = _mm_tn("g_w_out", y, dh1b, TILE_GRAD, TILE_GRAD, TK_GRAD).reshape(4, d_mix // 4, D)
    (dproj, d_cw, d_cb, d_wa, d_ba, d_wx, d_bx, d_lam), (rb_wu, ra_wout) = _lru_bwd(
        proj, hs, kept, dy, conv_full, wab, wxb, lru_lambda, LRU_TILE,
        comm=_merge([_chip_exchange(pb_wu), _pair_exchange(g_wout)]))
    slot_wu = _chip_sum("w_ffn_up", g_wu, ra_wu, rb_wu, place)
    pb_wout = _pair_sum("w_out", g_wout, ra_wout, place)
    (dproj, d_gn), (full_wu, rb_wout) = _ret_bwd(proj, rprev, dy, dproj, tables, ret_gn_w, RET_BLOCK,
                                                 comm=_merge([_pair_share(slot_wu), _chip_exchange(pb_wout)]))
    finish_big("w_ffn_up", full_wu)
    slot_wout = _chip_sum("w_out", g_wout, ra_wout, rb_wout, place)
    small = dict(conv_w=d_cw, conv_b=d_cb, gate_a_w=d_wa, gate_a_b=d_ba, gate_x_w=d_wx, gate_x_b=d_bx, lru_lambda=d_lam,
                 ret_gn_w=d_gn, ln2_w=d_ln2, final_norm_w=d_fw)
    packed, offs = _pack([small[n] for n in SMALL[1:]] + [loss])
    g_win, (full_wout, got_small) = _mm_tn("g_w_in", u1, dproj, TILE_GRAD, None, TK_GRAD, stacked_cols=n_in,
                                           comm=_merge([_pair_share(slot_wout), _gather_small(packed)]))
    finish_big("w_out", full_wout)
    (ra_win,) = _comm_call("rs_pair_w_in", [_pair_exchange(g_win)])
    pb_win = _pair_sum("w_in", g_win, ra_win, place)
    du1, (rb_win,) = _mm_nt_stacked("d_u1", dproj, w_in_st, TM, comm=_chip_exchange(pb_win))
    slot_win = _chip_sum("w_in", g_win, ra_win, rb_win, place)
    gx, d_ln1 = _rms_bwd("rms1_bwd", xs, ln1_w, du1, dh1, TM)
    packed1, _ = _pack([d_ln1])
    full_win, got_ln1 = _comm_call("reduce_tail", [_pair_share(slot_win), _gather_small(packed1)])
    finish_big("w_in", full_win)

    red = _unpack(all_sum(got_small, packed), offs, [small[n].shape for n in SMALL[1:]] + [(1, LANES)])
    g = dict(zip(SMALL[1:], red[:-1]))
    g["ln1_w"] = all_sum(got_ln1, packed1)[:-(-D // LANES)].reshape(1, D)
    loss_out = red[-1][0, 0]
    g["conv_w"] = lax.dynamic_slice(g["conv_w"], (0, chip * cw_cols), (CONV_W, cw_cols))
    packs = [_pack([t[n] for n in SMALL])[0] for t in (w, m, v)]
    gp, offs2 = _pack([g[n] for n in SMALL])
    outs = _adamw_call("adamw_small", packs[0], gp, packs[1], packs[2])
    shapes = [w[n].shape for n in SMALL]
    for dst, arr in zip((delta, new_m, new_v), outs):
        dst.update(zip(SMALL, _unpack(arr, offs2, shapes)))
    for n in SMALL:
        grad[n] = g[n].reshape(w[n].shape)

    return (loss_out, gx.reshape(x.shape), *[grad[n] for n in WEIGHTS], *[delta[n] for n in WEIGHTS],
            *[new_m[n] for n in WEIGHTS], *[new_v[n] for n in WEIGHTS])
```
